```python
import math
import jax, jax.numpy as jnp
from jax import lax
import numpy as np

D_MODEL = 1024
BATCH = 8
SEQ = 16384
DEPTH = 4

N_MIXERS = 2
N_SSD_LAYERS = (DEPTH + 1) // 2
N_SC_LAYERS = DEPTH // 2
RMS_EPS = 1e-5
D_FF = 2816
SSD_EXPAND = 2
SSD_D_INNER = SSD_EXPAND * D_MODEL
SSD_HEAD_DIM = 64
SSD_N_HEADS = SSD_D_INNER // SSD_HEAD_DIM
SSD_N_GROUPS = 4
SSD_HEADS_PER_GROUP = SSD_N_HEADS // SSD_N_GROUPS
SSD_D_STATE = 128
SSD_CONV_W = 4
SSD_CHUNK = 128
SSD_CONV_DIM = SSD_D_INNER + 2 * SSD_N_GROUPS * SSD_D_STATE
SSD_IN_DIM = SSD_D_INNER + SSD_CONV_DIM + SSD_N_HEADS
SSD_DT_MIN = 1e-3
SSD_DT_MAX = 1e-1
SC_CONV_W = 3

kernel_name = "hybrid_ssd_shortconv_macaron"


def rmsnorm(x, w):
    xf = x.astype(jnp.float32)
    inv = lax.rsqrt(jnp.mean(xf * xf, axis=-1, keepdims=True) + RMS_EPS)
    return (xf * inv).astype(x.dtype) * w


def swiglu(h, w_gate, w_up, w_down):
    return (jax.nn.silu(h @ w_gate) * (h @ w_up)) @ w_down


def causal_dwconv(u, w):
    k_w = w.shape[0]
    s = u.shape[1]
    upad = jnp.pad(u, ((0, 0), (k_w - 1, 0), (0, 0)))
    out = upad[:, 0:s] * w[0]
    for k in range(1, k_w):
        out = out + upad[:, k:k + s] * w[k]
    return out


def causal_decay(a_cs):
    t = a_cs.shape[-1]
    seg = a_cs[..., :, None] - a_cs[..., None, :]
    mask = jnp.tril(jnp.ones((t, t), dtype=bool))
    return jnp.exp(jnp.where(mask, seg, -jnp.inf))


def ssd_chunked(x, dt, a, bm, cm):
    b, s = x.shape[0], x.shape[1]
    nc = s // SSD_CHUNK
    G, E, P, L = SSD_N_GROUPS, SSD_HEADS_PER_GROUP, SSD_HEAD_DIM, SSD_CHUNK
    x_dt = x.astype(jnp.float32) * dt[..., None]
    xc = x_dt.reshape(b, nc, L, G, E, P)
    ac = (dt * a).reshape(b, nc, L, G, E).transpose(0, 3, 4, 1, 2)
    bc = bm.astype(jnp.float32).reshape(b, nc, L, G, SSD_D_STATE)
    cc = cm.astype(jnp.float32).reshape(b, nc, L, G, SSD_D_STATE)
    a_cs = jnp.cumsum(ac, axis=-1)
    lmat = causal_decay(a_cs)
    cb = jnp.einsum("bclgn,bcsgn->bgcls", cc, bc)
    y_diag = jnp.einsum("bgcls,bgecls,bcsgep->bclgep", cb, lmat, xc)
    decay_states = jnp.exp(a_cs[..., -1:] - a_cs)
    states = jnp.einsum("bclgn,bgecl,bclgep->bcgepn", bc, decay_states, xc)
    states = jnp.concatenate([jnp.zeros_like(states[:, :1]), states], axis=1)
    chunk_tot = jnp.pad(a_cs[..., -1], ((0, 0), (0, 0), (0, 0), (1, 0)))
    decay_chunk = causal_decay(jnp.cumsum(chunk_tot, axis=-1))
    new_states = jnp.einsum("bgezc,bcgepn->bzgepn", decay_chunk, states)
    prev_states = new_states[:, :-1]
    y_off = jnp.einsum("bclgn,bcgepn,bgecl->bclgep", cc, prev_states, jnp.exp(a_cs))
    return (y_diag + y_off).reshape(b, s, SSD_N_HEADS, P)


def ssd_mixer(h, w_in, conv_w, conv_b, dt_bias, a_log, d_skip, norm_w, w_out):
    b, s, _ = h.shape
    zxbcdt = h @ w_in
    z = zxbcdt[..., :SSD_D_INNER]
    xbc = zxbcdt[..., SSD_D_INNER:SSD_D_INNER + SSD_CONV_DIM]
    dt_raw = zxbcdt[..., SSD_D_INNER + SSD_CONV_DIM:]
    xbc = jax.nn.silu(causal_dwconv(xbc, conv_w) + conv_b)
    gn = SSD_N_GROUPS * SSD_D_STATE
    xs = xbc[..., :SSD_D_INNER].reshape(b, s, SSD_N_HEADS, SSD_HEAD_DIM)
    bm = xbc[..., SSD_D_INNER:SSD_D_INNER + gn].reshape(b, s, SSD_N_GROUPS, SSD_D_STATE)
    cm = xbc[..., SSD_D_INNER + gn:].reshape(b, s, SSD_N_GROUPS, SSD_D_STATE)
    dt = jax.nn.softplus(dt_raw.astype(jnp.float32) + dt_bias.astype(jnp.float32))
    a = -jnp.exp(a_log.astype(jnp.float32))
    y = ssd_chunked(xs, dt, a, bm, cm)
    y = y + xs.astype(jnp.float32) * d_skip.astype(jnp.float32)[:, None]
    y = y.reshape(b, s, SSD_D_INNER)
    g = y * jax.nn.silu(z.astype(jnp.float32))
    gg = g.reshape(b, s, SSD_N_GROUPS, SSD_D_INNER // SSD_N_GROUPS)
    gg = gg * lax.rsqrt(jnp.mean(gg * gg, axis=-1, keepdims=True) + RMS_EPS)
    g = gg.reshape(b, s, SSD_D_INNER).astype(h.dtype) * norm_w
    return g @ w_out


def shortconv_mixer(h, w_in, conv_w, w_out):
    bcu = h @ w_in
    bg = bcu[..., :D_MODEL]
    cg = bcu[..., D_MODEL:2 * D_MODEL]
    u = bcu[..., 2 * D_MODEL:]
    v = causal_dwconv(cg * u, conv_w)
    return (bg * v) @ w_out


def _fwd_setup_inputs(seed: int = 0) -> dict:
    key = jax.random.key(seed)
    ks = jax.random.split(key, 20)
    f32 = jnp.float32
    nrm = lambda k, shape, fan_in: jax.random.normal(k, shape, f32) * (fan_in ** -0.5)
    x = jax.random.normal(ks[0], (BATCH, SEQ, D_MODEL), f32)
    norm_w = 1.0 + 0.01 * jax.random.normal(ks[1], (DEPTH, 3, D_MODEL), f32)
    ffn_w_gate = nrm(ks[2], (DEPTH, 2, D_MODEL, D_FF), D_MODEL)
    ffn_w_up = nrm(ks[3], (DEPTH, 2, D_MODEL, D_FF), D_MODEL)
    ffn_w_down = nrm(ks[4], (DEPTH, 2, D_FF, D_MODEL), D_FF)
    ssd_w_in = nrm(ks[5], (N_SSD_LAYERS, D_MODEL, SSD_IN_DIM), D_MODEL)
    ssd_conv_w = nrm(ks[6], (N_SSD_LAYERS, SSD_CONV_W, SSD_CONV_DIM), SSD_CONV_W)
    ssd_conv_b = 0.01 * jax.random.normal(ks[7], (N_SSD_LAYERS, SSD_CONV_DIM), f32)
    dt0 = jnp.exp(jax.random.uniform(ks[8], (N_SSD_LAYERS, SSD_N_HEADS), f32)
                  * (math.log(SSD_DT_MAX) - math.log(SSD_DT_MIN)) + math.log(SSD_DT_MIN))
    ssd_dt_bias = dt0 + jnp.log(-jnp.expm1(-dt0))
    ssd_a_log = jnp.log(jax.random.uniform(ks[9], (N_SSD_LAYERS, SSD_N_HEADS), f32, 1.0, 16.0))
    ssd_d = 1.0 + 0.1 * jax.random.normal(ks[10], (N_SSD_LAYERS, SSD_N_HEADS), f32)
    ssd_norm_w = 1.0 + 0.01 * jax.random.normal(ks[11], (N_SSD_LAYERS, SSD_D_INNER), f32)
    ssd_w_out = nrm(ks[12], (N_SSD_LAYERS, SSD_D_INNER, D_MODEL), SSD_D_INNER)
    sc_w_in = nrm(ks[13], (N_SC_LAYERS, D_MODEL, 3 * D_MODEL), D_MODEL)
    sc_conv_w = nrm(ks[14], (N_SC_LAYERS, SC_CONV_W, D_MODEL), SC_CONV_W)
    sc_w_out = nrm(ks[15], (N_SC_LAYERS, D_MODEL, D_MODEL), D_MODEL)
    final_norm_w = 1.0 + 0.01 * jax.random.normal(ks[16], (D_MODEL,), f32)
    return {"x": x, "norm_w": norm_w, "ffn_w_gate": ffn_w_gate, "ffn_w_up": ffn_w_up,
            "ffn_w_down": ffn_w_down, "ssd_w_in": ssd_w_in, "ssd_conv_w": ssd_conv_w,
            "ssd_conv_b": ssd_conv_b, "ssd_dt_bias": ssd_dt_bias, "ssd_a_log": ssd_a_log,
            "ssd_d": ssd_d, "ssd_norm_w": ssd_norm_w, "ssd_w_out": ssd_w_out,
            "sc_w_in": sc_w_in, "sc_conv_w": sc_conv_w, "sc_w_out": sc_w_out,
            "final_norm_w": final_norm_w}


def _fwd_reference(x, norm_w, ffn_w_gate, ffn_w_up, ffn_w_down, ssd_w_in, ssd_conv_w, ssd_conv_b,
              ssd_dt_bias, ssd_a_log, ssd_d, ssd_norm_w, ssd_w_out, sc_w_in, sc_conv_w,
              sc_w_out, final_norm_w):
    for i in range(DEPTH):
        x = x + 0.5 * swiglu(rmsnorm(x, norm_w[i, 0]), ffn_w_gate[i, 0], ffn_w_up[i, 0], ffn_w_down[i, 0])
        h = rmsnorm(x, norm_w[i, 1])
        j = i // N_MIXERS
        if i % N_MIXERS == 0:
            mix = ssd_mixer(h, ssd_w_in[j], ssd_conv_w[j], ssd_conv_b[j], ssd_dt_bias[j],
                            ssd_a_log[j], ssd_d[j], ssd_norm_w[j], ssd_w_out[j])
        else:
            mix = shortconv_mixer(h, sc_w_in[j], sc_conv_w[j], sc_w_out[j])
        x = x + mix
        x = x + 0.5 * swiglu(rmsnorm(x, norm_w[i, 2]), ffn_w_gate[i, 1], ffn_w_up[i, 1], ffn_w_down[i, 1])
    return rmsnorm(x, final_norm_w)


import jax as _jax
import jax.numpy as _jnp

TWIN_FORMAT = 'train_step'
FWD_PARAMS = ['x', 'norm_w', 'ffn_w_gate', 'ffn_w_up', 'ffn_w_down', 'ssd_w_in', 'ssd_conv_w', 'ssd_conv_b', 'ssd_dt_bias', 'ssd_a_log', 'ssd_d', 'ssd_norm_w', 'ssd_w_out', 'sc_w_in', 'sc_conv_w', 'sc_w_out', 'final_norm_w']
TWIN_WEIGHTS = ['norm_w', 'ffn_w_gate', 'ffn_w_up', 'ffn_w_down', 'ssd_w_in', 'ssd_conv_w', 'ssd_conv_b', 'ssd_dt_bias', 'ssd_a_log', 'ssd_d', 'ssd_norm_w', 'ssd_w_out', 'sc_w_in', 'sc_conv_w', 'sc_w_out', 'final_norm_w']
TWIN_DIFF_INPUT = 'x'
TWIN_INPUTS = ['x', 'norm_w', 'ffn_w_gate', 'ffn_w_up', 'ffn_w_down', 'ssd_w_in', 'ssd_conv_w', 'ssd_conv_b', 'ssd_dt_bias', 'ssd_a_log', 'ssd_d', 'ssd_norm_w', 'ssd_w_out', 'sc_w_in', 'sc_conv_w', 'sc_w_out', 'final_norm_w', 'loss_target', 'm_norm_w', 'm_ffn_w_gate', 'm_ffn_w_up', 'm_ffn_w_down', 'm_ssd_w_in', 'm_ssd_conv_w', 'm_ssd_conv_b', 'm_ssd_dt_bias', 'm_ssd_a_log', 'm_ssd_d', 'm_ssd_norm_w', 'm_ssd_w_out', 'm_sc_w_in', 'm_sc_conv_w', 'm_sc_w_out', 'm_final_norm_w', 'v_norm_w', 'v_ffn_w_gate', 'v_ffn_w_up', 'v_ffn_w_down', 'v_ssd_w_in', 'v_ssd_conv_w', 'v_ssd_conv_b', 'v_ssd_dt_bias', 'v_ssd_a_log', 'v_ssd_d', 'v_ssd_norm_w', 'v_ssd_w_out', 'v_sc_w_in', 'v_sc_conv_w', 'v_sc_w_out', 'v_final_norm_w']
TWIN_OUTPUTS = ['loss', 'grad_x', 'grad_norm_w', 'grad_ffn_w_gate', 'grad_ffn_w_up', 'grad_ffn_w_down', 'grad_ssd_w_in', 'grad_ssd_conv_w', 'grad_ssd_conv_b', 'grad_ssd_dt_bias', 'grad_ssd_a_log', 'grad_ssd_d', 'grad_ssd_norm_w', 'grad_ssd_w_out', 'grad_sc_w_in', 'grad_sc_conv_w', 'grad_sc_w_out', 'grad_final_norm_w', 'delta_norm_w', 'delta_ffn_w_gate', 'delta_ffn_w_up', 'delta_ffn_w_down', 'delta_ssd_w_in', 'delta_ssd_conv_w', 'delta_ssd_conv_b', 'delta_ssd_dt_bias', 'delta_ssd_a_log', 'delta_ssd_d', 'delta_ssd_norm_w', 'delta_ssd_w_out', 'delta_sc_w_in', 'delta_sc_conv_w', 'delta_sc_w_out', 'delta_final_norm_w', 'new_m_norm_w', 'new_m_ffn_w_gate', 'new_m_ffn_w_up', 'new_m_ffn_w_down', 'new_m_ssd_w_in', 'new_m_ssd_conv_w', 'new_m_ssd_conv_b', 'new_m_ssd_dt_bias', 'new_m_ssd_a_log', 'new_m_ssd_d', 'new_m_ssd_norm_w', 'new_m_ssd_w_out', 'new_m_sc_w_in', 'new_m_sc_conv_w', 'new_m_sc_w_out', 'new_m_final_norm_w', 'new_v_norm_w', 'new_v_ffn_w_gate', 'new_v_ffn_w_up', 'new_v_ffn_w_down', 'new_v_ssd_w_in', 'new_v_ssd_conv_w', 'new_v_ssd_conv_b', 'new_v_ssd_dt_bias', 'new_v_ssd_a_log', 'new_v_ssd_d', 'new_v_ssd_norm_w', 'new_v_ssd_w_out', 'new_v_sc_w_in', 'new_v_sc_conv_w', 'new_v_sc_w_out', 'new_v_final_norm_w']
TWIN_LEAF_KINDS = {'loss': 'loss', 'grad_x': 'grad_x', 'grad_norm_w': 'grad_w', 'grad_ffn_w_gate': 'grad_w', 'grad_ffn_w_up': 'grad_w', 'grad_ffn_w_down': 'grad_w', 'grad_ssd_w_in': 'grad_w', 'grad_ssd_conv_w': 'grad_w', 'grad_ssd_conv_b': 'grad_w', 'grad_ssd_dt_bias': 'grad_w', 'grad_ssd_a_log': 'grad_w', 'grad_ssd_d': 'grad_w', 'grad_ssd_norm_w': 'grad_w', 'grad_ssd_w_out': 'grad_w', 'grad_sc_w_in': 'grad_w', 'grad_sc_conv_w': 'grad_w', 'grad_sc_w_out': 'grad_w', 'grad_final_norm_w': 'grad_w', 'delta_norm_w': 'delta_w', 'delta_ffn_w_gate': 'delta_w', 'delta_ffn_w_up': 'delta_w', 'delta_ffn_w_down': 'delta_w', 'delta_ssd_w_in': 'delta_w', 'delta_ssd_conv_w': 'delta_w', 'delta_ssd_conv_b': 'delta_w', 'delta_ssd_dt_bias': 'delta_w', 'delta_ssd_a_log': 'delta_w', 'delta_ssd_d': 'delta_w', 'delta_ssd_norm_w': 'delta_w', 'delta_ssd_w_out': 'delta_w', 'delta_sc_w_in': 'delta_w', 'delta_sc_conv_w': 'delta_w', 'delta_sc_w_out': 'delta_w', 'delta_final_norm_w': 'delta_w', 'new_m_norm_w': 'new_m', 'new_m_ffn_w_gate': 'new_m', 'new_m_ffn_w_up': 'new_m', 'new_m_ffn_w_down': 'new_m', 'new_m_ssd_w_in': 'new_m', 'new_m_ssd_conv_w': 'new_m', 'new_m_ssd_conv_b': 'new_m', 'new_m_ssd_dt_bias': 'new_m', 'new_m_ssd_a_log': 'new_m', 'new_m_ssd_d': 'new_m', 'new_m_ssd_norm_w': 'new_m', 'new_m_ssd_w_out': 'new_m', 'new_m_sc_w_in': 'new_m', 'new_m_sc_conv_w': 'new_m', 'new_m_sc_w_out': 'new_m', 'new_m_final_norm_w': 'new_m', 'new_v_norm_w': 'new_v', 'new_v_ffn_w_gate': 'new_v', 'new_v_ffn_w_up': 'new_v', 'new_v_ffn_w_down': 'new_v', 'new_v_ssd_w_in': 'new_v', 'new_v_ssd_conv_w': 'new_v', 'new_v_ssd_conv_b': 'new_v', 'new_v_ssd_dt_bias': 'new_v', 'new_v_ssd_a_log': 'new_v', 'new_v_ssd_d': 'new_v', 'new_v_ssd_norm_w': 'new_v', 'new_v_ssd_w_out': 'new_v', 'new_v_sc_w_in': 'new_v', 'new_v_sc_conv_w': 'new_v', 'new_v_sc_w_out': 'new_v', 'new_v_final_norm_w': 'new_v'}


def _forward(args):
    return _fwd_reference(*[args[k] for k in FWD_PARAMS])


def _output_shape():
    def fwd():
        inp = _fwd_setup_inputs(0)
        return _fwd_reference(*[inp[k] for k in FWD_PARAMS])
    out = _jax.eval_shape(fwd)
    return out.shape, out.dtype

N_MICROBATCH = 1
ADAM_LR = 0.001
ADAM_B1 = 0.9
ADAM_B2 = 0.999
ADAM_EPS = 1e-08
ADAM_WD = 0.01
ADAM_STEP = 10
PER_EXAMPLE_BATCH_AXIS = {'x': 0, 'loss_target': 0}
SHARED_INPUTS = []
_WEIGHT_DTYPES = {'norm_w': _jnp.float32, 'ffn_w_gate': _jnp.float32, 'ffn_w_up': _jnp.float32, 'ffn_w_down': _jnp.float32, 'ssd_w_in': _jnp.float32, 'ssd_conv_w': _jnp.float32, 'ssd_conv_b': _jnp.float32, 'ssd_dt_bias': _jnp.float32, 'ssd_a_log': _jnp.float32, 'ssd_d': _jnp.float32, 'ssd_norm_w': _jnp.float32, 'ssd_w_out': _jnp.float32, 'sc_w_in': _jnp.float32, 'sc_conv_w': _jnp.float32, 'sc_w_out': _jnp.float32, 'final_norm_w': _jnp.float32}
MOMENT_SCALE = {'norm_w': 3.007456e-01, 'ffn_w_gate': 7.232845e-02, 'ffn_w_up': 7.000625e-02, 'ffn_w_down': 1.162599e-01, 'ssd_w_in': 2.323106e-01, 'ssd_conv_w': 2.128758e-01, 'ssd_conv_b': 3.391987e-01, 'ssd_dt_bias': 5.417418e-01, 'ssd_a_log': 2.887793e+00, 'ssd_d': 1.436105e+00, 'ssd_norm_w': 2.781326e-01, 'ssd_w_out': 3.524815e-01, 'sc_w_in': 2.210666e-01, 'sc_conv_w': 2.262032e-01, 'sc_w_out': 2.199693e-01, 'final_norm_w': 1.281173e+02}


def _to_microbatches(a, axis):
    t = _jnp.moveaxis(a, axis, 0)
    t = t.reshape((N_MICROBATCH, t.shape[0] // N_MICROBATCH) + t.shape[1:])
    return _jnp.moveaxis(t, 1, axis + 1)


def setup_inputs(seed: int = 0) -> dict:
    inp = _fwd_setup_inputs(seed)
    key = _jax.random.fold_in(_jax.random.key(seed), 7919)
    shape, _ = _output_shape()
    out = dict(inp)
    out["loss_target"] = _jax.random.normal(_jax.random.fold_in(key, 0), shape, _jnp.float32)
    for i, name in enumerate(TWIN_WEIGHTS):
        w = inp[name].astype(_jnp.float32)
        if MOMENT_SCALE is None:
            s = _jnp.sqrt(_jnp.mean(_jnp.square(w)) + 1e-30)
        else:
            s = MOMENT_SCALE[name]
        km, kv = _jax.random.split(_jax.random.fold_in(key, i + 1))
        out[name] = w
        out["m_" + name] = s * _jax.random.normal(km, w.shape, _jnp.float32)
        out["v_" + name] = (s * s) * _jax.random.uniform(kv, w.shape, _jnp.float32, 0.5, 1.5)
    if N_MICROBATCH > 1:
        for name, axis in PER_EXAMPLE_BATCH_AXIS.items():
            out[name] = _to_microbatches(out[name], axis)
    return {'x': out['x'], 'norm_w': out['norm_w'], 'ffn_w_gate': out['ffn_w_gate'], 'ffn_w_up': out['ffn_w_up'], 'ffn_w_down': out['ffn_w_down'], 'ssd_w_in': out['ssd_w_in'], 'ssd_conv_w': out['ssd_conv_w'], 'ssd_conv_b': out['ssd_conv_b'], 'ssd_dt_bias': out['ssd_dt_bias'], 'ssd_a_log': out['ssd_a_log'], 'ssd_d': out['ssd_d'], 'ssd_norm_w': out['ssd_norm_w'], 'ssd_w_out': out['ssd_w_out'], 'sc_w_in': out['sc_w_in'], 'sc_conv_w': out['sc_conv_w'], 'sc_w_out': out['sc_w_out'], 'final_norm_w': out['final_norm_w'], 'loss_target': out['loss_target'], 'm_norm_w': out['m_norm_w'], 'm_ffn_w_gate': out['m_ffn_w_gate'], 'm_ffn_w_up': out['m_ffn_w_up'], 'm_ffn_w_down': out['m_ffn_w_down'], 'm_ssd_w_in': out['m_ssd_w_in'], 'm_ssd_conv_w': out['m_ssd_conv_w'], 'm_ssd_conv_b': out['m_ssd_conv_b'], 'm_ssd_dt_bias': out['m_ssd_dt_bias'], 'm_ssd_a_log': out['m_ssd_a_log'], 'm_ssd_d': out['m_ssd_d'], 'm_ssd_norm_w': out['m_ssd_norm_w'], 'm_ssd_w_out': out['m_ssd_w_out'], 'm_sc_w_in': out['m_sc_w_in'], 'm_sc_conv_w': out['m_sc_conv_w'], 'm_sc_w_out': out['m_sc_w_out'], 'm_final_norm_w': out['m_final_norm_w'], 'v_norm_w': out['v_norm_w'], 'v_ffn_w_gate': out['v_ffn_w_gate'], 'v_ffn_w_up': out['v_ffn_w_up'], 'v_ffn_w_down': out['v_ffn_w_down'], 'v_ssd_w_in': out['v_ssd_w_in'], 'v_ssd_conv_w': out['v_ssd_conv_w'], 'v_ssd_conv_b': out['v_ssd_conv_b'], 'v_ssd_dt_bias': out['v_ssd_dt_bias'], 'v_ssd_a_log': out['v_ssd_a_log'], 'v_ssd_d': out['v_ssd_d'], 'v_ssd_norm_w': out['v_ssd_norm_w'], 'v_ssd_w_out': out['v_ssd_w_out'], 'v_sc_w_in': out['v_sc_w_in'], 'v_sc_conv_w': out['v_sc_conv_w'], 'v_sc_w_out': out['v_sc_w_out'], 'v_final_norm_w': out['v_final_norm_w']}


def _loss(weights, diff, rest, loss_target):
    with _jax.named_scope("forward"):
        args = {**rest, TWIN_DIFF_INPUT: diff, **{k: w.astype(_WEIGHT_DTYPES[k]) for k, w in weights.items()}}
        y = _forward(args)
    with _jax.named_scope("loss_head"):
        err = _jnp.square(y.astype(_jnp.float32) - loss_target)
        return 0.5 * _jnp.sum(_jnp.mean(err, axis=-1)) if err.ndim else 0.5 * err


def _adamw(w, g, m, v):
    m = ADAM_B1 * m + (1.0 - ADAM_B1) * g
    v = ADAM_B2 * v + (1.0 - ADAM_B2) * _jnp.square(g)
    m_hat = m / (1.0 - ADAM_B1 ** ADAM_STEP)
    v_hat = v / (1.0 - ADAM_B2 ** ADAM_STEP)
    delta = -ADAM_LR * (m_hat / (_jnp.sqrt(v_hat) + ADAM_EPS) + ADAM_WD * w)
    return delta, m, v


def reference(x, norm_w, ffn_w_gate, ffn_w_up, ffn_w_down, ssd_w_in, ssd_conv_w, ssd_conv_b, ssd_dt_bias, ssd_a_log, ssd_d, ssd_norm_w, ssd_w_out, sc_w_in, sc_conv_w, sc_w_out, final_norm_w, loss_target, m_norm_w, m_ffn_w_gate, m_ffn_w_up, m_ffn_w_down, m_ssd_w_in, m_ssd_conv_w, m_ssd_conv_b, m_ssd_dt_bias, m_ssd_a_log, m_ssd_d, m_ssd_norm_w, m_ssd_w_out, m_sc_w_in, m_sc_conv_w, m_sc_w_out, m_final_norm_w, v_norm_w, v_ffn_w_gate, v_ffn_w_up, v_ffn_w_down, v_ssd_w_in, v_ssd_conv_w, v_ssd_conv_b, v_ssd_dt_bias, v_ssd_a_log, v_ssd_d, v_ssd_norm_w, v_ssd_w_out, v_sc_w_in, v_sc_conv_w, v_sc_w_out, v_final_norm_w):
    given = dict(x=x, norm_w=norm_w, ffn_w_gate=ffn_w_gate, ffn_w_up=ffn_w_up, ffn_w_down=ffn_w_down, ssd_w_in=ssd_w_in, ssd_conv_w=ssd_conv_w, ssd_conv_b=ssd_conv_b, ssd_dt_bias=ssd_dt_bias, ssd_a_log=ssd_a_log, ssd_d=ssd_d, ssd_norm_w=ssd_norm_w, ssd_w_out=ssd_w_out, sc_w_in=sc_w_in, sc_conv_w=sc_conv_w, sc_w_out=sc_w_out, final_norm_w=final_norm_w, loss_target=loss_target, m_norm_w=m_norm_w, m_ffn_w_gate=m_ffn_w_gate, m_ffn_w_up=m_ffn_w_up, m_ffn_w_down=m_ffn_w_down, m_ssd_w_in=m_ssd_w_in, m_ssd_conv_w=m_ssd_conv_w, m_ssd_conv_b=m_ssd_conv_b, m_ssd_dt_bias=m_ssd_dt_bias, m_ssd_a_log=m_ssd_a_log, m_ssd_d=m_ssd_d, m_ssd_norm_w=m_ssd_norm_w, m_ssd_w_out=m_ssd_w_out, m_sc_w_in=m_sc_w_in, m_sc_conv_w=m_sc_conv_w, m_sc_w_out=m_sc_w_out, m_final_norm_w=m_final_norm_w, v_norm_w=v_norm_w, v_ffn_w_gate=v_ffn_w_gate, v_ffn_w_up=v_ffn_w_up, v_ffn_w_down=v_ffn_w_down, v_ssd_w_in=v_ssd_w_in, v_ssd_conv_w=v_ssd_conv_w, v_ssd_conv_b=v_ssd_conv_b, v_ssd_dt_bias=v_ssd_dt_bias, v_ssd_a_log=v_ssd_a_log, v_ssd_d=v_ssd_d, v_ssd_norm_w=v_ssd_norm_w, v_ssd_w_out=v_ssd_w_out, v_sc_w_in=v_sc_w_in, v_sc_conv_w=v_sc_conv_w, v_sc_w_out=v_sc_w_out, v_final_norm_w=v_final_norm_w)
    weights = {n: given[n] for n in TWIN_WEIGHTS}
    shared = {n: given[n] for n in SHARED_INPUTS}
    per_example = {n: given[n] for n in ['x']}
    grad_fn = _jax.value_and_grad(_loss, argnums=(0, 1))

    def one_microbatch(ex, loss_target):
        ex = dict(ex)
        diff = ex.pop(TWIN_DIFF_INPUT)
        return grad_fn(weights, diff, {**shared, **ex}, loss_target)

    if N_MICROBATCH == 1:
        loss, (grad_w, grad_x) = one_microbatch(per_example, given["loss_target"])
    else:
        def body(carry, xs):
            loss_sum, grad_sum = carry
            l_k, (gw_k, gx_k) = one_microbatch(xs[0], xs[1])
            with _jax.named_scope("update"):
                return (loss_sum + l_k, _jax.tree.map(_jnp.add, grad_sum, gw_k)), gx_k

        init = (_jnp.zeros((), _jnp.float32), _jax.tree.map(_jnp.zeros_like, weights))
        (loss, grad_w), grad_x = _jax.lax.scan(body, init, (per_example, given["loss_target"]))
    with _jax.named_scope("update"):
        delta_w, new_m, new_v = {}, {}, {}
        for n in TWIN_WEIGHTS:
            delta_w[n], new_m[n], new_v[n] = _adamw(weights[n], grad_w[n], given["m_" + n], given["v_" + n])
    return (loss, grad_x, *[grad_w[n] for n in TWIN_WEIGHTS], *[delta_w[n] for n in TWIN_WEIGHTS],
            *[new_m[n] for n in TWIN_WEIGHTS], *[new_v[n] for n in TWIN_WEIGHTS])
```

```python
import functools

import jax
import jax.numpy as jnp
from jax import lax
from jax.experimental import pallas as pl
from jax.experimental.pallas import tpu as pltpu

f32 = jnp.float32
bf16 = jnp.bfloat16

D_MODEL = 1024
D_FF = 2816
DEPTH = 4
SSD_INNER = 2048
SSD_HEADS = 32
SSD_HEAD_DIM = 64
SSD_GROUPS = 4
SSD_STATE = 128
SSD_CONV_K = 4
SSD_CONV_DIM = 3072
SSD_IN_DIM = 5152
SSD_CHUNK = 128
SC_CONV_K = 3
RMS_EPS = 1e-5
N_DEV = 8
LANES = 128
HALO = 16
PACK_W = 1024
PACK_TILE = 256
VMEM_LIMIT = 56 * 1024 * 1024
NEG_BIG = -1e30

ADAM_LR = 0.001
ADAM_B1 = 0.9
ADAM_B2 = 0.999
ADAM_EPS = 1e-08
ADAM_WD = 0.01
ADAM_STEP = 10

NT_DIMS = (((1,), (1,)), ((), ()))
TN_DIMS = (((0,), (0,)), ((), ()))
MESH = pl.DeviceIdType.MESH


def _params(sem=None):
    return pltpu.CompilerParams(dimension_semantics=sem, vmem_limit_bytes=VMEM_LIMIT)


def _resident(shape):
    nd = len(shape)
    return pl.BlockSpec(tuple(shape), lambda *_: (0,) * nd, pipeline_mode=pl.Buffered(1))


def _rows(tm, width):
    return pl.BlockSpec((tm, width), lambda i: (i, 0))


def _sigmoid(v):
    return 1.0 / (1.0 + jnp.exp(-v))


def _softplus(v):
    return jnp.maximum(v, 0.0) + jnp.log(1.0 + jnp.exp(-jnp.abs(v)))


def _rms_fwd(xv, w):
    inv = lax.rsqrt(jnp.mean(xv * xv, axis=-1, keepdims=True) + RMS_EPS)
    xh = xv * inv
    return xh * w, xh, inv


def _rms_bwd(dh, xh, inv, w):
    dxh = dh * w
    dx = inv * (dxh - xh * jnp.mean(dxh * xh, axis=-1, keepdims=True))
    return dx, jnp.sum(dh * xh, axis=0, keepdims=True)


def _mm(a, b):
    return jnp.dot(a, b, preferred_element_type=f32)


def _mm_nt(a, b):
    return lax.dot_general(a, b, NT_DIMS, preferred_element_type=f32)


def _mm_tn(a, b):
    return lax.dot_general(a, b, TN_DIMS, preferred_element_type=f32)


def ffn_fwd(x, nw, wg, wu, wd, tm=256):
    T = x.shape[0]

    def body(x_ref, nw_ref, wg_ref, wu_ref, wd_ref, xo_ref, g_ref, u_ref, a_ref):
        xv = x_ref[...]
        h, _, _ = _rms_fwd(xv, nw_ref[...])
        hb = h.astype(bf16)
        g = _mm(hb, wg_ref[...])
        u = _mm(hb, wu_ref[...])
        ab = (g * _sigmoid(g) * u).astype(bf16)
        xo_ref[...] = xv + 0.5 * _mm(ab, wd_ref[...])
        g_ref[...] = g.astype(bf16)
        u_ref[...] = u.astype(bf16)
        a_ref[...] = ab

    return pl.pallas_call(
        body, name="ffn_fwd", grid=(T // tm,),
        in_specs=[_rows(tm, D_MODEL), _resident((1, D_MODEL)), _resident(wg.shape), _resident(wu.shape), _resident(wd.shape)],
        out_specs=[_rows(tm, D_MODEL), _rows(tm, D_FF), _rows(tm, D_FF), _rows(tm, D_FF)],
        out_shape=[jax.ShapeDtypeStruct((T, D_MODEL), f32)] + [jax.ShapeDtypeStruct((T, D_FF), bf16)] * 3,
        compiler_params=_params(("parallel",)),
    )(x, nw, wg, wu, wd)


def ffn_bwd_dx(x, dxo, g, u, nw, wg, wu, wd, tm=256):
    T = x.shape[0]

    def body(x_ref, dxo_ref, g_ref, u_ref, nw_ref, wg_ref, wu_ref, wd_ref, dx_ref, h_ref, dy_ref, dg_ref, du_ref, dnw_ref):
        w = nw_ref[...]
        h, xh, inv = _rms_fwd(x_ref[...], w)
        dxo_v = dxo_ref[...]
        dyb = (0.5 * dxo_v).astype(bf16)
        da = _mm_nt(dyb, wd_ref[...])
        gv = g_ref[...].astype(f32)
        uv = u_ref[...].astype(f32)
        s = _sigmoid(gv)
        dgb = (da * uv * (s * (1.0 + gv * (1.0 - s)))).astype(bf16)
        dub = (da * (gv * s)).astype(bf16)
        dh = _mm_nt(dgb, wg_ref[...]) + _mm_nt(dub, wu_ref[...])
        dxn, dw = _rms_bwd(dh, xh, inv, w)
        dx_ref[...] = dxo_v + dxn
        h_ref[...] = h.astype(bf16)
        dy_ref[...] = dyb
        dg_ref[...] = dgb
        du_ref[...] = dub

        @pl.when(pl.program_id(0) == 0)
        def _():
            dnw_ref[...] = jnp.zeros_like(dnw_ref)

        dnw_ref[...] += dw

    return pl.pallas_call(
        body, name="ffn_bwd_dx", grid=(T // tm,),
        in_specs=[_rows(tm, D_MODEL), _rows(tm, D_MODEL), _rows(tm, D_FF), _rows(tm, D_FF), _resident((1, D_MODEL)),
                  _resident(wg.shape), _resident(wu.shape), _resident(wd.shape)],
        out_specs=[_rows(tm, D_MODEL), _rows(tm, D_MODEL), _rows(tm, D_MODEL), _rows(tm, D_FF), _rows(tm, D_FF),
                   pl.BlockSpec((1, D_MODEL), lambda i: (0, 0))],
        out_shape=[jax.ShapeDtypeStruct((T, D_MODEL), f32), jax.ShapeDtypeStruct((T, D_MODEL), bf16), jax.ShapeDtypeStruct((T, D_MODEL), bf16),
                   jax.ShapeDtypeStruct((T, D_FF), bf16), jax.ShapeDtypeStruct((T, D_FF), bf16), jax.ShapeDtypeStruct((1, D_MODEL), f32)],
        compiler_params=_params(("arbitrary",)),
    )(x, dxo, g, u, nw, wg, wu, wd)


def tn_matmul(a, b, tk=512):
    T, M = a.shape
    N = b.shape[1]
    bn = N if M * N <= 3_200_000 else N // 2
    nk = T // tk

    def body(a_ref, b_ref, o_ref):
        @pl.when(pl.program_id(1) == 0)
        def _():
            o_ref[...] = jnp.zeros_like(o_ref)

        o_ref[...] += _mm_tn(a_ref[...], b_ref[...])

    return pl.pallas_call(
        body, name=f"tn_matmul_{M}x{N}", grid=(N // bn, nk),
        in_specs=[pl.BlockSpec((tk, M), lambda j, k: (k, 0)), pl.BlockSpec((tk, bn), lambda j, k: (k, j))],
        out_specs=pl.BlockSpec((M, bn), lambda j, k: (0, j)),
        out_shape=jax.ShapeDtypeStruct((M, N), f32),
        compiler_params=_params(("parallel", "arbitrary")),
    )(a, b)


def in_proj_fwd(x, nw, ws, out_dtypes, tm=256):
    T = x.shape[0]
    n = len(ws)

    def body(*refs):
        x_ref, nw_ref = refs[:2]
        w_refs = refs[2:2 + n]
        o_refs = refs[2 + n:]
        h, _, _ = _rms_fwd(x_ref[...], nw_ref[...])
        hb = h.astype(bf16)
        for w_ref, o_ref in zip(w_refs, o_refs):
            o_ref[...] = _mm(hb, w_ref[...]).astype(o_ref.dtype)

    return pl.pallas_call(
        body, name="in_proj_fwd_" + "_".join(str(w.shape[1]) for w in ws), grid=(T // tm,),
        in_specs=[_rows(tm, D_MODEL), _resident((1, D_MODEL))] + [_resident(w.shape) for w in ws],
        out_specs=[_rows(tm, w.shape[1]) for w in ws],
        out_shape=[jax.ShapeDtypeStruct((T, w.shape[1]), dt) for w, dt in zip(ws, out_dtypes)],
        compiler_params=_params(("parallel",)),
    )(x, nw, *ws)


def in_proj_bwd(x, nw, dxo, dys, ws, tm=256):
    T = x.shape[0]
    n = len(ws)

    def body(*refs):
        x_ref, nw_ref, dxo_ref = refs[:3]
        dy_refs = refs[3:3 + n]
        w_refs = refs[3 + n:3 + 2 * n]
        dx_ref, h_ref, dnw_ref = refs[3 + 2 * n:]
        w = nw_ref[...]
        h, xh, inv = _rms_fwd(x_ref[...], w)
        dh = _mm_nt(dy_refs[0][...], w_refs[0][...])
        for dy_ref, w_ref in zip(dy_refs[1:], w_refs[1:]):
            dh = dh + _mm_nt(dy_ref[...], w_ref[...])
        dxn, dw = _rms_bwd(dh, xh, inv, w)
        dx_ref[...] = dxo_ref[...] + dxn
        h_ref[...] = h.astype(bf16)

        @pl.when(pl.program_id(0) == 0)
        def _():
            dnw_ref[...] = jnp.zeros_like(dnw_ref)

        dnw_ref[...] += dw

    return pl.pallas_call(
        body, name="in_proj_bwd_" + "_".join(str(w.shape[1]) for w in ws), grid=(T // tm,),
        in_specs=[_rows(tm, D_MODEL), _resident((1, D_MODEL)), _rows(tm, D_MODEL)] + [_rows(tm, w.shape[1]) for w in ws]
        + [_resident(w.shape) for w in ws],
        out_specs=[_rows(tm, D_MODEL), _rows(tm, D_MODEL), pl.BlockSpec((1, D_MODEL), lambda i: (0, 0))],
        out_shape=[jax.ShapeDtypeStruct((T, D_MODEL), f32), jax.ShapeDtypeStruct((T, D_MODEL), bf16), jax.ShapeDtypeStruct((1, D_MODEL), f32)],
        compiler_params=_params(("arbitrary",)),
    )(x, nw, dxo, *dys, *ws)


def out_proj_fwd(x, a, w, tm=512):
    T = x.shape[0]
    K = a.shape[1]

    def body(x_ref, a_ref, w_ref, o_ref):
        o_ref[...] = x_ref[...] + _mm(a_ref[...], w_ref[...])

    return pl.pallas_call(
        body, name=f"out_proj_fwd_{K}", grid=(T // tm,),
        in_specs=[_rows(tm, D_MODEL), _rows(tm, K), _resident(w.shape)],
        out_specs=_rows(tm, D_MODEL), out_shape=jax.ShapeDtypeStruct((T, D_MODEL), f32),
        compiler_params=_params(("parallel",)),
    )(x, a, w)


def out_proj_bwd(dxo, w, tm=512):
    T = dxo.shape[0]
    K = w.shape[0]

    def body(dxo_ref, w_ref, da_ref, dy_ref):
        dyb = dxo_ref[...].astype(bf16)
        dy_ref[...] = dyb
        da_ref[...] = _mm_nt(dyb, w_ref[...]).astype(bf16)

    return pl.pallas_call(
        body, name=f"out_proj_bwd_{K}", grid=(T // tm,),
        in_specs=[_rows(tm, D_MODEL), _resident(w.shape)],
        out_specs=[_rows(tm, K), _rows(tm, D_MODEL)],
        out_shape=[jax.ShapeDtypeStruct((T, K), bf16), jax.ShapeDtypeStruct((T, D_MODEL), bf16)],
        compiler_params=_params(("parallel",)),
    )(dxo, w)


def _halo_spec(tm, width, n_tiles, reverse):
    per = tm // HALO

    def idx(i):
        t = (n_tiles - 1 - i) if reverse else i
        return (jnp.maximum(t * per - 1, 0), 0)

    return pl.BlockSpec((HALO, width), idx)


def _tile_spec(tm, width, n_tiles, reverse):
    if reverse:
        return pl.BlockSpec((tm, width), lambda i: (n_tiles - 1 - i, 0))
    return _rows(tm, width)


def _fill_window(ext_ref, tile, halo, first):
    ext_ref[pl.ds(0, HALO), :] = jnp.where(first, 0.0, halo)
    ext_ref[pl.ds(HALO, tile.shape[0]), :] = tile


def _causal_conv(ext_ref, w_ref, k_w, tm):
    base = HALO - (k_w - 1)
    out = w_ref[pl.ds(0, 1), :] * ext_ref[pl.ds(base, tm), :]
    for k in range(1, k_w):
        out = out + w_ref[pl.ds(k, 1), :] * ext_ref[pl.ds(base + k, tm), :]
    return out


def ssd_conv_fwd(xbc, conv_w, conv_b, dt_raw, dt_bias, tm=256):
    T = xbc.shape[0]
    nt = T // tm

    def body(x_ref, halo_ref, w_ref, b_ref, dtr_ref, dtb_ref, act_ref, dt_ref, ext_ref):
        _fill_window(ext_ref, x_ref[...].astype(f32), halo_ref[...].astype(f32), pl.program_id(0) == 0)
        pre = _causal_conv(ext_ref, w_ref, SSD_CONV_K, tm) + b_ref[...]
        act_ref[...] = (pre * _sigmoid(pre)).astype(bf16)
        dt_ref[...] = _softplus(dtr_ref[...] + dtb_ref[...])

    return pl.pallas_call(
        body, name="ssd_conv_fwd", grid=(nt,),
        in_specs=[_rows(tm, SSD_CONV_DIM), _halo_spec(tm, SSD_CONV_DIM, nt, False), _resident(conv_w.shape), _resident(conv_b.shape),
                  _rows(tm, LANES), _resident(dt_bias.shape)],
        out_specs=[_rows(tm, SSD_CONV_DIM), _rows(tm, LANES)],
        out_shape=[jax.ShapeDtypeStruct((T, SSD_CONV_DIM), bf16), jax.ShapeDtypeStruct((T, LANES), f32)],
        scratch_shapes=[pltpu.VMEM((tm + HALO, SSD_CONV_DIM), f32)],
        compiler_params=_params(("parallel",)),
    )(xbc, xbc, conv_w, conv_b, dt_raw, dt_bias)


def ssd_conv_bwd(xbc, conv_w, conv_b, dt_raw, dt_bias, dxs_a, dxs_b, db, dc, ddt, tm=256):
    T = xbc.shape[0]
    nt = T // tm
    K = SSD_CONV_K

    def body(x_ref, halo_ref, w_ref, b_ref, dtr_ref, dtb_ref, da_ref, dbb_ref, db_ref, dc_ref, ddt_ref,
             dx_ref, ddtr_ref, dw_ref, dbias_ref, ddtb_ref, ext_ref, dpre_ref, carry_ref):
        i = pl.program_id(0)

        @pl.when(i == 0)
        def _():
            carry_ref[...] = jnp.zeros_like(carry_ref)
            dw_ref[...] = jnp.zeros_like(dw_ref)
            dbias_ref[...] = jnp.zeros_like(dbias_ref)
            ddtb_ref[...] = jnp.zeros_like(ddtb_ref)

        _fill_window(ext_ref, x_ref[...].astype(f32), halo_ref[...].astype(f32), i == nt - 1)
        pre = _causal_conv(ext_ref, w_ref, K, tm) + b_ref[...]
        s = _sigmoid(pre)
        dact = jnp.concatenate([da_ref[...].astype(f32) + dbb_ref[...].astype(f32), db_ref[...].astype(f32), dc_ref[...].astype(f32)], axis=1)
        dpre = dact * (s * (1.0 + pre * (1.0 - s)))
        dpre_ref[pl.ds(0, tm), :] = dpre
        dpre_ref[pl.ds(tm, 8), :] = carry_ref[...]
        dx = w_ref[pl.ds(0, 1), :] * dpre_ref[pl.ds(K - 1, tm), :]
        for k in range(1, K):
            dx = dx + w_ref[pl.ds(k, 1), :] * dpre_ref[pl.ds(K - 1 - k, tm), :]
        dx_ref[...] = dx.astype(bf16)
        base = HALO - (K - 1)
        for k in range(K):
            dw_ref[pl.ds(k, 1), :] += jnp.sum(dpre * ext_ref[pl.ds(base + k, tm), :], axis=0, keepdims=True)
        dbias_ref[...] += jnp.sum(dpre, axis=0, keepdims=True)
        carry_ref[...] = dpre[0:8, :]
        ddtr = ddt_ref[...] * _sigmoid(dtr_ref[...] + dtb_ref[...])
        ddtr_ref[...] = ddtr.astype(bf16)
        ddtb_ref[...] += jnp.sum(ddtr, axis=0, keepdims=True)

    rev = functools.partial(_tile_spec, tm, n_tiles=nt, reverse=True)
    const = lambda shape: pl.BlockSpec(shape, lambda i: (0, 0))
    return pl.pallas_call(
        body, name="ssd_conv_bwd", grid=(nt,),
        in_specs=[rev(width=SSD_CONV_DIM), _halo_spec(tm, SSD_CONV_DIM, nt, True), _resident(conv_w.shape), _resident(conv_b.shape),
                  rev(width=LANES), _resident(dt_bias.shape), rev(width=SSD_INNER), rev(width=SSD_INNER),
                  rev(width=SSD_GROUPS * SSD_STATE), rev(width=SSD_GROUPS * SSD_STATE), rev(width=LANES)],
        out_specs=[rev(width=SSD_CONV_DIM), rev(width=LANES), const((8, SSD_CONV_DIM)), const((1, SSD_CONV_DIM)), const((1, LANES))],
        out_shape=[jax.ShapeDtypeStruct((T, SSD_CONV_DIM), bf16), jax.ShapeDtypeStruct((T, LANES), bf16),
                   jax.ShapeDtypeStruct((8, SSD_CONV_DIM), f32), jax.ShapeDtypeStruct((1, SSD_CONV_DIM), f32), jax.ShapeDtypeStruct((1, LANES), f32)],
        scratch_shapes=[pltpu.VMEM((tm + HALO, SSD_CONV_DIM), f32), pltpu.VMEM((tm + 8, SSD_CONV_DIM), f32), pltpu.VMEM((8, SSD_CONV_DIM), f32)],
        compiler_params=_params(("arbitrary",)),
    )(xbc, xbc, conv_w, conv_b, dt_raw, dt_bias, dxs_a, dxs_b, db, dc, ddt)


def _ssd_chunk(xs, bm, cm, dt, alog, st):
    L = SSD_CHUNK
    row = lax.broadcasted_iota(jnp.int32, (L, L), 0)
    col = lax.broadcasted_iota(jnp.int32, (L, L), 1)
    causal = row >= col
    tril = jnp.where(causal, 1.0, 0.0).astype(f32)
    lane = lax.broadcasted_iota(jnp.int32, (1, LANES), 1)
    sub = lax.broadcasted_iota(jnp.int32, (LANES, 1), 0)
    lo = lane < SSD_HEAD_DIM
    last_row = sub == L - 1

    dta = dt * (-jnp.exp(alog))
    a_cs = jnp.dot(tril, dta, precision=lax.Precision.HIGHEST, preferred_element_type=f32)
    a_cs_t = a_cs.T
    bmb = bm.astype(bf16)
    cmb = cm.astype(bf16)
    cb = _mm_nt(cmb, bmb)
    c_st = _mm(cmb, st.astype(bf16))

    def head_col(v, e):
        return jnp.sum(jnp.where(lane == e, v, 0.0), axis=1, keepdims=True)

    def head_row(v, e):
        return jnp.sum(jnp.where(sub == e, v, 0.0), axis=0, keepdims=True)

    ys, sts = [], []
    for j in range(4):
        e0, e1 = 2 * j, 2 * j + 1
        c0, c1 = head_col(a_cs, e0), head_col(a_cs, e1)
        acs_x = jnp.where(lo, c0, c1)
        dt_x = jnp.where(lo, head_col(dt, e0), head_col(dt, e1))
        xd = xs[:, j * LANES:(j + 1) * LANES] * dt_x
        m0 = cb * jnp.exp(jnp.where(causal, c0 - head_row(a_cs_t, e0), NEG_BIG))
        m1 = cb * jnp.exp(jnp.where(causal, c1 - head_row(a_cs_t, e1), NEG_BIG))
        mcat = jnp.concatenate([m0, m1], axis=1).astype(bf16)
        xcat = jnp.concatenate([jnp.where(lo, xd, 0.0), jnp.where(lo, 0.0, xd)], axis=0).astype(bf16)
        y_diag = _mm(mcat, xcat)
        a_last = jnp.sum(jnp.where(last_row, acs_x, 0.0), axis=0, keepdims=True)
        x_dec = (xd * jnp.exp(a_last - acs_x)).astype(bf16)
        s_new = _mm_tn(bmb, x_dec)
        y_off = c_st[:, j * LANES:(j + 1) * LANES] * jnp.exp(acs_x)
        ys.append(y_diag + y_off)
        sts.append(jnp.exp(a_last) * st[:, j * LANES:(j + 1) * LANES] + s_new)
    return jnp.concatenate(ys, axis=1), jnp.concatenate(sts, axis=1)


def _scan_specs(nc, reverse):
    L = SSD_CHUNK
    ch = (lambda c: nc - 1 - c) if reverse else (lambda c: c)
    gw = SSD_INNER // SSD_GROUPS
    xs = pl.BlockSpec((L, gw), lambda g, c: (ch(c), g))
    bm = pl.BlockSpec((L, SSD_STATE), lambda g, c: (ch(c), SSD_INNER // SSD_STATE + g))
    cm = pl.BlockSpec((L, SSD_STATE), lambda g, c: (ch(c), SSD_INNER // SSD_STATE + SSD_GROUPS + g))
    dt = pl.BlockSpec((None, L, LANES), lambda g, c: (g, ch(c), 0))
    alog = pl.BlockSpec((None, 1, LANES), lambda g, c: (g, 0, 0))
    st = pl.BlockSpec((None, None, SSD_STATE, gw), lambda g, c: (g, ch(c), 0, 0))
    y = pl.BlockSpec((L, gw), lambda g, c: (ch(c), g))
    grp = pl.BlockSpec((L, SSD_STATE), lambda g, c: (ch(c), g))
    return xs, bm, cm, dt, alog, st, y, grp


def ssd_scan_fwd(act, dt4, alog4):
    T = act.shape[0]
    nc = T // SSD_CHUNK
    gw = SSD_INNER // SSD_GROUPS
    xs_s, bm_s, cm_s, dt_s, alog_s, st_s, y_s, _ = _scan_specs(nc, False)

    def body(xs_ref, bm_ref, cm_ref, dt_ref, alog_ref, y_ref, st_ref, st_scr):
        @pl.when(pl.program_id(1) == 0)
        def _():
            st_scr[...] = jnp.zeros_like(st_scr)

        st = st_scr[...]
        st_ref[...] = st
        y, st_new = _ssd_chunk(xs_ref[...].astype(f32), bm_ref[...].astype(f32), cm_ref[...].astype(f32), dt_ref[...], alog_ref[...], st)
        y_ref[...] = y.astype(bf16)
        st_scr[...] = st_new

    return pl.pallas_call(
        body, name="ssd_scan_fwd", grid=(SSD_GROUPS, nc),
        in_specs=[xs_s, bm_s, cm_s, dt_s, alog_s], out_specs=[y_s, st_s],
        out_shape=[jax.ShapeDtypeStruct((T, SSD_INNER), bf16), jax.ShapeDtypeStruct((SSD_GROUPS, nc, SSD_STATE, gw), f32)],
        scratch_shapes=[pltpu.VMEM((SSD_STATE, gw), f32)],
        compiler_params=_params(("parallel", "arbitrary")),
    )(act, act, act, dt4, alog4)


def ssd_scan_bwd(act, dt4, alog4, states, dy):
    T = act.shape[0]
    nc = T // SSD_CHUNK
    gw = SSD_INNER // SSD_GROUPS
    xs_s, bm_s, cm_s, dt_s, alog_s, st_s, y_s, grp_s = _scan_specs(nc, True)

    def body(xs_ref, bm_ref, cm_ref, dt_ref, alog_ref, st_ref, dy_ref, dxs_ref, db_ref, dc_ref, ddt_ref, dalog_ref, dst_scr):
        @pl.when(pl.program_id(1) == 0)
        def _():
            dst_scr[...] = jnp.zeros_like(dst_scr)
            dalog_ref[...] = jnp.zeros_like(dalog_ref)

        _, vjp = jax.vjp(_ssd_chunk, xs_ref[...].astype(f32), bm_ref[...].astype(f32), cm_ref[...].astype(f32),
                         dt_ref[...], alog_ref[...], st_ref[...])
        dxs, dbm, dcm, ddt, dalog, dst = vjp((dy_ref[...].astype(f32), dst_scr[...]))
        dxs_ref[...] = dxs.astype(bf16)
        db_ref[...] = dbm.astype(bf16)
        dc_ref[...] = dcm.astype(bf16)
        ddt_ref[...] = ddt
        dalog_ref[...] += dalog
        dst_scr[...] = dst

    return pl.pallas_call(
        body, name="ssd_scan_bwd", grid=(SSD_GROUPS, nc),
        in_specs=[xs_s, bm_s, cm_s, dt_s, alog_s, st_s, y_s],
        out_specs=[y_s, grp_s, grp_s, dt_s, alog_s],
        out_shape=[jax.ShapeDtypeStruct((T, SSD_INNER), bf16), jax.ShapeDtypeStruct((T, SSD_GROUPS * SSD_STATE), bf16),
                   jax.ShapeDtypeStruct((T, SSD_GROUPS * SSD_STATE), bf16), jax.ShapeDtypeStruct((SSD_GROUPS, T, LANES), f32),
                   jax.ShapeDtypeStruct((SSD_GROUPS, 1, LANES), f32)],
        scratch_shapes=[pltpu.VMEM((SSD_STATE, gw), f32)],
        compiler_params=_params(("parallel", "arbitrary")),
    )(act, act, act, dt4, alog4, states, dy)


def _ssd_gate(y, xs, z, d_x, nw):
    g = (y + xs * d_x) * (z * _sigmoid(z))
    gw = SSD_INNER // SSD_GROUPS
    outs = []
    for k in range(SSD_GROUPS):
        gk = g[:, k * gw:(k + 1) * gw]
        outs.append(gk * lax.rsqrt(jnp.mean(gk * gk, axis=-1, keepdims=True) + RMS_EPS))
    return jnp.concatenate(outs, axis=1) * nw


def ssd_gate_fwd(y, act, z, d_x, nw, tm=256):
    T = y.shape[0]

    def body(y_ref, xs_ref, z_ref, d_ref, nw_ref, o_ref):
        o_ref[...] = _ssd_gate(y_ref[...].astype(f32), xs_ref[...].astype(f32), z_ref[...].astype(f32), d_ref[...], nw_ref[...]).astype(bf16)

    return pl.pallas_call(
        body, name="ssd_gate_fwd", grid=(T // tm,),
        in_specs=[_rows(tm, SSD_INNER), _rows(tm, SSD_INNER), _rows(tm, SSD_INNER), _resident(d_x.shape), _resident(nw.shape)],
        out_specs=_rows(tm, SSD_INNER), out_shape=jax.ShapeDtypeStruct((T, SSD_INNER), bf16),
        compiler_params=_params(("parallel",)),
    )(y, act, z, d_x, nw)


def ssd_gate_bwd(y, act, z, d_x, nw, dgn, tm=256):
    T = y.shape[0]

    def body(y_ref, xs_ref, z_ref, d_ref, nw_ref, dgn_ref, dy_ref, dxs_ref, dz_ref, dd_ref, dnw_ref):
        @pl.when(pl.program_id(0) == 0)
        def _():
            dd_ref[...] = jnp.zeros_like(dd_ref)
            dnw_ref[...] = jnp.zeros_like(dnw_ref)

        _, vjp = jax.vjp(_ssd_gate, y_ref[...].astype(f32), xs_ref[...].astype(f32), z_ref[...].astype(f32), d_ref[...], nw_ref[...])
        dy, dxs, dz, dd, dnw = vjp(dgn_ref[...].astype(f32))
        dy_ref[...] = dy.astype(bf16)
        dxs_ref[...] = dxs.astype(bf16)
        dz_ref[...] = dz.astype(bf16)
        dd_ref[...] += dd
        dnw_ref[...] += dnw

    const = pl.BlockSpec((1, SSD_INNER), lambda i: (0, 0))
    return pl.pallas_call(
        body, name="ssd_gate_bwd", grid=(T // tm,),
        in_specs=[_rows(tm, SSD_INNER), _rows(tm, SSD_INNER), _rows(tm, SSD_INNER), _resident(d_x.shape), _resident(nw.shape), _rows(tm, SSD_INNER)],
        out_specs=[_rows(tm, SSD_INNER)] * 3 + [const, const],
        out_shape=[jax.ShapeDtypeStruct((T, SSD_INNER), bf16)] * 3 + [jax.ShapeDtypeStruct((1, SSD_INNER), f32)] * 2,
        compiler_params=_params(("arbitrary",)),
    )(y, act, z, d_x, nw, dgn)


def sc_mid_fwd(bcu, conv_w, tm=256):
    T = bcu.shape[0]
    nt = T // tm
    Dm = D_MODEL

    def body(x_ref, halo_ref, w_ref, q_ref, ext_ref):
        xv = x_ref[...].astype(f32)
        hv = halo_ref[...].astype(f32)
        _fill_window(ext_ref, xv[:, Dm:2 * Dm] * xv[:, 2 * Dm:], hv[:, Dm:2 * Dm] * hv[:, 2 * Dm:], pl.program_id(0) == 0)
        q_ref[...] = (xv[:, :Dm] * _causal_conv(ext_ref, w_ref, SC_CONV_K, tm)).astype(bf16)

    return pl.pallas_call(
        body, name="sc_mid_fwd", grid=(nt,),
        in_specs=[_rows(tm, 3 * Dm), _halo_spec(tm, 3 * Dm, nt, False), _resident(conv_w.shape)],
        out_specs=_rows(tm, Dm), out_shape=jax.ShapeDtypeStruct((T, Dm), bf16),
        scratch_shapes=[pltpu.VMEM((tm + HALO, Dm), f32)],
        compiler_params=_params(("parallel",)),
    )(bcu, bcu, conv_w)


def sc_mid_bwd(bcu, conv_w, dq, tm=256):
    T = bcu.shape[0]
    nt = T // tm
    Dm = D_MODEL
    K = SC_CONV_K

    def body(x_ref, halo_ref, w_ref, dq_ref, dx_ref, dw_ref, ext_ref, dv_ref, carry_ref):
        i = pl.program_id(0)

        @pl.when(i == 0)
        def _():
            carry_ref[...] = jnp.zeros_like(carry_ref)
            dw_ref[...] = jnp.zeros_like(dw_ref)

        xv = x_ref[...].astype(f32)
        hv = halo_ref[...].astype(f32)
        bg, cg, uu = xv[:, :Dm], xv[:, Dm:2 * Dm], xv[:, 2 * Dm:]
        _fill_window(ext_ref, cg * uu, hv[:, Dm:2 * Dm] * hv[:, 2 * Dm:], i == nt - 1)
        v = _causal_conv(ext_ref, w_ref, K, tm)
        dqv = dq_ref[...].astype(f32)
        dv = dqv * bg
        dv_ref[pl.ds(0, tm), :] = dv
        dv_ref[pl.ds(tm, 8), :] = carry_ref[...]
        dp = w_ref[pl.ds(0, 1), :] * dv_ref[pl.ds(K - 1, tm), :]
        for k in range(1, K):
            dp = dp + w_ref[pl.ds(k, 1), :] * dv_ref[pl.ds(K - 1 - k, tm), :]
        base = HALO - (K - 1)
        for k in range(K):
            dw_ref[pl.ds(k, 1), :] += jnp.sum(dv * ext_ref[pl.ds(base + k, tm), :], axis=0, keepdims=True)
        carry_ref[...] = dv[0:8, :]
        dx_ref[...] = jnp.concatenate([dqv * v, dp * uu, dp * cg], axis=1).astype(bf16)

    return pl.pallas_call(
        body, name="sc_mid_bwd", grid=(nt,),
        in_specs=[_tile_spec(tm, 3 * Dm, nt, True), _halo_spec(tm, 3 * Dm, nt, True), _resident(conv_w.shape), _tile_spec(tm, Dm, nt, True)],
        out_specs=[_tile_spec(tm, 3 * Dm, nt, True), pl.BlockSpec((8, Dm), lambda i: (0, 0))],
        out_shape=[jax.ShapeDtypeStruct((T, 3 * Dm), bf16), jax.ShapeDtypeStruct((8, Dm), f32)],
        scratch_shapes=[pltpu.VMEM((tm + HALO, Dm), f32), pltpu.VMEM((tm + 8, Dm), f32), pltpu.VMEM((8, Dm), f32)],
        compiler_params=_params(("arbitrary",)),
    )(bcu, bcu, conv_w, dq)


def loss_head(x, fw, target, tm=512):
    T = x.shape[0]

    def body(x_ref, fw_ref, t_ref, loss_ref, dx_ref, dfw_ref):
        @pl.when(pl.program_id(0) == 0)
        def _():
            loss_ref[...] = jnp.zeros_like(loss_ref)
            dfw_ref[...] = jnp.zeros_like(dfw_ref)

        w = fw_ref[...]
        y, xh, inv = _rms_fwd(x_ref[...], w)
        err = y - t_ref[...]
        loss_ref[...] += 0.5 * jnp.sum(jnp.mean(err * err, axis=-1, keepdims=True), axis=0, keepdims=True)
        dx, dw = _rms_bwd(err * (1.0 / D_MODEL), xh, inv, w)
        dx_ref[...] = dx
        dfw_ref[...] += dw

    return pl.pallas_call(
        body, name="loss_head", grid=(T // tm,),
        in_specs=[_rows(tm, D_MODEL), _resident((1, D_MODEL)), _rows(tm, D_MODEL)],
        out_specs=[pl.BlockSpec((1, LANES), lambda i: (0, 0)), _rows(tm, D_MODEL), pl.BlockSpec((1, D_MODEL), lambda i: (0, 0))],
        out_shape=[jax.ShapeDtypeStruct((1, LANES), f32), jax.ShapeDtypeStruct((T, D_MODEL), f32), jax.ShapeDtypeStruct((1, D_MODEL), f32)],
        compiler_params=_params(("arbitrary",)),
    )(x, fw, target)


def adamw(g_parts, w, m, v, name="adamw"):
    R = w.shape[0]
    tr = PACK_TILE
    n = len(g_parts)
    arrays, specs = [], []
    for part in g_parts:
        if isinstance(part, tuple):
            idx, arr = part
            lead = len(idx)
            specs.append(pl.BlockSpec((None,) * lead + (tr, PACK_W), lambda i, s, idx=idx: tuple(s[k] for k in idx) + (i, 0)))
        else:
            arr = part
            specs.append(pl.BlockSpec((tr, PACK_W), lambda i, s: (i, 0)))
        arrays.append(arr)

    def body(s_ref, *refs):
        g_refs = refs[:n]
        w_ref, m_ref, v_ref, go_ref, d_ref, mo_ref, vo_ref = refs[n:]
        g = g_refs[0][...].astype(f32)
        for r in g_refs[1:]:
            g = g + r[...].astype(f32)
        m_new = ADAM_B1 * m_ref[...] + (1.0 - ADAM_B1) * g
        v_new = ADAM_B2 * v_ref[...] + (1.0 - ADAM_B2) * (g * g)
        m_hat = m_new / (1.0 - ADAM_B1 ** ADAM_STEP)
        v_hat = v_new / (1.0 - ADAM_B2 ** ADAM_STEP)
        go_ref[...] = g
        d_ref[...] = -ADAM_LR * (m_hat / (jnp.sqrt(v_hat) + ADAM_EPS) + ADAM_WD * w_ref[...])
        mo_ref[...] = m_new
        vo_ref[...] = v_new

    plain = pl.BlockSpec((tr, PACK_W), lambda i, s: (i, 0))
    return lambda sel: pl.pallas_call(
        body, name=name,
        grid_spec=pltpu.PrefetchScalarGridSpec(num_scalar_prefetch=1, grid=(R // tr,), in_specs=specs + [plain] * 3, out_specs=[plain] * 4),
        out_shape=[jax.ShapeDtypeStruct((R, PACK_W), f32)] * 4,
        compiler_params=_params(("parallel",)),
    )(sel, *arrays, w, m, v)


def pair_sum_bf16(sel, ga, gb):
    R = gb.shape[1]
    tr = PACK_TILE

    def body(s_ref, a_ref, b_ref, o_ref):
        o_ref[...] = (a_ref[...] + b_ref[...]).astype(bf16)

    return pl.pallas_call(
        body, name="pair_sum_bf16",
        grid_spec=pltpu.PrefetchScalarGridSpec(
            num_scalar_prefetch=1, grid=(4, R // tr),
            in_specs=[pl.BlockSpec((None, None, tr, PACK_W), lambda j, i, s: (s[0], j, i, 0)), pl.BlockSpec((None, tr, PACK_W), lambda j, i, s: (j, i, 0))],
            out_specs=pl.BlockSpec((None, tr, PACK_W), lambda j, i, s: (j, i, 0))),
        out_shape=jax.ShapeDtypeStruct((4, R, PACK_W), bf16),
        compiler_params=_params(("parallel", "parallel")),
    )(sel, ga, gb)


def sum_over_devices(gathered):
    _, R, W = gathered.shape

    def body(g_ref, o_ref):
        acc = g_ref[0]
        for k in range(1, N_DEV):
            acc = acc + g_ref[k]
        o_ref[...] = acc

    return pl.pallas_call(
        body, name="sum_over_devices", grid=(1,),
        in_specs=[pl.BlockSpec((N_DEV, R, W), lambda i: (0, 0, 0))], out_specs=pl.BlockSpec((R, W), lambda i: (0, 0)),
        out_shape=jax.ShapeDtypeStruct((R, W), f32), compiler_params=_params(("arbitrary",)),
    )(gathered)


_ANY = pl.BlockSpec(memory_space=pl.ANY)


def all_gather(block, name):
    R, W = block.shape

    def body(x_ref, out_ref, send_sems, recv_sems, local_sem):
        x, y, c = lax.axis_index("x"), lax.axis_index("y"), lax.axis_index("c")
        me, sibling = (x, y, c), (x, y, 1 - c)
        chips = [(1 - x, y), (x, 1 - y), (1 - x, 1 - y)]

        def slot(px, py, pc):
            return out_ref.at[4 * px + 2 * py + pc]

        def copy(k, blk, to, src=None):
            return pltpu.make_async_remote_copy(
                src_ref=slot(*blk) if src is None else src, dst_ref=slot(*blk),
                send_sem=send_sems.at[k], recv_sem=recv_sems.at[k], device_id=to, device_id_type=MESH)

        mine = pltpu.make_async_copy(x_ref, slot(*me), local_sem)
        mine.start()
        first = [copy(0, me, sibling, src=x_ref)] + [copy(1 + j, me, (*chip, c), src=x_ref) for j, chip in enumerate(chips)]
        for cp in first:
            cp.start()
        passed = [copy(4 + j, (*chip, c), sibling) for j, chip in enumerate(chips)]
        for j, chip in enumerate(chips):
            copy(1 + j, (*chip, c), me).wait_recv()
            passed[j].start()
        copy(0, sibling, me).wait_recv()
        for j, chip in enumerate(chips):
            copy(4 + j, (*chip, 1 - c), me).wait_recv()
        for cp in first + passed:
            cp.wait_send()
        mine.wait()

    return pl.pallas_call(
        body, name=name, in_specs=[_ANY], out_specs=_ANY, out_shape=jax.ShapeDtypeStruct((N_DEV, R, W), block.dtype),
        scratch_shapes=[pltpu.SemaphoreType.DMA((7,)), pltpu.SemaphoreType.DMA((7,)), pltpu.SemaphoreType.DMA],
    )(block)


def exchange_with_sibling(g):
    _, _, R, W = g.shape

    def body(g_ref, recv_ref, send_sem, recv_sem):
        x, y, c = lax.axis_index("x"), lax.axis_index("y"), lax.axis_index("c")
        cp = pltpu.make_async_remote_copy(src_ref=g_ref.at[1 - c], dst_ref=recv_ref, send_sem=send_sem, recv_sem=recv_sem,
                                          device_id=(x, y, 1 - c), device_id_type=MESH)
        cp.start()
        cp.wait()

    return pl.pallas_call(
        body, name="exchange_with_sibling", in_specs=[_ANY], out_specs=_ANY, out_shape=jax.ShapeDtypeStruct((4, R, W), g.dtype),
        scratch_shapes=[pltpu.SemaphoreType.DMA, pltpu.SemaphoreType.DMA],
    )(g)


def exchange_between_chips(part):
    _, R, W = part.shape

    def body(p_ref, recv_ref, send_sems, recv_sems):
        x, y, c = lax.axis_index("x"), lax.axis_index("y"), lax.axis_index("c")
        chips = [(1 - x, y), (x, 1 - y), (1 - x, 1 - y)]
        cps = [pltpu.make_async_remote_copy(src_ref=p_ref.at[2 * px + py], dst_ref=recv_ref.at[k], send_sem=send_sems.at[k],
                                            recv_sem=recv_sems.at[k], device_id=(px, py, c), device_id_type=MESH)
               for k, (px, py) in enumerate(chips)]
        for cp in cps:
            cp.start()
        for cp in cps:
            cp.wait()

    return pl.pallas_call(
        body, name="exchange_between_chips", in_specs=[_ANY], out_specs=_ANY, out_shape=jax.ShapeDtypeStruct((3, R, W), part.dtype),
        scratch_shapes=[pltpu.SemaphoreType.DMA((3,)), pltpu.SemaphoreType.DMA((3,))],
    )(part)


PARAMS = {
    "norm_w": ((DEPTH, 3, D_MODEL), 2),
    "ffn_w_gate": ((DEPTH, 2, D_MODEL, D_FF), 3),
    "ffn_w_up": ((DEPTH, 2, D_MODEL, D_FF), 3),
    "ffn_w_down": ((DEPTH, 2, D_FF, D_MODEL), 2),
    "ssd_w_in": ((2, D_MODEL, SSD_IN_DIM), 2),
    "ssd_conv_w": ((2, SSD_CONV_K, SSD_CONV_DIM), 2),
    "ssd_conv_b": ((2, SSD_CONV_DIM), None),
    "ssd_dt_bias": ((2, SSD_HEADS), None),
    "ssd_a_log": ((2, SSD_HEADS), None),
    "ssd_d": ((2, SSD_HEADS), None),
    "ssd_norm_w": ((2, SSD_INNER), None),
    "ssd_w_out": ((2, SSD_INNER, D_MODEL), 1),
    "sc_w_in": ((2, D_MODEL, 3 * D_MODEL), 2),
    "sc_conv_w": ((2, SC_CONV_K, D_MODEL), 2),
    "sc_w_out": ((2, D_MODEL, D_MODEL), 1),
    "final_norm_w": ((D_MODEL,), None),
}
NAMES = list(PARAMS)
BIG = ["ffn_w_gate", "ffn_w_up", "ffn_w_down", "ssd_w_in", "ssd_w_out", "sc_w_in", "sc_w_out"]
SMALL = [n for n in NAMES if n not in BIG]
SMALL_SHARDED = [n for n in SMALL if PARAMS[n][1] is not None]


def _round_up(n, m):
    return -(-n // m) * m


def _pack(flat_list, rows_multiple):
    flat = jnp.concatenate(flat_list)
    rows = _round_up(_round_up(flat.shape[0], PACK_W) // PACK_W, rows_multiple)
    return jnp.pad(flat, (0, rows * PACK_W - flat.shape[0])).reshape(rows, PACK_W)


def _unpack(packed, shapes, lead=()):
    flat = packed.reshape(lead + (-1,))
    out, off = [], 0
    for shp in shapes:
        n = 1
        for s in shp:
            n *= s
        out.append(flat[..., off:off + n].reshape(lead + tuple(shp)))
        off += n
    return out


def _local_shape(name):
    shp, ax = PARAMS[name]
    if ax is None:
        return shp
    return shp[:ax] + (shp[ax] // N_DEV,) + shp[ax + 1:]


def _full_from_gathered(g, name):
    shp, ax = PARAMS[name]
    return jnp.moveaxis(g, 0, ax).reshape(shp)


def _by_destination(full, name):
    shp, ax = PARAMS[name]
    loc = shp[ax] // N_DEV
    return jnp.moveaxis(full.reshape(shp[:ax] + (N_DEV, loc) + shp[ax + 1:]), ax, 0)


def _ssd_layer_fwd(xin, nw, p, j):
    z, xbc, dt_raw = in_proj_fwd(xin, nw, [p["ssd_wz"][j], p["ssd_wx"][j], p["ssd_wdt"][j]], [bf16, bf16, f32])
    act, dt = ssd_conv_fwd(xbc, p["ssd_conv_w"][j], p["ssd_conv_b"][j], dt_raw, p["ssd_dt_bias"][j])
    T = xin.shape[0]
    dt4 = jnp.pad(dt[:, :SSD_HEADS].reshape(T, SSD_GROUPS, 8).transpose(1, 0, 2), ((0, 0), (0, 0), (0, LANES - 8)))
    y, states = ssd_scan_fwd(act, dt4, p["ssd_alog4"][j])
    gn = ssd_gate_fwd(y, act, z, p["ssd_dx"][j], p["ssd_norm_w"][j])
    xout = out_proj_fwd(xin, gn, p["ssd_w_out"][j])
    return xout, (xin, z, xbc, dt_raw, act, dt4, y, states, gn)


def _ssd_layer_bwd(dxo, nw, p, j, saved):
    xin, z, xbc, dt_raw, act, dt4, y, states, gn = saved
    T = xin.shape[0]
    dgn, dyb = out_proj_bwd(dxo, p["ssd_w_out"][j])
    g = {"ssd_w_out": tn_matmul(gn, dyb)}
    dy, dxs_skip, dz, dd_x, dgnw = ssd_gate_bwd(y, act, z, p["ssd_dx"][j], p["ssd_norm_w"][j], dgn)
    g["ssd_norm_w"] = dgnw[0]
    g["ssd_d"] = jnp.sum(dd_x.reshape(SSD_HEADS, SSD_HEAD_DIM), axis=1)
    dxs, db, dc, ddt4, dalog4 = ssd_scan_bwd(act, dt4, p["ssd_alog4"][j], states, dy)
    g["ssd_a_log"] = dalog4[:, 0, :8].reshape(SSD_HEADS)
    ddt = jnp.pad(ddt4[:, :, :8].transpose(1, 0, 2).reshape(T, SSD_HEADS), ((0, 0), (0, LANES - SSD_HEADS)))
    dxbc, ddt_raw, dcw, dcb, ddtb = ssd_conv_bwd(xbc, p["ssd_conv_w"][j], p["ssd_conv_b"][j], dt_raw, p["ssd_dt_bias"][j], dxs, dxs_skip, db, dc, ddt)
    g["ssd_conv_w"] = dcw[:SSD_CONV_K]
    g["ssd_conv_b"] = dcb[0]
    g["ssd_dt_bias"] = ddtb[0, :SSD_HEADS]
    dx, h, dnw = in_proj_bwd(xin, nw, dxo, [dz, dxbc, ddt_raw], [p["ssd_wz"][j], p["ssd_wx"][j], p["ssd_wdt"][j]])
    g["ssd_w_in"] = jnp.concatenate([tn_matmul(h, dz), tn_matmul(h, dxbc), tn_matmul(h, ddt_raw)[:, :SSD_HEADS]], axis=1)
    return dx, dnw, g


def _sc_layer_fwd(xin, nw, p, j):
    (bcu,) = in_proj_fwd(xin, nw, [p["sc_w_in"][j]], [bf16])
    q = sc_mid_fwd(bcu, p["sc_conv_w"][j])
    return out_proj_fwd(xin, q, p["sc_w_out"][j]), (xin, bcu, q)


def _sc_layer_bwd(dxo, nw, p, j, saved):
    xin, bcu, q = saved
    dq, dyb = out_proj_bwd(dxo, p["sc_w_out"][j])
    g = {"sc_w_out": tn_matmul(q, dyb)}
    dbcu, dcw = sc_mid_bwd(bcu, p["sc_conv_w"][j], dq)
    g["sc_conv_w"] = dcw[:SC_CONV_K]
    dx, h, dnw = in_proj_bwd(xin, nw, dxo, [dbcu], [p["sc_w_in"][j]])
    g["sc_w_in"] = tn_matmul(h, dbcu)
    return dx, dnw, g


def kernel(x, norm_w, ffn_w_gate, ffn_w_up, ffn_w_down, ssd_w_in, ssd_conv_w, ssd_conv_b, ssd_dt_bias, ssd_a_log, ssd_d, ssd_norm_w, ssd_w_out, sc_w_in, sc_conv_w, sc_w_out, final_norm_w, loss_target, m_norm_w, m_ffn_w_gate, m_ffn_w_up, m_ffn_w_down, m_ssd_w_in, m_ssd_conv_w, m_ssd_conv_b, m_ssd_dt_bias, m_ssd_a_log, m_ssd_d, m_ssd_norm_w, m_ssd_w_out, m_sc_w_in, m_sc_conv_w, m_sc_w_out, m_final_norm_w, v_norm_w, v_ffn_w_gate, v_ffn_w_up, v_ffn_w_down, v_ssd_w_in, v_ssd_conv_w, v_ssd_conv_b, v_ssd_dt_bias, v_ssd_a_log, v_ssd_d, v_ssd_norm_w, v_ssd_w_out, v_sc_w_in, v_sc_conv_w, v_sc_w_out, v_final_norm_w):
    w_loc = dict(zip(NAMES, (norm_w, ffn_w_gate, ffn_w_up, ffn_w_down, ssd_w_in, ssd_conv_w, ssd_conv_b, ssd_dt_bias, ssd_a_log, ssd_d, ssd_norm_w, ssd_w_out, sc_w_in, sc_conv_w, sc_w_out, final_norm_w)))
    m_loc = dict(zip(NAMES, (m_norm_w, m_ffn_w_gate, m_ffn_w_up, m_ffn_w_down, m_ssd_w_in, m_ssd_conv_w, m_ssd_conv_b, m_ssd_dt_bias, m_ssd_a_log, m_ssd_d, m_ssd_norm_w, m_ssd_w_out, m_sc_w_in, m_sc_conv_w, m_sc_w_out, m_final_norm_w)))
    v_loc = dict(zip(NAMES, (v_norm_w, v_ffn_w_gate, v_ffn_w_up, v_ffn_w_down, v_ssd_w_in, v_ssd_conv_w, v_ssd_conv_b, v_ssd_dt_bias, v_ssd_a_log, v_ssd_d, v_ssd_norm_w, v_ssd_w_out, v_sc_w_in, v_sc_conv_w, v_sc_w_out, v_final_norm_w)))
    ax, ay, ac = lax.axis_index("x"), lax.axis_index("y"), lax.axis_index("c")
    my_chip = 2 * ax + ay
    my_dev = 4 * ax + 2 * ay + ac
    T = x.shape[1]

    big_shapes = [_local_shape(n) for n in BIG]
    w_big = _pack([w_loc[n].reshape(-1) for n in BIG], PACK_TILE)
    big_all = all_gather(w_big.astype(bf16), "all_gather_weights")
    full = {}
    for n, part in zip(BIG, _unpack(big_all, big_shapes, lead=(N_DEV,))):
        full[n] = _full_from_gathered(part, n)
    ss_shapes = [_local_shape(n) for n in SMALL_SHARDED]
    ss_all = all_gather(_pack([w_loc[n].reshape(-1) for n in SMALL_SHARDED], 8), "all_gather_small")
    for n, part in zip(SMALL_SHARDED, _unpack(ss_all, ss_shapes, lead=(N_DEV,))):
        full[n] = _full_from_gathered(part, n)
    for n in SMALL:
        if PARAMS[n][1] is None:
            full[n] = w_loc[n]

    p = dict(full)
    p["ssd_wz"] = full["ssd_w_in"][:, :, :SSD_INNER]
    p["ssd_wx"] = full["ssd_w_in"][:, :, SSD_INNER:SSD_INNER + SSD_CONV_DIM]
    p["ssd_wdt"] = jnp.pad(full["ssd_w_in"][:, :, SSD_INNER + SSD_CONV_DIM:], ((0, 0), (0, 0), (0, LANES - SSD_HEADS)))
    p["ssd_conv_b"] = full["ssd_conv_b"].reshape(2, 1, SSD_CONV_DIM)
    p["ssd_dt_bias"] = jnp.pad(full["ssd_dt_bias"], ((0, 0), (0, LANES - SSD_HEADS))).reshape(2, 1, LANES)
    p["ssd_alog4"] = jnp.pad(full["ssd_a_log"].reshape(2, SSD_GROUPS, 1, 8), ((0, 0), (0, 0), (0, 0), (0, LANES - 8)))
    p["ssd_dx"] = jnp.repeat(full["ssd_d"], SSD_HEAD_DIM, axis=1).reshape(2, 1, SSD_INNER)
    p["ssd_norm_w"] = full["ssd_norm_w"].reshape(2, 1, SSD_INNER)
    nw_all = full["norm_w"].reshape(DEPTH, 3, 1, D_MODEL)

    xc = x[0]
    saved = []
    for i in range(DEPTH):
        j = i // 2
        x1, g1, u1, a1 = ffn_fwd(xc, nw_all[i, 0], full["ffn_w_gate"][i, 0], full["ffn_w_up"][i, 0], full["ffn_w_down"][i, 0])
        if i % 2 == 0:
            x2, mix_saved = _ssd_layer_fwd(x1, nw_all[i, 1], p, j)
        else:
            x2, mix_saved = _sc_layer_fwd(x1, nw_all[i, 1], p, j)
        x3, g3, u3, a3 = ffn_fwd(x2, nw_all[i, 2], full["ffn_w_gate"][i, 1], full["ffn_w_up"][i, 1], full["ffn_w_down"][i, 1])
        saved.append(((xc, g1, u1, a1), mix_saved, (x2, g3, u3, a3)))
        xc = x3

    loss_row, dx, dfw = loss_head(xc, full["final_norm_w"].reshape(1, D_MODEL), loss_target[0])
    loss = lax.psum(loss_row[0, 0], ("x", "y", "c"))

    grads = {n: [None] * PARAMS[n][0][0] for n in NAMES if n != "final_norm_w"}
    grads["final_norm_w"] = dfw[0]
    dnorm = [[None] * 3 for _ in range(DEPTH)]
    dgate = [[None] * 2 for _ in range(DEPTH)]
    dup = [[None] * 2 for _ in range(DEPTH)]
    ddown = [[None] * 2 for _ in range(DEPTH)]

    def ffn_back(i, k, dxo, sv):
        xin, g_, u_, a_ = sv
        dxi, h, dyb, dg, du, dnw = ffn_bwd_dx(xin, dxo, g_, u_, nw_all[i, 2 * k], full["ffn_w_gate"][i, k], full["ffn_w_up"][i, k], full["ffn_w_down"][i, k])
        dnorm[i][2 * k] = dnw[0]
        dgate[i][k] = tn_matmul(h, dg)
        dup[i][k] = tn_matmul(h, du)
        ddown[i][k] = tn_matmul(a_, dyb)
        return dxi

    for i in reversed(range(DEPTH)):
        j = i // 2
        sv_a, sv_mix, sv_b = saved[i]
        dx = ffn_back(i, 1, dx, sv_b)
        if i % 2 == 0:
            dx, dnw, gm = _ssd_layer_bwd(dx, nw_all[i, 1], p, j, sv_mix)
        else:
            dx, dnw, gm = _sc_layer_bwd(dx, nw_all[i, 1], p, j, sv_mix)
        dnorm[i][1] = dnw[0]
        for n, val in gm.items():
            grads[n][j] = val
        dx = ffn_back(i, 0, dx, sv_a)

    grads["norm_w"] = jnp.stack([jnp.stack(r) for r in dnorm])
    grads["ffn_w_gate"] = jnp.stack([jnp.stack(r) for r in dgate])
    grads["ffn_w_up"] = jnp.stack([jnp.stack(r) for r in dup])
    grads["ffn_w_down"] = jnp.stack([jnp.stack(r) for r in ddown])
    for n in NAMES:
        if isinstance(grads[n], list):
            grads[n] = jnp.stack(grads[n])

    rb = w_big.shape[0]
    g_dest = jnp.concatenate([_by_destination(grads[n], n).reshape(N_DEV, -1) for n in BIG], axis=1)
    g_dest = jnp.pad(g_dest, ((0, 0), (0, rb * PACK_W - g_dest.shape[1]))).reshape(2, 2, 2, rb, PACK_W)
    g_dest = g_dest.transpose(2, 0, 1, 3, 4).reshape(2, 4, rb, PACK_W)
    from_sibling = exchange_with_sibling(g_dest)
    sel = jnp.stack([ac, my_chip]).astype(jnp.int32)
    chip_part = pair_sum_bf16(sel, g_dest, from_sibling)
    from_chips = exchange_between_chips(chip_part)
    m_big = _pack([m_loc[n].reshape(-1) for n in BIG], PACK_TILE)
    v_big = _pack([v_loc[n].reshape(-1) for n in BIG], PACK_TILE)
    parts = [((0, 1), g_dest), ((1,), from_sibling)] + [from_chips[k] for k in range(3)]
    big_out = adamw(parts, w_big, m_big, v_big, name="adamw_big")(sel)

    g_small = _pack([grads[n].reshape(-1) for n in SMALL], 8)
    g_small = sum_over_devices(all_gather(g_small, "all_gather_small_grads"))
    g_small_full = dict(zip(SMALL, _unpack(g_small, [PARAMS[n][0] for n in SMALL])))
    g_small_loc = []
    for n in SMALL:
        if PARAMS[n][1] is None:
            g_small_loc.append(g_small_full[n])
        else:
            g_small_loc.append(lax.dynamic_index_in_dim(_by_destination(g_small_full[n], n), my_dev, axis=0, keepdims=False))
    small_shapes = [_local_shape(n) for n in SMALL]
    pack_small = lambda d: _pack([d[n].reshape(-1) for n in SMALL], PACK_TILE)
    small_out = adamw([_pack([gl.reshape(-1) for gl in g_small_loc], PACK_TILE)], pack_small(w_loc), pack_small(m_loc), pack_small(v_loc), name="adamw_small")(sel)

    results = []
    for k in range(4):
        d = dict(zip(BIG, _unpack(big_out[k], big_shapes)))
        d.update(zip(SMALL, _unpack(small_out[k], small_shapes)))
        results.append(d)
    return (loss, dx[None], *[results[0][n] for n in NAMES], *[results[1][n] for n in NAMES],
            *[results[2][n] for n in NAMES], *[results[3][n] for n in NAMES])
```

```python
import functools

import jax
import jax.numpy as jnp
from jax import lax
from jax.experimental import pallas as pl
from jax.experimental.pallas import tpu as pltpu

f32 = jnp.float32
bf16 = jnp.bfloat16

D_MODEL = 1024
D_FF = 2816
DEPTH = 4
SSD_INNER = 2048
SSD_HEADS = 32
SSD_HEAD_DIM = 64
SSD_GROUPS = 4
SSD_STATE = 128
SSD_CONV_K = 4
SSD_CONV_DIM = 3072
SSD_IN_DIM = 5152
SSD_CHUNK = 128
SC_CONV_K = 3
RMS_EPS = 1e-5
N_DEV = 8
LANES = 128
HALO = 16
PACK_W = 1024
PACK_TILE = 256
VMEM_LIMIT = 56 * 1024 * 1024
NEG_BIG = -1e30

ADAM_LR = 0.001
ADAM_B1 = 0.9
ADAM_B2 = 0.999
ADAM_EPS = 1e-08
ADAM_WD = 0.01
ADAM_STEP = 10

NT_DIMS = (((1,), (1,)), ((), ()))
TN_DIMS = (((0,), (0,)), ((), ()))
MESH = pl.DeviceIdType.MESH


def _params(sem=None):
    return pltpu.CompilerParams(dimension_semantics=sem, vmem_limit_bytes=VMEM_LIMIT)


def _resident(shape):
    nd = len(shape)
    return pl.BlockSpec(tuple(shape), lambda *_: (0,) * nd, pipeline_mode=pl.Buffered(1))


def _rows(tm, width):
    return pl.BlockSpec((tm, width), lambda i: (i, 0))


def _sigmoid(v):
    return 1.0 / (1.0 + jnp.exp(-v))


def _softplus(v):
    return jnp.maximum(v, 0.0) + jnp.log(1.0 + jnp.exp(-jnp.abs(v)))


def _rms_fwd(xv, w):
    inv = lax.rsqrt(jnp.mean(xv * xv, axis=-1, keepdims=True) + RMS_EPS)
    xh = xv * inv
    return xh * w, xh, inv


def _rms_bwd(dh, xh, inv, w):
    dxh = dh * w
    dx = inv * (dxh - xh * jnp.mean(dxh * xh, axis=-1, keepdims=True))
    return dx, jnp.sum(dh * xh, axis=0, keepdims=True)


def _mm(a, b):
    return jnp.dot(a, b, preferred_element_type=f32)


def _mm_nt(a, b):
    return lax.dot_general(a, b, NT_DIMS, preferred_element_type=f32)


def _mm_tn(a, b):
    return lax.dot_general(a, b, TN_DIMS, preferred_element_type=f32)


def ffn_fwd(x, nw, wg, wu, wd, tm=256):
    T = x.shape[0]

    def body(x_ref, nw_ref, wg_ref, wu_ref, wd_ref, xo_ref, g_ref, u_ref, a_ref):
        xv = x_ref[...]
        h, _, _ = _rms_fwd(xv, nw_ref[...])
        hb = h.astype(bf16)
        g = _mm(hb, wg_ref[...])
        u = _mm(hb, wu_ref[...])
        ab = (g * _sigmoid(g) * u).astype(bf16)
        xo_ref[...] = xv + 0.5 * _mm(ab, wd_ref[...])
        g_ref[...] = g.astype(bf16)
        u_ref[...] = u.astype(bf16)
        a_ref[...] = ab

    return pl.pallas_call(
        body, name="ffn_fwd", grid=(T // tm,),
        in_specs=[_rows(tm, D_MODEL), _resident((1, D_MODEL)), _resident(wg.shape), _resident(wu.shape), _resident(wd.shape)],
        out_specs=[_rows(tm, D_MODEL), _rows(tm, D_FF), _rows(tm, D_FF), _rows(tm, D_FF)],
        out_shape=[jax.ShapeDtypeStruct((T, D_MODEL), f32)] + [jax.ShapeDtypeStruct((T, D_FF), bf16)] * 3,
        compiler_params=_params(("parallel",)),
    )(x, nw, wg, wu, wd)


def ffn_bwd_dx(x, dxo, g, u, nw, wg, wu, wd, tm=256):
    T = x.shape[0]

    def body(x_ref, dxo_ref, g_ref, u_ref, nw_ref, wg_ref, wu_ref, wd_ref, dx_ref, h_ref, dy_ref, dg_ref, du_ref, dnw_ref):
        w = nw_ref[...]
        h, xh, inv = _rms_fwd(x_ref[...], w)
        dxo_v = dxo_ref[...]
        dyb = (0.5 * dxo_v).astype(bf16)
        da = _mm_nt(dyb, wd_ref[...])
        gv = g_ref[...].astype(f32)
        uv = u_ref[...].astype(f32)
        s = _sigmoid(gv)
        dgb = (da * uv * (s * (1.0 + gv * (1.0 - s)))).astype(bf16)
        dub = (da * (gv * s)).astype(bf16)
        dh = _mm_nt(dgb, wg_ref[...]) + _mm_nt(dub, wu_ref[...])
        dxn, dw = _rms_bwd(dh, xh, inv, w)
        dx_ref[...] = dxo_v + dxn
        h_ref[...] = h.astype(bf16)
        dy_ref[...] = dyb
        dg_ref[...] = dgb
        du_ref[...] = dub

        @pl.when(pl.program_id(0) == 0)
        def _():
            dnw_ref[...] = jnp.zeros_like(dnw_ref)

        dnw_ref[...] += dw

    return pl.pallas_call(
        body, name="ffn_bwd_dx", grid=(T // tm,),
        in_specs=[_rows(tm, D_MODEL), _rows(tm, D_MODEL), _rows(tm, D_FF), _rows(tm, D_FF), _resident((1, D_MODEL)),
                  _resident(wg.shape), _resident(wu.shape), _resident(wd.shape)],
        out_specs=[_rows(tm, D_MODEL), _rows(tm, D_MODEL), _rows(tm, D_MODEL), _rows(tm, D_FF), _rows(tm, D_FF),
                   pl.BlockSpec((1, D_MODEL), lambda i: (0, 0))],
        out_shape=[jax.ShapeDtypeStruct((T, D_MODEL), f32), jax.ShapeDtypeStruct((T, D_MODEL), bf16), jax.ShapeDtypeStruct((T, D_MODEL), bf16),
                   jax.ShapeDtypeStruct((T, D_FF), bf16), jax.ShapeDtypeStruct((T, D_FF), bf16), jax.ShapeDtypeStruct((1, D_MODEL), f32)],
        compiler_params=_params(("arbitrary",)),
    )(x, dxo, g, u, nw, wg, wu, wd)


def tn_matmul(a, b, tk=512):
    T, M = a.shape
    N = b.shape[1]
    bn = N if M * N <= 3_200_000 else N // 2
    nk = T // tk

    def body(a_ref, b_ref, o_ref):
        @pl.when(pl.program_id(1) == 0)
        def _():
            o_ref[...] = jnp.zeros_like(o_ref)

        o_ref[...] += _mm_tn(a_ref[...], b_ref[...])

    return pl.pallas_call(
        body, name=f"tn_matmul_{M}x{N}", grid=(N // bn, nk),
        in_specs=[pl.BlockSpec((tk, M), lambda j, k: (k, 0)), pl.BlockSpec((tk, bn), lambda j, k: (k, j))],
        out_specs=pl.BlockSpec((M, bn), lambda j, k: (0, j)),
        out_shape=jax.ShapeDtypeStruct((M, N), f32),
        compiler_params=_params(("parallel", "arbitrary")),
    )(a, b)


def tn_matmul_to_shards(a, b, buf, idx, axis, tk=512):
    T, M = a.shape
    N = b.shape[1]
    m, n = buf.shape[-2:]
    nl = len(idx)
    halves = 2 if (axis == 1 and N >= 2048) else 1
    bn = N // halves
    nk = T // tk
    per = N_DEV // halves

    def body(a_ref, b_ref, buf_ref, o_ref, acc_ref):
        k = pl.program_id(1)

        @pl.when(k == 0)
        def _():
            acc_ref[...] = jnp.zeros_like(acc_ref)

        acc_ref[...] += _mm_tn(a_ref[...], b_ref[...])

        @pl.when(k == nk - 1)
        def _():
            for jj in range(per):
                if axis == 1:
                    o_ref[jj % 2, jj // 2] = acc_ref[:, pl.ds(jj * n, n)]
                else:
                    o_ref[jj % 2, jj // 2] = acc_ref[pl.ds(jj * m, m), :]

    none = (None,) * nl
    return pl.pallas_call(
        body, name=f"tn_matmul_to_shards_{M}x{N}_{axis}", grid=(halves, nk),
        in_specs=[pl.BlockSpec((tk, M), lambda j, k: (k, 0)), pl.BlockSpec((tk, bn), lambda j, k: (k, j)), _ANY],
        out_specs=pl.BlockSpec((2, 4 // halves) + none + (m, n), lambda j, k: (0, j) + tuple(idx) + (0, 0)),
        out_shape=jax.ShapeDtypeStruct(buf.shape, f32),
        scratch_shapes=[pltpu.VMEM((M, bn), f32)],
        input_output_aliases={2: 0},
        compiler_params=_params(("parallel", "arbitrary")),
    )(a, b, buf)


def in_proj_fwd(x, nw, ws, out_dtypes, tm=256):
    T = x.shape[0]
    n = len(ws)

    def body(*refs):
        x_ref, nw_ref = refs[:2]
        w_refs = refs[2:2 + n]
        o_refs = refs[2 + n:]
        h, _, _ = _rms_fwd(x_ref[...], nw_ref[...])
        hb = h.astype(bf16)
        for w_ref, o_ref in zip(w_refs, o_refs):
            o_ref[...] = _mm(hb, w_ref[...]).astype(o_ref.dtype)

    return pl.pallas_call(
        body, name="in_proj_fwd_" + "_".join(str(w.shape[1]) for w in ws), grid=(T // tm,),
        in_specs=[_rows(tm, D_MODEL), _resident((1, D_MODEL))] + [_resident(w.shape) for w in ws],
        out_specs=[_rows(tm, w.shape[1]) for w in ws],
        out_shape=[jax.ShapeDtypeStruct((T, w.shape[1]), dt) for w, dt in zip(ws, out_dtypes)],
        compiler_params=_params(("parallel",)),
    )(x, nw, *ws)


def in_proj_bwd(x, nw, dxo, dys, ws, tm=256):
    T = x.shape[0]
    n = len(ws)

    def body(*refs):
        x_ref, nw_ref, dxo_ref = refs[:3]
        dy_refs = refs[3:3 + n]
        w_refs = refs[3 + n:3 + 2 * n]
        dx_ref, h_ref, dnw_ref = refs[3 + 2 * n:]
        w = nw_ref[...]
        h, xh, inv = _rms_fwd(x_ref[...], w)
        dh = _mm_nt(dy_refs[0][...], w_refs[0][...])
        for dy_ref, w_ref in zip(dy_refs[1:], w_refs[1:]):
            dh = dh + _mm_nt(dy_ref[...], w_ref[...])
        dxn, dw = _rms_bwd(dh, xh, inv, w)
        dx_ref[...] = dxo_ref[...] + dxn
        h_ref[...] = h.astype(bf16)

        @pl.when(pl.program_id(0) == 0)
        def _():
            dnw_ref[...] = jnp.zeros_like(dnw_ref)

        dnw_ref[...] += dw

    return pl.pallas_call(
        body, name="in_proj_bwd_" + "_".join(str(w.shape[1]) for w in ws), grid=(T // tm,),
        in_specs=[_rows(tm, D_MODEL), _resident((1, D_MODEL)), _rows(tm, D_MODEL)] + [_rows(tm, w.shape[1]) for w in ws]
        + [_resident(w.shape) for w in ws],
        out_specs=[_rows(tm, D_MODEL), _rows(tm, D_MODEL), pl.BlockSpec((1, D_MODEL), lambda i: (0, 0))],
        out_shape=[jax.ShapeDtypeStruct((T, D_MODEL), f32), jax.ShapeDtypeStruct((T, D_MODEL), bf16), jax.ShapeDtypeStruct((1, D_MODEL), f32)],
        compiler_params=_params(("arbitrary",)),
    )(x, nw, dxo, *dys, *ws)


def out_proj_fwd(x, a, w, tm=512):
    T = x.shape[0]
    K = a.shape[1]

    def body(x_ref, a_ref, w_ref, o_ref):
        o_ref[...] = x_ref[...] + _mm(a_ref[...], w_ref[...])

    return pl.pallas_call(
        body, name=f"out_proj_fwd_{K}", grid=(T // tm,),
        in_specs=[_rows(tm, D_MODEL), _rows(tm, K), _resident(w.shape)],
        out_specs=_rows(tm, D_MODEL), out_shape=jax.ShapeDtypeStruct((T, D_MODEL), f32),
        compiler_params=_params(("parallel",)),
    )(x, a, w)


def out_proj_bwd(dxo, w, tm=512):
    T = dxo.shape[0]
    K = w.shape[0]

    def body(dxo_ref, w_ref, da_ref, dy_ref):
        dyb = dxo_ref[...].astype(bf16)
        dy_ref[...] = dyb
        da_ref[...] = _mm_nt(dyb, w_ref[...]).astype(bf16)

    return pl.pallas_call(
        body, name=f"out_proj_bwd_{K}", grid=(T // tm,),
        in_specs=[_rows(tm, D_MODEL), _resident(w.shape)],
        out_specs=[_rows(tm, K), _rows(tm, D_MODEL)],
        out_shape=[jax.ShapeDtypeStruct((T, K), bf16), jax.ShapeDtypeStruct((T, D_MODEL), bf16)],
        compiler_params=_params(("parallel",)),
    )(dxo, w)


def _halo_spec(tm, width, n_tiles, reverse):
    per = tm // HALO

    def idx(i):
        t = (n_tiles - 1 - i) if reverse else i
        return (jnp.maximum(t * per - 1, 0), 0)

    return pl.BlockSpec((HALO, width), idx)


def _tile_spec(tm, width, n_tiles, reverse):
    if reverse:
        return pl.BlockSpec((tm, width), lambda i: (n_tiles - 1 - i, 0))
    return _rows(tm, width)


def _fill_window(ext_ref, tile, halo, first):
    ext_ref[pl.ds(0, HALO), :] = jnp.where(first, 0.0, halo)
    ext_ref[pl.ds(HALO, tile.shape[0]), :] = tile


def _causal_conv(ext_ref, w_ref, k_w, tm):
    base = HALO - (k_w - 1)
    out = w_ref[pl.ds(0, 1), :] * ext_ref[pl.ds(base, tm), :]
    for k in range(1, k_w):
        out = out + w_ref[pl.ds(k, 1), :] * ext_ref[pl.ds(base + k, tm), :]
    return out


def ssd_conv_fwd(xbc, conv_w, conv_b, dt_raw, dt_bias, tm=256):
    T = xbc.shape[0]
    nt = T // tm

    def body(x_ref, halo_ref, w_ref, b_ref, dtr_ref, dtb_ref, act_ref, dt_ref, ext_ref):
        _fill_window(ext_ref, x_ref[...].astype(f32), halo_ref[...].astype(f32), pl.program_id(0) == 0)
        pre = _causal_conv(ext_ref, w_ref, SSD_CONV_K, tm) + b_ref[...]
        act_ref[...] = (pre * _sigmoid(pre)).astype(bf16)
        dt_ref[...] = _softplus(dtr_ref[...] + dtb_ref[...])

    return pl.pallas_call(
        body, name="ssd_conv_fwd", grid=(nt,),
        in_specs=[_rows(tm, SSD_CONV_DIM), _halo_spec(tm, SSD_CONV_DIM, nt, False), _resident(conv_w.shape), _resident(conv_b.shape),
                  _rows(tm, LANES), _resident(dt_bias.shape)],
        out_specs=[_rows(tm, SSD_CONV_DIM), _rows(tm, LANES)],
        out_shape=[jax.ShapeDtypeStruct((T, SSD_CONV_DIM), bf16), jax.ShapeDtypeStruct((T, LANES), f32)],
        scratch_shapes=[pltpu.VMEM((tm + HALO, SSD_CONV_DIM), f32)],
        compiler_params=_params(("parallel",)),
    )(xbc, xbc, conv_w, conv_b, dt_raw, dt_bias)


def ssd_conv_bwd(xbc, conv_w, conv_b, dt_raw, dt_bias, dxs_a, dxs_b, db, dc, ddt, tm=256):
    T = xbc.shape[0]
    nt = T // tm
    K = SSD_CONV_K

    def body(x_ref, halo_ref, w_ref, b_ref, dtr_ref, dtb_ref, da_ref, dbb_ref, db_ref, dc_ref, ddt_ref,
             dx_ref, ddtr_ref, dw_ref, dbias_ref, ddtb_ref, ext_ref, dpre_ref, carry_ref):
        i = pl.program_id(0)

        @pl.when(i == 0)
        def _():
            carry_ref[...] = jnp.zeros_like(carry_ref)
            dw_ref[...] = jnp.zeros_like(dw_ref)
            dbias_ref[...] = jnp.zeros_like(dbias_ref)
            ddtb_ref[...] = jnp.zeros_like(ddtb_ref)

        _fill_window(ext_ref, x_ref[...].astype(f32), halo_ref[...].astype(f32), i == nt - 1)
        pre = _causal_conv(ext_ref, w_ref, K, tm) + b_ref[...]
        s = _sigmoid(pre)
        dact = jnp.concatenate([da_ref[...].astype(f32) + dbb_ref[...].astype(f32), db_ref[...].astype(f32), dc_ref[...].astype(f32)], axis=1)
        dpre = dact * (s * (1.0 + pre * (1.0 - s)))
        dpre_ref[pl.ds(0, tm), :] = dpre
        dpre_ref[pl.ds(tm, 8), :] = carry_ref[...]
        dx = w_ref[pl.ds(0, 1), :] * dpre_ref[pl.ds(K - 1, tm), :]
        for k in range(1, K):
            dx = dx + w_ref[pl.ds(k, 1), :] * dpre_ref[pl.ds(K - 1 - k, tm), :]
        dx_ref[...] = dx.astype(bf16)
        base = HALO - (K - 1)
        for k in range(K):
            dw_ref[pl.ds(k, 1), :] += jnp.sum(dpre * ext_ref[pl.ds(base + k, tm), :], axis=0, keepdims=True)
        dbias_ref[...] += jnp.sum(dpre, axis=0, keepdims=True)
        carry_ref[...] = dpre[0:8, :]
        ddtr = ddt_ref[...] * _sigmoid(dtr_ref[...] + dtb_ref[...])
        ddtr_ref[...] = ddtr.astype(bf16)
        ddtb_ref[...] += jnp.sum(ddtr, axis=0, keepdims=True)

    rev = functools.partial(_tile_spec, tm, n_tiles=nt, reverse=True)
    const = lambda shape: pl.BlockSpec(shape, lambda i: (0, 0))
    return pl.pallas_call(
        body, name="ssd_conv_bwd", grid=(nt,),
        in_specs=[rev(width=SSD_CONV_DIM), _halo_spec(tm, SSD_CONV_DIM, nt, True), _resident(conv_w.shape), _resident(conv_b.shape),
                  rev(width=LANES), _resident(dt_bias.shape), rev(width=SSD_INNER), rev(width=SSD_INNER),
                  rev(width=SSD_GROUPS * SSD_STATE), rev(width=SSD_GROUPS * SSD_STATE), rev(width=LANES)],
        out_specs=[rev(width=SSD_CONV_DIM), rev(width=LANES), const((8, SSD_CONV_DIM)), const((1, SSD_CONV_DIM)), const((1, LANES))],
        out_shape=[jax.ShapeDtypeStruct((T, SSD_CONV_DIM), bf16), jax.ShapeDtypeStruct((T, LANES), bf16),
                   jax.ShapeDtypeStruct((8, SSD_CONV_DIM), f32), jax.ShapeDtypeStruct((1, SSD_CONV_DIM), f32), jax.ShapeDtypeStruct((1, LANES), f32)],
        scratch_shapes=[pltpu.VMEM((tm + HALO, SSD_CONV_DIM), f32), pltpu.VMEM((tm + 8, SSD_CONV_DIM), f32), pltpu.VMEM((8, SSD_CONV_DIM), f32)],
        compiler_params=_params(("arbitrary",)),
    )(xbc, xbc, conv_w, conv_b, dt_raw, dt_bias, dxs_a, dxs_b, db, dc, ddt)


def _ssd_chunk(xs, bm, cm, dt, alog, st):
    L = SSD_CHUNK
    row = lax.broadcasted_iota(jnp.int32, (L, L), 0)
    col = lax.broadcasted_iota(jnp.int32, (L, L), 1)
    causal = row >= col
    tril = jnp.where(causal, 1.0, 0.0).astype(f32)
    lane = lax.broadcasted_iota(jnp.int32, (1, LANES), 1)
    sub = lax.broadcasted_iota(jnp.int32, (LANES, 1), 0)
    lo = lane < SSD_HEAD_DIM
    last_row = sub == L - 1

    dta = dt * (-jnp.exp(alog))
    a_cs = jnp.dot(tril, dta, precision=lax.Precision.HIGHEST, preferred_element_type=f32)
    a_cs_t = a_cs.T
    bmb = bm.astype(bf16)
    cmb = cm.astype(bf16)
    cb = _mm_nt(cmb, bmb)
    c_st = _mm(cmb, st.astype(bf16))

    def head_col(v, e):
        return jnp.sum(jnp.where(lane == e, v, 0.0), axis=1, keepdims=True)

    def head_row(v, e):
        return jnp.sum(jnp.where(sub == e, v, 0.0), axis=0, keepdims=True)

    ys, sts = [], []
    for j in range(4):
        e0, e1 = 2 * j, 2 * j + 1
        c0, c1 = head_col(a_cs, e0), head_col(a_cs, e1)
        acs_x = jnp.where(lo, c0, c1)
        dt_x = jnp.where(lo, head_col(dt, e0), head_col(dt, e1))
        xd = xs[:, j * LANES:(j + 1) * LANES] * dt_x
        m0 = cb * jnp.exp(jnp.where(causal, c0 - head_row(a_cs_t, e0), NEG_BIG))
        m1 = cb * jnp.exp(jnp.where(causal, c1 - head_row(a_cs_t, e1), NEG_BIG))
        mcat = jnp.concatenate([m0, m1], axis=1).astype(bf16)
        xcat = jnp.concatenate([jnp.where(lo, xd, 0.0), jnp.where(lo, 0.0, xd)], axis=0).astype(bf16)
        y_diag = _mm(mcat, xcat)
        a_last = jnp.sum(jnp.where(last_row, acs_x, 0.0), axis=0, keepdims=True)
        x_dec = (xd * jnp.exp(a_last - acs_x)).astype(bf16)
        s_new = _mm_tn(bmb, x_dec)
        y_off = c_st[:, j * LANES:(j + 1) * LANES] * jnp.exp(acs_x)
        ys.append(y_diag + y_off)
        sts.append(jnp.exp(a_last) * st[:, j * LANES:(j + 1) * LANES] + s_new)
    return jnp.concatenate(ys, axis=1), jnp.concatenate(sts, axis=1)


def _scan_specs(nc, reverse):
    L = SSD_CHUNK
    ch = (lambda c: nc - 1 - c) if reverse else (lambda c: c)
    gw = SSD_INNER // SSD_GROUPS
    xs = pl.BlockSpec((L, gw), lambda g, c: (ch(c), g))
    bm = pl.BlockSpec((L, SSD_STATE), lambda g, c: (ch(c), SSD_INNER // SSD_STATE + g))
    cm = pl.BlockSpec((L, SSD_STATE), lambda g, c: (ch(c), SSD_INNER // SSD_STATE + SSD_GROUPS + g))
    dt = pl.BlockSpec((None, L, LANES), lambda g, c: (g, ch(c), 0))
    alog = pl.BlockSpec((None, 1, LANES), lambda g, c: (g, 0, 0))
    st = pl.BlockSpec((None, None, SSD_STATE, gw), lambda g, c: (g, ch(c), 0, 0))
    y = pl.BlockSpec((L, gw), lambda g, c: (ch(c), g))
    grp = pl.BlockSpec((L, SSD_STATE), lambda g, c: (ch(c), g))
    return xs, bm, cm, dt, alog, st, y, grp


def ssd_scan_fwd(act, dt4, alog4):
    T = act.shape[0]
    nc = T // SSD_CHUNK
    gw = SSD_INNER // SSD_GROUPS
    xs_s, bm_s, cm_s, dt_s, alog_s, st_s, y_s, _ = _scan_specs(nc, False)

    def body(xs_ref, bm_ref, cm_ref, dt_ref, alog_ref, y_ref, st_ref, st_scr):
        @pl.when(pl.program_id(1) == 0)
        def _():
            st_scr[...] = jnp.zeros_like(st_scr)

        st = st_scr[...]
        st_ref[...] = st
        y, st_new = _ssd_chunk(xs_ref[...].astype(f32), bm_ref[...].astype(f32), cm_ref[...].astype(f32), dt_ref[...], alog_ref[...], st)
        y_ref[...] = y.astype(bf16)
        st_scr[...] = st_new

    return pl.pallas_call(
        body, name="ssd_scan_fwd", grid=(SSD_GROUPS, nc),
        in_specs=[xs_s, bm_s, cm_s, dt_s, alog_s], out_specs=[y_s, st_s],
        out_shape=[jax.ShapeDtypeStruct((T, SSD_INNER), bf16), jax.ShapeDtypeStruct((SSD_GROUPS, nc, SSD_STATE, gw), f32)],
        scratch_shapes=[pltpu.VMEM((SSD_STATE, gw), f32)],
        compiler_params=_params(("parallel", "arbitrary")),
    )(act, act, act, dt4, alog4)


def ssd_scan_bwd(act, dt4, alog4, states, dy):
    T = act.shape[0]
    nc = T // SSD_CHUNK
    gw = SSD_INNER // SSD_GROUPS
    xs_s, bm_s, cm_s, dt_s, alog_s, st_s, y_s, grp_s = _scan_specs(nc, True)

    def body(xs_ref, bm_ref, cm_ref, dt_ref, alog_ref, st_ref, dy_ref, dxs_ref, db_ref, dc_ref, ddt_ref, dalog_ref, dst_scr):
        @pl.when(pl.program_id(1) == 0)
        def _():
            dst_scr[...] = jnp.zeros_like(dst_scr)
            dalog_ref[...] = jnp.zeros_like(dalog_ref)

        _, vjp = jax.vjp(_ssd_chunk, xs_ref[...].astype(f32), bm_ref[...].astype(f32), cm_ref[...].astype(f32),
                         dt_ref[...], alog_ref[...], st_ref[...])
        dxs, dbm, dcm, ddt, dalog, dst = vjp((dy_ref[...].astype(f32), dst_scr[...]))
        dxs_ref[...] = dxs.astype(bf16)
        db_ref[...] = dbm.astype(bf16)
        dc_ref[...] = dcm.astype(bf16)
        ddt_ref[...] = ddt
        dalog_ref[...] += dalog
        dst_scr[...] = dst

    return pl.pallas_call(
        body, name="ssd_scan_bwd", grid=(SSD_GROUPS, nc),
        in_specs=[xs_s, bm_s, cm_s, dt_s, alog_s, st_s, y_s],
        out_specs=[y_s, grp_s, grp_s, dt_s, alog_s],
        out_shape=[jax.ShapeDtypeStruct((T, SSD_INNER), bf16), jax.ShapeDtypeStruct((T, SSD_GROUPS * SSD_STATE), bf16),
                   jax.ShapeDtypeStruct((T, SSD_GROUPS * SSD_STATE), bf16), jax.ShapeDtypeStruct((SSD_GROUPS, T, LANES), f32),
                   jax.ShapeDtypeStruct((SSD_GROUPS, 1, LANES), f32)],
        scratch_shapes=[pltpu.VMEM((SSD_STATE, gw), f32)],
        compiler_params=_params(("parallel", "arbitrary")),
    )(act, act, act, dt4, alog4, states, dy)


def _ssd_gate(y, xs, z, d_x, nw):
    g = (y + xs * d_x) * (z * _sigmoid(z))
    gw = SSD_INNER // SSD_GROUPS
    outs = []
    for k in range(SSD_GROUPS):
        gk = g[:, k * gw:(k + 1) * gw]
        outs.append(gk * lax.rsqrt(jnp.mean(gk * gk, axis=-1, keepdims=True) + RMS_EPS))
    return jnp.concatenate(outs, axis=1) * nw


def ssd_gate_fwd(y, act, z, d_x, nw, tm=256):
    T = y.shape[0]

    def body(y_ref, xs_ref, z_ref, d_ref, nw_ref, o_ref):
        o_ref[...] = _ssd_gate(y_ref[...].astype(f32), xs_ref[...].astype(f32), z_ref[...].astype(f32), d_ref[...], nw_ref[...]).astype(bf16)

    return pl.pallas_call(
        body, name="ssd_gate_fwd", grid=(T // tm,),
        in_specs=[_rows(tm, SSD_INNER), _rows(tm, SSD_INNER), _rows(tm, SSD_INNER), _resident(d_x.shape), _resident(nw.shape)],
        out_specs=_rows(tm, SSD_INNER), out_shape=jax.ShapeDtypeStruct((T, SSD_INNER), bf16),
        compiler_params=_params(("parallel",)),
    )(y, act, z, d_x, nw)


def ssd_gate_bwd(y, act, z, d_x, nw, dgn, tm=256):
    T = y.shape[0]

    def body(y_ref, xs_ref, z_ref, d_ref, nw_ref, dgn_ref, dy_ref, dxs_ref, dz_ref, dd_ref, dnw_ref):
        @pl.when(pl.program_id(0) == 0)
        def _():
            dd_ref[...] = jnp.zeros_like(dd_ref)
            dnw_ref[...] = jnp.zeros_like(dnw_ref)

        _, vjp = jax.vjp(_ssd_gate, y_ref[...].astype(f32), xs_ref[...].astype(f32), z_ref[...].astype(f32), d_ref[...], nw_ref[...])
        dy, dxs, dz, dd, dnw = vjp(dgn_ref[...].astype(f32))
        dy_ref[...] = dy.astype(bf16)
        dxs_ref[...] = dxs.astype(bf16)
        dz_ref[...] = dz.astype(bf16)
        dd_ref[...] += dd
        dnw_ref[...] += dnw

    const = pl.BlockSpec((1, SSD_INNER), lambda i: (0, 0))
    return pl.pallas_call(
        body, name="ssd_gate_bwd", grid=(T // tm,),
        in_specs=[_rows(tm, SSD_INNER), _rows(tm, SSD_INNER), _rows(tm, SSD_INNER), _resident(d_x.shape), _resident(nw.shape), _rows(tm, SSD_INNER)],
        out_specs=[_rows(tm, SSD_INNER)] * 3 + [const, const],
        out_shape=[jax.ShapeDtypeStruct((T, SSD_INNER), bf16)] * 3 + [jax.ShapeDtypeStruct((1, SSD_INNER), f32)] * 2,
        compiler_params=_params(("arbitrary",)),
    )(y, act, z, d_x, nw, dgn)


def sc_mid_fwd(bcu, conv_w, tm=256):
    T = bcu.shape[0]
    nt = T // tm
    Dm = D_MODEL

    def body(x_ref, halo_ref, w_ref, q_ref, ext_ref):
        xv = x_ref[...].astype(f32)
        hv = halo_ref[...].astype(f32)
        _fill_window(ext_ref, xv[:, Dm:2 * Dm] * xv[:, 2 * Dm:], hv[:, Dm:2 * Dm] * hv[:, 2 * Dm:], pl.program_id(0) == 0)
        q_ref[...] = (xv[:, :Dm] * _causal_conv(ext_ref, w_ref, SC_CONV_K, tm)).astype(bf16)

    return pl.pallas_call(
        body, name="sc_mid_fwd", grid=(nt,),
        in_specs=[_rows(tm, 3 * Dm), _halo_spec(tm, 3 * Dm, nt, False), _resident(conv_w.shape)],
        out_specs=_rows(tm, Dm), out_shape=jax.ShapeDtypeStruct((T, Dm), bf16),
        scratch_shapes=[pltpu.VMEM((tm + HALO, Dm), f32)],
        compiler_params=_params(("parallel",)),
    )(bcu, bcu, conv_w)


def sc_mid_bwd(bcu, conv_w, dq, tm=256):
    T = bcu.shape[0]
    nt = T // tm
    Dm = D_MODEL
    K = SC_CONV_K

    def body(x_ref, halo_ref, w_ref, dq_ref, dx_ref, dw_ref, ext_ref, dv_ref, carry_ref):
        i = pl.program_id(0)

        @pl.when(i == 0)
        def _():
            carry_ref[...] = jnp.zeros_like(carry_ref)
            dw_ref[...] = jnp.zeros_like(dw_ref)

        xv = x_ref[...].astype(f32)
        hv = halo_ref[...].astype(f32)
        bg, cg, uu = xv[:, :Dm], xv[:, Dm:2 * Dm], xv[:, 2 * Dm:]
        _fill_window(ext_ref, cg * uu, hv[:, Dm:2 * Dm] * hv[:, 2 * Dm:], i == nt - 1)
        v = _causal_conv(ext_ref, w_ref, K, tm)
        dqv = dq_ref[...].astype(f32)
        dv = dqv * bg
        dv_ref[pl.ds(0, tm), :] = dv
        dv_ref[pl.ds(tm, 8), :] = carry_ref[...]
        dp = w_ref[pl.ds(0, 1), :] * dv_ref[pl.ds(K - 1, tm), :]
        for k in range(1, K):
            dp = dp + w_ref[pl.ds(k, 1), :] * dv_ref[pl.ds(K - 1 - k, tm), :]
        base = HALO - (K - 1)
        for k in range(K):
            dw_ref[pl.ds(k, 1), :] += jnp.sum(dv * ext_ref[pl.ds(base + k, tm), :], axis=0, keepdims=True)
        carry_ref[...] = dv[0:8, :]
        dx_ref[...] = jnp.concatenate([dqv * v, dp * uu, dp * cg], axis=1).astype(bf16)

    return pl.pallas_call(
        body, name="sc_mid_bwd", grid=(nt,),
        in_specs=[_tile_spec(tm, 3 * Dm, nt, True), _halo_spec(tm, 3 * Dm, nt, True), _resident(conv_w.shape), _tile_spec(tm, Dm, nt, True)],
        out_specs=[_tile_spec(tm, 3 * Dm, nt, True), pl.BlockSpec((8, Dm), lambda i: (0, 0))],
        out_shape=[jax.ShapeDtypeStruct((T, 3 * Dm), bf16), jax.ShapeDtypeStruct((8, Dm), f32)],
        scratch_shapes=[pltpu.VMEM((tm + HALO, Dm), f32), pltpu.VMEM((tm + 8, Dm), f32), pltpu.VMEM((8, Dm), f32)],
        compiler_params=_params(("arbitrary",)),
    )(bcu, bcu, conv_w, dq)


def loss_head(x, fw, target, tm=512):
    T = x.shape[0]

    def body(x_ref, fw_ref, t_ref, loss_ref, dx_ref, dfw_ref):
        @pl.when(pl.program_id(0) == 0)
        def _():
            loss_ref[...] = jnp.zeros_like(loss_ref)
            dfw_ref[...] = jnp.zeros_like(dfw_ref)

        w = fw_ref[...]
        y, xh, inv = _rms_fwd(x_ref[...], w)
        err = y - t_ref[...]
        loss_ref[...] += 0.5 * jnp.sum(jnp.mean(err * err, axis=-1, keepdims=True), axis=0, keepdims=True)
        dx, dw = _rms_bwd(err * (1.0 / D_MODEL), xh, inv, w)
        dx_ref[...] = dx
        dfw_ref[...] += dw

    return pl.pallas_call(
        body, name="loss_head", grid=(T // tm,),
        in_specs=[_rows(tm, D_MODEL), _resident((1, D_MODEL)), _rows(tm, D_MODEL)],
        out_specs=[pl.BlockSpec((1, LANES), lambda i: (0, 0)), _rows(tm, D_MODEL), pl.BlockSpec((1, D_MODEL), lambda i: (0, 0))],
        out_shape=[jax.ShapeDtypeStruct((1, LANES), f32), jax.ShapeDtypeStruct((T, D_MODEL), f32), jax.ShapeDtypeStruct((1, D_MODEL), f32)],
        compiler_params=_params(("arbitrary",)),
    )(x, fw, target)


def _row_tile(rows):
    return rows if rows <= 512 else 256


def adamw(g_parts, w, m, v, name="adamw"):
    A, B, n = w.shape
    tb = _row_tile(B)
    n_parts = len(g_parts)
    arrays, specs = [], []

    def lead_index(lead, s):
        return tuple(s[int(e[1])] if isinstance(e, str) else e for e in lead)

    for part in g_parts:
        lead, arr = part if isinstance(part, tuple) else ((), part)
        specs.append(pl.BlockSpec((None,) * (len(lead) + 1) + (tb, n), lambda a, t, s, lead=lead: lead_index(lead, s) + (a, t, 0)))
        arrays.append(arr)

    def body(s_ref, *refs):
        n = n_parts
        g_refs = refs[:n]
        w_ref, m_ref, v_ref, go_ref, d_ref, mo_ref, vo_ref = refs[n:]
        g = g_refs[0][...].astype(f32)
        for r in g_refs[1:]:
            g = g + r[...].astype(f32)
        m_new = ADAM_B1 * m_ref[...] + (1.0 - ADAM_B1) * g
        v_new = ADAM_B2 * v_ref[...] + (1.0 - ADAM_B2) * (g * g)
        m_hat = m_new / (1.0 - ADAM_B1 ** ADAM_STEP)
        v_hat = v_new / (1.0 - ADAM_B2 ** ADAM_STEP)
        go_ref[...] = g
        d_ref[...] = -ADAM_LR * (m_hat / (jnp.sqrt(v_hat) + ADAM_EPS) + ADAM_WD * w_ref[...])
        mo_ref[...] = m_new
        vo_ref[...] = v_new

    plain = pl.BlockSpec((None, tb, n), lambda a, t, s: (a, t, 0))
    return lambda sel: pl.pallas_call(
        body, name=name,
        grid_spec=pltpu.PrefetchScalarGridSpec(num_scalar_prefetch=1, grid=(A, B // tb), in_specs=specs + [plain] * 3, out_specs=[plain] * 4),
        out_shape=[jax.ShapeDtypeStruct((A, B, n), f32)] * 4,
        compiler_params=_params(("parallel", "parallel")),
    )(sel, *arrays, w, m, v)


def pair_sum_bf16(sel, ga, gb, name):
    _, A, B, n = gb.shape
    tb = _row_tile(B)

    def body(s_ref, a_ref, b_ref, o_ref):
        o_ref[...] = (a_ref[...] + b_ref[...]).astype(bf16)

    return pl.pallas_call(
        body, name=name,
        grid_spec=pltpu.PrefetchScalarGridSpec(
            num_scalar_prefetch=1, grid=(4, A, B // tb),
            in_specs=[pl.BlockSpec((None, None, None, tb, n), lambda j, a, t, s: (s[0], j, a, t, 0)),
                      pl.BlockSpec((None, None, tb, n), lambda j, a, t, s: (j, a, t, 0))],
            out_specs=pl.BlockSpec((None, None, tb, n), lambda j, a, t, s: (j, a, t, 0))),
        out_shape=jax.ShapeDtypeStruct((4, A, B, n), bf16),
        compiler_params=_params(("parallel", "parallel", "parallel")),
    )(sel, ga, gb)


def assemble(gathered, axis, tk=256):
    _, A, K, n = gathered.shape
    if axis == 1:
        def body(w_ref, o_ref):
            o_ref[...] = jnp.concatenate([w_ref[j] for j in range(N_DEV)], axis=1)

        return pl.pallas_call(
            body, name=f"assemble_cols_{K}x{n}", grid=(A, K // tk),
            in_specs=[pl.BlockSpec((N_DEV, None, tk, n), lambda a, t: (0, a, t, 0))],
            out_specs=pl.BlockSpec((None, tk, N_DEV * n), lambda a, t: (a, t, 0)),
            out_shape=jax.ShapeDtypeStruct((A, K, N_DEV * n), gathered.dtype),
            compiler_params=_params(("parallel", "parallel")),
        )(gathered)

    def body(w_ref, o_ref):
        for j in range(N_DEV):
            o_ref[pl.ds(j * K, K), :] = w_ref[j]

    return pl.pallas_call(
        body, name=f"assemble_rows_{K}x{n}", grid=(A,),
        in_specs=[pl.BlockSpec((N_DEV, None, K, n), lambda a: (0, a, 0, 0))],
        out_specs=pl.BlockSpec((None, N_DEV * K, n), lambda a: (a, 0, 0)),
        out_shape=jax.ShapeDtypeStruct((A, N_DEV * K, n), gathered.dtype),
        compiler_params=_params(("parallel",)),
    )(gathered)


SSD_IN_PAD = 5248


def assemble_ssd_in(gathered, tk=256):
    _, A, K, n = gathered.shape

    def body(w_ref, z_ref, x_ref, dt_ref, full_ref):
        full_ref[:, pl.ds(SSD_IN_PAD - LANES, LANES)] = jnp.zeros((tk, LANES), gathered.dtype)
        for j in range(N_DEV):
            full_ref[:, pl.ds(j * n, n)] = w_ref[j]
        z_ref[...] = full_ref[:, pl.ds(0, SSD_INNER)]
        x_ref[...] = full_ref[:, pl.ds(SSD_INNER, SSD_CONV_DIM)]
        dt_ref[...] = full_ref[:, pl.ds(SSD_INNER + SSD_CONV_DIM, LANES)]

    widths = (SSD_INNER, SSD_CONV_DIM, LANES)
    return pl.pallas_call(
        body, name="assemble_ssd_in", grid=(A, K // tk),
        in_specs=[pl.BlockSpec((N_DEV, None, tk, n), lambda a, t: (0, a, t, 0))],
        out_specs=[pl.BlockSpec((None, tk, w), lambda a, t: (a, t, 0)) for w in widths],
        out_shape=[jax.ShapeDtypeStruct((A, K, w), gathered.dtype) for w in widths],
        scratch_shapes=[pltpu.VMEM((tk, SSD_IN_PAD), gathered.dtype)],
        compiler_params=_params(("parallel", "parallel")),
    )(gathered)


def ssd_in_to_shards(dwz, dwx, dwdt, buf, j, tk=256):
    K = dwz.shape[0]
    n = buf.shape[-1]

    def body(z_ref, x_ref, dt_ref, buf_ref, o_ref, full_ref):
        full_ref[:, pl.ds(0, SSD_INNER)] = z_ref[...]
        full_ref[:, pl.ds(SSD_INNER, SSD_CONV_DIM)] = x_ref[...]
        full_ref[:, pl.ds(SSD_INNER + SSD_CONV_DIM, LANES)] = dt_ref[...]
        for d in range(N_DEV):
            o_ref[d % 2, d // 2] = full_ref[:, pl.ds(d * n, n)]

    return pl.pallas_call(
        body, name="ssd_in_to_shards", grid=(K // tk,),
        in_specs=[_rows(tk, SSD_INNER), _rows(tk, SSD_CONV_DIM), _rows(tk, LANES), _ANY],
        out_specs=pl.BlockSpec((2, 4, None, tk, n), lambda t: (0, 0, j, t, 0)),
        out_shape=jax.ShapeDtypeStruct(buf.shape, f32),
        scratch_shapes=[pltpu.VMEM((tk, SSD_IN_PAD), f32)],
        input_output_aliases={3: 0},
        compiler_params=_params(("parallel",)),
    )(dwz, dwx, dwdt, buf)


def sum_over_devices(gathered):
    _, R, W = gathered.shape

    def body(g_ref, o_ref):
        acc = g_ref[0]
        for k in range(1, N_DEV):
            acc = acc + g_ref[k]
        o_ref[...] = acc

    return pl.pallas_call(
        body, name="sum_over_devices", grid=(1,),
        in_specs=[pl.BlockSpec((N_DEV, R, W), lambda i: (0, 0, 0))], out_specs=pl.BlockSpec((R, W), lambda i: (0, 0)),
        out_shape=jax.ShapeDtypeStruct((R, W), f32), compiler_params=_params(("arbitrary",)),
    )(gathered)


_ANY = pl.BlockSpec(memory_space=pl.ANY)


def all_gather(blocks, name):
    n = len(blocks)

    def body(*refs):
        x_refs, out_refs = refs[:n], refs[n:2 * n]
        send_sems, recv_sems, local_sems = refs[2 * n:]
        x, y, c = lax.axis_index("x"), lax.axis_index("y"), lax.axis_index("c")
        me, sibling = (x, y, c), (x, y, 1 - c)
        chips = [(1 - x, y), (x, 1 - y), (1 - x, 1 - y)]

        def copy(a, k, blk, to, src=None):
            px, py, pc = blk
            slot = out_refs[a].at[4 * px + 2 * py + pc]
            return pltpu.make_async_remote_copy(
                src_ref=slot if src is None else src, dst_ref=slot,
                send_sem=send_sems.at[7 * a + k], recv_sem=recv_sems.at[7 * a + k], device_id=to, device_id_type=MESH)

        mine, first, passed = [], [], []
        for a in range(n):
            mine.append(pltpu.make_async_copy(x_refs[a], out_refs[a].at[4 * x + 2 * y + c], local_sems.at[a]))
            mine[-1].start()
            first += [copy(a, 0, me, sibling, src=x_refs[a])] + [copy(a, 1 + j, me, (*chip, c), src=x_refs[a]) for j, chip in enumerate(chips)]
        for cp in first:
            cp.start()
        for j, chip in enumerate(chips):
            for a in range(n):
                copy(a, 1 + j, (*chip, c), me).wait_recv()
                passed.append(copy(a, 4 + j, (*chip, c), sibling))
                passed[-1].start()
        for a in range(n):
            copy(a, 0, sibling, me).wait_recv()
            for j, chip in enumerate(chips):
                copy(a, 4 + j, (*chip, 1 - c), me).wait_recv()
        for cp in first + passed:
            cp.wait_send()
        for cp in mine:
            cp.wait()

    return pl.pallas_call(
        body, name=name, in_specs=[_ANY] * n, out_specs=[_ANY] * n,
        out_shape=[jax.ShapeDtypeStruct((N_DEV,) + b.shape, b.dtype) for b in blocks],
        scratch_shapes=[pltpu.SemaphoreType.DMA((7 * n,)), pltpu.SemaphoreType.DMA((7 * n,)), pltpu.SemaphoreType.DMA((n,))],
    )(*blocks)


def exchange_with_sibling(gs):
    n = len(gs)

    def body(*refs):
        g_refs, recv_refs = refs[:n], refs[n:2 * n]
        send_sems, recv_sems = refs[2 * n:]
        x, y, c = lax.axis_index("x"), lax.axis_index("y"), lax.axis_index("c")
        cps = [pltpu.make_async_remote_copy(src_ref=g_refs[a].at[1 - c], dst_ref=recv_refs[a], send_sem=send_sems.at[a],
                                            recv_sem=recv_sems.at[a], device_id=(x, y, 1 - c), device_id_type=MESH) for a in range(n)]
        for cp in cps:
            cp.start()
        for cp in cps:
            cp.wait()

    return pl.pallas_call(
        body, name="exchange_with_sibling", in_specs=[_ANY] * n, out_specs=[_ANY] * n,
        out_shape=[jax.ShapeDtypeStruct(g.shape[1:], g.dtype) for g in gs],
        scratch_shapes=[pltpu.SemaphoreType.DMA((n,)), pltpu.SemaphoreType.DMA((n,))],
    )(*gs)


def exchange_between_chips(parts):
    n = len(parts)

    def body(*refs):
        p_refs, recv_refs = refs[:n], refs[n:2 * n]
        send_sems, recv_sems = refs[2 * n:]
        x, y, c = lax.axis_index("x"), lax.axis_index("y"), lax.axis_index("c")
        chips = [(1 - x, y), (x, 1 - y), (1 - x, 1 - y)]
        cps = [pltpu.make_async_remote_copy(src_ref=p_refs[a].at[2 * px + py], dst_ref=recv_refs[a].at[k], send_sem=send_sems.at[3 * a + k],
                                            recv_sem=recv_sems.at[3 * a + k], device_id=(px, py, c), device_id_type=MESH)
               for a in range(n) for k, (px, py) in enumerate(chips)]
        for cp in cps:
            cp.start()
        for cp in cps:
            cp.wait()

    return pl.pallas_call(
        body, name="exchange_between_chips", in_specs=[_ANY] * n, out_specs=[_ANY] * n,
        out_shape=[jax.ShapeDtypeStruct((3,) + p.shape[1:], p.dtype) for p in parts],
        scratch_shapes=[pltpu.SemaphoreType.DMA((3 * n,)), pltpu.SemaphoreType.DMA((3 * n,))],
    )(*parts)


PARAMS = {
    "norm_w": ((DEPTH, 3, D_MODEL), 2),
    "ffn_w_gate": ((DEPTH, 2, D_MODEL, D_FF), 3),
    "ffn_w_up": ((DEPTH, 2, D_MODEL, D_FF), 3),
    "ffn_w_down": ((DEPTH, 2, D_FF, D_MODEL), 2),
    "ssd_w_in": ((2, D_MODEL, SSD_IN_DIM), 2),
    "ssd_conv_w": ((2, SSD_CONV_K, SSD_CONV_DIM), 2),
    "ssd_conv_b": ((2, SSD_CONV_DIM), None),
    "ssd_dt_bias": ((2, SSD_HEADS), None),
    "ssd_a_log": ((2, SSD_HEADS), None),
    "ssd_d": ((2, SSD_HEADS), None),
    "ssd_norm_w": ((2, SSD_INNER), None),
    "ssd_w_out": ((2, SSD_INNER, D_MODEL), 1),
    "sc_w_in": ((2, D_MODEL, 3 * D_MODEL), 2),
    "sc_conv_w": ((2, SC_CONV_K, D_MODEL), 2),
    "sc_w_out": ((2, D_MODEL, D_MODEL), 1),
    "final_norm_w": ((D_MODEL,), None),
}
NAMES = list(PARAMS)
BIG = ["ffn_w_gate", "ffn_w_up", "ffn_w_down", "ssd_w_in", "ssd_w_out", "sc_w_in", "sc_w_out"]
SMALL = [n for n in NAMES if n not in BIG]
SMALL_SHARDED = [n for n in SMALL if PARAMS[n][1] is not None]


def _round_up(n, m):
    return -(-n // m) * m


def _pack(flat_list, rows_multiple):
    flat = jnp.concatenate(flat_list)
    rows = _round_up(_round_up(flat.shape[0], PACK_W) // PACK_W, rows_multiple)
    return jnp.pad(flat, (0, rows * PACK_W - flat.shape[0])).reshape(rows, PACK_W)


def _unpack(packed, shapes, lead=()):
    flat = packed.reshape(lead + (-1,))
    out, off = [], 0
    for shp in shapes:
        n = 1
        for s in shp:
            n *= s
        out.append(flat[..., off:off + n].reshape(lead + tuple(shp)))
        off += n
    return out


def _local_shape(name):
    shp, ax = PARAMS[name]
    if ax is None:
        return shp
    return shp[:ax] + (shp[ax] // N_DEV,) + shp[ax + 1:]


def _full_from_gathered(g, name):
    shp, ax = PARAMS[name]
    return jnp.moveaxis(g, 0, ax).reshape(shp)


def _by_destination(full, name):
    shp, ax = PARAMS[name]
    loc = shp[ax] // N_DEV
    return jnp.moveaxis(full.reshape(shp[:ax] + (N_DEV, loc) + shp[ax + 1:]), ax, 0)


def _ssd_layer_fwd(xin, nw, p, j):
    z, xbc, dt_raw = in_proj_fwd(xin, nw, [p["ssd_wz"][j], p["ssd_wx"][j], p["ssd_wdt"][j]], [bf16, bf16, f32])
    act, dt = ssd_conv_fwd(xbc, p["ssd_conv_w"][j], p["ssd_conv_b"][j], dt_raw, p["ssd_dt_bias"][j])
    T = xin.shape[0]
    dt4 = jnp.pad(dt[:, :SSD_HEADS].reshape(T, SSD_GROUPS, 8).transpose(1, 0, 2), ((0, 0), (0, 0), (0, LANES - 8)))
    y, states = ssd_scan_fwd(act, dt4, p["ssd_alog4"][j])
    gn = ssd_gate_fwd(y, act, z, p["ssd_dx"][j], p["ssd_norm_w"][j])
    xout = out_proj_fwd(xin, gn, p["ssd_w_out"][j])
    return xout, (xin, z, xbc, dt_raw, act, dt4, y, states, gn)


def _ssd_layer_bwd(dxo, nw, p, j, saved, gbuf):
    xin, z, xbc, dt_raw, act, dt4, y, states, gn = saved
    T = xin.shape[0]
    dgn, dyb = out_proj_bwd(dxo, p["ssd_w_out"][j])
    gbuf["ssd_w_out"] = tn_matmul_to_shards(gn, dyb, gbuf["ssd_w_out"], (j,), 0)
    g = {}
    dy, dxs_skip, dz, dd_x, dgnw = ssd_gate_bwd(y, act, z, p["ssd_dx"][j], p["ssd_norm_w"][j], dgn)
    g["ssd_norm_w"] = dgnw[0]
    g["ssd_d"] = jnp.sum(dd_x.reshape(SSD_HEADS, SSD_HEAD_DIM), axis=1)
    dxs, db, dc, ddt4, dalog4 = ssd_scan_bwd(act, dt4, p["ssd_alog4"][j], states, dy)
    g["ssd_a_log"] = dalog4[:, 0, :8].reshape(SSD_HEADS)
    ddt = jnp.pad(ddt4[:, :, :8].transpose(1, 0, 2).reshape(T, SSD_HEADS), ((0, 0), (0, LANES - SSD_HEADS)))
    dxbc, ddt_raw, dcw, dcb, ddtb = ssd_conv_bwd(xbc, p["ssd_conv_w"][j], p["ssd_conv_b"][j], dt_raw, p["ssd_dt_bias"][j], dxs, dxs_skip, db, dc, ddt)
    g["ssd_conv_w"] = dcw[:SSD_CONV_K]
    g["ssd_conv_b"] = dcb[0]
    g["ssd_dt_bias"] = ddtb[0, :SSD_HEADS]
    dx, h, dnw = in_proj_bwd(xin, nw, dxo, [dz, dxbc, ddt_raw], [p["ssd_wz"][j], p["ssd_wx"][j], p["ssd_wdt"][j]])
    gbuf["ssd_w_in"] = ssd_in_to_shards(tn_matmul(h, dz), tn_matmul(h, dxbc), tn_matmul(h, ddt_raw), gbuf["ssd_w_in"], j)
    return dx, dnw, g


def _sc_layer_fwd(xin, nw, p, j):
    (bcu,) = in_proj_fwd(xin, nw, [p["sc_w_in"][j]], [bf16])
    q = sc_mid_fwd(bcu, p["sc_conv_w"][j])
    return out_proj_fwd(xin, q, p["sc_w_out"][j]), (xin, bcu, q)


def _sc_layer_bwd(dxo, nw, p, j, saved, gbuf):
    xin, bcu, q = saved
    dq, dyb = out_proj_bwd(dxo, p["sc_w_out"][j])
    gbuf["sc_w_out"] = tn_matmul_to_shards(q, dyb, gbuf["sc_w_out"], (j,), 0)
    dbcu, dcw = sc_mid_bwd(bcu, p["sc_conv_w"][j], dq)
    g = {"sc_conv_w": dcw[:SC_CONV_K]}
    dx, h, dnw = in_proj_bwd(xin, nw, dxo, [dbcu], [p["sc_w_in"][j]])
    gbuf["sc_w_in"] = tn_matmul_to_shards(h, dbcu, gbuf["sc_w_in"], (j,), 1)
    return dx, dnw, g


def kernel(x, norm_w, ffn_w_gate, ffn_w_up, ffn_w_down, ssd_w_in, ssd_conv_w, ssd_conv_b, ssd_dt_bias, ssd_a_log, ssd_d, ssd_norm_w, ssd_w_out, sc_w_in, sc_conv_w, sc_w_out, final_norm_w, loss_target, m_norm_w, m_ffn_w_gate, m_ffn_w_up, m_ffn_w_down, m_ssd_w_in, m_ssd_conv_w, m_ssd_conv_b, m_ssd_dt_bias, m_ssd_a_log, m_ssd_d, m_ssd_norm_w, m_ssd_w_out, m_sc_w_in, m_sc_conv_w, m_sc_w_out, m_final_norm_w, v_norm_w, v_ffn_w_gate, v_ffn_w_up, v_ffn_w_down, v_ssd_w_in, v_ssd_conv_w, v_ssd_conv_b, v_ssd_dt_bias, v_ssd_a_log, v_ssd_d, v_ssd_norm_w, v_ssd_w_out, v_sc_w_in, v_sc_conv_w, v_sc_w_out, v_final_norm_w):
    w_loc = dict(zip(NAMES, (norm_w, ffn_w_gate, ffn_w_up, ffn_w_down, ssd_w_in, ssd_conv_w, ssd_conv_b, ssd_dt_bias, ssd_a_log, ssd_d, ssd_norm_w, ssd_w_out, sc_w_in, sc_conv_w, sc_w_out, final_norm_w)))
    m_loc = dict(zip(NAMES, (m_norm_w, m_ffn_w_gate, m_ffn_w_up, m_ffn_w_down, m_ssd_w_in, m_ssd_conv_w, m_ssd_conv_b, m_ssd_dt_bias, m_ssd_a_log, m_ssd_d, m_ssd_norm_w, m_ssd_w_out, m_sc_w_in, m_sc_conv_w, m_sc_w_out, m_final_norm_w)))
    v_loc = dict(zip(NAMES, (v_norm_w, v_ffn_w_gate, v_ffn_w_up, v_ffn_w_down, v_ssd_w_in, v_ssd_conv_w, v_ssd_conv_b, v_ssd_dt_bias, v_ssd_a_log, v_ssd_d, v_ssd_norm_w, v_ssd_w_out, v_sc_w_in, v_sc_conv_w, v_sc_w_out, v_final_norm_w)))
    ax, ay, ac = lax.axis_index("x"), lax.axis_index("y"), lax.axis_index("c")
    my_chip = 2 * ax + ay
    my_dev = 4 * ax + 2 * ay + ac
    T = x.shape[1]

    def as3d(a):
        return a.reshape((-1,) + a.shape[-2:])

    ss_shapes = [_local_shape(n) for n in SMALL_SHARDED]
    gathered = all_gather([as3d(w_loc[n]).astype(bf16) for n in BIG] + [_pack([w_loc[n].reshape(-1) for n in SMALL_SHARDED], 8)],
                          "all_gather_weights")
    big_all = dict(zip(BIG, gathered[:-1]))
    full = {}
    for n in BIG:
        if n != "ssd_w_in":
            full[n] = assemble(big_all[n], 1 if PARAMS[n][1] == len(PARAMS[n][0]) - 1 else 0).reshape(PARAMS[n][0])
    for n, part in zip(SMALL_SHARDED, _unpack(gathered[-1], ss_shapes, lead=(N_DEV,))):
        full[n] = _full_from_gathered(part, n)
    for n in SMALL:
        if PARAMS[n][1] is None:
            full[n] = w_loc[n]

    p = dict(full)
    p["ssd_wz"], p["ssd_wx"], p["ssd_wdt"] = assemble_ssd_in(big_all["ssd_w_in"])
    p["ssd_conv_b"] = full["ssd_conv_b"].reshape(2, 1, SSD_CONV_DIM)
    p["ssd_dt_bias"] = jnp.pad(full["ssd_dt_bias"], ((0, 0), (0, LANES - SSD_HEADS))).reshape(2, 1, LANES)
    p["ssd_alog4"] = jnp.pad(full["ssd_a_log"].reshape(2, SSD_GROUPS, 1, 8), ((0, 0), (0, 0), (0, 0), (0, LANES - 8)))
    p["ssd_dx"] = jnp.repeat(full["ssd_d"], SSD_HEAD_DIM, axis=1).reshape(2, 1, SSD_INNER)
    p["ssd_norm_w"] = full["ssd_norm_w"].reshape(2, 1, SSD_INNER)
    nw_all = full["norm_w"].reshape(DEPTH, 3, 1, D_MODEL)

    xc = x[0]
    saved = []
    for i in range(DEPTH):
        j = i // 2
        x1, g1, u1, a1 = ffn_fwd(xc, nw_all[i, 0], full["ffn_w_gate"][i, 0], full["ffn_w_up"][i, 0], full["ffn_w_down"][i, 0])
        if i % 2 == 0:
            x2, mix_saved = _ssd_layer_fwd(x1, nw_all[i, 1], p, j)
        else:
            x2, mix_saved = _sc_layer_fwd(x1, nw_all[i, 1], p, j)
        x3, g3, u3, a3 = ffn_fwd(x2, nw_all[i, 2], full["ffn_w_gate"][i, 1], full["ffn_w_up"][i, 1], full["ffn_w_down"][i, 1])
        saved.append(((xc, g1, u1, a1), mix_saved, (x2, g3, u3, a3)))
        xc = x3

    loss_row, dx, dfw = loss_head(xc, full["final_norm_w"].reshape(1, D_MODEL), loss_target[0])
    loss = lax.psum(loss_row[0, 0], ("x", "y", "c"))

    grads = {n: [None] * PARAMS[n][0][0] for n in SMALL if n != "final_norm_w"}
    grads["final_norm_w"] = dfw[0]
    dnorm = [[None] * 3 for _ in range(DEPTH)]
    gbuf = {n: jnp.zeros((2, 4) + _local_shape(n), f32) for n in BIG}

    def ffn_back(i, k, dxo, sv):
        xin, g_, u_, a_ = sv
        dxi, h, dyb, dg, du, dnw = ffn_bwd_dx(xin, dxo, g_, u_, nw_all[i, 2 * k], full["ffn_w_gate"][i, k], full["ffn_w_up"][i, k], full["ffn_w_down"][i, k])
        dnorm[i][2 * k] = dnw[0]
        gbuf["ffn_w_gate"] = tn_matmul_to_shards(h, dg, gbuf["ffn_w_gate"], (i, k), 1)
        gbuf["ffn_w_up"] = tn_matmul_to_shards(h, du, gbuf["ffn_w_up"], (i, k), 1)
        gbuf["ffn_w_down"] = tn_matmul_to_shards(a_, dyb, gbuf["ffn_w_down"], (i, k), 0)
        return dxi

    for i in reversed(range(DEPTH)):
        j = i // 2
        sv_a, sv_mix, sv_b = saved[i]
        dx = ffn_back(i, 1, dx, sv_b)
        if i % 2 == 0:
            dx, dnw, gm = _ssd_layer_bwd(dx, nw_all[i, 1], p, j, sv_mix, gbuf)
        else:
            dx, dnw, gm = _sc_layer_bwd(dx, nw_all[i, 1], p, j, sv_mix, gbuf)
        dnorm[i][1] = dnw[0]
        for n, val in gm.items():
            grads[n][j] = val
        dx = ffn_back(i, 0, dx, sv_a)

    grads["norm_w"] = jnp.stack([jnp.stack(r) for r in dnorm])
    for n in SMALL:
        if isinstance(grads[n], list):
            grads[n] = jnp.stack(grads[n])

    sel = jnp.stack([ac, my_chip]).astype(jnp.int32)
    g5 = [gbuf[n].reshape((2, 4) + as3d(w_loc[n]).shape) for n in BIG]
    from_sibling = exchange_with_sibling(g5)
    chip_parts = [pair_sum_bf16(sel, g, fs, "pair_sum_" + n) for n, g, fs in zip(BIG, g5, from_sibling)]
    from_chips = exchange_between_chips(chip_parts)
    results = [{}, {}, {}, {}]
    for n, g, fs, fc in zip(BIG, g5, from_sibling, from_chips):
        parts = [(("s0", "s1"), g), (("s1",), fs), ((0,), fc), ((1,), fc), ((2,), fc)]
        outs = adamw(parts, as3d(w_loc[n]), as3d(m_loc[n]), as3d(v_loc[n]), name="adamw_" + n)(sel)
        for k in range(4):
            results[k][n] = outs[k].reshape(_local_shape(n))

    g_small = _pack([grads[n].reshape(-1) for n in SMALL], 8)
    g_small = sum_over_devices(all_gather([g_small], "all_gather_small_grads")[0])
    g_small_full = dict(zip(SMALL, _unpack(g_small, [PARAMS[n][0] for n in SMALL])))
    g_small_loc = []
    for n in SMALL:
        if PARAMS[n][1] is None:
            g_small_loc.append(g_small_full[n])
        else:
            g_small_loc.append(lax.dynamic_index_in_dim(_by_destination(g_small_full[n], n), my_dev, axis=0, keepdims=False))
    small_shapes = [_local_shape(n) for n in SMALL]
    pack_small = lambda d: _pack([d[n].reshape(-1) for n in SMALL], 8)[None]
    small_out = adamw([_pack([gl.reshape(-1) for gl in g_small_loc], 8)[None]], pack_small(w_loc), pack_small(m_loc), pack_small(v_loc), name="adamw_small")(sel)
    for k in range(4):
        results[k].update(zip(SMALL, _unpack(small_out[k], small_shapes)))
    return (loss, dx[None], *[results[0][n] for n in NAMES], *[results[1][n] for n in NAMES],
            *[results[2][n] for n in NAMES], *[results[3][n] for n in NAMES])
```

```python
import functools

import jax
import jax.numpy as jnp
from jax import lax
from jax.experimental import pallas as pl
from jax.experimental.pallas import tpu as pltpu

f32 = jnp.float32
bf16 = jnp.bfloat16

D_MODEL = 1024
D_FF = 2816
DEPTH = 4
SSD_INNER = 2048
SSD_HEADS = 32
SSD_HEAD_DIM = 64
SSD_GROUPS = 4
SSD_STATE = 128
SSD_CONV_K = 4
SSD_CONV_DIM = 3072
SSD_IN_DIM = 5152
SSD_CHUNK = 128
SC_CONV_K = 3
RMS_EPS = 1e-5
N_DEV = 8
LANES = 128
HALO = 16
PACK_W = 1024
PACK_TILE = 256
VMEM_LIMIT = 56 * 1024 * 1024
NEG_BIG = -1e30

ADAM_LR = 0.001
ADAM_B1 = 0.9
ADAM_B2 = 0.999
ADAM_EPS = 1e-08
ADAM_WD = 0.01
ADAM_STEP = 10

NT_DIMS = (((1,), (1,)), ((), ()))
TN_DIMS = (((0,), (0,)), ((), ()))
MESH = pl.DeviceIdType.MESH


def _params(sem=None):
    return pltpu.CompilerParams(dimension_semantics=sem, vmem_limit_bytes=VMEM_LIMIT)


def _resident(shape):
    nd = len(shape)
    return pl.BlockSpec(tuple(shape), lambda *_: (0,) * nd, pipeline_mode=pl.Buffered(1))


def _rows(tm, width):
    return pl.BlockSpec((tm, width), lambda i: (i, 0))


def _my_core_and_chip():
    return lax.axis_index("c"), 2 * lax.axis_index("x") + lax.axis_index("y")


def _sigmoid(v):
    return 1.0 / (1.0 + jnp.exp(-v))


def _softplus(v):
    return jnp.maximum(v, 0.0) + jnp.log(1.0 + jnp.exp(-jnp.abs(v)))


def _rms_fwd(xv, w):
    inv = lax.rsqrt(jnp.mean(xv * xv, axis=-1, keepdims=True) + RMS_EPS)
    xh = xv * inv
    return xh * w, xh, inv


def _rms_bwd(dh, xh, inv, w):
    dxh = dh * w
    dx = inv * (dxh - xh * jnp.mean(dxh * xh, axis=-1, keepdims=True))
    return dx, jnp.sum(dh * xh, axis=0, keepdims=True)


def _mm(a, b):
    return jnp.dot(a, b, preferred_element_type=f32)


def _mm_nt(a, b):
    return lax.dot_general(a, b, NT_DIMS, preferred_element_type=f32)


def _mm_tn(a, b):
    return lax.dot_general(a, b, TN_DIMS, preferred_element_type=f32)


def _layer_slab(w, idx):
    tail = w.shape[len(idx):]
    return pl.BlockSpec((None,) * len(idx) + tuple(tail), lambda *_: tuple(idx) + (0,) * len(tail), pipeline_mode=pl.Buffered(1))


def ffn_fwd(x, nw, wg, wu, wd, idx, tm=256):
    T = x.shape[0]

    def body(x_ref, nw_ref, wg_ref, wu_ref, wd_ref, xo_ref, g_ref, u_ref, a_ref):
        xv = x_ref[...]
        h, _, _ = _rms_fwd(xv, nw_ref[...])
        hb = h.astype(bf16)
        g = _mm(hb, wg_ref[...])
        u = _mm(hb, wu_ref[...])
        ab = (g * _sigmoid(g) * u).astype(bf16)
        xo_ref[...] = xv + 0.5 * _mm(ab, wd_ref[...])
        g_ref[...] = g.astype(bf16)
        u_ref[...] = u.astype(bf16)
        a_ref[...] = ab

    return pl.pallas_call(
        body, name="ffn_fwd", grid=(T // tm,),
        in_specs=[_rows(tm, D_MODEL), _resident((1, D_MODEL)), _layer_slab(wg, idx), _layer_slab(wu, idx), _layer_slab(wd, idx)],
        out_specs=[_rows(tm, D_MODEL), _rows(tm, D_FF), _rows(tm, D_FF), _rows(tm, D_FF)],
        out_shape=[jax.ShapeDtypeStruct((T, D_MODEL), f32)] + [jax.ShapeDtypeStruct((T, D_FF), bf16)] * 3,
        compiler_params=_params(("parallel",)),
    )(x, nw, wg, wu, wd)


def ffn_bwd_dx(x, dxo, g, u, nw, wg, wu, wd, idx, tm=256):
    T = x.shape[0]

    def body(x_ref, dxo_ref, g_ref, u_ref, nw_ref, wg_ref, wu_ref, wd_ref, dx_ref, h_ref, dy_ref, dg_ref, du_ref, dnw_ref):
        w = nw_ref[...]
        h, xh, inv = _rms_fwd(x_ref[...], w)
        dxo_v = dxo_ref[...]
        dyb = (0.5 * dxo_v).astype(bf16)
        da = _mm_nt(dyb, wd_ref[...])
        gv = g_ref[...].astype(f32)
        uv = u_ref[...].astype(f32)
        s = _sigmoid(gv)
        dgb = (da * uv * (s * (1.0 + gv * (1.0 - s)))).astype(bf16)
        dub = (da * (gv * s)).astype(bf16)
        dh = _mm_nt(dgb, wg_ref[...]) + _mm_nt(dub, wu_ref[...])
        dxn, dw = _rms_bwd(dh, xh, inv, w)
        dx_ref[...] = dxo_v + dxn
        h_ref[...] = h.astype(bf16)
        dy_ref[...] = dyb
        dg_ref[...] = dgb
        du_ref[...] = dub

        @pl.when(pl.program_id(0) == 0)
        def _():
            dnw_ref[...] = jnp.zeros_like(dnw_ref)

        dnw_ref[...] += dw

    return pl.pallas_call(
        body, name="ffn_bwd_dx", grid=(T // tm,),
        in_specs=[_rows(tm, D_MODEL), _rows(tm, D_MODEL), _rows(tm, D_FF), _rows(tm, D_FF), _resident((1, D_MODEL)),
                  _layer_slab(wg, idx), _layer_slab(wu, idx), _layer_slab(wd, idx)],
        out_specs=[_rows(tm, D_MODEL), _rows(tm, D_MODEL), _rows(tm, D_MODEL), _rows(tm, D_FF), _rows(tm, D_FF),
                   pl.BlockSpec((1, D_MODEL), lambda i: (0, 0))],
        out_shape=[jax.ShapeDtypeStruct((T, D_MODEL), f32), jax.ShapeDtypeStruct((T, D_MODEL), bf16), jax.ShapeDtypeStruct((T, D_MODEL), bf16),
                   jax.ShapeDtypeStruct((T, D_FF), bf16), jax.ShapeDtypeStruct((T, D_FF), bf16), jax.ShapeDtypeStruct((1, D_MODEL), f32)],
        compiler_params=_params(("arbitrary",)),
    )(x, dxo, g, u, nw, wg, wu, wd)


def tn_matmul(a, b, tk=512):
    T, M = a.shape
    N = b.shape[1]
    bn = N if M * N <= 3_200_000 else N // 2
    nk = T // tk

    def body(a_ref, b_ref, o_ref):
        @pl.when(pl.program_id(1) == 0)
        def _():
            o_ref[...] = jnp.zeros_like(o_ref)

        o_ref[...] += _mm_tn(a_ref[...], b_ref[...])

    return pl.pallas_call(
        body, name=f"tn_matmul_{M}x{N}", grid=(N // bn, nk),
        in_specs=[pl.BlockSpec((tk, M), lambda j, k: (k, 0)), pl.BlockSpec((tk, bn), lambda j, k: (k, j))],
        out_specs=pl.BlockSpec((M, bn), lambda j, k: (0, j)),
        out_shape=jax.ShapeDtypeStruct((M, N), f32),
        compiler_params=_params(("parallel", "arbitrary")),
    )(a, b)


def tn_matmul_to_shards(a, b, buf, idx, axis, tk=512):
    T, M = a.shape
    N = b.shape[1]
    m, n = buf.shape[-2:]
    nk = T // tk

    def body(a_ref, b_ref, buf_ref, o_ref, acc_ref):
        k = pl.program_id(0)

        @pl.when(k == 0)
        def _():
            acc_ref[...] = jnp.zeros_like(acc_ref)

        acc_ref[...] += _mm_tn(a_ref[...], b_ref[...])

        @pl.when(k == nk - 1)
        def _():
            my_c, my_chip = _my_core_and_chip()
            for d in range(N_DEV):
                piece = acc_ref[:, pl.ds(d * n, n)] if axis == 1 else acc_ref[pl.ds(d * m, m), :]
                o_ref[(d % 2) ^ my_c, (d // 2) ^ my_chip] = piece

    none = (None,) * len(idx)
    return pl.pallas_call(
        body, name=f"tn_matmul_to_shards_{M}x{N}_{axis}", grid=(nk,),
        in_specs=[pl.BlockSpec((tk, M), lambda k: (k, 0)), pl.BlockSpec((tk, N), lambda k: (k, 0)), _ANY],
        out_specs=pl.BlockSpec((2, 4) + none + (m, n), lambda k: (0, 0) + tuple(idx) + (0, 0)),
        out_shape=jax.ShapeDtypeStruct(buf.shape, f32),
        scratch_shapes=[pltpu.VMEM((M, N), f32)],
        input_output_aliases={2: 0},
        compiler_params=_params(("arbitrary",)),
    )(a, b, buf)


def in_proj_fwd(x, nw, ws, out_dtypes, tm=256):
    T = x.shape[0]
    n = len(ws)

    def body(*refs):
        x_ref, nw_ref = refs[:2]
        w_refs = refs[2:2 + n]
        o_refs = refs[2 + n:]
        h, _, _ = _rms_fwd(x_ref[...], nw_ref[...])
        hb = h.astype(bf16)
        for w_ref, o_ref in zip(w_refs, o_refs):
            o_ref[...] = _mm(hb, w_ref[...]).astype(o_ref.dtype)

    return pl.pallas_call(
        body, name="in_proj_fwd_" + "_".join(str(w.shape[1]) for w in ws), grid=(T // tm,),
        in_specs=[_rows(tm, D_MODEL), _resident((1, D_MODEL))] + [_resident(w.shape) for w in ws],
        out_specs=[_rows(tm, w.shape[1]) for w in ws],
        out_shape=[jax.ShapeDtypeStruct((T, w.shape[1]), dt) for w, dt in zip(ws, out_dtypes)],
        compiler_params=_params(("parallel",)),
    )(x, nw, *ws)


def in_proj_bwd(x, nw, dxo, dys, ws, tm=256):
    T = x.shape[0]
    n = len(ws)

    def body(*refs):
        x_ref, nw_ref, dxo_ref = refs[:3]
        dy_refs = refs[3:3 + n]
        w_refs = refs[3 + n:3 + 2 * n]
        dx_ref, h_ref, dnw_ref = refs[3 + 2 * n:]
        w = nw_ref[...]
        h, xh, inv = _rms_fwd(x_ref[...], w)
        dh = _mm_nt(dy_refs[0][...], w_refs[0][...])
        for dy_ref, w_ref in zip(dy_refs[1:], w_refs[1:]):
            dh = dh + _mm_nt(dy_ref[...], w_ref[...])
        dxn, dw = _rms_bwd(dh, xh, inv, w)
        dx_ref[...] = dxo_ref[...] + dxn
        h_ref[...] = h.astype(bf16)

        @pl.when(pl.program_id(0) == 0)
        def _():
            dnw_ref[...] = jnp.zeros_like(dnw_ref)

        dnw_ref[...] += dw

    return pl.pallas_call(
        body, name="in_proj_bwd_" + "_".join(str(w.shape[1]) for w in ws), grid=(T // tm,),
        in_specs=[_rows(tm, D_MODEL), _resident((1, D_MODEL)), _rows(tm, D_MODEL)] + [_rows(tm, w.shape[1]) for w in ws]
        + [_resident(w.shape) for w in ws],
        out_specs=[_rows(tm, D_MODEL), _rows(tm, D_MODEL), pl.BlockSpec((1, D_MODEL), lambda i: (0, 0))],
        out_shape=[jax.ShapeDtypeStruct((T, D_MODEL), f32), jax.ShapeDtypeStruct((T, D_MODEL), bf16), jax.ShapeDtypeStruct((1, D_MODEL), f32)],
        compiler_params=_params(("arbitrary",)),
    )(x, nw, dxo, *dys, *ws)


def out_proj_fwd(x, a, w, tm=512):
    T = x.shape[0]
    K = a.shape[1]

    def body(x_ref, a_ref, w_ref, o_ref):
        o_ref[...] = x_ref[...] + _mm(a_ref[...], w_ref[...])

    return pl.pallas_call(
        body, name=f"out_proj_fwd_{K}", grid=(T // tm,),
        in_specs=[_rows(tm, D_MODEL), _rows(tm, K), _resident(w.shape)],
        out_specs=_rows(tm, D_MODEL), out_shape=jax.ShapeDtypeStruct((T, D_MODEL), f32),
        compiler_params=_params(("parallel",)),
    )(x, a, w)


def out_proj_bwd(dxo, w, tm=512):
    T = dxo.shape[0]
    K = w.shape[0]

    def body(dxo_ref, w_ref, da_ref, dy_ref):
        dyb = dxo_ref[...].astype(bf16)
        dy_ref[...] = dyb
        da_ref[...] = _mm_nt(dyb, w_ref[...]).astype(bf16)

    return pl.pallas_call(
        body, name=f"out_proj_bwd_{K}", grid=(T // tm,),
        in_specs=[_rows(tm, D_MODEL), _resident(w.shape)],
        out_specs=[_rows(tm, K), _rows(tm, D_MODEL)],
        out_shape=[jax.ShapeDtypeStruct((T, K), bf16), jax.ShapeDtypeStruct((T, D_MODEL), bf16)],
        compiler_params=_params(("parallel",)),
    )(dxo, w)


def _halo_spec(tm, width, n_tiles, reverse):
    per = tm // HALO

    def idx(i):
        t = (n_tiles - 1 - i) if reverse else i
        return (jnp.maximum(t * per - 1, 0), 0)

    return pl.BlockSpec((HALO, width), idx)


def _tile_spec(tm, width, n_tiles, reverse):
    if reverse:
        return pl.BlockSpec((tm, width), lambda i: (n_tiles - 1 - i, 0))
    return _rows(tm, width)


ROW_BLOCK = 64


def _strip(s):
    return pl.ds(pl.multiple_of(s * LANES, LANES), LANES)


def _conv_rows(ext_ref, w_ref, cols, k_w, r0):
    base = HALO - (k_w - 1) + r0
    wins = [ext_ref[pl.ds(base + k, ROW_BLOCK), :] for k in range(k_w)]
    out = w_ref[pl.ds(0, 1), cols] * wins[0]
    for k in range(1, k_w):
        out = out + w_ref[pl.ds(k, 1), cols] * wins[k]
    return out, wins


def _shifted_back(d_ref, w_ref, cols, k_w, r0):
    out = w_ref[pl.ds(0, 1), cols] * d_ref[pl.ds(r0 + k_w - 1, ROW_BLOCK), :]
    for k in range(1, k_w):
        out = out + w_ref[pl.ds(k, 1), cols] * d_ref[pl.ds(r0 + k_w - 1 - k, ROW_BLOCK), :]
    return out


def ssd_conv_fwd(xbc, conv_w, conv_b, dt_raw, dt_bias, tm=256):
    T = xbc.shape[0]
    nt = T // tm
    K = SSD_CONV_K

    def body(x_ref, halo_ref, w_ref, b_ref, dtr_ref, dtb_ref, act_ref, dt_ref, ext_ref):
        first = pl.program_id(0) == 0

        def strip(s, carry):
            cols = _strip(s)
            ext_ref[pl.ds(0, HALO), :] = jnp.where(first, 0.0, halo_ref[:, cols].astype(f32))
            ext_ref[pl.ds(HALO, tm), :] = x_ref[:, cols].astype(f32)
            for r0 in range(0, tm, ROW_BLOCK):
                pre, _ = _conv_rows(ext_ref, w_ref, cols, K, r0)
                pre = pre + b_ref[:, cols]
                act_ref[pl.ds(r0, ROW_BLOCK), cols] = (pre * _sigmoid(pre)).astype(bf16)
            return carry

        lax.fori_loop(0, SSD_CONV_DIM // LANES, strip, 0)
        dt_ref[...] = _softplus(dtr_ref[...] + dtb_ref[...])

    return pl.pallas_call(
        body, name="ssd_conv_fwd", grid=(nt,),
        in_specs=[_rows(tm, SSD_CONV_DIM), _halo_spec(tm, SSD_CONV_DIM, nt, False), _resident(conv_w.shape), _resident(conv_b.shape),
                  _rows(tm, LANES), _resident(dt_bias.shape)],
        out_specs=[_rows(tm, SSD_CONV_DIM), _rows(tm, LANES)],
        out_shape=[jax.ShapeDtypeStruct((T, SSD_CONV_DIM), bf16), jax.ShapeDtypeStruct((T, LANES), f32)],
        scratch_shapes=[pltpu.VMEM((tm + HALO, LANES), f32)],
        compiler_params=_params(("parallel",)),
    )(xbc, xbc, conv_w, conv_b, dt_raw, dt_bias)


def ssd_conv_bwd(xbc, conv_w, conv_b, dt_raw, dt_bias, dxs_a, dxs_b, db, dc, ddt, tm=256):
    T = xbc.shape[0]
    nt = T // tm
    K = SSD_CONV_K

    def body(x_ref, halo_ref, w_ref, b_ref, dtr_ref, dtb_ref, da_ref, dbb_ref, db_ref, dc_ref, ddt_ref,
             dx_ref, ddtr_ref, dw_ref, dbias_ref, ddtb_ref, ext_ref, dpre_ref, carry_ref):
        i = pl.program_id(0)

        @pl.when(i == 0)
        def _():
            carry_ref[...] = jnp.zeros_like(carry_ref)
            dw_ref[...] = jnp.zeros_like(dw_ref)
            dbias_ref[...] = jnp.zeros_like(dbias_ref)
            ddtb_ref[...] = jnp.zeros_like(ddtb_ref)

        first_tile = i == nt - 1

        def run_strips(lo, hi, load_dact):
            def strip(s, carry):
                cols = _strip(s)
                ext_ref[pl.ds(0, HALO), :] = jnp.where(first_tile, 0.0, halo_ref[:, cols].astype(f32))
                ext_ref[pl.ds(HALO, tm), :] = x_ref[:, cols].astype(f32)
                dpre_ref[pl.ds(tm, 8), :] = carry_ref[:, cols]
                bias = b_ref[:, cols]
                dws = [jnp.zeros((1, LANES), f32) for _ in range(K)]
                dbs = jnp.zeros((1, LANES), f32)
                for r0 in range(0, tm, ROW_BLOCK):
                    pre, wins = _conv_rows(ext_ref, w_ref, cols, K, r0)
                    pre = pre + bias
                    sg = _sigmoid(pre)
                    dpre = load_dact(s, r0) * (sg * (1.0 + pre * (1.0 - sg)))
                    dpre_ref[pl.ds(r0, ROW_BLOCK), :] = dpre
                    dbs = dbs + jnp.sum(dpre, axis=0, keepdims=True)
                    for k in range(K):
                        dws[k] = dws[k] + jnp.sum(dpre * wins[k], axis=0, keepdims=True)
                carry_ref[:, cols] = dpre_ref[pl.ds(0, 8), :]
                for r0 in range(0, tm, ROW_BLOCK):
                    dx_ref[pl.ds(r0, ROW_BLOCK), cols] = _shifted_back(dpre_ref, w_ref, cols, K, r0).astype(bf16)
                for k in range(K):
                    dw_ref[pl.ds(k, 1), cols] += dws[k]
                dbias_ref[:, cols] += dbs
                return carry

            lax.fori_loop(lo, hi, strip, 0)

        rows = lambda r0: pl.ds(r0, ROW_BLOCK)
        n_x = SSD_INNER // LANES
        n_g = SSD_GROUPS * SSD_STATE // LANES
        run_strips(0, n_x, lambda s, r0: da_ref[rows(r0), _strip(s)].astype(f32) + dbb_ref[rows(r0), _strip(s)].astype(f32))
        run_strips(n_x, n_x + n_g, lambda s, r0: db_ref[rows(r0), _strip(s - n_x)].astype(f32))
        run_strips(n_x + n_g, n_x + 2 * n_g, lambda s, r0: dc_ref[rows(r0), _strip(s - n_x - n_g)].astype(f32))
        ddtr = ddt_ref[...] * _sigmoid(dtr_ref[...] + dtb_ref[...])
        ddtr_ref[...] = ddtr.astype(bf16)
        ddtb_ref[...] += jnp.sum(ddtr, axis=0, keepdims=True)

    rev = functools.partial(_tile_spec, tm, n_tiles=nt, reverse=True)
    const = lambda shape: pl.BlockSpec(shape, lambda i: (0, 0))
    return pl.pallas_call(
        body, name="ssd_conv_bwd", grid=(nt,),
        in_specs=[rev(width=SSD_CONV_DIM), _halo_spec(tm, SSD_CONV_DIM, nt, True), _resident(conv_w.shape), _resident(conv_b.shape),
                  rev(width=LANES), _resident(dt_bias.shape), rev(width=SSD_INNER), rev(width=SSD_INNER),
                  rev(width=SSD_GROUPS * SSD_STATE), rev(width=SSD_GROUPS * SSD_STATE), rev(width=LANES)],
        out_specs=[rev(width=SSD_CONV_DIM), rev(width=LANES), const((8, SSD_CONV_DIM)), const((1, SSD_CONV_DIM)), const((1, LANES))],
        out_shape=[jax.ShapeDtypeStruct((T, SSD_CONV_DIM), bf16), jax.ShapeDtypeStruct((T, LANES), bf16),
                   jax.ShapeDtypeStruct((8, SSD_CONV_DIM), f32), jax.ShapeDtypeStruct((1, SSD_CONV_DIM), f32), jax.ShapeDtypeStruct((1, LANES), f32)],
        scratch_shapes=[pltpu.VMEM((tm + HALO, LANES), f32), pltpu.VMEM((tm + 8, LANES), f32), pltpu.VMEM((8, SSD_CONV_DIM), f32)],
        compiler_params=_params(("arbitrary",)),
    )(xbc, xbc, conv_w, conv_b, dt_raw, dt_bias, dxs_a, dxs_b, db, dc, ddt)


def _ssd_chunk(xs, bm, cm, dt, alog, st):
    L = SSD_CHUNK
    row = lax.broadcasted_iota(jnp.int32, (L, L), 0)
    col = lax.broadcasted_iota(jnp.int32, (L, L), 1)
    causal = row >= col
    tril = jnp.where(causal, 1.0, 0.0).astype(f32)
    lane = lax.broadcasted_iota(jnp.int32, (1, LANES), 1)
    sub = lax.broadcasted_iota(jnp.int32, (LANES, 1), 0)
    lo = lane < SSD_HEAD_DIM
    last_row = sub == L - 1

    dta = dt * (-jnp.exp(alog))
    a_cs = jnp.dot(tril, dta, precision=lax.Precision.HIGHEST, preferred_element_type=f32)
    a_cs_t = a_cs.T
    bmb = bm.astype(bf16)
    cmb = cm.astype(bf16)
    cb = _mm_nt(cmb, bmb)
    c_st = _mm(cmb, st.astype(bf16))

    def head_col(v, e):
        return jnp.sum(jnp.where(lane == e, v, 0.0), axis=1, keepdims=True)

    def head_row(v, e):
        return jnp.sum(jnp.where(sub == e, v, 0.0), axis=0, keepdims=True)

    ys, sts = [], []
    for j in range(4):
        e0, e1 = 2 * j, 2 * j + 1
        c0, c1 = head_col(a_cs, e0), head_col(a_cs, e1)
        acs_x = jnp.where(lo, c0, c1)
        dt_x = jnp.where(lo, head_col(dt, e0), head_col(dt, e1))
        xd = xs[:, j * LANES:(j + 1) * LANES] * dt_x
        m0 = cb * jnp.exp(jnp.where(causal, c0 - head_row(a_cs_t, e0), NEG_BIG))
        m1 = cb * jnp.exp(jnp.where(causal, c1 - head_row(a_cs_t, e1), NEG_BIG))
        mcat = jnp.concatenate([m0, m1], axis=1).astype(bf16)
        xcat = jnp.concatenate([jnp.where(lo, xd, 0.0), jnp.where(lo, 0.0, xd)], axis=0).astype(bf16)
        y_diag = _mm(mcat, xcat)
        a_last = jnp.sum(jnp.where(last_row, acs_x, 0.0), axis=0, keepdims=True)
        x_dec = (xd * jnp.exp(a_last - acs_x)).astype(bf16)
        s_new = _mm_tn(bmb, x_dec)
        y_off = c_st[:, j * LANES:(j + 1) * LANES] * jnp.exp(acs_x)
        ys.append(y_diag + y_off)
        sts.append(jnp.exp(a_last) * st[:, j * LANES:(j + 1) * LANES] + s_new)
    return jnp.concatenate(ys, axis=1), jnp.concatenate(sts, axis=1)


SCAN_GROUPS = 4


def _scan_specs(nc, reverse):
    L = SSD_CHUNK
    gs = SCAN_GROUPS
    ch = (lambda c: nc - 1 - c) if reverse else (lambda c: c)
    gw = SSD_INNER // SSD_GROUPS
    b0 = SSD_INNER // (gs * SSD_STATE)
    c0 = (SSD_INNER + SSD_GROUPS * SSD_STATE) // (gs * SSD_STATE)
    xs = pl.BlockSpec((L, gs * gw), lambda g, c: (ch(c), g))
    bm = pl.BlockSpec((L, gs * SSD_STATE), lambda g, c: (ch(c), b0 + g))
    cm = pl.BlockSpec((L, gs * SSD_STATE), lambda g, c: (ch(c), c0 + g))
    dt = pl.BlockSpec((gs, L, LANES), lambda g, c: (g, ch(c), 0))
    alog = pl.BlockSpec((gs, 1, LANES), lambda g, c: (g, 0, 0))
    st = pl.BlockSpec((gs, None, SSD_STATE, gw), lambda g, c: (g, ch(c), 0, 0))
    y = pl.BlockSpec((L, gs * gw), lambda g, c: (ch(c), g))
    grp = pl.BlockSpec((L, gs * SSD_STATE), lambda g, c: (ch(c), g))
    return xs, bm, cm, dt, alog, st, y, grp


def ssd_scan_fwd(act, dt4, alog4):
    T = act.shape[0]
    nc = T // SSD_CHUNK
    gw = SSD_INNER // SSD_GROUPS
    xs_s, bm_s, cm_s, dt_s, alog_s, st_s, y_s, _ = _scan_specs(nc, False)

    def body(xs_ref, bm_ref, cm_ref, dt_ref, alog_ref, y_ref, st_ref, st_scr):
        @pl.when(pl.program_id(1) == 0)
        def _():
            st_scr[...] = jnp.zeros_like(st_scr)

        for q in range(SCAN_GROUPS):
            xc, gc = pl.ds(q * gw, gw), pl.ds(q * SSD_STATE, SSD_STATE)
            st = st_scr[q]
            st_ref[q] = st
            y, st_new = _ssd_chunk(xs_ref[:, xc].astype(f32), bm_ref[:, gc].astype(f32), cm_ref[:, gc].astype(f32), dt_ref[q], alog_ref[q], st)
            y_ref[:, xc] = y.astype(bf16)
            st_scr[q] = st_new

    return pl.pallas_call(
        body, name="ssd_scan_fwd", grid=(SSD_GROUPS // SCAN_GROUPS, nc),
        in_specs=[xs_s, bm_s, cm_s, dt_s, alog_s], out_specs=[y_s, st_s],
        out_shape=[jax.ShapeDtypeStruct((T, SSD_INNER), bf16), jax.ShapeDtypeStruct((SSD_GROUPS, nc, SSD_STATE, gw), f32)],
        scratch_shapes=[pltpu.VMEM((SCAN_GROUPS, SSD_STATE, gw), f32)],
        compiler_params=_params(("parallel", "arbitrary")),
    )(act, act, act, dt4, alog4)


def ssd_scan_bwd(act, dt4, alog4, states, dy):
    T = act.shape[0]
    nc = T // SSD_CHUNK
    gw = SSD_INNER // SSD_GROUPS
    xs_s, bm_s, cm_s, dt_s, alog_s, st_s, y_s, grp_s = _scan_specs(nc, True)

    def body(xs_ref, bm_ref, cm_ref, dt_ref, alog_ref, st_ref, dy_ref, dxs_ref, db_ref, dc_ref, ddt_ref, dalog_ref, dst_scr):
        @pl.when(pl.program_id(1) == 0)
        def _():
            dst_scr[...] = jnp.zeros_like(dst_scr)
            dalog_ref[...] = jnp.zeros_like(dalog_ref)

        for q in range(SCAN_GROUPS):
            xc, gc = pl.ds(q * gw, gw), pl.ds(q * SSD_STATE, SSD_STATE)
            _, vjp = jax.vjp(_ssd_chunk, xs_ref[:, xc].astype(f32), bm_ref[:, gc].astype(f32), cm_ref[:, gc].astype(f32),
                             dt_ref[q], alog_ref[q], st_ref[q])
            dxs, dbm, dcm, ddt, dalog, dst = vjp((dy_ref[:, xc].astype(f32), dst_scr[q]))
            dxs_ref[:, xc] = dxs.astype(bf16)
            db_ref[:, gc] = dbm.astype(bf16)
            dc_ref[:, gc] = dcm.astype(bf16)
            ddt_ref[q] = ddt
            dalog_ref[q] += dalog
            dst_scr[q] = dst

    return pl.pallas_call(
        body, name="ssd_scan_bwd", grid=(SSD_GROUPS // SCAN_GROUPS, nc),
        in_specs=[xs_s, bm_s, cm_s, dt_s, alog_s, st_s, y_s],
        out_specs=[y_s, grp_s, grp_s, dt_s, alog_s],
        out_shape=[jax.ShapeDtypeStruct((T, SSD_INNER), bf16), jax.ShapeDtypeStruct((T, SSD_GROUPS * SSD_STATE), bf16),
                   jax.ShapeDtypeStruct((T, SSD_GROUPS * SSD_STATE), bf16), jax.ShapeDtypeStruct((SSD_GROUPS, T, LANES), f32),
                   jax.ShapeDtypeStruct((SSD_GROUPS, 1, LANES), f32)],
        scratch_shapes=[pltpu.VMEM((SCAN_GROUPS, SSD_STATE, gw), f32)],
        compiler_params=_params(("parallel", "arbitrary")),
    )(act, act, act, dt4, alog4, states, dy)


GATE_ROWS = 16


def _ssd_gate(y, xs, z, d_x, nw):
    g = (y + xs * d_x) * (z * _sigmoid(z))
    return g * lax.rsqrt(jnp.mean(g * g, axis=-1, keepdims=True) + RMS_EPS) * nw


def _gate_blocks(tm, fn):
    gw = SSD_INNER // SSD_GROUPS

    def block(r, carry):
        rows = pl.ds(pl.multiple_of(r * GATE_ROWS, GATE_ROWS), GATE_ROWS)
        for k in range(SSD_GROUPS):
            fn(rows, pl.ds(k * gw, gw))
        return carry

    lax.fori_loop(0, tm // GATE_ROWS, block, 0)


def ssd_gate_fwd(y, act, z, d_x, nw, tm=256):
    T = y.shape[0]

    def body(y_ref, xs_ref, z_ref, d_ref, nw_ref, o_ref):
        def one(rows, cols):
            o_ref[rows, cols] = _ssd_gate(y_ref[rows, cols].astype(f32), xs_ref[rows, cols].astype(f32), z_ref[rows, cols].astype(f32),
                                          d_ref[:, cols], nw_ref[:, cols]).astype(bf16)

        _gate_blocks(tm, one)

    return pl.pallas_call(
        body, name="ssd_gate_fwd", grid=(T // tm,),
        in_specs=[_rows(tm, SSD_INNER), _rows(tm, SSD_INNER), _rows(tm, SSD_INNER), _resident(d_x.shape), _resident(nw.shape)],
        out_specs=_rows(tm, SSD_INNER), out_shape=jax.ShapeDtypeStruct((T, SSD_INNER), bf16),
        compiler_params=_params(("parallel",)),
    )(y, act, z, d_x, nw)


def ssd_gate_bwd(y, act, z, d_x, nw, dgn, tm=256):
    T = y.shape[0]

    def body(y_ref, xs_ref, z_ref, d_ref, nw_ref, dgn_ref, dy_ref, dxs_ref, dz_ref, dd_ref, dnw_ref):
        @pl.when(pl.program_id(0) == 0)
        def _():
            dd_ref[...] = jnp.zeros_like(dd_ref)
            dnw_ref[...] = jnp.zeros_like(dnw_ref)

        def one(rows, cols):
            _, vjp = jax.vjp(_ssd_gate, y_ref[rows, cols].astype(f32), xs_ref[rows, cols].astype(f32), z_ref[rows, cols].astype(f32),
                             d_ref[:, cols], nw_ref[:, cols])
            dy, dxs, dz, dd, dnw = vjp(dgn_ref[rows, cols].astype(f32))
            dy_ref[rows, cols] = dy.astype(bf16)
            dxs_ref[rows, cols] = dxs.astype(bf16)
            dz_ref[rows, cols] = dz.astype(bf16)
            dd_ref[:, cols] += dd
            dnw_ref[:, cols] += dnw

        _gate_blocks(tm, one)

    const = pl.BlockSpec((1, SSD_INNER), lambda i: (0, 0))
    return pl.pallas_call(
        body, name="ssd_gate_bwd", grid=(T // tm,),
        in_specs=[_rows(tm, SSD_INNER), _rows(tm, SSD_INNER), _rows(tm, SSD_INNER), _resident(d_x.shape), _resident(nw.shape), _rows(tm, SSD_INNER)],
        out_specs=[_rows(tm, SSD_INNER)] * 3 + [const, const],
        out_shape=[jax.ShapeDtypeStruct((T, SSD_INNER), bf16)] * 3 + [jax.ShapeDtypeStruct((1, SSD_INNER), f32)] * 2,
        compiler_params=_params(("arbitrary",)),
    )(y, act, z, d_x, nw, dgn)


def sc_mid_fwd(bcu, conv_w, tm=256):
    T = bcu.shape[0]
    nt = T // tm
    Dm = D_MODEL

    def body(x_ref, halo_ref, w_ref, q_ref, ext_ref):
        first = pl.program_id(0) == 0
        n_s = Dm // LANES

        def strip(s, carry):
            cols, c_cols, u_cols = _strip(s), _strip(s + n_s), _strip(s + 2 * n_s)
            ext_ref[pl.ds(0, HALO), :] = jnp.where(first, 0.0, halo_ref[:, c_cols].astype(f32) * halo_ref[:, u_cols].astype(f32))
            ext_ref[pl.ds(HALO, tm), :] = x_ref[:, c_cols].astype(f32) * x_ref[:, u_cols].astype(f32)
            for r0 in range(0, tm, ROW_BLOCK):
                rows = pl.ds(r0, ROW_BLOCK)
                v, _ = _conv_rows(ext_ref, w_ref, cols, SC_CONV_K, r0)
                q_ref[rows, cols] = (x_ref[rows, cols].astype(f32) * v).astype(bf16)
            return carry

        lax.fori_loop(0, n_s, strip, 0)

    return pl.pallas_call(
        body, name="sc_mid_fwd", grid=(nt,),
        in_specs=[_rows(tm, 3 * Dm), _halo_spec(tm, 3 * Dm, nt, False), _resident(conv_w.shape)],
        out_specs=_rows(tm, Dm), out_shape=jax.ShapeDtypeStruct((T, Dm), bf16),
        scratch_shapes=[pltpu.VMEM((tm + HALO, LANES), f32)],
        compiler_params=_params(("parallel",)),
    )(bcu, bcu, conv_w)


def sc_mid_bwd(bcu, conv_w, dq, tm=256):
    T = bcu.shape[0]
    nt = T // tm
    Dm = D_MODEL
    K = SC_CONV_K

    def body(x_ref, halo_ref, w_ref, dq_ref, dx_ref, dw_ref, ext_ref, dv_ref, carry_ref):
        i = pl.program_id(0)

        @pl.when(i == 0)
        def _():
            carry_ref[...] = jnp.zeros_like(carry_ref)
            dw_ref[...] = jnp.zeros_like(dw_ref)

        first_tile = i == nt - 1
        n_s = Dm // LANES

        def strip(s, carry):
            cols, c_cols, u_cols = _strip(s), _strip(s + n_s), _strip(s + 2 * n_s)
            ext_ref[pl.ds(0, HALO), :] = jnp.where(first_tile, 0.0, halo_ref[:, c_cols].astype(f32) * halo_ref[:, u_cols].astype(f32))
            ext_ref[pl.ds(HALO, tm), :] = x_ref[:, c_cols].astype(f32) * x_ref[:, u_cols].astype(f32)
            dv_ref[pl.ds(tm, 8), :] = carry_ref[:, cols]
            dws = [jnp.zeros((1, LANES), f32) for _ in range(K)]
            for r0 in range(0, tm, ROW_BLOCK):
                rows = pl.ds(r0, ROW_BLOCK)
                v, wins = _conv_rows(ext_ref, w_ref, cols, K, r0)
                dqv = dq_ref[rows, cols].astype(f32)
                dv = dqv * x_ref[rows, cols].astype(f32)
                dv_ref[rows, :] = dv
                dx_ref[rows, cols] = (dqv * v).astype(bf16)
                for k in range(K):
                    dws[k] = dws[k] + jnp.sum(dv * wins[k], axis=0, keepdims=True)
            carry_ref[:, cols] = dv_ref[pl.ds(0, 8), :]
            for r0 in range(0, tm, ROW_BLOCK):
                rows = pl.ds(r0, ROW_BLOCK)
                dp = _shifted_back(dv_ref, w_ref, cols, K, r0)
                dx_ref[rows, c_cols] = (dp * x_ref[rows, u_cols].astype(f32)).astype(bf16)
                dx_ref[rows, u_cols] = (dp * x_ref[rows, c_cols].astype(f32)).astype(bf16)
            for k in range(K):
                dw_ref[pl.ds(k, 1), cols] += dws[k]
            return carry

        lax.fori_loop(0, n_s, strip, 0)

    return pl.pallas_call(
        body, name="sc_mid_bwd", grid=(nt,),
        in_specs=[_tile_spec(tm, 3 * Dm, nt, True), _halo_spec(tm, 3 * Dm, nt, True), _resident(conv_w.shape), _tile_spec(tm, Dm, nt, True)],
        out_specs=[_tile_spec(tm, 3 * Dm, nt, True), pl.BlockSpec((8, Dm), lambda i: (0, 0))],
        out_shape=[jax.ShapeDtypeStruct((T, 3 * Dm), bf16), jax.ShapeDtypeStruct((8, Dm), f32)],
        scratch_shapes=[pltpu.VMEM((tm + HALO, LANES), f32), pltpu.VMEM((tm + 8, LANES), f32), pltpu.VMEM((8, Dm), f32)],
        compiler_params=_params(("arbitrary",)),
    )(bcu, bcu, conv_w, dq)


def loss_head(x, fw, target, tm=512):
    T = x.shape[0]

    def body(x_ref, fw_ref, t_ref, loss_ref, dx_ref, dfw_ref):
        @pl.when(pl.program_id(0) == 0)
        def _():
            loss_ref[...] = jnp.zeros_like(loss_ref)
            dfw_ref[...] = jnp.zeros_like(dfw_ref)

        w = fw_ref[...]
        y, xh, inv = _rms_fwd(x_ref[...], w)
        err = y - t_ref[...]
        loss_ref[...] += 0.5 * jnp.sum(jnp.mean(err * err, axis=-1, keepdims=True), axis=0, keepdims=True)
        dx, dw = _rms_bwd(err * (1.0 / D_MODEL), xh, inv, w)
        dx_ref[...] = dx
        dfw_ref[...] += dw

    return pl.pallas_call(
        body, name="loss_head", grid=(T // tm,),
        in_specs=[_rows(tm, D_MODEL), _resident((1, D_MODEL)), _rows(tm, D_MODEL)],
        out_specs=[pl.BlockSpec((1, LANES), lambda i: (0, 0)), _rows(tm, D_MODEL), pl.BlockSpec((1, D_MODEL), lambda i: (0, 0))],
        out_shape=[jax.ShapeDtypeStruct((1, LANES), f32), jax.ShapeDtypeStruct((T, D_MODEL), f32), jax.ShapeDtypeStruct((1, D_MODEL), f32)],
        compiler_params=_params(("arbitrary",)),
    )(x, fw, target)


def _row_tile(rows):
    return rows if rows <= 512 else 256


def adamw(g_parts, w, m, v, name="adamw"):
    A, B, n = w.shape
    tb = _row_tile(B)
    n_parts = len(g_parts)
    arrays, specs = [], []
    for part in g_parts:
        lead, arr = part if isinstance(part, tuple) else ((), part)
        specs.append(pl.BlockSpec((None,) * (len(lead) + 1) + (tb, n), lambda a, t, lead=lead: tuple(lead) + (a, t, 0)))
        arrays.append(arr)

    def body(*refs):
        n = n_parts
        g_refs = refs[:n]
        w_ref, m_ref, v_ref, go_ref, d_ref, mo_ref, vo_ref = refs[n:]
        g = g_refs[0][...].astype(f32)
        for r in g_refs[1:]:
            g = g + r[...].astype(f32)
        m_new = ADAM_B1 * m_ref[...] + (1.0 - ADAM_B1) * g
        v_new = ADAM_B2 * v_ref[...] + (1.0 - ADAM_B2) * (g * g)
        m_hat = m_new / (1.0 - ADAM_B1 ** ADAM_STEP)
        v_hat = v_new / (1.0 - ADAM_B2 ** ADAM_STEP)
        go_ref[...] = g
        d_ref[...] = -ADAM_LR * (m_hat / (jnp.sqrt(v_hat) + ADAM_EPS) + ADAM_WD * w_ref[...])
        mo_ref[...] = m_new
        vo_ref[...] = v_new

    plain = pl.BlockSpec((None, tb, n), lambda a, t: (a, t, 0))
    return pl.pallas_call(
        body, name=name, grid=(A, B // tb), in_specs=specs + [plain] * 3, out_specs=[plain] * 4,
        out_shape=[jax.ShapeDtypeStruct((A, B, n), f32)] * 4,
        compiler_params=_params(("parallel", "parallel")),
    )(*arrays, w, m, v)


def pair_sum_bf16(ga, gb, name):
    _, A, B, n = gb.shape
    tb = _row_tile(B)

    def body(a_ref, b_ref, o_ref):
        o_ref[...] = (a_ref[...] + b_ref[...]).astype(bf16)

    return pl.pallas_call(
        body, name=name, grid=(3, A, B // tb),
        in_specs=[pl.BlockSpec((None, None, None, tb, n), lambda j, a, t: (0, j + 1, a, t, 0)),
                  pl.BlockSpec((None, None, tb, n), lambda j, a, t: (j + 1, a, t, 0))],
        out_specs=pl.BlockSpec((None, None, tb, n), lambda j, a, t: (j + 1, a, t, 0)),
        out_shape=jax.ShapeDtypeStruct((4, A, B, n), bf16),
        compiler_params=_params(("parallel", "parallel", "parallel")),
    )(ga, gb)


def assemble(gathered, axis, tk=256):
    _, A, K, n = gathered.shape
    if axis == 1:
        def body(w_ref, o_ref):
            o_ref[...] = jnp.concatenate([w_ref[j] for j in range(N_DEV)], axis=1)

        return pl.pallas_call(
            body, name=f"assemble_cols_{K}x{n}", grid=(A, K // tk),
            in_specs=[pl.BlockSpec((N_DEV, None, tk, n), lambda a, t: (0, a, t, 0))],
            out_specs=pl.BlockSpec((None, tk, N_DEV * n), lambda a, t: (a, t, 0)),
            out_shape=jax.ShapeDtypeStruct((A, K, N_DEV * n), gathered.dtype),
            compiler_params=_params(("parallel", "parallel")),
        )(gathered)

    def body(w_ref, o_ref):
        for j in range(N_DEV):
            o_ref[pl.ds(j * K, K), :] = w_ref[j]

    return pl.pallas_call(
        body, name=f"assemble_rows_{K}x{n}", grid=(A,),
        in_specs=[pl.BlockSpec((N_DEV, None, K, n), lambda a: (0, a, 0, 0))],
        out_specs=pl.BlockSpec((None, N_DEV * K, n), lambda a: (a, 0, 0)),
        out_shape=jax.ShapeDtypeStruct((A, N_DEV * K, n), gathered.dtype),
        compiler_params=_params(("parallel",)),
    )(gathered)


SSD_IN_PAD = 5248


def assemble_ssd_in(gathered, tk=256):
    _, A, K, n = gathered.shape

    def body(w_ref, z_ref, x_ref, dt_ref, full_ref):
        full_ref[:, pl.ds(SSD_IN_PAD - LANES, LANES)] = jnp.zeros((tk, LANES), gathered.dtype)
        for j in range(N_DEV):
            full_ref[:, pl.ds(j * n, n)] = w_ref[j]
        z_ref[...] = full_ref[:, pl.ds(0, SSD_INNER)]
        x_ref[...] = full_ref[:, pl.ds(SSD_INNER, SSD_CONV_DIM)]
        dt_ref[...] = full_ref[:, pl.ds(SSD_INNER + SSD_CONV_DIM, LANES)]

    widths = (SSD_INNER, SSD_CONV_DIM, LANES)
    return pl.pallas_call(
        body, name="assemble_ssd_in", grid=(A, K // tk),
        in_specs=[pl.BlockSpec((N_DEV, None, tk, n), lambda a, t: (0, a, t, 0))],
        out_specs=[pl.BlockSpec((None, tk, w), lambda a, t: (a, t, 0)) for w in widths],
        out_shape=[jax.ShapeDtypeStruct((A, K, w), gathered.dtype) for w in widths],
        scratch_shapes=[pltpu.VMEM((tk, SSD_IN_PAD), gathered.dtype)],
        compiler_params=_params(("parallel", "parallel")),
    )(gathered)


def ssd_in_to_shards(dwz, dwx, dwdt, buf, j, tk=256):
    K = dwz.shape[0]
    n = buf.shape[-1]

    def body(z_ref, x_ref, dt_ref, buf_ref, o_ref, full_ref):
        full_ref[:, pl.ds(0, SSD_INNER)] = z_ref[...]
        full_ref[:, pl.ds(SSD_INNER, SSD_CONV_DIM)] = x_ref[...]
        full_ref[:, pl.ds(SSD_INNER + SSD_CONV_DIM, LANES)] = dt_ref[...]
        my_c, my_chip = _my_core_and_chip()
        for d in range(N_DEV):
            o_ref[(d % 2) ^ my_c, (d // 2) ^ my_chip] = full_ref[:, pl.ds(d * n, n)]

    return pl.pallas_call(
        body, name="ssd_in_to_shards", grid=(K // tk,),
        in_specs=[_rows(tk, SSD_INNER), _rows(tk, SSD_CONV_DIM), _rows(tk, LANES), _ANY],
        out_specs=pl.BlockSpec((2, 4, None, tk, n), lambda t: (0, 0, j, t, 0)),
        out_shape=jax.ShapeDtypeStruct(buf.shape, f32),
        scratch_shapes=[pltpu.VMEM((tk, SSD_IN_PAD), f32)],
        input_output_aliases={3: 0},
        compiler_params=_params(("parallel",)),
    )(dwz, dwx, dwdt, buf)


def sum_over_devices(gathered):
    _, R, W = gathered.shape

    def body(g_ref, o_ref):
        acc = g_ref[0]
        for k in range(1, N_DEV):
            acc = acc + g_ref[k]
        o_ref[...] = acc

    return pl.pallas_call(
        body, name="sum_over_devices", grid=(1,),
        in_specs=[pl.BlockSpec((N_DEV, R, W), lambda i: (0, 0, 0))], out_specs=pl.BlockSpec((R, W), lambda i: (0, 0)),
        out_shape=jax.ShapeDtypeStruct((R, W), f32), compiler_params=_params(("arbitrary",)),
    )(gathered)


_ANY = pl.BlockSpec(memory_space=pl.ANY)


def all_gather(blocks, name):
    n = len(blocks)

    def body(*refs):
        x_refs, out_refs = refs[:n], refs[n:2 * n]
        send_sems, recv_sems, local_sems = refs[2 * n:]
        x, y, c = lax.axis_index("x"), lax.axis_index("y"), lax.axis_index("c")
        me, sibling = (x, y, c), (x, y, 1 - c)
        chips = [(1 - x, y), (x, 1 - y), (1 - x, 1 - y)]

        def copy(a, k, blk, to, src=None):
            px, py, pc = blk
            slot = out_refs[a].at[4 * px + 2 * py + pc]
            return pltpu.make_async_remote_copy(
                src_ref=slot if src is None else src, dst_ref=slot,
                send_sem=send_sems.at[7 * a + k], recv_sem=recv_sems.at[7 * a + k], device_id=to, device_id_type=MESH)

        mine, first, passed = [], [], []
        for a in range(n):
            mine.append(pltpu.make_async_copy(x_refs[a], out_refs[a].at[4 * x + 2 * y + c], local_sems.at[a]))
            mine[-1].start()
            first += [copy(a, 0, me, sibling, src=x_refs[a])] + [copy(a, 1 + j, me, (*chip, c), src=x_refs[a]) for j, chip in enumerate(chips)]
        for cp in first:
            cp.start()
        for j, chip in enumerate(chips):
            for a in range(n):
                copy(a, 1 + j, (*chip, c), me).wait_recv()
                passed.append(copy(a, 4 + j, (*chip, c), sibling))
                passed[-1].start()
        for a in range(n):
            copy(a, 0, sibling, me).wait_recv()
            for j, chip in enumerate(chips):
                copy(a, 4 + j, (*chip, 1 - c), me).wait_recv()
        for cp in first + passed:
            cp.wait_send()
        for cp in mine:
            cp.wait()

    return pl.pallas_call(
        body, name=name, in_specs=[_ANY] * n, out_specs=[_ANY] * n,
        out_shape=[jax.ShapeDtypeStruct((N_DEV,) + b.shape, b.dtype) for b in blocks],
        scratch_shapes=[pltpu.SemaphoreType.DMA((7 * n,)), pltpu.SemaphoreType.DMA((7 * n,)), pltpu.SemaphoreType.DMA((n,))],
    )(*blocks)


def exchange_with_sibling(gs):
    n = len(gs)

    def body(*refs):
        g_refs, recv_refs = refs[:n], refs[n:2 * n]
        send_sems, recv_sems = refs[2 * n:]
        x, y, c = lax.axis_index("x"), lax.axis_index("y"), lax.axis_index("c")
        cps = [pltpu.make_async_remote_copy(src_ref=g_refs[a].at[1], dst_ref=recv_refs[a], send_sem=send_sems.at[a],
                                            recv_sem=recv_sems.at[a], device_id=(x, y, 1 - c), device_id_type=MESH) for a in range(n)]
        for cp in cps:
            cp.start()
        for cp in cps:
            cp.wait()

    return pl.pallas_call(
        body, name="exchange_with_sibling", in_specs=[_ANY] * n, out_specs=[_ANY] * n,
        out_shape=[jax.ShapeDtypeStruct(g.shape[1:], g.dtype) for g in gs],
        scratch_shapes=[pltpu.SemaphoreType.DMA((n,)), pltpu.SemaphoreType.DMA((n,))],
    )(*gs)


def exchange_between_chips(parts):
    n = len(parts)

    def body(*refs):
        p_refs, recv_refs = refs[:n], refs[n:2 * n]
        send_sems, recv_sems = refs[2 * n:]
        x, y, c = lax.axis_index("x"), lax.axis_index("y"), lax.axis_index("c")
        chips = [(2, (1 - x, y)), (1, (x, 1 - y)), (3, (1 - x, 1 - y))]
        cps = [pltpu.make_async_remote_copy(src_ref=p_refs[a].at[slot], dst_ref=recv_refs[a].at[k], send_sem=send_sems.at[3 * a + k],
                                            recv_sem=recv_sems.at[3 * a + k], device_id=(px, py, c), device_id_type=MESH)
               for a in range(n) for k, (slot, (px, py)) in enumerate(chips)]
        for cp in cps:
            cp.start()
        for cp in cps:
            cp.wait()

    return pl.pallas_call(
        body, name="exchange_between_chips", in_specs=[_ANY] * n, out_specs=[_ANY] * n,
        out_shape=[jax.ShapeDtypeStruct((3,) + p.shape[1:], p.dtype) for p in parts],
        scratch_shapes=[pltpu.SemaphoreType.DMA((3 * n,)), pltpu.SemaphoreType.DMA((3 * n,))],
    )(*parts)


PARAMS = {
    "norm_w": ((DEPTH, 3, D_MODEL), 2),
    "ffn_w_gate": ((DEPTH, 2, D_MODEL, D_FF), 3),
    "ffn_w_up": ((DEPTH, 2, D_MODEL, D_FF), 3),
    "ffn_w_down": ((DEPTH, 2, D_FF, D_MODEL), 2),
    "ssd_w_in": ((2, D_MODEL, SSD_IN_DIM), 2),
    "ssd_conv_w": ((2, SSD_CONV_K, SSD_CONV_DIM), 2),
    "ssd_conv_b": ((2, SSD_CONV_DIM), None),
    "ssd_dt_bias": ((2, SSD_HEADS), None),
    "ssd_a_log": ((2, SSD_HEADS), None),
    "ssd_d": ((2, SSD_HEADS), None),
    "ssd_norm_w": ((2, SSD_INNER), None),
    "ssd_w_out": ((2, SSD_INNER, D_MODEL), 1),
    "sc_w_in": ((2, D_MODEL, 3 * D_MODEL), 2),
    "sc_conv_w": ((2, SC_CONV_K, D_MODEL), 2),
    "sc_w_out": ((2, D_MODEL, D_MODEL), 1),
    "final_norm_w": ((D_MODEL,), None),
}
NAMES = list(PARAMS)
BIG = ["ffn_w_gate", "ffn_w_up", "ffn_w_down", "ssd_w_in", "ssd_w_out", "sc_w_in", "sc_w_out"]
SMALL = [n for n in NAMES if n not in BIG]
SMALL_SHARDED = [n for n in SMALL if PARAMS[n][1] is not None]


def _round_up(n, m):
    return -(-n // m) * m


def _pack(flat_list, rows_multiple):
    flat = jnp.concatenate(flat_list)
    rows = _round_up(_round_up(flat.shape[0], PACK_W) // PACK_W, rows_multiple)
    return jnp.pad(flat, (0, rows * PACK_W - flat.shape[0])).reshape(rows, PACK_W)


def _unpack(packed, shapes, lead=()):
    flat = packed.reshape(lead + (-1,))
    out, off = [], 0
    for shp in shapes:
        n = 1
        for s in shp:
            n *= s
        out.append(flat[..., off:off + n].reshape(lead + tuple(shp)))
        off += n
    return out


def _local_shape(name):
    shp, ax = PARAMS[name]
    if ax is None:
        return shp
    return shp[:ax] + (shp[ax] // N_DEV,) + shp[ax + 1:]


def _full_from_gathered(g, name):
    shp, ax = PARAMS[name]
    return jnp.moveaxis(g, 0, ax).reshape(shp)


def _by_destination(full, name):
    shp, ax = PARAMS[name]
    loc = shp[ax] // N_DEV
    return jnp.moveaxis(full.reshape(shp[:ax] + (N_DEV, loc) + shp[ax + 1:]), ax, 0)


def _ssd_layer_fwd(xin, nw, p, j):
    z, xbc, dt_raw = in_proj_fwd(xin, nw, [p["ssd_wz"][j], p["ssd_wx"][j], p["ssd_wdt"][j]], [bf16, bf16, f32])
    act, dt = ssd_conv_fwd(xbc, p["ssd_conv_w"][j], p["ssd_conv_b"][j], dt_raw, p["ssd_dt_bias"][j])
    T = xin.shape[0]
    dt4 = jnp.pad(dt[:, :SSD_HEADS].reshape(T, SSD_GROUPS, 8).transpose(1, 0, 2), ((0, 0), (0, 0), (0, LANES - 8)))
    y, states = ssd_scan_fwd(act, dt4, p["ssd_alog4"][j])
    gn = ssd_gate_fwd(y, act, z, p["ssd_dx"][j], p["ssd_norm_w"][j])
    xout = out_proj_fwd(xin, gn, p["ssd_w_out"][j])
    return xout, (xin, z, xbc, dt_raw, act, dt4, y, states, gn)


def _ssd_layer_bwd(dxo, nw, p, j, saved, gbuf):
    xin, z, xbc, dt_raw, act, dt4, y, states, gn = saved
    T = xin.shape[0]
    dgn, dyb = out_proj_bwd(dxo, p["ssd_w_out"][j])
    gbuf["ssd_w_out"] = tn_matmul_to_shards(gn, dyb, gbuf["ssd_w_out"], (j,), 0)
    g = {}
    dy, dxs_skip, dz, dd_x, dgnw = ssd_gate_bwd(y, act, z, p["ssd_dx"][j], p["ssd_norm_w"][j], dgn)
    g["ssd_norm_w"] = dgnw[0]
    g["ssd_d"] = jnp.sum(dd_x.reshape(SSD_HEADS, SSD_HEAD_DIM), axis=1)
    dxs, db, dc, ddt4, dalog4 = ssd_scan_bwd(act, dt4, p["ssd_alog4"][j], states, dy)
    g["ssd_a_log"] = dalog4[:, 0, :8].reshape(SSD_HEADS)
    ddt = jnp.pad(ddt4[:, :, :8].transpose(1, 0, 2).reshape(T, SSD_HEADS), ((0, 0), (0, LANES - SSD_HEADS)))
    dxbc, ddt_raw, dcw, dcb, ddtb = ssd_conv_bwd(xbc, p["ssd_conv_w"][j], p["ssd_conv_b"][j], dt_raw, p["ssd_dt_bias"][j], dxs, dxs_skip, db, dc, ddt)
    g["ssd_conv_w"] = dcw[:SSD_CONV_K]
    g["ssd_conv_b"] = dcb[0]
    g["ssd_dt_bias"] = ddtb[0, :SSD_HEADS]
    dx, h, dnw = in_proj_bwd(xin, nw, dxo, [dz, dxbc, ddt_raw], [p["ssd_wz"][j], p["ssd_wx"][j], p["ssd_wdt"][j]])
    gbuf["ssd_w_in"] = ssd_in_to_shards(tn_matmul(h, dz), tn_matmul(h, dxbc), tn_matmul(h, ddt_raw), gbuf["ssd_w_in"], j)
    return dx, dnw, g


def _sc_layer_fwd(xin, nw, p, j):
    (bcu,) = in_proj_fwd(xin, nw, [p["sc_w_in"][j]], [bf16])
    q = sc_mid_fwd(bcu, p["sc_conv_w"][j])
    return out_proj_fwd(xin, q, p["sc_w_out"][j]), (xin, bcu, q)


def _sc_layer_bwd(dxo, nw, p, j, saved, gbuf):
    xin, bcu, q = saved
    dq, dyb = out_proj_bwd(dxo, p["sc_w_out"][j])
    gbuf["sc_w_out"] = tn_matmul_to_shards(q, dyb, gbuf["sc_w_out"], (j,), 0)
    dbcu, dcw = sc_mid_bwd(bcu, p["sc_conv_w"][j], dq)
    g = {"sc_conv_w": dcw[:SC_CONV_K]}
    dx, h, dnw = in_proj_bwd(xin, nw, dxo, [dbcu], [p["sc_w_in"][j]])
    gbuf["sc_w_in"] = tn_matmul_to_shards(h, dbcu, gbuf["sc_w_in"], (j,), 1)
    return dx, dnw, g


def kernel(x, norm_w, ffn_w_gate, ffn_w_up, ffn_w_down, ssd_w_in, ssd_conv_w, ssd_conv_b, ssd_dt_bias, ssd_a_log, ssd_d, ssd_norm_w, ssd_w_out, sc_w_in, sc_conv_w, sc_w_out, final_norm_w, loss_target, m_norm_w, m_ffn_w_gate, m_ffn_w_up, m_ffn_w_down, m_ssd_w_in, m_ssd_conv_w, m_ssd_conv_b, m_ssd_dt_bias, m_ssd_a_log, m_ssd_d, m_ssd_norm_w, m_ssd_w_out, m_sc_w_in, m_sc_conv_w, m_sc_w_out, m_final_norm_w, v_norm_w, v_ffn_w_gate, v_ffn_w_up, v_ffn_w_down, v_ssd_w_in, v_ssd_conv_w, v_ssd_conv_b, v_ssd_dt_bias, v_ssd_a_log, v_ssd_d, v_ssd_norm_w, v_ssd_w_out, v_sc_w_in, v_sc_conv_w, v_sc_w_out, v_final_norm_w):
    w_loc = dict(zip(NAMES, (norm_w, ffn_w_gate, ffn_w_up, ffn_w_down, ssd_w_in, ssd_conv_w, ssd_conv_b, ssd_dt_bias, ssd_a_log, ssd_d, ssd_norm_w, ssd_w_out, sc_w_in, sc_conv_w, sc_w_out, final_norm_w)))
    m_loc = dict(zip(NAMES, (m_norm_w, m_ffn_w_gate, m_ffn_w_up, m_ffn_w_down, m_ssd_w_in, m_ssd_conv_w, m_ssd_conv_b, m_ssd_dt_bias, m_ssd_a_log, m_ssd_d, m_ssd_norm_w, m_ssd_w_out, m_sc_w_in, m_sc_conv_w, m_sc_w_out, m_final_norm_w)))
    v_loc = dict(zip(NAMES, (v_norm_w, v_ffn_w_gate, v_ffn_w_up, v_ffn_w_down, v_ssd_w_in, v_ssd_conv_w, v_ssd_conv_b, v_ssd_dt_bias, v_ssd_a_log, v_ssd_d, v_ssd_norm_w, v_ssd_w_out, v_sc_w_in, v_sc_conv_w, v_sc_w_out, v_final_norm_w)))
    ax, ay, ac = lax.axis_index("x"), lax.axis_index("y"), lax.axis_index("c")
    my_chip = 2 * ax + ay
    my_dev = 4 * ax + 2 * ay + ac
    T = x.shape[1]

    def as3d(a):
        return a.reshape((-1,) + a.shape[-2:])

    ss_shapes = [_local_shape(n) for n in SMALL_SHARDED]
    gathered = all_gather([as3d(w_loc[n]).astype(bf16) for n in BIG] + [_pack([w_loc[n].reshape(-1) for n in SMALL_SHARDED], 8)],
                          "all_gather_weights")
    big_all = dict(zip(BIG, gathered[:-1]))
    full = {}
    for n in BIG:
        if n != "ssd_w_in":
            full[n] = assemble(big_all[n], 1 if PARAMS[n][1] == len(PARAMS[n][0]) - 1 else 0).reshape(PARAMS[n][0])
    for n, part in zip(SMALL_SHARDED, _unpack(gathered[-1], ss_shapes, lead=(N_DEV,))):
        full[n] = _full_from_gathered(part, n)
    for n in SMALL:
        if PARAMS[n][1] is None:
            full[n] = w_loc[n]

    p = dict(full)
    p["ssd_wz"], p["ssd_wx"], p["ssd_wdt"] = assemble_ssd_in(big_all["ssd_w_in"])
    p["ssd_conv_b"] = full["ssd_conv_b"].reshape(2, 1, SSD_CONV_DIM)
    p["ssd_dt_bias"] = jnp.pad(full["ssd_dt_bias"], ((0, 0), (0, LANES - SSD_HEADS))).reshape(2, 1, LANES)
    p["ssd_alog4"] = jnp.pad(full["ssd_a_log"].reshape(2, SSD_GROUPS, 1, 8), ((0, 0), (0, 0), (0, 0), (0, LANES - 8)))
    p["ssd_dx"] = jnp.repeat(full["ssd_d"], SSD_HEAD_DIM, axis=1).reshape(2, 1, SSD_INNER)
    p["ssd_norm_w"] = full["ssd_norm_w"].reshape(2, 1, SSD_INNER)
    nw_all = full["norm_w"].reshape(DEPTH, 3, 1, D_MODEL)

    xc = x[0]
    saved = []
    for i in range(DEPTH):
        j = i // 2
        x1, g1, u1, a1 = ffn_fwd(xc, nw_all[i, 0], full["ffn_w_gate"], full["ffn_w_up"], full["ffn_w_down"], (i, 0))
        if i % 2 == 0:
            x2, mix_saved = _ssd_layer_fwd(x1, nw_all[i, 1], p, j)
        else:
            x2, mix_saved = _sc_layer_fwd(x1, nw_all[i, 1], p, j)
        x3, g3, u3, a3 = ffn_fwd(x2, nw_all[i, 2], full["ffn_w_gate"], full["ffn_w_up"], full["ffn_w_down"], (i, 1))
        saved.append(((xc, g1, u1, a1), mix_saved, (x2, g3, u3, a3)))
        xc = x3

    loss_row, dx, dfw = loss_head(xc, full["final_norm_w"].reshape(1, D_MODEL), loss_target[0])
    loss = lax.psum(loss_row[0, 0], ("x", "y", "c"))

    grads = {n: [None] * PARAMS[n][0][0] for n in SMALL if n != "final_norm_w"}
    grads["final_norm_w"] = dfw[0]
    dnorm = [[None] * 3 for _ in range(DEPTH)]
    gbuf = {n: jnp.zeros((2, 4) + _local_shape(n), f32) for n in BIG}

    def ffn_back(i, k, dxo, sv):
        xin, g_, u_, a_ = sv
        dxi, h, dyb, dg, du, dnw = ffn_bwd_dx(xin, dxo, g_, u_, nw_all[i, 2 * k], full["ffn_w_gate"], full["ffn_w_up"], full["ffn_w_down"], (i, k))
        dnorm[i][2 * k] = dnw[0]
        gbuf["ffn_w_gate"] = tn_matmul_to_shards(h, dg, gbuf["ffn_w_gate"], (i, k), 1)
        gbuf["ffn_w_up"] = tn_matmul_to_shards(h, du, gbuf["ffn_w_up"], (i, k), 1)
        gbuf["ffn_w_down"] = tn_matmul_to_shards(a_, dyb, gbuf["ffn_w_down"], (i, k), 0)
        return dxi

    for i in reversed(range(DEPTH)):
        j = i // 2
        sv_a, sv_mix, sv_b = saved[i]
        dx = ffn_back(i, 1, dx, sv_b)
        if i % 2 == 0:
            dx, dnw, gm = _ssd_layer_bwd(dx, nw_all[i, 1], p, j, sv_mix, gbuf)
        else:
            dx, dnw, gm = _sc_layer_bwd(dx, nw_all[i, 1], p, j, sv_mix, gbuf)
        dnorm[i][1] = dnw[0]
        for n, val in gm.items():
            grads[n][j] = val
        dx = ffn_back(i, 0, dx, sv_a)

    grads["norm_w"] = jnp.stack([jnp.stack(r) for r in dnorm])
    for n in SMALL:
        if isinstance(grads[n], list):
            grads[n] = jnp.stack(grads[n])

    g5 = [gbuf[n].reshape((2, 4) + as3d(w_loc[n]).shape) for n in BIG]
    from_sibling = exchange_with_sibling(g5)
    chip_parts = [pair_sum_bf16(g, fs, "pair_sum_" + n) for n, g, fs in zip(BIG, g5, from_sibling)]
    from_chips = exchange_between_chips(chip_parts)
    results = [{}, {}, {}, {}]
    for n, g, fs, fc in zip(BIG, g5, from_sibling, from_chips):
        parts = [((0, 0), g), ((0,), fs), ((0,), fc), ((1,), fc), ((2,), fc)]
        outs = adamw(parts, as3d(w_loc[n]), as3d(m_loc[n]), as3d(v_loc[n]), name="adamw_" + n)
        for k in range(4):
            results[k][n] = outs[k].reshape(_local_shape(n))

    g_small = _pack([grads[n].reshape(-1) for n in SMALL], 8)
    g_small = sum_over_devices(all_gather([g_small], "all_gather_small_grads")[0])
    g_small_full = dict(zip(SMALL, _unpack(g_small, [PARAMS[n][0] for n in SMALL])))
    g_small_loc = []
    for n in SMALL:
        if PARAMS[n][1] is None:
            g_small_loc.append(g_small_full[n])
        else:
            g_small_loc.append(lax.dynamic_index_in_dim(_by_destination(g_small_full[n], n), my_dev, axis=0, keepdims=False))
    small_shapes = [_local_shape(n) for n in SMALL]
    pack_small = lambda d: _pack([d[n].reshape(-1) for n in SMALL], 8)[None]
    small_out = adamw([_pack([gl.reshape(-1) for gl in g_small_loc], 8)[None]], pack_small(w_loc), pack_small(m_loc), pack_small(v_loc), name="adamw_small")
    for k in range(4):
        results[k].update(zip(SMALL, _unpack(small_out[k], small_shapes)))
    return (loss, dx[None], *[results[0][n] for n in NAMES], *[results[1][n] for n in NAMES],
            *[results[2][n] for n in NAMES], *[results[3][n] for n in NAMES])
```

```python
import functools

import jax
import jax.numpy as jnp
from jax import lax
from jax.experimental import pallas as pl
from jax.experimental.pallas import tpu as pltpu

f32 = jnp.float32
bf16 = jnp.bfloat16

D_MODEL = 1024
D_FF = 2816
DEPTH = 4
SSD_INNER = 2048
SSD_HEADS = 32
SSD_HEAD_DIM = 64
SSD_GROUPS = 4
SSD_STATE = 128
SSD_CONV_K = 4
SSD_CONV_DIM = 3072
SSD_IN_DIM = 5152
SSD_CHUNK = 128
SC_CONV_K = 3
RMS_EPS = 1e-5
N_DEV = 8
LANES = 128
HALO = 16
PACK_W = 1024
PACK_TILE = 256
VMEM_LIMIT = 56 * 1024 * 1024
NEG_BIG = -1e30

ADAM_LR = 0.001
ADAM_B1 = 0.9
ADAM_B2 = 0.999
ADAM_EPS = 1e-08
ADAM_WD = 0.01
ADAM_STEP = 10

NT_DIMS = (((1,), (1,)), ((), ()))
TN_DIMS = (((0,), (0,)), ((), ()))
MESH = pl.DeviceIdType.MESH


def _params(sem=None):
    return pltpu.CompilerParams(dimension_semantics=sem, vmem_limit_bytes=VMEM_LIMIT)


def _resident(shape):
    nd = len(shape)
    return pl.BlockSpec(tuple(shape), lambda *_: (0,) * nd, pipeline_mode=pl.Buffered(1))


def _rows(tm, width):
    return pl.BlockSpec((tm, width), lambda i: (i, 0))


def _my_core_and_chip():
    return lax.axis_index("c"), 2 * lax.axis_index("x") + lax.axis_index("y")


def _sigmoid(v):
    return 1.0 / (1.0 + jnp.exp(-v))


def _softplus(v):
    return jnp.maximum(v, 0.0) + jnp.log(1.0 + jnp.exp(-jnp.abs(v)))


def _rms_fwd(xv, w):
    inv = lax.rsqrt(jnp.mean(xv * xv, axis=-1, keepdims=True) + RMS_EPS)
    xh = xv * inv
    return xh * w, xh, inv


def _rms_bwd(dh, xh, inv, w):
    dxh = dh * w
    dx = inv * (dxh - xh * jnp.mean(dxh * xh, axis=-1, keepdims=True))
    return dx, jnp.sum(dh * xh, axis=0, keepdims=True)


def _mm(a, b):
    return jnp.dot(a, b, preferred_element_type=f32)


def _mm_nt(a, b):
    return lax.dot_general(a, b, NT_DIMS, preferred_element_type=f32)


def _mm_tn(a, b):
    return lax.dot_general(a, b, TN_DIMS, preferred_element_type=f32)


def _layer_slab(w, idx):
    tail = w.shape[len(idx):]
    return pl.BlockSpec((None,) * len(idx) + tuple(tail), lambda *_: tuple(idx) + (0,) * len(tail), pipeline_mode=pl.Buffered(1))


def ffn_fwd(x, nw, wg, wu, wd, idx, tm=256, rider=None):
    T = x.shape[0]
    nt = T // tm
    r_in, r_out, r_shapes, r_scratch, r_args = _rider_specs(rider)

    def body(x_ref, nw_ref, wg_ref, wu_ref, wd_ref, xo_ref, g_ref, u_ref, a_ref):
        xv = x_ref[...]
        h, _, _ = _rms_fwd(xv, nw_ref[...])
        hb = h.astype(bf16)
        g = _mm(hb, wg_ref[...])
        u = _mm(hb, wu_ref[...])
        ab = (g * _sigmoid(g) * u).astype(bf16)
        xo_ref[...] = xv + 0.5 * _mm(ab, wd_ref[...])
        g_ref[...] = g.astype(bf16)
        u_ref[...] = u.astype(bf16)
        a_ref[...] = ab

    hosted = _carry(body, 5, 4, rider, lambda: pl.program_id(0) == 0, lambda: pl.program_id(0) == nt - 1)
    return pl.pallas_call(
        hosted, name="ffn_fwd" if rider is None else "ffn_fwd_carrying", grid=(nt,),
        in_specs=[_rows(tm, D_MODEL), _resident((1, D_MODEL)), _layer_slab(wg, idx), _layer_slab(wu, idx), _layer_slab(wd, idx)] + r_in,
        out_specs=[_rows(tm, D_MODEL), _rows(tm, D_FF), _rows(tm, D_FF), _rows(tm, D_FF)] + r_out,
        out_shape=[jax.ShapeDtypeStruct((T, D_MODEL), f32)] + [jax.ShapeDtypeStruct((T, D_FF), bf16)] * 3 + r_shapes,
        scratch_shapes=r_scratch,
        compiler_params=_params(("parallel",) if rider is None else ("arbitrary",)),
    )(x, nw, wg, wu, wd, *r_args)


def ffn_bwd_dx(x, dxo, g, u, nw, wg, wu, wd, idx, tm=256):
    T = x.shape[0]

    def body(x_ref, dxo_ref, g_ref, u_ref, nw_ref, wg_ref, wu_ref, wd_ref, dx_ref, h_ref, dy_ref, dg_ref, du_ref, dnw_ref):
        w = nw_ref[...]
        h, xh, inv = _rms_fwd(x_ref[...], w)
        dxo_v = dxo_ref[...]
        dyb = (0.5 * dxo_v).astype(bf16)
        da = _mm_nt(dyb, wd_ref[...])
        gv = g_ref[...].astype(f32)
        uv = u_ref[...].astype(f32)
        s = _sigmoid(gv)
        dgb = (da * uv * (s * (1.0 + gv * (1.0 - s)))).astype(bf16)
        dub = (da * (gv * s)).astype(bf16)
        dh = _mm_nt(dgb, wg_ref[...]) + _mm_nt(dub, wu_ref[...])
        dxn, dw = _rms_bwd(dh, xh, inv, w)
        dx_ref[...] = dxo_v + dxn
        h_ref[...] = h.astype(bf16)
        dy_ref[...] = dyb
        dg_ref[...] = dgb
        du_ref[...] = dub

        @pl.when(pl.program_id(0) == 0)
        def _():
            dnw_ref[...] = jnp.zeros_like(dnw_ref)

        dnw_ref[...] += dw

    return pl.pallas_call(
        body, name="ffn_bwd_dx", grid=(T // tm,),
        in_specs=[_rows(tm, D_MODEL), _rows(tm, D_MODEL), _rows(tm, D_FF), _rows(tm, D_FF), _resident((1, D_MODEL)),
                  _layer_slab(wg, idx), _layer_slab(wu, idx), _layer_slab(wd, idx)],
        out_specs=[_rows(tm, D_MODEL), _rows(tm, D_MODEL), _rows(tm, D_MODEL), _rows(tm, D_FF), _rows(tm, D_FF),
                   pl.BlockSpec((1, D_MODEL), lambda i: (0, 0))],
        out_shape=[jax.ShapeDtypeStruct((T, D_MODEL), f32), jax.ShapeDtypeStruct((T, D_MODEL), bf16), jax.ShapeDtypeStruct((T, D_MODEL), bf16),
                   jax.ShapeDtypeStruct((T, D_FF), bf16), jax.ShapeDtypeStruct((T, D_FF), bf16), jax.ShapeDtypeStruct((1, D_MODEL), f32)],
        compiler_params=_params(("arbitrary",)),
    )(x, dxo, g, u, nw, wg, wu, wd)


def tn_matmul(a, b, tk=512):
    T, M = a.shape
    N = b.shape[1]
    bn = N if M * N <= 3_200_000 else N // 2
    nk = T // tk

    def body(a_ref, b_ref, o_ref):
        @pl.when(pl.program_id(1) == 0)
        def _():
            o_ref[...] = jnp.zeros_like(o_ref)

        o_ref[...] += _mm_tn(a_ref[...], b_ref[...])

    return pl.pallas_call(
        body, name=f"tn_matmul_{M}x{N}", grid=(N // bn, nk),
        in_specs=[pl.BlockSpec((tk, M), lambda j, k: (k, 0)), pl.BlockSpec((tk, bn), lambda j, k: (k, j))],
        out_specs=pl.BlockSpec((M, bn), lambda j, k: (0, j)),
        out_shape=jax.ShapeDtypeStruct((M, N), f32),
        compiler_params=_params(("parallel", "arbitrary")),
    )(a, b)


def tn_matmul_to_shards(a, b, buf, idx, axis, tk=512):
    T, M = a.shape
    N = b.shape[1]
    m, n = buf.shape[-2:]
    nk = T // tk

    def body(a_ref, b_ref, buf_ref, o_ref, acc_ref):
        k = pl.program_id(0)

        @pl.when(k == 0)
        def _():
            acc_ref[...] = jnp.zeros_like(acc_ref)

        acc_ref[...] += _mm_tn(a_ref[...], b_ref[...])

        @pl.when(k == nk - 1)
        def _():
            my_c, my_chip = _my_core_and_chip()
            for d in range(N_DEV):
                piece = acc_ref[:, pl.ds(d * n, n)] if axis == 1 else acc_ref[pl.ds(d * m, m), :]
                o_ref[(d % 2) ^ my_c, (d // 2) ^ my_chip] = piece

    none = (None,) * len(idx)
    return pl.pallas_call(
        body, name=f"tn_matmul_to_shards_{M}x{N}_{axis}", grid=(nk,),
        in_specs=[pl.BlockSpec((tk, M), lambda k: (k, 0)), pl.BlockSpec((tk, N), lambda k: (k, 0)), _ANY],
        out_specs=pl.BlockSpec((2, 4) + none + (m, n), lambda k: (0, 0) + tuple(idx) + (0, 0)),
        out_shape=jax.ShapeDtypeStruct(buf.shape, f32),
        scratch_shapes=[pltpu.VMEM((M, N), f32)],
        input_output_aliases={2: 0},
        compiler_params=_params(("arbitrary",)),
    )(a, b, buf)


def in_proj_fwd(x, nw, ws, out_dtypes, tm=256):
    T = x.shape[0]
    n = len(ws)

    def body(*refs):
        x_ref, nw_ref = refs[:2]
        w_refs = refs[2:2 + n]
        o_refs = refs[2 + n:]
        h, _, _ = _rms_fwd(x_ref[...], nw_ref[...])
        hb = h.astype(bf16)
        for w_ref, o_ref in zip(w_refs, o_refs):
            o_ref[...] = _mm(hb, w_ref[...]).astype(o_ref.dtype)

    return pl.pallas_call(
        body, name="in_proj_fwd_" + "_".join(str(w.shape[1]) for w in ws), grid=(T // tm,),
        in_specs=[_rows(tm, D_MODEL), _resident((1, D_MODEL))] + [_resident(w.shape) for w in ws],
        out_specs=[_rows(tm, w.shape[1]) for w in ws],
        out_shape=[jax.ShapeDtypeStruct((T, w.shape[1]), dt) for w, dt in zip(ws, out_dtypes)],
        compiler_params=_params(("parallel",)),
    )(x, nw, *ws)


def in_proj_bwd(x, nw, dxo, dys, ws, tm=256):
    T = x.shape[0]
    n = len(ws)

    def body(*refs):
        x_ref, nw_ref, dxo_ref = refs[:3]
        dy_refs = refs[3:3 + n]
        w_refs = refs[3 + n:3 + 2 * n]
        dx_ref, h_ref, dnw_ref = refs[3 + 2 * n:]
        w = nw_ref[...]
        h, xh, inv = _rms_fwd(x_ref[...], w)
        dh = _mm_nt(dy_refs[0][...], w_refs[0][...])
        for dy_ref, w_ref in zip(dy_refs[1:], w_refs[1:]):
            dh = dh + _mm_nt(dy_ref[...], w_ref[...])
        dxn, dw = _rms_bwd(dh, xh, inv, w)
        dx_ref[...] = dxo_ref[...] + dxn
        h_ref[...] = h.astype(bf16)

        @pl.when(pl.program_id(0) == 0)
        def _():
            dnw_ref[...] = jnp.zeros_like(dnw_ref)

        dnw_ref[...] += dw

    return pl.pallas_call(
        body, name="in_proj_bwd_" + "_".join(str(w.shape[1]) for w in ws), grid=(T // tm,),
        in_specs=[_rows(tm, D_MODEL), _resident((1, D_MODEL)), _rows(tm, D_MODEL)] + [_rows(tm, w.shape[1]) for w in ws]
        + [_resident(w.shape) for w in ws],
        out_specs=[_rows(tm, D_MODEL), _rows(tm, D_MODEL), pl.BlockSpec((1, D_MODEL), lambda i: (0, 0))],
        out_shape=[jax.ShapeDtypeStruct((T, D_MODEL), f32), jax.ShapeDtypeStruct((T, D_MODEL), bf16), jax.ShapeDtypeStruct((1, D_MODEL), f32)],
        compiler_params=_params(("arbitrary",)),
    )(x, nw, dxo, *dys, *ws)


def out_proj_fwd(x, a, w, tm=512):
    T = x.shape[0]
    K = a.shape[1]

    def body(x_ref, a_ref, w_ref, o_ref):
        o_ref[...] = x_ref[...] + _mm(a_ref[...], w_ref[...])

    return pl.pallas_call(
        body, name=f"out_proj_fwd_{K}", grid=(T // tm,),
        in_specs=[_rows(tm, D_MODEL), _rows(tm, K), _resident(w.shape)],
        out_specs=_rows(tm, D_MODEL), out_shape=jax.ShapeDtypeStruct((T, D_MODEL), f32),
        compiler_params=_params(("parallel",)),
    )(x, a, w)


def out_proj_bwd(dxo, w, tm=512):
    T = dxo.shape[0]
    K = w.shape[0]

    def body(dxo_ref, w_ref, da_ref, dy_ref):
        dyb = dxo_ref[...].astype(bf16)
        dy_ref[...] = dyb
        da_ref[...] = _mm_nt(dyb, w_ref[...]).astype(bf16)

    return pl.pallas_call(
        body, name=f"out_proj_bwd_{K}", grid=(T // tm,),
        in_specs=[_rows(tm, D_MODEL), _resident(w.shape)],
        out_specs=[_rows(tm, K), _rows(tm, D_MODEL)],
        out_shape=[jax.ShapeDtypeStruct((T, K), bf16), jax.ShapeDtypeStruct((T, D_MODEL), bf16)],
        compiler_params=_params(("parallel",)),
    )(dxo, w)


def _halo_spec(tm, width, n_tiles, reverse):
    per = tm // HALO

    def idx(i):
        t = (n_tiles - 1 - i) if reverse else i
        return (jnp.maximum(t * per - 1, 0), 0)

    return pl.BlockSpec((HALO, width), idx)


def _tile_spec(tm, width, n_tiles, reverse):
    if reverse:
        return pl.BlockSpec((tm, width), lambda i: (n_tiles - 1 - i, 0))
    return _rows(tm, width)


ROW_BLOCK = 64


def _strip(s):
    return pl.ds(pl.multiple_of(s * LANES, LANES), LANES)


def _conv_rows(ext_ref, w_ref, cols, k_w, r0):
    base = HALO - (k_w - 1) + r0
    wins = [ext_ref[pl.ds(base + k, ROW_BLOCK), :] for k in range(k_w)]
    out = w_ref[pl.ds(0, 1), cols] * wins[0]
    for k in range(1, k_w):
        out = out + w_ref[pl.ds(k, 1), cols] * wins[k]
    return out, wins


def _shifted_back(d_ref, w_ref, cols, k_w, r0):
    out = w_ref[pl.ds(0, 1), cols] * d_ref[pl.ds(r0 + k_w - 1, ROW_BLOCK), :]
    for k in range(1, k_w):
        out = out + w_ref[pl.ds(k, 1), cols] * d_ref[pl.ds(r0 + k_w - 1 - k, ROW_BLOCK), :]
    return out


def ssd_conv_fwd(xbc, conv_w, conv_b, dt_raw, dt_bias, tm=256):
    T = xbc.shape[0]
    nt = T // tm
    K = SSD_CONV_K

    def body(x_ref, halo_ref, w_ref, b_ref, dtr_ref, dtb_ref, act_ref, dt_ref, ext_ref):
        first = pl.program_id(0) == 0

        def strip(s, carry):
            cols = _strip(s)
            ext_ref[pl.ds(0, HALO), :] = jnp.where(first, 0.0, halo_ref[:, cols].astype(f32))
            ext_ref[pl.ds(HALO, tm), :] = x_ref[:, cols].astype(f32)
            for r0 in range(0, tm, ROW_BLOCK):
                pre, _ = _conv_rows(ext_ref, w_ref, cols, K, r0)
                pre = pre + b_ref[:, cols]
                act_ref[pl.ds(r0, ROW_BLOCK), cols] = (pre * _sigmoid(pre)).astype(bf16)
            return carry

        lax.fori_loop(0, SSD_CONV_DIM // LANES, strip, 0)
        dt_ref[...] = _softplus(dtr_ref[...] + dtb_ref[...])

    return pl.pallas_call(
        body, name="ssd_conv_fwd", grid=(nt,),
        in_specs=[_rows(tm, SSD_CONV_DIM), _halo_spec(tm, SSD_CONV_DIM, nt, False), _resident(conv_w.shape), _resident(conv_b.shape),
                  _rows(tm, LANES), _resident(dt_bias.shape)],
        out_specs=[_rows(tm, SSD_CONV_DIM), _rows(tm, LANES)],
        out_shape=[jax.ShapeDtypeStruct((T, SSD_CONV_DIM), bf16), jax.ShapeDtypeStruct((T, LANES), f32)],
        scratch_shapes=[pltpu.VMEM((tm + HALO, LANES), f32)],
        compiler_params=_params(("parallel",)),
    )(xbc, xbc, conv_w, conv_b, dt_raw, dt_bias)


def ssd_conv_bwd(xbc, conv_w, conv_b, dt_raw, dt_bias, dxs_a, dxs_b, db, dc, ddt, tm=256):
    T = xbc.shape[0]
    nt = T // tm
    K = SSD_CONV_K

    def body(x_ref, halo_ref, w_ref, b_ref, dtr_ref, dtb_ref, da_ref, dbb_ref, db_ref, dc_ref, ddt_ref,
             dx_ref, ddtr_ref, dw_ref, dbias_ref, ddtb_ref, ext_ref, dpre_ref, carry_ref):
        i = pl.program_id(0)

        @pl.when(i == 0)
        def _():
            carry_ref[...] = jnp.zeros_like(carry_ref)
            dw_ref[...] = jnp.zeros_like(dw_ref)
            dbias_ref[...] = jnp.zeros_like(dbias_ref)
            ddtb_ref[...] = jnp.zeros_like(ddtb_ref)

        first_tile = i == nt - 1

        def run_strips(lo, hi, load_dact):
            def strip(s, carry):
                cols = _strip(s)
                ext_ref[pl.ds(0, HALO), :] = jnp.where(first_tile, 0.0, halo_ref[:, cols].astype(f32))
                ext_ref[pl.ds(HALO, tm), :] = x_ref[:, cols].astype(f32)
                dpre_ref[pl.ds(tm, 8), :] = carry_ref[:, cols]
                bias = b_ref[:, cols]
                dws = [jnp.zeros((1, LANES), f32) for _ in range(K)]
                dbs = jnp.zeros((1, LANES), f32)
                for r0 in range(0, tm, ROW_BLOCK):
                    pre, wins = _conv_rows(ext_ref, w_ref, cols, K, r0)
                    pre = pre + bias
                    sg = _sigmoid(pre)
                    dpre = load_dact(s, r0) * (sg * (1.0 + pre * (1.0 - sg)))
                    dpre_ref[pl.ds(r0, ROW_BLOCK), :] = dpre
                    dbs = dbs + jnp.sum(dpre, axis=0, keepdims=True)
                    for k in range(K):
                        dws[k] = dws[k] + jnp.sum(dpre * wins[k], axis=0, keepdims=True)
                carry_ref[:, cols] = dpre_ref[pl.ds(0, 8), :]
                for r0 in range(0, tm, ROW_BLOCK):
                    dx_ref[pl.ds(r0, ROW_BLOCK), cols] = _shifted_back(dpre_ref, w_ref, cols, K, r0).astype(bf16)
                for k in range(K):
                    dw_ref[pl.ds(k, 1), cols] += dws[k]
                dbias_ref[:, cols] += dbs
                return carry

            lax.fori_loop(lo, hi, strip, 0)

        rows = lambda r0: pl.ds(r0, ROW_BLOCK)
        n_x = SSD_INNER // LANES
        n_g = SSD_GROUPS * SSD_STATE // LANES
        run_strips(0, n_x, lambda s, r0: da_ref[rows(r0), _strip(s)].astype(f32) + dbb_ref[rows(r0), _strip(s)].astype(f32))
        run_strips(n_x, n_x + n_g, lambda s, r0: db_ref[rows(r0), _strip(s - n_x)].astype(f32))
        run_strips(n_x + n_g, n_x + 2 * n_g, lambda s, r0: dc_ref[rows(r0), _strip(s - n_x - n_g)].astype(f32))
        ddtr = ddt_ref[...] * _sigmoid(dtr_ref[...] + dtb_ref[...])
        ddtr_ref[...] = ddtr.astype(bf16)
        ddtb_ref[...] += jnp.sum(ddtr, axis=0, keepdims=True)

    rev = functools.partial(_tile_spec, tm, n_tiles=nt, reverse=True)
    const = lambda shape: pl.BlockSpec(shape, lambda i: (0, 0))
    return pl.pallas_call(
        body, name="ssd_conv_bwd", grid=(nt,),
        in_specs=[rev(width=SSD_CONV_DIM), _halo_spec(tm, SSD_CONV_DIM, nt, True), _resident(conv_w.shape), _resident(conv_b.shape),
                  rev(width=LANES), _resident(dt_bias.shape), rev(width=SSD_INNER), rev(width=SSD_INNER),
                  rev(width=SSD_GROUPS * SSD_STATE), rev(width=SSD_GROUPS * SSD_STATE), rev(width=LANES)],
        out_specs=[rev(width=SSD_CONV_DIM), rev(width=LANES), const((8, SSD_CONV_DIM)), const((1, SSD_CONV_DIM)), const((1, LANES))],
        out_shape=[jax.ShapeDtypeStruct((T, SSD_CONV_DIM), bf16), jax.ShapeDtypeStruct((T, LANES), bf16),
                   jax.ShapeDtypeStruct((8, SSD_CONV_DIM), f32), jax.ShapeDtypeStruct((1, SSD_CONV_DIM), f32), jax.ShapeDtypeStruct((1, LANES), f32)],
        scratch_shapes=[pltpu.VMEM((tm + HALO, LANES), f32), pltpu.VMEM((tm + 8, LANES), f32), pltpu.VMEM((8, SSD_CONV_DIM), f32)],
        compiler_params=_params(("arbitrary",)),
    )(xbc, xbc, conv_w, conv_b, dt_raw, dt_bias, dxs_a, dxs_b, db, dc, ddt)


def _ssd_chunk(xs, bm, cm, dt, alog, st):
    L = SSD_CHUNK
    row = lax.broadcasted_iota(jnp.int32, (L, L), 0)
    col = lax.broadcasted_iota(jnp.int32, (L, L), 1)
    causal = row >= col
    tril = jnp.where(causal, 1.0, 0.0).astype(f32)
    lane = lax.broadcasted_iota(jnp.int32, (1, LANES), 1)
    sub = lax.broadcasted_iota(jnp.int32, (LANES, 1), 0)
    lo = lane < SSD_HEAD_DIM
    last_row = sub == L - 1

    dta = dt * (-jnp.exp(alog))
    a_cs = jnp.dot(tril, dta, precision=lax.Precision.HIGHEST, preferred_element_type=f32)
    a_cs_t = a_cs.T
    bmb = bm.astype(bf16)
    cmb = cm.astype(bf16)
    cb = _mm_nt(cmb, bmb)
    c_st = _mm(cmb, st.astype(bf16))

    def head_col(v, e):
        return jnp.sum(jnp.where(lane == e, v, 0.0), axis=1, keepdims=True)

    def head_row(v, e):
        return jnp.sum(jnp.where(sub == e, v, 0.0), axis=0, keepdims=True)

    ys, sts = [], []
    for j in range(4):
        e0, e1 = 2 * j, 2 * j + 1
        c0, c1 = head_col(a_cs, e0), head_col(a_cs, e1)
        acs_x = jnp.where(lo, c0, c1)
        dt_x = jnp.where(lo, head_col(dt, e0), head_col(dt, e1))
        xd = xs[:, j * LANES:(j + 1) * LANES] * dt_x
        m0 = cb * jnp.exp(jnp.where(causal, c0 - head_row(a_cs_t, e0), NEG_BIG))
        m1 = cb * jnp.exp(jnp.where(causal, c1 - head_row(a_cs_t, e1), NEG_BIG))
        mcat = jnp.concatenate([m0, m1], axis=1).astype(bf16)
        xcat = jnp.concatenate([jnp.where(lo, xd, 0.0), jnp.where(lo, 0.0, xd)], axis=0).astype(bf16)
        y_diag = _mm(mcat, xcat)
        a_last = jnp.sum(jnp.where(last_row, acs_x, 0.0), axis=0, keepdims=True)
        x_dec = (xd * jnp.exp(a_last - acs_x)).astype(bf16)
        s_new = _mm_tn(bmb, x_dec)
        y_off = c_st[:, j * LANES:(j + 1) * LANES] * jnp.exp(acs_x)
        ys.append(y_diag + y_off)
        sts.append(jnp.exp(a_last) * st[:, j * LANES:(j + 1) * LANES] + s_new)
    return jnp.concatenate(ys, axis=1), jnp.concatenate(sts, axis=1)


SCAN_GROUPS_FWD = 4
SCAN_GROUPS_BWD = 1


def _scan_specs(nc, reverse, gs):
    L = SSD_CHUNK
    ch = (lambda c: nc - 1 - c) if reverse else (lambda c: c)
    gw = SSD_INNER // SSD_GROUPS
    b0 = SSD_INNER // (gs * SSD_STATE)
    c0 = (SSD_INNER + SSD_GROUPS * SSD_STATE) // (gs * SSD_STATE)
    xs = pl.BlockSpec((L, gs * gw), lambda g, c: (ch(c), g))
    bm = pl.BlockSpec((L, gs * SSD_STATE), lambda g, c: (ch(c), b0 + g))
    cm = pl.BlockSpec((L, gs * SSD_STATE), lambda g, c: (ch(c), c0 + g))
    dt = pl.BlockSpec((gs, L, LANES), lambda g, c: (g, ch(c), 0))
    alog = pl.BlockSpec((gs, 1, LANES), lambda g, c: (g, 0, 0))
    st = pl.BlockSpec((gs, None, SSD_STATE, gw), lambda g, c: (g, ch(c), 0, 0))
    y = pl.BlockSpec((L, gs * gw), lambda g, c: (ch(c), g))
    grp = pl.BlockSpec((L, gs * SSD_STATE), lambda g, c: (ch(c), g))
    return xs, bm, cm, dt, alog, st, y, grp


def ssd_scan_fwd(act, dt4, alog4, rider=None):
    T = act.shape[0]
    nc = T // SSD_CHUNK
    gs = SCAN_GROUPS_FWD
    ng = SSD_GROUPS // gs
    gw = SSD_INNER // SSD_GROUPS
    xs_s, bm_s, cm_s, dt_s, alog_s, st_s, y_s, _ = _scan_specs(nc, False, gs)
    r_in, r_out, r_shapes, r_scratch, r_args = _rider_specs(rider)

    def body(xs_ref, bm_ref, cm_ref, dt_ref, alog_ref, y_ref, st_ref, st_scr):
        @pl.when(pl.program_id(1) == 0)
        def _():
            st_scr[...] = jnp.zeros_like(st_scr)

        for q in range(gs):
            xc, gc = pl.ds(q * gw, gw), pl.ds(q * SSD_STATE, SSD_STATE)
            st = st_scr[q]
            st_ref[q] = st
            y, st_new = _ssd_chunk(xs_ref[:, xc].astype(f32), bm_ref[:, gc].astype(f32), cm_ref[:, gc].astype(f32), dt_ref[q], alog_ref[q], st)
            y_ref[:, xc] = y.astype(bf16)
            st_scr[q] = st_new

    first = lambda: jnp.logical_and(pl.program_id(0) == 0, pl.program_id(1) == 0)
    last = lambda: jnp.logical_and(pl.program_id(0) == ng - 1, pl.program_id(1) == nc - 1)
    return pl.pallas_call(
        _carry(body, 5, 2, rider, first, last), name="ssd_scan_fwd" if rider is None else "ssd_scan_fwd_carrying", grid=(ng, nc),
        in_specs=[xs_s, bm_s, cm_s, dt_s, alog_s] + r_in, out_specs=[y_s, st_s] + r_out,
        out_shape=[jax.ShapeDtypeStruct((T, SSD_INNER), bf16), jax.ShapeDtypeStruct((SSD_GROUPS, nc, SSD_STATE, gw), f32)] + r_shapes,
        scratch_shapes=[pltpu.VMEM((gs, SSD_STATE, gw), f32)] + r_scratch,
        compiler_params=_params(("parallel" if rider is None else "arbitrary", "arbitrary")),
    )(act, act, act, dt4, alog4, *r_args)


def ssd_scan_bwd(act, dt4, alog4, states, dy, rider=None):
    T = act.shape[0]
    nc = T // SSD_CHUNK
    gs = SCAN_GROUPS_BWD
    ng = SSD_GROUPS // gs
    gw = SSD_INNER // SSD_GROUPS
    xs_s, bm_s, cm_s, dt_s, alog_s, st_s, y_s, grp_s = _scan_specs(nc, True, gs)
    r_in, r_out, r_shapes, r_scratch, r_args = _rider_specs(rider)

    def body(xs_ref, bm_ref, cm_ref, dt_ref, alog_ref, st_ref, dy_ref, dxs_ref, db_ref, dc_ref, ddt_ref, dalog_ref, dst_scr):
        @pl.when(pl.program_id(1) == 0)
        def _():
            dst_scr[...] = jnp.zeros_like(dst_scr)
            dalog_ref[...] = jnp.zeros_like(dalog_ref)

        for q in range(gs):
            xc, gc = pl.ds(q * gw, gw), pl.ds(q * SSD_STATE, SSD_STATE)
            _, vjp = jax.vjp(_ssd_chunk, xs_ref[:, xc].astype(f32), bm_ref[:, gc].astype(f32), cm_ref[:, gc].astype(f32),
                             dt_ref[q], alog_ref[q], st_ref[q])
            dxs, dbm, dcm, ddt, dalog, dst = vjp((dy_ref[:, xc].astype(f32), dst_scr[q]))
            dxs_ref[:, xc] = dxs.astype(bf16)
            db_ref[:, gc] = dbm.astype(bf16)
            dc_ref[:, gc] = dcm.astype(bf16)
            ddt_ref[q] = ddt
            dalog_ref[q] += dalog
            dst_scr[q] = dst

    first = lambda: jnp.logical_and(pl.program_id(0) == 0, pl.program_id(1) == 0)
    last = lambda: jnp.logical_and(pl.program_id(0) == ng - 1, pl.program_id(1) == nc - 1)
    return pl.pallas_call(
        _carry(body, 7, 5, rider, first, last), name="ssd_scan_bwd" if rider is None else "ssd_scan_bwd_carrying", grid=(ng, nc),
        in_specs=[xs_s, bm_s, cm_s, dt_s, alog_s, st_s, y_s] + r_in,
        out_specs=[y_s, grp_s, grp_s, dt_s, alog_s] + r_out,
        out_shape=[jax.ShapeDtypeStruct((T, SSD_INNER), bf16), jax.ShapeDtypeStruct((T, SSD_GROUPS * SSD_STATE), bf16),
                   jax.ShapeDtypeStruct((T, SSD_GROUPS * SSD_STATE), bf16), jax.ShapeDtypeStruct((SSD_GROUPS, T, LANES), f32),
                   jax.ShapeDtypeStruct((SSD_GROUPS, 1, LANES), f32)] + r_shapes,
        scratch_shapes=[pltpu.VMEM((gs, SSD_STATE, gw), f32)] + r_scratch,
        compiler_params=_params(("parallel" if rider is None else "arbitrary", "arbitrary")),
    )(act, act, act, dt4, alog4, states, dy, *r_args)


GATE_ROWS = 256


def _ssd_gate(y, xs, z, d_x, nw):
    g = (y + xs * d_x) * (z * _sigmoid(z))
    return g * lax.rsqrt(jnp.mean(g * g, axis=-1, keepdims=True) + RMS_EPS) * nw


def _gate_blocks(tm, fn):
    gw = SSD_INNER // SSD_GROUPS

    def block(r, carry):
        rows = pl.ds(r * GATE_ROWS if isinstance(r, int) else pl.multiple_of(r * GATE_ROWS, GATE_ROWS), GATE_ROWS)
        for k in range(SSD_GROUPS):
            fn(rows, pl.ds(k * gw, gw))
        return carry

    if tm == GATE_ROWS:
        block(0, 0)
    else:
        lax.fori_loop(0, tm // GATE_ROWS, block, 0)


def ssd_gate_fwd(y, act, z, d_x, nw, tm=256):
    T = y.shape[0]

    def body(y_ref, xs_ref, z_ref, d_ref, nw_ref, o_ref):
        def one(rows, cols):
            o_ref[rows, cols] = _ssd_gate(y_ref[rows, cols].astype(f32), xs_ref[rows, cols].astype(f32), z_ref[rows, cols].astype(f32),
                                          d_ref[:, cols], nw_ref[:, cols]).astype(bf16)

        _gate_blocks(tm, one)

    return pl.pallas_call(
        body, name="ssd_gate_fwd", grid=(T // tm,),
        in_specs=[_rows(tm, SSD_INNER), _rows(tm, SSD_INNER), _rows(tm, SSD_INNER), _resident(d_x.shape), _resident(nw.shape)],
        out_specs=_rows(tm, SSD_INNER), out_shape=jax.ShapeDtypeStruct((T, SSD_INNER), bf16),
        compiler_params=_params(("parallel",)),
    )(y, act, z, d_x, nw)


def ssd_gate_bwd(y, act, z, d_x, nw, dgn, tm=256):
    T = y.shape[0]

    def body(y_ref, xs_ref, z_ref, d_ref, nw_ref, dgn_ref, dy_ref, dxs_ref, dz_ref, dd_ref, dnw_ref):
        @pl.when(pl.program_id(0) == 0)
        def _():
            dd_ref[...] = jnp.zeros_like(dd_ref)
            dnw_ref[...] = jnp.zeros_like(dnw_ref)

        def one(rows, cols):
            _, vjp = jax.vjp(_ssd_gate, y_ref[rows, cols].astype(f32), xs_ref[rows, cols].astype(f32), z_ref[rows, cols].astype(f32),
                             d_ref[:, cols], nw_ref[:, cols])
            dy, dxs, dz, dd, dnw = vjp(dgn_ref[rows, cols].astype(f32))
            dy_ref[rows, cols] = dy.astype(bf16)
            dxs_ref[rows, cols] = dxs.astype(bf16)
            dz_ref[rows, cols] = dz.astype(bf16)
            dd_ref[:, cols] += dd
            dnw_ref[:, cols] += dnw

        _gate_blocks(tm, one)

    const = pl.BlockSpec((1, SSD_INNER), lambda i: (0, 0))
    return pl.pallas_call(
        body, name="ssd_gate_bwd", grid=(T // tm,),
        in_specs=[_rows(tm, SSD_INNER), _rows(tm, SSD_INNER), _rows(tm, SSD_INNER), _resident(d_x.shape), _resident(nw.shape), _rows(tm, SSD_INNER)],
        out_specs=[_rows(tm, SSD_INNER)] * 3 + [const, const],
        out_shape=[jax.ShapeDtypeStruct((T, SSD_INNER), bf16)] * 3 + [jax.ShapeDtypeStruct((1, SSD_INNER), f32)] * 2,
        compiler_params=_params(("arbitrary",)),
    )(y, act, z, d_x, nw, dgn)


def sc_mid_fwd(bcu, conv_w, tm=256):
    T = bcu.shape[0]
    nt = T // tm
    Dm = D_MODEL

    def body(x_ref, halo_ref, w_ref, q_ref, ext_ref):
        first = pl.program_id(0) == 0
        n_s = Dm // LANES

        def strip(s, carry):
            cols, c_cols, u_cols = _strip(s), _strip(s + n_s), _strip(s + 2 * n_s)
            ext_ref[pl.ds(0, HALO), :] = jnp.where(first, 0.0, halo_ref[:, c_cols].astype(f32) * halo_ref[:, u_cols].astype(f32))
            ext_ref[pl.ds(HALO, tm), :] = x_ref[:, c_cols].astype(f32) * x_ref[:, u_cols].astype(f32)
            for r0 in range(0, tm, ROW_BLOCK):
                rows = pl.ds(r0, ROW_BLOCK)
                v, _ = _conv_rows(ext_ref, w_ref, cols, SC_CONV_K, r0)
                q_ref[rows, cols] = (x_ref[rows, cols].astype(f32) * v).astype(bf16)
            return carry

        lax.fori_loop(0, n_s, strip, 0)

    return pl.pallas_call(
        body, name="sc_mid_fwd", grid=(nt,),
        in_specs=[_rows(tm, 3 * Dm), _halo_spec(tm, 3 * Dm, nt, False), _resident(conv_w.shape)],
        out_specs=_rows(tm, Dm), out_shape=jax.ShapeDtypeStruct((T, Dm), bf16),
        scratch_shapes=[pltpu.VMEM((tm + HALO, LANES), f32)],
        compiler_params=_params(("parallel",)),
    )(bcu, bcu, conv_w)


def sc_mid_bwd(bcu, conv_w, dq, tm=256):
    T = bcu.shape[0]
    nt = T // tm
    Dm = D_MODEL
    K = SC_CONV_K

    def body(x_ref, halo_ref, w_ref, dq_ref, dx_ref, dw_ref, ext_ref, dv_ref, carry_ref):
        i = pl.program_id(0)

        @pl.when(i == 0)
        def _():
            carry_ref[...] = jnp.zeros_like(carry_ref)
            dw_ref[...] = jnp.zeros_like(dw_ref)

        first_tile = i == nt - 1
        n_s = Dm // LANES

        def strip(s, carry):
            cols, c_cols, u_cols = _strip(s), _strip(s + n_s), _strip(s + 2 * n_s)
            ext_ref[pl.ds(0, HALO), :] = jnp.where(first_tile, 0.0, halo_ref[:, c_cols].astype(f32) * halo_ref[:, u_cols].astype(f32))
            ext_ref[pl.ds(HALO, tm), :] = x_ref[:, c_cols].astype(f32) * x_ref[:, u_cols].astype(f32)
            dv_ref[pl.ds(tm, 8), :] = carry_ref[:, cols]
            dws = [jnp.zeros((1, LANES), f32) for _ in range(K)]
            for r0 in range(0, tm, ROW_BLOCK):
                rows = pl.ds(r0, ROW_BLOCK)
                v, wins = _conv_rows(ext_ref, w_ref, cols, K, r0)
                dqv = dq_ref[rows, cols].astype(f32)
                dv = dqv * x_ref[rows, cols].astype(f32)
                dv_ref[rows, :] = dv
                dx_ref[rows, cols] = (dqv * v).astype(bf16)
                for k in range(K):
                    dws[k] = dws[k] + jnp.sum(dv * wins[k], axis=0, keepdims=True)
            carry_ref[:, cols] = dv_ref[pl.ds(0, 8), :]
            for r0 in range(0, tm, ROW_BLOCK):
                rows = pl.ds(r0, ROW_BLOCK)
                dp = _shifted_back(dv_ref, w_ref, cols, K, r0)
                dx_ref[rows, c_cols] = (dp * x_ref[rows, u_cols].astype(f32)).astype(bf16)
                dx_ref[rows, u_cols] = (dp * x_ref[rows, c_cols].astype(f32)).astype(bf16)
            for k in range(K):
                dw_ref[pl.ds(k, 1), cols] += dws[k]
            return carry

        lax.fori_loop(0, n_s, strip, 0)

    return pl.pallas_call(
        body, name="sc_mid_bwd", grid=(nt,),
        in_specs=[_tile_spec(tm, 3 * Dm, nt, True), _halo_spec(tm, 3 * Dm, nt, True), _resident(conv_w.shape), _tile_spec(tm, Dm, nt, True)],
        out_specs=[_tile_spec(tm, 3 * Dm, nt, True), pl.BlockSpec((8, Dm), lambda i: (0, 0))],
        out_shape=[jax.ShapeDtypeStruct((T, 3 * Dm), bf16), jax.ShapeDtypeStruct((8, Dm), f32)],
        scratch_shapes=[pltpu.VMEM((tm + HALO, LANES), f32), pltpu.VMEM((tm + 8, LANES), f32), pltpu.VMEM((8, Dm), f32)],
        compiler_params=_params(("arbitrary",)),
    )(bcu, bcu, conv_w, dq)


def loss_head(x, fw, target, tm=512):
    T = x.shape[0]

    def body(x_ref, fw_ref, t_ref, loss_ref, dx_ref, dfw_ref):
        @pl.when(pl.program_id(0) == 0)
        def _():
            loss_ref[...] = jnp.zeros_like(loss_ref)
            dfw_ref[...] = jnp.zeros_like(dfw_ref)

        w = fw_ref[...]
        y, xh, inv = _rms_fwd(x_ref[...], w)
        err = y - t_ref[...]
        loss_ref[...] += 0.5 * jnp.sum(jnp.mean(err * err, axis=-1, keepdims=True), axis=0, keepdims=True)
        dx, dw = _rms_bwd(err * (1.0 / D_MODEL), xh, inv, w)
        dx_ref[...] = dx
        dfw_ref[...] += dw

    return pl.pallas_call(
        body, name="loss_head", grid=(T // tm,),
        in_specs=[_rows(tm, D_MODEL), _resident((1, D_MODEL)), _rows(tm, D_MODEL)],
        out_specs=[pl.BlockSpec((1, LANES), lambda i: (0, 0)), _rows(tm, D_MODEL), pl.BlockSpec((1, D_MODEL), lambda i: (0, 0))],
        out_shape=[jax.ShapeDtypeStruct((1, LANES), f32), jax.ShapeDtypeStruct((T, D_MODEL), f32), jax.ShapeDtypeStruct((1, D_MODEL), f32)],
        compiler_params=_params(("arbitrary",)),
    )(x, fw, target)


def _row_tile(rows):
    return rows if rows <= 512 else 256


def adamw(g_parts, w, m, v, name="adamw", a0=0, prev=None):
    A, B, n = w.shape
    tb = _row_tile(B)
    n_parts = len(g_parts)
    arrays, specs = [], []
    for part in g_parts:
        lead, arr = part if isinstance(part, tuple) else ((), part)
        specs.append(pl.BlockSpec((None,) * (len(lead) + 1) + (tb, n), lambda a, t, lead=lead: tuple(lead) + (a, t, 0)))
        arrays.append(arr)
    na = arrays[0].shape[-3]
    prev = list(prev) if prev is not None else []

    def body(*refs):
        n = n_parts
        g_refs = refs[:n]
        w_ref, m_ref, v_ref = refs[n:n + 3]
        go_ref, d_ref, mo_ref, vo_ref = refs[n + 3 + len(prev):]
        g = g_refs[0][...].astype(f32)
        for r in g_refs[1:]:
            g = g + r[...].astype(f32)
        m_new = ADAM_B1 * m_ref[...] + (1.0 - ADAM_B1) * g
        v_new = ADAM_B2 * v_ref[...] + (1.0 - ADAM_B2) * (g * g)
        m_hat = m_new / (1.0 - ADAM_B1 ** ADAM_STEP)
        v_hat = v_new / (1.0 - ADAM_B2 ** ADAM_STEP)
        go_ref[...] = g
        d_ref[...] = -ADAM_LR * (m_hat / (jnp.sqrt(v_hat) + ADAM_EPS) + ADAM_WD * w_ref[...])
        mo_ref[...] = m_new
        vo_ref[...] = v_new

    plain = pl.BlockSpec((None, tb, n), lambda a, t: (a + a0, t, 0))
    return pl.pallas_call(
        body, name=name, grid=(na, B // tb), in_specs=specs + [plain] * 3 + [_ANY] * len(prev), out_specs=[plain] * 4,
        out_shape=[jax.ShapeDtypeStruct((A, B, n), f32)] * 4,
        input_output_aliases={n_parts + 3 + k: k for k in range(len(prev))},
        compiler_params=_params(("parallel", "parallel")),
    )(*arrays, w, m, v, *prev)


def pair_sum_bf16(ga, gb, name):
    _, A, B, n = gb.shape
    tb = _row_tile(B)

    def body(a_ref, b_ref, o_ref):
        o_ref[...] = (a_ref[...] + b_ref[...]).astype(bf16)

    return pl.pallas_call(
        body, name=name, grid=(3, A, B // tb),
        in_specs=[pl.BlockSpec((None, None, None, tb, n), lambda j, a, t: (0, j + 1, a, t, 0)),
                  pl.BlockSpec((None, None, tb, n), lambda j, a, t: (j + 1, a, t, 0))],
        out_specs=pl.BlockSpec((None, None, tb, n), lambda j, a, t: (j + 1, a, t, 0)),
        out_shape=jax.ShapeDtypeStruct((4, A, B, n), bf16),
        compiler_params=_params(("parallel", "parallel", "parallel")),
    )(ga, gb)


def assemble(gathered, axis, tk=256):
    _, A, K, n = gathered.shape
    if axis == 1:
        def body(w_ref, o_ref):
            o_ref[...] = jnp.concatenate([w_ref[j] for j in range(N_DEV)], axis=1)

        return pl.pallas_call(
            body, name=f"assemble_cols_{K}x{n}", grid=(A, K // tk),
            in_specs=[pl.BlockSpec((N_DEV, None, tk, n), lambda a, t: (0, a, t, 0))],
            out_specs=pl.BlockSpec((None, tk, N_DEV * n), lambda a, t: (a, t, 0)),
            out_shape=jax.ShapeDtypeStruct((A, K, N_DEV * n), gathered.dtype),
            compiler_params=_params(("parallel", "parallel")),
        )(gathered)

    def body(w_ref, o_ref):
        for j in range(N_DEV):
            o_ref[pl.ds(j * K, K), :] = w_ref[j]

    return pl.pallas_call(
        body, name=f"assemble_rows_{K}x{n}", grid=(A,),
        in_specs=[pl.BlockSpec((N_DEV, None, K, n), lambda a: (0, a, 0, 0))],
        out_specs=pl.BlockSpec((None, N_DEV * K, n), lambda a: (a, 0, 0)),
        out_shape=jax.ShapeDtypeStruct((A, N_DEV * K, n), gathered.dtype),
        compiler_params=_params(("parallel",)),
    )(gathered)


SSD_IN_PAD = 5248


def assemble_ssd_in(gathered, tk=256):
    _, A, K, n = gathered.shape

    def body(w_ref, z_ref, x_ref, dt_ref, full_ref):
        full_ref[:, pl.ds(SSD_IN_PAD - LANES, LANES)] = jnp.zeros((tk, LANES), gathered.dtype)
        for j in range(N_DEV):
            full_ref[:, pl.ds(j * n, n)] = w_ref[j]
        z_ref[...] = full_ref[:, pl.ds(0, SSD_INNER)]
        x_ref[...] = full_ref[:, pl.ds(SSD_INNER, SSD_CONV_DIM)]
        dt_ref[...] = full_ref[:, pl.ds(SSD_INNER + SSD_CONV_DIM, LANES)]

    widths = (SSD_INNER, SSD_CONV_DIM, LANES)
    return pl.pallas_call(
        body, name="assemble_ssd_in", grid=(A, K // tk),
        in_specs=[pl.BlockSpec((N_DEV, None, tk, n), lambda a, t: (0, a, t, 0))],
        out_specs=[pl.BlockSpec((None, tk, w), lambda a, t: (a, t, 0)) for w in widths],
        out_shape=[jax.ShapeDtypeStruct((A, K, w), gathered.dtype) for w in widths],
        scratch_shapes=[pltpu.VMEM((tk, SSD_IN_PAD), gathered.dtype)],
        compiler_params=_params(("parallel", "parallel")),
    )(gathered)


def ssd_in_to_shards(dwz, dwx, dwdt, buf, j, tk=256):
    K = dwz.shape[0]
    n = buf.shape[-1]

    def body(z_ref, x_ref, dt_ref, buf_ref, o_ref, full_ref):
        full_ref[:, pl.ds(0, SSD_INNER)] = z_ref[...]
        full_ref[:, pl.ds(SSD_INNER, SSD_CONV_DIM)] = x_ref[...]
        full_ref[:, pl.ds(SSD_INNER + SSD_CONV_DIM, LANES)] = dt_ref[...]
        my_c, my_chip = _my_core_and_chip()
        for d in range(N_DEV):
            o_ref[(d % 2) ^ my_c, (d // 2) ^ my_chip] = full_ref[:, pl.ds(d * n, n)]

    return pl.pallas_call(
        body, name="ssd_in_to_shards", grid=(K // tk,),
        in_specs=[_rows(tk, SSD_INNER), _rows(tk, SSD_CONV_DIM), _rows(tk, LANES), _ANY],
        out_specs=pl.BlockSpec((2, 4, None, tk, n), lambda t: (0, 0, j, t, 0)),
        out_shape=jax.ShapeDtypeStruct(buf.shape, f32),
        scratch_shapes=[pltpu.VMEM((tk, SSD_IN_PAD), f32)],
        input_output_aliases={3: 0},
        compiler_params=_params(("parallel",)),
    )(dwz, dwx, dwdt, buf)


def sum_over_devices(gathered):
    _, R, W = gathered.shape

    def body(g_ref, o_ref):
        acc = g_ref[0]
        for k in range(1, N_DEV):
            acc = acc + g_ref[k]
        o_ref[...] = acc

    return pl.pallas_call(
        body, name="sum_over_devices", grid=(1,),
        in_specs=[pl.BlockSpec((N_DEV, R, W), lambda i: (0, 0, 0))], out_specs=pl.BlockSpec((R, W), lambda i: (0, 0)),
        out_shape=jax.ShapeDtypeStruct((R, W), f32), compiler_params=_params(("arbitrary",)),
    )(gathered)


_ANY = pl.BlockSpec(memory_space=pl.ANY)


class _Exchange:
    def __init__(self, inputs, out_shapes, scratch, start, finish):
        self.inputs, self.out_shapes, self.scratch, self.start, self.finish = inputs, out_shapes, scratch, start, finish

    def run(self, name):
        ni, no = len(self.inputs), len(self.out_shapes)

        def body(*refs):
            parts = (refs[:ni], refs[ni:ni + no], refs[ni + no:])
            self.start(*parts)
            self.finish(*parts)

        return pl.pallas_call(body, name=name, in_specs=[_ANY] * ni, out_specs=[_ANY] * no, out_shape=self.out_shapes,
                              scratch_shapes=self.scratch)(*self.inputs)


def _carry(body, n_in, n_out, rider, first, last):
    if rider is None:
        return body
    ri, ro = len(rider.inputs), len(rider.out_shapes)

    def hosted(*refs):
        a, b, c = n_in + ri, n_in + ri + n_out, n_in + ri + n_out + ro
        rs = len(refs) - c - len(rider.scratch)
        parts = (refs[n_in:a], refs[b:c], refs[c + rs:])

        @pl.when(first())
        def _():
            rider.start(*parts)

        body(*refs[:n_in], *refs[a:b], *refs[c:c + rs])

        @pl.when(last())
        def _():
            rider.finish(*parts)

    return hosted


def _rider_specs(rider):
    if rider is None:
        return [], [], [], [], []
    return [_ANY] * len(rider.inputs), [_ANY] * len(rider.out_shapes), list(rider.out_shapes), list(rider.scratch), list(rider.inputs)


def all_gather(blocks):
    n = len(blocks)

    def plan(x_refs, out_refs, sems):
        send_sems, recv_sems, local_sems = sems
        x, y, c = lax.axis_index("x"), lax.axis_index("y"), lax.axis_index("c")
        me, sibling = (x, y, c), (x, y, 1 - c)
        chips = [(1 - x, y), (x, 1 - y), (1 - x, 1 - y)]

        def copy(a, k, blk, to, src=None):
            px, py, pc = blk
            slot = out_refs[a].at[4 * px + 2 * py + pc]
            return pltpu.make_async_remote_copy(
                src_ref=slot if src is None else src, dst_ref=slot,
                send_sem=send_sems.at[7 * a + k], recv_sem=recv_sems.at[7 * a + k], device_id=to, device_id_type=MESH)

        mine = [pltpu.make_async_copy(x_refs[a], out_refs[a].at[4 * x + 2 * y + c], local_sems.at[a]) for a in range(n)]
        first = []
        for a in range(n):
            first += [copy(a, 0, me, sibling, src=x_refs[a])] + [copy(a, 1 + j, me, (*chip, c), src=x_refs[a]) for j, chip in enumerate(chips)]
        return c, me, sibling, chips, copy, mine, first

    def start(x_refs, out_refs, sems):
        _, _, _, _, _, mine, first = plan(x_refs, out_refs, sems)
        for cp in mine + first:
            cp.start()

    def finish(x_refs, out_refs, sems):
        c, me, sibling, chips, copy, mine, first = plan(x_refs, out_refs, sems)
        passed = []
        for j, chip in enumerate(chips):
            for a in range(n):
                copy(a, 1 + j, (*chip, c), me).wait_recv()
                passed.append(copy(a, 4 + j, (*chip, c), sibling))
                passed[-1].start()
        for a in range(n):
            copy(a, 0, sibling, me).wait_recv()
            for j, chip in enumerate(chips):
                copy(a, 4 + j, (*chip, 1 - c), me).wait_recv()
        for cp in first + passed:
            cp.wait_send()
        for cp in mine:
            cp.wait()

    return _Exchange(list(blocks), [jax.ShapeDtypeStruct((N_DEV,) + b.shape, b.dtype) for b in blocks],
                     [pltpu.SemaphoreType.DMA((7 * n,)), pltpu.SemaphoreType.DMA((7 * n,)), pltpu.SemaphoreType.DMA((n,))], start, finish)


def exchange_with_sibling(gs):
    n = len(gs)

    def body(*refs):
        g_refs, recv_refs = refs[:n], refs[n:2 * n]
        send_sems, recv_sems = refs[2 * n:]
        x, y, c = lax.axis_index("x"), lax.axis_index("y"), lax.axis_index("c")
        cps = [pltpu.make_async_remote_copy(src_ref=g_refs[a].at[1], dst_ref=recv_refs[a], send_sem=send_sems.at[a],
                                            recv_sem=recv_sems.at[a], device_id=(x, y, 1 - c), device_id_type=MESH) for a in range(n)]
        for cp in cps:
            cp.start()
        for cp in cps:
            cp.wait()

    return pl.pallas_call(
        body, name="exchange_with_sibling", in_specs=[_ANY] * n, out_specs=[_ANY] * n,
        out_shape=[jax.ShapeDtypeStruct(g.shape[1:], g.dtype) for g in gs],
        scratch_shapes=[pltpu.SemaphoreType.DMA((n,)), pltpu.SemaphoreType.DMA((n,))],
    )(*gs)


def exchange_between_chips(parts):
    n = len(parts)

    def plan(p_refs, recv_refs, sems):
        send_sems, recv_sems = sems
        x, y, c = lax.axis_index("x"), lax.axis_index("y"), lax.axis_index("c")
        chips = [(2, (1 - x, y)), (1, (x, 1 - y)), (3, (1 - x, 1 - y))]
        return [pltpu.make_async_remote_copy(src_ref=p_refs[a].at[slot], dst_ref=recv_refs[a].at[k], send_sem=send_sems.at[3 * a + k],
                                             recv_sem=recv_sems.at[3 * a + k], device_id=(px, py, c), device_id_type=MESH)
                for a in range(n) for k, (slot, (px, py)) in enumerate(chips)]

    def start(*refs):
        for cp in plan(*refs):
            cp.start()

    def finish(*refs):
        for cp in plan(*refs):
            cp.wait()

    return _Exchange(list(parts), [jax.ShapeDtypeStruct((3,) + p.shape[1:], p.dtype) for p in parts],
                     [pltpu.SemaphoreType.DMA((3 * n,)), pltpu.SemaphoreType.DMA((3 * n,))], start, finish)


PARAMS = {
    "norm_w": ((DEPTH, 3, D_MODEL), 2),
    "ffn_w_gate": ((DEPTH, 2, D_MODEL, D_FF), 3),
    "ffn_w_up": ((DEPTH, 2, D_MODEL, D_FF), 3),
    "ffn_w_down": ((DEPTH, 2, D_FF, D_MODEL), 2),
    "ssd_w_in": ((2, D_MODEL, SSD_IN_DIM), 2),
    "ssd_conv_w": ((2, SSD_CONV_K, SSD_CONV_DIM), 2),
    "ssd_conv_b": ((2, SSD_CONV_DIM), None),
    "ssd_dt_bias": ((2, SSD_HEADS), None),
    "ssd_a_log": ((2, SSD_HEADS), None),
    "ssd_d": ((2, SSD_HEADS), None),
    "ssd_norm_w": ((2, SSD_INNER), None),
    "ssd_w_out": ((2, SSD_INNER, D_MODEL), 1),
    "sc_w_in": ((2, D_MODEL, 3 * D_MODEL), 2),
    "sc_conv_w": ((2, SC_CONV_K, D_MODEL), 2),
    "sc_w_out": ((2, D_MODEL, D_MODEL), 1),
    "final_norm_w": ((D_MODEL,), None),
}
NAMES = list(PARAMS)
BIG = ["ffn_w_gate", "ffn_w_up", "ffn_w_down", "ssd_w_in", "ssd_w_out", "sc_w_in", "sc_w_out"]
SMALL = [n for n in NAMES if n not in BIG]
SMALL_SHARDED = [n for n in SMALL if PARAMS[n][1] is not None]


def _round_up(n, m):
    return -(-n // m) * m


def _pack(flat_list, rows_multiple):
    flat = jnp.concatenate(flat_list)
    rows = _round_up(_round_up(flat.shape[0], PACK_W) // PACK_W, rows_multiple)
    return jnp.pad(flat, (0, rows * PACK_W - flat.shape[0])).reshape(rows, PACK_W)


def _unpack(packed, shapes, lead=()):
    flat = packed.reshape(lead + (-1,))
    out, off = [], 0
    for shp in shapes:
        n = 1
        for s in shp:
            n *= s
        out.append(flat[..., off:off + n].reshape(lead + tuple(shp)))
        off += n
    return out


def _local_shape(name):
    shp, ax = PARAMS[name]
    if ax is None:
        return shp
    return shp[:ax] + (shp[ax] // N_DEV,) + shp[ax + 1:]


def _full_from_gathered(g, name):
    shp, ax = PARAMS[name]
    return jnp.moveaxis(g, 0, ax).reshape(shp)


def _by_destination(full, name):
    shp, ax = PARAMS[name]
    loc = shp[ax] // N_DEV
    return jnp.moveaxis(full.reshape(shp[:ax] + (N_DEV, loc) + shp[ax + 1:]), ax, 0)


def _ssd_layer_fwd(xin, nw, p, rider=None):
    z, xbc, dt_raw = in_proj_fwd(xin, nw, [p["ssd_wz"], p["ssd_wx"], p["ssd_wdt"]], [bf16, bf16, f32])
    act, dt = ssd_conv_fwd(xbc, p["ssd_conv_w"], p["ssd_conv_b"], dt_raw, p["ssd_dt_bias"])
    T = xin.shape[0]
    dt4 = jnp.pad(dt[:, :SSD_HEADS].reshape(T, SSD_GROUPS, 8).transpose(1, 0, 2), ((0, 0), (0, 0), (0, LANES - 8)))
    y, states, *got = ssd_scan_fwd(act, dt4, p["ssd_alog4"], rider=rider)
    gn = ssd_gate_fwd(y, act, z, p["ssd_dx"], p["ssd_norm_w"])
    xout = out_proj_fwd(xin, gn, p["ssd_w_out"])
    return xout, (xin, z, xbc, dt_raw, act, dt4, y, states, gn), got


def _ssd_layer_bwd(dxo, nw, p, saved, gbuf, slab, rider=None):
    xin, z, xbc, dt_raw, act, dt4, y, states, gn = saved
    T = xin.shape[0]
    dgn, dyb = out_proj_bwd(dxo, p["ssd_w_out"])
    gbuf["ssd_w_out"] = tn_matmul_to_shards(gn, dyb, gbuf["ssd_w_out"], (slab,), 0)
    g = {}
    dy, dxs_skip, dz, dd_x, dgnw = ssd_gate_bwd(y, act, z, p["ssd_dx"], p["ssd_norm_w"], dgn)
    g["ssd_norm_w"] = dgnw[0]
    g["ssd_d"] = jnp.sum(dd_x.reshape(SSD_HEADS, SSD_HEAD_DIM), axis=1)
    dxs, db, dc, ddt4, dalog4, *got = ssd_scan_bwd(act, dt4, p["ssd_alog4"], states, dy, rider=rider)
    g["ssd_a_log"] = dalog4[:, 0, :8].reshape(SSD_HEADS)
    ddt = jnp.pad(ddt4[:, :, :8].transpose(1, 0, 2).reshape(T, SSD_HEADS), ((0, 0), (0, LANES - SSD_HEADS)))
    dxbc, ddt_raw, dcw, dcb, ddtb = ssd_conv_bwd(xbc, p["ssd_conv_w"], p["ssd_conv_b"], dt_raw, p["ssd_dt_bias"], dxs, dxs_skip, db, dc, ddt)
    g["ssd_conv_w"] = dcw[:SSD_CONV_K]
    g["ssd_conv_b"] = dcb[0]
    g["ssd_dt_bias"] = ddtb[0, :SSD_HEADS]
    dx, h, dnw = in_proj_bwd(xin, nw, dxo, [dz, dxbc, ddt_raw], [p["ssd_wz"], p["ssd_wx"], p["ssd_wdt"]])
    gbuf["ssd_w_in"] = ssd_in_to_shards(tn_matmul(h, dz), tn_matmul(h, dxbc), tn_matmul(h, ddt_raw), gbuf["ssd_w_in"], slab)
    return dx, dnw, g, got


def _sc_layer_fwd(xin, nw, p):
    (bcu,) = in_proj_fwd(xin, nw, [p["sc_w_in"]], [bf16])
    q = sc_mid_fwd(bcu, p["sc_conv_w"])
    return out_proj_fwd(xin, q, p["sc_w_out"]), (xin, bcu, q)


def _sc_layer_bwd(dxo, nw, p, saved, gbuf, slab):
    xin, bcu, q = saved
    dq, dyb = out_proj_bwd(dxo, p["sc_w_out"])
    gbuf["sc_w_out"] = tn_matmul_to_shards(q, dyb, gbuf["sc_w_out"], (slab,), 0)
    dbcu, dcw = sc_mid_bwd(bcu, p["sc_conv_w"], dq)
    g = {"sc_conv_w": dcw[:SC_CONV_K]}
    dx, h, dnw = in_proj_bwd(xin, nw, dxo, [dbcu], [p["sc_w_in"]])
    gbuf["sc_w_in"] = tn_matmul_to_shards(h, dbcu, gbuf["sc_w_in"], (slab,), 1)
    return dx, dnw, g


def kernel(x, norm_w, ffn_w_gate, ffn_w_up, ffn_w_down, ssd_w_in, ssd_conv_w, ssd_conv_b, ssd_dt_bias, ssd_a_log, ssd_d, ssd_norm_w, ssd_w_out, sc_w_in, sc_conv_w, sc_w_out, final_norm_w, loss_target, m_norm_w, m_ffn_w_gate, m_ffn_w_up, m_ffn_w_down, m_ssd_w_in, m_ssd_conv_w, m_ssd_conv_b, m_ssd_dt_bias, m_ssd_a_log, m_ssd_d, m_ssd_norm_w, m_ssd_w_out, m_sc_w_in, m_sc_conv_w, m_sc_w_out, m_final_norm_w, v_norm_w, v_ffn_w_gate, v_ffn_w_up, v_ffn_w_down, v_ssd_w_in, v_ssd_conv_w, v_ssd_conv_b, v_ssd_dt_bias, v_ssd_a_log, v_ssd_d, v_ssd_norm_w, v_ssd_w_out, v_sc_w_in, v_sc_conv_w, v_sc_w_out, v_final_norm_w):
    w_loc = dict(zip(NAMES, (norm_w, ffn_w_gate, ffn_w_up, ffn_w_down, ssd_w_in, ssd_conv_w, ssd_conv_b, ssd_dt_bias, ssd_a_log, ssd_d, ssd_norm_w, ssd_w_out, sc_w_in, sc_conv_w, sc_w_out, final_norm_w)))
    m_loc = dict(zip(NAMES, (m_norm_w, m_ffn_w_gate, m_ffn_w_up, m_ffn_w_down, m_ssd_w_in, m_ssd_conv_w, m_ssd_conv_b, m_ssd_dt_bias, m_ssd_a_log, m_ssd_d, m_ssd_norm_w, m_ssd_w_out, m_sc_w_in, m_sc_conv_w, m_sc_w_out, m_final_norm_w)))
    v_loc = dict(zip(NAMES, (v_norm_w, v_ffn_w_gate, v_ffn_w_up, v_ffn_w_down, v_ssd_w_in, v_ssd_conv_w, v_ssd_conv_b, v_ssd_dt_bias, v_ssd_a_log, v_ssd_d, v_ssd_norm_w, v_ssd_w_out, v_sc_w_in, v_sc_conv_w, v_sc_w_out, v_final_norm_w)))
    ax, ay, ac = lax.axis_index("x"), lax.axis_index("y"), lax.axis_index("c")
    my_chip = 2 * ax + ay
    my_dev = 4 * ax + 2 * ay + ac
    T = x.shape[1]

    def as3d(a):
        return a.reshape((-1,) + a.shape[-2:])

    wb = {n: as3d(w_loc[n]).astype(bf16) for n in BIG}

    def layer_names(i):
        return ["ffn_w_gate", "ffn_w_up", "ffn_w_down"] + (["ssd_w_in", "ssd_w_out"] if i % 2 == 0 else ["sc_w_in", "sc_w_out"])

    def layer_blocks(i):
        return [wb[n][2 * i:2 * i + 2] if n.startswith("ffn") else wb[n][i // 2:i // 2 + 1] for n in layer_names(i)]

    def layer_weights(i, gathered):
        q = {}
        for n, g in zip(layer_names(i), gathered):
            if n == "ssd_w_in":
                wz, wx, wdt = assemble_ssd_in(g)
                q["ssd_wz"], q["ssd_wx"], q["ssd_wdt"] = wz[0], wx[0], wdt[0]
            elif n.startswith("ffn"):
                q[n] = assemble(g, 1 if PARAMS[n][1] == len(PARAMS[n][0]) - 1 else 0)
            else:
                q[n] = assemble(g, 1 if PARAMS[n][1] == len(PARAMS[n][0]) - 1 else 0)[0]
        return q

    ss_shapes = [_local_shape(n) for n in SMALL_SHARDED]
    gathered0 = all_gather(layer_blocks(0) + [_pack([w_loc[n].reshape(-1) for n in SMALL_SHARDED], 8)]).run("all_gather_layer0")
    full = {}
    for n, part in zip(SMALL_SHARDED, _unpack(gathered0[-1], ss_shapes, lead=(N_DEV,))):
        full[n] = _full_from_gathered(part, n)
    for n in SMALL:
        if PARAMS[n][1] is None:
            full[n] = w_loc[n]
    small = {
        "ssd_conv_w": full["ssd_conv_w"],
        "ssd_conv_b": full["ssd_conv_b"].reshape(2, 1, SSD_CONV_DIM),
        "ssd_dt_bias": jnp.pad(full["ssd_dt_bias"], ((0, 0), (0, LANES - SSD_HEADS))).reshape(2, 1, LANES),
        "ssd_alog4": jnp.pad(full["ssd_a_log"].reshape(2, SSD_GROUPS, 1, 8), ((0, 0), (0, 0), (0, 0), (0, LANES - 8))),
        "ssd_dx": jnp.repeat(full["ssd_d"], SSD_HEAD_DIM, axis=1).reshape(2, 1, SSD_INNER),
        "ssd_norm_w": full["ssd_norm_w"].reshape(2, 1, SSD_INNER),
        "sc_conv_w": full["sc_conv_w"],
    }
    nw_all = full["norm_w"].reshape(DEPTH, 3, 1, D_MODEL)

    def with_small(q, i):
        q.update({n: v[i // 2] for n, v in small.items() if n.startswith("ssd" if i % 2 == 0 else "sc")})
        return q

    xc = x[0]
    saved = []
    lw = [with_small(layer_weights(0, gathered0[:-1]), 0), None, None, None]
    for i in range(DEPTH):
        q = lw[i]
        riders = [all_gather(layer_blocks(r)) for r in (1, 2, 3)] if i == 0 else [None, None, None]
        x1, g1, u1, a1, *got1 = ffn_fwd(xc, nw_all[i, 0], q["ffn_w_gate"], q["ffn_w_up"], q["ffn_w_down"], (0,), rider=riders[0])
        if i % 2 == 0:
            x2, mix_saved, got2 = _ssd_layer_fwd(x1, nw_all[i, 1], q, rider=riders[1])
        else:
            x2, mix_saved = _sc_layer_fwd(x1, nw_all[i, 1], q)
        x3, g3, u3, a3, *got3 = ffn_fwd(x2, nw_all[i, 2], q["ffn_w_gate"], q["ffn_w_up"], q["ffn_w_down"], (1,), rider=riders[2])
        if i == 0:
            for r, got in zip((1, 2, 3), (got1, got2, got3)):
                lw[r] = with_small(layer_weights(r, got), r)
        saved.append(((xc, g1, u1, a1), mix_saved, (x2, g3, u3, a3)))
        xc = x3

    loss_row, dx, dfw = loss_head(xc, full["final_norm_w"].reshape(1, D_MODEL), loss_target[0])
    loss = lax.psum(loss_row[0, 0], ("x", "y", "c"))

    grads = {n: [None] * PARAMS[n][0][0] for n in SMALL if n != "final_norm_w"}
    grads["final_norm_w"] = dfw[0]
    dnorm = [[None] * 3 for _ in range(DEPTH)]
    def slabs(n, late):
        if n.startswith("ffn"):
            return (0, 2) if late else (2, 6)
        if n.startswith("ssd"):
            return (0, 1) if late else (1, 1)
        return (0, 0) if late else (0, 2)

    def new_bufs(late):
        return {n: jnp.zeros((2, 4, slabs(n, late)[1]) + wb[n].shape[1:], f32) for n in BIG if slabs(n, late)[1]}

    gb_early, gb_late = new_bufs(False), new_bufs(True)

    def ffn_back(i, k, dxo, sv):
        xin, g_, u_, a_ = sv
        q = lw[i]
        gbuf = gb_late if i == 0 else gb_early
        slab = 2 * i + k - slabs("ffn_w_gate", i == 0)[0]
        dxi, h, dyb, dg, du, dnw = ffn_bwd_dx(xin, dxo, g_, u_, nw_all[i, 2 * k], q["ffn_w_gate"], q["ffn_w_up"], q["ffn_w_down"], (k,))
        dnorm[i][2 * k] = dnw[0]
        gbuf["ffn_w_gate"] = tn_matmul_to_shards(h, dg, gbuf["ffn_w_gate"], (slab,), 1)
        gbuf["ffn_w_up"] = tn_matmul_to_shards(h, du, gbuf["ffn_w_up"], (slab,), 1)
        gbuf["ffn_w_down"] = tn_matmul_to_shards(a_, dyb, gbuf["ffn_w_down"], (slab,), 0)
        return dxi

    def reduce_in_chip(gbuf):
        names = list(gbuf)
        bufs = [gbuf[n] for n in names]
        from_sibling = exchange_with_sibling(bufs)
        return names, bufs, from_sibling, [pair_sum_bf16(g, fs, "pair_sum_" + n) for n, g, fs in zip(names, bufs, from_sibling)]

    early = None
    for i in reversed(range(DEPTH)):
        j = i // 2
        sv_a, sv_mix, sv_b = saved[i]
        dx = ffn_back(i, 1, dx, sv_b)
        if i % 2 == 0:
            rider = exchange_between_chips(early[3]) if i == 0 else None
            dx, dnw, gm, got = _ssd_layer_bwd(dx, nw_all[i, 1], lw[i], sv_mix, gb_late if i == 0 else gb_early, 0, rider=rider)
            if i == 0:
                early_from_chips = got
        else:
            dx, dnw, gm = _sc_layer_bwd(dx, nw_all[i, 1], lw[i], sv_mix, gb_early, j)
        dnorm[i][1] = dnw[0]
        for n, val in gm.items():
            grads[n][j] = val
        dx = ffn_back(i, 0, dx, sv_a)
        if i == 1:
            early = reduce_in_chip(gb_early)

    grads["norm_w"] = jnp.stack([jnp.stack(r) for r in dnorm])
    for n in SMALL:
        if isinstance(grads[n], list):
            grads[n] = jnp.stack(grads[n])

    late = reduce_in_chip(gb_late)
    late_from_chips = exchange_between_chips(late[3]).run("exchange_between_chips")
    results = [{}, {}, {}, {}]
    outs = {}
    for (names, bufs, from_sibling, _), from_chips, is_late in ((late, late_from_chips, True), (early, early_from_chips, False)):
        for n, g, fs, fc in zip(names, bufs, from_sibling, from_chips):
            parts = [((0, 0), g), ((0,), fs), ((0,), fc), ((1,), fc), ((2,), fc)]
            outs[n] = adamw(parts, as3d(w_loc[n]), as3d(m_loc[n]), as3d(v_loc[n]), name="adamw_" + n + ("_late" if is_late else "_early"),
                            a0=slabs(n, is_late)[0], prev=outs.get(n))
    for n in BIG:
        for k in range(4):
            results[k][n] = outs[n][k].reshape(_local_shape(n))

    g_small = _pack([grads[n].reshape(-1) for n in SMALL], 8)
    g_small = sum_over_devices(all_gather([g_small]).run("all_gather_small_grads")[0])
    g_small_full = dict(zip(SMALL, _unpack(g_small, [PARAMS[n][0] for n in SMALL])))
    g_small_loc = []
    for n in SMALL:
        if PARAMS[n][1] is None:
            g_small_loc.append(g_small_full[n])
        else:
            g_small_loc.append(lax.dynamic_index_in_dim(_by_destination(g_small_full[n], n), my_dev, axis=0, keepdims=False))
    small_shapes = [_local_shape(n) for n in SMALL]
    pack_small = lambda d: _pack([d[n].reshape(-1) for n in SMALL], 8)[None]
    small_out = adamw([_pack([gl.reshape(-1) for gl in g_small_loc], 8)[None]], pack_small(w_loc), pack_small(m_loc), pack_small(v_loc), name="adamw_small")
    for k in range(4):
        results[k].update(zip(SMALL, _unpack(small_out[k], small_shapes)))
    return (loss, dx[None], *[results[0][n] for n in NAMES], *[results[1][n] for n in NAMES],
            *[results[2][n] for n in NAMES], *[results[3][n] for n in NAMES])
```

```python
import functools

import jax
import jax.numpy as jnp
from jax import lax
from jax.experimental import pallas as pl
from jax.experimental.pallas import tpu as pltpu

f32 = jnp.float32
bf16 = jnp.bfloat16

D_MODEL = 1024
D_FF = 2816
DEPTH = 4
SSD_INNER = 2048
SSD_HEADS = 32
SSD_HEAD_DIM = 64
SSD_GROUPS = 4
SSD_STATE = 128
SSD_CONV_K = 4
SSD_CONV_DIM = 3072
SSD_IN_DIM = 5152
SSD_CHUNK = 128
SC_CONV_K = 3
RMS_EPS = 1e-5
N_DEV = 8
LANES = 128
HALO = 16
PACK_W = 1024
PACK_TILE = 256
VMEM_LIMIT = 56 * 1024 * 1024
NEG_BIG = -1e30

ADAM_LR = 0.001
ADAM_B1 = 0.9
ADAM_B2 = 0.999
ADAM_EPS = 1e-08
ADAM_WD = 0.01
ADAM_STEP = 10

NT_DIMS = (((1,), (1,)), ((), ()))
TN_DIMS = (((0,), (0,)), ((), ()))
MESH = pl.DeviceIdType.MESH


def _params(sem=None):
    return pltpu.CompilerParams(dimension_semantics=sem, vmem_limit_bytes=VMEM_LIMIT)


def _resident(shape):
    nd = len(shape)
    return pl.BlockSpec(tuple(shape), lambda *_: (0,) * nd, pipeline_mode=pl.Buffered(1))


def _rows(tm, width):
    return pl.BlockSpec((tm, width), lambda i: (i, 0))


def _my_core_and_chip():
    return lax.axis_index("c"), 2 * lax.axis_index("x") + lax.axis_index("y")


def _sigmoid(v):
    return 1.0 / (1.0 + jnp.exp(-v))


def _softplus(v):
    return jnp.maximum(v, 0.0) + jnp.log(1.0 + jnp.exp(-jnp.abs(v)))


def _rms_fwd(xv, w):
    inv = lax.rsqrt(jnp.mean(xv * xv, axis=-1, keepdims=True) + RMS_EPS)
    xh = xv * inv
    return xh * w, xh, inv


def _rms_bwd(dh, xh, inv, w):
    dxh = dh * w
    dx = inv * (dxh - xh * jnp.mean(dxh * xh, axis=-1, keepdims=True))
    return dx, jnp.sum(dh * xh, axis=0, keepdims=True)


def _mm(a, b):
    return jnp.dot(a, b, preferred_element_type=f32)


def _mm_nt(a, b):
    return lax.dot_general(a, b, NT_DIMS, preferred_element_type=f32)


def _mm_tn(a, b):
    return lax.dot_general(a, b, TN_DIMS, preferred_element_type=f32)


FFN_CHUNK = D_FF


def _ffn_chunks():
    return [(c0, min(FFN_CHUNK, D_FF - c0)) for c0 in range(0, D_FF, FFN_CHUNK)]


def _layer_slab(w, idx):
    tail = w.shape[len(idx):]
    return pl.BlockSpec((None,) * len(idx) + tuple(tail), lambda *_: tuple(idx) + (0,) * len(tail), pipeline_mode=pl.Buffered(1))


def ffn_fwd(x, nw, wg, wu, wd, idx, tm=512, rider=None):
    T = x.shape[0]
    nt = T // tm
    r_in, r_out, r_shapes, r_scratch, r_args = _rider_specs(rider)

    def body(x_ref, nw_ref, wg_ref, wu_ref, wd_ref, xo_ref, g_ref, u_ref, a_ref):
        xv = x_ref[...]
        h, _, _ = _rms_fwd(xv, nw_ref[...])
        hb = h.astype(bf16)
        y = None
        for c0, fc in _ffn_chunks():
            cols = pl.ds(c0, fc)
            g = _mm(hb, wg_ref[:, cols])
            u = _mm(hb, wu_ref[:, cols])
            ab = (g * _sigmoid(g) * u).astype(bf16)
            g_ref[:, cols] = g.astype(bf16)
            u_ref[:, cols] = u.astype(bf16)
            a_ref[:, cols] = ab
            part = _mm(ab, wd_ref[cols, :])
            y = part if y is None else y + part
        xo_ref[...] = xv + 0.5 * y

    hosted = _carry(body, 5, 4, rider, lambda: pl.program_id(0) == 0, lambda: pl.program_id(0) == nt - 1)
    return pl.pallas_call(
        hosted, name="ffn_fwd" if rider is None else "ffn_fwd_carrying", grid=(nt,),
        in_specs=[_rows(tm, D_MODEL), _resident((1, D_MODEL)), _layer_slab(wg, idx), _layer_slab(wu, idx), _layer_slab(wd, idx)] + r_in,
        out_specs=[_rows(tm, D_MODEL), _rows(tm, D_FF), _rows(tm, D_FF), _rows(tm, D_FF)] + r_out,
        out_shape=[jax.ShapeDtypeStruct((T, D_MODEL), f32)] + [jax.ShapeDtypeStruct((T, D_FF), bf16)] * 3 + r_shapes,
        scratch_shapes=r_scratch,
        compiler_params=_params(("parallel",) if rider is None else ("arbitrary",)),
    )(x, nw, wg, wu, wd, *r_args)


def ffn_bwd_dx(x, dxo, g, u, nw, wg, wu, wd, idx, tm=256, rider=None):
    T = x.shape[0]
    nt = T // tm
    r_in, r_out, r_shapes, r_scratch, r_args = _rider_specs(rider)

    def body(x_ref, dxo_ref, g_ref, u_ref, nw_ref, wg_ref, wu_ref, wd_ref, dx_ref, h_ref, dy_ref, dg_ref, du_ref, dnw_ref):
        w = nw_ref[...]
        h, xh, inv = _rms_fwd(x_ref[...], w)
        dxo_v = dxo_ref[...]
        dyb = (0.5 * dxo_v).astype(bf16)
        dh = None
        for c0, fc in _ffn_chunks():
            cols = pl.ds(c0, fc)
            da = _mm_nt(dyb, wd_ref[cols, :])
            gv = g_ref[:, cols].astype(f32)
            uv = u_ref[:, cols].astype(f32)
            s = _sigmoid(gv)
            dgb = (da * uv * (s * (1.0 + gv * (1.0 - s)))).astype(bf16)
            dub = (da * (gv * s)).astype(bf16)
            dg_ref[:, cols] = dgb
            du_ref[:, cols] = dub
            part = _mm_nt(dgb, wg_ref[:, cols]) + _mm_nt(dub, wu_ref[:, cols])
            dh = part if dh is None else dh + part
        dxn, dw = _rms_bwd(dh, xh, inv, w)
        dx_ref[...] = dxo_v + dxn
        h_ref[...] = h.astype(bf16)
        dy_ref[...] = dyb

        @pl.when(pl.program_id(0) == 0)
        def _():
            dnw_ref[...] = jnp.zeros_like(dnw_ref)

        dnw_ref[...] += dw

    hosted = _carry(body, 8, 6, rider, lambda: pl.program_id(0) == 0, lambda: pl.program_id(0) == nt - 1)
    return pl.pallas_call(
        hosted, name="ffn_bwd_dx" if rider is None else "ffn_bwd_dx_carrying", grid=(nt,),
        in_specs=[_rows(tm, D_MODEL), _rows(tm, D_MODEL), _rows(tm, D_FF), _rows(tm, D_FF), _resident((1, D_MODEL)),
                  _layer_slab(wg, idx), _layer_slab(wu, idx), _layer_slab(wd, idx)] + r_in,
        out_specs=[_rows(tm, D_MODEL), _rows(tm, D_MODEL), _rows(tm, D_MODEL), _rows(tm, D_FF), _rows(tm, D_FF),
                   pl.BlockSpec((1, D_MODEL), lambda i: (0, 0))] + r_out,
        out_shape=[jax.ShapeDtypeStruct((T, D_MODEL), f32), jax.ShapeDtypeStruct((T, D_MODEL), bf16), jax.ShapeDtypeStruct((T, D_MODEL), bf16),
                   jax.ShapeDtypeStruct((T, D_FF), bf16), jax.ShapeDtypeStruct((T, D_FF), bf16), jax.ShapeDtypeStruct((1, D_MODEL), f32)] + r_shapes,
        scratch_shapes=r_scratch,
        compiler_params=_params(("arbitrary",)),
    )(x, dxo, g, u, nw, wg, wu, wd, *r_args)


def tn_matmul(a, b, tk=512):
    T, M = a.shape
    N = b.shape[1]
    bn = N if M * N <= 3_200_000 else N // 2
    nk = T // tk

    def body(a_ref, b_ref, o_ref):
        @pl.when(pl.program_id(1) == 0)
        def _():
            o_ref[...] = jnp.zeros_like(o_ref)

        o_ref[...] += _mm_tn(a_ref[...], b_ref[...])

    return pl.pallas_call(
        body, name=f"tn_matmul_{M}x{N}", grid=(N // bn, nk),
        in_specs=[pl.BlockSpec((tk, M), lambda j, k: (k, 0)), pl.BlockSpec((tk, bn), lambda j, k: (k, j))],
        out_specs=pl.BlockSpec((M, bn), lambda j, k: (0, j)),
        out_shape=jax.ShapeDtypeStruct((M, N), f32),
        compiler_params=_params(("parallel", "arbitrary")),
    )(a, b)


def tn_matmul_to_shards(a, b, buf, idx, axis, tk=512):
    T, M = a.shape
    N = b.shape[1]
    m, n = buf.shape[-2:]
    nk = T // tk

    def body(a_ref, b_ref, buf_ref, o_ref, acc_ref):
        k = pl.program_id(0)

        @pl.when(k == 0)
        def _():
            acc_ref[...] = jnp.zeros_like(acc_ref)

        acc_ref[...] += _mm_tn(a_ref[...], b_ref[...])

        @pl.when(k == nk - 1)
        def _():
            my_c, my_chip = _my_core_and_chip()
            for d in range(N_DEV):
                piece = acc_ref[:, pl.ds(d * n, n)] if axis == 1 else acc_ref[pl.ds(d * m, m), :]
                o_ref[(d % 2) ^ my_c, (d // 2) ^ my_chip] = piece

    none = (None,) * len(idx)
    return pl.pallas_call(
        body, name=f"tn_matmul_to_shards_{M}x{N}_{axis}", grid=(nk,),
        in_specs=[pl.BlockSpec((tk, M), lambda k: (k, 0)), pl.BlockSpec((tk, N), lambda k: (k, 0)), _ANY],
        out_specs=pl.BlockSpec((2, 4) + none + (m, n), lambda k: (0, 0) + tuple(idx) + (0, 0)),
        out_shape=jax.ShapeDtypeStruct(buf.shape, f32),
        scratch_shapes=[pltpu.VMEM((M, N), f32)],
        input_output_aliases={2: 0},
        compiler_params=_params(("arbitrary",)),
    )(a, b, buf)


def in_proj_fwd(x, nw, ws, out_dtypes, tm=256):
    T = x.shape[0]
    n = len(ws)

    def body(*refs):
        x_ref, nw_ref = refs[:2]
        w_refs = refs[2:2 + n]
        o_refs = refs[2 + n:]
        h, _, _ = _rms_fwd(x_ref[...], nw_ref[...])
        hb = h.astype(bf16)
        for w_ref, o_ref in zip(w_refs, o_refs):
            o_ref[...] = _mm(hb, w_ref[...]).astype(o_ref.dtype)

    return pl.pallas_call(
        body, name="in_proj_fwd_" + "_".join(str(w.shape[1]) for w in ws), grid=(T // tm,),
        in_specs=[_rows(tm, D_MODEL), _resident((1, D_MODEL))] + [_resident(w.shape) for w in ws],
        out_specs=[_rows(tm, w.shape[1]) for w in ws],
        out_shape=[jax.ShapeDtypeStruct((T, w.shape[1]), dt) for w, dt in zip(ws, out_dtypes)],
        compiler_params=_params(("parallel",)),
    )(x, nw, *ws)


def in_proj_bwd(x, nw, dxo, dys, ws, tm=256):
    T = x.shape[0]
    n = len(ws)

    def body(*refs):
        x_ref, nw_ref, dxo_ref = refs[:3]
        dy_refs = refs[3:3 + n]
        w_refs = refs[3 + n:3 + 2 * n]
        dx_ref, h_ref, dnw_ref = refs[3 + 2 * n:]
        w = nw_ref[...]
        h, xh, inv = _rms_fwd(x_ref[...], w)
        dh = _mm_nt(dy_refs[0][...], w_refs[0][...])
        for dy_ref, w_ref in zip(dy_refs[1:], w_refs[1:]):
            dh = dh + _mm_nt(dy_ref[...], w_ref[...])
        dxn, dw = _rms_bwd(dh, xh, inv, w)
        dx_ref[...] = dxo_ref[...] + dxn
        h_ref[...] = h.astype(bf16)

        @pl.when(pl.program_id(0) == 0)
        def _():
            dnw_ref[...] = jnp.zeros_like(dnw_ref)

        dnw_ref[...] += dw

    return pl.pallas_call(
        body, name="in_proj_bwd_" + "_".join(str(w.shape[1]) for w in ws), grid=(T // tm,),
        in_specs=[_rows(tm, D_MODEL), _resident((1, D_MODEL)), _rows(tm, D_MODEL)] + [_rows(tm, w.shape[1]) for w in ws]
        + [_resident(w.shape) for w in ws],
        out_specs=[_rows(tm, D_MODEL), _rows(tm, D_MODEL), pl.BlockSpec((1, D_MODEL), lambda i: (0, 0))],
        out_shape=[jax.ShapeDtypeStruct((T, D_MODEL), f32), jax.ShapeDtypeStruct((T, D_MODEL), bf16), jax.ShapeDtypeStruct((1, D_MODEL), f32)],
        compiler_params=_params(("arbitrary",)),
    )(x, nw, dxo, *dys, *ws)


def out_proj_fwd(x, a, w, tm=512):
    T = x.shape[0]
    K = a.shape[1]

    def body(x_ref, a_ref, w_ref, o_ref):
        o_ref[...] = x_ref[...] + _mm(a_ref[...], w_ref[...])

    return pl.pallas_call(
        body, name=f"out_proj_fwd_{K}", grid=(T // tm,),
        in_specs=[_rows(tm, D_MODEL), _rows(tm, K), _resident(w.shape)],
        out_specs=_rows(tm, D_MODEL), out_shape=jax.ShapeDtypeStruct((T, D_MODEL), f32),
        compiler_params=_params(("parallel",)),
    )(x, a, w)


def out_proj_bwd(dxo, w, tm=512):
    T = dxo.shape[0]
    K = w.shape[0]

    def body(dxo_ref, w_ref, da_ref, dy_ref):
        dyb = dxo_ref[...].astype(bf16)
        dy_ref[...] = dyb
        da_ref[...] = _mm_nt(dyb, w_ref[...]).astype(bf16)

    return pl.pallas_call(
        body, name=f"out_proj_bwd_{K}", grid=(T // tm,),
        in_specs=[_rows(tm, D_MODEL), _resident(w.shape)],
        out_specs=[_rows(tm, K), _rows(tm, D_MODEL)],
        out_shape=[jax.ShapeDtypeStruct((T, K), bf16), jax.ShapeDtypeStruct((T, D_MODEL), bf16)],
        compiler_params=_params(("parallel",)),
    )(dxo, w)


def _halo_spec(tm, width, n_tiles, reverse):
    per = tm // HALO

    def idx(i):
        t = (n_tiles - 1 - i) if reverse else i
        return (jnp.maximum(t * per - 1, 0), 0)

    return pl.BlockSpec((HALO, width), idx)


def _tile_spec(tm, width, n_tiles, reverse):
    if reverse:
        return pl.BlockSpec((tm, width), lambda i: (n_tiles - 1 - i, 0))
    return _rows(tm, width)


ROW_BLOCK = 64


def _strip(s):
    return pl.ds(pl.multiple_of(s * LANES, LANES), LANES)


def _conv_rows(ext_ref, w_ref, cols, k_w, r0):
    base = HALO - (k_w - 1) + r0
    wins = [ext_ref[pl.ds(base + k, ROW_BLOCK), :] for k in range(k_w)]
    out = w_ref[pl.ds(0, 1), cols] * wins[0]
    for k in range(1, k_w):
        out = out + w_ref[pl.ds(k, 1), cols] * wins[k]
    return out, wins


def _shifted_back(d_ref, w_ref, cols, k_w, r0):
    out = w_ref[pl.ds(0, 1), cols] * d_ref[pl.ds(r0 + k_w - 1, ROW_BLOCK), :]
    for k in range(1, k_w):
        out = out + w_ref[pl.ds(k, 1), cols] * d_ref[pl.ds(r0 + k_w - 1 - k, ROW_BLOCK), :]
    return out


def ssd_conv_fwd(xbc, conv_w, conv_b, dt_raw, dt_bias, tm=256):
    T = xbc.shape[0]
    nt = T // tm
    K = SSD_CONV_K

    def body(x_ref, halo_ref, w_ref, b_ref, dtr_ref, dtb_ref, act_ref, dt_ref, ext_ref):
        first = pl.program_id(0) == 0

        def strip(s, carry):
            cols = _strip(s)
            ext_ref[pl.ds(0, HALO), :] = jnp.where(first, 0.0, halo_ref[:, cols].astype(f32))
            ext_ref[pl.ds(HALO, tm), :] = x_ref[:, cols].astype(f32)
            for r0 in range(0, tm, ROW_BLOCK):
                pre, _ = _conv_rows(ext_ref, w_ref, cols, K, r0)
                pre = pre + b_ref[:, cols]
                act_ref[pl.ds(r0, ROW_BLOCK), cols] = (pre * _sigmoid(pre)).astype(bf16)
            return carry

        lax.fori_loop(0, SSD_CONV_DIM // LANES, strip, 0)
        dt = _softplus(dtr_ref[...] + dtb_ref[...])
        lane = lax.broadcasted_iota(jnp.int32, (1, LANES), 1)
        for g in range(SSD_GROUPS):
            dt_ref[g] = jnp.where(lane < 8, dt if g == 0 else pltpu.roll(dt, LANES - 8 * g, axis=1), 0.0)

    return pl.pallas_call(
        body, name="ssd_conv_fwd", grid=(nt,),
        in_specs=[_rows(tm, SSD_CONV_DIM), _halo_spec(tm, SSD_CONV_DIM, nt, False), _resident(conv_w.shape), _resident(conv_b.shape),
                  _rows(tm, LANES), _resident(dt_bias.shape)],
        out_specs=[_rows(tm, SSD_CONV_DIM), pl.BlockSpec((SSD_GROUPS, tm, LANES), lambda i: (0, i, 0))],
        out_shape=[jax.ShapeDtypeStruct((T, SSD_CONV_DIM), bf16), jax.ShapeDtypeStruct((SSD_GROUPS, T, LANES), f32)],
        scratch_shapes=[pltpu.VMEM((tm + HALO, LANES), f32)],
        compiler_params=_params(("parallel",)),
    )(xbc, xbc, conv_w, conv_b, dt_raw, dt_bias)


def ssd_conv_bwd(xbc, conv_w, conv_b, dt_raw, dt_bias, dxs_a, dxs_b, db, dc, ddt, tm=256):
    T = xbc.shape[0]
    nt = T // tm
    K = SSD_CONV_K

    def body(x_ref, halo_ref, w_ref, b_ref, dtr_ref, dtb_ref, da_ref, dbb_ref, db_ref, dc_ref, ddt_ref,
             dx_ref, ddtr_ref, dw_ref, dbias_ref, ddtb_ref, ext_ref, dpre_ref, carry_ref):
        i = pl.program_id(0)

        @pl.when(i == 0)
        def _():
            carry_ref[...] = jnp.zeros_like(carry_ref)
            dw_ref[...] = jnp.zeros_like(dw_ref)
            dbias_ref[...] = jnp.zeros_like(dbias_ref)
            ddtb_ref[...] = jnp.zeros_like(ddtb_ref)

        first_tile = i == nt - 1

        def run_strips(lo, hi, load_dact):
            def strip(s, carry):
                cols = _strip(s)
                ext_ref[pl.ds(0, HALO), :] = jnp.where(first_tile, 0.0, halo_ref[:, cols].astype(f32))
                ext_ref[pl.ds(HALO, tm), :] = x_ref[:, cols].astype(f32)
                dpre_ref[pl.ds(tm, 8), :] = carry_ref[:, cols]
                bias = b_ref[:, cols]
                dws = [jnp.zeros((1, LANES), f32) for _ in range(K)]
                dbs = jnp.zeros((1, LANES), f32)
                for r0 in range(0, tm, ROW_BLOCK):
                    pre, wins = _conv_rows(ext_ref, w_ref, cols, K, r0)
                    pre = pre + bias
                    sg = _sigmoid(pre)
                    dpre = load_dact(s, r0) * (sg * (1.0 + pre * (1.0 - sg)))
                    dpre_ref[pl.ds(r0, ROW_BLOCK), :] = dpre
                    dbs = dbs + jnp.sum(dpre, axis=0, keepdims=True)
                    for k in range(K):
                        dws[k] = dws[k] + jnp.sum(dpre * wins[k], axis=0, keepdims=True)
                carry_ref[:, cols] = dpre_ref[pl.ds(0, 8), :]
                for r0 in range(0, tm, ROW_BLOCK):
                    dx_ref[pl.ds(r0, ROW_BLOCK), cols] = _shifted_back(dpre_ref, w_ref, cols, K, r0).astype(bf16)
                for k in range(K):
                    dw_ref[pl.ds(k, 1), cols] += dws[k]
                dbias_ref[:, cols] += dbs
                return carry

            lax.fori_loop(lo, hi, strip, 0)

        rows = lambda r0: pl.ds(r0, ROW_BLOCK)
        n_x = SSD_INNER // LANES
        n_g = SSD_GROUPS * SSD_STATE // LANES
        run_strips(0, n_x, lambda s, r0: da_ref[rows(r0), _strip(s)].astype(f32) + dbb_ref[rows(r0), _strip(s)].astype(f32))
        run_strips(n_x, n_x + n_g, lambda s, r0: db_ref[rows(r0), _strip(s - n_x)].astype(f32))
        run_strips(n_x + n_g, n_x + 2 * n_g, lambda s, r0: dc_ref[rows(r0), _strip(s - n_x - n_g)].astype(f32))
        lane = lax.broadcasted_iota(jnp.int32, (1, LANES), 1)
        ddt = jnp.where(lane < 8, ddt_ref[0], 0.0)
        for g in range(1, SSD_GROUPS):
            ddt = ddt + pltpu.roll(jnp.where(lane < 8, ddt_ref[g], 0.0), 8 * g, axis=1)
        ddtr = ddt * _sigmoid(dtr_ref[...] + dtb_ref[...])
        ddtr_ref[...] = ddtr.astype(bf16)
        ddtb_ref[...] += jnp.sum(ddtr, axis=0, keepdims=True)

    rev = functools.partial(_tile_spec, tm, n_tiles=nt, reverse=True)
    const = lambda shape: pl.BlockSpec(shape, lambda i: (0, 0))
    return pl.pallas_call(
        body, name="ssd_conv_bwd", grid=(nt,),
        in_specs=[rev(width=SSD_CONV_DIM), _halo_spec(tm, SSD_CONV_DIM, nt, True), _resident(conv_w.shape), _resident(conv_b.shape),
                  rev(width=LANES), _resident(dt_bias.shape), rev(width=SSD_INNER), rev(width=SSD_INNER),
                  rev(width=SSD_GROUPS * SSD_STATE), rev(width=SSD_GROUPS * SSD_STATE),
                  pl.BlockSpec((SSD_GROUPS, tm, LANES), lambda i: (0, nt - 1 - i, 0))],
        out_specs=[rev(width=SSD_CONV_DIM), rev(width=LANES), const((8, SSD_CONV_DIM)), const((1, SSD_CONV_DIM)), const((1, LANES))],
        out_shape=[jax.ShapeDtypeStruct((T, SSD_CONV_DIM), bf16), jax.ShapeDtypeStruct((T, LANES), bf16),
                   jax.ShapeDtypeStruct((8, SSD_CONV_DIM), f32), jax.ShapeDtypeStruct((1, SSD_CONV_DIM), f32), jax.ShapeDtypeStruct((1, LANES), f32)],
        scratch_shapes=[pltpu.VMEM((tm + HALO, LANES), f32), pltpu.VMEM((tm + 8, LANES), f32), pltpu.VMEM((8, SSD_CONV_DIM), f32)],
        compiler_params=_params(("arbitrary",)),
    )(xbc, xbc, conv_w, conv_b, dt_raw, dt_bias, dxs_a, dxs_b, db, dc, ddt)


def _ssd_chunk(xs, bm, cm, dt, alog, st):
    L = SSD_CHUNK
    row = lax.broadcasted_iota(jnp.int32, (L, L), 0)
    col = lax.broadcasted_iota(jnp.int32, (L, L), 1)
    causal = row >= col
    tril = jnp.where(causal, 1.0, 0.0).astype(f32)
    lane = lax.broadcasted_iota(jnp.int32, (1, LANES), 1)
    sub = lax.broadcasted_iota(jnp.int32, (LANES, 1), 0)
    lo = lane < SSD_HEAD_DIM
    last_row = sub == L - 1

    dta = dt * (-jnp.exp(alog))
    a_cs = jnp.dot(tril, dta, precision=lax.Precision.HIGHEST, preferred_element_type=f32)
    a_cs_t = a_cs.T
    bmb = bm.astype(bf16)
    cmb = cm.astype(bf16)
    cb = _mm_nt(cmb, bmb)
    c_st = _mm(cmb, st.astype(bf16))

    def head_col(v, e):
        return jnp.sum(jnp.where(lane == e, v, 0.0), axis=1, keepdims=True)

    def head_row(v, e):
        return jnp.sum(jnp.where(sub == e, v, 0.0), axis=0, keepdims=True)

    ys, sts = [], []
    for j in range(4):
        e0, e1 = 2 * j, 2 * j + 1
        c0, c1 = head_col(a_cs, e0), head_col(a_cs, e1)
        acs_x = jnp.where(lo, c0, c1)
        dt_x = jnp.where(lo, head_col(dt, e0), head_col(dt, e1))
        xd = xs[:, j * LANES:(j + 1) * LANES] * dt_x
        m0 = cb * jnp.exp(jnp.where(causal, c0 - head_row(a_cs_t, e0), NEG_BIG))
        m1 = cb * jnp.exp(jnp.where(causal, c1 - head_row(a_cs_t, e1), NEG_BIG))
        mcat = jnp.concatenate([m0, m1], axis=1).astype(bf16)
        xcat = jnp.concatenate([jnp.where(lo, xd, 0.0), jnp.where(lo, 0.0, xd)], axis=0).astype(bf16)
        y_diag = _mm(mcat, xcat)
        a_last = jnp.sum(jnp.where(last_row, acs_x, 0.0), axis=0, keepdims=True)
        x_dec = (xd * jnp.exp(a_last - acs_x)).astype(bf16)
        s_new = _mm_tn(bmb, x_dec)
        y_off = c_st[:, j * LANES:(j + 1) * LANES] * jnp.exp(acs_x)
        ys.append(y_diag + y_off)
        sts.append(jnp.exp(a_last) * st[:, j * LANES:(j + 1) * LANES] + s_new)
    return jnp.concatenate(ys, axis=1), jnp.concatenate(sts, axis=1)


SCAN_GROUPS_FWD = 4
SCAN_GROUPS_BWD = 1


def _scan_specs(nc, reverse, gs):
    L = SSD_CHUNK
    ch = (lambda c: nc - 1 - c) if reverse else (lambda c: c)
    gw = SSD_INNER // SSD_GROUPS
    b0 = SSD_INNER // (gs * SSD_STATE)
    c0 = (SSD_INNER + SSD_GROUPS * SSD_STATE) // (gs * SSD_STATE)
    xs = pl.BlockSpec((L, gs * gw), lambda g, c: (ch(c), g))
    bm = pl.BlockSpec((L, gs * SSD_STATE), lambda g, c: (ch(c), b0 + g))
    cm = pl.BlockSpec((L, gs * SSD_STATE), lambda g, c: (ch(c), c0 + g))
    dt = pl.BlockSpec((gs, L, LANES), lambda g, c: (g, ch(c), 0))
    alog = pl.BlockSpec((gs, 1, LANES), lambda g, c: (g, 0, 0))
    st = pl.BlockSpec((gs, None, SSD_STATE, gw), lambda g, c: (g, ch(c), 0, 0))
    y = pl.BlockSpec((L, gs * gw), lambda g, c: (ch(c), g))
    grp = pl.BlockSpec((L, gs * SSD_STATE), lambda g, c: (ch(c), g))
    return xs, bm, cm, dt, alog, st, y, grp


def ssd_scan_fwd(act, dt4, alog4, rider=None):
    T = act.shape[0]
    nc = T // SSD_CHUNK
    gs = SCAN_GROUPS_FWD
    ng = SSD_GROUPS // gs
    gw = SSD_INNER // SSD_GROUPS
    xs_s, bm_s, cm_s, dt_s, alog_s, st_s, y_s, _ = _scan_specs(nc, False, gs)
    r_in, r_out, r_shapes, r_scratch, r_args = _rider_specs(rider)

    def body(xs_ref, bm_ref, cm_ref, dt_ref, alog_ref, y_ref, st_ref, st_scr):
        @pl.when(pl.program_id(1) == 0)
        def _():
            st_scr[...] = jnp.zeros_like(st_scr)

        for q in range(gs):
            xc, gc = pl.ds(q * gw, gw), pl.ds(q * SSD_STATE, SSD_STATE)
            st = st_scr[q]
            st_ref[q] = st
            y, st_new = _ssd_chunk(xs_ref[:, xc].astype(f32), bm_ref[:, gc].astype(f32), cm_ref[:, gc].astype(f32), dt_ref[q], alog_ref[q], st)
            y_ref[:, xc] = y.astype(bf16)
            st_scr[q] = st_new

    first = lambda: jnp.logical_and(pl.program_id(0) == 0, pl.program_id(1) == 0)
    last = lambda: jnp.logical_and(pl.program_id(0) == ng - 1, pl.program_id(1) == nc - 1)
    return pl.pallas_call(
        _carry(body, 5, 2, rider, first, last), name="ssd_scan_fwd" if rider is None else "ssd_scan_fwd_carrying", grid=(ng, nc),
        in_specs=[xs_s, bm_s, cm_s, dt_s, alog_s] + r_in, out_specs=[y_s, st_s] + r_out,
        out_shape=[jax.ShapeDtypeStruct((T, SSD_INNER), bf16), jax.ShapeDtypeStruct((SSD_GROUPS, nc, SSD_STATE, gw), f32)] + r_shapes,
        scratch_shapes=[pltpu.VMEM((gs, SSD_STATE, gw), f32)] + r_scratch,
        compiler_params=_params(("parallel" if rider is None else "arbitrary", "arbitrary")),
    )(act, act, act, dt4, alog4, *r_args)


def ssd_scan_bwd(act, dt4, alog4, states, dy, rider=None):
    T = act.shape[0]
    nc = T // SSD_CHUNK
    gs = SCAN_GROUPS_BWD
    ng = SSD_GROUPS // gs
    gw = SSD_INNER // SSD_GROUPS
    xs_s, bm_s, cm_s, dt_s, alog_s, st_s, y_s, grp_s = _scan_specs(nc, True, gs)
    r_in, r_out, r_shapes, r_scratch, r_args = _rider_specs(rider)

    def body(xs_ref, bm_ref, cm_ref, dt_ref, alog_ref, st_ref, dy_ref, dxs_ref, db_ref, dc_ref, ddt_ref, dalog_ref, dst_scr):
        @pl.when(pl.program_id(1) == 0)
        def _():
            dst_scr[...] = jnp.zeros_like(dst_scr)
            dalog_ref[...] = jnp.zeros_like(dalog_ref)

        for q in range(gs):
            xc, gc = pl.ds(q * gw, gw), pl.ds(q * SSD_STATE, SSD_STATE)
            _, vjp = jax.vjp(_ssd_chunk, xs_ref[:, xc].astype(f32), bm_ref[:, gc].astype(f32), cm_ref[:, gc].astype(f32),
                             dt_ref[q], alog_ref[q], st_ref[q])
            dxs, dbm, dcm, ddt, dalog, dst = vjp((dy_ref[:, xc].astype(f32), dst_scr[q]))
            dxs_ref[:, xc] = dxs.astype(bf16)
            db_ref[:, gc] = dbm.astype(bf16)
            dc_ref[:, gc] = dcm.astype(bf16)
            ddt_ref[q] = ddt
            dalog_ref[q] += dalog
            dst_scr[q] = dst

    first = lambda: jnp.logical_and(pl.program_id(0) == 0, pl.program_id(1) == 0)
    last = lambda: jnp.logical_and(pl.program_id(0) == ng - 1, pl.program_id(1) == nc - 1)
    return pl.pallas_call(
        _carry(body, 7, 5, rider, first, last), name="ssd_scan_bwd" if rider is None else "ssd_scan_bwd_carrying", grid=(ng, nc),
        in_specs=[xs_s, bm_s, cm_s, dt_s, alog_s, st_s, y_s] + r_in,
        out_specs=[y_s, grp_s, grp_s, dt_s, alog_s] + r_out,
        out_shape=[jax.ShapeDtypeStruct((T, SSD_INNER), bf16), jax.ShapeDtypeStruct((T, SSD_GROUPS * SSD_STATE), bf16),
                   jax.ShapeDtypeStruct((T, SSD_GROUPS * SSD_STATE), bf16), jax.ShapeDtypeStruct((SSD_GROUPS, T, LANES), f32),
                   jax.ShapeDtypeStruct((SSD_GROUPS, 1, LANES), f32)] + r_shapes,
        scratch_shapes=[pltpu.VMEM((gs, SSD_STATE, gw), f32)] + r_scratch,
        compiler_params=_params(("parallel" if rider is None else "arbitrary", "arbitrary")),
    )(act, act, act, dt4, alog4, states, dy, *r_args)


GATE_ROWS = 256


def _ssd_gate(y, xs, z, d_x, nw):
    g = (y + xs * d_x) * (z * _sigmoid(z))
    return g * lax.rsqrt(jnp.mean(g * g, axis=-1, keepdims=True) + RMS_EPS) * nw


def _gate_blocks(tm, fn):
    gw = SSD_INNER // SSD_GROUPS

    def block(r, carry):
        rows = pl.ds(r * GATE_ROWS if isinstance(r, int) else pl.multiple_of(r * GATE_ROWS, GATE_ROWS), GATE_ROWS)
        for k in range(SSD_GROUPS):
            fn(rows, pl.ds(k * gw, gw))
        return carry

    if tm == GATE_ROWS:
        block(0, 0)
    else:
        lax.fori_loop(0, tm // GATE_ROWS, block, 0)


def ssd_gate_fwd(y, act, z, d_x, nw, tm=256):
    T = y.shape[0]

    def body(y_ref, xs_ref, z_ref, d_ref, nw_ref, o_ref):
        def one(rows, cols):
            o_ref[rows, cols] = _ssd_gate(y_ref[rows, cols].astype(f32), xs_ref[rows, cols].astype(f32), z_ref[rows, cols].astype(f32),
                                          d_ref[:, cols], nw_ref[:, cols]).astype(bf16)

        _gate_blocks(tm, one)

    return pl.pallas_call(
        body, name="ssd_gate_fwd", grid=(T // tm,),
        in_specs=[_rows(tm, SSD_INNER), _rows(tm, SSD_INNER), _rows(tm, SSD_INNER), _resident(d_x.shape), _resident(nw.shape)],
        out_specs=_rows(tm, SSD_INNER), out_shape=jax.ShapeDtypeStruct((T, SSD_INNER), bf16),
        compiler_params=_params(("parallel",)),
    )(y, act, z, d_x, nw)


def ssd_gate_bwd(y, act, z, d_x, nw, dgn, tm=256):
    T = y.shape[0]

    def body(y_ref, xs_ref, z_ref, d_ref, nw_ref, dgn_ref, dy_ref, dxs_ref, dz_ref, dd_ref, dnw_ref):
        @pl.when(pl.program_id(0) == 0)
        def _():
            dd_ref[...] = jnp.zeros_like(dd_ref)
            dnw_ref[...] = jnp.zeros_like(dnw_ref)

        def one(rows, cols):
            _, vjp = jax.vjp(_ssd_gate, y_ref[rows, cols].astype(f32), xs_ref[rows, cols].astype(f32), z_ref[rows, cols].astype(f32),
                             d_ref[:, cols], nw_ref[:, cols])
            dy, dxs, dz, dd, dnw = vjp(dgn_ref[rows, cols].astype(f32))
            dy_ref[rows, cols] = dy.astype(bf16)
            dxs_ref[rows, cols] = dxs.astype(bf16)
            dz_ref[rows, cols] = dz.astype(bf16)
            dd_ref[:, cols] += dd
            dnw_ref[:, cols] += dnw

        _gate_blocks(tm, one)

    const = pl.BlockSpec((1, SSD_INNER), lambda i: (0, 0))
    return pl.pallas_call(
        body, name="ssd_gate_bwd", grid=(T // tm,),
        in_specs=[_rows(tm, SSD_INNER), _rows(tm, SSD_INNER), _rows(tm, SSD_INNER), _resident(d_x.shape), _resident(nw.shape), _rows(tm, SSD_INNER)],
        out_specs=[_rows(tm, SSD_INNER)] * 3 + [const, const],
        out_shape=[jax.ShapeDtypeStruct((T, SSD_INNER), bf16)] * 3 + [jax.ShapeDtypeStruct((1, SSD_INNER), f32)] * 2,
        compiler_params=_params(("arbitrary",)),
    )(y, act, z, d_x, nw, dgn)


def sc_mid_fwd(bcu, conv_w, tm=256):
    T = bcu.shape[0]
    nt = T // tm
    Dm = D_MODEL

    def body(x_ref, halo_ref, w_ref, q_ref, ext_ref):
        first = pl.program_id(0) == 0
        n_s = Dm // LANES

        def strip(s, carry):
            cols, c_cols, u_cols = _strip(s), _strip(s + n_s), _strip(s + 2 * n_s)
            ext_ref[pl.ds(0, HALO), :] = jnp.where(first, 0.0, halo_ref[:, c_cols].astype(f32) * halo_ref[:, u_cols].astype(f32))
            ext_ref[pl.ds(HALO, tm), :] = x_ref[:, c_cols].astype(f32) * x_ref[:, u_cols].astype(f32)
            for r0 in range(0, tm, ROW_BLOCK):
                rows = pl.ds(r0, ROW_BLOCK)
                v, _ = _conv_rows(ext_ref, w_ref, cols, SC_CONV_K, r0)
                q_ref[rows, cols] = (x_ref[rows, cols].astype(f32) * v).astype(bf16)
            return carry

        lax.fori_loop(0, n_s, strip, 0)

    return pl.pallas_call(
        body, name="sc_mid_fwd", grid=(nt,),
        in_specs=[_rows(tm, 3 * Dm), _halo_spec(tm, 3 * Dm, nt, False), _resident(conv_w.shape)],
        out_specs=_rows(tm, Dm), out_shape=jax.ShapeDtypeStruct((T, Dm), bf16),
        scratch_shapes=[pltpu.VMEM((tm + HALO, LANES), f32)],
        compiler_params=_params(("parallel",)),
    )(bcu, bcu, conv_w)


def sc_mid_bwd(bcu, conv_w, dq, tm=256):
    T = bcu.shape[0]
    nt = T // tm
    Dm = D_MODEL
    K = SC_CONV_K

    def body(x_ref, halo_ref, w_ref, dq_ref, dx_ref, dw_ref, ext_ref, dv_ref, carry_ref):
        i = pl.program_id(0)

        @pl.when(i == 0)
        def _():
            carry_ref[...] = jnp.zeros_like(carry_ref)
            dw_ref[...] = jnp.zeros_like(dw_ref)

        first_tile = i == nt - 1
        n_s = Dm // LANES

        def strip(s, carry):
            cols, c_cols, u_cols = _strip(s), _strip(s + n_s), _strip(s + 2 * n_s)
            ext_ref[pl.ds(0, HALO), :] = jnp.where(first_tile, 0.0, halo_ref[:, c_cols].astype(f32) * halo_ref[:, u_cols].astype(f32))
            ext_ref[pl.ds(HALO, tm), :] = x_ref[:, c_cols].astype(f32) * x_ref[:, u_cols].astype(f32)
            dv_ref[pl.ds(tm, 8), :] = carry_ref[:, cols]
            dws = [jnp.zeros((1, LANES), f32) for _ in range(K)]
            for r0 in range(0, tm, ROW_BLOCK):
                rows = pl.ds(r0, ROW_BLOCK)
                v, wins = _conv_rows(ext_ref, w_ref, cols, K, r0)
                dqv = dq_ref[rows, cols].astype(f32)
                dv = dqv * x_ref[rows, cols].astype(f32)
                dv_ref[rows, :] = dv
                dx_ref[rows, cols] = (dqv * v).astype(bf16)
                for k in range(K):
                    dws[k] = dws[k] + jnp.sum(dv * wins[k], axis=0, keepdims=True)
            carry_ref[:, cols] = dv_ref[pl.ds(0, 8), :]
            for r0 in range(0, tm, ROW_BLOCK):
                rows = pl.ds(r0, ROW_BLOCK)
                dp = _shifted_back(dv_ref, w_ref, cols, K, r0)
                dx_ref[rows, c_cols] = (dp * x_ref[rows, u_cols].astype(f32)).astype(bf16)
                dx_ref[rows, u_cols] = (dp * x_ref[rows, c_cols].astype(f32)).astype(bf16)
            for k in range(K):
                dw_ref[pl.ds(k, 1), cols] += dws[k]
            return carry

        lax.fori_loop(0, n_s, strip, 0)

    return pl.pallas_call(
        body, name="sc_mid_bwd", grid=(nt,),
        in_specs=[_tile_spec(tm, 3 * Dm, nt, True), _halo_spec(tm, 3 * Dm, nt, True), _resident(conv_w.shape), _tile_spec(tm, Dm, nt, True)],
        out_specs=[_tile_spec(tm, 3 * Dm, nt, True), pl.BlockSpec((8, Dm), lambda i: (0, 0))],
        out_shape=[jax.ShapeDtypeStruct((T, 3 * Dm), bf16), jax.ShapeDtypeStruct((8, Dm), f32)],
        scratch_shapes=[pltpu.VMEM((tm + HALO, LANES), f32), pltpu.VMEM((tm + 8, LANES), f32), pltpu.VMEM((8, Dm), f32)],
        compiler_params=_params(("arbitrary",)),
    )(bcu, bcu, conv_w, dq)


def loss_head(x, fw, target, tm=512):
    T = x.shape[0]

    def body(x_ref, fw_ref, t_ref, loss_ref, dx_ref, dfw_ref):
        @pl.when(pl.program_id(0) == 0)
        def _():
            loss_ref[...] = jnp.zeros_like(loss_ref)
            dfw_ref[...] = jnp.zeros_like(dfw_ref)

        w = fw_ref[...]
        y, xh, inv = _rms_fwd(x_ref[...], w)
        err = y - t_ref[...]
        loss_ref[...] += 0.5 * jnp.sum(jnp.mean(err * err, axis=-1, keepdims=True), axis=0, keepdims=True)
        dx, dw = _rms_bwd(err * (1.0 / D_MODEL), xh, inv, w)
        dx_ref[...] = dx
        dfw_ref[...] += dw

    return pl.pallas_call(
        body, name="loss_head", grid=(T // tm,),
        in_specs=[_rows(tm, D_MODEL), _resident((1, D_MODEL)), _rows(tm, D_MODEL)],
        out_specs=[pl.BlockSpec((1, LANES), lambda i: (0, 0)), _rows(tm, D_MODEL), pl.BlockSpec((1, D_MODEL), lambda i: (0, 0))],
        out_shape=[jax.ShapeDtypeStruct((1, LANES), f32), jax.ShapeDtypeStruct((T, D_MODEL), f32), jax.ShapeDtypeStruct((1, D_MODEL), f32)],
        compiler_params=_params(("arbitrary",)),
    )(x, fw, target)


def _row_tile(rows):
    return rows if rows <= 512 else 256


def adamw(g_parts, w, m, v, name="adamw", a0=0, prev=None):
    A, B, n = w.shape
    tb = _row_tile(B)
    n_parts = len(g_parts)
    arrays, specs = [], []
    for part in g_parts:
        lead, arr = part if isinstance(part, tuple) else ((), part)
        specs.append(pl.BlockSpec((None,) * (len(lead) + 1) + (tb, n), lambda a, t, lead=lead: tuple(lead) + (a, t, 0)))
        arrays.append(arr)
    na = arrays[0].shape[-3]
    prev = list(prev) if prev is not None else []

    def body(*refs):
        n = n_parts
        g_refs = refs[:n]
        w_ref, m_ref, v_ref = refs[n:n + 3]
        go_ref, d_ref, mo_ref, vo_ref = refs[n + 3 + len(prev):]
        g = g_refs[0][...].astype(f32)
        for r in g_refs[1:]:
            g = g + r[...].astype(f32)
        m_new = ADAM_B1 * m_ref[...] + (1.0 - ADAM_B1) * g
        v_new = ADAM_B2 * v_ref[...] + (1.0 - ADAM_B2) * (g * g)
        m_hat = m_new / (1.0 - ADAM_B1 ** ADAM_STEP)
        v_hat = v_new / (1.0 - ADAM_B2 ** ADAM_STEP)
        go_ref[...] = g
        d_ref[...] = -ADAM_LR * (m_hat / (jnp.sqrt(v_hat) + ADAM_EPS) + ADAM_WD * w_ref[...])
        mo_ref[...] = m_new
        vo_ref[...] = v_new

    plain = pl.BlockSpec((None, tb, n), lambda a, t: (a + a0, t, 0))
    return pl.pallas_call(
        body, name=name, grid=(na, B // tb), in_specs=specs + [plain] * 3 + [_ANY] * len(prev), out_specs=[plain] * 4,
        out_shape=[jax.ShapeDtypeStruct((A, B, n), f32)] * 4,
        input_output_aliases={n_parts + 3 + k: k for k in range(len(prev))},
        compiler_params=_params(("parallel", "parallel")),
    )(*arrays, w, m, v, *prev)


def pair_sum_bf16(ga, gb, name):
    _, A, B, n = gb.shape
    tb = _row_tile(B)

    def body(a_ref, b_ref, o_ref):
        o_ref[...] = (a_ref[...] + b_ref[...]).astype(bf16)

    return pl.pallas_call(
        body, name=name, grid=(3, A, B // tb),
        in_specs=[pl.BlockSpec((None, None, None, tb, n), lambda j, a, t: (0, j + 1, a, t, 0)),
                  pl.BlockSpec((None, None, tb, n), lambda j, a, t: (j + 1, a, t, 0))],
        out_specs=pl.BlockSpec((None, None, tb, n), lambda j, a, t: (j + 1, a, t, 0)),
        out_shape=jax.ShapeDtypeStruct((4, A, B, n), bf16),
        compiler_params=_params(("parallel", "parallel", "parallel")),
    )(ga, gb)


def assemble(gathered, axis, tk=256):
    _, A, K, n = gathered.shape
    if axis == 1:
        def body(w_ref, o_ref):
            o_ref[...] = jnp.concatenate([w_ref[j] for j in range(N_DEV)], axis=1)

        return pl.pallas_call(
            body, name=f"assemble_cols_{K}x{n}", grid=(A, K // tk),
            in_specs=[pl.BlockSpec((N_DEV, None, tk, n), lambda a, t: (0, a, t, 0))],
            out_specs=pl.BlockSpec((None, tk, N_DEV * n), lambda a, t: (a, t, 0)),
            out_shape=jax.ShapeDtypeStruct((A, K, N_DEV * n), gathered.dtype),
            compiler_params=_params(("parallel", "parallel")),
        )(gathered)

    def body(w_ref, o_ref):
        for j in range(N_DEV):
            o_ref[pl.ds(j * K, K), :] = w_ref[j]

    return pl.pallas_call(
        body, name=f"assemble_rows_{K}x{n}", grid=(A,),
        in_specs=[pl.BlockSpec((N_DEV, None, K, n), lambda a: (0, a, 0, 0))],
        out_specs=pl.BlockSpec((None, N_DEV * K, n), lambda a: (a, 0, 0)),
        out_shape=jax.ShapeDtypeStruct((A, N_DEV * K, n), gathered.dtype),
        compiler_params=_params(("parallel",)),
    )(gathered)


SSD_IN_PAD = 5248


def assemble_ssd_in(gathered, tk=256):
    _, A, K, n = gathered.shape

    def body(w_ref, z_ref, x_ref, dt_ref, full_ref):
        full_ref[:, pl.ds(SSD_IN_PAD - LANES, LANES)] = jnp.zeros((tk, LANES), gathered.dtype)
        for j in range(N_DEV):
            full_ref[:, pl.ds(j * n, n)] = w_ref[j]
        z_ref[...] = full_ref[:, pl.ds(0, SSD_INNER)]
        x_ref[...] = full_ref[:, pl.ds(SSD_INNER, SSD_CONV_DIM)]
        dt_ref[...] = full_ref[:, pl.ds(SSD_INNER + SSD_CONV_DIM, LANES)]

    widths = (SSD_INNER, SSD_CONV_DIM, LANES)
    return pl.pallas_call(
        body, name="assemble_ssd_in", grid=(A, K // tk),
        in_specs=[pl.BlockSpec((N_DEV, None, tk, n), lambda a, t: (0, a, t, 0))],
        out_specs=[pl.BlockSpec((None, tk, w), lambda a, t: (a, t, 0)) for w in widths],
        out_shape=[jax.ShapeDtypeStruct((A, K, w), gathered.dtype) for w in widths],
        scratch_shapes=[pltpu.VMEM((tk, SSD_IN_PAD), gathered.dtype)],
        compiler_params=_params(("parallel", "parallel")),
    )(gathered)


def ssd_in_to_shards(dwz, dwx, dwdt, buf, j, tk=256):
    K = dwz.shape[0]
    n = buf.shape[-1]

    def body(z_ref, x_ref, dt_ref, buf_ref, o_ref, full_ref):
        full_ref[:, pl.ds(0, SSD_INNER)] = z_ref[...]
        full_ref[:, pl.ds(SSD_INNER, SSD_CONV_DIM)] = x_ref[...]
        full_ref[:, pl.ds(SSD_INNER + SSD_CONV_DIM, LANES)] = dt_ref[...]
        my_c, my_chip = _my_core_and_chip()
        for d in range(N_DEV):
            o_ref[(d % 2) ^ my_c, (d // 2) ^ my_chip] = full_ref[:, pl.ds(d * n, n)]

    return pl.pallas_call(
        body, name="ssd_in_to_shards", grid=(K // tk,),
        in_specs=[_rows(tk, SSD_INNER), _rows(tk, SSD_CONV_DIM), _rows(tk, LANES), _ANY],
        out_specs=pl.BlockSpec((2, 4, None, tk, n), lambda t: (0, 0, j, t, 0)),
        out_shape=jax.ShapeDtypeStruct(buf.shape, f32),
        scratch_shapes=[pltpu.VMEM((tk, SSD_IN_PAD), f32)],
        input_output_aliases={3: 0},
        compiler_params=_params(("parallel",)),
    )(dwz, dwx, dwdt, buf)


def sum_over_devices(gathered):
    _, R, W = gathered.shape

    def body(g_ref, o_ref):
        acc = g_ref[0]
        for k in range(1, N_DEV):
            acc = acc + g_ref[k]
        o_ref[...] = acc

    return pl.pallas_call(
        body, name="sum_over_devices", grid=(1,),
        in_specs=[pl.BlockSpec((N_DEV, R, W), lambda i: (0, 0, 0))], out_specs=pl.BlockSpec((R, W), lambda i: (0, 0)),
        out_shape=jax.ShapeDtypeStruct((R, W), f32), compiler_params=_params(("arbitrary",)),
    )(gathered)


_ANY = pl.BlockSpec(memory_space=pl.ANY)


class _Exchange:
    def __init__(self, inputs, out_shapes, scratch, start, finish):
        self.inputs, self.out_shapes, self.scratch, self.start, self.finish = inputs, out_shapes, scratch, start, finish

    def run(self, name):
        ni, no = len(self.inputs), len(self.out_shapes)

        def body(*refs):
            parts = (refs[:ni], refs[ni:ni + no], refs[ni + no:])
            self.start(*parts)
            self.finish(*parts)

        return pl.pallas_call(body, name=name, in_specs=[_ANY] * ni, out_specs=[_ANY] * no, out_shape=self.out_shapes,
                              scratch_shapes=self.scratch)(*self.inputs)


def _carry(body, n_in, n_out, rider, first, last):
    if rider is None:
        return body
    ri, ro = len(rider.inputs), len(rider.out_shapes)

    def hosted(*refs):
        a, b, c = n_in + ri, n_in + ri + n_out, n_in + ri + n_out + ro
        rs = len(refs) - c - len(rider.scratch)
        parts = (refs[n_in:a], refs[b:c], refs[c + rs:])

        @pl.when(first())
        def _():
            rider.start(*parts)

        body(*refs[:n_in], *refs[a:b], *refs[c:c + rs])

        @pl.when(last())
        def _():
            rider.finish(*parts)

    return hosted


def _rider_specs(rider):
    if rider is None:
        return [], [], [], [], []
    return [_ANY] * len(rider.inputs), [_ANY] * len(rider.out_shapes), list(rider.out_shapes), list(rider.scratch), list(rider.inputs)


def all_gather(blocks):
    n = len(blocks)

    def plan(x_refs, out_refs, sems):
        send_sems, recv_sems, local_sems = sems
        x, y, c = lax.axis_index("x"), lax.axis_index("y"), lax.axis_index("c")
        me, sibling = (x, y, c), (x, y, 1 - c)
        chips = [(1 - x, y), (x, 1 - y), (1 - x, 1 - y)]

        def copy(a, k, blk, to, src=None):
            px, py, pc = blk
            slot = out_refs[a].at[4 * px + 2 * py + pc]
            return pltpu.make_async_remote_copy(
                src_ref=slot if src is None else src, dst_ref=slot,
                send_sem=send_sems.at[7 * a + k], recv_sem=recv_sems.at[7 * a + k], device_id=to, device_id_type=MESH)

        mine = [pltpu.make_async_copy(x_refs[a], out_refs[a].at[4 * x + 2 * y + c], local_sems.at[a]) for a in range(n)]
        first = []
        for a in range(n):
            first += [copy(a, 0, me, sibling, src=x_refs[a])] + [copy(a, 1 + j, me, (*chip, c), src=x_refs[a]) for j, chip in enumerate(chips)]
        return c, me, sibling, chips, copy, mine, first

    def start(x_refs, out_refs, sems):
        _, _, _, _, _, mine, first = plan(x_refs, out_refs, sems)
        for cp in mine + first:
            cp.start()

    def finish(x_refs, out_refs, sems):
        c, me, sibling, chips, copy, mine, first = plan(x_refs, out_refs, sems)
        passed = []
        for j, chip in enumerate(chips):
            for a in range(n):
                copy(a, 1 + j, (*chip, c), me).wait_recv()
                passed.append(copy(a, 4 + j, (*chip, c), sibling))
                passed[-1].start()
        for a in range(n):
            copy(a, 0, sibling, me).wait_recv()
            for j, chip in enumerate(chips):
                copy(a, 4 + j, (*chip, 1 - c), me).wait_recv()
        for cp in first + passed:
            cp.wait_send()
        for cp in mine:
            cp.wait()

    return _Exchange(list(blocks), [jax.ShapeDtypeStruct((N_DEV,) + b.shape, b.dtype) for b in blocks],
                     [pltpu.SemaphoreType.DMA((7 * n,)), pltpu.SemaphoreType.DMA((7 * n,)), pltpu.SemaphoreType.DMA((n,))], start, finish)


def exchange_with_sibling(gs):
    n = len(gs)

    def body(*refs):
        g_refs, recv_refs = refs[:n], refs[n:2 * n]
        send_sems, recv_sems = refs[2 * n:]
        x, y, c = lax.axis_index("x"), lax.axis_index("y"), lax.axis_index("c")
        cps = [pltpu.make_async_remote_copy(src_ref=g_refs[a].at[1], dst_ref=recv_refs[a], send_sem=send_sems.at[a],
                                            recv_sem=recv_sems.at[a], device_id=(x, y, 1 - c), device_id_type=MESH) for a in range(n)]
        for cp in cps:
            cp.start()
        for cp in cps:
            cp.wait()

    return pl.pallas_call(
        body, name="exchange_with_sibling", in_specs=[_ANY] * n, out_specs=[_ANY] * n,
        out_shape=[jax.ShapeDtypeStruct(g.shape[1:], g.dtype) for g in gs],
        scratch_shapes=[pltpu.SemaphoreType.DMA((n,)), pltpu.SemaphoreType.DMA((n,))],
    )(*gs)


def exchange_between_chips(parts):
    n = len(parts)

    def plan(p_refs, recv_refs, sems):
        send_sems, recv_sems = sems
        x, y, c = lax.axis_index("x"), lax.axis_index("y"), lax.axis_index("c")
        chips = [(2, (1 - x, y)), (1, (x, 1 - y)), (3, (1 - x, 1 - y))]
        return [pltpu.make_async_remote_copy(src_ref=p_refs[a].at[slot], dst_ref=recv_refs[a].at[k], send_sem=send_sems.at[3 * a + k],
                                             recv_sem=recv_sems.at[3 * a + k], device_id=(px, py, c), device_id_type=MESH)
                for a in range(n) for k, (slot, (px, py)) in enumerate(chips)]

    def start(*refs):
        for cp in plan(*refs):
            cp.start()

    def finish(*refs):
        for cp in plan(*refs):
            cp.wait()

    return _Exchange(list(parts), [jax.ShapeDtypeStruct((3,) + p.shape[1:], p.dtype) for p in parts],
                     [pltpu.SemaphoreType.DMA((3 * n,)), pltpu.SemaphoreType.DMA((3 * n,))], start, finish)


PARAMS = {
    "norm_w": ((DEPTH, 3, D_MODEL), 2),
    "ffn_w_gate": ((DEPTH, 2, D_MODEL, D_FF), 3),
    "ffn_w_up": ((DEPTH, 2, D_MODEL, D_FF), 3),
    "ffn_w_down": ((DEPTH, 2, D_FF, D_MODEL), 2),
    "ssd_w_in": ((2, D_MODEL, SSD_IN_DIM), 2),
    "ssd_conv_w": ((2, SSD_CONV_K, SSD_CONV_DIM), 2),
    "ssd_conv_b": ((2, SSD_CONV_DIM), None),
    "ssd_dt_bias": ((2, SSD_HEADS), None),
    "ssd_a_log": ((2, SSD_HEADS), None),
    "ssd_d": ((2, SSD_HEADS), None),
    "ssd_norm_w": ((2, SSD_INNER), None),
    "ssd_w_out": ((2, SSD_INNER, D_MODEL), 1),
    "sc_w_in": ((2, D_MODEL, 3 * D_MODEL), 2),
    "sc_conv_w": ((2, SC_CONV_K, D_MODEL), 2),
    "sc_w_out": ((2, D_MODEL, D_MODEL), 1),
    "final_norm_w": ((D_MODEL,), None),
}
NAMES = list(PARAMS)
BIG = ["ffn_w_gate", "ffn_w_up", "ffn_w_down", "ssd_w_in", "ssd_w_out", "sc_w_in", "sc_w_out"]
SMALL = [n for n in NAMES if n not in BIG]
SMALL_SHARDED = [n for n in SMALL if PARAMS[n][1] is not None]


def _round_up(n, m):
    return -(-n // m) * m


def _pack(flat_list, rows_multiple):
    flat = jnp.concatenate(flat_list)
    rows = _round_up(_round_up(flat.shape[0], PACK_W) // PACK_W, rows_multiple)
    return jnp.pad(flat, (0, rows * PACK_W - flat.shape[0])).reshape(rows, PACK_W)


def _unpack(packed, shapes, lead=()):
    flat = packed.reshape(lead + (-1,))
    out, off = [], 0
    for shp in shapes:
        n = 1
        for s in shp:
            n *= s
        out.append(flat[..., off:off + n].reshape(lead + tuple(shp)))
        off += n
    return out


def _local_shape(name):
    shp, ax = PARAMS[name]
    if ax is None:
        return shp
    return shp[:ax] + (shp[ax] // N_DEV,) + shp[ax + 1:]


def _full_from_gathered(g, name):
    shp, ax = PARAMS[name]
    return jnp.moveaxis(g, 0, ax).reshape(shp)


def _by_destination(full, name):
    shp, ax = PARAMS[name]
    loc = shp[ax] // N_DEV
    return jnp.moveaxis(full.reshape(shp[:ax] + (N_DEV, loc) + shp[ax + 1:]), ax, 0)


def _ssd_layer_fwd(xin, nw, p, rider=None):
    z, xbc, dt_raw = in_proj_fwd(xin, nw, [p["ssd_wz"], p["ssd_wx"], p["ssd_wdt"]], [bf16, bf16, f32])
    act, dt4 = ssd_conv_fwd(xbc, p["ssd_conv_w"], p["ssd_conv_b"], dt_raw, p["ssd_dt_bias"])
    y, states, *got = ssd_scan_fwd(act, dt4, p["ssd_alog4"], rider=rider)
    gn = ssd_gate_fwd(y, act, z, p["ssd_dx"], p["ssd_norm_w"])
    xout = out_proj_fwd(xin, gn, p["ssd_w_out"])
    return xout, (xin, z, xbc, dt_raw, act, dt4, y, states, gn), got


def _ssd_layer_bwd(dxo, nw, p, saved, gbuf, slab, rider=None):
    xin, z, xbc, dt_raw, act, dt4, y, states, gn = saved
    T = xin.shape[0]
    dgn, dyb = out_proj_bwd(dxo, p["ssd_w_out"])
    gbuf["ssd_w_out"] = tn_matmul_to_shards(gn, dyb, gbuf["ssd_w_out"], (slab,), 0)
    g = {}
    dy, dxs_skip, dz, dd_x, dgnw = ssd_gate_bwd(y, act, z, p["ssd_dx"], p["ssd_norm_w"], dgn)
    g["ssd_norm_w"] = dgnw[0]
    g["ssd_d"] = jnp.sum(dd_x.reshape(SSD_HEADS, SSD_HEAD_DIM), axis=1)
    dxs, db, dc, ddt4, dalog4, *got = ssd_scan_bwd(act, dt4, p["ssd_alog4"], states, dy, rider=rider)
    g["ssd_a_log"] = dalog4[:, 0, :8].reshape(SSD_HEADS)
    dxbc, ddt_raw, dcw, dcb, ddtb = ssd_conv_bwd(xbc, p["ssd_conv_w"], p["ssd_conv_b"], dt_raw, p["ssd_dt_bias"], dxs, dxs_skip, db, dc, ddt4)
    g["ssd_conv_w"] = dcw[:SSD_CONV_K]
    g["ssd_conv_b"] = dcb[0]
    g["ssd_dt_bias"] = ddtb[0, :SSD_HEADS]
    dx, h, dnw = in_proj_bwd(xin, nw, dxo, [dz, dxbc, ddt_raw], [p["ssd_wz"], p["ssd_wx"], p["ssd_wdt"]])
    gbuf["ssd_w_in"] = ssd_in_to_shards(tn_matmul(h, dz), tn_matmul(h, dxbc), tn_matmul(h, ddt_raw), gbuf["ssd_w_in"], slab)
    return dx, dnw, g, got


def _sc_layer_fwd(xin, nw, p):
    (bcu,) = in_proj_fwd(xin, nw, [p["sc_w_in"]], [bf16])
    q = sc_mid_fwd(bcu, p["sc_conv_w"])
    return out_proj_fwd(xin, q, p["sc_w_out"]), (xin, bcu, q)


def _sc_layer_bwd(dxo, nw, p, saved, gbuf, slab):
    xin, bcu, q = saved
    dq, dyb = out_proj_bwd(dxo, p["sc_w_out"])
    gbuf["sc_w_out"] = tn_matmul_to_shards(q, dyb, gbuf["sc_w_out"], (slab,), 0)
    dbcu, dcw = sc_mid_bwd(bcu, p["sc_conv_w"], dq)
    g = {"sc_conv_w": dcw[:SC_CONV_K]}
    dx, h, dnw = in_proj_bwd(xin, nw, dxo, [dbcu], [p["sc_w_in"]])
    gbuf["sc_w_in"] = tn_matmul_to_shards(h, dbcu, gbuf["sc_w_in"], (slab,), 1)
    return dx, dnw, g


def kernel(x, norm_w, ffn_w_gate, ffn_w_up, ffn_w_down, ssd_w_in, ssd_conv_w, ssd_conv_b, ssd_dt_bias, ssd_a_log, ssd_d, ssd_norm_w, ssd_w_out, sc_w_in, sc_conv_w, sc_w_out, final_norm_w, loss_target, m_norm_w, m_ffn_w_gate, m_ffn_w_up, m_ffn_w_down, m_ssd_w_in, m_ssd_conv_w, m_ssd_conv_b, m_ssd_dt_bias, m_ssd_a_log, m_ssd_d, m_ssd_norm_w, m_ssd_w_out, m_sc_w_in, m_sc_conv_w, m_sc_w_out, m_final_norm_w, v_norm_w, v_ffn_w_gate, v_ffn_w_up, v_ffn_w_down, v_ssd_w_in, v_ssd_conv_w, v_ssd_conv_b, v_ssd_dt_bias, v_ssd_a_log, v_ssd_d, v_ssd_norm_w, v_ssd_w_out, v_sc_w_in, v_sc_conv_w, v_sc_w_out, v_final_norm_w):
    w_loc = dict(zip(NAMES, (norm_w, ffn_w_gate, ffn_w_up, ffn_w_down, ssd_w_in, ssd_conv_w, ssd_conv_b, ssd_dt_bias, ssd_a_log, ssd_d, ssd_norm_w, ssd_w_out, sc_w_in, sc_conv_w, sc_w_out, final_norm_w)))
    m_loc = dict(zip(NAMES, (m_norm_w, m_ffn_w_gate, m_ffn_w_up, m_ffn_w_down, m_ssd_w_in, m_ssd_conv_w, m_ssd_conv_b, m_ssd_dt_bias, m_ssd_a_log, m_ssd_d, m_ssd_norm_w, m_ssd_w_out, m_sc_w_in, m_sc_conv_w, m_sc_w_out, m_final_norm_w)))
    v_loc = dict(zip(NAMES, (v_norm_w, v_ffn_w_gate, v_ffn_w_up, v_ffn_w_down, v_ssd_w_in, v_ssd_conv_w, v_ssd_conv_b, v_ssd_dt_bias, v_ssd_a_log, v_ssd_d, v_ssd_norm_w, v_ssd_w_out, v_sc_w_in, v_sc_conv_w, v_sc_w_out, v_final_norm_w)))
    ax, ay, ac = lax.axis_index("x"), lax.axis_index("y"), lax.axis_index("c")
    my_chip = 2 * ax + ay
    my_dev = 4 * ax + 2 * ay + ac
    T = x.shape[1]

    def as3d(a):
        return a.reshape((-1,) + a.shape[-2:])

    wb = {n: as3d(w_loc[n]).astype(bf16) for n in BIG}

    FFN = ["ffn_w_gate", "ffn_w_up", "ffn_w_down"]

    def mixer_names(i):
        return ["ssd_w_in", "ssd_w_out"] if i % 2 == 0 else ["sc_w_in", "sc_w_out"]

    ag_sets = [[(n, 0, 1) for n in FFN], [(n, 1, 1) for n in FFN] + [(n, 0, 1) for n in mixer_names(0)]]
    ag_sets += [[(n, 2 * r, 2) for n in FFN] + [(n, r // 2, 1) for n in mixer_names(r)] for r in (1, 2, 3)]

    def set_blocks(spec):
        return [wb[n][a0:a0 + na] for n, a0, na in spec]

    def set_weights(spec, gathered):
        q = {}
        for (n, _, _), g in zip(spec, gathered):
            if n == "ssd_w_in":
                q["ssd_wz"], q["ssd_wx"], q["ssd_wdt"] = assemble_ssd_in(g)
            else:
                q[n] = assemble(g, 1 if PARAMS[n][1] == len(PARAMS[n][0]) - 1 else 0)
        return q

    ss_shapes = [_local_shape(n) for n in SMALL_SHARDED]
    gathered0 = all_gather(set_blocks(ag_sets[0]) + [_pack([w_loc[n].reshape(-1) for n in SMALL_SHARDED], 8)]).run("all_gather_first")
    full = {}
    for n, part in zip(SMALL_SHARDED, _unpack(gathered0[-1], ss_shapes, lead=(N_DEV,))):
        full[n] = _full_from_gathered(part, n)
    for n in SMALL:
        if PARAMS[n][1] is None:
            full[n] = w_loc[n]
    small = {
        "ssd_conv_w": full["ssd_conv_w"],
        "ssd_conv_b": full["ssd_conv_b"].reshape(2, 1, SSD_CONV_DIM),
        "ssd_dt_bias": jnp.pad(full["ssd_dt_bias"], ((0, 0), (0, LANES - SSD_HEADS))).reshape(2, 1, LANES),
        "ssd_alog4": jnp.pad(full["ssd_a_log"].reshape(2, SSD_GROUPS, 1, 8), ((0, 0), (0, 0), (0, 0), (0, LANES - 8))),
        "ssd_dx": jnp.repeat(full["ssd_d"], SSD_HEAD_DIM, axis=1).reshape(2, 1, SSD_INNER),
        "ssd_norm_w": full["ssd_norm_w"].reshape(2, 1, SSD_INNER),
        "sc_conv_w": full["sc_conv_w"],
    }
    nw_all = full["norm_w"].reshape(DEPTH, 3, 1, D_MODEL)

    ffn_w = [[None, None] for _ in range(DEPTH)]
    mix_w = [None] * DEPTH

    def arrived(s, gathered):
        q = set_weights(ag_sets[s], gathered)
        ffn = tuple(q[n] for n in FFN)
        if s == 0:
            ffn_w[0][0] = ffn + ((0,),)
            return
        i = 0 if s == 1 else s - 1
        if s == 1:
            ffn_w[0][1] = ffn + ((0,),)
        else:
            ffn_w[i] = [ffn + ((0,),), ffn + ((1,),)]
        m = {n: v[0] for n, v in q.items() if n not in FFN}
        m.update({n: v[i // 2] for n, v in small.items() if n.startswith("ssd" if i % 2 == 0 else "sc")})
        mix_w[i] = m

    def rider_for(s):
        return all_gather(set_blocks(ag_sets[s]))

    xc = x[0]
    saved = []
    arrived(0, gathered0[:-1])
    for i in range(DEPTH):
        carried = {0: (1, 2, 3), 1: (4, None, None)}.get(i, (None, None, None))
        wg, wu, wd, idx = ffn_w[i][0]
        x1, g1, u1, a1, *got = ffn_fwd(xc, nw_all[i, 0], wg, wu, wd, idx, rider=rider_for(carried[0]) if carried[0] else None)
        if carried[0]:
            arrived(carried[0], got)
        if i % 2 == 0:
            x2, mix_saved, got = _ssd_layer_fwd(x1, nw_all[i, 1], mix_w[i], rider=rider_for(carried[1]) if carried[1] else None)
            if carried[1]:
                arrived(carried[1], got)
        else:
            x2, mix_saved = _sc_layer_fwd(x1, nw_all[i, 1], mix_w[i])
        wg, wu, wd, idx = ffn_w[i][1]
        x3, g3, u3, a3, *got = ffn_fwd(x2, nw_all[i, 2], wg, wu, wd, idx, rider=rider_for(carried[2]) if carried[2] else None)
        if carried[2]:
            arrived(carried[2], got)
        saved.append(((xc, g1, u1, a1), mix_saved, (x2, g3, u3, a3)))
        xc = x3

    loss_row, dx, dfw = loss_head(xc, full["final_norm_w"].reshape(1, D_MODEL), loss_target[0])
    loss = lax.psum(loss_row[0, 0], ("x", "y", "c"))

    grads = {n: [None] * PARAMS[n][0][0] for n in SMALL if n != "final_norm_w"}
    grads["final_norm_w"] = dfw[0]
    dnorm = [[None] * 3 for _ in range(DEPTH)]
    def slabs(n, which):
        if n.startswith("ffn"):
            return {"early": (2, 6), "mid": (1, 1), "last": (0, 1)}[which]
        if n.startswith("ssd"):
            return {"early": (1, 1), "mid": (0, 1), "last": (0, 0)}[which]
        return {"early": (0, 2), "mid": (0, 0), "last": (0, 0)}[which]

    gb = {which: {n: jnp.zeros((2, 4, slabs(n, which)[1]) + wb[n].shape[1:], f32) for n in BIG if slabs(n, which)[1]}
          for which in ("early", "mid", "last")}

    def ffn_back(i, k, dxo, sv, rider=None):
        xin, g_, u_, a_ = sv
        which = "early" if i > 0 else ("mid" if k == 1 else "last")
        gbuf = gb[which]
        slab = 2 * i + k - slabs("ffn_w_gate", which)[0]
        wg, wu, wd, idx = ffn_w[i][k]
        dxi, h, dyb, dg, du, dnw, *got = ffn_bwd_dx(xin, dxo, g_, u_, nw_all[i, 2 * k], wg, wu, wd, idx, rider=rider)
        dnorm[i][2 * k] = dnw[0]
        gbuf["ffn_w_gate"] = tn_matmul_to_shards(h, dg, gbuf["ffn_w_gate"], (slab,), 1)
        gbuf["ffn_w_up"] = tn_matmul_to_shards(h, du, gbuf["ffn_w_up"], (slab,), 1)
        gbuf["ffn_w_down"] = tn_matmul_to_shards(a_, dyb, gbuf["ffn_w_down"], (slab,), 0)
        return dxi, got

    def reduce_in_chip(gbuf):
        names = list(gbuf)
        bufs = [gbuf[n] for n in names]
        from_sibling = exchange_with_sibling(bufs)
        return names, bufs, from_sibling, [pair_sum_bf16(g, fs, "pair_sum_" + n) for n, g, fs in zip(names, bufs, from_sibling)]

    reduced, from_chips = {}, {}
    for i in reversed(range(DEPTH)):
        j = i // 2
        sv_a, sv_mix, sv_b = saved[i]
        dx, _ = ffn_back(i, 1, dx, sv_b)
        if i % 2 == 0:
            rider = exchange_between_chips(reduced["early"][3]) if i == 0 else None
            dx, dnw, gm, got = _ssd_layer_bwd(dx, nw_all[i, 1], mix_w[i], sv_mix, gb["mid" if i == 0 else "early"], 0, rider=rider)
            if i == 0:
                from_chips["early"] = got
                reduced["mid"] = reduce_in_chip(gb["mid"])
        else:
            dx, dnw, gm = _sc_layer_bwd(dx, nw_all[i, 1], mix_w[i], sv_mix, gb["early"], j)
        dnorm[i][1] = dnw[0]
        for n, val in gm.items():
            grads[n][j] = val
        dx, got = ffn_back(i, 0, dx, sv_a, rider=exchange_between_chips(reduced["mid"][3]) if i == 0 else None)
        if i == 0:
            from_chips["mid"] = got
        if i == 1:
            reduced["early"] = reduce_in_chip(gb["early"])

    grads["norm_w"] = jnp.stack([jnp.stack(r) for r in dnorm])
    for n in SMALL:
        if isinstance(grads[n], list):
            grads[n] = jnp.stack(grads[n])

    reduced["last"] = reduce_in_chip(gb["last"])
    from_chips["last"] = exchange_between_chips(reduced["last"][3]).run("exchange_between_chips")
    results = [{}, {}, {}, {}]
    outs = {}
    for which in ("last", "mid", "early"):
        names, bufs, from_sibling, _ = reduced[which]
        for n, g, fs, fc in zip(names, bufs, from_sibling, from_chips[which]):
            parts = [((0, 0), g), ((0,), fs), ((0,), fc), ((1,), fc), ((2,), fc)]
            outs[n] = adamw(parts, as3d(w_loc[n]), as3d(m_loc[n]), as3d(v_loc[n]), name="adamw_" + n + "_" + which,
                            a0=slabs(n, which)[0], prev=outs.get(n))
    for n in BIG:
        for k in range(4):
            results[k][n] = outs[n][k].reshape(_local_shape(n))

    g_small = _pack([grads[n].reshape(-1) for n in SMALL], 8)
    g_small = sum_over_devices(all_gather([g_small]).run("all_gather_small_grads")[0])
    g_small_full = dict(zip(SMALL, _unpack(g_small, [PARAMS[n][0] for n in SMALL])))
    g_small_loc = []
    for n in SMALL:
        if PARAMS[n][1] is None:
            g_small_loc.append(g_small_full[n])
        else:
            g_small_loc.append(lax.dynamic_index_in_dim(_by_destination(g_small_full[n], n), my_dev, axis=0, keepdims=False))
    small_shapes = [_local_shape(n) for n in SMALL]
    pack_small = lambda d: _pack([d[n].reshape(-1) for n in SMALL], 8)[None]
    small_out = adamw([_pack([gl.reshape(-1) for gl in g_small_loc], 8)[None]], pack_small(w_loc), pack_small(m_loc), pack_small(v_loc), name="adamw_small")
    for k in range(4):
        results[k].update(zip(SMALL, _unpack(small_out[k], small_shapes)))
    return (loss, dx[None], *[results[0][n] for n in NAMES], *[results[1][n] for n in NAMES],
            *[results[2][n] for n in NAMES], *[results[3][n] for n in NAMES])
```

```python
import functools

import jax
import jax.numpy as jnp
from jax import lax
from jax.experimental import pallas as pl
from jax.experimental.pallas import tpu as pltpu

f32 = jnp.float32
bf16 = jnp.bfloat16

D_MODEL = 1024
D_FF = 2816
DEPTH = 4
SSD_INNER = 2048
SSD_HEADS = 32
SSD_HEAD_DIM = 64
SSD_GROUPS = 4
SSD_STATE = 128
SSD_CONV_K = 4
SSD_CONV_DIM = 3072
SSD_IN_DIM = 5152
SSD_CHUNK = 128
SC_CONV_K = 3
RMS_EPS = 1e-5
N_DEV = 8
LANES = 128
HALO = 16
PACK_W = 1024
PACK_TILE = 256
VMEM_LIMIT = 56 * 1024 * 1024
NEG_BIG = -1e30

ADAM_LR = 0.001
ADAM_B1 = 0.9
ADAM_B2 = 0.999
ADAM_EPS = 1e-08
ADAM_WD = 0.01
ADAM_STEP = 10

NT_DIMS = (((1,), (1,)), ((), ()))
TN_DIMS = (((0,), (0,)), ((), ()))
MESH = pl.DeviceIdType.MESH


def _params(sem=None):
    return pltpu.CompilerParams(dimension_semantics=sem, vmem_limit_bytes=VMEM_LIMIT)


def _resident(shape):
    nd = len(shape)
    return pl.BlockSpec(tuple(shape), lambda *_: (0,) * nd, pipeline_mode=pl.Buffered(1))


def _rows(tm, width):
    return pl.BlockSpec((tm, width), lambda i: (i, 0))


def _my_core_and_chip():
    return lax.axis_index("c"), 2 * lax.axis_index("x") + lax.axis_index("y")


def _sigmoid(v):
    return 0.5 * jnp.tanh(0.5 * v) + 0.5


def _softplus(v):
    return jnp.maximum(v, 0.0) + jnp.log(1.0 + jnp.exp(-jnp.abs(v)))


def _rms_fwd(xv, w):
    inv = lax.rsqrt(jnp.mean(xv * xv, axis=-1, keepdims=True) + RMS_EPS)
    xh = xv * inv
    return xh * w, xh, inv


def _rms_bwd(dh, xh, inv, w):
    dxh = dh * w
    dx = inv * (dxh - xh * jnp.mean(dxh * xh, axis=-1, keepdims=True))
    return dx, jnp.sum(dh * xh, axis=0, keepdims=True)


def _mm(a, b):
    return jnp.dot(a, b, preferred_element_type=f32)


def _mm_nt(a, b):
    return lax.dot_general(a, b, NT_DIMS, preferred_element_type=f32)


def _mm_tn(a, b):
    return lax.dot_general(a, b, TN_DIMS, preferred_element_type=f32)


FFN_CHUNK = D_FF


def _ffn_chunks():
    return [(c0, min(FFN_CHUNK, D_FF - c0)) for c0 in range(0, D_FF, FFN_CHUNK)]


def _layer_slab(w, idx):
    tail = w.shape[len(idx):]
    return pl.BlockSpec((None,) * len(idx) + tuple(tail), lambda *_: tuple(idx) + (0,) * len(tail), pipeline_mode=pl.Buffered(1))


def ffn_fwd(x, nw, wg, wu, wd, idx, tm=512, rider=None):
    T = x.shape[0]
    nt = T // tm
    r_in, r_out, r_shapes, r_scratch, r_args = _rider_specs(rider)

    def body(x_ref, nw_ref, wg_ref, wu_ref, wd_ref, xo_ref, g_ref, u_ref, a_ref):
        xv = x_ref[...]
        h, _, _ = _rms_fwd(xv, nw_ref[...])
        hb = h.astype(bf16)
        y = None
        for c0, fc in _ffn_chunks():
            cols = pl.ds(c0, fc)
            g = _mm(hb, wg_ref[:, cols])
            u = _mm(hb, wu_ref[:, cols])
            ab = (g * _sigmoid(g) * u).astype(bf16)
            g_ref[:, cols] = g.astype(bf16)
            u_ref[:, cols] = u.astype(bf16)
            a_ref[:, cols] = ab
            part = _mm(ab, wd_ref[cols, :])
            y = part if y is None else y + part
        xo_ref[...] = xv + 0.5 * y

    hosted = _carry(body, 5, 4, rider, lambda: pl.program_id(0) == 0, lambda: pl.program_id(0) == nt - 1)
    return pl.pallas_call(
        hosted, name="ffn_fwd" if rider is None else "ffn_fwd_carrying", grid=(nt,),
        in_specs=[_rows(tm, D_MODEL), _resident((1, D_MODEL)), _layer_slab(wg, idx), _layer_slab(wu, idx), _layer_slab(wd, idx)] + r_in,
        out_specs=[_rows(tm, D_MODEL), _rows(tm, D_FF), _rows(tm, D_FF), _rows(tm, D_FF)] + r_out,
        out_shape=[jax.ShapeDtypeStruct((T, D_MODEL), f32)] + [jax.ShapeDtypeStruct((T, D_FF), bf16)] * 3 + r_shapes,
        scratch_shapes=r_scratch,
        compiler_params=_params(("parallel",) if rider is None else ("arbitrary",)),
    )(x, nw, wg, wu, wd, *r_args)


def ffn_bwd_dx(x, dxo, g, u, nw, wg, wu, wd, idx, tm=256, rider=None):
    T = x.shape[0]
    nt = T // tm
    r_in, r_out, r_shapes, r_scratch, r_args = _rider_specs(rider)

    def body(x_ref, dxo_ref, g_ref, u_ref, nw_ref, wg_ref, wu_ref, wd_ref, dx_ref, h_ref, dy_ref, dg_ref, du_ref, dnw_ref):
        w = nw_ref[...]
        h, xh, inv = _rms_fwd(x_ref[...], w)
        dxo_v = dxo_ref[...]
        dyb = (0.5 * dxo_v).astype(bf16)
        dh = None
        for c0, fc in _ffn_chunks():
            cols = pl.ds(c0, fc)
            da = _mm_nt(dyb, wd_ref[cols, :])
            gv = g_ref[:, cols].astype(f32)
            uv = u_ref[:, cols].astype(f32)
            s = _sigmoid(gv)
            dgb = (da * uv * (s * (1.0 + gv * (1.0 - s)))).astype(bf16)
            dub = (da * (gv * s)).astype(bf16)
            dg_ref[:, cols] = dgb
            du_ref[:, cols] = dub
            part = _mm_nt(dgb, wg_ref[:, cols]) + _mm_nt(dub, wu_ref[:, cols])
            dh = part if dh is None else dh + part
        dxn, dw = _rms_bwd(dh, xh, inv, w)
        dx_ref[...] = dxo_v + dxn
        h_ref[...] = h.astype(bf16)
        dy_ref[...] = dyb

        @pl.when(pl.program_id(0) == 0)
        def _():
            dnw_ref[...] = jnp.zeros_like(dnw_ref)

        dnw_ref[...] += dw

    hosted = _carry(body, 8, 6, rider, lambda: pl.program_id(0) == 0, lambda: pl.program_id(0) == nt - 1)
    return pl.pallas_call(
        hosted, name="ffn_bwd_dx" if rider is None else "ffn_bwd_dx_carrying", grid=(nt,),
        in_specs=[_rows(tm, D_MODEL), _rows(tm, D_MODEL), _rows(tm, D_FF), _rows(tm, D_FF), _resident((1, D_MODEL)),
                  _layer_slab(wg, idx), _layer_slab(wu, idx), _layer_slab(wd, idx)] + r_in,
        out_specs=[_rows(tm, D_MODEL), _rows(tm, D_MODEL), _rows(tm, D_MODEL), _rows(tm, D_FF), _rows(tm, D_FF),
                   pl.BlockSpec((1, D_MODEL), lambda i: (0, 0))] + r_out,
        out_shape=[jax.ShapeDtypeStruct((T, D_MODEL), f32), jax.ShapeDtypeStruct((T, D_MODEL), bf16), jax.ShapeDtypeStruct((T, D_MODEL), bf16),
                   jax.ShapeDtypeStruct((T, D_FF), bf16), jax.ShapeDtypeStruct((T, D_FF), bf16), jax.ShapeDtypeStruct((1, D_MODEL), f32)] + r_shapes,
        scratch_shapes=r_scratch,
        compiler_params=_params(("arbitrary",)),
    )(x, dxo, g, u, nw, wg, wu, wd, *r_args)


def tn_matmul(a, b, tk=512):
    T, M = a.shape
    N = b.shape[1]
    bn = N if M * N <= 3_200_000 else N // 2
    nk = T // tk

    def body(a_ref, b_ref, o_ref):
        @pl.when(pl.program_id(1) == 0)
        def _():
            o_ref[...] = jnp.zeros_like(o_ref)

        o_ref[...] += _mm_tn(a_ref[...], b_ref[...])

    return pl.pallas_call(
        body, name=f"tn_matmul_{M}x{N}", grid=(N // bn, nk),
        in_specs=[pl.BlockSpec((tk, M), lambda j, k: (k, 0)), pl.BlockSpec((tk, bn), lambda j, k: (k, j))],
        out_specs=pl.BlockSpec((M, bn), lambda j, k: (0, j)),
        out_shape=jax.ShapeDtypeStruct((M, N), f32),
        compiler_params=_params(("parallel", "arbitrary")),
    )(a, b)


def tn_matmul_to_shards(a, b, buf, idx, axis, tk=512):
    T, M = a.shape
    N = b.shape[1]
    m, n = buf.shape[-2:]
    nk = T // tk
    fresh = isinstance(buf, jax.ShapeDtypeStruct)

    def body(a_ref, b_ref, *rest):
        o_ref, acc_ref = rest[-2:]
        k = pl.program_id(0)

        @pl.when(k == 0)
        def _():
            acc_ref[...] = jnp.zeros_like(acc_ref)

        acc_ref[...] += _mm_tn(a_ref[...], b_ref[...])

        @pl.when(k == nk - 1)
        def _():
            my_c, my_chip = _my_core_and_chip()
            for d in range(N_DEV):
                piece = acc_ref[:, pl.ds(d * n, n)] if axis == 1 else acc_ref[pl.ds(d * m, m), :]
                o_ref[(d % 2) ^ my_c, (d // 2) ^ my_chip] = piece

    none = (None,) * len(idx)
    return pl.pallas_call(
        body, name=f"tn_matmul_to_shards_{M}x{N}_{axis}", grid=(nk,),
        in_specs=[pl.BlockSpec((tk, M), lambda k: (k, 0)), pl.BlockSpec((tk, N), lambda k: (k, 0))] + ([] if fresh else [_ANY]),
        out_specs=pl.BlockSpec((2, 4) + none + (m, n), lambda k: (0, 0) + tuple(idx) + (0, 0)),
        out_shape=jax.ShapeDtypeStruct(buf.shape, f32),
        scratch_shapes=[pltpu.VMEM((M, N), f32)],
        input_output_aliases={} if fresh else {2: 0},
        compiler_params=_params(("arbitrary",)),
    )(a, b, *([] if fresh else [buf]))


def in_proj_fwd(x, nw, ws, out_dtypes, tm=256):
    T = x.shape[0]
    n = len(ws)

    def body(*refs):
        x_ref, nw_ref = refs[:2]
        w_refs = refs[2:2 + n]
        o_refs = refs[2 + n:]
        h, _, _ = _rms_fwd(x_ref[...], nw_ref[...])
        hb = h.astype(bf16)
        for w_ref, o_ref in zip(w_refs, o_refs):
            o_ref[...] = _mm(hb, w_ref[...]).astype(o_ref.dtype)

    return pl.pallas_call(
        body, name="in_proj_fwd_" + "_".join(str(w.shape[1]) for w in ws), grid=(T // tm,),
        in_specs=[_rows(tm, D_MODEL), _resident((1, D_MODEL))] + [_resident(w.shape) for w in ws],
        out_specs=[_rows(tm, w.shape[1]) for w in ws],
        out_shape=[jax.ShapeDtypeStruct((T, w.shape[1]), dt) for w, dt in zip(ws, out_dtypes)],
        compiler_params=_params(("parallel",)),
    )(x, nw, *ws)


def in_proj_bwd(x, nw, dxo, dys, ws, tm=256):
    T = x.shape[0]
    n = len(ws)

    def body(*refs):
        x_ref, nw_ref, dxo_ref = refs[:3]
        dy_refs = refs[3:3 + n]
        w_refs = refs[3 + n:3 + 2 * n]
        dx_ref, h_ref, dnw_ref = refs[3 + 2 * n:]
        w = nw_ref[...]
        h, xh, inv = _rms_fwd(x_ref[...], w)
        dh = _mm_nt(dy_refs[0][...], w_refs[0][...])
        for dy_ref, w_ref in zip(dy_refs[1:], w_refs[1:]):
            dh = dh + _mm_nt(dy_ref[...], w_ref[...])
        dxn, dw = _rms_bwd(dh, xh, inv, w)
        dx_ref[...] = dxo_ref[...] + dxn
        h_ref[...] = h.astype(bf16)

        @pl.when(pl.program_id(0) == 0)
        def _():
            dnw_ref[...] = jnp.zeros_like(dnw_ref)

        dnw_ref[...] += dw

    return pl.pallas_call(
        body, name="in_proj_bwd_" + "_".join(str(w.shape[1]) for w in ws), grid=(T // tm,),
        in_specs=[_rows(tm, D_MODEL), _resident((1, D_MODEL)), _rows(tm, D_MODEL)] + [_rows(tm, w.shape[1]) for w in ws]
        + [_resident(w.shape) for w in ws],
        out_specs=[_rows(tm, D_MODEL), _rows(tm, D_MODEL), pl.BlockSpec((1, D_MODEL), lambda i: (0, 0))],
        out_shape=[jax.ShapeDtypeStruct((T, D_MODEL), f32), jax.ShapeDtypeStruct((T, D_MODEL), bf16), jax.ShapeDtypeStruct((1, D_MODEL), f32)],
        compiler_params=_params(("arbitrary",)),
    )(x, nw, dxo, *dys, *ws)


def out_proj_fwd(x, a, w, tm=512):
    T = x.shape[0]
    K = a.shape[1]

    def body(x_ref, a_ref, w_ref, o_ref):
        o_ref[...] = x_ref[...] + _mm(a_ref[...], w_ref[...])

    return pl.pallas_call(
        body, name=f"out_proj_fwd_{K}", grid=(T // tm,),
        in_specs=[_rows(tm, D_MODEL), _rows(tm, K), _resident(w.shape)],
        out_specs=_rows(tm, D_MODEL), out_shape=jax.ShapeDtypeStruct((T, D_MODEL), f32),
        compiler_params=_params(("parallel",)),
    )(x, a, w)


def out_proj_bwd(dxo, w, tm=512):
    T = dxo.shape[0]
    K = w.shape[0]

    def body(dxo_ref, w_ref, da_ref, dy_ref):
        dyb = dxo_ref[...].astype(bf16)
        dy_ref[...] = dyb
        da_ref[...] = _mm_nt(dyb, w_ref[...]).astype(bf16)

    return pl.pallas_call(
        body, name=f"out_proj_bwd_{K}", grid=(T // tm,),
        in_specs=[_rows(tm, D_MODEL), _resident(w.shape)],
        out_specs=[_rows(tm, K), _rows(tm, D_MODEL)],
        out_shape=[jax.ShapeDtypeStruct((T, K), bf16), jax.ShapeDtypeStruct((T, D_MODEL), bf16)],
        compiler_params=_params(("parallel",)),
    )(dxo, w)


def _halo_spec(tm, width, n_tiles, reverse):
    per = tm // HALO

    def idx(i):
        t = (n_tiles - 1 - i) if reverse else i
        return (jnp.maximum(t * per - 1, 0), 0)

    return pl.BlockSpec((HALO, width), idx)


def _tile_spec(tm, width, n_tiles, reverse):
    if reverse:
        return pl.BlockSpec((tm, width), lambda i: (n_tiles - 1 - i, 0))
    return _rows(tm, width)


ROW_BLOCK = 64


def _strip(s):
    return pl.ds(pl.multiple_of(s * LANES, LANES), LANES)


def _conv_rows(ext_ref, w_ref, cols, k_w, r0):
    base = HALO - (k_w - 1) + r0
    wins = [ext_ref[pl.ds(base + k, ROW_BLOCK), :] for k in range(k_w)]
    out = w_ref[pl.ds(0, 1), cols] * wins[0]
    for k in range(1, k_w):
        out = out + w_ref[pl.ds(k, 1), cols] * wins[k]
    return out, wins


def _shifted_back(d_ref, w_ref, cols, k_w, r0):
    out = w_ref[pl.ds(0, 1), cols] * d_ref[pl.ds(r0 + k_w - 1, ROW_BLOCK), :]
    for k in range(1, k_w):
        out = out + w_ref[pl.ds(k, 1), cols] * d_ref[pl.ds(r0 + k_w - 1 - k, ROW_BLOCK), :]
    return out


def ssd_conv_fwd(xbc, conv_w, conv_b, dt_raw, dt_bias, tm=256):
    T = xbc.shape[0]
    nt = T // tm
    K = SSD_CONV_K

    def body(x_ref, halo_ref, w_ref, b_ref, dtr_ref, dtb_ref, act_ref, dt_ref, ext_ref):
        first = pl.program_id(0) == 0

        def strip(s, carry):
            cols = _strip(s)
            ext_ref[pl.ds(0, HALO), :] = jnp.where(first, 0.0, halo_ref[:, cols].astype(f32))
            ext_ref[pl.ds(HALO, tm), :] = x_ref[:, cols].astype(f32)
            for r0 in range(0, tm, ROW_BLOCK):
                pre, _ = _conv_rows(ext_ref, w_ref, cols, K, r0)
                pre = pre + b_ref[:, cols]
                act_ref[pl.ds(r0, ROW_BLOCK), cols] = (pre * _sigmoid(pre)).astype(bf16)
            return carry

        lax.fori_loop(0, SSD_CONV_DIM // LANES, strip, 0)
        dt = _softplus(dtr_ref[...] + dtb_ref[...])
        lane = lax.broadcasted_iota(jnp.int32, (1, LANES), 1)
        for g in range(SSD_GROUPS):
            dt_ref[g] = jnp.where(lane < 8, dt if g == 0 else pltpu.roll(dt, LANES - 8 * g, axis=1), 0.0)

    return pl.pallas_call(
        body, name="ssd_conv_fwd", grid=(nt,),
        in_specs=[_rows(tm, SSD_CONV_DIM), _halo_spec(tm, SSD_CONV_DIM, nt, False), _resident(conv_w.shape), _resident(conv_b.shape),
                  _rows(tm, LANES), _resident(dt_bias.shape)],
        out_specs=[_rows(tm, SSD_CONV_DIM), pl.BlockSpec((SSD_GROUPS, tm, LANES), lambda i: (0, i, 0))],
        out_shape=[jax.ShapeDtypeStruct((T, SSD_CONV_DIM), bf16), jax.ShapeDtypeStruct((SSD_GROUPS, T, LANES), f32)],
        scratch_shapes=[pltpu.VMEM((tm + HALO, LANES), f32)],
        compiler_params=_params(("parallel",)),
    )(xbc, xbc, conv_w, conv_b, dt_raw, dt_bias)


def ssd_conv_bwd(xbc, conv_w, conv_b, dt_raw, dt_bias, dxs_a, dxs_b, db, dc, ddt, tm=256):
    T = xbc.shape[0]
    nt = T // tm
    K = SSD_CONV_K

    def body(x_ref, halo_ref, w_ref, b_ref, dtr_ref, dtb_ref, da_ref, dbb_ref, db_ref, dc_ref, ddt_ref,
             dx_ref, ddtr_ref, dw_ref, dbias_ref, ddtb_ref, ext_ref, dpre_ref, carry_ref):
        i = pl.program_id(0)

        @pl.when(i == 0)
        def _():
            carry_ref[...] = jnp.zeros_like(carry_ref)
            dw_ref[...] = jnp.zeros_like(dw_ref)
            dbias_ref[...] = jnp.zeros_like(dbias_ref)
            ddtb_ref[...] = jnp.zeros_like(ddtb_ref)

        first_tile = i == nt - 1

        def run_strips(lo, hi, load_dact):
            def strip(s, carry):
                cols = _strip(s)
                ext_ref[pl.ds(0, HALO), :] = jnp.where(first_tile, 0.0, halo_ref[:, cols].astype(f32))
                ext_ref[pl.ds(HALO, tm), :] = x_ref[:, cols].astype(f32)
                dpre_ref[pl.ds(tm, 8), :] = carry_ref[:, cols]
                bias = b_ref[:, cols]
                dws = [jnp.zeros((1, LANES), f32) for _ in range(K)]
                dbs = jnp.zeros((1, LANES), f32)
                for r0 in range(0, tm, ROW_BLOCK):
                    pre, wins = _conv_rows(ext_ref, w_ref, cols, K, r0)
                    pre = pre + bias
                    sg = _sigmoid(pre)
                    dpre = load_dact(s, r0) * (sg * (1.0 + pre * (1.0 - sg)))
                    dpre_ref[pl.ds(r0, ROW_BLOCK), :] = dpre
                    dbs = dbs + jnp.sum(dpre, axis=0, keepdims=True)
                    for k in range(K):
                        dws[k] = dws[k] + jnp.sum(dpre * wins[k], axis=0, keepdims=True)
                carry_ref[:, cols] = dpre_ref[pl.ds(0, 8), :]
                for r0 in range(0, tm, ROW_BLOCK):
                    dx_ref[pl.ds(r0, ROW_BLOCK), cols] = _shifted_back(dpre_ref, w_ref, cols, K, r0).astype(bf16)
                for k in range(K):
                    dw_ref[pl.ds(k, 1), cols] += dws[k]
                dbias_ref[:, cols] += dbs
                return carry

            lax.fori_loop(lo, hi, strip, 0)

        rows = lambda r0: pl.ds(r0, ROW_BLOCK)
        n_x = SSD_INNER // LANES
        n_g = SSD_GROUPS * SSD_STATE // LANES
        run_strips(0, n_x, lambda s, r0: da_ref[rows(r0), _strip(s)].astype(f32) + dbb_ref[rows(r0), _strip(s)].astype(f32))
        run_strips(n_x, n_x + n_g, lambda s, r0: db_ref[rows(r0), _strip(s - n_x)].astype(f32))
        run_strips(n_x + n_g, n_x + 2 * n_g, lambda s, r0: dc_ref[rows(r0), _strip(s - n_x - n_g)].astype(f32))
        lane = lax.broadcasted_iota(jnp.int32, (1, LANES), 1)
        ddt = jnp.where(lane < 8, ddt_ref[0], 0.0)
        for g in range(1, SSD_GROUPS):
            ddt = ddt + pltpu.roll(jnp.where(lane < 8, ddt_ref[g], 0.0), 8 * g, axis=1)
        ddtr = ddt * _sigmoid(dtr_ref[...] + dtb_ref[...])
        ddtr_ref[...] = ddtr.astype(bf16)
        ddtb_ref[...] += jnp.sum(ddtr, axis=0, keepdims=True)

    rev = functools.partial(_tile_spec, tm, n_tiles=nt, reverse=True)
    const = lambda shape: pl.BlockSpec(shape, lambda i: (0, 0))
    return pl.pallas_call(
        body, name="ssd_conv_bwd", grid=(nt,),
        in_specs=[rev(width=SSD_CONV_DIM), _halo_spec(tm, SSD_CONV_DIM, nt, True), _resident(conv_w.shape), _resident(conv_b.shape),
                  rev(width=LANES), _resident(dt_bias.shape), rev(width=SSD_INNER), rev(width=SSD_INNER),
                  rev(width=SSD_GROUPS * SSD_STATE), rev(width=SSD_GROUPS * SSD_STATE),
                  pl.BlockSpec((SSD_GROUPS, tm, LANES), lambda i: (0, nt - 1 - i, 0))],
        out_specs=[rev(width=SSD_CONV_DIM), rev(width=LANES), const((8, SSD_CONV_DIM)), const((1, SSD_CONV_DIM)), const((1, LANES))],
        out_shape=[jax.ShapeDtypeStruct((T, SSD_CONV_DIM), bf16), jax.ShapeDtypeStruct((T, LANES), bf16),
                   jax.ShapeDtypeStruct((8, SSD_CONV_DIM), f32), jax.ShapeDtypeStruct((1, SSD_CONV_DIM), f32), jax.ShapeDtypeStruct((1, LANES), f32)],
        scratch_shapes=[pltpu.VMEM((tm + HALO, LANES), f32), pltpu.VMEM((tm + 8, LANES), f32), pltpu.VMEM((8, SSD_CONV_DIM), f32)],
        compiler_params=_params(("arbitrary",)),
    )(xbc, xbc, conv_w, conv_b, dt_raw, dt_bias, dxs_a, dxs_b, db, dc, ddt)


def _ssd_chunk(xs, bm, cm, dt, alog, st):
    L = SSD_CHUNK
    row = lax.broadcasted_iota(jnp.int32, (L, L), 0)
    col = lax.broadcasted_iota(jnp.int32, (L, L), 1)
    causal = row >= col
    tril = jnp.where(causal, 1.0, 0.0).astype(f32)
    lane = lax.broadcasted_iota(jnp.int32, (1, LANES), 1)
    sub = lax.broadcasted_iota(jnp.int32, (LANES, 1), 0)
    lo = lane < SSD_HEAD_DIM
    last_row = sub == L - 1

    dta = dt * (-jnp.exp(alog))
    a_cs = jnp.dot(tril, dta, precision=lax.Precision.HIGHEST, preferred_element_type=f32)
    a_cs_t = a_cs.T
    bmb = bm.astype(bf16)
    cmb = cm.astype(bf16)
    cb = _mm_nt(cmb, bmb)
    c_st = _mm(cmb, st.astype(bf16))

    def head_col(v, e):
        return jnp.sum(jnp.where(lane == e, v, 0.0), axis=1, keepdims=True)

    def head_row(v, e):
        return jnp.sum(jnp.where(sub == e, v, 0.0), axis=0, keepdims=True)

    ys, sts = [], []
    for j in range(4):
        e0, e1 = 2 * j, 2 * j + 1
        c0, c1 = head_col(a_cs, e0), head_col(a_cs, e1)
        acs_x = jnp.where(lo, c0, c1)
        dt_x = jnp.where(lo, head_col(dt, e0), head_col(dt, e1))
        xd = xs[:, j * LANES:(j + 1) * LANES] * dt_x
        m0 = cb * jnp.exp(jnp.where(causal, c0 - head_row(a_cs_t, e0), NEG_BIG))
        m1 = cb * jnp.exp(jnp.where(causal, c1 - head_row(a_cs_t, e1), NEG_BIG))
        mcat = jnp.concatenate([m0, m1], axis=1).astype(bf16)
        xcat = jnp.concatenate([jnp.where(lo, xd, 0.0), jnp.where(lo, 0.0, xd)], axis=0).astype(bf16)
        y_diag = _mm(mcat, xcat)
        a_last = jnp.sum(jnp.where(last_row, acs_x, 0.0), axis=0, keepdims=True)
        x_dec = (xd * jnp.exp(a_last - acs_x)).astype(bf16)
        s_new = _mm_tn(bmb, x_dec)
        y_off = c_st[:, j * LANES:(j + 1) * LANES] * jnp.exp(acs_x)
        ys.append(y_diag + y_off)
        sts.append(jnp.exp(a_last) * st[:, j * LANES:(j + 1) * LANES] + s_new)
    return jnp.concatenate(ys, axis=1), jnp.concatenate(sts, axis=1)


SCAN_GROUPS_FWD = 4
SCAN_GROUPS_BWD = 1


def _scan_specs(nc, reverse, gs):
    L = SSD_CHUNK
    ch = (lambda c: nc - 1 - c) if reverse else (lambda c: c)
    gw = SSD_INNER // SSD_GROUPS
    b0 = SSD_INNER // (gs * SSD_STATE)
    c0 = (SSD_INNER + SSD_GROUPS * SSD_STATE) // (gs * SSD_STATE)
    xs = pl.BlockSpec((L, gs * gw), lambda g, c: (ch(c), g))
    bm = pl.BlockSpec((L, gs * SSD_STATE), lambda g, c: (ch(c), b0 + g))
    cm = pl.BlockSpec((L, gs * SSD_STATE), lambda g, c: (ch(c), c0 + g))
    dt = pl.BlockSpec((gs, L, LANES), lambda g, c: (g, ch(c), 0))
    alog = pl.BlockSpec((gs, 1, LANES), lambda g, c: (g, 0, 0))
    st = pl.BlockSpec((gs, None, SSD_STATE, gw), lambda g, c: (g, ch(c), 0, 0))
    y = pl.BlockSpec((L, gs * gw), lambda g, c: (ch(c), g))
    grp = pl.BlockSpec((L, gs * SSD_STATE), lambda g, c: (ch(c), g))
    return xs, bm, cm, dt, alog, st, y, grp


def ssd_scan_fwd(act, dt4, alog4, rider=None):
    T = act.shape[0]
    nc = T // SSD_CHUNK
    gs = SCAN_GROUPS_FWD
    ng = SSD_GROUPS // gs
    gw = SSD_INNER // SSD_GROUPS
    xs_s, bm_s, cm_s, dt_s, alog_s, st_s, y_s, _ = _scan_specs(nc, False, gs)
    r_in, r_out, r_shapes, r_scratch, r_args = _rider_specs(rider)

    def body(xs_ref, bm_ref, cm_ref, dt_ref, alog_ref, y_ref, st_ref, st_scr):
        @pl.when(pl.program_id(1) == 0)
        def _():
            st_scr[...] = jnp.zeros_like(st_scr)

        for q in range(gs):
            xc, gc = pl.ds(q * gw, gw), pl.ds(q * SSD_STATE, SSD_STATE)
            st = st_scr[q]
            st_ref[q] = st
            y, st_new = _ssd_chunk(xs_ref[:, xc].astype(f32), bm_ref[:, gc].astype(f32), cm_ref[:, gc].astype(f32), dt_ref[q], alog_ref[q], st)
            y_ref[:, xc] = y.astype(bf16)
            st_scr[q] = st_new

    first = lambda: jnp.logical_and(pl.program_id(0) == 0, pl.program_id(1) == 0)
    last = lambda: jnp.logical_and(pl.program_id(0) == ng - 1, pl.program_id(1) == nc - 1)
    return pl.pallas_call(
        _carry(body, 5, 2, rider, first, last), name="ssd_scan_fwd" if rider is None else "ssd_scan_fwd_carrying", grid=(ng, nc),
        in_specs=[xs_s, bm_s, cm_s, dt_s, alog_s] + r_in, out_specs=[y_s, st_s] + r_out,
        out_shape=[jax.ShapeDtypeStruct((T, SSD_INNER), bf16), jax.ShapeDtypeStruct((SSD_GROUPS, nc, SSD_STATE, gw), f32)] + r_shapes,
        scratch_shapes=[pltpu.VMEM((gs, SSD_STATE, gw), f32)] + r_scratch,
        compiler_params=_params(("parallel" if rider is None else "arbitrary", "arbitrary")),
    )(act, act, act, dt4, alog4, *r_args)


def ssd_scan_bwd(act, dt4, alog4, states, dy, rider=None):
    T = act.shape[0]
    nc = T // SSD_CHUNK
    gs = SCAN_GROUPS_BWD
    ng = SSD_GROUPS // gs
    gw = SSD_INNER // SSD_GROUPS
    xs_s, bm_s, cm_s, dt_s, alog_s, st_s, y_s, grp_s = _scan_specs(nc, True, gs)
    r_in, r_out, r_shapes, r_scratch, r_args = _rider_specs(rider)

    def body(xs_ref, bm_ref, cm_ref, dt_ref, alog_ref, st_ref, dy_ref, dxs_ref, db_ref, dc_ref, ddt_ref, dalog_ref, dst_scr):
        @pl.when(pl.program_id(1) == 0)
        def _():
            dst_scr[...] = jnp.zeros_like(dst_scr)
            dalog_ref[...] = jnp.zeros_like(dalog_ref)

        for q in range(gs):
            xc, gc = pl.ds(q * gw, gw), pl.ds(q * SSD_STATE, SSD_STATE)
            _, vjp = jax.vjp(_ssd_chunk, xs_ref[:, xc].astype(f32), bm_ref[:, gc].astype(f32), cm_ref[:, gc].astype(f32),
                             dt_ref[q], alog_ref[q], st_ref[q])
            dxs, dbm, dcm, ddt, dalog, dst = vjp((dy_ref[:, xc].astype(f32), dst_scr[q]))
            dxs_ref[:, xc] = dxs.astype(bf16)
            db_ref[:, gc] = dbm.astype(bf16)
            dc_ref[:, gc] = dcm.astype(bf16)
            ddt_ref[q] = ddt
            dalog_ref[q] += dalog
            dst_scr[q] = dst

    first = lambda: jnp.logical_and(pl.program_id(0) == 0, pl.program_id(1) == 0)
    last = lambda: jnp.logical_and(pl.program_id(0) == ng - 1, pl.program_id(1) == nc - 1)
    return pl.pallas_call(
        _carry(body, 7, 5, rider, first, last), name="ssd_scan_bwd" if rider is None else "ssd_scan_bwd_carrying", grid=(ng, nc),
        in_specs=[xs_s, bm_s, cm_s, dt_s, alog_s, st_s, y_s] + r_in,
        out_specs=[y_s, grp_s, grp_s, dt_s, alog_s] + r_out,
        out_shape=[jax.ShapeDtypeStruct((T, SSD_INNER), bf16), jax.ShapeDtypeStruct((T, SSD_GROUPS * SSD_STATE), bf16),
                   jax.ShapeDtypeStruct((T, SSD_GROUPS * SSD_STATE), bf16), jax.ShapeDtypeStruct((SSD_GROUPS, T, LANES), f32),
                   jax.ShapeDtypeStruct((SSD_GROUPS, 1, LANES), f32)] + r_shapes,
        scratch_shapes=[pltpu.VMEM((gs, SSD_STATE, gw), f32)] + r_scratch,
        compiler_params=_params(("parallel" if rider is None else "arbitrary", "arbitrary")),
    )(act, act, act, dt4, alog4, states, dy, *r_args)


GATE_ROWS = 256


def _ssd_gate(y, xs, z, d_x, nw):
    g = (y + xs * d_x) * (z * _sigmoid(z))
    return g * lax.rsqrt(jnp.mean(g * g, axis=-1, keepdims=True) + RMS_EPS) * nw


def _gate_blocks(tm, fn):
    gw = SSD_INNER // SSD_GROUPS

    def block(r, carry):
        rows = pl.ds(r * GATE_ROWS if isinstance(r, int) else pl.multiple_of(r * GATE_ROWS, GATE_ROWS), GATE_ROWS)
        for k in range(SSD_GROUPS):
            fn(rows, pl.ds(k * gw, gw))
        return carry

    if tm == GATE_ROWS:
        block(0, 0)
    else:
        lax.fori_loop(0, tm // GATE_ROWS, block, 0)


def ssd_gate_fwd(y, act, z, d_x, nw, tm=256):
    T = y.shape[0]

    def body(y_ref, xs_ref, z_ref, d_ref, nw_ref, o_ref):
        def one(rows, cols):
            o_ref[rows, cols] = _ssd_gate(y_ref[rows, cols].astype(f32), xs_ref[rows, cols].astype(f32), z_ref[rows, cols].astype(f32),
                                          d_ref[:, cols], nw_ref[:, cols]).astype(bf16)

        _gate_blocks(tm, one)

    return pl.pallas_call(
        body, name="ssd_gate_fwd", grid=(T // tm,),
        in_specs=[_rows(tm, SSD_INNER), _rows(tm, SSD_INNER), _rows(tm, SSD_INNER), _resident(d_x.shape), _resident(nw.shape)],
        out_specs=_rows(tm, SSD_INNER), out_shape=jax.ShapeDtypeStruct((T, SSD_INNER), bf16),
        compiler_params=_params(("parallel",)),
    )(y, act, z, d_x, nw)


def ssd_gate_bwd(y, act, z, d_x, nw, dgn, tm=256):
    T = y.shape[0]

    def body(y_ref, xs_ref, z_ref, d_ref, nw_ref, dgn_ref, dy_ref, dxs_ref, dz_ref, dd_ref, dnw_ref):
        @pl.when(pl.program_id(0) == 0)
        def _():
            dd_ref[...] = jnp.zeros_like(dd_ref)
            dnw_ref[...] = jnp.zeros_like(dnw_ref)

        def one(rows, cols):
            _, vjp = jax.vjp(_ssd_gate, y_ref[rows, cols].astype(f32), xs_ref[rows, cols].astype(f32), z_ref[rows, cols].astype(f32),
                             d_ref[:, cols], nw_ref[:, cols])
            dy, dxs, dz, dd, dnw = vjp(dgn_ref[rows, cols].astype(f32))
            dy_ref[rows, cols] = dy.astype(bf16)
            dxs_ref[rows, cols] = dxs.astype(bf16)
            dz_ref[rows, cols] = dz.astype(bf16)
            dd_ref[:, cols] += dd
            dnw_ref[:, cols] += dnw

        _gate_blocks(tm, one)

    const = pl.BlockSpec((1, SSD_INNER), lambda i: (0, 0))
    return pl.pallas_call(
        body, name="ssd_gate_bwd", grid=(T // tm,),
        in_specs=[_rows(tm, SSD_INNER), _rows(tm, SSD_INNER), _rows(tm, SSD_INNER), _resident(d_x.shape), _resident(nw.shape), _rows(tm, SSD_INNER)],
        out_specs=[_rows(tm, SSD_INNER)] * 3 + [const, const],
        out_shape=[jax.ShapeDtypeStruct((T, SSD_INNER), bf16)] * 3 + [jax.ShapeDtypeStruct((1, SSD_INNER), f32)] * 2,
        compiler_params=_params(("arbitrary",)),
    )(y, act, z, d_x, nw, dgn)


def sc_mid_fwd(bcu, conv_w, tm=256):
    T = bcu.shape[0]
    nt = T // tm
    Dm = D_MODEL

    def body(x_ref, halo_ref, w_ref, q_ref, ext_ref):
        first = pl.program_id(0) == 0
        n_s = Dm // LANES

        def strip(s, carry):
            cols, c_cols, u_cols = _strip(s), _strip(s + n_s), _strip(s + 2 * n_s)
            ext_ref[pl.ds(0, HALO), :] = jnp.where(first, 0.0, halo_ref[:, c_cols].astype(f32) * halo_ref[:, u_cols].astype(f32))
            ext_ref[pl.ds(HALO, tm), :] = x_ref[:, c_cols].astype(f32) * x_ref[:, u_cols].astype(f32)
            for r0 in range(0, tm, ROW_BLOCK):
                rows = pl.ds(r0, ROW_BLOCK)
                v, _ = _conv_rows(ext_ref, w_ref, cols, SC_CONV_K, r0)
                q_ref[rows, cols] = (x_ref[rows, cols].astype(f32) * v).astype(bf16)
            return carry

        lax.fori_loop(0, n_s, strip, 0)

    return pl.pallas_call(
        body, name="sc_mid_fwd", grid=(nt,),
        in_specs=[_rows(tm, 3 * Dm), _halo_spec(tm, 3 * Dm, nt, False), _resident(conv_w.shape)],
        out_specs=_rows(tm, Dm), out_shape=jax.ShapeDtypeStruct((T, Dm), bf16),
        scratch_shapes=[pltpu.VMEM((tm + HALO, LANES), f32)],
        compiler_params=_params(("parallel",)),
    )(bcu, bcu, conv_w)


def sc_mid_bwd(bcu, conv_w, dq, tm=256):
    T = bcu.shape[0]
    nt = T // tm
    Dm = D_MODEL
    K = SC_CONV_K

    def body(x_ref, halo_ref, w_ref, dq_ref, dx_ref, dw_ref, ext_ref, dv_ref, carry_ref):
        i = pl.program_id(0)

        @pl.when(i == 0)
        def _():
            carry_ref[...] = jnp.zeros_like(carry_ref)
            dw_ref[...] = jnp.zeros_like(dw_ref)

        first_tile = i == nt - 1
        n_s = Dm // LANES

        def strip(s, carry):
            cols, c_cols, u_cols = _strip(s), _strip(s + n_s), _strip(s + 2 * n_s)
            ext_ref[pl.ds(0, HALO), :] = jnp.where(first_tile, 0.0, halo_ref[:, c_cols].astype(f32) * halo_ref[:, u_cols].astype(f32))
            ext_ref[pl.ds(HALO, tm), :] = x_ref[:, c_cols].astype(f32) * x_ref[:, u_cols].astype(f32)
            dv_ref[pl.ds(tm, 8), :] = carry_ref[:, cols]
            dws = [jnp.zeros((1, LANES), f32) for _ in range(K)]
            for r0 in range(0, tm, ROW_BLOCK):
                rows = pl.ds(r0, ROW_BLOCK)
                v, wins = _conv_rows(ext_ref, w_ref, cols, K, r0)
                dqv = dq_ref[rows, cols].astype(f32)
                dv = dqv * x_ref[rows, cols].astype(f32)
                dv_ref[rows, :] = dv
                dx_ref[rows, cols] = (dqv * v).astype(bf16)
                for k in range(K):
                    dws[k] = dws[k] + jnp.sum(dv * wins[k], axis=0, keepdims=True)
            carry_ref[:, cols] = dv_ref[pl.ds(0, 8), :]
            for r0 in range(0, tm, ROW_BLOCK):
                rows = pl.ds(r0, ROW_BLOCK)
                dp = _shifted_back(dv_ref, w_ref, cols, K, r0)
                dx_ref[rows, c_cols] = (dp * x_ref[rows, u_cols].astype(f32)).astype(bf16)
                dx_ref[rows, u_cols] = (dp * x_ref[rows, c_cols].astype(f32)).astype(bf16)
            for k in range(K):
                dw_ref[pl.ds(k, 1), cols] += dws[k]
            return carry

        lax.fori_loop(0, n_s, strip, 0)

    return pl.pallas_call(
        body, name="sc_mid_bwd", grid=(nt,),
        in_specs=[_tile_spec(tm, 3 * Dm, nt, True), _halo_spec(tm, 3 * Dm, nt, True), _resident(conv_w.shape), _tile_spec(tm, Dm, nt, True)],
        out_specs=[_tile_spec(tm, 3 * Dm, nt, True), pl.BlockSpec((8, Dm), lambda i: (0, 0))],
        out_shape=[jax.ShapeDtypeStruct((T, 3 * Dm), bf16), jax.ShapeDtypeStruct((8, Dm), f32)],
        scratch_shapes=[pltpu.VMEM((tm + HALO, LANES), f32), pltpu.VMEM((tm + 8, LANES), f32), pltpu.VMEM((8, Dm), f32)],
        compiler_params=_params(("arbitrary",)),
    )(bcu, bcu, conv_w, dq)


def loss_head(x, fw, target, tm=512):
    T = x.shape[0]

    def body(x_ref, fw_ref, t_ref, loss_ref, dx_ref, dfw_ref):
        @pl.when(pl.program_id(0) == 0)
        def _():
            loss_ref[...] = jnp.zeros_like(loss_ref)
            dfw_ref[...] = jnp.zeros_like(dfw_ref)

        w = fw_ref[...]
        y, xh, inv = _rms_fwd(x_ref[...], w)
        err = y - t_ref[...]
        loss_ref[...] += 0.5 * jnp.sum(jnp.mean(err * err, axis=-1, keepdims=True), axis=0, keepdims=True)
        dx, dw = _rms_bwd(err * (1.0 / D_MODEL), xh, inv, w)
        dx_ref[...] = dx
        dfw_ref[...] += dw

    return pl.pallas_call(
        body, name="loss_head", grid=(T // tm,),
        in_specs=[_rows(tm, D_MODEL), _resident((1, D_MODEL)), _rows(tm, D_MODEL)],
        out_specs=[pl.BlockSpec((1, LANES), lambda i: (0, 0)), _rows(tm, D_MODEL), pl.BlockSpec((1, D_MODEL), lambda i: (0, 0))],
        out_shape=[jax.ShapeDtypeStruct((1, LANES), f32), jax.ShapeDtypeStruct((T, D_MODEL), f32), jax.ShapeDtypeStruct((1, D_MODEL), f32)],
        compiler_params=_params(("arbitrary",)),
    )(x, fw, target)


def _row_tile(rows):
    return rows if rows <= 512 else 256


def adamw(g_parts, w, m, v, name="adamw", a0=0, prev=None):
    A, B, n = w.shape
    tb = _row_tile(B)
    n_parts = len(g_parts)
    arrays, specs = [], []
    for part in g_parts:
        lead, arr = part if isinstance(part, tuple) else ((), part)
        specs.append(pl.BlockSpec((None,) * (len(lead) + 1) + (tb, n), lambda a, t, lead=lead: tuple(lead) + (a, t, 0)))
        arrays.append(arr)
    na = arrays[0].shape[-3]
    prev = list(prev) if prev is not None else []

    def body(*refs):
        n = n_parts
        g_refs = refs[:n]
        w_ref, m_ref, v_ref = refs[n:n + 3]
        go_ref, d_ref, mo_ref, vo_ref = refs[n + 3 + len(prev):]
        g = g_refs[0][...].astype(f32)
        for r in g_refs[1:]:
            g = g + r[...].astype(f32)
        m_new = ADAM_B1 * m_ref[...] + (1.0 - ADAM_B1) * g
        v_new = ADAM_B2 * v_ref[...] + (1.0 - ADAM_B2) * (g * g)
        m_hat = m_new / (1.0 - ADAM_B1 ** ADAM_STEP)
        v_hat = v_new / (1.0 - ADAM_B2 ** ADAM_STEP)
        go_ref[...] = g
        d_ref[...] = -ADAM_LR * (m_hat / (jnp.sqrt(v_hat) + ADAM_EPS) + ADAM_WD * w_ref[...])
        mo_ref[...] = m_new
        vo_ref[...] = v_new

    plain = pl.BlockSpec((None, tb, n), lambda a, t: (a + a0, t, 0))
    return pl.pallas_call(
        body, name=name, grid=(na, B // tb), in_specs=specs + [plain] * 3 + [_ANY] * len(prev), out_specs=[plain] * 4,
        out_shape=[jax.ShapeDtypeStruct((A, B, n), f32)] * 4,
        input_output_aliases={n_parts + 3 + k: k for k in range(len(prev))},
        compiler_params=_params(("parallel", "parallel")),
    )(*arrays, w, m, v, *prev)


def pair_sum_bf16(ga, gb, name):
    _, A, B, n = gb.shape
    tb = _row_tile(B)

    def body(a_ref, b_ref, o_ref):
        o_ref[...] = (a_ref[...] + b_ref[...]).astype(bf16)

    return pl.pallas_call(
        body, name=name, grid=(3, A, B // tb),
        in_specs=[pl.BlockSpec((None, None, None, tb, n), lambda j, a, t: (0, j + 1, a, t, 0)),
                  pl.BlockSpec((None, None, tb, n), lambda j, a, t: (j + 1, a, t, 0))],
        out_specs=pl.BlockSpec((None, None, tb, n), lambda j, a, t: (j + 1, a, t, 0)),
        out_shape=jax.ShapeDtypeStruct((4, A, B, n), bf16),
        compiler_params=_params(("parallel", "parallel", "parallel")),
    )(ga, gb)


def assemble(gathered, axis, tk=256):
    _, A, K, n = gathered.shape
    if axis == 1:
        def body(w_ref, o_ref):
            o_ref[...] = jnp.concatenate([w_ref[j] for j in range(N_DEV)], axis=1)

        return pl.pallas_call(
            body, name=f"assemble_cols_{K}x{n}", grid=(A, K // tk),
            in_specs=[pl.BlockSpec((N_DEV, None, tk, n), lambda a, t: (0, a, t, 0))],
            out_specs=pl.BlockSpec((None, tk, N_DEV * n), lambda a, t: (a, t, 0)),
            out_shape=jax.ShapeDtypeStruct((A, K, N_DEV * n), gathered.dtype),
            compiler_params=_params(("parallel", "parallel")),
        )(gathered)

    def body(w_ref, o_ref):
        for j in range(N_DEV):
            o_ref[pl.ds(j * K, K), :] = w_ref[j]

    return pl.pallas_call(
        body, name=f"assemble_rows_{K}x{n}", grid=(A,),
        in_specs=[pl.BlockSpec((N_DEV, None, K, n), lambda a: (0, a, 0, 0))],
        out_specs=pl.BlockSpec((None, N_DEV * K, n), lambda a: (a, 0, 0)),
        out_shape=jax.ShapeDtypeStruct((A, N_DEV * K, n), gathered.dtype),
        compiler_params=_params(("parallel",)),
    )(gathered)


SSD_IN_PAD = 5248


def assemble_ssd_in(gathered, tk=256):
    _, A, K, n = gathered.shape

    def body(w_ref, z_ref, x_ref, dt_ref, full_ref):
        full_ref[:, pl.ds(SSD_IN_PAD - LANES, LANES)] = jnp.zeros((tk, LANES), gathered.dtype)
        for j in range(N_DEV):
            full_ref[:, pl.ds(j * n, n)] = w_ref[j]
        z_ref[...] = full_ref[:, pl.ds(0, SSD_INNER)]
        x_ref[...] = full_ref[:, pl.ds(SSD_INNER, SSD_CONV_DIM)]
        dt_ref[...] = full_ref[:, pl.ds(SSD_INNER + SSD_CONV_DIM, LANES)]

    widths = (SSD_INNER, SSD_CONV_DIM, LANES)
    return pl.pallas_call(
        body, name="assemble_ssd_in", grid=(A, K // tk),
        in_specs=[pl.BlockSpec((N_DEV, None, tk, n), lambda a, t: (0, a, t, 0))],
        out_specs=[pl.BlockSpec((None, tk, w), lambda a, t: (a, t, 0)) for w in widths],
        out_shape=[jax.ShapeDtypeStruct((A, K, w), gathered.dtype) for w in widths],
        scratch_shapes=[pltpu.VMEM((tk, SSD_IN_PAD), gathered.dtype)],
        compiler_params=_params(("parallel", "parallel")),
    )(gathered)


def ssd_in_to_shards(dwz, dwx, dwdt, buf, j, tk=256):
    K = dwz.shape[0]
    n = buf.shape[-1]
    fresh = isinstance(buf, jax.ShapeDtypeStruct)

    def body(z_ref, x_ref, dt_ref, *rest):
        o_ref, full_ref = rest[-2:]
        full_ref[:, pl.ds(0, SSD_INNER)] = z_ref[...]
        full_ref[:, pl.ds(SSD_INNER, SSD_CONV_DIM)] = x_ref[...]
        full_ref[:, pl.ds(SSD_INNER + SSD_CONV_DIM, LANES)] = dt_ref[...]
        my_c, my_chip = _my_core_and_chip()
        for d in range(N_DEV):
            o_ref[(d % 2) ^ my_c, (d // 2) ^ my_chip] = full_ref[:, pl.ds(d * n, n)]

    return pl.pallas_call(
        body, name="ssd_in_to_shards", grid=(K // tk,),
        in_specs=[_rows(tk, SSD_INNER), _rows(tk, SSD_CONV_DIM), _rows(tk, LANES)] + ([] if fresh else [_ANY]),
        out_specs=pl.BlockSpec((2, 4, None, tk, n), lambda t: (0, 0, j, t, 0)),
        out_shape=jax.ShapeDtypeStruct(buf.shape, f32),
        scratch_shapes=[pltpu.VMEM((tk, SSD_IN_PAD), f32)],
        input_output_aliases={} if fresh else {3: 0},
        compiler_params=_params(("parallel",)),
    )(dwz, dwx, dwdt, *([] if fresh else [buf]))


def sum_over_devices(gathered):
    _, R, W = gathered.shape

    def body(g_ref, o_ref):
        acc = g_ref[0]
        for k in range(1, N_DEV):
            acc = acc + g_ref[k]
        o_ref[...] = acc

    return pl.pallas_call(
        body, name="sum_over_devices", grid=(1,),
        in_specs=[pl.BlockSpec((N_DEV, R, W), lambda i: (0, 0, 0))], out_specs=pl.BlockSpec((R, W), lambda i: (0, 0)),
        out_shape=jax.ShapeDtypeStruct((R, W), f32), compiler_params=_params(("arbitrary",)),
    )(gathered)


_ANY = pl.BlockSpec(memory_space=pl.ANY)


class _Exchange:
    def __init__(self, inputs, out_shapes, scratch, start, finish):
        self.inputs, self.out_shapes, self.scratch, self.start, self.finish = inputs, out_shapes, scratch, start, finish

    def run(self, name):
        ni, no = len(self.inputs), len(self.out_shapes)

        def body(*refs):
            parts = (refs[:ni], refs[ni:ni + no], refs[ni + no:])
            self.start(*parts)
            self.finish(*parts)

        return pl.pallas_call(body, name=name, in_specs=[_ANY] * ni, out_specs=[_ANY] * no, out_shape=self.out_shapes,
                              scratch_shapes=self.scratch)(*self.inputs)


def _carry(body, n_in, n_out, rider, first, last):
    if rider is None:
        return body
    ri, ro = len(rider.inputs), len(rider.out_shapes)

    def hosted(*refs):
        a, b, c = n_in + ri, n_in + ri + n_out, n_in + ri + n_out + ro
        rs = len(refs) - c - len(rider.scratch)
        parts = (refs[n_in:a], refs[b:c], refs[c + rs:])

        @pl.when(first())
        def _():
            rider.start(*parts)

        body(*refs[:n_in], *refs[a:b], *refs[c:c + rs])

        @pl.when(last())
        def _():
            rider.finish(*parts)

    return hosted


def _rider_specs(rider):
    if rider is None:
        return [], [], [], [], []
    return [_ANY] * len(rider.inputs), [_ANY] * len(rider.out_shapes), list(rider.out_shapes), list(rider.scratch), list(rider.inputs)


def all_gather(blocks):
    n = len(blocks)

    def plan(x_refs, out_refs, sems):
        send_sems, recv_sems, local_sems = sems
        x, y, c = lax.axis_index("x"), lax.axis_index("y"), lax.axis_index("c")
        me, sibling = (x, y, c), (x, y, 1 - c)
        chips = [(1 - x, y), (x, 1 - y), (1 - x, 1 - y)]

        def copy(a, k, blk, to, src=None):
            px, py, pc = blk
            slot = out_refs[a].at[4 * px + 2 * py + pc]
            return pltpu.make_async_remote_copy(
                src_ref=slot if src is None else src, dst_ref=slot,
                send_sem=send_sems.at[7 * a + k], recv_sem=recv_sems.at[7 * a + k], device_id=to, device_id_type=MESH)

        mine = [pltpu.make_async_copy(x_refs[a], out_refs[a].at[4 * x + 2 * y + c], local_sems.at[a]) for a in range(n)]
        first = []
        for a in range(n):
            first += [copy(a, 0, me, sibling, src=x_refs[a])] + [copy(a, 1 + j, me, (*chip, c), src=x_refs[a]) for j, chip in enumerate(chips)]
        return c, me, sibling, chips, copy, mine, first

    def start(x_refs, out_refs, sems):
        _, _, _, _, _, mine, first = plan(x_refs, out_refs, sems)
        for cp in mine + first:
            cp.start()

    def finish(x_refs, out_refs, sems):
        c, me, sibling, chips, copy, mine, first = plan(x_refs, out_refs, sems)
        passed = []
        for j, chip in enumerate(chips):
            for a in range(n):
                copy(a, 1 + j, (*chip, c), me).wait_recv()
                passed.append(copy(a, 4 + j, (*chip, c), sibling))
                passed[-1].start()
        for a in range(n):
            copy(a, 0, sibling, me).wait_recv()
            for j, chip in enumerate(chips):
                copy(a, 4 + j, (*chip, 1 - c), me).wait_recv()
        for cp in first + passed:
            cp.wait_send()
        for cp in mine:
            cp.wait()

    return _Exchange(list(blocks), [jax.ShapeDtypeStruct((N_DEV,) + b.shape, b.dtype) for b in blocks],
                     [pltpu.SemaphoreType.DMA((7 * n,)), pltpu.SemaphoreType.DMA((7 * n,)), pltpu.SemaphoreType.DMA((n,))], start, finish)


def exchange_with_sibling(gs):
    n = len(gs)

    def plan(g_refs, recv_refs, sems):
        send_sems, recv_sems = sems
        x, y, c = lax.axis_index("x"), lax.axis_index("y"), lax.axis_index("c")
        return [pltpu.make_async_remote_copy(src_ref=g_refs[a].at[1], dst_ref=recv_refs[a], send_sem=send_sems.at[a],
                                             recv_sem=recv_sems.at[a], device_id=(x, y, 1 - c), device_id_type=MESH) for a in range(n)]

    def start(*refs):
        for cp in plan(*refs):
            cp.start()

    def finish(*refs):
        for cp in plan(*refs):
            cp.wait()

    return _Exchange(list(gs), [jax.ShapeDtypeStruct(g.shape[1:], g.dtype) for g in gs],
                     [pltpu.SemaphoreType.DMA((n,)), pltpu.SemaphoreType.DMA((n,))], start, finish)


def exchange_between_chips(parts):
    n = len(parts)

    def plan(p_refs, recv_refs, sems):
        send_sems, recv_sems = sems
        x, y, c = lax.axis_index("x"), lax.axis_index("y"), lax.axis_index("c")
        chips = [(2, (1 - x, y)), (1, (x, 1 - y)), (3, (1 - x, 1 - y))]
        return [pltpu.make_async_remote_copy(src_ref=p_refs[a].at[slot], dst_ref=recv_refs[a].at[k], send_sem=send_sems.at[3 * a + k],
                                             recv_sem=recv_sems.at[3 * a + k], device_id=(px, py, c), device_id_type=MESH)
                for a in range(n) for k, (slot, (px, py)) in enumerate(chips)]

    def start(*refs):
        for cp in plan(*refs):
            cp.start()

    def finish(*refs):
        for cp in plan(*refs):
            cp.wait()

    return _Exchange(list(parts), [jax.ShapeDtypeStruct((3,) + p.shape[1:], p.dtype) for p in parts],
                     [pltpu.SemaphoreType.DMA((3 * n,)), pltpu.SemaphoreType.DMA((3 * n,))], start, finish)


PARAMS = {
    "norm_w": ((DEPTH, 3, D_MODEL), 2),
    "ffn_w_gate": ((DEPTH, 2, D_MODEL, D_FF), 3),
    "ffn_w_up": ((DEPTH, 2, D_MODEL, D_FF), 3),
    "ffn_w_down": ((DEPTH, 2, D_FF, D_MODEL), 2),
    "ssd_w_in": ((2, D_MODEL, SSD_IN_DIM), 2),
    "ssd_conv_w": ((2, SSD_CONV_K, SSD_CONV_DIM), 2),
    "ssd_conv_b": ((2, SSD_CONV_DIM), None),
    "ssd_dt_bias": ((2, SSD_HEADS), None),
    "ssd_a_log": ((2, SSD_HEADS), None),
    "ssd_d": ((2, SSD_HEADS), None),
    "ssd_norm_w": ((2, SSD_INNER), None),
    "ssd_w_out": ((2, SSD_INNER, D_MODEL), 1),
    "sc_w_in": ((2, D_MODEL, 3 * D_MODEL), 2),
    "sc_conv_w": ((2, SC_CONV_K, D_MODEL), 2),
    "sc_w_out": ((2, D_MODEL, D_MODEL), 1),
    "final_norm_w": ((D_MODEL,), None),
}
NAMES = list(PARAMS)
BIG = ["ffn_w_gate", "ffn_w_up", "ffn_w_down", "ssd_w_in", "ssd_w_out", "sc_w_in", "sc_w_out"]
SMALL = [n for n in NAMES if n not in BIG]
SMALL_SHARDED = [n for n in SMALL if PARAMS[n][1] is not None]


def _round_up(n, m):
    return -(-n // m) * m


def _pack(flat_list, rows_multiple):
    flat = jnp.concatenate(flat_list)
    rows = _round_up(_round_up(flat.shape[0], PACK_W) // PACK_W, rows_multiple)
    return jnp.pad(flat, (0, rows * PACK_W - flat.shape[0])).reshape(rows, PACK_W)


def _unpack(packed, shapes, lead=()):
    flat = packed.reshape(lead + (-1,))
    out, off = [], 0
    for shp in shapes:
        n = 1
        for s in shp:
            n *= s
        out.append(flat[..., off:off + n].reshape(lead + tuple(shp)))
        off += n
    return out


def _local_shape(name):
    shp, ax = PARAMS[name]
    if ax is None:
        return shp
    return shp[:ax] + (shp[ax] // N_DEV,) + shp[ax + 1:]


def _full_from_gathered(g, name):
    shp, ax = PARAMS[name]
    return jnp.moveaxis(g, 0, ax).reshape(shp)


def _by_destination(full, name):
    shp, ax = PARAMS[name]
    loc = shp[ax] // N_DEV
    return jnp.moveaxis(full.reshape(shp[:ax] + (N_DEV, loc) + shp[ax + 1:]), ax, 0)


def _ssd_layer_fwd(xin, nw, p, rider=None):
    z, xbc, dt_raw = in_proj_fwd(xin, nw, [p["ssd_wz"], p["ssd_wx"], p["ssd_wdt"]], [bf16, bf16, f32])
    act, dt4 = ssd_conv_fwd(xbc, p["ssd_conv_w"], p["ssd_conv_b"], dt_raw, p["ssd_dt_bias"])
    y, states, *got = ssd_scan_fwd(act, dt4, p["ssd_alog4"], rider=rider)
    gn = ssd_gate_fwd(y, act, z, p["ssd_dx"], p["ssd_norm_w"])
    xout = out_proj_fwd(xin, gn, p["ssd_w_out"])
    return xout, (xin, z, xbc, dt_raw, act, dt4, y, states, gn), got


def _ssd_layer_bwd(dxo, nw, p, saved, gbuf, slab, rider=None):
    xin, z, xbc, dt_raw, act, dt4, y, states, gn = saved
    T = xin.shape[0]
    dgn, dyb = out_proj_bwd(dxo, p["ssd_w_out"])
    gbuf["ssd_w_out"] = tn_matmul_to_shards(gn, dyb, gbuf["ssd_w_out"], (slab,), 0)
    g = {}
    dy, dxs_skip, dz, dd_x, dgnw = ssd_gate_bwd(y, act, z, p["ssd_dx"], p["ssd_norm_w"], dgn)
    g["ssd_norm_w"] = dgnw[0]
    g["ssd_d"] = jnp.sum(dd_x.reshape(SSD_HEADS, SSD_HEAD_DIM), axis=1)
    dxs, db, dc, ddt4, dalog4, *got = ssd_scan_bwd(act, dt4, p["ssd_alog4"], states, dy, rider=rider)
    g["ssd_a_log"] = dalog4[:, 0, :8].reshape(SSD_HEADS)
    dxbc, ddt_raw, dcw, dcb, ddtb = ssd_conv_bwd(xbc, p["ssd_conv_w"], p["ssd_conv_b"], dt_raw, p["ssd_dt_bias"], dxs, dxs_skip, db, dc, ddt4)
    g["ssd_conv_w"] = dcw[:SSD_CONV_K]
    g["ssd_conv_b"] = dcb[0]
    g["ssd_dt_bias"] = ddtb[0, :SSD_HEADS]
    dx, h, dnw = in_proj_bwd(xin, nw, dxo, [dz, dxbc, ddt_raw], [p["ssd_wz"], p["ssd_wx"], p["ssd_wdt"]])
    gbuf["ssd_w_in"] = ssd_in_to_shards(tn_matmul(h, dz), tn_matmul(h, dxbc), tn_matmul(h, ddt_raw), gbuf["ssd_w_in"], slab)
    return dx, dnw, g, got


def _sc_layer_fwd(xin, nw, p):
    (bcu,) = in_proj_fwd(xin, nw, [p["sc_w_in"]], [bf16])
    q = sc_mid_fwd(bcu, p["sc_conv_w"])
    return out_proj_fwd(xin, q, p["sc_w_out"]), (xin, bcu, q)


def _sc_layer_bwd(dxo, nw, p, saved, gbuf, slab):
    xin, bcu, q = saved
    dq, dyb = out_proj_bwd(dxo, p["sc_w_out"])
    gbuf["sc_w_out"] = tn_matmul_to_shards(q, dyb, gbuf["sc_w_out"], (slab,), 0)
    dbcu, dcw = sc_mid_bwd(bcu, p["sc_conv_w"], dq)
    g = {"sc_conv_w": dcw[:SC_CONV_K]}
    dx, h, dnw = in_proj_bwd(xin, nw, dxo, [dbcu], [p["sc_w_in"]])
    gbuf["sc_w_in"] = tn_matmul_to_shards(h, dbcu, gbuf["sc_w_in"], (slab,), 1)
    return dx, dnw, g


def kernel(x, norm_w, ffn_w_gate, ffn_w_up, ffn_w_down, ssd_w_in, ssd_conv_w, ssd_conv_b, ssd_dt_bias, ssd_a_log, ssd_d, ssd_norm_w, ssd_w_out, sc_w_in, sc_conv_w, sc_w_out, final_norm_w, loss_target, m_norm_w, m_ffn_w_gate, m_ffn_w_up, m_ffn_w_down, m_ssd_w_in, m_ssd_conv_w, m_ssd_conv_b, m_ssd_dt_bias, m_ssd_a_log, m_ssd_d, m_ssd_norm_w, m_ssd_w_out, m_sc_w_in, m_sc_conv_w, m_sc_w_out, m_final_norm_w, v_norm_w, v_ffn_w_gate, v_ffn_w_up, v_ffn_w_down, v_ssd_w_in, v_ssd_conv_w, v_ssd_conv_b, v_ssd_dt_bias, v_ssd_a_log, v_ssd_d, v_ssd_norm_w, v_ssd_w_out, v_sc_w_in, v_sc_conv_w, v_sc_w_out, v_final_norm_w):
    w_loc = dict(zip(NAMES, (norm_w, ffn_w_gate, ffn_w_up, ffn_w_down, ssd_w_in, ssd_conv_w, ssd_conv_b, ssd_dt_bias, ssd_a_log, ssd_d, ssd_norm_w, ssd_w_out, sc_w_in, sc_conv_w, sc_w_out, final_norm_w)))
    m_loc = dict(zip(NAMES, (m_norm_w, m_ffn_w_gate, m_ffn_w_up, m_ffn_w_down, m_ssd_w_in, m_ssd_conv_w, m_ssd_conv_b, m_ssd_dt_bias, m_ssd_a_log, m_ssd_d, m_ssd_norm_w, m_ssd_w_out, m_sc_w_in, m_sc_conv_w, m_sc_w_out, m_final_norm_w)))
    v_loc = dict(zip(NAMES, (v_norm_w, v_ffn_w_gate, v_ffn_w_up, v_ffn_w_down, v_ssd_w_in, v_ssd_conv_w, v_ssd_conv_b, v_ssd_dt_bias, v_ssd_a_log, v_ssd_d, v_ssd_norm_w, v_ssd_w_out, v_sc_w_in, v_sc_conv_w, v_sc_w_out, v_final_norm_w)))
    ax, ay, ac = lax.axis_index("x"), lax.axis_index("y"), lax.axis_index("c")
    my_chip = 2 * ax + ay
    my_dev = 4 * ax + 2 * ay + ac
    T = x.shape[1]

    def as3d(a):
        return a.reshape((-1,) + a.shape[-2:])

    wb = {n: as3d(w_loc[n]).astype(bf16) for n in BIG}

    FFN = ["ffn_w_gate", "ffn_w_up", "ffn_w_down"]

    def mixer_names(i):
        return ["ssd_w_in", "ssd_w_out"] if i % 2 == 0 else ["sc_w_in", "sc_w_out"]

    ag_sets = [[(n, 0, 1) for n in FFN], [(n, 1, 1) for n in FFN] + [(n, 0, 1) for n in mixer_names(0)]]
    ag_sets += [[(n, 2 * r, 2) for n in FFN] + [(n, r // 2, 1) for n in mixer_names(r)] for r in (1, 2, 3)]

    def set_blocks(spec):
        return [wb[n][a0:a0 + na] for n, a0, na in spec]

    def set_weights(spec, gathered):
        q = {}
        for (n, _, _), g in zip(spec, gathered):
            if n == "ssd_w_in":
                q["ssd_wz"], q["ssd_wx"], q["ssd_wdt"] = assemble_ssd_in(g)
            else:
                q[n] = assemble(g, 1 if PARAMS[n][1] == len(PARAMS[n][0]) - 1 else 0)
        return q

    ss_shapes = [_local_shape(n) for n in SMALL_SHARDED]
    gathered0 = all_gather(set_blocks(ag_sets[0]) + [_pack([w_loc[n].reshape(-1) for n in SMALL_SHARDED], 8)]).run("all_gather_first")
    full = {}
    for n, part in zip(SMALL_SHARDED, _unpack(gathered0[-1], ss_shapes, lead=(N_DEV,))):
        full[n] = _full_from_gathered(part, n)
    for n in SMALL:
        if PARAMS[n][1] is None:
            full[n] = w_loc[n]
    small = {
        "ssd_conv_w": full["ssd_conv_w"],
        "ssd_conv_b": full["ssd_conv_b"].reshape(2, 1, SSD_CONV_DIM),
        "ssd_dt_bias": jnp.pad(full["ssd_dt_bias"], ((0, 0), (0, LANES - SSD_HEADS))).reshape(2, 1, LANES),
        "ssd_alog4": jnp.pad(full["ssd_a_log"].reshape(2, SSD_GROUPS, 1, 8), ((0, 0), (0, 0), (0, 0), (0, LANES - 8))),
        "ssd_dx": jnp.repeat(full["ssd_d"], SSD_HEAD_DIM, axis=1).reshape(2, 1, SSD_INNER),
        "ssd_norm_w": full["ssd_norm_w"].reshape(2, 1, SSD_INNER),
        "sc_conv_w": full["sc_conv_w"],
    }
    nw_all = full["norm_w"].reshape(DEPTH, 3, 1, D_MODEL)

    ffn_w = [[None, None] for _ in range(DEPTH)]
    mix_w = [None] * DEPTH

    def arrived(s, gathered):
        q = set_weights(ag_sets[s], gathered)
        ffn = tuple(q[n] for n in FFN)
        if s == 0:
            ffn_w[0][0] = ffn + ((0,),)
            return
        i = 0 if s == 1 else s - 1
        if s == 1:
            ffn_w[0][1] = ffn + ((0,),)
        else:
            ffn_w[i] = [ffn + ((0,),), ffn + ((1,),)]
        m = {n: v[0] for n, v in q.items() if n not in FFN}
        m.update({n: v[i // 2] for n, v in small.items() if n.startswith("ssd" if i % 2 == 0 else "sc")})
        mix_w[i] = m

    def rider_for(s):
        return all_gather(set_blocks(ag_sets[s]))

    xc = x[0]
    saved = []
    arrived(0, gathered0[:-1])
    for i in range(DEPTH):
        carried = {0: (1, 2, 3), 1: (4, None, None)}.get(i, (None, None, None))
        wg, wu, wd, idx = ffn_w[i][0]
        x1, g1, u1, a1, *got = ffn_fwd(xc, nw_all[i, 0], wg, wu, wd, idx, rider=rider_for(carried[0]) if carried[0] else None)
        if carried[0]:
            arrived(carried[0], got)
        if i % 2 == 0:
            x2, mix_saved, got = _ssd_layer_fwd(x1, nw_all[i, 1], mix_w[i], rider=rider_for(carried[1]) if carried[1] else None)
            if carried[1]:
                arrived(carried[1], got)
        else:
            x2, mix_saved = _sc_layer_fwd(x1, nw_all[i, 1], mix_w[i])
        wg, wu, wd, idx = ffn_w[i][1]
        x3, g3, u3, a3, *got = ffn_fwd(x2, nw_all[i, 2], wg, wu, wd, idx, rider=rider_for(carried[2]) if carried[2] else None)
        if carried[2]:
            arrived(carried[2], got)
        saved.append(((xc, g1, u1, a1), mix_saved, (x2, g3, u3, a3)))
        xc = x3

    loss_row, dx, dfw = loss_head(xc, full["final_norm_w"].reshape(1, D_MODEL), loss_target[0])
    loss = lax.psum(loss_row[0, 0], ("x", "y", "c"))

    grads = {n: [None] * PARAMS[n][0][0] for n in SMALL if n != "final_norm_w"}
    grads["final_norm_w"] = dfw[0]
    dnorm = [[None] * 3 for _ in range(DEPTH)]
    def slabs(n, which):
        if n.startswith("ffn"):
            return {"early": (2, 6), "mid": (1, 1), "last": (0, 1)}[which]
        if n.startswith("ssd"):
            return {"early": (1, 1), "mid": (0, 1), "last": (0, 0)}[which]
        return {"early": (0, 2), "mid": (0, 0), "last": (0, 0)}[which]

    gb = {which: {n: jax.ShapeDtypeStruct((2, 4, slabs(n, which)[1]) + wb[n].shape[1:], f32) for n in BIG if slabs(n, which)[1]}
          for which in ("early", "mid", "last")}

    def ffn_back(i, k, dxo, sv, rider=None):
        xin, g_, u_, a_ = sv
        which = "early" if i > 0 else ("mid" if k == 1 else "last")
        gbuf = gb[which]
        slab = 2 * i + k - slabs("ffn_w_gate", which)[0]
        wg, wu, wd, idx = ffn_w[i][k]
        dxi, h, dyb, dg, du, dnw, *got = ffn_bwd_dx(xin, dxo, g_, u_, nw_all[i, 2 * k], wg, wu, wd, idx, rider=rider)
        dnorm[i][2 * k] = dnw[0]
        gbuf["ffn_w_gate"] = tn_matmul_to_shards(h, dg, gbuf["ffn_w_gate"], (slab,), 1)
        gbuf["ffn_w_up"] = tn_matmul_to_shards(h, du, gbuf["ffn_w_up"], (slab,), 1)
        gbuf["ffn_w_down"] = tn_matmul_to_shards(a_, dyb, gbuf["ffn_w_down"], (slab,), 0)
        return dxi, got

    def reduce_in_chip(gbuf, from_sibling=None):
        names = list(gbuf)
        bufs = [gbuf[n] for n in names]
        if from_sibling is None:
            from_sibling = exchange_with_sibling(bufs).run("exchange_with_sibling")
        return names, bufs, from_sibling, [pair_sum_bf16(g, fs, "pair_sum_" + n) for n, g, fs in zip(names, bufs, from_sibling)]

    reduced, from_chips = {}, {}
    for i in reversed(range(DEPTH)):
        j = i // 2
        sv_a, sv_mix, sv_b = saved[i]
        if i == 0:
            dx, got = ffn_back(i, 1, dx, sv_b, rider=exchange_with_sibling(list(gb["early"].values())))
            reduced["early"] = reduce_in_chip(gb["early"], from_sibling=got)
        else:
            dx, _ = ffn_back(i, 1, dx, sv_b)
        if i % 2 == 0:
            rider = exchange_between_chips(reduced["early"][3]) if i == 0 else None
            dx, dnw, gm, got = _ssd_layer_bwd(dx, nw_all[i, 1], mix_w[i], sv_mix, gb["mid" if i == 0 else "early"], 0, rider=rider)
            if i == 0:
                from_chips["early"] = got
                reduced["mid"] = reduce_in_chip(gb["mid"])
        else:
            dx, dnw, gm = _sc_layer_bwd(dx, nw_all[i, 1], mix_w[i], sv_mix, gb["early"], j)
        dnorm[i][1] = dnw[0]
        for n, val in gm.items():
            grads[n][j] = val
        dx, got = ffn_back(i, 0, dx, sv_a, rider=exchange_between_chips(reduced["mid"][3]) if i == 0 else None)
        if i == 0:
            from_chips["mid"] = got

    grads["norm_w"] = jnp.stack([jnp.stack(r) for r in dnorm])
    for n in SMALL:
        if isinstance(grads[n], list):
            grads[n] = jnp.stack(grads[n])

    reduced["last"] = reduce_in_chip(gb["last"])
    from_chips["last"] = exchange_between_chips(reduced["last"][3]).run("exchange_between_chips")
    results = [{}, {}, {}, {}]
    outs = {}
    for which in ("last", "mid", "early"):
        names, bufs, from_sibling, _ = reduced[which]
        for n, g, fs, fc in zip(names, bufs, from_sibling, from_chips[which]):
            parts = [((0, 0), g), ((0,), fs), ((0,), fc), ((1,), fc), ((2,), fc)]
            outs[n] = adamw(parts, as3d(w_loc[n]), as3d(m_loc[n]), as3d(v_loc[n]), name="adamw_" + n + "_" + which,
                            a0=slabs(n, which)[0], prev=outs.get(n))
    for n in BIG:
        for k in range(4):
            results[k][n] = outs[n][k].reshape(_local_shape(n))

    g_small = _pack([grads[n].reshape(-1) for n in SMALL], 8)
    g_small = sum_over_devices(all_gather([g_small]).run("all_gather_small_grads")[0])
    g_small_full = dict(zip(SMALL, _unpack(g_small, [PARAMS[n][0] for n in SMALL])))
    g_small_loc = []
    for n in SMALL:
        if PARAMS[n][1] is None:
            g_small_loc.append(g_small_full[n])
        else:
            g_small_loc.append(lax.dynamic_index_in_dim(_by_destination(g_small_full[n], n), my_dev, axis=0, keepdims=False))
    small_shapes = [_local_shape(n) for n in SMALL]
    pack_small = lambda d: _pack([d[n].reshape(-1) for n in SMALL], 8)[None]
    small_out = adamw([_pack([gl.reshape(-1) for gl in g_small_loc], 8)[None]], pack_small(w_loc), pack_small(m_loc), pack_small(v_loc), name="adamw_small")
    for k in range(4):
        results[k].update(zip(SMALL, _unpack(small_out[k], small_shapes)))
    return (loss, dx[None], *[results[0][n] for n in NAMES], *[results[1][n] for n in NAMES],
            *[results[2][n] for n in NAMES], *[results[3][n] for n in NAMES])
```

```python
import functools

import jax
import jax.numpy as jnp
from jax import lax
from jax.experimental import pallas as pl
from jax.experimental.pallas import tpu as pltpu

f32 = jnp.float32
bf16 = jnp.bfloat16

D_MODEL = 1024
D_FF = 2816
DEPTH = 4
SSD_INNER = 2048
SSD_HEADS = 32
SSD_HEAD_DIM = 64
SSD_GROUPS = 4
SSD_STATE = 128
SSD_CONV_K = 4
SSD_CONV_DIM = 3072
SSD_IN_DIM = 5152
SSD_CHUNK = 128
SC_CONV_K = 3
RMS_EPS = 1e-5
N_DEV = 8
LANES = 128
HALO = 16
PACK_W = 1024
PACK_TILE = 256
VMEM_LIMIT = 56 * 1024 * 1024
NEG_BIG = -1e30

ADAM_LR = 0.001
ADAM_B1 = 0.9
ADAM_B2 = 0.999
ADAM_EPS = 1e-08
ADAM_WD = 0.01
ADAM_STEP = 10

NT_DIMS = (((1,), (1,)), ((), ()))
TN_DIMS = (((0,), (0,)), ((), ()))
MESH = pl.DeviceIdType.MESH


def _params(sem=None):
    return pltpu.CompilerParams(dimension_semantics=sem, vmem_limit_bytes=VMEM_LIMIT)


def _resident(shape):
    nd = len(shape)
    return pl.BlockSpec(tuple(shape), lambda *_: (0,) * nd, pipeline_mode=pl.Buffered(1))


def _rows(tm, width):
    return pl.BlockSpec((tm, width), lambda i: (i, 0))


def _my_core_and_chip():
    return lax.axis_index("c"), 2 * lax.axis_index("x") + lax.axis_index("y")


def _sigmoid(v):
    return 0.5 * jnp.tanh(0.5 * v) + 0.5


def _softplus(v):
    return jnp.maximum(v, 0.0) + jnp.log(1.0 + jnp.exp(-jnp.abs(v)))


def _rms_fwd(xv, w):
    inv = lax.rsqrt(jnp.mean(xv * xv, axis=-1, keepdims=True) + RMS_EPS)
    xh = xv * inv
    return xh * w, xh, inv


def _rms_bwd(dh, xh, inv, w):
    dxh = dh * w
    dx = inv * (dxh - xh * jnp.mean(dxh * xh, axis=-1, keepdims=True))
    return dx, jnp.sum(dh * xh, axis=0, keepdims=True)


def _mm(a, b):
    return jnp.dot(a, b, preferred_element_type=f32)


def _mm_nt(a, b):
    return lax.dot_general(a, b, NT_DIMS, preferred_element_type=f32)


def _mm_tn(a, b):
    return lax.dot_general(a, b, TN_DIMS, preferred_element_type=f32)


FFN_CHUNK = D_FF


def _ffn_chunks():
    return [(c0, min(FFN_CHUNK, D_FF - c0)) for c0 in range(0, D_FF, FFN_CHUNK)]


def _layer_slab(w, idx):
    tail = w.shape[len(idx):]
    return pl.BlockSpec((None,) * len(idx) + tuple(tail), lambda *_: tuple(idx) + (0,) * len(tail), pipeline_mode=pl.Buffered(1))


def ffn_fwd(x, nw, wg, wu, wd, idx, tm=512, rider=None):
    T = x.shape[0]
    nt = T // tm
    r_in, r_out, r_shapes, r_scratch, r_args = _rider_specs(rider)

    def body(x_ref, nw_ref, wg_ref, wu_ref, wd_ref, xo_ref, g_ref, u_ref, a_ref):
        xv = x_ref[...]
        h, _, _ = _rms_fwd(xv, nw_ref[...])
        hb = h.astype(bf16)
        y = None
        for c0, fc in _ffn_chunks():
            cols = pl.ds(c0, fc)
            g = _mm(hb, wg_ref[:, cols])
            u = _mm(hb, wu_ref[:, cols])
            ab = (g * _sigmoid(g) * u).astype(bf16)
            g_ref[:, cols] = g.astype(bf16)
            u_ref[:, cols] = u.astype(bf16)
            a_ref[:, cols] = ab
            part = _mm(ab, wd_ref[cols, :])
            y = part if y is None else y + part
        xo_ref[...] = xv + 0.5 * y

    hosted = _carry(body, 5, 4, rider, lambda: pl.program_id(0) == 0, lambda: pl.program_id(0) == nt - 1)
    return pl.pallas_call(
        hosted, name="ffn_fwd" if rider is None else "ffn_fwd_carrying", grid=(nt,),
        in_specs=[_rows(tm, D_MODEL), _resident((1, D_MODEL)), _layer_slab(wg, idx), _layer_slab(wu, idx), _layer_slab(wd, idx)] + r_in,
        out_specs=[_rows(tm, D_MODEL), _rows(tm, D_FF), _rows(tm, D_FF), _rows(tm, D_FF)] + r_out,
        out_shape=[jax.ShapeDtypeStruct((T, D_MODEL), f32)] + [jax.ShapeDtypeStruct((T, D_FF), bf16)] * 3 + r_shapes,
        scratch_shapes=r_scratch,
        compiler_params=_params(("parallel",) if rider is None else ("arbitrary",)),
    )(x, nw, wg, wu, wd, *r_args)


def ffn_bwd_dx(x, dxo, g, u, nw, wg, wu, wd, idx, tm=256, rider=None):
    T = x.shape[0]
    nt = T // tm
    r_in, r_out, r_shapes, r_scratch, r_args = _rider_specs(rider)

    def body(x_ref, dxo_ref, g_ref, u_ref, nw_ref, wg_ref, wu_ref, wd_ref, dx_ref, h_ref, dy_ref, dg_ref, du_ref, dnw_ref):
        w = nw_ref[...]
        h, xh, inv = _rms_fwd(x_ref[...], w)
        dxo_v = dxo_ref[...]
        dyb = (0.5 * dxo_v).astype(bf16)
        dh = None
        for c0, fc in _ffn_chunks():
            cols = pl.ds(c0, fc)
            da = _mm_nt(dyb, wd_ref[cols, :])
            gv = g_ref[:, cols].astype(f32)
            uv = u_ref[:, cols].astype(f32)
            s = _sigmoid(gv)
            dgb = (da * uv * (s * (1.0 + gv * (1.0 - s)))).astype(bf16)
            dub = (da * (gv * s)).astype(bf16)
            dg_ref[:, cols] = dgb
            du_ref[:, cols] = dub
            part = _mm_nt(dgb, wg_ref[:, cols]) + _mm_nt(dub, wu_ref[:, cols])
            dh = part if dh is None else dh + part
        dxn, dw = _rms_bwd(dh, xh, inv, w)
        dx_ref[...] = dxo_v + dxn
        h_ref[...] = h.astype(bf16)
        dy_ref[...] = dyb

        @pl.when(pl.program_id(0) == 0)
        def _():
            dnw_ref[...] = jnp.zeros_like(dnw_ref)

        dnw_ref[...] += dw

    hosted = _carry(body, 8, 6, rider, lambda: pl.program_id(0) == 0, lambda: pl.program_id(0) == nt - 1)
    return pl.pallas_call(
        hosted, name="ffn_bwd_dx" if rider is None else "ffn_bwd_dx_carrying", grid=(nt,),
        in_specs=[_rows(tm, D_MODEL), _rows(tm, D_MODEL), _rows(tm, D_FF), _rows(tm, D_FF), _resident((1, D_MODEL)),
                  _layer_slab(wg, idx), _layer_slab(wu, idx), _layer_slab(wd, idx)] + r_in,
        out_specs=[_rows(tm, D_MODEL), _rows(tm, D_MODEL), _rows(tm, D_MODEL), _rows(tm, D_FF), _rows(tm, D_FF),
                   pl.BlockSpec((1, D_MODEL), lambda i: (0, 0))] + r_out,
        out_shape=[jax.ShapeDtypeStruct((T, D_MODEL), f32), jax.ShapeDtypeStruct((T, D_MODEL), bf16), jax.ShapeDtypeStruct((T, D_MODEL), bf16),
                   jax.ShapeDtypeStruct((T, D_FF), bf16), jax.ShapeDtypeStruct((T, D_FF), bf16), jax.ShapeDtypeStruct((1, D_MODEL), f32)] + r_shapes,
        scratch_shapes=r_scratch,
        compiler_params=_params(("arbitrary",)),
    )(x, dxo, g, u, nw, wg, wu, wd, *r_args)


def tn_matmul(a, b, tk=512):
    T, M = a.shape
    N = b.shape[1]
    bn = N if M * N <= 3_200_000 else N // 2
    nk = T // tk

    def body(a_ref, b_ref, o_ref):
        @pl.when(pl.program_id(1) == 0)
        def _():
            o_ref[...] = jnp.zeros_like(o_ref)

        o_ref[...] += _mm_tn(a_ref[...], b_ref[...])

    return pl.pallas_call(
        body, name=f"tn_matmul_{M}x{N}", grid=(N // bn, nk),
        in_specs=[pl.BlockSpec((tk, M), lambda j, k: (k, 0)), pl.BlockSpec((tk, bn), lambda j, k: (k, j))],
        out_specs=pl.BlockSpec((M, bn), lambda j, k: (0, j)),
        out_shape=jax.ShapeDtypeStruct((M, N), f32),
        compiler_params=_params(("parallel", "arbitrary")),
    )(a, b)


def tn_matmul_to_shards(a, b, buf, idx, axis, tk=512):
    T, M = a.shape
    N = b.shape[1]
    m, n = buf.shape[-2:]
    nk = T // tk
    fresh = isinstance(buf, jax.ShapeDtypeStruct)

    def body(a_ref, b_ref, *rest):
        o_ref, acc_ref = rest[-2:]
        k = pl.program_id(0)

        @pl.when(k == 0)
        def _():
            acc_ref[...] = jnp.zeros_like(acc_ref)

        acc_ref[...] += _mm_tn(a_ref[...], b_ref[...])

        @pl.when(k == nk - 1)
        def _():
            my_c, my_chip = _my_core_and_chip()
            for d in range(N_DEV):
                piece = acc_ref[:, pl.ds(d * n, n)] if axis == 1 else acc_ref[pl.ds(d * m, m), :]
                o_ref[(d % 2) ^ my_c, (d // 2) ^ my_chip] = piece

    none = (None,) * len(idx)
    return pl.pallas_call(
        body, name=f"tn_matmul_to_shards_{M}x{N}_{axis}", grid=(nk,),
        in_specs=[pl.BlockSpec((tk, M), lambda k: (k, 0)), pl.BlockSpec((tk, N), lambda k: (k, 0))] + ([] if fresh else [_ANY]),
        out_specs=pl.BlockSpec((2, 4) + none + (m, n), lambda k: (0, 0) + tuple(idx) + (0, 0)),
        out_shape=jax.ShapeDtypeStruct(buf.shape, f32),
        scratch_shapes=[pltpu.VMEM((M, N), f32)],
        input_output_aliases={} if fresh else {2: 0},
        compiler_params=_params(("arbitrary",)),
    )(a, b, *([] if fresh else [buf]))


def in_proj_fwd(x, nw, ws, out_dtypes, tm=512):
    T = x.shape[0]
    n = len(ws)

    def body(*refs):
        x_ref, nw_ref = refs[:2]
        w_refs = refs[2:2 + n]
        o_refs = refs[2 + n:]
        h, _, _ = _rms_fwd(x_ref[...], nw_ref[...])
        hb = h.astype(bf16)
        for w_ref, o_ref in zip(w_refs, o_refs):
            o_ref[...] = _mm(hb, w_ref[...]).astype(o_ref.dtype)

    return pl.pallas_call(
        body, name="in_proj_fwd_" + "_".join(str(w.shape[1]) for w in ws), grid=(T // tm,),
        in_specs=[_rows(tm, D_MODEL), _resident((1, D_MODEL))] + [_resident(w.shape) for w in ws],
        out_specs=[_rows(tm, w.shape[1]) for w in ws],
        out_shape=[jax.ShapeDtypeStruct((T, w.shape[1]), dt) for w, dt in zip(ws, out_dtypes)],
        compiler_params=_params(("parallel",)),
    )(x, nw, *ws)


def in_proj_bwd(x, nw, dxo, dys, ws, tm=512):
    T = x.shape[0]
    n = len(ws)

    def body(*refs):
        x_ref, nw_ref, dxo_ref = refs[:3]
        dy_refs = refs[3:3 + n]
        w_refs = refs[3 + n:3 + 2 * n]
        dx_ref, h_ref, dnw_ref = refs[3 + 2 * n:]
        w = nw_ref[...]
        h, xh, inv = _rms_fwd(x_ref[...], w)
        dh = _mm_nt(dy_refs[0][...], w_refs[0][...])
        for dy_ref, w_ref in zip(dy_refs[1:], w_refs[1:]):
            dh = dh + _mm_nt(dy_ref[...], w_ref[...])
        dxn, dw = _rms_bwd(dh, xh, inv, w)
        dx_ref[...] = dxo_ref[...] + dxn
        h_ref[...] = h.astype(bf16)

        @pl.when(pl.program_id(0) == 0)
        def _():
            dnw_ref[...] = jnp.zeros_like(dnw_ref)

        dnw_ref[...] += dw

    return pl.pallas_call(
        body, name="in_proj_bwd_" + "_".join(str(w.shape[1]) for w in ws), grid=(T // tm,),
        in_specs=[_rows(tm, D_MODEL), _resident((1, D_MODEL)), _rows(tm, D_MODEL)] + [_rows(tm, w.shape[1]) for w in ws]
        + [_resident(w.shape) for w in ws],
        out_specs=[_rows(tm, D_MODEL), _rows(tm, D_MODEL), pl.BlockSpec((1, D_MODEL), lambda i: (0, 0))],
        out_shape=[jax.ShapeDtypeStruct((T, D_MODEL), f32), jax.ShapeDtypeStruct((T, D_MODEL), bf16), jax.ShapeDtypeStruct((1, D_MODEL), f32)],
        compiler_params=_params(("arbitrary",)),
    )(x, nw, dxo, *dys, *ws)


def out_proj_fwd(x, a, w, tm=512):
    T = x.shape[0]
    K = a.shape[1]

    def body(x_ref, a_ref, w_ref, o_ref):
        o_ref[...] = x_ref[...] + _mm(a_ref[...], w_ref[...])

    return pl.pallas_call(
        body, name=f"out_proj_fwd_{K}", grid=(T // tm,),
        in_specs=[_rows(tm, D_MODEL), _rows(tm, K), _resident(w.shape)],
        out_specs=_rows(tm, D_MODEL), out_shape=jax.ShapeDtypeStruct((T, D_MODEL), f32),
        compiler_params=_params(("parallel",)),
    )(x, a, w)


def out_proj_bwd(dxo, w, tm=512):
    T = dxo.shape[0]
    K = w.shape[0]

    def body(dxo_ref, w_ref, da_ref, dy_ref):
        dyb = dxo_ref[...].astype(bf16)
        dy_ref[...] = dyb
        da_ref[...] = _mm_nt(dyb, w_ref[...]).astype(bf16)

    return pl.pallas_call(
        body, name=f"out_proj_bwd_{K}", grid=(T // tm,),
        in_specs=[_rows(tm, D_MODEL), _resident(w.shape)],
        out_specs=[_rows(tm, K), _rows(tm, D_MODEL)],
        out_shape=[jax.ShapeDtypeStruct((T, K), bf16), jax.ShapeDtypeStruct((T, D_MODEL), bf16)],
        compiler_params=_params(("parallel",)),
    )(dxo, w)


def _halo_spec(tm, width, n_tiles, reverse):
    per = tm // HALO

    def idx(i):
        t = (n_tiles - 1 - i) if reverse else i
        return (jnp.maximum(t * per - 1, 0), 0)

    return pl.BlockSpec((HALO, width), idx)


def _tile_spec(tm, width, n_tiles, reverse):
    if reverse:
        return pl.BlockSpec((tm, width), lambda i: (n_tiles - 1 - i, 0))
    return _rows(tm, width)


ROW_BLOCK = 64


def _strip(s):
    return pl.ds(pl.multiple_of(s * LANES, LANES), LANES)


def _conv_rows(ext_ref, w_ref, cols, k_w, r0):
    base = HALO - (k_w - 1) + r0
    wins = [ext_ref[pl.ds(base + k, ROW_BLOCK), :] for k in range(k_w)]
    out = w_ref[pl.ds(0, 1), cols] * wins[0]
    for k in range(1, k_w):
        out = out + w_ref[pl.ds(k, 1), cols] * wins[k]
    return out, wins


def _shifted_back(d_ref, w_ref, cols, k_w, r0):
    out = w_ref[pl.ds(0, 1), cols] * d_ref[pl.ds(r0 + k_w - 1, ROW_BLOCK), :]
    for k in range(1, k_w):
        out = out + w_ref[pl.ds(k, 1), cols] * d_ref[pl.ds(r0 + k_w - 1 - k, ROW_BLOCK), :]
    return out


def ssd_conv_fwd(xbc, conv_w, conv_b, dt_raw, dt_bias, tm=512):
    T = xbc.shape[0]
    nt = T // tm
    K = SSD_CONV_K

    def body(x_ref, halo_ref, w_ref, b_ref, dtr_ref, dtb_ref, act_ref, dt_ref, ext_ref):
        first = pl.program_id(0) == 0

        def strip(s, carry):
            cols = _strip(s)
            ext_ref[pl.ds(0, HALO), :] = jnp.where(first, 0.0, halo_ref[:, cols].astype(f32))
            ext_ref[pl.ds(HALO, tm), :] = x_ref[:, cols].astype(f32)
            for r0 in range(0, tm, ROW_BLOCK):
                pre, _ = _conv_rows(ext_ref, w_ref, cols, K, r0)
                pre = pre + b_ref[:, cols]
                act_ref[pl.ds(r0, ROW_BLOCK), cols] = (pre * _sigmoid(pre)).astype(bf16)
            return carry

        lax.fori_loop(0, SSD_CONV_DIM // LANES, strip, 0)
        dt = _softplus(dtr_ref[...] + dtb_ref[...])
        lane = lax.broadcasted_iota(jnp.int32, (1, LANES), 1)
        for g in range(SSD_GROUPS):
            dt_ref[g] = jnp.where(lane < 8, dt if g == 0 else pltpu.roll(dt, LANES - 8 * g, axis=1), 0.0)

    return pl.pallas_call(
        body, name="ssd_conv_fwd", grid=(nt,),
        in_specs=[_rows(tm, SSD_CONV_DIM), _halo_spec(tm, SSD_CONV_DIM, nt, False), _resident(conv_w.shape), _resident(conv_b.shape),
                  _rows(tm, LANES), _resident(dt_bias.shape)],
        out_specs=[_rows(tm, SSD_CONV_DIM), pl.BlockSpec((SSD_GROUPS, tm, LANES), lambda i: (0, i, 0))],
        out_shape=[jax.ShapeDtypeStruct((T, SSD_CONV_DIM), bf16), jax.ShapeDtypeStruct((SSD_GROUPS, T, LANES), f32)],
        scratch_shapes=[pltpu.VMEM((tm + HALO, LANES), f32)],
        compiler_params=_params(("parallel",)),
    )(xbc, xbc, conv_w, conv_b, dt_raw, dt_bias)


def ssd_conv_bwd(xbc, conv_w, conv_b, dt_raw, dt_bias, dxs_a, dxs_b, db, dc, ddt, tm=512):
    T = xbc.shape[0]
    nt = T // tm
    K = SSD_CONV_K

    def body(x_ref, halo_ref, w_ref, b_ref, dtr_ref, dtb_ref, da_ref, dbb_ref, db_ref, dc_ref, ddt_ref,
             dx_ref, ddtr_ref, dw_ref, dbias_ref, ddtb_ref, ext_ref, dpre_ref, carry_ref):
        i = pl.program_id(0)

        @pl.when(i == 0)
        def _():
            carry_ref[...] = jnp.zeros_like(carry_ref)
            dw_ref[...] = jnp.zeros_like(dw_ref)
            dbias_ref[...] = jnp.zeros_like(dbias_ref)
            ddtb_ref[...] = jnp.zeros_like(ddtb_ref)

        first_tile = i == nt - 1

        def run_strips(lo, hi, load_dact):
            def strip(s, carry):
                cols = _strip(s)
                ext_ref[pl.ds(0, HALO), :] = jnp.where(first_tile, 0.0, halo_ref[:, cols].astype(f32))
                ext_ref[pl.ds(HALO, tm), :] = x_ref[:, cols].astype(f32)
                dpre_ref[pl.ds(tm, 8), :] = carry_ref[:, cols]
                bias = b_ref[:, cols]
                dws = [jnp.zeros((1, LANES), f32) for _ in range(K)]
                dbs = jnp.zeros((1, LANES), f32)
                for r0 in range(0, tm, ROW_BLOCK):
                    pre, wins = _conv_rows(ext_ref, w_ref, cols, K, r0)
                    pre = pre + bias
                    sg = _sigmoid(pre)
                    dpre = load_dact(s, r0) * (sg * (1.0 + pre * (1.0 - sg)))
                    dpre_ref[pl.ds(r0, ROW_BLOCK), :] = dpre
                    dbs = dbs + jnp.sum(dpre, axis=0, keepdims=True)
                    for k in range(K):
                        dws[k] = dws[k] + jnp.sum(dpre * wins[k], axis=0, keepdims=True)
                carry_ref[:, cols] = dpre_ref[pl.ds(0, 8), :]
                for r0 in range(0, tm, ROW_BLOCK):
                    dx_ref[pl.ds(r0, ROW_BLOCK), cols] = _shifted_back(dpre_ref, w_ref, cols, K, r0).astype(bf16)
                for k in range(K):
                    dw_ref[pl.ds(k, 1), cols] += dws[k]
                dbias_ref[:, cols] += dbs
                return carry

            lax.fori_loop(lo, hi, strip, 0)

        rows = lambda r0: pl.ds(r0, ROW_BLOCK)
        n_x = SSD_INNER // LANES
        n_g = SSD_GROUPS * SSD_STATE // LANES
        run_strips(0, n_x, lambda s, r0: da_ref[rows(r0), _strip(s)].astype(f32) + dbb_ref[rows(r0), _strip(s)].astype(f32))
        run_strips(n_x, n_x + n_g, lambda s, r0: db_ref[rows(r0), _strip(s - n_x)].astype(f32))
        run_strips(n_x + n_g, n_x + 2 * n_g, lambda s, r0: dc_ref[rows(r0), _strip(s - n_x - n_g)].astype(f32))
        lane = lax.broadcasted_iota(jnp.int32, (1, LANES), 1)
        ddt = jnp.where(lane < 8, ddt_ref[0], 0.0)
        for g in range(1, SSD_GROUPS):
            ddt = ddt + pltpu.roll(jnp.where(lane < 8, ddt_ref[g], 0.0), 8 * g, axis=1)
        ddtr = ddt * _sigmoid(dtr_ref[...] + dtb_ref[...])
        ddtr_ref[...] = ddtr.astype(bf16)
        ddtb_ref[...] += jnp.sum(ddtr, axis=0, keepdims=True)

    rev = functools.partial(_tile_spec, tm, n_tiles=nt, reverse=True)
    const = lambda shape: pl.BlockSpec(shape, lambda i: (0, 0))
    return pl.pallas_call(
        body, name="ssd_conv_bwd", grid=(nt,),
        in_specs=[rev(width=SSD_CONV_DIM), _halo_spec(tm, SSD_CONV_DIM, nt, True), _resident(conv_w.shape), _resident(conv_b.shape),
                  rev(width=LANES), _resident(dt_bias.shape), rev(width=SSD_INNER), rev(width=SSD_INNER),
                  rev(width=SSD_GROUPS * SSD_STATE), rev(width=SSD_GROUPS * SSD_STATE),
                  pl.BlockSpec((SSD_GROUPS, tm, LANES), lambda i: (0, nt - 1 - i, 0))],
        out_specs=[rev(width=SSD_CONV_DIM), rev(width=LANES), const((8, SSD_CONV_DIM)), const((1, SSD_CONV_DIM)), const((1, LANES))],
        out_shape=[jax.ShapeDtypeStruct((T, SSD_CONV_DIM), bf16), jax.ShapeDtypeStruct((T, LANES), bf16),
                   jax.ShapeDtypeStruct((8, SSD_CONV_DIM), f32), jax.ShapeDtypeStruct((1, SSD_CONV_DIM), f32), jax.ShapeDtypeStruct((1, LANES), f32)],
        scratch_shapes=[pltpu.VMEM((tm + HALO, LANES), f32), pltpu.VMEM((tm + 8, LANES), f32), pltpu.VMEM((8, SSD_CONV_DIM), f32)],
        compiler_params=_params(("arbitrary",)),
    )(xbc, xbc, conv_w, conv_b, dt_raw, dt_bias, dxs_a, dxs_b, db, dc, ddt)


def _ssd_chunk(xs, bm, cm, dt, alog, st):
    L = SSD_CHUNK
    row = lax.broadcasted_iota(jnp.int32, (L, L), 0)
    col = lax.broadcasted_iota(jnp.int32, (L, L), 1)
    causal = row >= col
    tril = jnp.where(causal, 1.0, 0.0).astype(f32)
    lane = lax.broadcasted_iota(jnp.int32, (1, LANES), 1)
    sub = lax.broadcasted_iota(jnp.int32, (LANES, 1), 0)
    lo = lane < SSD_HEAD_DIM
    last_row = sub == L - 1

    dta = dt * (-jnp.exp(alog))
    a_cs = jnp.dot(tril, dta, precision=lax.Precision.HIGHEST, preferred_element_type=f32)
    a_cs_t = a_cs.T
    bmb = bm.astype(bf16)
    cmb = cm.astype(bf16)
    cb = _mm_nt(cmb, bmb)
    c_st = _mm(cmb, st.astype(bf16))

    def head_col(v, e):
        return jnp.sum(jnp.where(lane == e, v, 0.0), axis=1, keepdims=True)

    def head_row(v, e):
        return jnp.sum(jnp.where(sub == e, v, 0.0), axis=0, keepdims=True)

    ys, sts = [], []
    for j in range(4):
        e0, e1 = 2 * j, 2 * j + 1
        c0, c1 = head_col(a_cs, e0), head_col(a_cs, e1)
        acs_x = jnp.where(lo, c0, c1)
        dt_x = jnp.where(lo, head_col(dt, e0), head_col(dt, e1))
        xd = xs[:, j * LANES:(j + 1) * LANES] * dt_x
        m0 = cb * jnp.exp(jnp.where(causal, c0 - head_row(a_cs_t, e0), NEG_BIG))
        m1 = cb * jnp.exp(jnp.where(causal, c1 - head_row(a_cs_t, e1), NEG_BIG))
        mcat = jnp.concatenate([m0, m1], axis=1).astype(bf16)
        xcat = jnp.concatenate([jnp.where(lo, xd, 0.0), jnp.where(lo, 0.0, xd)], axis=0).astype(bf16)
        y_diag = _mm(mcat, xcat)
        a_last = jnp.sum(jnp.where(last_row, acs_x, 0.0), axis=0, keepdims=True)
        x_dec = (xd * jnp.exp(a_last - acs_x)).astype(bf16)
        s_new = _mm_tn(bmb, x_dec)
        y_off = c_st[:, j * LANES:(j + 1) * LANES] * jnp.exp(acs_x)
        ys.append(y_diag + y_off)
        sts.append(jnp.exp(a_last) * st[:, j * LANES:(j + 1) * LANES] + s_new)
    return jnp.concatenate(ys, axis=1), jnp.concatenate(sts, axis=1)


SCAN_GROUPS_FWD = 4
SCAN_GROUPS_BWD = 1


def _scan_specs(nc, reverse, gs):
    L = SSD_CHUNK
    ch = (lambda c: nc - 1 - c) if reverse else (lambda c: c)
    gw = SSD_INNER // SSD_GROUPS
    b0 = SSD_INNER // (gs * SSD_STATE)
    c0 = (SSD_INNER + SSD_GROUPS * SSD_STATE) // (gs * SSD_STATE)
    xs = pl.BlockSpec((L, gs * gw), lambda g, c: (ch(c), g))
    bm = pl.BlockSpec((L, gs * SSD_STATE), lambda g, c: (ch(c), b0 + g))
    cm = pl.BlockSpec((L, gs * SSD_STATE), lambda g, c: (ch(c), c0 + g))
    dt = pl.BlockSpec((gs, L, LANES), lambda g, c: (g, ch(c), 0))
    alog = pl.BlockSpec((gs, 1, LANES), lambda g, c: (g, 0, 0))
    st = pl.BlockSpec((gs, None, SSD_STATE, gw), lambda g, c: (g, ch(c), 0, 0))
    y = pl.BlockSpec((L, gs * gw), lambda g, c: (ch(c), g))
    grp = pl.BlockSpec((L, gs * SSD_STATE), lambda g, c: (ch(c), g))
    return xs, bm, cm, dt, alog, st, y, grp


def ssd_scan_fwd(act, dt4, alog4, rider=None):
    T = act.shape[0]
    nc = T // SSD_CHUNK
    gs = SCAN_GROUPS_FWD
    ng = SSD_GROUPS // gs
    gw = SSD_INNER // SSD_GROUPS
    xs_s, bm_s, cm_s, dt_s, alog_s, st_s, y_s, _ = _scan_specs(nc, False, gs)
    r_in, r_out, r_shapes, r_scratch, r_args = _rider_specs(rider)

    def body(xs_ref, bm_ref, cm_ref, dt_ref, alog_ref, y_ref, st_ref, st_scr):
        @pl.when(pl.program_id(1) == 0)
        def _():
            st_scr[...] = jnp.zeros_like(st_scr)

        for q in range(gs):
            xc, gc = pl.ds(q * gw, gw), pl.ds(q * SSD_STATE, SSD_STATE)
            st = st_scr[q]
            st_ref[q] = st
            y, st_new = _ssd_chunk(xs_ref[:, xc].astype(f32), bm_ref[:, gc].astype(f32), cm_ref[:, gc].astype(f32), dt_ref[q], alog_ref[q], st)
            y_ref[:, xc] = y.astype(bf16)
            st_scr[q] = st_new

    first = lambda: jnp.logical_and(pl.program_id(0) == 0, pl.program_id(1) == 0)
    last = lambda: jnp.logical_and(pl.program_id(0) == ng - 1, pl.program_id(1) == nc - 1)
    return pl.pallas_call(
        _carry(body, 5, 2, rider, first, last), name="ssd_scan_fwd" if rider is None else "ssd_scan_fwd_carrying", grid=(ng, nc),
        in_specs=[xs_s, bm_s, cm_s, dt_s, alog_s] + r_in, out_specs=[y_s, st_s] + r_out,
        out_shape=[jax.ShapeDtypeStruct((T, SSD_INNER), bf16), jax.ShapeDtypeStruct((SSD_GROUPS, nc, SSD_STATE, gw), f32)] + r_shapes,
        scratch_shapes=[pltpu.VMEM((gs, SSD_STATE, gw), f32)] + r_scratch,
        compiler_params=_params(("parallel" if rider is None else "arbitrary", "arbitrary")),
    )(act, act, act, dt4, alog4, *r_args)


def ssd_scan_bwd(act, dt4, alog4, states, dy, rider=None):
    T = act.shape[0]
    nc = T // SSD_CHUNK
    gs = SCAN_GROUPS_BWD
    ng = SSD_GROUPS // gs
    gw = SSD_INNER // SSD_GROUPS
    xs_s, bm_s, cm_s, dt_s, alog_s, st_s, y_s, grp_s = _scan_specs(nc, True, gs)
    r_in, r_out, r_shapes, r_scratch, r_args = _rider_specs(rider)

    def body(xs_ref, bm_ref, cm_ref, dt_ref, alog_ref, st_ref, dy_ref, dxs_ref, db_ref, dc_ref, ddt_ref, dalog_ref, dst_scr):
        @pl.when(pl.program_id(1) == 0)
        def _():
            dst_scr[...] = jnp.zeros_like(dst_scr)
            dalog_ref[...] = jnp.zeros_like(dalog_ref)

        for q in range(gs):
            xc, gc = pl.ds(q * gw, gw), pl.ds(q * SSD_STATE, SSD_STATE)
            _, vjp = jax.vjp(_ssd_chunk, xs_ref[:, xc].astype(f32), bm_ref[:, gc].astype(f32), cm_ref[:, gc].astype(f32),
                             dt_ref[q], alog_ref[q], st_ref[q])
            dxs, dbm, dcm, ddt, dalog, dst = vjp((dy_ref[:, xc].astype(f32), dst_scr[q]))
            dxs_ref[:, xc] = dxs.astype(bf16)
            db_ref[:, gc] = dbm.astype(bf16)
            dc_ref[:, gc] = dcm.astype(bf16)
            ddt_ref[q] = ddt
            dalog_ref[q] += dalog
            dst_scr[q] = dst

    first = lambda: jnp.logical_and(pl.program_id(0) == 0, pl.program_id(1) == 0)
    last = lambda: jnp.logical_and(pl.program_id(0) == ng - 1, pl.program_id(1) == nc - 1)
    return pl.pallas_call(
        _carry(body, 7, 5, rider, first, last), name="ssd_scan_bwd" if rider is None else "ssd_scan_bwd_carrying", grid=(ng, nc),
        in_specs=[xs_s, bm_s, cm_s, dt_s, alog_s, st_s, y_s] + r_in,
        out_specs=[y_s, grp_s, grp_s, dt_s, alog_s] + r_out,
        out_shape=[jax.ShapeDtypeStruct((T, SSD_INNER), bf16), jax.ShapeDtypeStruct((T, SSD_GROUPS * SSD_STATE), bf16),
                   jax.ShapeDtypeStruct((T, SSD_GROUPS * SSD_STATE), bf16), jax.ShapeDtypeStruct((SSD_GROUPS, T, LANES), f32),
                   jax.ShapeDtypeStruct((SSD_GROUPS, 1, LANES), f32)] + r_shapes,
        scratch_shapes=[pltpu.VMEM((gs, SSD_STATE, gw), f32)] + r_scratch,
        compiler_params=_params(("parallel" if rider is None else "arbitrary", "arbitrary")),
    )(act, act, act, dt4, alog4, states, dy, *r_args)


GATE_ROWS = 256


def _ssd_gate(y, xs, z, d_x, nw):
    g = (y + xs * d_x) * (z * _sigmoid(z))
    return g * lax.rsqrt(jnp.mean(g * g, axis=-1, keepdims=True) + RMS_EPS) * nw


def _gate_blocks(tm, fn):
    gw = SSD_INNER // SSD_GROUPS

    def block(r, carry):
        rows = pl.ds(r * GATE_ROWS if isinstance(r, int) else pl.multiple_of(r * GATE_ROWS, GATE_ROWS), GATE_ROWS)
        for k in range(SSD_GROUPS):
            fn(rows, pl.ds(k * gw, gw))
        return carry

    if tm == GATE_ROWS:
        block(0, 0)
    else:
        lax.fori_loop(0, tm // GATE_ROWS, block, 0)


def ssd_gate_fwd(y, act, z, d_x, nw, tm=512):
    T = y.shape[0]

    def body(y_ref, xs_ref, z_ref, d_ref, nw_ref, o_ref):
        def one(rows, cols):
            o_ref[rows, cols] = _ssd_gate(y_ref[rows, cols].astype(f32), xs_ref[rows, cols].astype(f32), z_ref[rows, cols].astype(f32),
                                          d_ref[:, cols], nw_ref[:, cols]).astype(bf16)

        _gate_blocks(tm, one)

    return pl.pallas_call(
        body, name="ssd_gate_fwd", grid=(T // tm,),
        in_specs=[_rows(tm, SSD_INNER), _rows(tm, SSD_INNER), _rows(tm, SSD_INNER), _resident(d_x.shape), _resident(nw.shape)],
        out_specs=_rows(tm, SSD_INNER), out_shape=jax.ShapeDtypeStruct((T, SSD_INNER), bf16),
        compiler_params=_params(("parallel",)),
    )(y, act, z, d_x, nw)


def ssd_gate_bwd(y, act, z, d_x, nw, dgn, tm=512):
    T = y.shape[0]

    def body(y_ref, xs_ref, z_ref, d_ref, nw_ref, dgn_ref, dy_ref, dxs_ref, dz_ref, dd_ref, dnw_ref):
        @pl.when(pl.program_id(0) == 0)
        def _():
            dd_ref[...] = jnp.zeros_like(dd_ref)
            dnw_ref[...] = jnp.zeros_like(dnw_ref)

        def one(rows, cols):
            _, vjp = jax.vjp(_ssd_gate, y_ref[rows, cols].astype(f32), xs_ref[rows, cols].astype(f32), z_ref[rows, cols].astype(f32),
                             d_ref[:, cols], nw_ref[:, cols])
            dy, dxs, dz, dd, dnw = vjp(dgn_ref[rows, cols].astype(f32))
            dy_ref[rows, cols] = dy.astype(bf16)
            dxs_ref[rows, cols] = dxs.astype(bf16)
            dz_ref[rows, cols] = dz.astype(bf16)
            dd_ref[:, cols] += dd
            dnw_ref[:, cols] += dnw

        _gate_blocks(tm, one)

    const = pl.BlockSpec((1, SSD_INNER), lambda i: (0, 0))
    return pl.pallas_call(
        body, name="ssd_gate_bwd", grid=(T // tm,),
        in_specs=[_rows(tm, SSD_INNER), _rows(tm, SSD_INNER), _rows(tm, SSD_INNER), _resident(d_x.shape), _resident(nw.shape), _rows(tm, SSD_INNER)],
        out_specs=[_rows(tm, SSD_INNER)] * 3 + [const, const],
        out_shape=[jax.ShapeDtypeStruct((T, SSD_INNER), bf16)] * 3 + [jax.ShapeDtypeStruct((1, SSD_INNER), f32)] * 2,
        compiler_params=_params(("arbitrary",)),
    )(y, act, z, d_x, nw, dgn)


def sc_mid_fwd(bcu, conv_w, tm=512):
    T = bcu.shape[0]
    nt = T // tm
    Dm = D_MODEL

    def body(x_ref, halo_ref, w_ref, q_ref, ext_ref):
        first = pl.program_id(0) == 0
        n_s = Dm // LANES

        def strip(s, carry):
            cols, c_cols, u_cols = _strip(s), _strip(s + n_s), _strip(s + 2 * n_s)
            ext_ref[pl.ds(0, HALO), :] = jnp.where(first, 0.0, halo_ref[:, c_cols].astype(f32) * halo_ref[:, u_cols].astype(f32))
            ext_ref[pl.ds(HALO, tm), :] = x_ref[:, c_cols].astype(f32) * x_ref[:, u_cols].astype(f32)
            for r0 in range(0, tm, ROW_BLOCK):
                rows = pl.ds(r0, ROW_BLOCK)
                v, _ = _conv_rows(ext_ref, w_ref, cols, SC_CONV_K, r0)
                q_ref[rows, cols] = (x_ref[rows, cols].astype(f32) * v).astype(bf16)
            return carry

        lax.fori_loop(0, n_s, strip, 0)

    return pl.pallas_call(
        body, name="sc_mid_fwd", grid=(nt,),
        in_specs=[_rows(tm, 3 * Dm), _halo_spec(tm, 3 * Dm, nt, False), _resident(conv_w.shape)],
        out_specs=_rows(tm, Dm), out_shape=jax.ShapeDtypeStruct((T, Dm), bf16),
        scratch_shapes=[pltpu.VMEM((tm + HALO, LANES), f32)],
        compiler_params=_params(("parallel",)),
    )(bcu, bcu, conv_w)


def sc_mid_bwd(bcu, conv_w, dq, tm=512):
    T = bcu.shape[0]
    nt = T // tm
    Dm = D_MODEL
    K = SC_CONV_K

    def body(x_ref, halo_ref, w_ref, dq_ref, dx_ref, dw_ref, ext_ref, dv_ref, carry_ref):
        i = pl.program_id(0)

        @pl.when(i == 0)
        def _():
            carry_ref[...] = jnp.zeros_like(carry_ref)
            dw_ref[...] = jnp.zeros_like(dw_ref)

        first_tile = i == nt - 1
        n_s = Dm // LANES

        def strip(s, carry):
            cols, c_cols, u_cols = _strip(s), _strip(s + n_s), _strip(s + 2 * n_s)
            ext_ref[pl.ds(0, HALO), :] = jnp.where(first_tile, 0.0, halo_ref[:, c_cols].astype(f32) * halo_ref[:, u_cols].astype(f32))
            ext_ref[pl.ds(HALO, tm), :] = x_ref[:, c_cols].astype(f32) * x_ref[:, u_cols].astype(f32)
            dv_ref[pl.ds(tm, 8), :] = carry_ref[:, cols]
            dws = [jnp.zeros((1, LANES), f32) for _ in range(K)]
            for r0 in range(0, tm, ROW_BLOCK):
                rows = pl.ds(r0, ROW_BLOCK)
                v, wins = _conv_rows(ext_ref, w_ref, cols, K, r0)
                dqv = dq_ref[rows, cols].astype(f32)
                dv = dqv * x_ref[rows, cols].astype(f32)
                dv_ref[rows, :] = dv
                dx_ref[rows, cols] = (dqv * v).astype(bf16)
                for k in range(K):
                    dws[k] = dws[k] + jnp.sum(dv * wins[k], axis=0, keepdims=True)
            carry_ref[:, cols] = dv_ref[pl.ds(0, 8), :]
            for r0 in range(0, tm, ROW_BLOCK):
                rows = pl.ds(r0, ROW_BLOCK)
                dp = _shifted_back(dv_ref, w_ref, cols, K, r0)
                dx_ref[rows, c_cols] = (dp * x_ref[rows, u_cols].astype(f32)).astype(bf16)
                dx_ref[rows, u_cols] = (dp * x_ref[rows, c_cols].astype(f32)).astype(bf16)
            for k in range(K):
                dw_ref[pl.ds(k, 1), cols] += dws[k]
            return carry

        lax.fori_loop(0, n_s, strip, 0)

    return pl.pallas_call(
        body, name="sc_mid_bwd", grid=(nt,),
        in_specs=[_tile_spec(tm, 3 * Dm, nt, True), _halo_spec(tm, 3 * Dm, nt, True), _resident(conv_w.shape), _tile_spec(tm, Dm, nt, True)],
        out_specs=[_tile_spec(tm, 3 * Dm, nt, True), pl.BlockSpec((8, Dm), lambda i: (0, 0))],
        out_shape=[jax.ShapeDtypeStruct((T, 3 * Dm), bf16), jax.ShapeDtypeStruct((8, Dm), f32)],
        scratch_shapes=[pltpu.VMEM((tm + HALO, LANES), f32), pltpu.VMEM((tm + 8, LANES), f32), pltpu.VMEM((8, Dm), f32)],
        compiler_params=_params(("arbitrary",)),
    )(bcu, bcu, conv_w, dq)


def loss_head(x, fw, target, tm=512):
    T = x.shape[0]

    def body(x_ref, fw_ref, t_ref, loss_ref, dx_ref, dfw_ref):
        @pl.when(pl.program_id(0) == 0)
        def _():
            loss_ref[...] = jnp.zeros_like(loss_ref)
            dfw_ref[...] = jnp.zeros_like(dfw_ref)

        w = fw_ref[...]
        y, xh, inv = _rms_fwd(x_ref[...], w)
        err = y - t_ref[...]
        loss_ref[...] += 0.5 * jnp.sum(jnp.mean(err * err, axis=-1, keepdims=True), axis=0, keepdims=True)
        dx, dw = _rms_bwd(err * (1.0 / D_MODEL), xh, inv, w)
        dx_ref[...] = dx
        dfw_ref[...] += dw

    return pl.pallas_call(
        body, name="loss_head", grid=(T // tm,),
        in_specs=[_rows(tm, D_MODEL), _resident((1, D_MODEL)), _rows(tm, D_MODEL)],
        out_specs=[pl.BlockSpec((1, LANES), lambda i: (0, 0)), _rows(tm, D_MODEL), pl.BlockSpec((1, D_MODEL), lambda i: (0, 0))],
        out_shape=[jax.ShapeDtypeStruct((1, LANES), f32), jax.ShapeDtypeStruct((T, D_MODEL), f32), jax.ShapeDtypeStruct((1, D_MODEL), f32)],
        compiler_params=_params(("arbitrary",)),
    )(x, fw, target)


def _row_tile(rows):
    return rows if rows <= 512 else 256


def adamw(g_parts, w, m, v, name="adamw", a0=0, prev=None):
    A, B, n = w.shape
    tb = _row_tile(B)
    n_parts = len(g_parts)
    arrays, specs = [], []
    for part in g_parts:
        lead, arr = part if isinstance(part, tuple) else ((), part)
        specs.append(pl.BlockSpec((None,) * (len(lead) + 1) + (tb, n), lambda a, t, lead=lead: tuple(lead) + (a, t, 0)))
        arrays.append(arr)
    na = arrays[0].shape[-3]
    prev = list(prev) if prev is not None else []

    def body(*refs):
        n = n_parts
        g_refs = refs[:n]
        w_ref, m_ref, v_ref = refs[n:n + 3]
        go_ref, d_ref, mo_ref, vo_ref = refs[n + 3 + len(prev):]
        g = g_refs[0][...].astype(f32)
        for r in g_refs[1:]:
            g = g + r[...].astype(f32)
        m_new = ADAM_B1 * m_ref[...] + (1.0 - ADAM_B1) * g
        v_new = ADAM_B2 * v_ref[...] + (1.0 - ADAM_B2) * (g * g)
        m_hat = m_new / (1.0 - ADAM_B1 ** ADAM_STEP)
        v_hat = v_new / (1.0 - ADAM_B2 ** ADAM_STEP)
        go_ref[...] = g
        d_ref[...] = -ADAM_LR * (m_hat / (jnp.sqrt(v_hat) + ADAM_EPS) + ADAM_WD * w_ref[...])
        mo_ref[...] = m_new
        vo_ref[...] = v_new

    plain = pl.BlockSpec((None, tb, n), lambda a, t: (a + a0, t, 0))
    return pl.pallas_call(
        body, name=name, grid=(na, B // tb), in_specs=specs + [plain] * 3 + [_ANY] * len(prev), out_specs=[plain] * 4,
        out_shape=[jax.ShapeDtypeStruct((A, B, n), f32)] * 4,
        input_output_aliases={n_parts + 3 + k: k for k in range(len(prev))},
        compiler_params=_params(("parallel", "parallel")),
    )(*arrays, w, m, v, *prev)


def pair_sum_bf16(ga, gb, name):
    _, A, B, n = gb.shape
    tb = _row_tile(B)

    def body(a_ref, b_ref, o_ref):
        o_ref[...] = (a_ref[...] + b_ref[...]).astype(bf16)

    return pl.pallas_call(
        body, name=name, grid=(3, A, B // tb),
        in_specs=[pl.BlockSpec((None, None, None, tb, n), lambda j, a, t: (0, j + 1, a, t, 0)),
                  pl.BlockSpec((None, None, tb, n), lambda j, a, t: (j + 1, a, t, 0))],
        out_specs=pl.BlockSpec((None, None, tb, n), lambda j, a, t: (j + 1, a, t, 0)),
        out_shape=jax.ShapeDtypeStruct((4, A, B, n), bf16),
        compiler_params=_params(("parallel", "parallel", "parallel")),
    )(ga, gb)


def assemble(gathered, axis, tk=256):
    _, A, K, n = gathered.shape
    if axis == 1:
        def body(w_ref, o_ref):
            o_ref[...] = jnp.concatenate([w_ref[j] for j in range(N_DEV)], axis=1)

        return pl.pallas_call(
            body, name=f"assemble_cols_{K}x{n}", grid=(A, K // tk),
            in_specs=[pl.BlockSpec((N_DEV, None, tk, n), lambda a, t: (0, a, t, 0))],
            out_specs=pl.BlockSpec((None, tk, N_DEV * n), lambda a, t: (a, t, 0)),
            out_shape=jax.ShapeDtypeStruct((A, K, N_DEV * n), gathered.dtype),
            compiler_params=_params(("parallel", "parallel")),
        )(gathered)

    def body(w_ref, o_ref):
        for j in range(N_DEV):
            o_ref[pl.ds(j * K, K), :] = w_ref[j]

    return pl.pallas_call(
        body, name=f"assemble_rows_{K}x{n}", grid=(A,),
        in_specs=[pl.BlockSpec((N_DEV, None, K, n), lambda a: (0, a, 0, 0))],
        out_specs=pl.BlockSpec((None, N_DEV * K, n), lambda a: (a, 0, 0)),
        out_shape=jax.ShapeDtypeStruct((A, N_DEV * K, n), gathered.dtype),
        compiler_params=_params(("parallel",)),
    )(gathered)


SSD_IN_PAD = 5248


def assemble_ssd_in(gathered, tk=256):
    _, A, K, n = gathered.shape

    def body(w_ref, z_ref, x_ref, dt_ref, full_ref):
        full_ref[:, pl.ds(SSD_IN_PAD - LANES, LANES)] = jnp.zeros((tk, LANES), gathered.dtype)
        for j in range(N_DEV):
            full_ref[:, pl.ds(j * n, n)] = w_ref[j]
        z_ref[...] = full_ref[:, pl.ds(0, SSD_INNER)]
        x_ref[...] = full_ref[:, pl.ds(SSD_INNER, SSD_CONV_DIM)]
        dt_ref[...] = full_ref[:, pl.ds(SSD_INNER + SSD_CONV_DIM, LANES)]

    widths = (SSD_INNER, SSD_CONV_DIM, LANES)
    return pl.pallas_call(
        body, name="assemble_ssd_in", grid=(A, K // tk),
        in_specs=[pl.BlockSpec((N_DEV, None, tk, n), lambda a, t: (0, a, t, 0))],
        out_specs=[pl.BlockSpec((None, tk, w), lambda a, t: (a, t, 0)) for w in widths],
        out_shape=[jax.ShapeDtypeStruct((A, K, w), gathered.dtype) for w in widths],
        scratch_shapes=[pltpu.VMEM((tk, SSD_IN_PAD), gathered.dtype)],
        compiler_params=_params(("parallel", "parallel")),
    )(gathered)


def ssd_in_to_shards(dwz, dwx, dwdt, buf, j, tk=256):
    K = dwz.shape[0]
    n = buf.shape[-1]
    fresh = isinstance(buf, jax.ShapeDtypeStruct)

    def body(z_ref, x_ref, dt_ref, *rest):
        o_ref, full_ref = rest[-2:]
        full_ref[:, pl.ds(0, SSD_INNER)] = z_ref[...]
        full_ref[:, pl.ds(SSD_INNER, SSD_CONV_DIM)] = x_ref[...]
        full_ref[:, pl.ds(SSD_INNER + SSD_CONV_DIM, LANES)] = dt_ref[...]
        my_c, my_chip = _my_core_and_chip()
        for d in range(N_DEV):
            o_ref[(d % 2) ^ my_c, (d // 2) ^ my_chip] = full_ref[:, pl.ds(d * n, n)]

    return pl.pallas_call(
        body, name="ssd_in_to_shards", grid=(K // tk,),
        in_specs=[_rows(tk, SSD_INNER), _rows(tk, SSD_CONV_DIM), _rows(tk, LANES)] + ([] if fresh else [_ANY]),
        out_specs=pl.BlockSpec((2, 4, None, tk, n), lambda t: (0, 0, j, t, 0)),
        out_shape=jax.ShapeDtypeStruct(buf.shape, f32),
        scratch_shapes=[pltpu.VMEM((tk, SSD_IN_PAD), f32)],
        input_output_aliases={} if fresh else {3: 0},
        compiler_params=_params(("parallel",)),
    )(dwz, dwx, dwdt, *([] if fresh else [buf]))


def sum_over_devices(gathered):
    _, R, W = gathered.shape

    def body(g_ref, o_ref):
        acc = g_ref[0]
        for k in range(1, N_DEV):
            acc = acc + g_ref[k]
        o_ref[...] = acc

    return pl.pallas_call(
        body, name="sum_over_devices", grid=(1,),
        in_specs=[pl.BlockSpec((N_DEV, R, W), lambda i: (0, 0, 0))], out_specs=pl.BlockSpec((R, W), lambda i: (0, 0)),
        out_shape=jax.ShapeDtypeStruct((R, W), f32), compiler_params=_params(("arbitrary",)),
    )(gathered)


_ANY = pl.BlockSpec(memory_space=pl.ANY)


class _Exchange:
    def __init__(self, inputs, out_shapes, scratch, start, finish):
        self.inputs, self.out_shapes, self.scratch, self.start, self.finish = inputs, out_shapes, scratch, start, finish

    def run(self, name):
        ni, no = len(self.inputs), len(self.out_shapes)

        def body(*refs):
            parts = (refs[:ni], refs[ni:ni + no], refs[ni + no:])
            self.start(*parts)
            self.finish(*parts)

        return pl.pallas_call(body, name=name, in_specs=[_ANY] * ni, out_specs=[_ANY] * no, out_shape=self.out_shapes,
                              scratch_shapes=self.scratch)(*self.inputs)


def _carry(body, n_in, n_out, rider, first, last):
    if rider is None:
        return body
    ri, ro = len(rider.inputs), len(rider.out_shapes)

    def hosted(*refs):
        a, b, c = n_in + ri, n_in + ri + n_out, n_in + ri + n_out + ro
        rs = len(refs) - c - len(rider.scratch)
        parts = (refs[n_in:a], refs[b:c], refs[c + rs:])

        @pl.when(first())
        def _():
            rider.start(*parts)

        body(*refs[:n_in], *refs[a:b], *refs[c:c + rs])

        @pl.when(last())
        def _():
            rider.finish(*parts)

    return hosted


def _rider_specs(rider):
    if rider is None:
        return [], [], [], [], []
    return [_ANY] * len(rider.inputs), [_ANY] * len(rider.out_shapes), list(rider.out_shapes), list(rider.scratch), list(rider.inputs)


def all_gather(blocks):
    n = len(blocks)

    def plan(x_refs, out_refs, sems):
        send_sems, recv_sems, local_sems = sems
        x, y, c = lax.axis_index("x"), lax.axis_index("y"), lax.axis_index("c")
        me, sibling = (x, y, c), (x, y, 1 - c)
        chips = [(1 - x, y), (x, 1 - y), (1 - x, 1 - y)]

        def copy(a, k, blk, to, src=None):
            px, py, pc = blk
            slot = out_refs[a].at[4 * px + 2 * py + pc]
            return pltpu.make_async_remote_copy(
                src_ref=slot if src is None else src, dst_ref=slot,
                send_sem=send_sems.at[7 * a + k], recv_sem=recv_sems.at[7 * a + k], device_id=to, device_id_type=MESH)

        mine = [pltpu.make_async_copy(x_refs[a], out_refs[a].at[4 * x + 2 * y + c], local_sems.at[a]) for a in range(n)]
        first = []
        for a in range(n):
            first += [copy(a, 0, me, sibling, src=x_refs[a])] + [copy(a, 1 + j, me, (*chip, c), src=x_refs[a]) for j, chip in enumerate(chips)]
        return c, me, sibling, chips, copy, mine, first

    def start(x_refs, out_refs, sems):
        _, _, _, _, _, mine, first = plan(x_refs, out_refs, sems)
        for cp in mine + first:
            cp.start()

    def finish(x_refs, out_refs, sems):
        c, me, sibling, chips, copy, mine, first = plan(x_refs, out_refs, sems)
        passed = []
        for j, chip in enumerate(chips):
            for a in range(n):
                copy(a, 1 + j, (*chip, c), me).wait_recv()
                passed.append(copy(a, 4 + j, (*chip, c), sibling))
                passed[-1].start()
        for a in range(n):
            copy(a, 0, sibling, me).wait_recv()
            for j, chip in enumerate(chips):
                copy(a, 4 + j, (*chip, 1 - c), me).wait_recv()
        for cp in first + passed:
            cp.wait_send()
        for cp in mine:
            cp.wait()

    return _Exchange(list(blocks), [jax.ShapeDtypeStruct((N_DEV,) + b.shape, b.dtype) for b in blocks],
                     [pltpu.SemaphoreType.DMA((7 * n,)), pltpu.SemaphoreType.DMA((7 * n,)), pltpu.SemaphoreType.DMA((n,))], start, finish)


def exchange_with_sibling(gs):
    n = len(gs)

    def plan(g_refs, recv_refs, sems):
        send_sems, recv_sems = sems
        x, y, c = lax.axis_index("x"), lax.axis_index("y"), lax.axis_index("c")
        return [pltpu.make_async_remote_copy(src_ref=g_refs[a].at[1], dst_ref=recv_refs[a], send_sem=send_sems.at[a],
                                             recv_sem=recv_sems.at[a], device_id=(x, y, 1 - c), device_id_type=MESH) for a in range(n)]

    def start(*refs):
        for cp in plan(*refs):
            cp.start()

    def finish(*refs):
        for cp in plan(*refs):
            cp.wait()

    return _Exchange(list(gs), [jax.ShapeDtypeStruct(g.shape[1:], g.dtype) for g in gs],
                     [pltpu.SemaphoreType.DMA((n,)), pltpu.SemaphoreType.DMA((n,))], start, finish)


def exchange_between_chips(parts):
    n = len(parts)

    def plan(p_refs, recv_refs, sems):
        send_sems, recv_sems = sems
        x, y, c = lax.axis_index("x"), lax.axis_index("y"), lax.axis_index("c")
        chips = [(2, (1 - x, y)), (1, (x, 1 - y)), (3, (1 - x, 1 - y))]
        return [pltpu.make_async_remote_copy(src_ref=p_refs[a].at[slot], dst_ref=recv_refs[a].at[k], send_sem=send_sems.at[3 * a + k],
                                             recv_sem=recv_sems.at[3 * a + k], device_id=(px, py, c), device_id_type=MESH)
                for a in range(n) for k, (slot, (px, py)) in enumerate(chips)]

    def start(*refs):
        for cp in plan(*refs):
            cp.start()

    def finish(*refs):
        for cp in plan(*refs):
            cp.wait()

    return _Exchange(list(parts), [jax.ShapeDtypeStruct((3,) + p.shape[1:], p.dtype) for p in parts],
                     [pltpu.SemaphoreType.DMA((3 * n,)), pltpu.SemaphoreType.DMA((3 * n,))], start, finish)


PARAMS = {
    "norm_w": ((DEPTH, 3, D_MODEL), 2),
    "ffn_w_gate": ((DEPTH, 2, D_MODEL, D_FF), 3),
    "ffn_w_up": ((DEPTH, 2, D_MODEL, D_FF), 3),
    "ffn_w_down": ((DEPTH, 2, D_FF, D_MODEL), 2),
    "ssd_w_in": ((2, D_MODEL, SSD_IN_DIM), 2),
    "ssd_conv_w": ((2, SSD_CONV_K, SSD_CONV_DIM), 2),
    "ssd_conv_b": ((2, SSD_CONV_DIM), None),
    "ssd_dt_bias": ((2, SSD_HEADS), None),
    "ssd_a_log": ((2, SSD_HEADS), None),
    "ssd_d": ((2, SSD_HEADS), None),
    "ssd_norm_w": ((2, SSD_INNER), None),
    "ssd_w_out": ((2, SSD_INNER, D_MODEL), 1),
    "sc_w_in": ((2, D_MODEL, 3 * D_MODEL), 2),
    "sc_conv_w": ((2, SC_CONV_K, D_MODEL), 2),
    "sc_w_out": ((2, D_MODEL, D_MODEL), 1),
    "final_norm_w": ((D_MODEL,), None),
}
NAMES = list(PARAMS)
BIG = ["ffn_w_gate", "ffn_w_up", "ffn_w_down", "ssd_w_in", "ssd_w_out", "sc_w_in", "sc_w_out"]
SMALL = [n for n in NAMES if n not in BIG]
SMALL_SHARDED = [n for n in SMALL if PARAMS[n][1] is not None]


def _round_up(n, m):
    return -(-n // m) * m


def _pack(flat_list, rows_multiple):
    flat = jnp.concatenate(flat_list)
    rows = _round_up(_round_up(flat.shape[0], PACK_W) // PACK_W, rows_multiple)
    return jnp.pad(flat, (0, rows * PACK_W - flat.shape[0])).reshape(rows, PACK_W)


def _unpack(packed, shapes, lead=()):
    flat = packed.reshape(lead + (-1,))
    out, off = [], 0
    for shp in shapes:
        n = 1
        for s in shp:
            n *= s
        out.append(flat[..., off:off + n].reshape(lead + tuple(shp)))
        off += n
    return out


def _local_shape(name):
    shp, ax = PARAMS[name]
    if ax is None:
        return shp
    return shp[:ax] + (shp[ax] // N_DEV,) + shp[ax + 1:]


def _full_from_gathered(g, name):
    shp, ax = PARAMS[name]
    return jnp.moveaxis(g, 0, ax).reshape(shp)


def _by_destination(full, name):
    shp, ax = PARAMS[name]
    loc = shp[ax] // N_DEV
    return jnp.moveaxis(full.reshape(shp[:ax] + (N_DEV, loc) + shp[ax + 1:]), ax, 0)


def _ssd_layer_fwd(xin, nw, p, rider=None):
    z, xbc, dt_raw = in_proj_fwd(xin, nw, [p["ssd_wz"], p["ssd_wx"], p["ssd_wdt"]], [bf16, bf16, f32])
    act, dt4 = ssd_conv_fwd(xbc, p["ssd_conv_w"], p["ssd_conv_b"], dt_raw, p["ssd_dt_bias"])
    y, states, *got = ssd_scan_fwd(act, dt4, p["ssd_alog4"], rider=rider)
    gn = ssd_gate_fwd(y, act, z, p["ssd_dx"], p["ssd_norm_w"])
    xout = out_proj_fwd(xin, gn, p["ssd_w_out"])
    return xout, (xin, z, xbc, dt_raw, act, dt4, y, states, gn), got


def _ssd_layer_bwd(dxo, nw, p, saved, gbuf, slab, rider=None):
    xin, z, xbc, dt_raw, act, dt4, y, states, gn = saved
    T = xin.shape[0]
    dgn, dyb = out_proj_bwd(dxo, p["ssd_w_out"])
    gbuf["ssd_w_out"] = tn_matmul_to_shards(gn, dyb, gbuf["ssd_w_out"], (slab,), 0)
    g = {}
    dy, dxs_skip, dz, dd_x, dgnw = ssd_gate_bwd(y, act, z, p["ssd_dx"], p["ssd_norm_w"], dgn)
    g["ssd_norm_w"] = dgnw[0]
    g["ssd_d"] = jnp.sum(dd_x.reshape(SSD_HEADS, SSD_HEAD_DIM), axis=1)
    dxs, db, dc, ddt4, dalog4, *got = ssd_scan_bwd(act, dt4, p["ssd_alog4"], states, dy, rider=rider)
    g["ssd_a_log"] = dalog4[:, 0, :8].reshape(SSD_HEADS)
    dxbc, ddt_raw, dcw, dcb, ddtb = ssd_conv_bwd(xbc, p["ssd_conv_w"], p["ssd_conv_b"], dt_raw, p["ssd_dt_bias"], dxs, dxs_skip, db, dc, ddt4)
    g["ssd_conv_w"] = dcw[:SSD_CONV_K]
    g["ssd_conv_b"] = dcb[0]
    g["ssd_dt_bias"] = ddtb[0, :SSD_HEADS]
    dx, h, dnw = in_proj_bwd(xin, nw, dxo, [dz, dxbc, ddt_raw], [p["ssd_wz"], p["ssd_wx"], p["ssd_wdt"]])
    gbuf["ssd_w_in"] = ssd_in_to_shards(tn_matmul(h, dz), tn_matmul(h, dxbc), tn_matmul(h, ddt_raw), gbuf["ssd_w_in"], slab)
    return dx, dnw, g, got


def _sc_layer_fwd(xin, nw, p):
    (bcu,) = in_proj_fwd(xin, nw, [p["sc_w_in"]], [bf16])
    q = sc_mid_fwd(bcu, p["sc_conv_w"])
    return out_proj_fwd(xin, q, p["sc_w_out"]), (xin, bcu, q)


def _sc_layer_bwd(dxo, nw, p, saved, gbuf, slab):
    xin, bcu, q = saved
    dq, dyb = out_proj_bwd(dxo, p["sc_w_out"])
    gbuf["sc_w_out"] = tn_matmul_to_shards(q, dyb, gbuf["sc_w_out"], (slab,), 0)
    dbcu, dcw = sc_mid_bwd(bcu, p["sc_conv_w"], dq)
    g = {"sc_conv_w": dcw[:SC_CONV_K]}
    dx, h, dnw = in_proj_bwd(xin, nw, dxo, [dbcu], [p["sc_w_in"]])
    gbuf["sc_w_in"] = tn_matmul_to_shards(h, dbcu, gbuf["sc_w_in"], (slab,), 1)
    return dx, dnw, g


def kernel(x, norm_w, ffn_w_gate, ffn_w_up, ffn_w_down, ssd_w_in, ssd_conv_w, ssd_conv_b, ssd_dt_bias, ssd_a_log, ssd_d, ssd_norm_w, ssd_w_out, sc_w_in, sc_conv_w, sc_w_out, final_norm_w, loss_target, m_norm_w, m_ffn_w_gate, m_ffn_w_up, m_ffn_w_down, m_ssd_w_in, m_ssd_conv_w, m_ssd_conv_b, m_ssd_dt_bias, m_ssd_a_log, m_ssd_d, m_ssd_norm_w, m_ssd_w_out, m_sc_w_in, m_sc_conv_w, m_sc_w_out, m_final_norm_w, v_norm_w, v_ffn_w_gate, v_ffn_w_up, v_ffn_w_down, v_ssd_w_in, v_ssd_conv_w, v_ssd_conv_b, v_ssd_dt_bias, v_ssd_a_log, v_ssd_d, v_ssd_norm_w, v_ssd_w_out, v_sc_w_in, v_sc_conv_w, v_sc_w_out, v_final_norm_w):
    w_loc = dict(zip(NAMES, (norm_w, ffn_w_gate, ffn_w_up, ffn_w_down, ssd_w_in, ssd_conv_w, ssd_conv_b, ssd_dt_bias, ssd_a_log, ssd_d, ssd_norm_w, ssd_w_out, sc_w_in, sc_conv_w, sc_w_out, final_norm_w)))
    m_loc = dict(zip(NAMES, (m_norm_w, m_ffn_w_gate, m_ffn_w_up, m_ffn_w_down, m_ssd_w_in, m_ssd_conv_w, m_ssd_conv_b, m_ssd_dt_bias, m_ssd_a_log, m_ssd_d, m_ssd_norm_w, m_ssd_w_out, m_sc_w_in, m_sc_conv_w, m_sc_w_out, m_final_norm_w)))
    v_loc = dict(zip(NAMES, (v_norm_w, v_ffn_w_gate, v_ffn_w_up, v_ffn_w_down, v_ssd_w_in, v_ssd_conv_w, v_ssd_conv_b, v_ssd_dt_bias, v_ssd_a_log, v_ssd_d, v_ssd_norm_w, v_ssd_w_out, v_sc_w_in, v_sc_conv_w, v_sc_w_out, v_final_norm_w)))
    ax, ay, ac = lax.axis_index("x"), lax.axis_index("y"), lax.axis_index("c")
    my_chip = 2 * ax + ay
    my_dev = 4 * ax + 2 * ay + ac
    T = x.shape[1]

    def as3d(a):
        return a.reshape((-1,) + a.shape[-2:])

    wb = {n: as3d(w_loc[n]).astype(bf16) for n in BIG}

    FFN = ["ffn_w_gate", "ffn_w_up", "ffn_w_down"]

    def mixer_names(i):
        return ["ssd_w_in", "ssd_w_out"] if i % 2 == 0 else ["sc_w_in", "sc_w_out"]

    ag_sets = [[(n, 0, 1) for n in FFN], [(n, 1, 1) for n in FFN] + [(n, 0, 1) for n in mixer_names(0)]]
    ag_sets += [[(n, 2 * r, 2) for n in FFN] + [(n, r // 2, 1) for n in mixer_names(r)] for r in (1, 2, 3)]

    def set_blocks(spec):
        return [wb[n][a0:a0 + na] for n, a0, na in spec]

    def set_weights(spec, gathered):
        q = {}
        for (n, _, _), g in zip(spec, gathered):
            if n == "ssd_w_in":
                q["ssd_wz"], q["ssd_wx"], q["ssd_wdt"] = assemble_ssd_in(g)
            else:
                q[n] = assemble(g, 1 if PARAMS[n][1] == len(PARAMS[n][0]) - 1 else 0)
        return q

    ss_shapes = [_local_shape(n) for n in SMALL_SHARDED]
    gathered0 = all_gather(set_blocks(ag_sets[0]) + [_pack([w_loc[n].reshape(-1) for n in SMALL_SHARDED], 8)]).run("all_gather_first")
    full = {}
    for n, part in zip(SMALL_SHARDED, _unpack(gathered0[-1], ss_shapes, lead=(N_DEV,))):
        full[n] = _full_from_gathered(part, n)
    for n in SMALL:
        if PARAMS[n][1] is None:
            full[n] = w_loc[n]
    small = {
        "ssd_conv_w": full["ssd_conv_w"],
        "ssd_conv_b": full["ssd_conv_b"].reshape(2, 1, SSD_CONV_DIM),
        "ssd_dt_bias": jnp.pad(full["ssd_dt_bias"], ((0, 0), (0, LANES - SSD_HEADS))).reshape(2, 1, LANES),
        "ssd_alog4": jnp.pad(full["ssd_a_log"].reshape(2, SSD_GROUPS, 1, 8), ((0, 0), (0, 0), (0, 0), (0, LANES - 8))),
        "ssd_dx": jnp.repeat(full["ssd_d"], SSD_HEAD_DIM, axis=1).reshape(2, 1, SSD_INNER),
        "ssd_norm_w": full["ssd_norm_w"].reshape(2, 1, SSD_INNER),
        "sc_conv_w": full["sc_conv_w"],
    }
    nw_all = full["norm_w"].reshape(DEPTH, 3, 1, D_MODEL)

    ffn_w = [[None, None] for _ in range(DEPTH)]
    mix_w = [None] * DEPTH

    def arrived(s, gathered):
        q = set_weights(ag_sets[s], gathered)
        ffn = tuple(q[n] for n in FFN)
        if s == 0:
            ffn_w[0][0] = ffn + ((0,),)
            return
        i = 0 if s == 1 else s - 1
        if s == 1:
            ffn_w[0][1] = ffn + ((0,),)
        else:
            ffn_w[i] = [ffn + ((0,),), ffn + ((1,),)]
        m = {n: v[0] for n, v in q.items() if n not in FFN}
        m.update({n: v[i // 2] for n, v in small.items() if n.startswith("ssd" if i % 2 == 0 else "sc")})
        mix_w[i] = m

    def rider_for(s):
        return all_gather(set_blocks(ag_sets[s]))

    xc = x[0]
    saved = []
    arrived(0, gathered0[:-1])
    for i in range(DEPTH):
        carried = {0: (1, 2, 3), 1: (4, None, None)}.get(i, (None, None, None))
        wg, wu, wd, idx = ffn_w[i][0]
        x1, g1, u1, a1, *got = ffn_fwd(xc, nw_all[i, 0], wg, wu, wd, idx, rider=rider_for(carried[0]) if carried[0] else None)
        if carried[0]:
            arrived(carried[0], got)
        if i % 2 == 0:
            x2, mix_saved, got = _ssd_layer_fwd(x1, nw_all[i, 1], mix_w[i], rider=rider_for(carried[1]) if carried[1] else None)
            if carried[1]:
                arrived(carried[1], got)
        else:
            x2, mix_saved = _sc_layer_fwd(x1, nw_all[i, 1], mix_w[i])
        wg, wu, wd, idx = ffn_w[i][1]
        x3, g3, u3, a3, *got = ffn_fwd(x2, nw_all[i, 2], wg, wu, wd, idx, rider=rider_for(carried[2]) if carried[2] else None)
        if carried[2]:
            arrived(carried[2], got)
        saved.append(((xc, g1, u1, a1), mix_saved, (x2, g3, u3, a3)))
        xc = x3

    loss_row, dx, dfw = loss_head(xc, full["final_norm_w"].reshape(1, D_MODEL), loss_target[0])
    loss = lax.psum(loss_row[0, 0], ("x", "y", "c"))

    grads = {n: [None] * PARAMS[n][0][0] for n in SMALL if n != "final_norm_w"}
    grads["final_norm_w"] = dfw[0]
    dnorm = [[None] * 3 for _ in range(DEPTH)]
    def slabs(n, which):
        if n.startswith("ffn"):
            return {"early": (2, 6), "mid": (1, 1), "last": (0, 1)}[which]
        if n.startswith("ssd"):
            return {"early": (1, 1), "mid": (0, 1), "last": (0, 0)}[which]
        return {"early": (0, 2), "mid": (0, 0), "last": (0, 0)}[which]

    gb = {which: {n: jax.ShapeDtypeStruct((2, 4, slabs(n, which)[1]) + wb[n].shape[1:], f32) for n in BIG if slabs(n, which)[1]}
          for which in ("early", "mid", "last")}

    def ffn_back(i, k, dxo, sv, rider=None):
        xin, g_, u_, a_ = sv
        which = "early" if i > 0 else ("mid" if k == 1 else "last")
        gbuf = gb[which]
        slab = 2 * i + k - slabs("ffn_w_gate", which)[0]
        wg, wu, wd, idx = ffn_w[i][k]
        dxi, h, dyb, dg, du, dnw, *got = ffn_bwd_dx(xin, dxo, g_, u_, nw_all[i, 2 * k], wg, wu, wd, idx, rider=rider)
        dnorm[i][2 * k] = dnw[0]
        gbuf["ffn_w_gate"] = tn_matmul_to_shards(h, dg, gbuf["ffn_w_gate"], (slab,), 1)
        gbuf["ffn_w_up"] = tn_matmul_to_shards(h, du, gbuf["ffn_w_up"], (slab,), 1)
        gbuf["ffn_w_down"] = tn_matmul_to_shards(a_, dyb, gbuf["ffn_w_down"], (slab,), 0)
        return dxi, got

    def reduce_in_chip(gbuf, from_sibling=None):
        names = list(gbuf)
        bufs = [gbuf[n] for n in names]
        if from_sibling is None:
            from_sibling = exchange_with_sibling(bufs).run("exchange_with_sibling")
        return names, bufs, from_sibling, [pair_sum_bf16(g, fs, "pair_sum_" + n) for n, g, fs in zip(names, bufs, from_sibling)]

    reduced, from_chips = {}, {}
    for i in reversed(range(DEPTH)):
        j = i // 2
        sv_a, sv_mix, sv_b = saved[i]
        if i == 0:
            dx, got = ffn_back(i, 1, dx, sv_b, rider=exchange_with_sibling(list(gb["early"].values())))
            reduced["early"] = reduce_in_chip(gb["early"], from_sibling=got)
        else:
            dx, _ = ffn_back(i, 1, dx, sv_b)
        if i % 2 == 0:
            rider = exchange_between_chips(reduced["early"][3]) if i == 0 else None
            dx, dnw, gm, got = _ssd_layer_bwd(dx, nw_all[i, 1], mix_w[i], sv_mix, gb["mid" if i == 0 else "early"], 0, rider=rider)
            if i == 0:
                from_chips["early"] = got
                reduced["mid"] = reduce_in_chip(gb["mid"])
        else:
            dx, dnw, gm = _sc_layer_bwd(dx, nw_all[i, 1], mix_w[i], sv_mix, gb["early"], j)
        dnorm[i][1] = dnw[0]
        for n, val in gm.items():
            grads[n][j] = val
        dx, got = ffn_back(i, 0, dx, sv_a, rider=exchange_between_chips(reduced["mid"][3]) if i == 0 else None)
        if i == 0:
            from_chips["mid"] = got

    grads["norm_w"] = jnp.stack([jnp.stack(r) for r in dnorm])
    for n in SMALL:
        if isinstance(grads[n], list):
            grads[n] = jnp.stack(grads[n])

    reduced["last"] = reduce_in_chip(gb["last"])
    from_chips["last"] = exchange_between_chips(reduced["last"][3]).run("exchange_between_chips")
    results = [{}, {}, {}, {}]
    outs = {}
    for which in ("last", "mid", "early"):
        names, bufs, from_sibling, _ = reduced[which]
        for n, g, fs, fc in zip(names, bufs, from_sibling, from_chips[which]):
            parts = [((0, 0), g), ((0,), fs), ((0,), fc), ((1,), fc), ((2,), fc)]
            outs[n] = adamw(parts, as3d(w_loc[n]), as3d(m_loc[n]), as3d(v_loc[n]), name="adamw_" + n + "_" + which,
                            a0=slabs(n, which)[0], prev=outs.get(n))
    for n in BIG:
        for k in range(4):
            results[k][n] = outs[n][k].reshape(_local_shape(n))

    g_small = _pack([grads[n].reshape(-1) for n in SMALL], 8)
    g_small = sum_over_devices(all_gather([g_small]).run("all_gather_small_grads")[0])
    g_small_full = dict(zip(SMALL, _unpack(g_small, [PARAMS[n][0] for n in SMALL])))
    g_small_loc = []
    for n in SMALL:
        if PARAMS[n][1] is None:
            g_small_loc.append(g_small_full[n])
        else:
            g_small_loc.append(lax.dynamic_index_in_dim(_by_destination(g_small_full[n], n), my_dev, axis=0, keepdims=False))
    small_shapes = [_local_shape(n) for n in SMALL]
    pack_small = lambda d: _pack([d[n].reshape(-1) for n in SMALL], 8)[None]
    small_out = adamw([_pack([gl.reshape(-1) for gl in g_small_loc], 8)[None]], pack_small(w_loc), pack_small(m_loc), pack_small(v_loc), name="adamw_small")
    for k in range(4):
        results[k].update(zip(SMALL, _unpack(small_out[k], small_shapes)))
    return (loss, dx[None], *[results[0][n] for n in NAMES], *[results[1][n] for n in NAMES],
            *[results[2][n] for n in NAMES], *[results[3][n] for n in NAMES])
```

```python
import functools

import jax
import jax.numpy as jnp
from jax import lax
from jax.experimental import pallas as pl
from jax.experimental.pallas import tpu as pltpu

f32 = jnp.float32
bf16 = jnp.bfloat16

D_MODEL = 1024
D_FF = 2816
DEPTH = 4
SSD_INNER = 2048
SSD_HEADS = 32
SSD_HEAD_DIM = 64
SSD_GROUPS = 4
SSD_STATE = 128
SSD_CONV_K = 4
SSD_CONV_DIM = 3072
SSD_IN_DIM = 5152
SSD_CHUNK = 128
SC_CONV_K = 3
RMS_EPS = 1e-5
N_DEV = 8
LANES = 128
HALO = 16
PACK_W = 1024
PACK_TILE = 256
VMEM_LIMIT = 56 * 1024 * 1024
NEG_BIG = -1e30

ADAM_LR = 0.001
ADAM_B1 = 0.9
ADAM_B2 = 0.999
ADAM_EPS = 1e-08
ADAM_WD = 0.01
ADAM_STEP = 10

NT_DIMS = (((1,), (1,)), ((), ()))
TN_DIMS = (((0,), (0,)), ((), ()))
MESH = pl.DeviceIdType.MESH


def _params(sem=None):
    return pltpu.CompilerParams(dimension_semantics=sem, vmem_limit_bytes=VMEM_LIMIT)


def _resident(shape):
    nd = len(shape)
    return pl.BlockSpec(tuple(shape), lambda *_: (0,) * nd, pipeline_mode=pl.Buffered(1))


def _rows(tm, width):
    return pl.BlockSpec((tm, width), lambda i: (i, 0))


def _my_core_and_chip():
    return lax.axis_index("c"), 2 * lax.axis_index("x") + lax.axis_index("y")


def _sigmoid(v):
    return 0.5 * jnp.tanh(0.5 * v) + 0.5


def _softplus(v):
    return jnp.maximum(v, 0.0) + jnp.log(1.0 + jnp.exp(-jnp.abs(v)))


def _rms_fwd(xv, w):
    inv = lax.rsqrt(jnp.mean(xv * xv, axis=-1, keepdims=True) + RMS_EPS)
    xh = xv * inv
    return xh * w, xh, inv


def _rms_bwd(dh, xh, inv, w):
    dxh = dh * w
    dx = inv * (dxh - xh * jnp.mean(dxh * xh, axis=-1, keepdims=True))
    return dx, jnp.sum(dh * xh, axis=0, keepdims=True)


def _mm(a, b):
    return jnp.dot(a, b, preferred_element_type=f32)


def _mm_nt(a, b):
    return lax.dot_general(a, b, NT_DIMS, preferred_element_type=f32)


def _mm_tn(a, b):
    return lax.dot_general(a, b, TN_DIMS, preferred_element_type=f32)


FFN_CHUNK = D_FF


def _ffn_chunks():
    return [(c0, min(FFN_CHUNK, D_FF - c0)) for c0 in range(0, D_FF, FFN_CHUNK)]


def _layer_slab(w, idx):
    tail = w.shape[len(idx):]
    return pl.BlockSpec((None,) * len(idx) + tuple(tail), lambda *_: tuple(idx) + (0,) * len(tail), pipeline_mode=pl.Buffered(1))


def ffn_fwd(x, nw, wg, wu, wd, idx, tm=512, rider=None):
    T = x.shape[0]
    nt = T // tm
    r_in, r_out, r_shapes, r_scratch, r_args = _rider_specs(rider)

    def body(x_ref, nw_ref, wg_ref, wu_ref, wd_ref, xo_ref, g_ref, u_ref, a_ref):
        xv = x_ref[...]
        h, _, _ = _rms_fwd(xv, nw_ref[...])
        hb = h.astype(bf16)
        y = None
        for c0, fc in _ffn_chunks():
            cols = pl.ds(c0, fc)
            g = _mm(hb, wg_ref[:, cols])
            u = _mm(hb, wu_ref[:, cols])
            ab = (g * _sigmoid(g) * u).astype(bf16)
            g_ref[:, cols] = g.astype(bf16)
            u_ref[:, cols] = u.astype(bf16)
            a_ref[:, cols] = ab
            part = _mm(ab, wd_ref[cols, :])
            y = part if y is None else y + part
        xo_ref[...] = xv + 0.5 * y

    hosted = _carry(body, 5, 4, rider, lambda: pl.program_id(0) == 0, lambda: pl.program_id(0) == nt - 1)
    return pl.pallas_call(
        hosted, name="ffn_fwd" if rider is None else "ffn_fwd_carrying", grid=(nt,),
        in_specs=[_rows(tm, D_MODEL), _resident((1, D_MODEL)), _layer_slab(wg, idx), _layer_slab(wu, idx), _layer_slab(wd, idx)] + r_in,
        out_specs=[_rows(tm, D_MODEL), _rows(tm, D_FF), _rows(tm, D_FF), _rows(tm, D_FF)] + r_out,
        out_shape=[jax.ShapeDtypeStruct((T, D_MODEL), f32)] + [jax.ShapeDtypeStruct((T, D_FF), bf16)] * 3 + r_shapes,
        scratch_shapes=r_scratch,
        compiler_params=_params(("parallel",) if rider is None else ("arbitrary",)),
    )(x, nw, wg, wu, wd, *r_args)


def ffn_bwd_dx(x, dxo, g, u, nw, wg, wu, wd, idx, tm=256, rider=None):
    T = x.shape[0]
    nt = T // tm
    r_in, r_out, r_shapes, r_scratch, r_args = _rider_specs(rider)

    def body(x_ref, dxo_ref, g_ref, u_ref, nw_ref, wg_ref, wu_ref, wd_ref, dx_ref, h_ref, dy_ref, dg_ref, du_ref, dnw_ref):
        w = nw_ref[...]
        h, xh, inv = _rms_fwd(x_ref[...], w)
        dxo_v = dxo_ref[...]
        dyb = (0.5 * dxo_v).astype(bf16)
        dh = None
        for c0, fc in _ffn_chunks():
            cols = pl.ds(c0, fc)
            da = _mm_nt(dyb, wd_ref[cols, :])
            gv = g_ref[:, cols].astype(f32)
            uv = u_ref[:, cols].astype(f32)
            s = _sigmoid(gv)
            dgb = (da * uv * (s * (1.0 + gv * (1.0 - s)))).astype(bf16)
            dub = (da * (gv * s)).astype(bf16)
            dg_ref[:, cols] = dgb
            du_ref[:, cols] = dub
            part = _mm_nt(dgb, wg_ref[:, cols]) + _mm_nt(dub, wu_ref[:, cols])
            dh = part if dh is None else dh + part
        dxn, dw = _rms_bwd(dh, xh, inv, w)
        dx_ref[...] = dxo_v + dxn
        h_ref[...] = h.astype(bf16)
        dy_ref[...] = dyb

        @pl.when(pl.program_id(0) == 0)
        def _():
            dnw_ref[...] = jnp.zeros_like(dnw_ref)

        dnw_ref[...] += dw

    hosted = _carry(body, 8, 6, rider, lambda: pl.program_id(0) == 0, lambda: pl.program_id(0) == nt - 1)
    return pl.pallas_call(
        hosted, name="ffn_bwd_dx" if rider is None else "ffn_bwd_dx_carrying", grid=(nt,),
        in_specs=[_rows(tm, D_MODEL), _rows(tm, D_MODEL), _rows(tm, D_FF), _rows(tm, D_FF), _resident((1, D_MODEL)),
                  _layer_slab(wg, idx), _layer_slab(wu, idx), _layer_slab(wd, idx)] + r_in,
        out_specs=[_rows(tm, D_MODEL), _rows(tm, D_MODEL), _rows(tm, D_MODEL), _rows(tm, D_FF), _rows(tm, D_FF),
                   pl.BlockSpec((1, D_MODEL), lambda i: (0, 0))] + r_out,
        out_shape=[jax.ShapeDtypeStruct((T, D_MODEL), f32), jax.ShapeDtypeStruct((T, D_MODEL), bf16), jax.ShapeDtypeStruct((T, D_MODEL), bf16),
                   jax.ShapeDtypeStruct((T, D_FF), bf16), jax.ShapeDtypeStruct((T, D_FF), bf16), jax.ShapeDtypeStruct((1, D_MODEL), f32)] + r_shapes,
        scratch_shapes=r_scratch,
        compiler_params=_params(("arbitrary",)),
    )(x, dxo, g, u, nw, wg, wu, wd, *r_args)


def tn_matmul(a, b, tk=1024):
    T, M = a.shape
    N = b.shape[1]
    bn = N if M * N <= 3_200_000 else N // 2
    nk = T // tk

    def body(a_ref, b_ref, o_ref):
        @pl.when(pl.program_id(1) == 0)
        def _():
            o_ref[...] = jnp.zeros_like(o_ref)

        o_ref[...] += _mm_tn(a_ref[...], b_ref[...])

    return pl.pallas_call(
        body, name=f"tn_matmul_{M}x{N}", grid=(N // bn, nk),
        in_specs=[pl.BlockSpec((tk, M), lambda j, k: (k, 0)), pl.BlockSpec((tk, bn), lambda j, k: (k, j))],
        out_specs=pl.BlockSpec((M, bn), lambda j, k: (0, j)),
        out_shape=jax.ShapeDtypeStruct((M, N), f32),
        compiler_params=_params(("parallel", "arbitrary")),
    )(a, b)


def tn_matmul_to_shards(a, b, buf, idx, axis):
    T, M = a.shape
    N = b.shape[1]
    m, n = buf.shape[-2:]
    tk = 1024 if M * N <= 2_200_000 else 512
    nk = T // tk
    fresh = isinstance(buf, jax.ShapeDtypeStruct)

    def body(a_ref, b_ref, *rest):
        o_ref, acc_ref = rest[-2:]
        k = pl.program_id(0)

        @pl.when(k == 0)
        def _():
            acc_ref[...] = jnp.zeros_like(acc_ref)

        acc_ref[...] += _mm_tn(a_ref[...], b_ref[...])

        @pl.when(k == nk - 1)
        def _():
            my_c, my_chip = _my_core_and_chip()
            for d in range(N_DEV):
                piece = acc_ref[:, pl.ds(d * n, n)] if axis == 1 else acc_ref[pl.ds(d * m, m), :]
                o_ref[(d % 2) ^ my_c, (d // 2) ^ my_chip] = piece

    none = (None,) * len(idx)
    return pl.pallas_call(
        body, name=f"tn_matmul_to_shards_{M}x{N}_{axis}", grid=(nk,),
        in_specs=[pl.BlockSpec((tk, M), lambda k: (k, 0)), pl.BlockSpec((tk, N), lambda k: (k, 0))] + ([] if fresh else [_ANY]),
        out_specs=pl.BlockSpec((2, 4) + none + (m, n), lambda k: (0, 0) + tuple(idx) + (0, 0)),
        out_shape=jax.ShapeDtypeStruct(buf.shape, f32),
        scratch_shapes=[pltpu.VMEM((M, N), f32)],
        input_output_aliases={} if fresh else {2: 0},
        compiler_params=_params(("arbitrary",)),
    )(a, b, *([] if fresh else [buf]))


def in_proj_fwd(x, nw, ws, out_dtypes, tm=512):
    T = x.shape[0]
    n = len(ws)

    def body(*refs):
        x_ref, nw_ref = refs[:2]
        w_refs = refs[2:2 + n]
        o_refs = refs[2 + n:]
        h, _, _ = _rms_fwd(x_ref[...], nw_ref[...])
        hb = h.astype(bf16)
        for w_ref, o_ref in zip(w_refs, o_refs):
            o_ref[...] = _mm(hb, w_ref[...]).astype(o_ref.dtype)

    return pl.pallas_call(
        body, name="in_proj_fwd_" + "_".join(str(w.shape[1]) for w in ws), grid=(T // tm,),
        in_specs=[_rows(tm, D_MODEL), _resident((1, D_MODEL))] + [_resident(w.shape) for w in ws],
        out_specs=[_rows(tm, w.shape[1]) for w in ws],
        out_shape=[jax.ShapeDtypeStruct((T, w.shape[1]), dt) for w, dt in zip(ws, out_dtypes)],
        compiler_params=_params(("parallel",)),
    )(x, nw, *ws)


def in_proj_bwd(x, nw, dxo, dys, ws, tm=512):
    T = x.shape[0]
    n = len(ws)

    def body(*refs):
        x_ref, nw_ref, dxo_ref = refs[:3]
        dy_refs = refs[3:3 + n]
        w_refs = refs[3 + n:3 + 2 * n]
        dx_ref, h_ref, dnw_ref = refs[3 + 2 * n:]
        w = nw_ref[...]
        h, xh, inv = _rms_fwd(x_ref[...], w)
        dh = _mm_nt(dy_refs[0][...], w_refs[0][...])
        for dy_ref, w_ref in zip(dy_refs[1:], w_refs[1:]):
            dh = dh + _mm_nt(dy_ref[...], w_ref[...])
        dxn, dw = _rms_bwd(dh, xh, inv, w)
        dx_ref[...] = dxo_ref[...] + dxn
        h_ref[...] = h.astype(bf16)

        @pl.when(pl.program_id(0) == 0)
        def _():
            dnw_ref[...] = jnp.zeros_like(dnw_ref)

        dnw_ref[...] += dw

    return pl.pallas_call(
        body, name="in_proj_bwd_" + "_".join(str(w.shape[1]) for w in ws), grid=(T // tm,),
        in_specs=[_rows(tm, D_MODEL), _resident((1, D_MODEL)), _rows(tm, D_MODEL)] + [_rows(tm, w.shape[1]) for w in ws]
        + [_resident(w.shape) for w in ws],
        out_specs=[_rows(tm, D_MODEL), _rows(tm, D_MODEL), pl.BlockSpec((1, D_MODEL), lambda i: (0, 0))],
        out_shape=[jax.ShapeDtypeStruct((T, D_MODEL), f32), jax.ShapeDtypeStruct((T, D_MODEL), bf16), jax.ShapeDtypeStruct((1, D_MODEL), f32)],
        compiler_params=_params(("arbitrary",)),
    )(x, nw, dxo, *dys, *ws)


def out_proj_fwd(x, a, w, tm=1024):
    T = x.shape[0]
    K = a.shape[1]

    def body(x_ref, a_ref, w_ref, o_ref):
        o_ref[...] = x_ref[...] + _mm(a_ref[...], w_ref[...])

    return pl.pallas_call(
        body, name=f"out_proj_fwd_{K}", grid=(T // tm,),
        in_specs=[_rows(tm, D_MODEL), _rows(tm, K), _resident(w.shape)],
        out_specs=_rows(tm, D_MODEL), out_shape=jax.ShapeDtypeStruct((T, D_MODEL), f32),
        compiler_params=_params(("parallel",)),
    )(x, a, w)


def out_proj_bwd(dxo, w, tm=1024):
    T = dxo.shape[0]
    K = w.shape[0]

    def body(dxo_ref, w_ref, da_ref, dy_ref):
        dyb = dxo_ref[...].astype(bf16)
        dy_ref[...] = dyb
        da_ref[...] = _mm_nt(dyb, w_ref[...]).astype(bf16)

    return pl.pallas_call(
        body, name=f"out_proj_bwd_{K}", grid=(T // tm,),
        in_specs=[_rows(tm, D_MODEL), _resident(w.shape)],
        out_specs=[_rows(tm, K), _rows(tm, D_MODEL)],
        out_shape=[jax.ShapeDtypeStruct((T, K), bf16), jax.ShapeDtypeStruct((T, D_MODEL), bf16)],
        compiler_params=_params(("parallel",)),
    )(dxo, w)


def _halo_spec(tm, width, n_tiles, reverse):
    per = tm // HALO

    def idx(i):
        t = (n_tiles - 1 - i) if reverse else i
        return (jnp.maximum(t * per - 1, 0), 0)

    return pl.BlockSpec((HALO, width), idx)


def _tile_spec(tm, width, n_tiles, reverse):
    if reverse:
        return pl.BlockSpec((tm, width), lambda i: (n_tiles - 1 - i, 0))
    return _rows(tm, width)


ROW_BLOCK = 64


def _strip(s):
    return pl.ds(pl.multiple_of(s * LANES, LANES), LANES)


def _conv_rows(ext_ref, w_ref, cols, k_w, r0):
    base = HALO - (k_w - 1) + r0
    wins = [ext_ref[pl.ds(base + k, ROW_BLOCK), :] for k in range(k_w)]
    out = w_ref[pl.ds(0, 1), cols] * wins[0]
    for k in range(1, k_w):
        out = out + w_ref[pl.ds(k, 1), cols] * wins[k]
    return out, wins


def _shifted_back(d_ref, w_ref, cols, k_w, r0):
    out = w_ref[pl.ds(0, 1), cols] * d_ref[pl.ds(r0 + k_w - 1, ROW_BLOCK), :]
    for k in range(1, k_w):
        out = out + w_ref[pl.ds(k, 1), cols] * d_ref[pl.ds(r0 + k_w - 1 - k, ROW_BLOCK), :]
    return out


def ssd_conv_fwd(xbc, conv_w, conv_b, dt_raw, dt_bias, tm=512):
    T = xbc.shape[0]
    nt = T // tm
    K = SSD_CONV_K

    def body(x_ref, halo_ref, w_ref, b_ref, dtr_ref, dtb_ref, act_ref, dt_ref, ext_ref):
        first = pl.program_id(0) == 0

        def strip(s, carry):
            cols = _strip(s)
            ext_ref[pl.ds(0, HALO), :] = jnp.where(first, 0.0, halo_ref[:, cols].astype(f32))
            ext_ref[pl.ds(HALO, tm), :] = x_ref[:, cols].astype(f32)
            for r0 in range(0, tm, ROW_BLOCK):
                pre, _ = _conv_rows(ext_ref, w_ref, cols, K, r0)
                pre = pre + b_ref[:, cols]
                act_ref[pl.ds(r0, ROW_BLOCK), cols] = (pre * _sigmoid(pre)).astype(bf16)
            return carry

        lax.fori_loop(0, SSD_CONV_DIM // LANES, strip, 0)
        dt = _softplus(dtr_ref[...] + dtb_ref[...])
        lane = lax.broadcasted_iota(jnp.int32, (1, LANES), 1)
        for g in range(SSD_GROUPS):
            dt_ref[g] = jnp.where(lane < 8, dt if g == 0 else pltpu.roll(dt, LANES - 8 * g, axis=1), 0.0)

    return pl.pallas_call(
        body, name="ssd_conv_fwd", grid=(nt,),
        in_specs=[_rows(tm, SSD_CONV_DIM), _halo_spec(tm, SSD_CONV_DIM, nt, False), _resident(conv_w.shape), _resident(conv_b.shape),
                  _rows(tm, LANES), _resident(dt_bias.shape)],
        out_specs=[_rows(tm, SSD_CONV_DIM), pl.BlockSpec((SSD_GROUPS, tm, LANES), lambda i: (0, i, 0))],
        out_shape=[jax.ShapeDtypeStruct((T, SSD_CONV_DIM), bf16), jax.ShapeDtypeStruct((SSD_GROUPS, T, LANES), f32)],
        scratch_shapes=[pltpu.VMEM((tm + HALO, LANES), f32)],
        compiler_params=_params(("parallel",)),
    )(xbc, xbc, conv_w, conv_b, dt_raw, dt_bias)


def ssd_conv_bwd(xbc, conv_w, conv_b, dt_raw, dt_bias, dxs_a, dxs_b, db, dc, ddt, tm=512):
    T = xbc.shape[0]
    nt = T // tm
    K = SSD_CONV_K

    def body(x_ref, halo_ref, w_ref, b_ref, dtr_ref, dtb_ref, da_ref, dbb_ref, db_ref, dc_ref, ddt_ref,
             dx_ref, ddtr_ref, dw_ref, dbias_ref, ddtb_ref, ext_ref, dpre_ref, carry_ref):
        i = pl.program_id(0)

        @pl.when(i == 0)
        def _():
            carry_ref[...] = jnp.zeros_like(carry_ref)
            dw_ref[...] = jnp.zeros_like(dw_ref)
            dbias_ref[...] = jnp.zeros_like(dbias_ref)
            ddtb_ref[...] = jnp.zeros_like(ddtb_ref)

        first_tile = i == nt - 1

        def run_strips(lo, hi, load_dact):
            def strip(s, carry):
                cols = _strip(s)
                ext_ref[pl.ds(0, HALO), :] = jnp.where(first_tile, 0.0, halo_ref[:, cols].astype(f32))
                ext_ref[pl.ds(HALO, tm), :] = x_ref[:, cols].astype(f32)
                dpre_ref[pl.ds(tm, 8), :] = carry_ref[:, cols]
                bias = b_ref[:, cols]
                dws = [jnp.zeros((1, LANES), f32) for _ in range(K)]
                dbs = jnp.zeros((1, LANES), f32)
                for r0 in range(0, tm, ROW_BLOCK):
                    pre, wins = _conv_rows(ext_ref, w_ref, cols, K, r0)
                    pre = pre + bias
                    sg = _sigmoid(pre)
                    dpre = load_dact(s, r0) * (sg * (1.0 + pre * (1.0 - sg)))
                    dpre_ref[pl.ds(r0, ROW_BLOCK), :] = dpre
                    dbs = dbs + jnp.sum(dpre, axis=0, keepdims=True)
                    for k in range(K):
                        dws[k] = dws[k] + jnp.sum(dpre * wins[k], axis=0, keepdims=True)
                carry_ref[:, cols] = dpre_ref[pl.ds(0, 8), :]
                for r0 in range(0, tm, ROW_BLOCK):
                    dx_ref[pl.ds(r0, ROW_BLOCK), cols] = _shifted_back(dpre_ref, w_ref, cols, K, r0).astype(bf16)
                for k in range(K):
                    dw_ref[pl.ds(k, 1), cols] += dws[k]
                dbias_ref[:, cols] += dbs
                return carry

            lax.fori_loop(lo, hi, strip, 0)

        rows = lambda r0: pl.ds(r0, ROW_BLOCK)
        n_x = SSD_INNER // LANES
        n_g = SSD_GROUPS * SSD_STATE // LANES
        run_strips(0, n_x, lambda s, r0: da_ref[rows(r0), _strip(s)].astype(f32) + dbb_ref[rows(r0), _strip(s)].astype(f32))
        run_strips(n_x, n_x + n_g, lambda s, r0: db_ref[rows(r0), _strip(s - n_x)].astype(f32))
        run_strips(n_x + n_g, n_x + 2 * n_g, lambda s, r0: dc_ref[rows(r0), _strip(s - n_x - n_g)].astype(f32))
        lane = lax.broadcasted_iota(jnp.int32, (1, LANES), 1)
        ddt = jnp.where(lane < 8, ddt_ref[0], 0.0)
        for g in range(1, SSD_GROUPS):
            ddt = ddt + pltpu.roll(jnp.where(lane < 8, ddt_ref[g], 0.0), 8 * g, axis=1)
        ddtr = ddt * _sigmoid(dtr_ref[...] + dtb_ref[...])
        ddtr_ref[...] = ddtr.astype(bf16)
        ddtb_ref[...] += jnp.sum(ddtr, axis=0, keepdims=True)

    rev = functools.partial(_tile_spec, tm, n_tiles=nt, reverse=True)
    const = lambda shape: pl.BlockSpec(shape, lambda i: (0, 0))
    return pl.pallas_call(
        body, name="ssd_conv_bwd", grid=(nt,),
        in_specs=[rev(width=SSD_CONV_DIM), _halo_spec(tm, SSD_CONV_DIM, nt, True), _resident(conv_w.shape), _resident(conv_b.shape),
                  rev(width=LANES), _resident(dt_bias.shape), rev(width=SSD_INNER), rev(width=SSD_INNER),
                  rev(width=SSD_GROUPS * SSD_STATE), rev(width=SSD_GROUPS * SSD_STATE),
                  pl.BlockSpec((SSD_GROUPS, tm, LANES), lambda i: (0, nt - 1 - i, 0))],
        out_specs=[rev(width=SSD_CONV_DIM), rev(width=LANES), const((8, SSD_CONV_DIM)), const((1, SSD_CONV_DIM)), const((1, LANES))],
        out_shape=[jax.ShapeDtypeStruct((T, SSD_CONV_DIM), bf16), jax.ShapeDtypeStruct((T, LANES), bf16),
                   jax.ShapeDtypeStruct((8, SSD_CONV_DIM), f32), jax.ShapeDtypeStruct((1, SSD_CONV_DIM), f32), jax.ShapeDtypeStruct((1, LANES), f32)],
        scratch_shapes=[pltpu.VMEM((tm + HALO, LANES), f32), pltpu.VMEM((tm + 8, LANES), f32), pltpu.VMEM((8, SSD_CONV_DIM), f32)],
        compiler_params=_params(("arbitrary",)),
    )(xbc, xbc, conv_w, conv_b, dt_raw, dt_bias, dxs_a, dxs_b, db, dc, ddt)


def _ssd_chunk(xs, bm, cm, dt, alog, st):
    L = SSD_CHUNK
    row = lax.broadcasted_iota(jnp.int32, (L, L), 0)
    col = lax.broadcasted_iota(jnp.int32, (L, L), 1)
    causal = row >= col
    tril = jnp.where(causal, 1.0, 0.0).astype(f32)
    lane = lax.broadcasted_iota(jnp.int32, (1, LANES), 1)
    sub = lax.broadcasted_iota(jnp.int32, (LANES, 1), 0)
    lo = lane < SSD_HEAD_DIM
    last_row = sub == L - 1

    dta = dt * (-jnp.exp(alog))
    a_cs = jnp.dot(tril, dta, precision=lax.Precision.HIGHEST, preferred_element_type=f32)
    a_cs_t = a_cs.T
    bmb = bm.astype(bf16)
    cmb = cm.astype(bf16)
    cb = _mm_nt(cmb, bmb)
    c_st = _mm(cmb, st.astype(bf16))

    def head_col(v, e):
        return jnp.sum(jnp.where(lane == e, v, 0.0), axis=1, keepdims=True)

    def head_row(v, e):
        return jnp.sum(jnp.where(sub == e, v, 0.0), axis=0, keepdims=True)

    ys, sts = [], []
    for j in range(4):
        e0, e1 = 2 * j, 2 * j + 1
        c0, c1 = head_col(a_cs, e0), head_col(a_cs, e1)
        acs_x = jnp.where(lo, c0, c1)
        dt_x = jnp.where(lo, head_col(dt, e0), head_col(dt, e1))
        xd = xs[:, j * LANES:(j + 1) * LANES] * dt_x
        m0 = cb * jnp.exp(jnp.where(causal, c0 - head_row(a_cs_t, e0), NEG_BIG))
        m1 = cb * jnp.exp(jnp.where(causal, c1 - head_row(a_cs_t, e1), NEG_BIG))
        mcat = jnp.concatenate([m0, m1], axis=1).astype(bf16)
        xcat = jnp.concatenate([jnp.where(lo, xd, 0.0), jnp.where(lo, 0.0, xd)], axis=0).astype(bf16)
        y_diag = _mm(mcat, xcat)
        a_last = jnp.sum(jnp.where(last_row, acs_x, 0.0), axis=0, keepdims=True)
        x_dec = (xd * jnp.exp(a_last - acs_x)).astype(bf16)
        s_new = _mm_tn(bmb, x_dec)
        y_off = c_st[:, j * LANES:(j + 1) * LANES] * jnp.exp(acs_x)
        ys.append(y_diag + y_off)
        sts.append(jnp.exp(a_last) * st[:, j * LANES:(j + 1) * LANES] + s_new)
    return jnp.concatenate(ys, axis=1), jnp.concatenate(sts, axis=1)


SCAN_GROUPS_FWD = 4
SCAN_GROUPS_BWD = 1


def _scan_specs(nc, reverse, gs):
    L = SSD_CHUNK
    ch = (lambda c: nc - 1 - c) if reverse else (lambda c: c)
    gw = SSD_INNER // SSD_GROUPS
    b0 = SSD_INNER // (gs * SSD_STATE)
    c0 = (SSD_INNER + SSD_GROUPS * SSD_STATE) // (gs * SSD_STATE)
    xs = pl.BlockSpec((L, gs * gw), lambda g, c: (ch(c), g))
    bm = pl.BlockSpec((L, gs * SSD_STATE), lambda g, c: (ch(c), b0 + g))
    cm = pl.BlockSpec((L, gs * SSD_STATE), lambda g, c: (ch(c), c0 + g))
    dt = pl.BlockSpec((gs, L, LANES), lambda g, c: (g, ch(c), 0))
    alog = pl.BlockSpec((gs, 1, LANES), lambda g, c: (g, 0, 0))
    st = pl.BlockSpec((gs, None, SSD_STATE, gw), lambda g, c: (g, ch(c), 0, 0))
    y = pl.BlockSpec((L, gs * gw), lambda g, c: (ch(c), g))
    grp = pl.BlockSpec((L, gs * SSD_STATE), lambda g, c: (ch(c), g))
    return xs, bm, cm, dt, alog, st, y, grp


def ssd_scan_fwd(act, dt4, alog4, rider=None):
    T = act.shape[0]
    nc = T // SSD_CHUNK
    gs = SCAN_GROUPS_FWD
    ng = SSD_GROUPS // gs
    gw = SSD_INNER // SSD_GROUPS
    xs_s, bm_s, cm_s, dt_s, alog_s, st_s, y_s, _ = _scan_specs(nc, False, gs)
    r_in, r_out, r_shapes, r_scratch, r_args = _rider_specs(rider)

    def body(xs_ref, bm_ref, cm_ref, dt_ref, alog_ref, y_ref, st_ref, st_scr):
        @pl.when(pl.program_id(1) == 0)
        def _():
            st_scr[...] = jnp.zeros_like(st_scr)

        for q in range(gs):
            xc, gc = pl.ds(q * gw, gw), pl.ds(q * SSD_STATE, SSD_STATE)
            st = st_scr[q]
            st_ref[q] = st
            y, st_new = _ssd_chunk(xs_ref[:, xc].astype(f32), bm_ref[:, gc].astype(f32), cm_ref[:, gc].astype(f32), dt_ref[q], alog_ref[q], st)
            y_ref[:, xc] = y.astype(bf16)
            st_scr[q] = st_new

    first = lambda: jnp.logical_and(pl.program_id(0) == 0, pl.program_id(1) == 0)
    last = lambda: jnp.logical_and(pl.program_id(0) == ng - 1, pl.program_id(1) == nc - 1)
    return pl.pallas_call(
        _carry(body, 5, 2, rider, first, last), name="ssd_scan_fwd" if rider is None else "ssd_scan_fwd_carrying", grid=(ng, nc),
        in_specs=[xs_s, bm_s, cm_s, dt_s, alog_s] + r_in, out_specs=[y_s, st_s] + r_out,
        out_shape=[jax.ShapeDtypeStruct((T, SSD_INNER), bf16), jax.ShapeDtypeStruct((SSD_GROUPS, nc, SSD_STATE, gw), f32)] + r_shapes,
        scratch_shapes=[pltpu.VMEM((gs, SSD_STATE, gw), f32)] + r_scratch,
        compiler_params=_params(("parallel" if rider is None else "arbitrary", "arbitrary")),
    )(act, act, act, dt4, alog4, *r_args)


def ssd_scan_bwd(act, dt4, alog4, states, dy, rider=None):
    T = act.shape[0]
    nc = T // SSD_CHUNK
    gs = SCAN_GROUPS_BWD
    ng = SSD_GROUPS // gs
    gw = SSD_INNER // SSD_GROUPS
    xs_s, bm_s, cm_s, dt_s, alog_s, st_s, y_s, grp_s = _scan_specs(nc, True, gs)
    r_in, r_out, r_shapes, r_scratch, r_args = _rider_specs(rider)

    def body(xs_ref, bm_ref, cm_ref, dt_ref, alog_ref, st_ref, dy_ref, dxs_ref, db_ref, dc_ref, ddt_ref, dalog_ref, dst_scr):
        @pl.when(pl.program_id(1) == 0)
        def _():
            dst_scr[...] = jnp.zeros_like(dst_scr)
            dalog_ref[...] = jnp.zeros_like(dalog_ref)

        for q in range(gs):
            xc, gc = pl.ds(q * gw, gw), pl.ds(q * SSD_STATE, SSD_STATE)
            _, vjp = jax.vjp(_ssd_chunk, xs_ref[:, xc].astype(f32), bm_ref[:, gc].astype(f32), cm_ref[:, gc].astype(f32),
                             dt_ref[q], alog_ref[q], st_ref[q])
            dxs, dbm, dcm, ddt, dalog, dst = vjp((dy_ref[:, xc].astype(f32), dst_scr[q]))
            dxs_ref[:, xc] = dxs.astype(bf16)
            db_ref[:, gc] = dbm.astype(bf16)
            dc_ref[:, gc] = dcm.astype(bf16)
            ddt_ref[q] = ddt
            dalog_ref[q] += dalog
            dst_scr[q] = dst

    first = lambda: jnp.logical_and(pl.program_id(0) == 0, pl.program_id(1) == 0)
    last = lambda: jnp.logical_and(pl.program_id(0) == ng - 1, pl.program_id(1) == nc - 1)
    return pl.pallas_call(
        _carry(body, 7, 5, rider, first, last), name="ssd_scan_bwd" if rider is None else "ssd_scan_bwd_carrying", grid=(ng, nc),
        in_specs=[xs_s, bm_s, cm_s, dt_s, alog_s, st_s, y_s] + r_in,
        out_specs=[y_s, grp_s, grp_s, dt_s, alog_s] + r_out,
        out_shape=[jax.ShapeDtypeStruct((T, SSD_INNER), bf16), jax.ShapeDtypeStruct((T, SSD_GROUPS * SSD_STATE), bf16),
                   jax.ShapeDtypeStruct((T, SSD_GROUPS * SSD_STATE), bf16), jax.ShapeDtypeStruct((SSD_GROUPS, T, LANES), f32),
                   jax.ShapeDtypeStruct((SSD_GROUPS, 1, LANES), f32)] + r_shapes,
        scratch_shapes=[pltpu.VMEM((gs, SSD_STATE, gw), f32)] + r_scratch,
        compiler_params=_params(("parallel" if rider is None else "arbitrary", "arbitrary")),
    )(act, act, act, dt4, alog4, states, dy, *r_args)


GATE_ROWS = 256


def _ssd_gate(y, xs, z, d_x, nw):
    g = (y + xs * d_x) * (z * _sigmoid(z))
    return g * lax.rsqrt(jnp.mean(g * g, axis=-1, keepdims=True) + RMS_EPS) * nw


def _gate_blocks(tm, fn):
    gw = SSD_INNER // SSD_GROUPS

    def block(r, carry):
        rows = pl.ds(r * GATE_ROWS if isinstance(r, int) else pl.multiple_of(r * GATE_ROWS, GATE_ROWS), GATE_ROWS)
        for k in range(SSD_GROUPS):
            fn(rows, pl.ds(k * gw, gw))
        return carry

    if tm == GATE_ROWS:
        block(0, 0)
    else:
        lax.fori_loop(0, tm // GATE_ROWS, block, 0)


def ssd_gate_fwd(y, act, z, d_x, nw, tm=512):
    T = y.shape[0]

    def body(y_ref, xs_ref, z_ref, d_ref, nw_ref, o_ref):
        def one(rows, cols):
            o_ref[rows, cols] = _ssd_gate(y_ref[rows, cols].astype(f32), xs_ref[rows, cols].astype(f32), z_ref[rows, cols].astype(f32),
                                          d_ref[:, cols], nw_ref[:, cols]).astype(bf16)

        _gate_blocks(tm, one)

    return pl.pallas_call(
        body, name="ssd_gate_fwd", grid=(T // tm,),
        in_specs=[_rows(tm, SSD_INNER), _rows(tm, SSD_INNER), _rows(tm, SSD_INNER), _resident(d_x.shape), _resident(nw.shape)],
        out_specs=_rows(tm, SSD_INNER), out_shape=jax.ShapeDtypeStruct((T, SSD_INNER), bf16),
        compiler_params=_params(("parallel",)),
    )(y, act, z, d_x, nw)


def ssd_gate_bwd(y, act, z, d_x, nw, dgn, tm=512):
    T = y.shape[0]

    def body(y_ref, xs_ref, z_ref, d_ref, nw_ref, dgn_ref, dy_ref, dxs_ref, dz_ref, dd_ref, dnw_ref):
        @pl.when(pl.program_id(0) == 0)
        def _():
            dd_ref[...] = jnp.zeros_like(dd_ref)
            dnw_ref[...] = jnp.zeros_like(dnw_ref)

        def one(rows, cols):
            _, vjp = jax.vjp(_ssd_gate, y_ref[rows, cols].astype(f32), xs_ref[rows, cols].astype(f32), z_ref[rows, cols].astype(f32),
                             d_ref[:, cols], nw_ref[:, cols])
            dy, dxs, dz, dd, dnw = vjp(dgn_ref[rows, cols].astype(f32))
            dy_ref[rows, cols] = dy.astype(bf16)
            dxs_ref[rows, cols] = dxs.astype(bf16)
            dz_ref[rows, cols] = dz.astype(bf16)
            dd_ref[:, cols] += dd
            dnw_ref[:, cols] += dnw

        _gate_blocks(tm, one)

    const = pl.BlockSpec((1, SSD_INNER), lambda i: (0, 0))
    return pl.pallas_call(
        body, name="ssd_gate_bwd", grid=(T // tm,),
        in_specs=[_rows(tm, SSD_INNER), _rows(tm, SSD_INNER), _rows(tm, SSD_INNER), _resident(d_x.shape), _resident(nw.shape), _rows(tm, SSD_INNER)],
        out_specs=[_rows(tm, SSD_INNER)] * 3 + [const, const],
        out_shape=[jax.ShapeDtypeStruct((T, SSD_INNER), bf16)] * 3 + [jax.ShapeDtypeStruct((1, SSD_INNER), f32)] * 2,
        compiler_params=_params(("arbitrary",)),
    )(y, act, z, d_x, nw, dgn)


def sc_mid_fwd(bcu, conv_w, tm=512):
    T = bcu.shape[0]
    nt = T // tm
    Dm = D_MODEL

    def body(x_ref, halo_ref, w_ref, q_ref, ext_ref):
        first = pl.program_id(0) == 0
        n_s = Dm // LANES

        def strip(s, carry):
            cols, c_cols, u_cols = _strip(s), _strip(s + n_s), _strip(s + 2 * n_s)
            ext_ref[pl.ds(0, HALO), :] = jnp.where(first, 0.0, halo_ref[:, c_cols].astype(f32) * halo_ref[:, u_cols].astype(f32))
            ext_ref[pl.ds(HALO, tm), :] = x_ref[:, c_cols].astype(f32) * x_ref[:, u_cols].astype(f32)
            for r0 in range(0, tm, ROW_BLOCK):
                rows = pl.ds(r0, ROW_BLOCK)
                v, _ = _conv_rows(ext_ref, w_ref, cols, SC_CONV_K, r0)
                q_ref[rows, cols] = (x_ref[rows, cols].astype(f32) * v).astype(bf16)
            return carry

        lax.fori_loop(0, n_s, strip, 0)

    return pl.pallas_call(
        body, name="sc_mid_fwd", grid=(nt,),
        in_specs=[_rows(tm, 3 * Dm), _halo_spec(tm, 3 * Dm, nt, False), _resident(conv_w.shape)],
        out_specs=_rows(tm, Dm), out_shape=jax.ShapeDtypeStruct((T, Dm), bf16),
        scratch_shapes=[pltpu.VMEM((tm + HALO, LANES), f32)],
        compiler_params=_params(("parallel",)),
    )(bcu, bcu, conv_w)


def sc_mid_bwd(bcu, conv_w, dq, tm=512):
    T = bcu.shape[0]
    nt = T // tm
    Dm = D_MODEL
    K = SC_CONV_K

    def body(x_ref, halo_ref, w_ref, dq_ref, dx_ref, dw_ref, ext_ref, dv_ref, carry_ref):
        i = pl.program_id(0)

        @pl.when(i == 0)
        def _():
            carry_ref[...] = jnp.zeros_like(carry_ref)
            dw_ref[...] = jnp.zeros_like(dw_ref)

        first_tile = i == nt - 1
        n_s = Dm // LANES

        def strip(s, carry):
            cols, c_cols, u_cols = _strip(s), _strip(s + n_s), _strip(s + 2 * n_s)
            ext_ref[pl.ds(0, HALO), :] = jnp.where(first_tile, 0.0, halo_ref[:, c_cols].astype(f32) * halo_ref[:, u_cols].astype(f32))
            ext_ref[pl.ds(HALO, tm), :] = x_ref[:, c_cols].astype(f32) * x_ref[:, u_cols].astype(f32)
            dv_ref[pl.ds(tm, 8), :] = carry_ref[:, cols]
            dws = [jnp.zeros((1, LANES), f32) for _ in range(K)]
            for r0 in range(0, tm, ROW_BLOCK):
                rows = pl.ds(r0, ROW_BLOCK)
                v, wins = _conv_rows(ext_ref, w_ref, cols, K, r0)
                dqv = dq_ref[rows, cols].astype(f32)
                dv = dqv * x_ref[rows, cols].astype(f32)
                dv_ref[rows, :] = dv
                dx_ref[rows, cols] = (dqv * v).astype(bf16)
                for k in range(K):
                    dws[k] = dws[k] + jnp.sum(dv * wins[k], axis=0, keepdims=True)
            carry_ref[:, cols] = dv_ref[pl.ds(0, 8), :]
            for r0 in range(0, tm, ROW_BLOCK):
                rows = pl.ds(r0, ROW_BLOCK)
                dp = _shifted_back(dv_ref, w_ref, cols, K, r0)
                dx_ref[rows, c_cols] = (dp * x_ref[rows, u_cols].astype(f32)).astype(bf16)
                dx_ref[rows, u_cols] = (dp * x_ref[rows, c_cols].astype(f32)).astype(bf16)
            for k in range(K):
                dw_ref[pl.ds(k, 1), cols] += dws[k]
            return carry

        lax.fori_loop(0, n_s, strip, 0)

    return pl.pallas_call(
        body, name="sc_mid_bwd", grid=(nt,),
        in_specs=[_tile_spec(tm, 3 * Dm, nt, True), _halo_spec(tm, 3 * Dm, nt, True), _resident(conv_w.shape), _tile_spec(tm, Dm, nt, True)],
        out_specs=[_tile_spec(tm, 3 * Dm, nt, True), pl.BlockSpec((8, Dm), lambda i: (0, 0))],
        out_shape=[jax.ShapeDtypeStruct((T, 3 * Dm), bf16), jax.ShapeDtypeStruct((8, Dm), f32)],
        scratch_shapes=[pltpu.VMEM((tm + HALO, LANES), f32), pltpu.VMEM((tm + 8, LANES), f32), pltpu.VMEM((8, Dm), f32)],
        compiler_params=_params(("arbitrary",)),
    )(bcu, bcu, conv_w, dq)


def loss_head(x, fw, target, tm=1024):
    T = x.shape[0]

    def body(x_ref, fw_ref, t_ref, loss_ref, dx_ref, dfw_ref):
        @pl.when(pl.program_id(0) == 0)
        def _():
            loss_ref[...] = jnp.zeros_like(loss_ref)
            dfw_ref[...] = jnp.zeros_like(dfw_ref)

        w = fw_ref[...]
        y, xh, inv = _rms_fwd(x_ref[...], w)
        err = y - t_ref[...]
        loss_ref[...] += 0.5 * jnp.sum(jnp.mean(err * err, axis=-1, keepdims=True), axis=0, keepdims=True)
        dx, dw = _rms_bwd(err * (1.0 / D_MODEL), xh, inv, w)
        dx_ref[...] = dx
        dfw_ref[...] += dw

    return pl.pallas_call(
        body, name="loss_head", grid=(T // tm,),
        in_specs=[_rows(tm, D_MODEL), _resident((1, D_MODEL)), _rows(tm, D_MODEL)],
        out_specs=[pl.BlockSpec((1, LANES), lambda i: (0, 0)), _rows(tm, D_MODEL), pl.BlockSpec((1, D_MODEL), lambda i: (0, 0))],
        out_shape=[jax.ShapeDtypeStruct((1, LANES), f32), jax.ShapeDtypeStruct((T, D_MODEL), f32), jax.ShapeDtypeStruct((1, D_MODEL), f32)],
        compiler_params=_params(("arbitrary",)),
    )(x, fw, target)


ELEMENTWISE_TILE_BYTES = 1_600_000


def _row_tile(rows, width):
    row_bytes = 4 * _round_up(width, LANES)
    tile = rows
    while tile * row_bytes > ELEMENTWISE_TILE_BYTES and tile % 16 == 0:
        tile //= 2
    return tile


def adamw(g_parts, w, m, v, name="adamw", a0=0, prev=None):
    A, B, n = w.shape
    tb = _row_tile(B, n)
    n_parts = len(g_parts)
    arrays, specs = [], []
    for part in g_parts:
        lead, arr = part if isinstance(part, tuple) else ((), part)
        specs.append(pl.BlockSpec((None,) * (len(lead) + 1) + (tb, n), lambda a, t, lead=lead: tuple(lead) + (a, t, 0)))
        arrays.append(arr)
    na = arrays[0].shape[-3]
    prev = list(prev) if prev is not None else []

    def body(*refs):
        n = n_parts
        g_refs = refs[:n]
        w_ref, m_ref, v_ref = refs[n:n + 3]
        go_ref, d_ref, mo_ref, vo_ref = refs[n + 3 + len(prev):]
        g = g_refs[0][...].astype(f32)
        for r in g_refs[1:]:
            g = g + r[...].astype(f32)
        m_new = ADAM_B1 * m_ref[...] + (1.0 - ADAM_B1) * g
        v_new = ADAM_B2 * v_ref[...] + (1.0 - ADAM_B2) * (g * g)
        m_hat = m_new / (1.0 - ADAM_B1 ** ADAM_STEP)
        v_hat = v_new / (1.0 - ADAM_B2 ** ADAM_STEP)
        go_ref[...] = g
        d_ref[...] = -ADAM_LR * (m_hat / (jnp.sqrt(v_hat) + ADAM_EPS) + ADAM_WD * w_ref[...])
        mo_ref[...] = m_new
        vo_ref[...] = v_new

    plain = pl.BlockSpec((None, tb, n), lambda a, t: (a + a0, t, 0))
    return pl.pallas_call(
        body, name=name, grid=(na, B // tb), in_specs=specs + [plain] * 3 + [_ANY] * len(prev), out_specs=[plain] * 4,
        out_shape=[jax.ShapeDtypeStruct((A, B, n), f32)] * 4,
        input_output_aliases={n_parts + 3 + k: k for k in range(len(prev))},
        compiler_params=_params(("parallel", "parallel")),
    )(*arrays, w, m, v, *prev)


def pair_sum_bf16(ga, gb, name):
    _, A, B, n = gb.shape
    tb = _row_tile(B, n)

    def body(a_ref, b_ref, o_ref):
        o_ref[...] = (a_ref[...] + b_ref[...]).astype(bf16)

    return pl.pallas_call(
        body, name=name, grid=(3, A, B // tb),
        in_specs=[pl.BlockSpec((None, None, None, tb, n), lambda j, a, t: (0, j + 1, a, t, 0)),
                  pl.BlockSpec((None, None, tb, n), lambda j, a, t: (j + 1, a, t, 0))],
        out_specs=pl.BlockSpec((None, None, tb, n), lambda j, a, t: (j + 1, a, t, 0)),
        out_shape=jax.ShapeDtypeStruct((4, A, B, n), bf16),
        compiler_params=_params(("parallel", "parallel", "parallel")),
    )(ga, gb)


def assemble(gathered, axis, tk=256):
    _, A, K, n = gathered.shape
    if axis == 1:
        def body(w_ref, o_ref):
            o_ref[...] = jnp.concatenate([w_ref[j] for j in range(N_DEV)], axis=1)

        return pl.pallas_call(
            body, name=f"assemble_cols_{K}x{n}", grid=(A, K // tk),
            in_specs=[pl.BlockSpec((N_DEV, None, tk, n), lambda a, t: (0, a, t, 0))],
            out_specs=pl.BlockSpec((None, tk, N_DEV * n), lambda a, t: (a, t, 0)),
            out_shape=jax.ShapeDtypeStruct((A, K, N_DEV * n), gathered.dtype),
            compiler_params=_params(("parallel", "parallel")),
        )(gathered)

    def body(w_ref, o_ref):
        for j in range(N_DEV):
            o_ref[pl.ds(j * K, K), :] = w_ref[j]

    return pl.pallas_call(
        body, name=f"assemble_rows_{K}x{n}", grid=(A,),
        in_specs=[pl.BlockSpec((N_DEV, None, K, n), lambda a: (0, a, 0, 0))],
        out_specs=pl.BlockSpec((None, N_DEV * K, n), lambda a: (a, 0, 0)),
        out_shape=jax.ShapeDtypeStruct((A, N_DEV * K, n), gathered.dtype),
        compiler_params=_params(("parallel",)),
    )(gathered)


SSD_IN_PAD = 5248


def assemble_ssd_in(gathered, tk=256):
    _, A, K, n = gathered.shape

    def body(w_ref, z_ref, x_ref, dt_ref, full_ref):
        full_ref[:, pl.ds(SSD_IN_PAD - LANES, LANES)] = jnp.zeros((tk, LANES), gathered.dtype)
        for j in range(N_DEV):
            full_ref[:, pl.ds(j * n, n)] = w_ref[j]
        z_ref[...] = full_ref[:, pl.ds(0, SSD_INNER)]
        x_ref[...] = full_ref[:, pl.ds(SSD_INNER, SSD_CONV_DIM)]
        dt_ref[...] = full_ref[:, pl.ds(SSD_INNER + SSD_CONV_DIM, LANES)]

    widths = (SSD_INNER, SSD_CONV_DIM, LANES)
    return pl.pallas_call(
        body, name="assemble_ssd_in", grid=(A, K // tk),
        in_specs=[pl.BlockSpec((N_DEV, None, tk, n), lambda a, t: (0, a, t, 0))],
        out_specs=[pl.BlockSpec((None, tk, w), lambda a, t: (a, t, 0)) for w in widths],
        out_shape=[jax.ShapeDtypeStruct((A, K, w), gathered.dtype) for w in widths],
        scratch_shapes=[pltpu.VMEM((tk, SSD_IN_PAD), gathered.dtype)],
        compiler_params=_params(("parallel", "parallel")),
    )(gathered)


def ssd_in_to_shards(dwz, dwx, dwdt, buf, j, tk=256):
    K = dwz.shape[0]
    n = buf.shape[-1]
    fresh = isinstance(buf, jax.ShapeDtypeStruct)

    def body(z_ref, x_ref, dt_ref, *rest):
        o_ref, full_ref = rest[-2:]
        full_ref[:, pl.ds(0, SSD_INNER)] = z_ref[...]
        full_ref[:, pl.ds(SSD_INNER, SSD_CONV_DIM)] = x_ref[...]
        full_ref[:, pl.ds(SSD_INNER + SSD_CONV_DIM, LANES)] = dt_ref[...]
        my_c, my_chip = _my_core_and_chip()
        for d in range(N_DEV):
            o_ref[(d % 2) ^ my_c, (d // 2) ^ my_chip] = full_ref[:, pl.ds(d * n, n)]

    return pl.pallas_call(
        body, name="ssd_in_to_shards", grid=(K // tk,),
        in_specs=[_rows(tk, SSD_INNER), _rows(tk, SSD_CONV_DIM), _rows(tk, LANES)] + ([] if fresh else [_ANY]),
        out_specs=pl.BlockSpec((2, 4, None, tk, n), lambda t: (0, 0, j, t, 0)),
        out_shape=jax.ShapeDtypeStruct(buf.shape, f32),
        scratch_shapes=[pltpu.VMEM((tk, SSD_IN_PAD), f32)],
        input_output_aliases={} if fresh else {3: 0},
        compiler_params=_params(("parallel",)),
    )(dwz, dwx, dwdt, *([] if fresh else [buf]))


def sum_over_devices(gathered):
    _, R, W = gathered.shape

    def body(g_ref, o_ref):
        acc = g_ref[0]
        for k in range(1, N_DEV):
            acc = acc + g_ref[k]
        o_ref[...] = acc

    return pl.pallas_call(
        body, name="sum_over_devices", grid=(1,),
        in_specs=[pl.BlockSpec((N_DEV, R, W), lambda i: (0, 0, 0))], out_specs=pl.BlockSpec((R, W), lambda i: (0, 0)),
        out_shape=jax.ShapeDtypeStruct((R, W), f32), compiler_params=_params(("arbitrary",)),
    )(gathered)


_ANY = pl.BlockSpec(memory_space=pl.ANY)


class _Exchange:
    def __init__(self, inputs, out_shapes, scratch, start, finish):
        self.inputs, self.out_shapes, self.scratch, self.start, self.finish = inputs, out_shapes, scratch, start, finish

    def run(self, name):
        ni, no = len(self.inputs), len(self.out_shapes)

        def body(*refs):
            parts = (refs[:ni], refs[ni:ni + no], refs[ni + no:])
            self.start(*parts)
            self.finish(*parts)

        return pl.pallas_call(body, name=name, in_specs=[_ANY] * ni, out_specs=[_ANY] * no, out_shape=self.out_shapes,
                              scratch_shapes=self.scratch)(*self.inputs)


def _carry(body, n_in, n_out, rider, first, last):
    if rider is None:
        return body
    ri, ro = len(rider.inputs), len(rider.out_shapes)

    def hosted(*refs):
        a, b, c = n_in + ri, n_in + ri + n_out, n_in + ri + n_out + ro
        rs = len(refs) - c - len(rider.scratch)
        parts = (refs[n_in:a], refs[b:c], refs[c + rs:])

        @pl.when(first())
        def _():
            rider.start(*parts)

        body(*refs[:n_in], *refs[a:b], *refs[c:c + rs])

        @pl.when(last())
        def _():
            rider.finish(*parts)

    return hosted


def _rider_specs(rider):
    if rider is None:
        return [], [], [], [], []
    return [_ANY] * len(rider.inputs), [_ANY] * len(rider.out_shapes), list(rider.out_shapes), list(rider.scratch), list(rider.inputs)


def all_gather(blocks):
    n = len(blocks)

    def plan(x_refs, out_refs, sems):
        send_sems, recv_sems, local_sems = sems
        x, y, c = lax.axis_index("x"), lax.axis_index("y"), lax.axis_index("c")
        me, sibling = (x, y, c), (x, y, 1 - c)
        chips = [(1 - x, y), (x, 1 - y), (1 - x, 1 - y)]

        def copy(a, k, blk, to, src=None):
            px, py, pc = blk
            slot = out_refs[a].at[4 * px + 2 * py + pc]
            return pltpu.make_async_remote_copy(
                src_ref=slot if src is None else src, dst_ref=slot,
                send_sem=send_sems.at[7 * a + k], recv_sem=recv_sems.at[7 * a + k], device_id=to, device_id_type=MESH)

        mine = [pltpu.make_async_copy(x_refs[a], out_refs[a].at[4 * x + 2 * y + c], local_sems.at[a]) for a in range(n)]
        first = []
        for a in range(n):
            first += [copy(a, 0, me, sibling, src=x_refs[a])] + [copy(a, 1 + j, me, (*chip, c), src=x_refs[a]) for j, chip in enumerate(chips)]
        return c, me, sibling, chips, copy, mine, first

    def start(x_refs, out_refs, sems):
        _, _, _, _, _, mine, first = plan(x_refs, out_refs, sems)
        for cp in mine + first:
            cp.start()

    def finish(x_refs, out_refs, sems):
        c, me, sibling, chips, copy, mine, first = plan(x_refs, out_refs, sems)
        passed = []
        for j, chip in enumerate(chips):
            for a in range(n):
                copy(a, 1 + j, (*chip, c), me).wait_recv()
                passed.append(copy(a, 4 + j, (*chip, c), sibling))
                passed[-1].start()
        for a in range(n):
            copy(a, 0, sibling, me).wait_recv()
            for j, chip in enumerate(chips):
                copy(a, 4 + j, (*chip, 1 - c), me).wait_recv()
        for cp in first + passed:
            cp.wait_send()
        for cp in mine:
            cp.wait()

    return _Exchange(list(blocks), [jax.ShapeDtypeStruct((N_DEV,) + b.shape, b.dtype) for b in blocks],
                     [pltpu.SemaphoreType.DMA((7 * n,)), pltpu.SemaphoreType.DMA((7 * n,)), pltpu.SemaphoreType.DMA((n,))], start, finish)


def exchange_with_sibling(gs):
    n = len(gs)

    def plan(g_refs, recv_refs, sems):
        send_sems, recv_sems = sems
        x, y, c = lax.axis_index("x"), lax.axis_index("y"), lax.axis_index("c")
        return [pltpu.make_async_remote_copy(src_ref=g_refs[a].at[1], dst_ref=recv_refs[a], send_sem=send_sems.at[a],
                                             recv_sem=recv_sems.at[a], device_id=(x, y, 1 - c), device_id_type=MESH) for a in range(n)]

    def start(*refs):
        for cp in plan(*refs):
            cp.start()

    def finish(*refs):
        for cp in plan(*refs):
            cp.wait()

    return _Exchange(list(gs), [jax.ShapeDtypeStruct(g.shape[1:], g.dtype) for g in gs],
                     [pltpu.SemaphoreType.DMA((n,)), pltpu.SemaphoreType.DMA((n,))], start, finish)


def exchange_between_chips(parts):
    n = len(parts)

    def plan(p_refs, recv_refs, sems):
        send_sems, recv_sems = sems
        x, y, c = lax.axis_index("x"), lax.axis_index("y"), lax.axis_index("c")
        chips = [(2, (1 - x, y)), (1, (x, 1 - y)), (3, (1 - x, 1 - y))]
        return [pltpu.make_async_remote_copy(src_ref=p_refs[a].at[slot], dst_ref=recv_refs[a].at[k], send_sem=send_sems.at[3 * a + k],
                                             recv_sem=recv_sems.at[3 * a + k], device_id=(px, py, c), device_id_type=MESH)
                for a in range(n) for k, (slot, (px, py)) in enumerate(chips)]

    def start(*refs):
        for cp in plan(*refs):
            cp.start()

    def finish(*refs):
        for cp in plan(*refs):
            cp.wait()

    return _Exchange(list(parts), [jax.ShapeDtypeStruct((3,) + p.shape[1:], p.dtype) for p in parts],
                     [pltpu.SemaphoreType.DMA((3 * n,)), pltpu.SemaphoreType.DMA((3 * n,))], start, finish)


PARAMS = {
    "norm_w": ((DEPTH, 3, D_MODEL), 2),
    "ffn_w_gate": ((DEPTH, 2, D_MODEL, D_FF), 3),
    "ffn_w_up": ((DEPTH, 2, D_MODEL, D_FF), 3),
    "ffn_w_down": ((DEPTH, 2, D_FF, D_MODEL), 2),
    "ssd_w_in": ((2, D_MODEL, SSD_IN_DIM), 2),
    "ssd_conv_w": ((2, SSD_CONV_K, SSD_CONV_DIM), 2),
    "ssd_conv_b": ((2, SSD_CONV_DIM), None),
    "ssd_dt_bias": ((2, SSD_HEADS), None),
    "ssd_a_log": ((2, SSD_HEADS), None),
    "ssd_d": ((2, SSD_HEADS), None),
    "ssd_norm_w": ((2, SSD_INNER), None),
    "ssd_w_out": ((2, SSD_INNER, D_MODEL), 1),
    "sc_w_in": ((2, D_MODEL, 3 * D_MODEL), 2),
    "sc_conv_w": ((2, SC_CONV_K, D_MODEL), 2),
    "sc_w_out": ((2, D_MODEL, D_MODEL), 1),
    "final_norm_w": ((D_MODEL,), None),
}
NAMES = list(PARAMS)
BIG = ["ffn_w_gate", "ffn_w_up", "ffn_w_down", "ssd_w_in", "ssd_w_out", "sc_w_in", "sc_w_out"]
SMALL = [n for n in NAMES if n not in BIG]
SMALL_SHARDED = [n for n in SMALL if PARAMS[n][1] is not None]


def _round_up(n, m):
    return -(-n // m) * m


def _pack(flat_list, rows_multiple):
    flat = jnp.concatenate(flat_list)
    rows = _round_up(_round_up(flat.shape[0], PACK_W) // PACK_W, rows_multiple)
    return jnp.pad(flat, (0, rows * PACK_W - flat.shape[0])).reshape(rows, PACK_W)


def _unpack(packed, shapes, lead=()):
    flat = packed.reshape(lead + (-1,))
    out, off = [], 0
    for shp in shapes:
        n = 1
        for s in shp:
            n *= s
        out.append(flat[..., off:off + n].reshape(lead + tuple(shp)))
        off += n
    return out


def _local_shape(name):
    shp, ax = PARAMS[name]
    if ax is None:
        return shp
    return shp[:ax] + (shp[ax] // N_DEV,) + shp[ax + 1:]


def _full_from_gathered(g, name):
    shp, ax = PARAMS[name]
    return jnp.moveaxis(g, 0, ax).reshape(shp)


def _by_destination(full, name):
    shp, ax = PARAMS[name]
    loc = shp[ax] // N_DEV
    return jnp.moveaxis(full.reshape(shp[:ax] + (N_DEV, loc) + shp[ax + 1:]), ax, 0)


def _ssd_layer_fwd(xin, nw, p, rider=None):
    z, xbc, dt_raw = in_proj_fwd(xin, nw, [p["ssd_wz"], p["ssd_wx"], p["ssd_wdt"]], [bf16, bf16, f32])
    act, dt4 = ssd_conv_fwd(xbc, p["ssd_conv_w"], p["ssd_conv_b"], dt_raw, p["ssd_dt_bias"])
    y, states, *got = ssd_scan_fwd(act, dt4, p["ssd_alog4"], rider=rider)
    gn = ssd_gate_fwd(y, act, z, p["ssd_dx"], p["ssd_norm_w"])
    xout = out_proj_fwd(xin, gn, p["ssd_w_out"])
    return xout, (xin, z, xbc, dt_raw, act, dt4, y, states, gn), got


def _ssd_layer_bwd(dxo, nw, p, saved, gbuf, slab, rider=None):
    xin, z, xbc, dt_raw, act, dt4, y, states, gn = saved
    T = xin.shape[0]
    dgn, dyb = out_proj_bwd(dxo, p["ssd_w_out"])
    gbuf["ssd_w_out"] = tn_matmul_to_shards(gn, dyb, gbuf["ssd_w_out"], (slab,), 0)
    g = {}
    dy, dxs_skip, dz, dd_x, dgnw = ssd_gate_bwd(y, act, z, p["ssd_dx"], p["ssd_norm_w"], dgn)
    g["ssd_norm_w"] = dgnw[0]
    g["ssd_d"] = jnp.sum(dd_x.reshape(SSD_HEADS, SSD_HEAD_DIM), axis=1)
    dxs, db, dc, ddt4, dalog4, *got = ssd_scan_bwd(act, dt4, p["ssd_alog4"], states, dy, rider=rider)
    g["ssd_a_log"] = dalog4[:, 0, :8].reshape(SSD_HEADS)
    dxbc, ddt_raw, dcw, dcb, ddtb = ssd_conv_bwd(xbc, p["ssd_conv_w"], p["ssd_conv_b"], dt_raw, p["ssd_dt_bias"], dxs, dxs_skip, db, dc, ddt4)
    g["ssd_conv_w"] = dcw[:SSD_CONV_K]
    g["ssd_conv_b"] = dcb[0]
    g["ssd_dt_bias"] = ddtb[0, :SSD_HEADS]
    dx, h, dnw = in_proj_bwd(xin, nw, dxo, [dz, dxbc, ddt_raw], [p["ssd_wz"], p["ssd_wx"], p["ssd_wdt"]])
    gbuf["ssd_w_in"] = ssd_in_to_shards(tn_matmul(h, dz), tn_matmul(h, dxbc), tn_matmul(h, ddt_raw), gbuf["ssd_w_in"], slab)
    return dx, dnw, g, got


def _sc_layer_fwd(xin, nw, p):
    (bcu,) = in_proj_fwd(xin, nw, [p["sc_w_in"]], [bf16])
    q = sc_mid_fwd(bcu, p["sc_conv_w"])
    return out_proj_fwd(xin, q, p["sc_w_out"]), (xin, bcu, q)


def _sc_layer_bwd(dxo, nw, p, saved, gbuf, slab):
    xin, bcu, q = saved
    dq, dyb = out_proj_bwd(dxo, p["sc_w_out"])
    gbuf["sc_w_out"] = tn_matmul_to_shards(q, dyb, gbuf["sc_w_out"], (slab,), 0)
    dbcu, dcw = sc_mid_bwd(bcu, p["sc_conv_w"], dq)
    g = {"sc_conv_w": dcw[:SC_CONV_K]}
    dx, h, dnw = in_proj_bwd(xin, nw, dxo, [dbcu], [p["sc_w_in"]])
    gbuf["sc_w_in"] = tn_matmul_to_shards(h, dbcu, gbuf["sc_w_in"], (slab,), 1)
    return dx, dnw, g


def kernel(x, norm_w, ffn_w_gate, ffn_w_up, ffn_w_down, ssd_w_in, ssd_conv_w, ssd_conv_b, ssd_dt_bias, ssd_a_log, ssd_d, ssd_norm_w, ssd_w_out, sc_w_in, sc_conv_w, sc_w_out, final_norm_w, loss_target, m_norm_w, m_ffn_w_gate, m_ffn_w_up, m_ffn_w_down, m_ssd_w_in, m_ssd_conv_w, m_ssd_conv_b, m_ssd_dt_bias, m_ssd_a_log, m_ssd_d, m_ssd_norm_w, m_ssd_w_out, m_sc_w_in, m_sc_conv_w, m_sc_w_out, m_final_norm_w, v_norm_w, v_ffn_w_gate, v_ffn_w_up, v_ffn_w_down, v_ssd_w_in, v_ssd_conv_w, v_ssd_conv_b, v_ssd_dt_bias, v_ssd_a_log, v_ssd_d, v_ssd_norm_w, v_ssd_w_out, v_sc_w_in, v_sc_conv_w, v_sc_w_out, v_final_norm_w):
    w_loc = dict(zip(NAMES, (norm_w, ffn_w_gate, ffn_w_up, ffn_w_down, ssd_w_in, ssd_conv_w, ssd_conv_b, ssd_dt_bias, ssd_a_log, ssd_d, ssd_norm_w, ssd_w_out, sc_w_in, sc_conv_w, sc_w_out, final_norm_w)))
    m_loc = dict(zip(NAMES, (m_norm_w, m_ffn_w_gate, m_ffn_w_up, m_ffn_w_down, m_ssd_w_in, m_ssd_conv_w, m_ssd_conv_b, m_ssd_dt_bias, m_ssd_a_log, m_ssd_d, m_ssd_norm_w, m_ssd_w_out, m_sc_w_in, m_sc_conv_w, m_sc_w_out, m_final_norm_w)))
    v_loc = dict(zip(NAMES, (v_norm_w, v_ffn_w_gate, v_ffn_w_up, v_ffn_w_down, v_ssd_w_in, v_ssd_conv_w, v_ssd_conv_b, v_ssd_dt_bias, v_ssd_a_log, v_ssd_d, v_ssd_norm_w, v_ssd_w_out, v_sc_w_in, v_sc_conv_w, v_sc_w_out, v_final_norm_w)))
    ax, ay, ac = lax.axis_index("x"), lax.axis_index("y"), lax.axis_index("c")
    my_chip = 2 * ax + ay
    my_dev = 4 * ax + 2 * ay + ac
    T = x.shape[1]

    def as3d(a):
        return a.reshape((-1,) + a.shape[-2:])

    wb = {n: as3d(w_loc[n]).astype(bf16) for n in BIG}

    FFN = ["ffn_w_gate", "ffn_w_up", "ffn_w_down"]

    def mixer_names(i):
        return ["ssd_w_in", "ssd_w_out"] if i % 2 == 0 else ["sc_w_in", "sc_w_out"]

    ag_sets = [[(n, 0, 1) for n in FFN], [(n, 1, 1) for n in FFN] + [(n, 0, 1) for n in mixer_names(0)]]
    ag_sets += [[(n, 2 * r, 2) for n in FFN] + [(n, r // 2, 1) for n in mixer_names(r)] for r in (1, 2, 3)]

    def set_blocks(spec):
        return [wb[n][a0:a0 + na] for n, a0, na in spec]

    def set_weights(spec, gathered):
        q = {}
        for (n, _, _), g in zip(spec, gathered):
            if n == "ssd_w_in":
                q["ssd_wz"], q["ssd_wx"], q["ssd_wdt"] = assemble_ssd_in(g)
            else:
                q[n] = assemble(g, 1 if PARAMS[n][1] == len(PARAMS[n][0]) - 1 else 0)
        return q

    ss_shapes = [_local_shape(n) for n in SMALL_SHARDED]
    gathered0 = all_gather(set_blocks(ag_sets[0]) + [_pack([w_loc[n].reshape(-1) for n in SMALL_SHARDED], 8)]).run("all_gather_first")
    full = {}
    for n, part in zip(SMALL_SHARDED, _unpack(gathered0[-1], ss_shapes, lead=(N_DEV,))):
        full[n] = _full_from_gathered(part, n)
    for n in SMALL:
        if PARAMS[n][1] is None:
            full[n] = w_loc[n]
    small = {
        "ssd_conv_w": full["ssd_conv_w"],
        "ssd_conv_b": full["ssd_conv_b"].reshape(2, 1, SSD_CONV_DIM),
        "ssd_dt_bias": jnp.pad(full["ssd_dt_bias"], ((0, 0), (0, LANES - SSD_HEADS))).reshape(2, 1, LANES),
        "ssd_alog4": jnp.pad(full["ssd_a_log"].reshape(2, SSD_GROUPS, 1, 8), ((0, 0), (0, 0), (0, 0), (0, LANES - 8))),
        "ssd_dx": jnp.repeat(full["ssd_d"], SSD_HEAD_DIM, axis=1).reshape(2, 1, SSD_INNER),
        "ssd_norm_w": full["ssd_norm_w"].reshape(2, 1, SSD_INNER),
        "sc_conv_w": full["sc_conv_w"],
    }
    nw_all = full["norm_w"].reshape(DEPTH, 3, 1, D_MODEL)

    ffn_w = [[None, None] for _ in range(DEPTH)]
    mix_w = [None] * DEPTH

    def arrived(s, gathered):
        q = set_weights(ag_sets[s], gathered)
        ffn = tuple(q[n] for n in FFN)
        if s == 0:
            ffn_w[0][0] = ffn + ((0,),)
            return
        i = 0 if s == 1 else s - 1
        if s == 1:
            ffn_w[0][1] = ffn + ((0,),)
        else:
            ffn_w[i] = [ffn + ((0,),), ffn + ((1,),)]
        m = {n: v[0] for n, v in q.items() if n not in FFN}
        m.update({n: v[i // 2] for n, v in small.items() if n.startswith("ssd" if i % 2 == 0 else "sc")})
        mix_w[i] = m

    def rider_for(s):
        return all_gather(set_blocks(ag_sets[s]))

    xc = x[0]
    saved = []
    arrived(0, gathered0[:-1])
    for i in range(DEPTH):
        carried = {0: (1, 2, 3), 1: (4, None, None)}.get(i, (None, None, None))
        wg, wu, wd, idx = ffn_w[i][0]
        x1, g1, u1, a1, *got = ffn_fwd(xc, nw_all[i, 0], wg, wu, wd, idx, rider=rider_for(carried[0]) if carried[0] else None)
        if carried[0]:
            arrived(carried[0], got)
        if i % 2 == 0:
            x2, mix_saved, got = _ssd_layer_fwd(x1, nw_all[i, 1], mix_w[i], rider=rider_for(carried[1]) if carried[1] else None)
            if carried[1]:
                arrived(carried[1], got)
        else:
            x2, mix_saved = _sc_layer_fwd(x1, nw_all[i, 1], mix_w[i])
        wg, wu, wd, idx = ffn_w[i][1]
        x3, g3, u3, a3, *got = ffn_fwd(x2, nw_all[i, 2], wg, wu, wd, idx, rider=rider_for(carried[2]) if carried[2] else None)
        if carried[2]:
            arrived(carried[2], got)
        saved.append(((xc, g1, u1, a1), mix_saved, (x2, g3, u3, a3)))
        xc = x3

    loss_row, dx, dfw = loss_head(xc, full["final_norm_w"].reshape(1, D_MODEL), loss_target[0])
    loss = lax.psum(loss_row[0, 0], ("x", "y", "c"))

    grads = {n: [None] * PARAMS[n][0][0] for n in SMALL if n != "final_norm_w"}
    grads["final_norm_w"] = dfw[0]
    dnorm = [[None] * 3 for _ in range(DEPTH)]
    def slabs(n, which):
        if n.startswith("ffn"):
            return {"early": (2, 6), "mid": (1, 1), "last": (0, 1)}[which]
        if n.startswith("ssd"):
            return {"early": (1, 1), "mid": (0, 1), "last": (0, 0)}[which]
        return {"early": (0, 2), "mid": (0, 0), "last": (0, 0)}[which]

    gb = {which: {n: jax.ShapeDtypeStruct((2, 4, slabs(n, which)[1]) + wb[n].shape[1:], f32) for n in BIG if slabs(n, which)[1]}
          for which in ("early", "mid", "last")}

    def ffn_back(i, k, dxo, sv, rider=None):
        xin, g_, u_, a_ = sv
        which = "early" if i > 0 else ("mid" if k == 1 else "last")
        gbuf = gb[which]
        slab = 2 * i + k - slabs("ffn_w_gate", which)[0]
        wg, wu, wd, idx = ffn_w[i][k]
        dxi, h, dyb, dg, du, dnw, *got = ffn_bwd_dx(xin, dxo, g_, u_, nw_all[i, 2 * k], wg, wu, wd, idx, rider=rider)
        dnorm[i][2 * k] = dnw[0]
        gbuf["ffn_w_gate"] = tn_matmul_to_shards(h, dg, gbuf["ffn_w_gate"], (slab,), 1)
        gbuf["ffn_w_up"] = tn_matmul_to_shards(h, du, gbuf["ffn_w_up"], (slab,), 1)
        gbuf["ffn_w_down"] = tn_matmul_to_shards(a_, dyb, gbuf["ffn_w_down"], (slab,), 0)
        return dxi, got

    def reduce_in_chip(gbuf, from_sibling=None):
        names = list(gbuf)
        bufs = [gbuf[n] for n in names]
        if from_sibling is None:
            from_sibling = exchange_with_sibling(bufs).run("exchange_with_sibling")
        return names, bufs, from_sibling, [pair_sum_bf16(g, fs, "pair_sum_" + n) for n, g, fs in zip(names, bufs, from_sibling)]

    reduced, from_chips = {}, {}
    for i in reversed(range(DEPTH)):
        j = i // 2
        sv_a, sv_mix, sv_b = saved[i]
        if i == 0:
            dx, got = ffn_back(i, 1, dx, sv_b, rider=exchange_with_sibling(list(gb["early"].values())))
            reduced["early"] = reduce_in_chip(gb["early"], from_sibling=got)
        else:
            dx, _ = ffn_back(i, 1, dx, sv_b)
        if i % 2 == 0:
            rider = exchange_between_chips(reduced["early"][3]) if i == 0 else None
            dx, dnw, gm, got = _ssd_layer_bwd(dx, nw_all[i, 1], mix_w[i], sv_mix, gb["mid" if i == 0 else "early"], 0, rider=rider)
            if i == 0:
                from_chips["early"] = got
                reduced["mid"] = reduce_in_chip(gb["mid"])
        else:
            dx, dnw, gm = _sc_layer_bwd(dx, nw_all[i, 1], mix_w[i], sv_mix, gb["early"], j)
        dnorm[i][1] = dnw[0]
        for n, val in gm.items():
            grads[n][j] = val
        dx, got = ffn_back(i, 0, dx, sv_a, rider=exchange_between_chips(reduced["mid"][3]) if i == 0 else None)
        if i == 0:
            from_chips["mid"] = got

    grads["norm_w"] = jnp.stack([jnp.stack(r) for r in dnorm])
    for n in SMALL:
        if isinstance(grads[n], list):
            grads[n] = jnp.stack(grads[n])

    reduced["last"] = reduce_in_chip(gb["last"])
    from_chips["last"] = exchange_between_chips(reduced["last"][3]).run("exchange_between_chips")
    results = [{}, {}, {}, {}]
    outs = {}
    for which in ("last", "mid", "early"):
        names, bufs, from_sibling, _ = reduced[which]
        for n, g, fs, fc in zip(names, bufs, from_sibling, from_chips[which]):
            parts = [((0, 0), g), ((0,), fs), ((0,), fc), ((1,), fc), ((2,), fc)]
            outs[n] = adamw(parts, as3d(w_loc[n]), as3d(m_loc[n]), as3d(v_loc[n]), name="adamw_" + n + "_" + which,
                            a0=slabs(n, which)[0], prev=outs.get(n))
    for n in BIG:
        for k in range(4):
            results[k][n] = outs[n][k].reshape(_local_shape(n))

    g_small = _pack([grads[n].reshape(-1) for n in SMALL], 8)
    g_small = sum_over_devices(all_gather([g_small]).run("all_gather_small_grads")[0])
    g_small_full = dict(zip(SMALL, _unpack(g_small, [PARAMS[n][0] for n in SMALL])))
    g_small_loc = []
    for n in SMALL:
        if PARAMS[n][1] is None:
            g_small_loc.append(g_small_full[n])
        else:
            g_small_loc.append(lax.dynamic_index_in_dim(_by_destination(g_small_full[n], n), my_dev, axis=0, keepdims=False))
    small_shapes = [_local_shape(n) for n in SMALL]
    pack_small = lambda d: _pack([d[n].reshape(-1) for n in SMALL], 8)[None]
    small_out = adamw([_pack([gl.reshape(-1) for gl in g_small_loc], 8)[None]], pack_small(w_loc), pack_small(m_loc), pack_small(v_loc), name="adamw_small")
    for k in range(4):
        results[k].update(zip(SMALL, _unpack(small_out[k], small_shapes)))
    return (loss, dx[None], *[results[0][n] for n in NAMES], *[results[1][n] for n in NAMES],
            *[results[2][n] for n in NAMES], *[results[3][n] for n in NAMES])
```

```python
import functools

import jax
import jax.numpy as jnp
from jax import lax
from jax.experimental import pallas as pl
from jax.experimental.pallas import tpu as pltpu

f32 = jnp.float32
bf16 = jnp.bfloat16

D_MODEL = 1024
D_FF = 2816
DEPTH = 4
SSD_INNER = 2048
SSD_HEADS = 32
SSD_HEAD_DIM = 64
SSD_GROUPS = 4
SSD_STATE = 128
SSD_CONV_K = 4
SSD_CONV_DIM = 3072
SSD_IN_DIM = 5152
SSD_CHUNK = 128
SC_CONV_K = 3
RMS_EPS = 1e-5
N_DEV = 8
LANES = 128
HALO = 16
PACK_W = 1024
PACK_TILE = 256
VMEM_LIMIT = 56 * 1024 * 1024
NEG_BIG = -1e30

ADAM_LR = 0.001
ADAM_B1 = 0.9
ADAM_B2 = 0.999
ADAM_EPS = 1e-08
ADAM_WD = 0.01
ADAM_STEP = 10

NT_DIMS = (((1,), (1,)), ((), ()))
TN_DIMS = (((0,), (0,)), ((), ()))
MESH = pl.DeviceIdType.MESH


def _params(sem=None):
    return pltpu.CompilerParams(dimension_semantics=sem, vmem_limit_bytes=VMEM_LIMIT)


def _resident(shape):
    nd = len(shape)
    return pl.BlockSpec(tuple(shape), lambda *_: (0,) * nd, pipeline_mode=pl.Buffered(1))


def _rows(tm, width):
    return pl.BlockSpec((tm, width), lambda i: (i, 0))


def _my_core_and_chip():
    return lax.axis_index("c"), 2 * lax.axis_index("x") + lax.axis_index("y")


def _sigmoid(v):
    return 0.5 * jnp.tanh(0.5 * v) + 0.5


def _softplus(v):
    return jnp.maximum(v, 0.0) + jnp.log(1.0 + jnp.exp(-jnp.abs(v)))


def _rms_fwd(xv, w):
    inv = lax.rsqrt(jnp.mean(xv * xv, axis=-1, keepdims=True) + RMS_EPS)
    xh = xv * inv
    return xh * w, xh, inv


def _rms_bwd(dh, xh, inv, w):
    dxh = dh * w
    dx = inv * (dxh - xh * jnp.mean(dxh * xh, axis=-1, keepdims=True))
    return dx, jnp.sum(dh * xh, axis=0, keepdims=True)


def _mm(a, b):
    return jnp.dot(a, b, preferred_element_type=f32)


def _mm_nt(a, b):
    return lax.dot_general(a, b, NT_DIMS, preferred_element_type=f32)


def _mm_tn(a, b):
    return lax.dot_general(a, b, TN_DIMS, preferred_element_type=f32)


FFN_CHUNK = D_FF


def _ffn_chunks():
    return [(c0, min(FFN_CHUNK, D_FF - c0)) for c0 in range(0, D_FF, FFN_CHUNK)]


def _layer_slab(w, idx):
    tail = w.shape[len(idx):]
    return pl.BlockSpec((None,) * len(idx) + tuple(tail), lambda *_: tuple(idx) + (0,) * len(tail), pipeline_mode=pl.Buffered(1))


def ffn_fwd(x, nw, wg, wu, wd, idx, tm=512, rider=None):
    T = x.shape[0]
    nt = T // tm
    r_in, r_out, r_shapes, r_scratch, r_args = _rider_specs(rider)

    def body(x_ref, nw_ref, wg_ref, wu_ref, wd_ref, xo_ref, g_ref, u_ref, a_ref):
        xv = x_ref[...]
        h, _, _ = _rms_fwd(xv, nw_ref[...])
        hb = h.astype(bf16)
        y = None
        for c0, fc in _ffn_chunks():
            cols = pl.ds(c0, fc)
            g = _mm(hb, wg_ref[:, cols])
            u = _mm(hb, wu_ref[:, cols])
            ab = (g * _sigmoid(g) * u).astype(bf16)
            g_ref[:, cols] = g.astype(bf16)
            u_ref[:, cols] = u.astype(bf16)
            a_ref[:, cols] = ab
            part = _mm(ab, wd_ref[cols, :])
            y = part if y is None else y + part
        xo_ref[...] = xv + 0.5 * y

    hosted = _carry(body, 5, 4, rider, lambda: pl.program_id(0) == 0, lambda: pl.program_id(0) == nt - 1)
    return pl.pallas_call(
        hosted, name="ffn_fwd" if rider is None else "ffn_fwd_carrying", grid=(nt,),
        in_specs=[_rows(tm, D_MODEL), _resident((1, D_MODEL)), _layer_slab(wg, idx), _layer_slab(wu, idx), _layer_slab(wd, idx)] + r_in,
        out_specs=[_rows(tm, D_MODEL), _rows(tm, D_FF), _rows(tm, D_FF), _rows(tm, D_FF)] + r_out,
        out_shape=[jax.ShapeDtypeStruct((T, D_MODEL), f32)] + [jax.ShapeDtypeStruct((T, D_FF), bf16)] * 3 + r_shapes,
        scratch_shapes=r_scratch,
        compiler_params=_params(("parallel",) if rider is None else ("arbitrary",)),
    )(x, nw, wg, wu, wd, *r_args)


def ffn_bwd_dx(x, dxo, g, u, nw, wg, wu, wd, idx, tm=256, rider=None):
    T = x.shape[0]
    nt = T // tm
    r_in, r_out, r_shapes, r_scratch, r_args = _rider_specs(rider)

    def body(x_ref, dxo_ref, g_ref, u_ref, nw_ref, wg_ref, wu_ref, wd_ref, dx_ref, h_ref, dy_ref, dg_ref, du_ref, dnw_ref):
        w = nw_ref[...]
        h, xh, inv = _rms_fwd(x_ref[...], w)
        dxo_v = dxo_ref[...]
        dyb = (0.5 * dxo_v).astype(bf16)
        dh = None
        for c0, fc in _ffn_chunks():
            cols = pl.ds(c0, fc)
            da = _mm_nt(dyb, wd_ref[cols, :])
            gv = g_ref[:, cols].astype(f32)
            uv = u_ref[:, cols].astype(f32)
            s = _sigmoid(gv)
            dgb = (da * uv * (s * (1.0 + gv * (1.0 - s)))).astype(bf16)
            dub = (da * (gv * s)).astype(bf16)
            dg_ref[:, cols] = dgb
            du_ref[:, cols] = dub
            part = _mm_nt(dgb, wg_ref[:, cols]) + _mm_nt(dub, wu_ref[:, cols])
            dh = part if dh is None else dh + part
        dxn, dw = _rms_bwd(dh, xh, inv, w)
        dx_ref[...] = dxo_v + dxn
        h_ref[...] = h.astype(bf16)
        dy_ref[...] = dyb

        @pl.when(pl.program_id(0) == 0)
        def _():
            dnw_ref[...] = jnp.zeros_like(dnw_ref)

        dnw_ref[...] += dw

    hosted = _carry(body, 8, 6, rider, lambda: pl.program_id(0) == 0, lambda: pl.program_id(0) == nt - 1)
    return pl.pallas_call(
        hosted, name="ffn_bwd_dx" if rider is None else "ffn_bwd_dx_carrying", grid=(nt,),
        in_specs=[_rows(tm, D_MODEL), _rows(tm, D_MODEL), _rows(tm, D_FF), _rows(tm, D_FF), _resident((1, D_MODEL)),
                  _layer_slab(wg, idx), _layer_slab(wu, idx), _layer_slab(wd, idx)] + r_in,
        out_specs=[_rows(tm, D_MODEL), _rows(tm, D_MODEL), _rows(tm, D_MODEL), _rows(tm, D_FF), _rows(tm, D_FF),
                   pl.BlockSpec((1, D_MODEL), lambda i: (0, 0))] + r_out,
        out_shape=[jax.ShapeDtypeStruct((T, D_MODEL), f32), jax.ShapeDtypeStruct((T, D_MODEL), bf16), jax.ShapeDtypeStruct((T, D_MODEL), bf16),
                   jax.ShapeDtypeStruct((T, D_FF), bf16), jax.ShapeDtypeStruct((T, D_FF), bf16), jax.ShapeDtypeStruct((1, D_MODEL), f32)] + r_shapes,
        scratch_shapes=r_scratch,
        compiler_params=_params(("arbitrary",)),
    )(x, dxo, g, u, nw, wg, wu, wd, *r_args)


def tn_matmul(a, b, tk=1024):
    T, M = a.shape
    N = b.shape[1]
    bn = N if M * N <= 3_200_000 else N // 2
    nk = T // tk

    def body(a_ref, b_ref, o_ref):
        @pl.when(pl.program_id(1) == 0)
        def _():
            o_ref[...] = jnp.zeros_like(o_ref)

        o_ref[...] += _mm_tn(a_ref[...], b_ref[...])

    return pl.pallas_call(
        body, name=f"tn_matmul_{M}x{N}", grid=(N // bn, nk),
        in_specs=[pl.BlockSpec((tk, M), lambda j, k: (k, 0)), pl.BlockSpec((tk, bn), lambda j, k: (k, j))],
        out_specs=pl.BlockSpec((M, bn), lambda j, k: (0, j)),
        out_shape=jax.ShapeDtypeStruct((M, N), f32),
        compiler_params=_params(("parallel", "arbitrary")),
    )(a, b)


def tn_matmul_to_shards(a, b, buf, idx, axis):
    T, M = a.shape
    N = b.shape[1]
    m, n = buf.shape[-2:]
    tk = 1024 if M * N <= 2_200_000 else 512
    nk = T // tk
    fresh = isinstance(buf, jax.ShapeDtypeStruct)

    def body(a_ref, b_ref, *rest):
        o_ref, acc_ref = rest[-2:]
        k = pl.program_id(0)

        @pl.when(k == 0)
        def _():
            acc_ref[...] = jnp.zeros_like(acc_ref)

        acc_ref[...] += _mm_tn(a_ref[...], b_ref[...])

        @pl.when(k == nk - 1)
        def _():
            my_c, my_chip = _my_core_and_chip()
            for d in range(N_DEV):
                piece = acc_ref[:, pl.ds(d * n, n)] if axis == 1 else acc_ref[pl.ds(d * m, m), :]
                o_ref[(d % 2) ^ my_c, (d // 2) ^ my_chip] = piece

    none = (None,) * len(idx)
    return pl.pallas_call(
        body, name=f"tn_matmul_to_shards_{M}x{N}_{axis}", grid=(nk,),
        in_specs=[pl.BlockSpec((tk, M), lambda k: (k, 0)), pl.BlockSpec((tk, N), lambda k: (k, 0))] + ([] if fresh else [_ANY]),
        out_specs=pl.BlockSpec((2, 4) + none + (m, n), lambda k: (0, 0) + tuple(idx) + (0, 0)),
        out_shape=jax.ShapeDtypeStruct(buf.shape, f32),
        scratch_shapes=[pltpu.VMEM((M, N), f32)],
        input_output_aliases={} if fresh else {2: 0},
        compiler_params=_params(("arbitrary",)),
    )(a, b, *([] if fresh else [buf]))


def in_proj_fwd(x, nw, ws, out_dtypes, tm=512):
    T = x.shape[0]
    n = len(ws)

    def body(*refs):
        x_ref, nw_ref = refs[:2]
        w_refs = refs[2:2 + n]
        o_refs = refs[2 + n:]
        h, _, _ = _rms_fwd(x_ref[...], nw_ref[...])
        hb = h.astype(bf16)
        for w_ref, o_ref in zip(w_refs, o_refs):
            o_ref[...] = _mm(hb, w_ref[...]).astype(o_ref.dtype)

    return pl.pallas_call(
        body, name="in_proj_fwd_" + "_".join(str(w.shape[1]) for w in ws), grid=(T // tm,),
        in_specs=[_rows(tm, D_MODEL), _resident((1, D_MODEL))] + [_resident(w.shape) for w in ws],
        out_specs=[_rows(tm, w.shape[1]) for w in ws],
        out_shape=[jax.ShapeDtypeStruct((T, w.shape[1]), dt) for w, dt in zip(ws, out_dtypes)],
        compiler_params=_params(("parallel",)),
    )(x, nw, *ws)


def in_proj_bwd(x, nw, dxo, dys, ws, tm=512):
    T = x.shape[0]
    n = len(ws)

    def body(*refs):
        x_ref, nw_ref, dxo_ref = refs[:3]
        dy_refs = refs[3:3 + n]
        w_refs = refs[3 + n:3 + 2 * n]
        dx_ref, h_ref, dnw_ref = refs[3 + 2 * n:]
        w = nw_ref[...]
        h, xh, inv = _rms_fwd(x_ref[...], w)
        dh = _mm_nt(dy_refs[0][...], w_refs[0][...])
        for dy_ref, w_ref in zip(dy_refs[1:], w_refs[1:]):
            dh = dh + _mm_nt(dy_ref[...], w_ref[...])
        dxn, dw = _rms_bwd(dh, xh, inv, w)
        dx_ref[...] = dxo_ref[...] + dxn
        h_ref[...] = h.astype(bf16)

        @pl.when(pl.program_id(0) == 0)
        def _():
            dnw_ref[...] = jnp.zeros_like(dnw_ref)

        dnw_ref[...] += dw

    return pl.pallas_call(
        body, name="in_proj_bwd_" + "_".join(str(w.shape[1]) for w in ws), grid=(T // tm,),
        in_specs=[_rows(tm, D_MODEL), _resident((1, D_MODEL)), _rows(tm, D_MODEL)] + [_rows(tm, w.shape[1]) for w in ws]
        + [_resident(w.shape) for w in ws],
        out_specs=[_rows(tm, D_MODEL), _rows(tm, D_MODEL), pl.BlockSpec((1, D_MODEL), lambda i: (0, 0))],
        out_shape=[jax.ShapeDtypeStruct((T, D_MODEL), f32), jax.ShapeDtypeStruct((T, D_MODEL), bf16), jax.ShapeDtypeStruct((1, D_MODEL), f32)],
        compiler_params=_params(("arbitrary",)),
    )(x, nw, dxo, *dys, *ws)


def out_proj_fwd(x, a, w, tm=1024):
    T = x.shape[0]
    K = a.shape[1]

    def body(x_ref, a_ref, w_ref, o_ref):
        o_ref[...] = x_ref[...] + _mm(a_ref[...], w_ref[...])

    return pl.pallas_call(
        body, name=f"out_proj_fwd_{K}", grid=(T // tm,),
        in_specs=[_rows(tm, D_MODEL), _rows(tm, K), _resident(w.shape)],
        out_specs=_rows(tm, D_MODEL), out_shape=jax.ShapeDtypeStruct((T, D_MODEL), f32),
        compiler_params=_params(("parallel",)),
    )(x, a, w)


def _halo_spec(tm, width, n_tiles, reverse):
    per = tm // HALO

    def idx(i):
        t = (n_tiles - 1 - i) if reverse else i
        return (jnp.maximum(t * per - 1, 0), 0)

    return pl.BlockSpec((HALO, width), idx)


def _tile_spec(tm, width, n_tiles, reverse):
    if reverse:
        return pl.BlockSpec((tm, width), lambda i: (n_tiles - 1 - i, 0))
    return _rows(tm, width)


ROW_BLOCK = 64


def _strip(s):
    return pl.ds(pl.multiple_of(s * LANES, LANES), LANES)


def _conv_rows(ext_ref, w_ref, cols, k_w, r0):
    base = HALO - (k_w - 1) + r0
    wins = [ext_ref[pl.ds(base + k, ROW_BLOCK), :] for k in range(k_w)]
    out = w_ref[pl.ds(0, 1), cols] * wins[0]
    for k in range(1, k_w):
        out = out + w_ref[pl.ds(k, 1), cols] * wins[k]
    return out, wins


def _shifted_back(d_ref, w_ref, cols, k_w, r0):
    out = w_ref[pl.ds(0, 1), cols] * d_ref[pl.ds(r0 + k_w - 1, ROW_BLOCK), :]
    for k in range(1, k_w):
        out = out + w_ref[pl.ds(k, 1), cols] * d_ref[pl.ds(r0 + k_w - 1 - k, ROW_BLOCK), :]
    return out


def ssd_conv_fwd(xbc, conv_w, conv_b, dt_raw, dt_bias, tm=512):
    T = xbc.shape[0]
    nt = T // tm
    K = SSD_CONV_K

    def body(x_ref, halo_ref, w_ref, b_ref, dtr_ref, dtb_ref, act_ref, dt_ref, ext_ref):
        first = pl.program_id(0) == 0

        def strip(s, carry):
            cols = _strip(s)
            ext_ref[pl.ds(0, HALO), :] = jnp.where(first, 0.0, halo_ref[:, cols].astype(f32))
            ext_ref[pl.ds(HALO, tm), :] = x_ref[:, cols].astype(f32)
            for r0 in range(0, tm, ROW_BLOCK):
                pre, _ = _conv_rows(ext_ref, w_ref, cols, K, r0)
                pre = pre + b_ref[:, cols]
                act_ref[pl.ds(r0, ROW_BLOCK), cols] = (pre * _sigmoid(pre)).astype(bf16)
            return carry

        lax.fori_loop(0, SSD_CONV_DIM // LANES, strip, 0)
        dt = _softplus(dtr_ref[...] + dtb_ref[...])
        lane = lax.broadcasted_iota(jnp.int32, (1, LANES), 1)
        for g in range(SSD_GROUPS):
            dt_ref[g] = jnp.where(lane < 8, dt if g == 0 else pltpu.roll(dt, LANES - 8 * g, axis=1), 0.0)

    return pl.pallas_call(
        body, name="ssd_conv_fwd", grid=(nt,),
        in_specs=[_rows(tm, SSD_CONV_DIM), _halo_spec(tm, SSD_CONV_DIM, nt, False), _resident(conv_w.shape), _resident(conv_b.shape),
                  _rows(tm, LANES), _resident(dt_bias.shape)],
        out_specs=[_rows(tm, SSD_CONV_DIM), pl.BlockSpec((SSD_GROUPS, tm, LANES), lambda i: (0, i, 0))],
        out_shape=[jax.ShapeDtypeStruct((T, SSD_CONV_DIM), bf16), jax.ShapeDtypeStruct((SSD_GROUPS, T, LANES), f32)],
        scratch_shapes=[pltpu.VMEM((tm + HALO, LANES), f32)],
        compiler_params=_params(("parallel",)),
    )(xbc, xbc, conv_w, conv_b, dt_raw, dt_bias)


def ssd_conv_bwd(xbc, conv_w, conv_b, dt_raw, dt_bias, dxs_a, dxs_b, db, dc, ddt, tm=512):
    T = xbc.shape[0]
    nt = T // tm
    K = SSD_CONV_K

    def body(x_ref, halo_ref, w_ref, b_ref, dtr_ref, dtb_ref, da_ref, dbb_ref, db_ref, dc_ref, ddt_ref,
             dx_ref, ddtr_ref, dw_ref, dbias_ref, ddtb_ref, ext_ref, dpre_ref, carry_ref):
        i = pl.program_id(0)

        @pl.when(i == 0)
        def _():
            carry_ref[...] = jnp.zeros_like(carry_ref)
            dw_ref[...] = jnp.zeros_like(dw_ref)
            dbias_ref[...] = jnp.zeros_like(dbias_ref)
            ddtb_ref[...] = jnp.zeros_like(ddtb_ref)

        first_tile = i == nt - 1

        def run_strips(lo, hi, load_dact):
            def strip(s, carry):
                cols = _strip(s)
                ext_ref[pl.ds(0, HALO), :] = jnp.where(first_tile, 0.0, halo_ref[:, cols].astype(f32))
                ext_ref[pl.ds(HALO, tm), :] = x_ref[:, cols].astype(f32)
                dpre_ref[pl.ds(tm, 8), :] = carry_ref[:, cols]
                bias = b_ref[:, cols]
                dws = [jnp.zeros((1, LANES), f32) for _ in range(K)]
                dbs = jnp.zeros((1, LANES), f32)
                for r0 in range(0, tm, ROW_BLOCK):
                    pre, wins = _conv_rows(ext_ref, w_ref, cols, K, r0)
                    pre = pre + bias
                    sg = _sigmoid(pre)
                    dpre = load_dact(s, r0) * (sg * (1.0 + pre * (1.0 - sg)))
                    dpre_ref[pl.ds(r0, ROW_BLOCK), :] = dpre
                    dbs = dbs + jnp.sum(dpre, axis=0, keepdims=True)
                    for k in range(K):
                        dws[k] = dws[k] + jnp.sum(dpre * wins[k], axis=0, keepdims=True)
                carry_ref[:, cols] = dpre_ref[pl.ds(0, 8), :]
                for r0 in range(0, tm, ROW_BLOCK):
                    dx_ref[pl.ds(r0, ROW_BLOCK), cols] = _shifted_back(dpre_ref, w_ref, cols, K, r0).astype(bf16)
                for k in range(K):
                    dw_ref[pl.ds(k, 1), cols] += dws[k]
                dbias_ref[:, cols] += dbs
                return carry

            lax.fori_loop(lo, hi, strip, 0)

        rows = lambda r0: pl.ds(r0, ROW_BLOCK)
        n_x = SSD_INNER // LANES
        n_g = SSD_GROUPS * SSD_STATE // LANES
        run_strips(0, n_x, lambda s, r0: da_ref[rows(r0), _strip(s)].astype(f32) + dbb_ref[rows(r0), _strip(s)].astype(f32))
        run_strips(n_x, n_x + n_g, lambda s, r0: db_ref[rows(r0), _strip(s - n_x)].astype(f32))
        run_strips(n_x + n_g, n_x + 2 * n_g, lambda s, r0: dc_ref[rows(r0), _strip(s - n_x - n_g)].astype(f32))
        lane = lax.broadcasted_iota(jnp.int32, (1, LANES), 1)
        ddt = jnp.where(lane < 8, ddt_ref[0], 0.0)
        for g in range(1, SSD_GROUPS):
            ddt = ddt + pltpu.roll(jnp.where(lane < 8, ddt_ref[g], 0.0), 8 * g, axis=1)
        ddtr = ddt * _sigmoid(dtr_ref[...] + dtb_ref[...])
        ddtr_ref[...] = ddtr.astype(bf16)
        ddtb_ref[...] += jnp.sum(ddtr, axis=0, keepdims=True)

    rev = functools.partial(_tile_spec, tm, n_tiles=nt, reverse=True)
    const = lambda shape: pl.BlockSpec(shape, lambda i: (0, 0))
    return pl.pallas_call(
        body, name="ssd_conv_bwd", grid=(nt,),
        in_specs=[rev(width=SSD_CONV_DIM), _halo_spec(tm, SSD_CONV_DIM, nt, True), _resident(conv_w.shape), _resident(conv_b.shape),
                  rev(width=LANES), _resident(dt_bias.shape), rev(width=SSD_INNER), rev(width=SSD_INNER),
                  rev(width=SSD_GROUPS * SSD_STATE), rev(width=SSD_GROUPS * SSD_STATE),
                  pl.BlockSpec((SSD_GROUPS, tm, LANES), lambda i: (0, nt - 1 - i, 0))],
        out_specs=[rev(width=SSD_CONV_DIM), rev(width=LANES), const((8, SSD_CONV_DIM)), const((1, SSD_CONV_DIM)), const((1, LANES))],
        out_shape=[jax.ShapeDtypeStruct((T, SSD_CONV_DIM), bf16), jax.ShapeDtypeStruct((T, LANES), bf16),
                   jax.ShapeDtypeStruct((8, SSD_CONV_DIM), f32), jax.ShapeDtypeStruct((1, SSD_CONV_DIM), f32), jax.ShapeDtypeStruct((1, LANES), f32)],
        scratch_shapes=[pltpu.VMEM((tm + HALO, LANES), f32), pltpu.VMEM((tm + 8, LANES), f32), pltpu.VMEM((8, SSD_CONV_DIM), f32)],
        compiler_params=_params(("arbitrary",)),
    )(xbc, xbc, conv_w, conv_b, dt_raw, dt_bias, dxs_a, dxs_b, db, dc, ddt)


def _ssd_chunk(xs, bm, cm, dt, alog, st):
    L = SSD_CHUNK
    row = lax.broadcasted_iota(jnp.int32, (L, L), 0)
    col = lax.broadcasted_iota(jnp.int32, (L, L), 1)
    causal = row >= col
    tril = jnp.where(causal, 1.0, 0.0).astype(f32)
    lane = lax.broadcasted_iota(jnp.int32, (1, LANES), 1)
    sub = lax.broadcasted_iota(jnp.int32, (LANES, 1), 0)
    lo = lane < SSD_HEAD_DIM
    last_row = sub == L - 1

    dta = dt * (-jnp.exp(alog))
    a_cs = jnp.dot(tril, dta, precision=lax.Precision.HIGHEST, preferred_element_type=f32)
    a_cs_t = a_cs.T
    bmb = bm.astype(bf16)
    cmb = cm.astype(bf16)
    cb = _mm_nt(cmb, bmb)
    c_st = _mm(cmb, st.astype(bf16))

    def head_col(v, e):
        return jnp.sum(jnp.where(lane == e, v, 0.0), axis=1, keepdims=True)

    def head_row(v, e):
        return jnp.sum(jnp.where(sub == e, v, 0.0), axis=0, keepdims=True)

    ys, sts = [], []
    for j in range(4):
        e0, e1 = 2 * j, 2 * j + 1
        c0, c1 = head_col(a_cs, e0), head_col(a_cs, e1)
        acs_x = jnp.where(lo, c0, c1)
        dt_x = jnp.where(lo, head_col(dt, e0), head_col(dt, e1))
        xd = xs[:, j * LANES:(j + 1) * LANES] * dt_x
        m0 = cb * jnp.exp(jnp.where(causal, c0 - head_row(a_cs_t, e0), NEG_BIG))
        m1 = cb * jnp.exp(jnp.where(causal, c1 - head_row(a_cs_t, e1), NEG_BIG))
        mcat = jnp.concatenate([m0, m1], axis=1).astype(bf16)
        xcat = jnp.concatenate([jnp.where(lo, xd, 0.0), jnp.where(lo, 0.0, xd)], axis=0).astype(bf16)
        y_diag = _mm(mcat, xcat)
        a_last = jnp.sum(jnp.where(last_row, acs_x, 0.0), axis=0, keepdims=True)
        x_dec = (xd * jnp.exp(a_last - acs_x)).astype(bf16)
        s_new = _mm_tn(bmb, x_dec)
        y_off = c_st[:, j * LANES:(j + 1) * LANES] * jnp.exp(acs_x)
        ys.append(y_diag + y_off)
        sts.append(jnp.exp(a_last) * st[:, j * LANES:(j + 1) * LANES] + s_new)
    return jnp.concatenate(ys, axis=1), jnp.concatenate(sts, axis=1)


SCAN_GROUPS_FWD = 4
SCAN_GROUPS_BWD = 1


def _scan_specs(nc, reverse, gs):
    L = SSD_CHUNK
    ch = (lambda c: nc - 1 - c) if reverse else (lambda c: c)
    gw = SSD_INNER // SSD_GROUPS
    b0 = SSD_INNER // (gs * SSD_STATE)
    c0 = (SSD_INNER + SSD_GROUPS * SSD_STATE) // (gs * SSD_STATE)
    xs = pl.BlockSpec((L, gs * gw), lambda g, c: (ch(c), g))
    bm = pl.BlockSpec((L, gs * SSD_STATE), lambda g, c: (ch(c), b0 + g))
    cm = pl.BlockSpec((L, gs * SSD_STATE), lambda g, c: (ch(c), c0 + g))
    dt = pl.BlockSpec((gs, L, LANES), lambda g, c: (g, ch(c), 0))
    alog = pl.BlockSpec((gs, 1, LANES), lambda g, c: (g, 0, 0))
    st = pl.BlockSpec((gs, None, SSD_STATE, gw), lambda g, c: (g, ch(c), 0, 0))
    y = pl.BlockSpec((L, gs * gw), lambda g, c: (ch(c), g))
    grp = pl.BlockSpec((L, gs * SSD_STATE), lambda g, c: (ch(c), g))
    return xs, bm, cm, dt, alog, st, y, grp


def ssd_scan_fwd(act, dt4, alog4, rider=None):
    T = act.shape[0]
    nc = T // SSD_CHUNK
    gs = SCAN_GROUPS_FWD
    ng = SSD_GROUPS // gs
    gw = SSD_INNER // SSD_GROUPS
    xs_s, bm_s, cm_s, dt_s, alog_s, st_s, y_s, _ = _scan_specs(nc, False, gs)
    r_in, r_out, r_shapes, r_scratch, r_args = _rider_specs(rider)

    def body(xs_ref, bm_ref, cm_ref, dt_ref, alog_ref, y_ref, st_ref, st_scr):
        @pl.when(pl.program_id(1) == 0)
        def _():
            st_scr[...] = jnp.zeros_like(st_scr)

        for q in range(gs):
            xc, gc = pl.ds(q * gw, gw), pl.ds(q * SSD_STATE, SSD_STATE)
            st = st_scr[q]
            st_ref[q] = st
            y, st_new = _ssd_chunk(xs_ref[:, xc].astype(f32), bm_ref[:, gc].astype(f32), cm_ref[:, gc].astype(f32), dt_ref[q], alog_ref[q], st)
            y_ref[:, xc] = y.astype(bf16)
            st_scr[q] = st_new

    first = lambda: jnp.logical_and(pl.program_id(0) == 0, pl.program_id(1) == 0)
    last = lambda: jnp.logical_and(pl.program_id(0) == ng - 1, pl.program_id(1) == nc - 1)
    return pl.pallas_call(
        _carry(body, 5, 2, rider, first, last), name="ssd_scan_fwd" if rider is None else "ssd_scan_fwd_carrying", grid=(ng, nc),
        in_specs=[xs_s, bm_s, cm_s, dt_s, alog_s] + r_in, out_specs=[y_s, st_s] + r_out,
        out_shape=[jax.ShapeDtypeStruct((T, SSD_INNER), bf16), jax.ShapeDtypeStruct((SSD_GROUPS, nc, SSD_STATE, gw), f32)] + r_shapes,
        scratch_shapes=[pltpu.VMEM((gs, SSD_STATE, gw), f32)] + r_scratch,
        compiler_params=_params(("parallel" if rider is None else "arbitrary", "arbitrary")),
    )(act, act, act, dt4, alog4, *r_args)


def ssd_scan_bwd(act, dt4, alog4, states, dy, rider=None):
    T = act.shape[0]
    nc = T // SSD_CHUNK
    gs = SCAN_GROUPS_BWD
    ng = SSD_GROUPS // gs
    gw = SSD_INNER // SSD_GROUPS
    xs_s, bm_s, cm_s, dt_s, alog_s, st_s, y_s, grp_s = _scan_specs(nc, True, gs)
    r_in, r_out, r_shapes, r_scratch, r_args = _rider_specs(rider)

    def body(xs_ref, bm_ref, cm_ref, dt_ref, alog_ref, st_ref, dy_ref, dxs_ref, db_ref, dc_ref, ddt_ref, dalog_ref, dst_scr):
        @pl.when(pl.program_id(1) == 0)
        def _():
            dst_scr[...] = jnp.zeros_like(dst_scr)
            dalog_ref[...] = jnp.zeros_like(dalog_ref)

        for q in range(gs):
            xc, gc = pl.ds(q * gw, gw), pl.ds(q * SSD_STATE, SSD_STATE)
            _, vjp = jax.vjp(_ssd_chunk, xs_ref[:, xc].astype(f32), bm_ref[:, gc].astype(f32), cm_ref[:, gc].astype(f32),
                             dt_ref[q], alog_ref[q], st_ref[q])
            dxs, dbm, dcm, ddt, dalog, dst = vjp((dy_ref[:, xc].astype(f32), dst_scr[q]))
            dxs_ref[:, xc] = dxs.astype(bf16)
            db_ref[:, gc] = dbm.astype(bf16)
            dc_ref[:, gc] = dcm.astype(bf16)
            ddt_ref[q] = ddt
            dalog_ref[q] += dalog
            dst_scr[q] = dst

    first = lambda: jnp.logical_and(pl.program_id(0) == 0, pl.program_id(1) == 0)
    last = lambda: jnp.logical_and(pl.program_id(0) == ng - 1, pl.program_id(1) == nc - 1)
    return pl.pallas_call(
        _carry(body, 7, 5, rider, first, last), name="ssd_scan_bwd" if rider is None else "ssd_scan_bwd_carrying", grid=(ng, nc),
        in_specs=[xs_s, bm_s, cm_s, dt_s, alog_s, st_s, y_s] + r_in,
        out_specs=[y_s, grp_s, grp_s, dt_s, alog_s] + r_out,
        out_shape=[jax.ShapeDtypeStruct((T, SSD_INNER), bf16), jax.ShapeDtypeStruct((T, SSD_GROUPS * SSD_STATE), bf16),
                   jax.ShapeDtypeStruct((T, SSD_GROUPS * SSD_STATE), bf16), jax.ShapeDtypeStruct((SSD_GROUPS, T, LANES), f32),
                   jax.ShapeDtypeStruct((SSD_GROUPS, 1, LANES), f32)] + r_shapes,
        scratch_shapes=[pltpu.VMEM((gs, SSD_STATE, gw), f32)] + r_scratch,
        compiler_params=_params(("parallel" if rider is None else "arbitrary", "arbitrary")),
    )(act, act, act, dt4, alog4, states, dy, *r_args)


GATE_ROWS = 256


def _ssd_gate(y, xs, z, d_x, nw):
    g = (y + xs * d_x) * (z * _sigmoid(z))
    return g * lax.rsqrt(jnp.mean(g * g, axis=-1, keepdims=True) + RMS_EPS) * nw


def _gate_blocks(tm, fn):
    gw = SSD_INNER // SSD_GROUPS

    def block(r, carry):
        rows = pl.ds(r * GATE_ROWS if isinstance(r, int) else pl.multiple_of(r * GATE_ROWS, GATE_ROWS), GATE_ROWS)
        for k in range(SSD_GROUPS):
            fn(rows, pl.ds(k * gw, gw))
        return carry

    if tm == GATE_ROWS:
        block(0, 0)
    else:
        lax.fori_loop(0, tm // GATE_ROWS, block, 0)


def ssd_gate_fwd(y, act, z, d_x, nw, tm=512):
    T = y.shape[0]

    def body(y_ref, xs_ref, z_ref, d_ref, nw_ref, o_ref):
        def one(rows, cols):
            o_ref[rows, cols] = _ssd_gate(y_ref[rows, cols].astype(f32), xs_ref[rows, cols].astype(f32), z_ref[rows, cols].astype(f32),
                                          d_ref[:, cols], nw_ref[:, cols]).astype(bf16)

        _gate_blocks(tm, one)

    return pl.pallas_call(
        body, name="ssd_gate_fwd", grid=(T // tm,),
        in_specs=[_rows(tm, SSD_INNER), _rows(tm, SSD_INNER), _rows(tm, SSD_INNER), _resident(d_x.shape), _resident(nw.shape)],
        out_specs=_rows(tm, SSD_INNER), out_shape=jax.ShapeDtypeStruct((T, SSD_INNER), bf16),
        compiler_params=_params(("parallel",)),
    )(y, act, z, d_x, nw)


def ssd_gate_bwd(y, act, z, d_x, nw, dxo, w_out, tm=512):
    T = y.shape[0]

    def body(y_ref, xs_ref, z_ref, d_ref, nw_ref, dxo_ref, w_ref, dy_ref, dxs_ref, dz_ref, dd_ref, dnw_ref, dyb_ref):
        @pl.when(pl.program_id(0) == 0)
        def _():
            dd_ref[...] = jnp.zeros_like(dd_ref)
            dnw_ref[...] = jnp.zeros_like(dnw_ref)

        dyb_ref[...] = dxo_ref[...].astype(bf16)

        def one(rows, cols):
            _, vjp = jax.vjp(_ssd_gate, y_ref[rows, cols].astype(f32), xs_ref[rows, cols].astype(f32), z_ref[rows, cols].astype(f32),
                             d_ref[:, cols], nw_ref[:, cols])
            dy, dxs, dz, dd, dnw = vjp(_mm_nt(dyb_ref[rows, :], w_ref[cols, :]))
            dy_ref[rows, cols] = dy.astype(bf16)
            dxs_ref[rows, cols] = dxs.astype(bf16)
            dz_ref[rows, cols] = dz.astype(bf16)
            dd_ref[:, cols] += dd
            dnw_ref[:, cols] += dnw

        _gate_blocks(tm, one)

    const = pl.BlockSpec((1, SSD_INNER), lambda i: (0, 0))
    return pl.pallas_call(
        body, name="ssd_gate_bwd", grid=(T // tm,),
        in_specs=[_rows(tm, SSD_INNER), _rows(tm, SSD_INNER), _rows(tm, SSD_INNER), _resident(d_x.shape), _resident(nw.shape),
                  _rows(tm, D_MODEL), _resident(w_out.shape)],
        out_specs=[_rows(tm, SSD_INNER)] * 3 + [const, const, _rows(tm, D_MODEL)],
        out_shape=[jax.ShapeDtypeStruct((T, SSD_INNER), bf16)] * 3 + [jax.ShapeDtypeStruct((1, SSD_INNER), f32)] * 2
        + [jax.ShapeDtypeStruct((T, D_MODEL), bf16)],
        compiler_params=_params(("arbitrary",)),
    )(y, act, z, d_x, nw, dxo, w_out)


def sc_mid_fwd(bcu, conv_w, tm=512):
    T = bcu.shape[0]
    nt = T // tm
    Dm = D_MODEL

    def body(x_ref, halo_ref, w_ref, q_ref, ext_ref):
        first = pl.program_id(0) == 0
        n_s = Dm // LANES

        def strip(s, carry):
            cols, c_cols, u_cols = _strip(s), _strip(s + n_s), _strip(s + 2 * n_s)
            ext_ref[pl.ds(0, HALO), :] = jnp.where(first, 0.0, halo_ref[:, c_cols].astype(f32) * halo_ref[:, u_cols].astype(f32))
            ext_ref[pl.ds(HALO, tm), :] = x_ref[:, c_cols].astype(f32) * x_ref[:, u_cols].astype(f32)
            for r0 in range(0, tm, ROW_BLOCK):
                rows = pl.ds(r0, ROW_BLOCK)
                v, _ = _conv_rows(ext_ref, w_ref, cols, SC_CONV_K, r0)
                q_ref[rows, cols] = (x_ref[rows, cols].astype(f32) * v).astype(bf16)
            return carry

        lax.fori_loop(0, n_s, strip, 0)

    return pl.pallas_call(
        body, name="sc_mid_fwd", grid=(nt,),
        in_specs=[_rows(tm, 3 * Dm), _halo_spec(tm, 3 * Dm, nt, False), _resident(conv_w.shape)],
        out_specs=_rows(tm, Dm), out_shape=jax.ShapeDtypeStruct((T, Dm), bf16),
        scratch_shapes=[pltpu.VMEM((tm + HALO, LANES), f32)],
        compiler_params=_params(("parallel",)),
    )(bcu, bcu, conv_w)


def sc_mid_bwd(bcu, conv_w, dxo, w_out, tm=512):
    T = bcu.shape[0]
    nt = T // tm
    Dm = D_MODEL
    K = SC_CONV_K

    def body(x_ref, halo_ref, w_ref, dxo_ref, wo_ref, dx_ref, dw_ref, dyb_ref, ext_ref, dv_ref, carry_ref, dq_ref):
        i = pl.program_id(0)

        @pl.when(i == 0)
        def _():
            carry_ref[...] = jnp.zeros_like(carry_ref)
            dw_ref[...] = jnp.zeros_like(dw_ref)

        dyb = dxo_ref[...].astype(bf16)
        dyb_ref[...] = dyb
        dq_ref[...] = _mm_nt(dyb, wo_ref[...])
        first_tile = i == nt - 1
        n_s = Dm // LANES

        def strip(s, carry):
            cols, c_cols, u_cols = _strip(s), _strip(s + n_s), _strip(s + 2 * n_s)
            ext_ref[pl.ds(0, HALO), :] = jnp.where(first_tile, 0.0, halo_ref[:, c_cols].astype(f32) * halo_ref[:, u_cols].astype(f32))
            ext_ref[pl.ds(HALO, tm), :] = x_ref[:, c_cols].astype(f32) * x_ref[:, u_cols].astype(f32)
            dv_ref[pl.ds(tm, 8), :] = carry_ref[:, cols]
            dws = [jnp.zeros((1, LANES), f32) for _ in range(K)]
            for r0 in range(0, tm, ROW_BLOCK):
                rows = pl.ds(r0, ROW_BLOCK)
                v, wins = _conv_rows(ext_ref, w_ref, cols, K, r0)
                dqv = dq_ref[rows, cols]
                dv = dqv * x_ref[rows, cols].astype(f32)
                dv_ref[rows, :] = dv
                dx_ref[rows, cols] = (dqv * v).astype(bf16)
                for k in range(K):
                    dws[k] = dws[k] + jnp.sum(dv * wins[k], axis=0, keepdims=True)
            carry_ref[:, cols] = dv_ref[pl.ds(0, 8), :]
            for r0 in range(0, tm, ROW_BLOCK):
                rows = pl.ds(r0, ROW_BLOCK)
                dp = _shifted_back(dv_ref, w_ref, cols, K, r0)
                dx_ref[rows, c_cols] = (dp * x_ref[rows, u_cols].astype(f32)).astype(bf16)
                dx_ref[rows, u_cols] = (dp * x_ref[rows, c_cols].astype(f32)).astype(bf16)
            for k in range(K):
                dw_ref[pl.ds(k, 1), cols] += dws[k]
            return carry

        lax.fori_loop(0, n_s, strip, 0)

    return pl.pallas_call(
        body, name="sc_mid_bwd", grid=(nt,),
        in_specs=[_tile_spec(tm, 3 * Dm, nt, True), _halo_spec(tm, 3 * Dm, nt, True), _resident(conv_w.shape), _tile_spec(tm, Dm, nt, True),
                  _resident(w_out.shape)],
        out_specs=[_tile_spec(tm, 3 * Dm, nt, True), pl.BlockSpec((8, Dm), lambda i: (0, 0)), _tile_spec(tm, Dm, nt, True)],
        out_shape=[jax.ShapeDtypeStruct((T, 3 * Dm), bf16), jax.ShapeDtypeStruct((8, Dm), f32), jax.ShapeDtypeStruct((T, Dm), bf16)],
        scratch_shapes=[pltpu.VMEM((tm + HALO, LANES), f32), pltpu.VMEM((tm + 8, LANES), f32), pltpu.VMEM((8, Dm), f32),
                        pltpu.VMEM((tm, Dm), f32)],
        compiler_params=_params(("arbitrary",)),
    )(bcu, bcu, conv_w, dxo, w_out)


def loss_head(x, fw, target, tm=1024):
    T = x.shape[0]

    def body(x_ref, fw_ref, t_ref, loss_ref, dx_ref, dfw_ref):
        @pl.when(pl.program_id(0) == 0)
        def _():
            loss_ref[...] = jnp.zeros_like(loss_ref)
            dfw_ref[...] = jnp.zeros_like(dfw_ref)

        w = fw_ref[...]
        y, xh, inv = _rms_fwd(x_ref[...], w)
        err = y - t_ref[...]
        loss_ref[...] += 0.5 * jnp.sum(jnp.mean(err * err, axis=-1, keepdims=True), axis=0, keepdims=True)
        dx, dw = _rms_bwd(err * (1.0 / D_MODEL), xh, inv, w)
        dx_ref[...] = dx
        dfw_ref[...] += dw

    return pl.pallas_call(
        body, name="loss_head", grid=(T // tm,),
        in_specs=[_rows(tm, D_MODEL), _resident((1, D_MODEL)), _rows(tm, D_MODEL)],
        out_specs=[pl.BlockSpec((1, LANES), lambda i: (0, 0)), _rows(tm, D_MODEL), pl.BlockSpec((1, D_MODEL), lambda i: (0, 0))],
        out_shape=[jax.ShapeDtypeStruct((1, LANES), f32), jax.ShapeDtypeStruct((T, D_MODEL), f32), jax.ShapeDtypeStruct((1, D_MODEL), f32)],
        compiler_params=_params(("arbitrary",)),
    )(x, fw, target)


ELEMENTWISE_TILE_BYTES = 1_600_000


def _row_tile(rows, width):
    row_bytes = 4 * _round_up(width, LANES)
    tile = rows
    while tile * row_bytes > ELEMENTWISE_TILE_BYTES and tile % 16 == 0:
        tile //= 2
    return tile


def adamw(g_parts, w, m, v, name="adamw", a0=0, prev=None):
    A, B, n = w.shape
    tb = _row_tile(B, n)
    n_parts = len(g_parts)
    arrays, specs = [], []
    for part in g_parts:
        lead, arr = part if isinstance(part, tuple) else ((), part)
        specs.append(pl.BlockSpec((None,) * (len(lead) + 1) + (tb, n), lambda a, t, lead=lead: tuple(lead) + (a, t, 0)))
        arrays.append(arr)
    na = arrays[0].shape[-3]
    prev = list(prev) if prev is not None else []

    def body(*refs):
        n = n_parts
        g_refs = refs[:n]
        w_ref, m_ref, v_ref = refs[n:n + 3]
        go_ref, d_ref, mo_ref, vo_ref = refs[n + 3 + len(prev):]
        g = g_refs[0][...].astype(f32)
        for r in g_refs[1:]:
            g = g + r[...].astype(f32)
        m_new = ADAM_B1 * m_ref[...] + (1.0 - ADAM_B1) * g
        v_new = ADAM_B2 * v_ref[...] + (1.0 - ADAM_B2) * (g * g)
        m_hat = m_new / (1.0 - ADAM_B1 ** ADAM_STEP)
        v_hat = v_new / (1.0 - ADAM_B2 ** ADAM_STEP)
        go_ref[...] = g
        d_ref[...] = -ADAM_LR * (m_hat / (jnp.sqrt(v_hat) + ADAM_EPS) + ADAM_WD * w_ref[...])
        mo_ref[...] = m_new
        vo_ref[...] = v_new

    plain = pl.BlockSpec((None, tb, n), lambda a, t: (a + a0, t, 0))
    return pl.pallas_call(
        body, name=name, grid=(na, B // tb), in_specs=specs + [plain] * 3 + [_ANY] * len(prev), out_specs=[plain] * 4,
        out_shape=[jax.ShapeDtypeStruct((A, B, n), f32)] * 4,
        input_output_aliases={n_parts + 3 + k: k for k in range(len(prev))},
        compiler_params=_params(("parallel", "parallel")),
    )(*arrays, w, m, v, *prev)


def pair_sum_bf16(ga, gb, name):
    _, A, B, n = gb.shape
    tb = _row_tile(B, n)

    def body(a_ref, b_ref, o_ref):
        o_ref[...] = (a_ref[...] + b_ref[...]).astype(bf16)

    return pl.pallas_call(
        body, name=name, grid=(3, A, B // tb),
        in_specs=[pl.BlockSpec((None, None, None, tb, n), lambda j, a, t: (0, j + 1, a, t, 0)),
                  pl.BlockSpec((None, None, tb, n), lambda j, a, t: (j + 1, a, t, 0))],
        out_specs=pl.BlockSpec((None, None, tb, n), lambda j, a, t: (j + 1, a, t, 0)),
        out_shape=jax.ShapeDtypeStruct((4, A, B, n), bf16),
        compiler_params=_params(("parallel", "parallel", "parallel")),
    )(ga, gb)


def assemble(gathered, axis, tk=256):
    _, A, K, n = gathered.shape
    if axis == 1:
        def body(w_ref, o_ref):
            o_ref[...] = jnp.concatenate([w_ref[j] for j in range(N_DEV)], axis=1)

        return pl.pallas_call(
            body, name=f"assemble_cols_{K}x{n}", grid=(A, K // tk),
            in_specs=[pl.BlockSpec((N_DEV, None, tk, n), lambda a, t: (0, a, t, 0))],
            out_specs=pl.BlockSpec((None, tk, N_DEV * n), lambda a, t: (a, t, 0)),
            out_shape=jax.ShapeDtypeStruct((A, K, N_DEV * n), gathered.dtype),
            compiler_params=_params(("parallel", "parallel")),
        )(gathered)

    def body(w_ref, o_ref):
        for j in range(N_DEV):
            o_ref[pl.ds(j * K, K), :] = w_ref[j]

    return pl.pallas_call(
        body, name=f"assemble_rows_{K}x{n}", grid=(A,),
        in_specs=[pl.BlockSpec((N_DEV, None, K, n), lambda a: (0, a, 0, 0))],
        out_specs=pl.BlockSpec((None, N_DEV * K, n), lambda a: (a, 0, 0)),
        out_shape=jax.ShapeDtypeStruct((A, N_DEV * K, n), gathered.dtype),
        compiler_params=_params(("parallel",)),
    )(gathered)


SSD_IN_PAD = 5248


def assemble_ssd_in(gathered, tk=256):
    _, A, K, n = gathered.shape

    def body(w_ref, z_ref, x_ref, dt_ref, full_ref):
        full_ref[:, pl.ds(SSD_IN_PAD - LANES, LANES)] = jnp.zeros((tk, LANES), gathered.dtype)
        for j in range(N_DEV):
            full_ref[:, pl.ds(j * n, n)] = w_ref[j]
        z_ref[...] = full_ref[:, pl.ds(0, SSD_INNER)]
        x_ref[...] = full_ref[:, pl.ds(SSD_INNER, SSD_CONV_DIM)]
        dt_ref[...] = full_ref[:, pl.ds(SSD_INNER + SSD_CONV_DIM, LANES)]

    widths = (SSD_INNER, SSD_CONV_DIM, LANES)
    return pl.pallas_call(
        body, name="assemble_ssd_in", grid=(A, K // tk),
        in_specs=[pl.BlockSpec((N_DEV, None, tk, n), lambda a, t: (0, a, t, 0))],
        out_specs=[pl.BlockSpec((None, tk, w), lambda a, t: (a, t, 0)) for w in widths],
        out_shape=[jax.ShapeDtypeStruct((A, K, w), gathered.dtype) for w in widths],
        scratch_shapes=[pltpu.VMEM((tk, SSD_IN_PAD), gathered.dtype)],
        compiler_params=_params(("parallel", "parallel")),
    )(gathered)


def ssd_in_to_shards(dwz, dwx, dwdt, buf, j, tk=256):
    K = dwz.shape[0]
    n = buf.shape[-1]
    fresh = isinstance(buf, jax.ShapeDtypeStruct)

    def body(z_ref, x_ref, dt_ref, *rest):
        o_ref, full_ref = rest[-2:]
        full_ref[:, pl.ds(0, SSD_INNER)] = z_ref[...]
        full_ref[:, pl.ds(SSD_INNER, SSD_CONV_DIM)] = x_ref[...]
        full_ref[:, pl.ds(SSD_INNER + SSD_CONV_DIM, LANES)] = dt_ref[...]
        my_c, my_chip = _my_core_and_chip()
        for d in range(N_DEV):
            o_ref[(d % 2) ^ my_c, (d // 2) ^ my_chip] = full_ref[:, pl.ds(d * n, n)]

    return pl.pallas_call(
        body, name="ssd_in_to_shards", grid=(K // tk,),
        in_specs=[_rows(tk, SSD_INNER), _rows(tk, SSD_CONV_DIM), _rows(tk, LANES)] + ([] if fresh else [_ANY]),
        out_specs=pl.BlockSpec((2, 4, None, tk, n), lambda t: (0, 0, j, t, 0)),
        out_shape=jax.ShapeDtypeStruct(buf.shape, f32),
        scratch_shapes=[pltpu.VMEM((tk, SSD_IN_PAD), f32)],
        input_output_aliases={} if fresh else {3: 0},
        compiler_params=_params(("parallel",)),
    )(dwz, dwx, dwdt, *([] if fresh else [buf]))


def sum_over_devices(gathered):
    _, R, W = gathered.shape

    def body(g_ref, o_ref):
        acc = g_ref[0]
        for k in range(1, N_DEV):
            acc = acc + g_ref[k]
        o_ref[...] = acc

    return pl.pallas_call(
        body, name="sum_over_devices", grid=(1,),
        in_specs=[pl.BlockSpec((N_DEV, R, W), lambda i: (0, 0, 0))], out_specs=pl.BlockSpec((R, W), lambda i: (0, 0)),
        out_shape=jax.ShapeDtypeStruct((R, W), f32), compiler_params=_params(("arbitrary",)),
    )(gathered)


_ANY = pl.BlockSpec(memory_space=pl.ANY)


class _Exchange:
    def __init__(self, inputs, out_shapes, scratch, start, finish):
        self.inputs, self.out_shapes, self.scratch, self.start, self.finish = inputs, out_shapes, scratch, start, finish

    def run(self, name):
        ni, no = len(self.inputs), len(self.out_shapes)

        def body(*refs):
            parts = (refs[:ni], refs[ni:ni + no], refs[ni + no:])
            self.start(*parts)
            self.finish(*parts)

        return pl.pallas_call(body, name=name, in_specs=[_ANY] * ni, out_specs=[_ANY] * no, out_shape=self.out_shapes,
                              scratch_shapes=self.scratch)(*self.inputs)


def _carry(body, n_in, n_out, rider, first, last):
    if rider is None:
        return body
    ri, ro = len(rider.inputs), len(rider.out_shapes)

    def hosted(*refs):
        a, b, c = n_in + ri, n_in + ri + n_out, n_in + ri + n_out + ro
        rs = len(refs) - c - len(rider.scratch)
        parts = (refs[n_in:a], refs[b:c], refs[c + rs:])

        @pl.when(first())
        def _():
            rider.start(*parts)

        body(*refs[:n_in], *refs[a:b], *refs[c:c + rs])

        @pl.when(last())
        def _():
            rider.finish(*parts)

    return hosted


def _rider_specs(rider):
    if rider is None:
        return [], [], [], [], []
    return [_ANY] * len(rider.inputs), [_ANY] * len(rider.out_shapes), list(rider.out_shapes), list(rider.scratch), list(rider.inputs)


def all_gather(blocks):
    n = len(blocks)

    def plan(x_refs, out_refs, sems):
        send_sems, recv_sems, local_sems = sems
        x, y, c = lax.axis_index("x"), lax.axis_index("y"), lax.axis_index("c")
        me, sibling = (x, y, c), (x, y, 1 - c)
        chips = [(1 - x, y), (x, 1 - y), (1 - x, 1 - y)]

        def copy(a, k, blk, to, src=None):
            px, py, pc = blk
            slot = out_refs[a].at[4 * px + 2 * py + pc]
            return pltpu.make_async_remote_copy(
                src_ref=slot if src is None else src, dst_ref=slot,
                send_sem=send_sems.at[7 * a + k], recv_sem=recv_sems.at[7 * a + k], device_id=to, device_id_type=MESH)

        mine = [pltpu.make_async_copy(x_refs[a], out_refs[a].at[4 * x + 2 * y + c], local_sems.at[a]) for a in range(n)]
        first = []
        for a in range(n):
            first += [copy(a, 0, me, sibling, src=x_refs[a])] + [copy(a, 1 + j, me, (*chip, c), src=x_refs[a]) for j, chip in enumerate(chips)]
        return c, me, sibling, chips, copy, mine, first

    def start(x_refs, out_refs, sems):
        _, _, _, _, _, mine, first = plan(x_refs, out_refs, sems)
        for cp in mine + first:
            cp.start()

    def finish(x_refs, out_refs, sems):
        c, me, sibling, chips, copy, mine, first = plan(x_refs, out_refs, sems)
        passed = []
        for j, chip in enumerate(chips):
            for a in range(n):
                copy(a, 1 + j, (*chip, c), me).wait_recv()
                passed.append(copy(a, 4 + j, (*chip, c), sibling))
                passed[-1].start()
        for a in range(n):
            copy(a, 0, sibling, me).wait_recv()
            for j, chip in enumerate(chips):
                copy(a, 4 + j, (*chip, 1 - c), me).wait_recv()
        for cp in first + passed:
            cp.wait_send()
        for cp in mine:
            cp.wait()

    return _Exchange(list(blocks), [jax.ShapeDtypeStruct((N_DEV,) + b.shape, b.dtype) for b in blocks],
                     [pltpu.SemaphoreType.DMA((7 * n,)), pltpu.SemaphoreType.DMA((7 * n,)), pltpu.SemaphoreType.DMA((n,))], start, finish)


def exchange_with_sibling(gs):
    n = len(gs)

    def plan(g_refs, recv_refs, sems):
        send_sems, recv_sems = sems
        x, y, c = lax.axis_index("x"), lax.axis_index("y"), lax.axis_index("c")
        return [pltpu.make_async_remote_copy(src_ref=g_refs[a].at[1], dst_ref=recv_refs[a], send_sem=send_sems.at[a],
                                             recv_sem=recv_sems.at[a], device_id=(x, y, 1 - c), device_id_type=MESH) for a in range(n)]

    def start(*refs):
        for cp in plan(*refs):
            cp.start()

    def finish(*refs):
        for cp in plan(*refs):
            cp.wait()

    return _Exchange(list(gs), [jax.ShapeDtypeStruct(g.shape[1:], g.dtype) for g in gs],
                     [pltpu.SemaphoreType.DMA((n,)), pltpu.SemaphoreType.DMA((n,))], start, finish)


def exchange_between_chips(parts):
    n = len(parts)

    def plan(p_refs, recv_refs, sems):
        send_sems, recv_sems = sems
        x, y, c = lax.axis_index("x"), lax.axis_index("y"), lax.axis_index("c")
        chips = [(2, (1 - x, y)), (1, (x, 1 - y)), (3, (1 - x, 1 - y))]
        return [pltpu.make_async_remote_copy(src_ref=p_refs[a].at[slot], dst_ref=recv_refs[a].at[k], send_sem=send_sems.at[3 * a + k],
                                             recv_sem=recv_sems.at[3 * a + k], device_id=(px, py, c), device_id_type=MESH)
                for a in range(n) for k, (slot, (px, py)) in enumerate(chips)]

    def start(*refs):
        for cp in plan(*refs):
            cp.start()

    def finish(*refs):
        for cp in plan(*refs):
            cp.wait()

    return _Exchange(list(parts), [jax.ShapeDtypeStruct((3,) + p.shape[1:], p.dtype) for p in parts],
                     [pltpu.SemaphoreType.DMA((3 * n,)), pltpu.SemaphoreType.DMA((3 * n,))], start, finish)


PARAMS = {
    "norm_w": ((DEPTH, 3, D_MODEL), 2),
    "ffn_w_gate": ((DEPTH, 2, D_MODEL, D_FF), 3),
    "ffn_w_up": ((DEPTH, 2, D_MODEL, D_FF), 3),
    "ffn_w_down": ((DEPTH, 2, D_FF, D_MODEL), 2),
    "ssd_w_in": ((2, D_MODEL, SSD_IN_DIM), 2),
    "ssd_conv_w": ((2, SSD_CONV_K, SSD_CONV_DIM), 2),
    "ssd_conv_b": ((2, SSD_CONV_DIM), None),
    "ssd_dt_bias": ((2, SSD_HEADS), None),
    "ssd_a_log": ((2, SSD_HEADS), None),
    "ssd_d": ((2, SSD_HEADS), None),
    "ssd_norm_w": ((2, SSD_INNER), None),
    "ssd_w_out": ((2, SSD_INNER, D_MODEL), 1),
    "sc_w_in": ((2, D_MODEL, 3 * D_MODEL), 2),
    "sc_conv_w": ((2, SC_CONV_K, D_MODEL), 2),
    "sc_w_out": ((2, D_MODEL, D_MODEL), 1),
    "final_norm_w": ((D_MODEL,), None),
}
NAMES = list(PARAMS)
BIG = ["ffn_w_gate", "ffn_w_up", "ffn_w_down", "ssd_w_in", "ssd_w_out", "sc_w_in", "sc_w_out"]
SMALL = [n for n in NAMES if n not in BIG]
SMALL_SHARDED = [n for n in SMALL if PARAMS[n][1] is not None]


def _round_up(n, m):
    return -(-n // m) * m


def _pack(flat_list, rows_multiple):
    flat = jnp.concatenate(flat_list)
    rows = _round_up(_round_up(flat.shape[0], PACK_W) // PACK_W, rows_multiple)
    return jnp.pad(flat, (0, rows * PACK_W - flat.shape[0])).reshape(rows, PACK_W)


def _unpack(packed, shapes, lead=()):
    flat = packed.reshape(lead + (-1,))
    out, off = [], 0
    for shp in shapes:
        n = 1
        for s in shp:
            n *= s
        out.append(flat[..., off:off + n].reshape(lead + tuple(shp)))
        off += n
    return out


def _local_shape(name):
    shp, ax = PARAMS[name]
    if ax is None:
        return shp
    return shp[:ax] + (shp[ax] // N_DEV,) + shp[ax + 1:]


def _full_from_gathered(g, name):
    shp, ax = PARAMS[name]
    return jnp.moveaxis(g, 0, ax).reshape(shp)


def _by_destination(full, name):
    shp, ax = PARAMS[name]
    loc = shp[ax] // N_DEV
    return jnp.moveaxis(full.reshape(shp[:ax] + (N_DEV, loc) + shp[ax + 1:]), ax, 0)


def _ssd_layer_fwd(xin, nw, p, rider=None):
    z, xbc, dt_raw = in_proj_fwd(xin, nw, [p["ssd_wz"], p["ssd_wx"], p["ssd_wdt"]], [bf16, bf16, f32])
    act, dt4 = ssd_conv_fwd(xbc, p["ssd_conv_w"], p["ssd_conv_b"], dt_raw, p["ssd_dt_bias"])
    y, states, *got = ssd_scan_fwd(act, dt4, p["ssd_alog4"], rider=rider)
    gn = ssd_gate_fwd(y, act, z, p["ssd_dx"], p["ssd_norm_w"])
    xout = out_proj_fwd(xin, gn, p["ssd_w_out"])
    return xout, (xin, z, xbc, dt_raw, act, dt4, y, states, gn), got


def _ssd_layer_bwd(dxo, nw, p, saved, gbuf, slab, rider=None):
    xin, z, xbc, dt_raw, act, dt4, y, states, gn = saved
    T = xin.shape[0]
    dy, dxs_skip, dz, dd_x, dgnw, dyb = ssd_gate_bwd(y, act, z, p["ssd_dx"], p["ssd_norm_w"], dxo, p["ssd_w_out"])
    gbuf["ssd_w_out"] = tn_matmul_to_shards(gn, dyb, gbuf["ssd_w_out"], (slab,), 0)
    g = {}
    g["ssd_norm_w"] = dgnw[0]
    g["ssd_d"] = jnp.sum(dd_x.reshape(SSD_HEADS, SSD_HEAD_DIM), axis=1)
    dxs, db, dc, ddt4, dalog4, *got = ssd_scan_bwd(act, dt4, p["ssd_alog4"], states, dy, rider=rider)
    g["ssd_a_log"] = dalog4[:, 0, :8].reshape(SSD_HEADS)
    dxbc, ddt_raw, dcw, dcb, ddtb = ssd_conv_bwd(xbc, p["ssd_conv_w"], p["ssd_conv_b"], dt_raw, p["ssd_dt_bias"], dxs, dxs_skip, db, dc, ddt4)
    g["ssd_conv_w"] = dcw[:SSD_CONV_K]
    g["ssd_conv_b"] = dcb[0]
    g["ssd_dt_bias"] = ddtb[0, :SSD_HEADS]
    dx, h, dnw = in_proj_bwd(xin, nw, dxo, [dz, dxbc, ddt_raw], [p["ssd_wz"], p["ssd_wx"], p["ssd_wdt"]])
    gbuf["ssd_w_in"] = ssd_in_to_shards(tn_matmul(h, dz), tn_matmul(h, dxbc), tn_matmul(h, ddt_raw), gbuf["ssd_w_in"], slab)
    return dx, dnw, g, got


def _sc_layer_fwd(xin, nw, p):
    (bcu,) = in_proj_fwd(xin, nw, [p["sc_w_in"]], [bf16])
    q = sc_mid_fwd(bcu, p["sc_conv_w"])
    return out_proj_fwd(xin, q, p["sc_w_out"]), (xin, bcu, q)


def _sc_layer_bwd(dxo, nw, p, saved, gbuf, slab):
    xin, bcu, q = saved
    dbcu, dcw, dyb = sc_mid_bwd(bcu, p["sc_conv_w"], dxo, p["sc_w_out"])
    gbuf["sc_w_out"] = tn_matmul_to_shards(q, dyb, gbuf["sc_w_out"], (slab,), 0)
    g = {"sc_conv_w": dcw[:SC_CONV_K]}
    dx, h, dnw = in_proj_bwd(xin, nw, dxo, [dbcu], [p["sc_w_in"]])
    gbuf["sc_w_in"] = tn_matmul_to_shards(h, dbcu, gbuf["sc_w_in"], (slab,), 1)
    return dx, dnw, g


def kernel(x, norm_w, ffn_w_gate, ffn_w_up, ffn_w_down, ssd_w_in, ssd_conv_w, ssd_conv_b, ssd_dt_bias, ssd_a_log, ssd_d, ssd_norm_w, ssd_w_out, sc_w_in, sc_conv_w, sc_w_out, final_norm_w, loss_target, m_norm_w, m_ffn_w_gate, m_ffn_w_up, m_ffn_w_down, m_ssd_w_in, m_ssd_conv_w, m_ssd_conv_b, m_ssd_dt_bias, m_ssd_a_log, m_ssd_d, m_ssd_norm_w, m_ssd_w_out, m_sc_w_in, m_sc_conv_w, m_sc_w_out, m_final_norm_w, v_norm_w, v_ffn_w_gate, v_ffn_w_up, v_ffn_w_down, v_ssd_w_in, v_ssd_conv_w, v_ssd_conv_b, v_ssd_dt_bias, v_ssd_a_log, v_ssd_d, v_ssd_norm_w, v_ssd_w_out, v_sc_w_in, v_sc_conv_w, v_sc_w_out, v_final_norm_w):
    w_loc = dict(zip(NAMES, (norm_w, ffn_w_gate, ffn_w_up, ffn_w_down, ssd_w_in, ssd_conv_w, ssd_conv_b, ssd_dt_bias, ssd_a_log, ssd_d, ssd_norm_w, ssd_w_out, sc_w_in, sc_conv_w, sc_w_out, final_norm_w)))
    m_loc = dict(zip(NAMES, (m_norm_w, m_ffn_w_gate, m_ffn_w_up, m_ffn_w_down, m_ssd_w_in, m_ssd_conv_w, m_ssd_conv_b, m_ssd_dt_bias, m_ssd_a_log, m_ssd_d, m_ssd_norm_w, m_ssd_w_out, m_sc_w_in, m_sc_conv_w, m_sc_w_out, m_final_norm_w)))
    v_loc = dict(zip(NAMES, (v_norm_w, v_ffn_w_gate, v_ffn_w_up, v_ffn_w_down, v_ssd_w_in, v_ssd_conv_w, v_ssd_conv_b, v_ssd_dt_bias, v_ssd_a_log, v_ssd_d, v_ssd_norm_w, v_ssd_w_out, v_sc_w_in, v_sc_conv_w, v_sc_w_out, v_final_norm_w)))
    ax, ay, ac = lax.axis_index("x"), lax.axis_index("y"), lax.axis_index("c")
    my_chip = 2 * ax + ay
    my_dev = 4 * ax + 2 * ay + ac
    T = x.shape[1]

    def as3d(a):
        return a.reshape((-1,) + a.shape[-2:])

    wb = {n: as3d(w_loc[n]).astype(bf16) for n in BIG}

    FFN = ["ffn_w_gate", "ffn_w_up", "ffn_w_down"]

    def mixer_names(i):
        return ["ssd_w_in", "ssd_w_out"] if i % 2 == 0 else ["sc_w_in", "sc_w_out"]

    ag_sets = [[(n, 0, 1) for n in FFN], [(n, 1, 1) for n in FFN] + [(n, 0, 1) for n in mixer_names(0)]]
    ag_sets += [[(n, 2 * r, 2) for n in FFN] + [(n, r // 2, 1) for n in mixer_names(r)] for r in (1, 2, 3)]

    def set_blocks(spec):
        return [wb[n][a0:a0 + na] for n, a0, na in spec]

    def set_weights(spec, gathered):
        q = {}
        for (n, _, _), g in zip(spec, gathered):
            if n == "ssd_w_in":
                q["ssd_wz"], q["ssd_wx"], q["ssd_wdt"] = assemble_ssd_in(g)
            else:
                q[n] = assemble(g, 1 if PARAMS[n][1] == len(PARAMS[n][0]) - 1 else 0)
        return q

    ss_shapes = [_local_shape(n) for n in SMALL_SHARDED]
    gathered0 = all_gather(set_blocks(ag_sets[0]) + [_pack([w_loc[n].reshape(-1) for n in SMALL_SHARDED], 8)]).run("all_gather_first")
    full = {}
    for n, part in zip(SMALL_SHARDED, _unpack(gathered0[-1], ss_shapes, lead=(N_DEV,))):
        full[n] = _full_from_gathered(part, n)
    for n in SMALL:
        if PARAMS[n][1] is None:
            full[n] = w_loc[n]
    small = {
        "ssd_conv_w": full["ssd_conv_w"],
        "ssd_conv_b": full["ssd_conv_b"].reshape(2, 1, SSD_CONV_DIM),
        "ssd_dt_bias": jnp.pad(full["ssd_dt_bias"], ((0, 0), (0, LANES - SSD_HEADS))).reshape(2, 1, LANES),
        "ssd_alog4": jnp.pad(full["ssd_a_log"].reshape(2, SSD_GROUPS, 1, 8), ((0, 0), (0, 0), (0, 0), (0, LANES - 8))),
        "ssd_dx": jnp.repeat(full["ssd_d"], SSD_HEAD_DIM, axis=1).reshape(2, 1, SSD_INNER),
        "ssd_norm_w": full["ssd_norm_w"].reshape(2, 1, SSD_INNER),
        "sc_conv_w": full["sc_conv_w"],
    }
    nw_all = full["norm_w"].reshape(DEPTH, 3, 1, D_MODEL)

    ffn_w = [[None, None] for _ in range(DEPTH)]
    mix_w = [None] * DEPTH

    def arrived(s, gathered):
        q = set_weights(ag_sets[s], gathered)
        ffn = tuple(q[n] for n in FFN)
        if s == 0:
            ffn_w[0][0] = ffn + ((0,),)
            return
        i = 0 if s == 1 else s - 1
        if s == 1:
            ffn_w[0][1] = ffn + ((0,),)
        else:
            ffn_w[i] = [ffn + ((0,),), ffn + ((1,),)]
        m = {n: v[0] for n, v in q.items() if n not in FFN}
        m.update({n: v[i // 2] for n, v in small.items() if n.startswith("ssd" if i % 2 == 0 else "sc")})
        mix_w[i] = m

    def rider_for(s):
        return all_gather(set_blocks(ag_sets[s]))

    xc = x[0]
    saved = []
    arrived(0, gathered0[:-1])
    for i in range(DEPTH):
        carried = {0: (1, 2, 3), 1: (4, None, None)}.get(i, (None, None, None))
        wg, wu, wd, idx = ffn_w[i][0]
        x1, g1, u1, a1, *got = ffn_fwd(xc, nw_all[i, 0], wg, wu, wd, idx, rider=rider_for(carried[0]) if carried[0] else None)
        if carried[0]:
            arrived(carried[0], got)
        if i % 2 == 0:
            x2, mix_saved, got = _ssd_layer_fwd(x1, nw_all[i, 1], mix_w[i], rider=rider_for(carried[1]) if carried[1] else None)
            if carried[1]:
                arrived(carried[1], got)
        else:
            x2, mix_saved = _sc_layer_fwd(x1, nw_all[i, 1], mix_w[i])
        wg, wu, wd, idx = ffn_w[i][1]
        x3, g3, u3, a3, *got = ffn_fwd(x2, nw_all[i, 2], wg, wu, wd, idx, rider=rider_for(carried[2]) if carried[2] else None)
        if carried[2]:
            arrived(carried[2], got)
        saved.append(((xc, g1, u1, a1), mix_saved, (x2, g3, u3, a3)))
        xc = x3

    loss_row, dx, dfw = loss_head(xc, full["final_norm_w"].reshape(1, D_MODEL), loss_target[0])
    loss = lax.psum(loss_row[0, 0], ("x", "y", "c"))

    grads = {n: [None] * PARAMS[n][0][0] for n in SMALL if n != "final_norm_w"}
    grads["final_norm_w"] = dfw[0]
    dnorm = [[None] * 3 for _ in range(DEPTH)]
    def slabs(n, which):
        if n.startswith("ffn"):
            return {"early": (2, 6), "mid": (1, 1), "last": (0, 1)}[which]
        if n.startswith("ssd"):
            return {"early": (1, 1), "mid": (0, 1), "last": (0, 0)}[which]
        return {"early": (0, 2), "mid": (0, 0), "last": (0, 0)}[which]

    gb = {which: {n: jax.ShapeDtypeStruct((2, 4, slabs(n, which)[1]) + wb[n].shape[1:], f32) for n in BIG if slabs(n, which)[1]}
          for which in ("early", "mid", "last")}

    def ffn_back(i, k, dxo, sv, rider=None):
        xin, g_, u_, a_ = sv
        which = "early" if i > 0 else ("mid" if k == 1 else "last")
        gbuf = gb[which]
        slab = 2 * i + k - slabs("ffn_w_gate", which)[0]
        wg, wu, wd, idx = ffn_w[i][k]
        dxi, h, dyb, dg, du, dnw, *got = ffn_bwd_dx(xin, dxo, g_, u_, nw_all[i, 2 * k], wg, wu, wd, idx, rider=rider)
        dnorm[i][2 * k] = dnw[0]
        gbuf["ffn_w_gate"] = tn_matmul_to_shards(h, dg, gbuf["ffn_w_gate"], (slab,), 1)
        gbuf["ffn_w_up"] = tn_matmul_to_shards(h, du, gbuf["ffn_w_up"], (slab,), 1)
        gbuf["ffn_w_down"] = tn_matmul_to_shards(a_, dyb, gbuf["ffn_w_down"], (slab,), 0)
        return dxi, got

    def reduce_in_chip(gbuf, from_sibling=None):
        names = list(gbuf)
        bufs = [gbuf[n] for n in names]
        if from_sibling is None:
            from_sibling = exchange_with_sibling(bufs).run("exchange_with_sibling")
        return names, bufs, from_sibling, [pair_sum_bf16(g, fs, "pair_sum_" + n) for n, g, fs in zip(names, bufs, from_sibling)]

    reduced, from_chips = {}, {}
    for i in reversed(range(DEPTH)):
        j = i // 2
        sv_a, sv_mix, sv_b = saved[i]
        if i == 0:
            dx, got = ffn_back(i, 1, dx, sv_b, rider=exchange_with_sibling(list(gb["early"].values())))
            reduced["early"] = reduce_in_chip(gb["early"], from_sibling=got)
        else:
            dx, _ = ffn_back(i, 1, dx, sv_b)
        if i % 2 == 0:
            rider = exchange_between_chips(reduced["early"][3]) if i == 0 else None
            dx, dnw, gm, got = _ssd_layer_bwd(dx, nw_all[i, 1], mix_w[i], sv_mix, gb["mid" if i == 0 else "early"], 0, rider=rider)
            if i == 0:
                from_chips["early"] = got
                reduced["mid"] = reduce_in_chip(gb["mid"])
        else:
            dx, dnw, gm = _sc_layer_bwd(dx, nw_all[i, 1], mix_w[i], sv_mix, gb["early"], j)
        dnorm[i][1] = dnw[0]
        for n, val in gm.items():
            grads[n][j] = val
        dx, got = ffn_back(i, 0, dx, sv_a, rider=exchange_between_chips(reduced["mid"][3]) if i == 0 else None)
        if i == 0:
            from_chips["mid"] = got

    grads["norm_w"] = jnp.stack([jnp.stack(r) for r in dnorm])
    for n in SMALL:
        if isinstance(grads[n], list):
            grads[n] = jnp.stack(grads[n])

    reduced["last"] = reduce_in_chip(gb["last"])
    from_chips["last"] = exchange_between_chips(reduced["last"][3]).run("exchange_between_chips")
    results = [{}, {}, {}, {}]
    outs = {}
    for which in ("last", "mid", "early"):
        names, bufs, from_sibling, _ = reduced[which]
        for n, g, fs, fc in zip(names, bufs, from_sibling, from_chips[which]):
            parts = [((0, 0), g), ((0,), fs), ((0,), fc), ((1,), fc), ((2,), fc)]
            outs[n] = adamw(parts, as3d(w_loc[n]), as3d(m_loc[n]), as3d(v_loc[n]), name="adamw_" + n + "_" + which,
                            a0=slabs(n, which)[0], prev=outs.get(n))
    for n in BIG:
        for k in range(4):
            results[k][n] = outs[n][k].reshape(_local_shape(n))

    g_small = _pack([grads[n].reshape(-1) for n in SMALL], 8)
    g_small = sum_over_devices(all_gather([g_small]).run("all_gather_small_grads")[0])
    g_small_full = dict(zip(SMALL, _unpack(g_small, [PARAMS[n][0] for n in SMALL])))
    g_small_loc = []
    for n in SMALL:
        if PARAMS[n][1] is None:
            g_small_loc.append(g_small_full[n])
        else:
            g_small_loc.append(lax.dynamic_index_in_dim(_by_destination(g_small_full[n], n), my_dev, axis=0, keepdims=False))
    small_shapes = [_local_shape(n) for n in SMALL]
    pack_small = lambda d: _pack([d[n].reshape(-1) for n in SMALL], 8)[None]
    small_out = adamw([_pack([gl.reshape(-1) for gl in g_small_loc], 8)[None]], pack_small(w_loc), pack_small(m_loc), pack_small(v_loc), name="adamw_small")
    for k in range(4):
        results[k].update(zip(SMALL, _unpack(small_out[k], small_shapes)))
    return (loss, dx[None], *[results[0][n] for n in NAMES], *[results[1][n] for n in NAMES],
            *[results[2][n] for n in NAMES], *[results[3][n] for n in NAMES])
```

```python
import functools

import jax
import jax.numpy as jnp
from jax import lax
from jax.experimental import pallas as pl
from jax.experimental.pallas import tpu as pltpu

f32 = jnp.float32
bf16 = jnp.bfloat16

D_MODEL = 1024
D_FF = 2816
DEPTH = 4
SSD_INNER = 2048
SSD_HEADS = 32
SSD_HEAD_DIM = 64
SSD_GROUPS = 4
SSD_STATE = 128
SSD_CONV_K = 4
SSD_CONV_DIM = 3072
SSD_IN_DIM = 5152
SSD_CHUNK = 128
SC_CONV_K = 3
RMS_EPS = 1e-5
N_DEV = 8
LANES = 128
HALO = 16
PACK_W = 1024
VMEM_LIMIT = 56 * 1024 * 1024
NEG_BIG = -1e30

ADAM_LR = 0.001
ADAM_B1 = 0.9
ADAM_B2 = 0.999
ADAM_EPS = 1e-08
ADAM_WD = 0.01
ADAM_STEP = 10

NT_DIMS = (((1,), (1,)), ((), ()))
TN_DIMS = (((0,), (0,)), ((), ()))
MESH = pl.DeviceIdType.MESH


def _params(sem=None):
    return pltpu.CompilerParams(dimension_semantics=sem, vmem_limit_bytes=VMEM_LIMIT)


def _resident(shape):
    nd = len(shape)
    return pl.BlockSpec(tuple(shape), lambda *_: (0,) * nd, pipeline_mode=pl.Buffered(1))


def _rows(tm, width):
    return pl.BlockSpec((tm, width), lambda i: (i, 0))


def _my_core_and_chip():
    return lax.axis_index("c"), 2 * lax.axis_index("x") + lax.axis_index("y")


def _sigmoid(v):
    return 0.5 * jnp.tanh(0.5 * v) + 0.5


def _softplus(v):
    return jnp.maximum(v, 0.0) + jnp.log(1.0 + jnp.exp(-jnp.abs(v)))


def _rms_fwd(xv, w):
    inv = lax.rsqrt(jnp.mean(xv * xv, axis=-1, keepdims=True) + RMS_EPS)
    xh = xv * inv
    return xh * w, xh, inv


def _rms_bwd(dh, xh, inv, w):
    dxh = dh * w
    dx = inv * (dxh - xh * jnp.mean(dxh * xh, axis=-1, keepdims=True))
    return dx, jnp.sum(dh * xh, axis=0, keepdims=True)


def _mm(a, b):
    return jnp.dot(a, b, preferred_element_type=f32)


def _mm_nt(a, b):
    return lax.dot_general(a, b, NT_DIMS, preferred_element_type=f32)


def _mm_tn(a, b):
    return lax.dot_general(a, b, TN_DIMS, preferred_element_type=f32)


def _layer_slab(w, idx):
    tail = w.shape[len(idx):]
    return pl.BlockSpec((None,) * len(idx) + tuple(tail), lambda *_: tuple(idx) + (0,) * len(tail), pipeline_mode=pl.Buffered(1))


def ffn_fwd(x, nw, wg, wu, wd, idx, tm=512, rider=None):
    T = x.shape[0]
    nt = T // tm
    r_in, r_out, r_shapes, r_scratch, r_args = _rider_specs(rider)

    def body(x_ref, nw_ref, wg_ref, wu_ref, wd_ref, xo_ref, g_ref, u_ref, a_ref):
        xv = x_ref[...]
        h, _, _ = _rms_fwd(xv, nw_ref[...])
        hb = h.astype(bf16)
        g = _mm(hb, wg_ref[...])
        u = _mm(hb, wu_ref[...])
        ab = (g * _sigmoid(g) * u).astype(bf16)
        g_ref[...] = g.astype(bf16)
        u_ref[...] = u.astype(bf16)
        a_ref[...] = ab
        xo_ref[...] = xv + 0.5 * _mm(ab, wd_ref[...])

    hosted = _carry(body, 5, 4, rider, lambda: pl.program_id(0) == 0, lambda: pl.program_id(0) == nt - 1)
    return pl.pallas_call(
        hosted, name="ffn_fwd" if rider is None else "ffn_fwd_carrying", grid=(nt,),
        in_specs=[_rows(tm, D_MODEL), _resident((1, D_MODEL)), _layer_slab(wg, idx), _layer_slab(wu, idx), _layer_slab(wd, idx)] + r_in,
        out_specs=[_rows(tm, D_MODEL), _rows(tm, D_FF), _rows(tm, D_FF), _rows(tm, D_FF)] + r_out,
        out_shape=[jax.ShapeDtypeStruct((T, D_MODEL), f32)] + [jax.ShapeDtypeStruct((T, D_FF), bf16)] * 3 + r_shapes,
        scratch_shapes=r_scratch,
        compiler_params=_params(("parallel",) if rider is None else ("arbitrary",)),
    )(x, nw, wg, wu, wd, *r_args)


def ffn_bwd_dx(x, dxo, g, u, nw, wg, wu, wd, idx, tm=256, rider=None):
    T = x.shape[0]
    nt = T // tm
    r_in, r_out, r_shapes, r_scratch, r_args = _rider_specs(rider)

    def body(x_ref, dxo_ref, g_ref, u_ref, nw_ref, wg_ref, wu_ref, wd_ref, dx_ref, h_ref, dy_ref, dg_ref, du_ref, dnw_ref):
        w = nw_ref[...]
        h, xh, inv = _rms_fwd(x_ref[...], w)
        dxo_v = dxo_ref[...]
        dyb = (0.5 * dxo_v).astype(bf16)
        da = _mm_nt(dyb, wd_ref[...])
        gv = g_ref[...].astype(f32)
        uv = u_ref[...].astype(f32)
        s = _sigmoid(gv)
        dgb = (da * uv * (s * (1.0 + gv * (1.0 - s)))).astype(bf16)
        dub = (da * (gv * s)).astype(bf16)
        dg_ref[...] = dgb
        du_ref[...] = dub
        dh = _mm_nt(dgb, wg_ref[...]) + _mm_nt(dub, wu_ref[...])
        dxn, dw = _rms_bwd(dh, xh, inv, w)
        dx_ref[...] = dxo_v + dxn
        h_ref[...] = h.astype(bf16)
        dy_ref[...] = dyb

        @pl.when(pl.program_id(0) == 0)
        def _():
            dnw_ref[...] = jnp.zeros_like(dnw_ref)

        dnw_ref[...] += dw

    hosted = _carry(body, 8, 6, rider, lambda: pl.program_id(0) == 0, lambda: pl.program_id(0) == nt - 1)
    return pl.pallas_call(
        hosted, name="ffn_bwd_dx" if rider is None else "ffn_bwd_dx_carrying", grid=(nt,),
        in_specs=[_rows(tm, D_MODEL), _rows(tm, D_MODEL), _rows(tm, D_FF), _rows(tm, D_FF), _resident((1, D_MODEL)),
                  _layer_slab(wg, idx), _layer_slab(wu, idx), _layer_slab(wd, idx)] + r_in,
        out_specs=[_rows(tm, D_MODEL), _rows(tm, D_MODEL), _rows(tm, D_MODEL), _rows(tm, D_FF), _rows(tm, D_FF),
                   pl.BlockSpec((1, D_MODEL), lambda i: (0, 0))] + r_out,
        out_shape=[jax.ShapeDtypeStruct((T, D_MODEL), f32), jax.ShapeDtypeStruct((T, D_MODEL), bf16), jax.ShapeDtypeStruct((T, D_MODEL), bf16),
                   jax.ShapeDtypeStruct((T, D_FF), bf16), jax.ShapeDtypeStruct((T, D_FF), bf16), jax.ShapeDtypeStruct((1, D_MODEL), f32)] + r_shapes,
        scratch_shapes=r_scratch,
        compiler_params=_params(("arbitrary",)),
    )(x, dxo, g, u, nw, wg, wu, wd, *r_args)


def tn_matmul(a, b, tk=1024):
    T, M = a.shape
    N = b.shape[1]
    bn = N if M * N <= 3_200_000 else N // 2
    nk = T // tk

    def body(a_ref, b_ref, o_ref):
        @pl.when(pl.program_id(1) == 0)
        def _():
            o_ref[...] = jnp.zeros_like(o_ref)

        o_ref[...] += _mm_tn(a_ref[...], b_ref[...])

    return pl.pallas_call(
        body, name=f"tn_matmul_{M}x{N}", grid=(N // bn, nk),
        in_specs=[pl.BlockSpec((tk, M), lambda j, k: (k, 0)), pl.BlockSpec((tk, bn), lambda j, k: (k, j))],
        out_specs=pl.BlockSpec((M, bn), lambda j, k: (0, j)),
        out_shape=jax.ShapeDtypeStruct((M, N), f32),
        compiler_params=_params(("parallel", "arbitrary")),
    )(a, b)


def tn_matmul_to_shards(a, b, buf, idx, axis):
    T, M = a.shape
    N = b.shape[1]
    m, n = buf.shape[-2:]
    tk = 1024 if M * N <= 2_200_000 else 512
    nk = T // tk
    fresh = isinstance(buf, jax.ShapeDtypeStruct)

    def body(a_ref, b_ref, *rest):
        o_ref, acc_ref = rest[-2:]
        k = pl.program_id(0)

        @pl.when(k == 0)
        def _():
            acc_ref[...] = jnp.zeros_like(acc_ref)

        acc_ref[...] += _mm_tn(a_ref[...], b_ref[...])

        @pl.when(k == nk - 1)
        def _():
            my_c, my_chip = _my_core_and_chip()
            for d in range(N_DEV):
                piece = acc_ref[:, pl.ds(d * n, n)] if axis == 1 else acc_ref[pl.ds(d * m, m), :]
                o_ref[(d % 2) ^ my_c, (d // 2) ^ my_chip] = piece

    none = (None,) * len(idx)
    return pl.pallas_call(
        body, name=f"tn_matmul_to_shards_{M}x{N}_{axis}", grid=(nk,),
        in_specs=[pl.BlockSpec((tk, M), lambda k: (k, 0)), pl.BlockSpec((tk, N), lambda k: (k, 0))] + ([] if fresh else [_ANY]),
        out_specs=pl.BlockSpec((2, 4) + none + (m, n), lambda k: (0, 0) + tuple(idx) + (0, 0)),
        out_shape=jax.ShapeDtypeStruct(buf.shape, f32),
        scratch_shapes=[pltpu.VMEM((M, N), f32)],
        input_output_aliases={} if fresh else {2: 0},
        compiler_params=_params(("arbitrary",)),
    )(a, b, *([] if fresh else [buf]))


def in_proj_fwd(x, nw, ws, out_dtypes, tm=512):
    T = x.shape[0]
    n = len(ws)

    def body(*refs):
        x_ref, nw_ref = refs[:2]
        w_refs = refs[2:2 + n]
        o_refs = refs[2 + n:]
        h, _, _ = _rms_fwd(x_ref[...], nw_ref[...])
        hb = h.astype(bf16)
        for w_ref, o_ref in zip(w_refs, o_refs):
            o_ref[...] = _mm(hb, w_ref[...]).astype(o_ref.dtype)

    return pl.pallas_call(
        body, name="in_proj_fwd_" + "_".join(str(w.shape[1]) for w in ws), grid=(T // tm,),
        in_specs=[_rows(tm, D_MODEL), _resident((1, D_MODEL))] + [_resident(w.shape) for w in ws],
        out_specs=[_rows(tm, w.shape[1]) for w in ws],
        out_shape=[jax.ShapeDtypeStruct((T, w.shape[1]), dt) for w, dt in zip(ws, out_dtypes)],
        compiler_params=_params(("parallel",)),
    )(x, nw, *ws)


def in_proj_bwd(x, nw, dxo, dys, ws, tm=512):
    T = x.shape[0]
    n = len(ws)

    def body(*refs):
        x_ref, nw_ref, dxo_ref = refs[:3]
        dy_refs = refs[3:3 + n]
        w_refs = refs[3 + n:3 + 2 * n]
        dx_ref, h_ref, dnw_ref = refs[3 + 2 * n:]
        w = nw_ref[...]
        h, xh, inv = _rms_fwd(x_ref[...], w)
        dh = _mm_nt(dy_refs[0][...], w_refs[0][...])
        for dy_ref, w_ref in zip(dy_refs[1:], w_refs[1:]):
            dh = dh + _mm_nt(dy_ref[...], w_ref[...])
        dxn, dw = _rms_bwd(dh, xh, inv, w)
        dx_ref[...] = dxo_ref[...] + dxn
        h_ref[...] = h.astype(bf16)

        @pl.when(pl.program_id(0) == 0)
        def _():
            dnw_ref[...] = jnp.zeros_like(dnw_ref)

        dnw_ref[...] += dw

    return pl.pallas_call(
        body, name="in_proj_bwd_" + "_".join(str(w.shape[1]) for w in ws), grid=(T // tm,),
        in_specs=[_rows(tm, D_MODEL), _resident((1, D_MODEL)), _rows(tm, D_MODEL)] + [_rows(tm, w.shape[1]) for w in ws]
        + [_resident(w.shape) for w in ws],
        out_specs=[_rows(tm, D_MODEL), _rows(tm, D_MODEL), pl.BlockSpec((1, D_MODEL), lambda i: (0, 0))],
        out_shape=[jax.ShapeDtypeStruct((T, D_MODEL), f32), jax.ShapeDtypeStruct((T, D_MODEL), bf16), jax.ShapeDtypeStruct((1, D_MODEL), f32)],
        compiler_params=_params(("arbitrary",)),
    )(x, nw, dxo, *dys, *ws)


def out_proj_fwd(x, a, w, tm=1024):
    T = x.shape[0]
    K = a.shape[1]

    def body(x_ref, a_ref, w_ref, o_ref):
        o_ref[...] = x_ref[...] + _mm(a_ref[...], w_ref[...])

    return pl.pallas_call(
        body, name=f"out_proj_fwd_{K}", grid=(T // tm,),
        in_specs=[_rows(tm, D_MODEL), _rows(tm, K), _resident(w.shape)],
        out_specs=_rows(tm, D_MODEL), out_shape=jax.ShapeDtypeStruct((T, D_MODEL), f32),
        compiler_params=_params(("parallel",)),
    )(x, a, w)


def _halo_spec(tm, width, n_tiles, reverse):
    per = tm // HALO

    def idx(i):
        t = (n_tiles - 1 - i) if reverse else i
        return (jnp.maximum(t * per - 1, 0), 0)

    return pl.BlockSpec((HALO, width), idx)


def _tile_spec(tm, width, n_tiles, reverse):
    if reverse:
        return pl.BlockSpec((tm, width), lambda i: (n_tiles - 1 - i, 0))
    return _rows(tm, width)


ROW_BLOCK = 64


def _strip(s):
    return pl.ds(pl.multiple_of(s * LANES, LANES), LANES)


def _conv_rows(ext_ref, w_ref, cols, k_w, r0):
    base = HALO - (k_w - 1) + r0
    wins = [ext_ref[pl.ds(base + k, ROW_BLOCK), :] for k in range(k_w)]
    out = w_ref[pl.ds(0, 1), cols] * wins[0]
    for k in range(1, k_w):
        out = out + w_ref[pl.ds(k, 1), cols] * wins[k]
    return out, wins


def _shifted_back(d_ref, w_ref, cols, k_w, r0):
    out = w_ref[pl.ds(0, 1), cols] * d_ref[pl.ds(r0 + k_w - 1, ROW_BLOCK), :]
    for k in range(1, k_w):
        out = out + w_ref[pl.ds(k, 1), cols] * d_ref[pl.ds(r0 + k_w - 1 - k, ROW_BLOCK), :]
    return out


def ssd_conv_fwd(xbc, conv_w, conv_b, dt_raw, dt_bias, tm=512):
    T = xbc.shape[0]
    nt = T // tm
    K = SSD_CONV_K

    def body(x_ref, halo_ref, w_ref, b_ref, dtr_ref, dtb_ref, act_ref, dt_ref, ext_ref):
        first = pl.program_id(0) == 0

        def strip(s, carry):
            cols = _strip(s)
            ext_ref[pl.ds(0, HALO), :] = jnp.where(first, 0.0, halo_ref[:, cols].astype(f32))
            ext_ref[pl.ds(HALO, tm), :] = x_ref[:, cols].astype(f32)
            for r0 in range(0, tm, ROW_BLOCK):
                pre, _ = _conv_rows(ext_ref, w_ref, cols, K, r0)
                pre = pre + b_ref[:, cols]
                act_ref[pl.ds(r0, ROW_BLOCK), cols] = (pre * _sigmoid(pre)).astype(bf16)
            return carry

        lax.fori_loop(0, SSD_CONV_DIM // LANES, strip, 0)
        dt = _softplus(dtr_ref[...] + dtb_ref[...])
        lane = lax.broadcasted_iota(jnp.int32, (1, LANES), 1)
        for g in range(SSD_GROUPS):
            dt_ref[g] = jnp.where(lane < 8, dt if g == 0 else pltpu.roll(dt, LANES - 8 * g, axis=1), 0.0)

    return pl.pallas_call(
        body, name="ssd_conv_fwd", grid=(nt,),
        in_specs=[_rows(tm, SSD_CONV_DIM), _halo_spec(tm, SSD_CONV_DIM, nt, False), _resident(conv_w.shape), _resident(conv_b.shape),
                  _rows(tm, LANES), _resident(dt_bias.shape)],
        out_specs=[_rows(tm, SSD_CONV_DIM), pl.BlockSpec((SSD_GROUPS, tm, LANES), lambda i: (0, i, 0))],
        out_shape=[jax.ShapeDtypeStruct((T, SSD_CONV_DIM), bf16), jax.ShapeDtypeStruct((SSD_GROUPS, T, LANES), f32)],
        scratch_shapes=[pltpu.VMEM((tm + HALO, LANES), f32)],
        compiler_params=_params(("parallel",)),
    )(xbc, xbc, conv_w, conv_b, dt_raw, dt_bias)


def ssd_conv_bwd(xbc, conv_w, conv_b, dt_raw, dt_bias, dxs_a, dxs_b, db, dc, ddt, tm=512):
    T = xbc.shape[0]
    nt = T // tm
    K = SSD_CONV_K

    def body(x_ref, halo_ref, w_ref, b_ref, dtr_ref, dtb_ref, da_ref, dbb_ref, db_ref, dc_ref, ddt_ref,
             dx_ref, ddtr_ref, dw_ref, dbias_ref, ddtb_ref, ext_ref, dpre_ref, carry_ref):
        i = pl.program_id(0)

        @pl.when(i == 0)
        def _():
            carry_ref[...] = jnp.zeros_like(carry_ref)
            dw_ref[...] = jnp.zeros_like(dw_ref)
            dbias_ref[...] = jnp.zeros_like(dbias_ref)
            ddtb_ref[...] = jnp.zeros_like(ddtb_ref)

        first_tile = i == nt - 1

        def run_strips(lo, hi, load_dact):
            def strip(s, carry):
                cols = _strip(s)
                ext_ref[pl.ds(0, HALO), :] = jnp.where(first_tile, 0.0, halo_ref[:, cols].astype(f32))
                ext_ref[pl.ds(HALO, tm), :] = x_ref[:, cols].astype(f32)
                dpre_ref[pl.ds(tm, 8), :] = carry_ref[:, cols]
                bias = b_ref[:, cols]
                dws = [jnp.zeros((1, LANES), f32) for _ in range(K)]
                dbs = jnp.zeros((1, LANES), f32)
                for r0 in range(0, tm, ROW_BLOCK):
                    pre, wins = _conv_rows(ext_ref, w_ref, cols, K, r0)
                    pre = pre + bias
                    sg = _sigmoid(pre)
                    dpre = load_dact(s, r0) * (sg * (1.0 + pre * (1.0 - sg)))
                    dpre_ref[pl.ds(r0, ROW_BLOCK), :] = dpre
                    dbs = dbs + jnp.sum(dpre, axis=0, keepdims=True)
                    for k in range(K):
                        dws[k] = dws[k] + jnp.sum(dpre * wins[k], axis=0, keepdims=True)
                carry_ref[:, cols] = dpre_ref[pl.ds(0, 8), :]
                for r0 in range(0, tm, ROW_BLOCK):
                    dx_ref[pl.ds(r0, ROW_BLOCK), cols] = _shifted_back(dpre_ref, w_ref, cols, K, r0).astype(bf16)
                for k in range(K):
                    dw_ref[pl.ds(k, 1), cols] += dws[k]
                dbias_ref[:, cols] += dbs
                return carry

            lax.fori_loop(lo, hi, strip, 0)

        rows = lambda r0: pl.ds(r0, ROW_BLOCK)
        n_x = SSD_INNER // LANES
        n_g = SSD_GROUPS * SSD_STATE // LANES
        run_strips(0, n_x, lambda s, r0: da_ref[rows(r0), _strip(s)].astype(f32) + dbb_ref[rows(r0), _strip(s)].astype(f32))
        run_strips(n_x, n_x + n_g, lambda s, r0: db_ref[rows(r0), _strip(s - n_x)].astype(f32))
        run_strips(n_x + n_g, n_x + 2 * n_g, lambda s, r0: dc_ref[rows(r0), _strip(s - n_x - n_g)].astype(f32))
        lane = lax.broadcasted_iota(jnp.int32, (1, LANES), 1)
        ddt = jnp.where(lane < 8, ddt_ref[0], 0.0)
        for g in range(1, SSD_GROUPS):
            ddt = ddt + pltpu.roll(jnp.where(lane < 8, ddt_ref[g], 0.0), 8 * g, axis=1)
        ddtr = ddt * _sigmoid(dtr_ref[...] + dtb_ref[...])
        ddtr_ref[...] = ddtr.astype(bf16)
        ddtb_ref[...] += jnp.sum(ddtr, axis=0, keepdims=True)

    rev = functools.partial(_tile_spec, tm, n_tiles=nt, reverse=True)
    const = lambda shape: pl.BlockSpec(shape, lambda i: (0, 0))
    return pl.pallas_call(
        body, name="ssd_conv_bwd", grid=(nt,),
        in_specs=[rev(width=SSD_CONV_DIM), _halo_spec(tm, SSD_CONV_DIM, nt, True), _resident(conv_w.shape), _resident(conv_b.shape),
                  rev(width=LANES), _resident(dt_bias.shape), rev(width=SSD_INNER), rev(width=SSD_INNER),
                  rev(width=SSD_GROUPS * SSD_STATE), rev(width=SSD_GROUPS * SSD_STATE),
                  pl.BlockSpec((SSD_GROUPS, tm, LANES), lambda i: (0, nt - 1 - i, 0))],
        out_specs=[rev(width=SSD_CONV_DIM), rev(width=LANES), const((8, SSD_CONV_DIM)), const((1, SSD_CONV_DIM)), const((1, LANES))],
        out_shape=[jax.ShapeDtypeStruct((T, SSD_CONV_DIM), bf16), jax.ShapeDtypeStruct((T, LANES), bf16),
                   jax.ShapeDtypeStruct((8, SSD_CONV_DIM), f32), jax.ShapeDtypeStruct((1, SSD_CONV_DIM), f32), jax.ShapeDtypeStruct((1, LANES), f32)],
        scratch_shapes=[pltpu.VMEM((tm + HALO, LANES), f32), pltpu.VMEM((tm + 8, LANES), f32), pltpu.VMEM((8, SSD_CONV_DIM), f32)],
        compiler_params=_params(("arbitrary",)),
    )(xbc, xbc, conv_w, conv_b, dt_raw, dt_bias, dxs_a, dxs_b, db, dc, ddt)


def _ssd_chunk(xs, bm, cm, dt, alog, st):
    L = SSD_CHUNK
    row = lax.broadcasted_iota(jnp.int32, (L, L), 0)
    col = lax.broadcasted_iota(jnp.int32, (L, L), 1)
    causal = row >= col
    tril = jnp.where(causal, 1.0, 0.0).astype(f32)
    lane = lax.broadcasted_iota(jnp.int32, (1, LANES), 1)
    sub = lax.broadcasted_iota(jnp.int32, (LANES, 1), 0)
    lo = lane < SSD_HEAD_DIM
    last_row = sub == L - 1

    dta = dt * (-jnp.exp(alog))
    a_cs = jnp.dot(tril, dta, precision=lax.Precision.HIGHEST, preferred_element_type=f32)
    a_cs_t = a_cs.T
    bmb = bm.astype(bf16)
    cmb = cm.astype(bf16)
    cb = _mm_nt(cmb, bmb)
    c_st = _mm(cmb, st.astype(bf16))

    def head_col(v, e):
        return jnp.sum(jnp.where(lane == e, v, 0.0), axis=1, keepdims=True)

    def head_row(v, e):
        return jnp.sum(jnp.where(sub == e, v, 0.0), axis=0, keepdims=True)

    ys, sts = [], []
    for j in range(4):
        e0, e1 = 2 * j, 2 * j + 1
        c0, c1 = head_col(a_cs, e0), head_col(a_cs, e1)
        acs_x = jnp.where(lo, c0, c1)
        dt_x = jnp.where(lo, head_col(dt, e0), head_col(dt, e1))
        xd = xs[:, j * LANES:(j + 1) * LANES] * dt_x
        m0 = cb * jnp.exp(jnp.where(causal, c0 - head_row(a_cs_t, e0), NEG_BIG))
        m1 = cb * jnp.exp(jnp.where(causal, c1 - head_row(a_cs_t, e1), NEG_BIG))
        mcat = jnp.concatenate([m0, m1], axis=1).astype(bf16)
        xcat = jnp.concatenate([jnp.where(lo, xd, 0.0), jnp.where(lo, 0.0, xd)], axis=0).astype(bf16)
        y_diag = _mm(mcat, xcat)
        a_last = jnp.sum(jnp.where(last_row, acs_x, 0.0), axis=0, keepdims=True)
        x_dec = (xd * jnp.exp(a_last - acs_x)).astype(bf16)
        s_new = _mm_tn(bmb, x_dec)
        y_off = c_st[:, j * LANES:(j + 1) * LANES] * jnp.exp(acs_x)
        ys.append(y_diag + y_off)
        sts.append(jnp.exp(a_last) * st[:, j * LANES:(j + 1) * LANES] + s_new)
    return jnp.concatenate(ys, axis=1), jnp.concatenate(sts, axis=1)


SCAN_GROUPS_FWD = 4
SCAN_GROUPS_BWD = 1


def _scan_specs(nc, reverse, gs):
    L = SSD_CHUNK
    ch = (lambda c: nc - 1 - c) if reverse else (lambda c: c)
    gw = SSD_INNER // SSD_GROUPS
    b0 = SSD_INNER // (gs * SSD_STATE)
    c0 = (SSD_INNER + SSD_GROUPS * SSD_STATE) // (gs * SSD_STATE)
    xs = pl.BlockSpec((L, gs * gw), lambda g, c: (ch(c), g))
    bm = pl.BlockSpec((L, gs * SSD_STATE), lambda g, c: (ch(c), b0 + g))
    cm = pl.BlockSpec((L, gs * SSD_STATE), lambda g, c: (ch(c), c0 + g))
    dt = pl.BlockSpec((gs, L, LANES), lambda g, c: (g, ch(c), 0))
    alog = pl.BlockSpec((gs, 1, LANES), lambda g, c: (g, 0, 0))
    st = pl.BlockSpec((gs, None, SSD_STATE, gw), lambda g, c: (g, ch(c), 0, 0))
    y = pl.BlockSpec((L, gs * gw), lambda g, c: (ch(c), g))
    grp = pl.BlockSpec((L, gs * SSD_STATE), lambda g, c: (ch(c), g))
    return xs, bm, cm, dt, alog, st, y, grp


def ssd_scan_fwd(act, dt4, alog4, rider=None):
    T = act.shape[0]
    nc = T // SSD_CHUNK
    gs = SCAN_GROUPS_FWD
    ng = SSD_GROUPS // gs
    gw = SSD_INNER // SSD_GROUPS
    xs_s, bm_s, cm_s, dt_s, alog_s, st_s, y_s, _ = _scan_specs(nc, False, gs)
    r_in, r_out, r_shapes, r_scratch, r_args = _rider_specs(rider)

    def body(xs_ref, bm_ref, cm_ref, dt_ref, alog_ref, y_ref, st_ref, st_scr):
        @pl.when(pl.program_id(1) == 0)
        def _():
            st_scr[...] = jnp.zeros_like(st_scr)

        for q in range(gs):
            xc, gc = pl.ds(q * gw, gw), pl.ds(q * SSD_STATE, SSD_STATE)
            st = st_scr[q]
            st_ref[q] = st
            y, st_new = _ssd_chunk(xs_ref[:, xc].astype(f32), bm_ref[:, gc].astype(f32), cm_ref[:, gc].astype(f32), dt_ref[q], alog_ref[q], st)
            y_ref[:, xc] = y.astype(bf16)
            st_scr[q] = st_new

    first = lambda: jnp.logical_and(pl.program_id(0) == 0, pl.program_id(1) == 0)
    last = lambda: jnp.logical_and(pl.program_id(0) == ng - 1, pl.program_id(1) == nc - 1)
    return pl.pallas_call(
        _carry(body, 5, 2, rider, first, last), name="ssd_scan_fwd" if rider is None else "ssd_scan_fwd_carrying", grid=(ng, nc),
        in_specs=[xs_s, bm_s, cm_s, dt_s, alog_s] + r_in, out_specs=[y_s, st_s] + r_out,
        out_shape=[jax.ShapeDtypeStruct((T, SSD_INNER), bf16), jax.ShapeDtypeStruct((SSD_GROUPS, nc, SSD_STATE, gw), f32)] + r_shapes,
        scratch_shapes=[pltpu.VMEM((gs, SSD_STATE, gw), f32)] + r_scratch,
        compiler_params=_params(("parallel" if rider is None else "arbitrary", "arbitrary")),
    )(act, act, act, dt4, alog4, *r_args)


def ssd_scan_bwd(act, dt4, alog4, states, dy, rider=None):
    T = act.shape[0]
    nc = T // SSD_CHUNK
    gs = SCAN_GROUPS_BWD
    ng = SSD_GROUPS // gs
    gw = SSD_INNER // SSD_GROUPS
    xs_s, bm_s, cm_s, dt_s, alog_s, st_s, y_s, grp_s = _scan_specs(nc, True, gs)
    r_in, r_out, r_shapes, r_scratch, r_args = _rider_specs(rider)

    def body(xs_ref, bm_ref, cm_ref, dt_ref, alog_ref, st_ref, dy_ref, dxs_ref, db_ref, dc_ref, ddt_ref, dalog_ref, dst_scr):
        @pl.when(pl.program_id(1) == 0)
        def _():
            dst_scr[...] = jnp.zeros_like(dst_scr)
            dalog_ref[...] = jnp.zeros_like(dalog_ref)

        for q in range(gs):
            xc, gc = pl.ds(q * gw, gw), pl.ds(q * SSD_STATE, SSD_STATE)
            _, vjp = jax.vjp(_ssd_chunk, xs_ref[:, xc].astype(f32), bm_ref[:, gc].astype(f32), cm_ref[:, gc].astype(f32),
                             dt_ref[q], alog_ref[q], st_ref[q])
            dxs, dbm, dcm, ddt, dalog, dst = vjp((dy_ref[:, xc].astype(f32), dst_scr[q]))
            dxs_ref[:, xc] = dxs.astype(bf16)
            db_ref[:, gc] = dbm.astype(bf16)
            dc_ref[:, gc] = dcm.astype(bf16)
            ddt_ref[q] = ddt
            dalog_ref[q] += dalog
            dst_scr[q] = dst

    first = lambda: jnp.logical_and(pl.program_id(0) == 0, pl.program_id(1) == 0)
    last = lambda: jnp.logical_and(pl.program_id(0) == ng - 1, pl.program_id(1) == nc - 1)
    return pl.pallas_call(
        _carry(body, 7, 5, rider, first, last), name="ssd_scan_bwd" if rider is None else "ssd_scan_bwd_carrying", grid=(ng, nc),
        in_specs=[xs_s, bm_s, cm_s, dt_s, alog_s, st_s, y_s] + r_in,
        out_specs=[y_s, grp_s, grp_s, dt_s, alog_s] + r_out,
        out_shape=[jax.ShapeDtypeStruct((T, SSD_INNER), bf16), jax.ShapeDtypeStruct((T, SSD_GROUPS * SSD_STATE), bf16),
                   jax.ShapeDtypeStruct((T, SSD_GROUPS * SSD_STATE), bf16), jax.ShapeDtypeStruct((SSD_GROUPS, T, LANES), f32),
                   jax.ShapeDtypeStruct((SSD_GROUPS, 1, LANES), f32)] + r_shapes,
        scratch_shapes=[pltpu.VMEM((gs, SSD_STATE, gw), f32)] + r_scratch,
        compiler_params=_params(("parallel" if rider is None else "arbitrary", "arbitrary")),
    )(act, act, act, dt4, alog4, states, dy, *r_args)


GATE_ROWS = 256


def _ssd_gate(y, xs, z, d_x, nw):
    g = (y + xs * d_x) * (z * _sigmoid(z))
    return g * lax.rsqrt(jnp.mean(g * g, axis=-1, keepdims=True) + RMS_EPS) * nw


def _gate_blocks(tm, fn):
    gw = SSD_INNER // SSD_GROUPS

    def block(r, carry):
        rows = pl.ds(r * GATE_ROWS if isinstance(r, int) else pl.multiple_of(r * GATE_ROWS, GATE_ROWS), GATE_ROWS)
        for k in range(SSD_GROUPS):
            fn(rows, pl.ds(k * gw, gw))
        return carry

    if tm == GATE_ROWS:
        block(0, 0)
    else:
        lax.fori_loop(0, tm // GATE_ROWS, block, 0)


def ssd_gate_fwd(y, act, z, d_x, nw, tm=512):
    T = y.shape[0]

    def body(y_ref, xs_ref, z_ref, d_ref, nw_ref, o_ref):
        def one(rows, cols):
            o_ref[rows, cols] = _ssd_gate(y_ref[rows, cols].astype(f32), xs_ref[rows, cols].astype(f32), z_ref[rows, cols].astype(f32),
                                          d_ref[:, cols], nw_ref[:, cols]).astype(bf16)

        _gate_blocks(tm, one)

    return pl.pallas_call(
        body, name="ssd_gate_fwd", grid=(T // tm,),
        in_specs=[_rows(tm, SSD_INNER), _rows(tm, SSD_INNER), _rows(tm, SSD_INNER), _resident(d_x.shape), _resident(nw.shape)],
        out_specs=_rows(tm, SSD_INNER), out_shape=jax.ShapeDtypeStruct((T, SSD_INNER), bf16),
        compiler_params=_params(("parallel",)),
    )(y, act, z, d_x, nw)


def ssd_gate_bwd(y, act, z, d_x, nw, dxo, w_out, tm=512):
    T = y.shape[0]

    def body(y_ref, xs_ref, z_ref, d_ref, nw_ref, dxo_ref, w_ref, dy_ref, dxs_ref, dz_ref, dd_ref, dnw_ref, dyb_ref):
        @pl.when(pl.program_id(0) == 0)
        def _():
            dd_ref[...] = jnp.zeros_like(dd_ref)
            dnw_ref[...] = jnp.zeros_like(dnw_ref)

        dyb_ref[...] = dxo_ref[...].astype(bf16)

        def one(rows, cols):
            _, vjp = jax.vjp(_ssd_gate, y_ref[rows, cols].astype(f32), xs_ref[rows, cols].astype(f32), z_ref[rows, cols].astype(f32),
                             d_ref[:, cols], nw_ref[:, cols])
            dy, dxs, dz, dd, dnw = vjp(_mm_nt(dyb_ref[rows, :], w_ref[cols, :]))
            dy_ref[rows, cols] = dy.astype(bf16)
            dxs_ref[rows, cols] = dxs.astype(bf16)
            dz_ref[rows, cols] = dz.astype(bf16)
            dd_ref[:, cols] += dd
            dnw_ref[:, cols] += dnw

        _gate_blocks(tm, one)

    const = pl.BlockSpec((1, SSD_INNER), lambda i: (0, 0))
    return pl.pallas_call(
        body, name="ssd_gate_bwd", grid=(T // tm,),
        in_specs=[_rows(tm, SSD_INNER), _rows(tm, SSD_INNER), _rows(tm, SSD_INNER), _resident(d_x.shape), _resident(nw.shape),
                  _rows(tm, D_MODEL), _resident(w_out.shape)],
        out_specs=[_rows(tm, SSD_INNER)] * 3 + [const, const, _rows(tm, D_MODEL)],
        out_shape=[jax.ShapeDtypeStruct((T, SSD_INNER), bf16)] * 3 + [jax.ShapeDtypeStruct((1, SSD_INNER), f32)] * 2
        + [jax.ShapeDtypeStruct((T, D_MODEL), bf16)],
        compiler_params=_params(("arbitrary",)),
    )(y, act, z, d_x, nw, dxo, w_out)


def sc_mid_fwd(bcu, conv_w, tm=512):
    T = bcu.shape[0]
    nt = T // tm
    Dm = D_MODEL

    def body(x_ref, halo_ref, w_ref, q_ref, ext_ref):
        first = pl.program_id(0) == 0
        n_s = Dm // LANES

        def strip(s, carry):
            cols, c_cols, u_cols = _strip(s), _strip(s + n_s), _strip(s + 2 * n_s)
            ext_ref[pl.ds(0, HALO), :] = jnp.where(first, 0.0, halo_ref[:, c_cols].astype(f32) * halo_ref[:, u_cols].astype(f32))
            ext_ref[pl.ds(HALO, tm), :] = x_ref[:, c_cols].astype(f32) * x_ref[:, u_cols].astype(f32)
            for r0 in range(0, tm, ROW_BLOCK):
                rows = pl.ds(r0, ROW_BLOCK)
                v, _ = _conv_rows(ext_ref, w_ref, cols, SC_CONV_K, r0)
                q_ref[rows, cols] = (x_ref[rows, cols].astype(f32) * v).astype(bf16)
            return carry

        lax.fori_loop(0, n_s, strip, 0)

    return pl.pallas_call(
        body, name="sc_mid_fwd", grid=(nt,),
        in_specs=[_rows(tm, 3 * Dm), _halo_spec(tm, 3 * Dm, nt, False), _resident(conv_w.shape)],
        out_specs=_rows(tm, Dm), out_shape=jax.ShapeDtypeStruct((T, Dm), bf16),
        scratch_shapes=[pltpu.VMEM((tm + HALO, LANES), f32)],
        compiler_params=_params(("parallel",)),
    )(bcu, bcu, conv_w)


def sc_mid_bwd(bcu, conv_w, dxo, w_out, tm=512):
    T = bcu.shape[0]
    nt = T // tm
    Dm = D_MODEL
    K = SC_CONV_K

    def body(x_ref, halo_ref, w_ref, dxo_ref, wo_ref, dx_ref, dw_ref, dyb_ref, ext_ref, dv_ref, carry_ref, dq_ref):
        i = pl.program_id(0)

        @pl.when(i == 0)
        def _():
            carry_ref[...] = jnp.zeros_like(carry_ref)
            dw_ref[...] = jnp.zeros_like(dw_ref)

        dyb = dxo_ref[...].astype(bf16)
        dyb_ref[...] = dyb
        dq_ref[...] = _mm_nt(dyb, wo_ref[...])
        first_tile = i == nt - 1
        n_s = Dm // LANES

        def strip(s, carry):
            cols, c_cols, u_cols = _strip(s), _strip(s + n_s), _strip(s + 2 * n_s)
            ext_ref[pl.ds(0, HALO), :] = jnp.where(first_tile, 0.0, halo_ref[:, c_cols].astype(f32) * halo_ref[:, u_cols].astype(f32))
            ext_ref[pl.ds(HALO, tm), :] = x_ref[:, c_cols].astype(f32) * x_ref[:, u_cols].astype(f32)
            dv_ref[pl.ds(tm, 8), :] = carry_ref[:, cols]
            dws = [jnp.zeros((1, LANES), f32) for _ in range(K)]
            for r0 in range(0, tm, ROW_BLOCK):
                rows = pl.ds(r0, ROW_BLOCK)
                v, wins = _conv_rows(ext_ref, w_ref, cols, K, r0)
                dqv = dq_ref[rows, cols]
                dv = dqv * x_ref[rows, cols].astype(f32)
                dv_ref[rows, :] = dv
                dx_ref[rows, cols] = (dqv * v).astype(bf16)
                for k in range(K):
                    dws[k] = dws[k] + jnp.sum(dv * wins[k], axis=0, keepdims=True)
            carry_ref[:, cols] = dv_ref[pl.ds(0, 8), :]
            for r0 in range(0, tm, ROW_BLOCK):
                rows = pl.ds(r0, ROW_BLOCK)
                dp = _shifted_back(dv_ref, w_ref, cols, K, r0)
                dx_ref[rows, c_cols] = (dp * x_ref[rows, u_cols].astype(f32)).astype(bf16)
                dx_ref[rows, u_cols] = (dp * x_ref[rows, c_cols].astype(f32)).astype(bf16)
            for k in range(K):
                dw_ref[pl.ds(k, 1), cols] += dws[k]
            return carry

        lax.fori_loop(0, n_s, strip, 0)

    return pl.pallas_call(
        body, name="sc_mid_bwd", grid=(nt,),
        in_specs=[_tile_spec(tm, 3 * Dm, nt, True), _halo_spec(tm, 3 * Dm, nt, True), _resident(conv_w.shape), _tile_spec(tm, Dm, nt, True),
                  _resident(w_out.shape)],
        out_specs=[_tile_spec(tm, 3 * Dm, nt, True), pl.BlockSpec((8, Dm), lambda i: (0, 0)), _tile_spec(tm, Dm, nt, True)],
        out_shape=[jax.ShapeDtypeStruct((T, 3 * Dm), bf16), jax.ShapeDtypeStruct((8, Dm), f32), jax.ShapeDtypeStruct((T, Dm), bf16)],
        scratch_shapes=[pltpu.VMEM((tm + HALO, LANES), f32), pltpu.VMEM((tm + 8, LANES), f32), pltpu.VMEM((8, Dm), f32),
                        pltpu.VMEM((tm, Dm), f32)],
        compiler_params=_params(("arbitrary",)),
    )(bcu, bcu, conv_w, dxo, w_out)


def loss_head(x, fw, target, tm=1024):
    T = x.shape[0]

    def body(x_ref, fw_ref, t_ref, loss_ref, dx_ref, dfw_ref):
        @pl.when(pl.program_id(0) == 0)
        def _():
            loss_ref[...] = jnp.zeros_like(loss_ref)
            dfw_ref[...] = jnp.zeros_like(dfw_ref)

        w = fw_ref[...]
        y, xh, inv = _rms_fwd(x_ref[...], w)
        err = y - t_ref[...]
        loss_ref[...] += 0.5 * jnp.sum(jnp.mean(err * err, axis=-1, keepdims=True), axis=0, keepdims=True)
        dx, dw = _rms_bwd(err * (1.0 / D_MODEL), xh, inv, w)
        dx_ref[...] = dx
        dfw_ref[...] += dw

    return pl.pallas_call(
        body, name="loss_head", grid=(T // tm,),
        in_specs=[_rows(tm, D_MODEL), _resident((1, D_MODEL)), _rows(tm, D_MODEL)],
        out_specs=[pl.BlockSpec((1, LANES), lambda i: (0, 0)), _rows(tm, D_MODEL), pl.BlockSpec((1, D_MODEL), lambda i: (0, 0))],
        out_shape=[jax.ShapeDtypeStruct((1, LANES), f32), jax.ShapeDtypeStruct((T, D_MODEL), f32), jax.ShapeDtypeStruct((1, D_MODEL), f32)],
        compiler_params=_params(("arbitrary",)),
    )(x, fw, target)


ELEMENTWISE_TILE_BYTES = 1_600_000


def _row_tile(rows, width):
    row_bytes = 4 * _round_up(width, LANES)
    tile = rows
    while tile * row_bytes > ELEMENTWISE_TILE_BYTES and tile % 16 == 0:
        tile //= 2
    return tile


def adamw(g_parts, w, m, v, name="adamw", a0=0, prev=None):
    A, B, n = w.shape
    tb = _row_tile(B, n)
    n_parts = len(g_parts)
    arrays, specs = [], []
    for part in g_parts:
        lead, arr = part if isinstance(part, tuple) else ((), part)
        specs.append(pl.BlockSpec((None,) * (len(lead) + 1) + (tb, n), lambda a, t, lead=lead: tuple(lead) + (a, t, 0)))
        arrays.append(arr)
    na = arrays[0].shape[-3]
    prev = list(prev) if prev is not None else []

    def body(*refs):
        n = n_parts
        g_refs = refs[:n]
        w_ref, m_ref, v_ref = refs[n:n + 3]
        go_ref, d_ref, mo_ref, vo_ref = refs[n + 3 + len(prev):]
        g = g_refs[0][...].astype(f32)
        for r in g_refs[1:]:
            g = g + r[...].astype(f32)
        m_new = ADAM_B1 * m_ref[...] + (1.0 - ADAM_B1) * g
        v_new = ADAM_B2 * v_ref[...] + (1.0 - ADAM_B2) * (g * g)
        m_hat = m_new / (1.0 - ADAM_B1 ** ADAM_STEP)
        v_hat = v_new / (1.0 - ADAM_B2 ** ADAM_STEP)
        go_ref[...] = g
        d_ref[...] = -ADAM_LR * (m_hat / (jnp.sqrt(v_hat) + ADAM_EPS) + ADAM_WD * w_ref[...])
        mo_ref[...] = m_new
        vo_ref[...] = v_new

    plain = pl.BlockSpec((None, tb, n), lambda a, t: (a + a0, t, 0))
    return pl.pallas_call(
        body, name=name, grid=(na, B // tb), in_specs=specs + [plain] * 3 + [_ANY] * len(prev), out_specs=[plain] * 4,
        out_shape=[jax.ShapeDtypeStruct((A, B, n), f32)] * 4,
        input_output_aliases={n_parts + 3 + k: k for k in range(len(prev))},
        compiler_params=_params(("parallel", "parallel")),
    )(*arrays, w, m, v, *prev)


def pair_sum_bf16(ga, gb, name):
    _, A, B, n = gb.shape
    tb = _row_tile(B, n)

    def body(a_ref, b_ref, o_ref):
        o_ref[...] = (a_ref[...] + b_ref[...]).astype(bf16)

    return pl.pallas_call(
        body, name=name, grid=(3, A, B // tb),
        in_specs=[pl.BlockSpec((None, None, None, tb, n), lambda j, a, t: (0, j + 1, a, t, 0)),
                  pl.BlockSpec((None, None, tb, n), lambda j, a, t: (j + 1, a, t, 0))],
        out_specs=pl.BlockSpec((None, None, tb, n), lambda j, a, t: (j + 1, a, t, 0)),
        out_shape=jax.ShapeDtypeStruct((4, A, B, n), bf16),
        compiler_params=_params(("parallel", "parallel", "parallel")),
    )(ga, gb)


def assemble(gathered, axis, tk=256):
    _, A, K, n = gathered.shape
    if axis == 1:
        def body(w_ref, o_ref):
            o_ref[...] = jnp.concatenate([w_ref[j] for j in range(N_DEV)], axis=1)

        return pl.pallas_call(
            body, name=f"assemble_cols_{K}x{n}", grid=(A, K // tk),
            in_specs=[pl.BlockSpec((N_DEV, None, tk, n), lambda a, t: (0, a, t, 0))],
            out_specs=pl.BlockSpec((None, tk, N_DEV * n), lambda a, t: (a, t, 0)),
            out_shape=jax.ShapeDtypeStruct((A, K, N_DEV * n), gathered.dtype),
            compiler_params=_params(("parallel", "parallel")),
        )(gathered)

    def body(w_ref, o_ref):
        for j in range(N_DEV):
            o_ref[pl.ds(j * K, K), :] = w_ref[j]

    return pl.pallas_call(
        body, name=f"assemble_rows_{K}x{n}", grid=(A,),
        in_specs=[pl.BlockSpec((N_DEV, None, K, n), lambda a: (0, a, 0, 0))],
        out_specs=pl.BlockSpec((None, N_DEV * K, n), lambda a: (a, 0, 0)),
        out_shape=jax.ShapeDtypeStruct((A, N_DEV * K, n), gathered.dtype),
        compiler_params=_params(("parallel",)),
    )(gathered)


SSD_IN_PAD = -(-SSD_IN_DIM // LANES) * LANES


def assemble_ssd_in(gathered, tk=256):
    _, A, K, n = gathered.shape

    def body(w_ref, z_ref, x_ref, dt_ref, full_ref):
        full_ref[:, pl.ds(SSD_IN_PAD - LANES, LANES)] = jnp.zeros((tk, LANES), gathered.dtype)
        for j in range(N_DEV):
            full_ref[:, pl.ds(j * n, n)] = w_ref[j]
        z_ref[...] = full_ref[:, pl.ds(0, SSD_INNER)]
        x_ref[...] = full_ref[:, pl.ds(SSD_INNER, SSD_CONV_DIM)]
        dt_ref[...] = full_ref[:, pl.ds(SSD_INNER + SSD_CONV_DIM, LANES)]

    widths = (SSD_INNER, SSD_CONV_DIM, LANES)
    return pl.pallas_call(
        body, name="assemble_ssd_in", grid=(A, K // tk),
        in_specs=[pl.BlockSpec((N_DEV, None, tk, n), lambda a, t: (0, a, t, 0))],
        out_specs=[pl.BlockSpec((None, tk, w), lambda a, t: (a, t, 0)) for w in widths],
        out_shape=[jax.ShapeDtypeStruct((A, K, w), gathered.dtype) for w in widths],
        scratch_shapes=[pltpu.VMEM((tk, SSD_IN_PAD), gathered.dtype)],
        compiler_params=_params(("parallel", "parallel")),
    )(gathered)


def ssd_in_to_shards(dwz, dwx, dwdt, buf, j, tk=256):
    K = dwz.shape[0]
    n = buf.shape[-1]
    fresh = isinstance(buf, jax.ShapeDtypeStruct)

    def body(z_ref, x_ref, dt_ref, *rest):
        o_ref, full_ref = rest[-2:]
        full_ref[:, pl.ds(0, SSD_INNER)] = z_ref[...]
        full_ref[:, pl.ds(SSD_INNER, SSD_CONV_DIM)] = x_ref[...]
        full_ref[:, pl.ds(SSD_INNER + SSD_CONV_DIM, LANES)] = dt_ref[...]
        my_c, my_chip = _my_core_and_chip()
        for d in range(N_DEV):
            o_ref[(d % 2) ^ my_c, (d // 2) ^ my_chip] = full_ref[:, pl.ds(d * n, n)]

    return pl.pallas_call(
        body, name="ssd_in_to_shards", grid=(K // tk,),
        in_specs=[_rows(tk, SSD_INNER), _rows(tk, SSD_CONV_DIM), _rows(tk, LANES)] + ([] if fresh else [_ANY]),
        out_specs=pl.BlockSpec((2, 4, None, tk, n), lambda t: (0, 0, j, t, 0)),
        out_shape=jax.ShapeDtypeStruct(buf.shape, f32),
        scratch_shapes=[pltpu.VMEM((tk, SSD_IN_PAD), f32)],
        input_output_aliases={} if fresh else {3: 0},
        compiler_params=_params(("parallel",)),
    )(dwz, dwx, dwdt, *([] if fresh else [buf]))


def sum_over_devices(gathered):
    _, R, W = gathered.shape

    def body(g_ref, o_ref):
        acc = g_ref[0]
        for k in range(1, N_DEV):
            acc = acc + g_ref[k]
        o_ref[...] = acc

    return pl.pallas_call(
        body, name="sum_over_devices", grid=(1,),
        in_specs=[pl.BlockSpec((N_DEV, R, W), lambda i: (0, 0, 0))], out_specs=pl.BlockSpec((R, W), lambda i: (0, 0)),
        out_shape=jax.ShapeDtypeStruct((R, W), f32), compiler_params=_params(("arbitrary",)),
    )(gathered)


_ANY = pl.BlockSpec(memory_space=pl.ANY)


class _Exchange:
    def __init__(self, inputs, out_shapes, scratch, start, finish):
        self.inputs, self.out_shapes, self.scratch, self.start, self.finish = inputs, out_shapes, scratch, start, finish

    def run(self, name):
        ni, no = len(self.inputs), len(self.out_shapes)

        def body(*refs):
            parts = (refs[:ni], refs[ni:ni + no], refs[ni + no:])
            self.start(*parts)
            self.finish(*parts)

        return pl.pallas_call(body, name=name, in_specs=[_ANY] * ni, out_specs=[_ANY] * no, out_shape=self.out_shapes,
                              scratch_shapes=self.scratch)(*self.inputs)


def _carry(body, n_in, n_out, rider, first, last):
    if rider is None:
        return body
    ri, ro = len(rider.inputs), len(rider.out_shapes)

    def hosted(*refs):
        a, b, c = n_in + ri, n_in + ri + n_out, n_in + ri + n_out + ro
        rs = len(refs) - c - len(rider.scratch)
        parts = (refs[n_in:a], refs[b:c], refs[c + rs:])

        @pl.when(first())
        def _():
            rider.start(*parts)

        body(*refs[:n_in], *refs[a:b], *refs[c:c + rs])

        @pl.when(last())
        def _():
            rider.finish(*parts)

    return hosted


def _rider_specs(rider):
    if rider is None:
        return [], [], [], [], []
    return [_ANY] * len(rider.inputs), [_ANY] * len(rider.out_shapes), list(rider.out_shapes), list(rider.scratch), list(rider.inputs)


def all_gather(blocks):
    n = len(blocks)

    def plan(x_refs, out_refs, sems):
        send_sems, recv_sems, local_sems = sems
        x, y, c = lax.axis_index("x"), lax.axis_index("y"), lax.axis_index("c")
        me, sibling = (x, y, c), (x, y, 1 - c)
        chips = [(1 - x, y), (x, 1 - y), (1 - x, 1 - y)]

        def copy(a, k, blk, to, src=None):
            px, py, pc = blk
            slot = out_refs[a].at[4 * px + 2 * py + pc]
            return pltpu.make_async_remote_copy(
                src_ref=slot if src is None else src, dst_ref=slot,
                send_sem=send_sems.at[7 * a + k], recv_sem=recv_sems.at[7 * a + k], device_id=to, device_id_type=MESH)

        mine = [pltpu.make_async_copy(x_refs[a], out_refs[a].at[4 * x + 2 * y + c], local_sems.at[a]) for a in range(n)]
        first = []
        for a in range(n):
            first += [copy(a, 0, me, sibling, src=x_refs[a])] + [copy(a, 1 + j, me, (*chip, c), src=x_refs[a]) for j, chip in enumerate(chips)]
        return c, me, sibling, chips, copy, mine, first

    def start(x_refs, out_refs, sems):
        _, _, _, _, _, mine, first = plan(x_refs, out_refs, sems)
        for cp in mine + first:
            cp.start()

    def finish(x_refs, out_refs, sems):
        c, me, sibling, chips, copy, mine, first = plan(x_refs, out_refs, sems)
        passed = []
        for j, chip in enumerate(chips):
            for a in range(n):
                copy(a, 1 + j, (*chip, c), me).wait_recv()
                passed.append(copy(a, 4 + j, (*chip, c), sibling))
                passed[-1].start()
        for a in range(n):
            copy(a, 0, sibling, me).wait_recv()
            for j, chip in enumerate(chips):
                copy(a, 4 + j, (*chip, 1 - c), me).wait_recv()
        for cp in first + passed:
            cp.wait_send()
        for cp in mine:
            cp.wait()

    return _Exchange(list(blocks), [jax.ShapeDtypeStruct((N_DEV,) + b.shape, b.dtype) for b in blocks],
                     [pltpu.SemaphoreType.DMA((7 * n,)), pltpu.SemaphoreType.DMA((7 * n,)), pltpu.SemaphoreType.DMA((n,))], start, finish)


def exchange_with_sibling(gs):
    n = len(gs)

    def plan(g_refs, recv_refs, sems):
        send_sems, recv_sems = sems
        x, y, c = lax.axis_index("x"), lax.axis_index("y"), lax.axis_index("c")
        return [pltpu.make_async_remote_copy(src_ref=g_refs[a].at[1], dst_ref=recv_refs[a], send_sem=send_sems.at[a],
                                             recv_sem=recv_sems.at[a], device_id=(x, y, 1 - c), device_id_type=MESH) for a in range(n)]

    def start(*refs):
        for cp in plan(*refs):
            cp.start()

    def finish(*refs):
        for cp in plan(*refs):
            cp.wait()

    return _Exchange(list(gs), [jax.ShapeDtypeStruct(g.shape[1:], g.dtype) for g in gs],
                     [pltpu.SemaphoreType.DMA((n,)), pltpu.SemaphoreType.DMA((n,))], start, finish)


def exchange_between_chips(parts):
    n = len(parts)

    def plan(p_refs, recv_refs, sems):
        send_sems, recv_sems = sems
        x, y, c = lax.axis_index("x"), lax.axis_index("y"), lax.axis_index("c")
        chips = [(2, (1 - x, y)), (1, (x, 1 - y)), (3, (1 - x, 1 - y))]
        return [pltpu.make_async_remote_copy(src_ref=p_refs[a].at[slot], dst_ref=recv_refs[a].at[k], send_sem=send_sems.at[3 * a + k],
                                             recv_sem=recv_sems.at[3 * a + k], device_id=(px, py, c), device_id_type=MESH)
                for a in range(n) for k, (slot, (px, py)) in enumerate(chips)]

    def start(*refs):
        for cp in plan(*refs):
            cp.start()

    def finish(*refs):
        for cp in plan(*refs):
            cp.wait()

    return _Exchange(list(parts), [jax.ShapeDtypeStruct((3,) + p.shape[1:], p.dtype) for p in parts],
                     [pltpu.SemaphoreType.DMA((3 * n,)), pltpu.SemaphoreType.DMA((3 * n,))], start, finish)


PARAMS = {
    "norm_w": ((DEPTH, 3, D_MODEL), 2),
    "ffn_w_gate": ((DEPTH, 2, D_MODEL, D_FF), 3),
    "ffn_w_up": ((DEPTH, 2, D_MODEL, D_FF), 3),
    "ffn_w_down": ((DEPTH, 2, D_FF, D_MODEL), 2),
    "ssd_w_in": ((2, D_MODEL, SSD_IN_DIM), 2),
    "ssd_conv_w": ((2, SSD_CONV_K, SSD_CONV_DIM), 2),
    "ssd_conv_b": ((2, SSD_CONV_DIM), None),
    "ssd_dt_bias": ((2, SSD_HEADS), None),
    "ssd_a_log": ((2, SSD_HEADS), None),
    "ssd_d": ((2, SSD_HEADS), None),
    "ssd_norm_w": ((2, SSD_INNER), None),
    "ssd_w_out": ((2, SSD_INNER, D_MODEL), 1),
    "sc_w_in": ((2, D_MODEL, 3 * D_MODEL), 2),
    "sc_conv_w": ((2, SC_CONV_K, D_MODEL), 2),
    "sc_w_out": ((2, D_MODEL, D_MODEL), 1),
    "final_norm_w": ((D_MODEL,), None),
}
NAMES = list(PARAMS)
BIG = ["ffn_w_gate", "ffn_w_up", "ffn_w_down", "ssd_w_in", "ssd_w_out", "sc_w_in", "sc_w_out"]
SMALL = [n for n in NAMES if n not in BIG]
SMALL_SHARDED = [n for n in SMALL if PARAMS[n][1] is not None]


def _round_up(n, m):
    return -(-n // m) * m


def _pack(flat_list, rows_multiple):
    flat = jnp.concatenate(flat_list)
    rows = _round_up(_round_up(flat.shape[0], PACK_W) // PACK_W, rows_multiple)
    return jnp.pad(flat, (0, rows * PACK_W - flat.shape[0])).reshape(rows, PACK_W)


def _unpack(packed, shapes, lead=()):
    flat = packed.reshape(lead + (-1,))
    out, off = [], 0
    for shp in shapes:
        n = 1
        for s in shp:
            n *= s
        out.append(flat[..., off:off + n].reshape(lead + tuple(shp)))
        off += n
    return out


def _local_shape(name):
    shp, ax = PARAMS[name]
    if ax is None:
        return shp
    return shp[:ax] + (shp[ax] // N_DEV,) + shp[ax + 1:]


def _full_from_gathered(g, name):
    shp, ax = PARAMS[name]
    return jnp.moveaxis(g, 0, ax).reshape(shp)


def _by_destination(full, name):
    shp, ax = PARAMS[name]
    loc = shp[ax] // N_DEV
    return jnp.moveaxis(full.reshape(shp[:ax] + (N_DEV, loc) + shp[ax + 1:]), ax, 0)


def _ssd_layer_fwd(xin, nw, p, rider=None):
    z, xbc, dt_raw = in_proj_fwd(xin, nw, [p["ssd_wz"], p["ssd_wx"], p["ssd_wdt"]], [bf16, bf16, f32])
    act, dt4 = ssd_conv_fwd(xbc, p["ssd_conv_w"], p["ssd_conv_b"], dt_raw, p["ssd_dt_bias"])
    y, states, *got = ssd_scan_fwd(act, dt4, p["ssd_alog4"], rider=rider)
    gn = ssd_gate_fwd(y, act, z, p["ssd_dx"], p["ssd_norm_w"])
    xout = out_proj_fwd(xin, gn, p["ssd_w_out"])
    return xout, (xin, z, xbc, dt_raw, act, dt4, y, states, gn), got


def _ssd_layer_bwd(dxo, nw, p, saved, gbuf, slab, rider=None):
    xin, z, xbc, dt_raw, act, dt4, y, states, gn = saved
    T = xin.shape[0]
    dy, dxs_skip, dz, dd_x, dgnw, dyb = ssd_gate_bwd(y, act, z, p["ssd_dx"], p["ssd_norm_w"], dxo, p["ssd_w_out"])
    gbuf["ssd_w_out"] = tn_matmul_to_shards(gn, dyb, gbuf["ssd_w_out"], (slab,), 0)
    g = {}
    g["ssd_norm_w"] = dgnw[0]
    g["ssd_d"] = jnp.sum(dd_x.reshape(SSD_HEADS, SSD_HEAD_DIM), axis=1)
    dxs, db, dc, ddt4, dalog4, *got = ssd_scan_bwd(act, dt4, p["ssd_alog4"], states, dy, rider=rider)
    g["ssd_a_log"] = dalog4[:, 0, :8].reshape(SSD_HEADS)
    dxbc, ddt_raw, dcw, dcb, ddtb = ssd_conv_bwd(xbc, p["ssd_conv_w"], p["ssd_conv_b"], dt_raw, p["ssd_dt_bias"], dxs, dxs_skip, db, dc, ddt4)
    g["ssd_conv_w"] = dcw[:SSD_CONV_K]
    g["ssd_conv_b"] = dcb[0]
    g["ssd_dt_bias"] = ddtb[0, :SSD_HEADS]
    dx, h, dnw = in_proj_bwd(xin, nw, dxo, [dz, dxbc, ddt_raw], [p["ssd_wz"], p["ssd_wx"], p["ssd_wdt"]])
    gbuf["ssd_w_in"] = ssd_in_to_shards(tn_matmul(h, dz), tn_matmul(h, dxbc), tn_matmul(h, ddt_raw), gbuf["ssd_w_in"], slab)
    return dx, dnw, g, got


def _sc_layer_fwd(xin, nw, p):
    (bcu,) = in_proj_fwd(xin, nw, [p["sc_w_in"]], [bf16])
    q = sc_mid_fwd(bcu, p["sc_conv_w"])
    return out_proj_fwd(xin, q, p["sc_w_out"]), (xin, bcu, q)


def _sc_layer_bwd(dxo, nw, p, saved, gbuf, slab):
    xin, bcu, q = saved
    dbcu, dcw, dyb = sc_mid_bwd(bcu, p["sc_conv_w"], dxo, p["sc_w_out"])
    gbuf["sc_w_out"] = tn_matmul_to_shards(q, dyb, gbuf["sc_w_out"], (slab,), 0)
    g = {"sc_conv_w": dcw[:SC_CONV_K]}
    dx, h, dnw = in_proj_bwd(xin, nw, dxo, [dbcu], [p["sc_w_in"]])
    gbuf["sc_w_in"] = tn_matmul_to_shards(h, dbcu, gbuf["sc_w_in"], (slab,), 1)
    return dx, dnw, g


def kernel(x, norm_w, ffn_w_gate, ffn_w_up, ffn_w_down, ssd_w_in, ssd_conv_w, ssd_conv_b, ssd_dt_bias, ssd_a_log, ssd_d, ssd_norm_w, ssd_w_out, sc_w_in, sc_conv_w, sc_w_out, final_norm_w, loss_target, m_norm_w, m_ffn_w_gate, m_ffn_w_up, m_ffn_w_down, m_ssd_w_in, m_ssd_conv_w, m_ssd_conv_b, m_ssd_dt_bias, m_ssd_a_log, m_ssd_d, m_ssd_norm_w, m_ssd_w_out, m_sc_w_in, m_sc_conv_w, m_sc_w_out, m_final_norm_w, v_norm_w, v_ffn_w_gate, v_ffn_w_up, v_ffn_w_down, v_ssd_w_in, v_ssd_conv_w, v_ssd_conv_b, v_ssd_dt_bias, v_ssd_a_log, v_ssd_d, v_ssd_norm_w, v_ssd_w_out, v_sc_w_in, v_sc_conv_w, v_sc_w_out, v_final_norm_w):
    w_loc = dict(zip(NAMES, (norm_w, ffn_w_gate, ffn_w_up, ffn_w_down, ssd_w_in, ssd_conv_w, ssd_conv_b, ssd_dt_bias, ssd_a_log, ssd_d, ssd_norm_w, ssd_w_out, sc_w_in, sc_conv_w, sc_w_out, final_norm_w)))
    m_loc = dict(zip(NAMES, (m_norm_w, m_ffn_w_gate, m_ffn_w_up, m_ffn_w_down, m_ssd_w_in, m_ssd_conv_w, m_ssd_conv_b, m_ssd_dt_bias, m_ssd_a_log, m_ssd_d, m_ssd_norm_w, m_ssd_w_out, m_sc_w_in, m_sc_conv_w, m_sc_w_out, m_final_norm_w)))
    v_loc = dict(zip(NAMES, (v_norm_w, v_ffn_w_gate, v_ffn_w_up, v_ffn_w_down, v_ssd_w_in, v_ssd_conv_w, v_ssd_conv_b, v_ssd_dt_bias, v_ssd_a_log, v_ssd_d, v_ssd_norm_w, v_ssd_w_out, v_sc_w_in, v_sc_conv_w, v_sc_w_out, v_final_norm_w)))
    my_dev = 4 * lax.axis_index("x") + 2 * lax.axis_index("y") + lax.axis_index("c")

    def as3d(a):
        return a.reshape((-1,) + a.shape[-2:])

    wb = {n: as3d(w_loc[n]).astype(bf16) for n in BIG}

    FFN = ["ffn_w_gate", "ffn_w_up", "ffn_w_down"]

    def mixer_names(i):
        return ["ssd_w_in", "ssd_w_out"] if i % 2 == 0 else ["sc_w_in", "sc_w_out"]

    ag_sets = [[(n, 0, 1) for n in FFN], [(n, 1, 1) for n in FFN] + [(n, 0, 1) for n in mixer_names(0)]]
    ag_sets += [[(n, 2 * r, 2) for n in FFN] + [(n, r // 2, 1) for n in mixer_names(r)] for r in (1, 2, 3)]

    def set_blocks(spec):
        return [wb[n][a0:a0 + na] for n, a0, na in spec]

    def set_weights(spec, gathered):
        q = {}
        for (n, _, _), g in zip(spec, gathered):
            if n == "ssd_w_in":
                q["ssd_wz"], q["ssd_wx"], q["ssd_wdt"] = assemble_ssd_in(g)
            else:
                q[n] = assemble(g, 1 if PARAMS[n][1] == len(PARAMS[n][0]) - 1 else 0)
        return q

    ss_shapes = [_local_shape(n) for n in SMALL_SHARDED]
    gathered0 = all_gather(set_blocks(ag_sets[0]) + [_pack([w_loc[n].reshape(-1) for n in SMALL_SHARDED], 8)]).run("all_gather_first")
    full = {}
    for n, part in zip(SMALL_SHARDED, _unpack(gathered0[-1], ss_shapes, lead=(N_DEV,))):
        full[n] = _full_from_gathered(part, n)
    for n in SMALL:
        if PARAMS[n][1] is None:
            full[n] = w_loc[n]
    small = {
        "ssd_conv_w": full["ssd_conv_w"],
        "ssd_conv_b": full["ssd_conv_b"].reshape(2, 1, SSD_CONV_DIM),
        "ssd_dt_bias": jnp.pad(full["ssd_dt_bias"], ((0, 0), (0, LANES - SSD_HEADS))).reshape(2, 1, LANES),
        "ssd_alog4": jnp.pad(full["ssd_a_log"].reshape(2, SSD_GROUPS, 1, 8), ((0, 0), (0, 0), (0, 0), (0, LANES - 8))),
        "ssd_dx": jnp.repeat(full["ssd_d"], SSD_HEAD_DIM, axis=1).reshape(2, 1, SSD_INNER),
        "ssd_norm_w": full["ssd_norm_w"].reshape(2, 1, SSD_INNER),
        "sc_conv_w": full["sc_conv_w"],
    }
    nw_all = full["norm_w"].reshape(DEPTH, 3, 1, D_MODEL)

    ffn_w = [[None, None] for _ in range(DEPTH)]
    mix_w = [None] * DEPTH

    def arrived(s, gathered):
        q = set_weights(ag_sets[s], gathered)
        ffn = tuple(q[n] for n in FFN)
        if s == 0:
            ffn_w[0][0] = ffn + ((0,),)
            return
        i = 0 if s == 1 else s - 1
        if s == 1:
            ffn_w[0][1] = ffn + ((0,),)
        else:
            ffn_w[i] = [ffn + ((0,),), ffn + ((1,),)]
        m = {n: v[0] for n, v in q.items() if n not in FFN}
        m.update({n: v[i // 2] for n, v in small.items() if n.startswith("ssd" if i % 2 == 0 else "sc")})
        mix_w[i] = m

    def rider_for(s):
        return all_gather(set_blocks(ag_sets[s]))

    xc = x[0]
    saved = []
    arrived(0, gathered0[:-1])
    for i in range(DEPTH):
        carried = {0: (1, 2, 3), 1: (4, None, None)}.get(i, (None, None, None))
        wg, wu, wd, idx = ffn_w[i][0]
        x1, g1, u1, a1, *got = ffn_fwd(xc, nw_all[i, 0], wg, wu, wd, idx, rider=rider_for(carried[0]) if carried[0] else None)
        if carried[0]:
            arrived(carried[0], got)
        if i % 2 == 0:
            x2, mix_saved, got = _ssd_layer_fwd(x1, nw_all[i, 1], mix_w[i], rider=rider_for(carried[1]) if carried[1] else None)
            if carried[1]:
                arrived(carried[1], got)
        else:
            x2, mix_saved = _sc_layer_fwd(x1, nw_all[i, 1], mix_w[i])
        wg, wu, wd, idx = ffn_w[i][1]
        x3, g3, u3, a3, *got = ffn_fwd(x2, nw_all[i, 2], wg, wu, wd, idx, rider=rider_for(carried[2]) if carried[2] else None)
        if carried[2]:
            arrived(carried[2], got)
        saved.append(((xc, g1, u1, a1), mix_saved, (x2, g3, u3, a3)))
        xc = x3

    loss_row, dx, dfw = loss_head(xc, full["final_norm_w"].reshape(1, D_MODEL), loss_target[0])
    loss = lax.psum(loss_row[0, 0], ("x", "y", "c"))

    grads = {n: [None] * PARAMS[n][0][0] for n in SMALL if n != "final_norm_w"}
    grads["final_norm_w"] = dfw[0]
    dnorm = [[None] * 3 for _ in range(DEPTH)]
    def slabs(n, which):
        if n.startswith("ffn"):
            return {"early": (2, 6), "mid": (1, 1), "last": (0, 1)}[which]
        if n.startswith("ssd"):
            return {"early": (1, 1), "mid": (0, 1), "last": (0, 0)}[which]
        return {"early": (0, 2), "mid": (0, 0), "last": (0, 0)}[which]

    gb = {which: {n: jax.ShapeDtypeStruct((2, 4, slabs(n, which)[1]) + wb[n].shape[1:], f32) for n in BIG if slabs(n, which)[1]}
          for which in ("early", "mid", "last")}

    def ffn_back(i, k, dxo, sv, rider=None):
        xin, g_, u_, a_ = sv
        which = "early" if i > 0 else ("mid" if k == 1 else "last")
        gbuf = gb[which]
        slab = 2 * i + k - slabs("ffn_w_gate", which)[0]
        wg, wu, wd, idx = ffn_w[i][k]
        dxi, h, dyb, dg, du, dnw, *got = ffn_bwd_dx(xin, dxo, g_, u_, nw_all[i, 2 * k], wg, wu, wd, idx, rider=rider)
        dnorm[i][2 * k] = dnw[0]
        gbuf["ffn_w_gate"] = tn_matmul_to_shards(h, dg, gbuf["ffn_w_gate"], (slab,), 1)
        gbuf["ffn_w_up"] = tn_matmul_to_shards(h, du, gbuf["ffn_w_up"], (slab,), 1)
        gbuf["ffn_w_down"] = tn_matmul_to_shards(a_, dyb, gbuf["ffn_w_down"], (slab,), 0)
        return dxi, got

    def reduce_in_chip(gbuf, from_sibling=None):
        names = list(gbuf)
        bufs = [gbuf[n] for n in names]
        if from_sibling is None:
            from_sibling = exchange_with_sibling(bufs).run("exchange_with_sibling")
        return names, bufs, from_sibling, [pair_sum_bf16(g, fs, "pair_sum_" + n) for n, g, fs in zip(names, bufs, from_sibling)]

    reduced, from_chips = {}, {}
    for i in reversed(range(DEPTH)):
        j = i // 2
        sv_a, sv_mix, sv_b = saved[i]
        if i == 0:
            dx, got = ffn_back(i, 1, dx, sv_b, rider=exchange_with_sibling(list(gb["early"].values())))
            reduced["early"] = reduce_in_chip(gb["early"], from_sibling=got)
        else:
            dx, _ = ffn_back(i, 1, dx, sv_b)
        if i % 2 == 0:
            rider = exchange_between_chips(reduced["early"][3]) if i == 0 else None
            dx, dnw, gm, got = _ssd_layer_bwd(dx, nw_all[i, 1], mix_w[i], sv_mix, gb["mid" if i == 0 else "early"], 0, rider=rider)
            if i == 0:
                from_chips["early"] = got
                reduced["mid"] = reduce_in_chip(gb["mid"])
        else:
            dx, dnw, gm = _sc_layer_bwd(dx, nw_all[i, 1], mix_w[i], sv_mix, gb["early"], j)
        dnorm[i][1] = dnw[0]
        for n, val in gm.items():
            grads[n][j] = val
        dx, got = ffn_back(i, 0, dx, sv_a, rider=exchange_between_chips(reduced["mid"][3]) if i == 0 else None)
        if i == 0:
            from_chips["mid"] = got

    grads["norm_w"] = jnp.stack([jnp.stack(r) for r in dnorm])
    for n in SMALL:
        if isinstance(grads[n], list):
            grads[n] = jnp.stack(grads[n])

    reduced["last"] = reduce_in_chip(gb["last"])
    from_chips["last"] = exchange_between_chips(reduced["last"][3]).run("exchange_between_chips")
    results = [{}, {}, {}, {}]
    outs = {}
    for which in ("last", "mid", "early"):
        names, bufs, from_sibling, _ = reduced[which]
        for n, g, fs, fc in zip(names, bufs, from_sibling, from_chips[which]):
            parts = [((0, 0), g), ((0,), fs), ((0,), fc), ((1,), fc), ((2,), fc)]
            outs[n] = adamw(parts, as3d(w_loc[n]), as3d(m_loc[n]), as3d(v_loc[n]), name="adamw_" + n + "_" + which,
                            a0=slabs(n, which)[0], prev=outs.get(n))
    for n in BIG:
        for k in range(4):
            results[k][n] = outs[n][k].reshape(_local_shape(n))

    g_small = _pack([grads[n].reshape(-1) for n in SMALL], 8)
    g_small = sum_over_devices(all_gather([g_small]).run("all_gather_small_grads")[0])
    g_small_full = dict(zip(SMALL, _unpack(g_small, [PARAMS[n][0] for n in SMALL])))
    g_small_loc = []
    for n in SMALL:
        if PARAMS[n][1] is None:
            g_small_loc.append(g_small_full[n])
        else:
            g_small_loc.append(lax.dynamic_index_in_dim(_by_destination(g_small_full[n], n), my_dev, axis=0, keepdims=False))
    small_shapes = [_local_shape(n) for n in SMALL]
    pack_small = lambda d: _pack([d[n].reshape(-1) for n in SMALL], 8)[None]
    small_out = adamw([_pack([gl.reshape(-1) for gl in g_small_loc], 8)[None]], pack_small(w_loc), pack_small(m_loc), pack_small(v_loc), name="adamw_small")
    for k in range(4):
        results[k].update(zip(SMALL, _unpack(small_out[k], small_shapes)))
    return (loss, dx[None], *[results[0][n] for n in NAMES], *[results[1][n] for n in NAMES],
            *[results[2][n] for n in NAMES], *[results[3][n] for n in NAMES])
```

```python
import functools

import jax
import jax.numpy as jnp
from jax import lax
from jax.experimental import pallas as pl
from jax.experimental.pallas import tpu as pltpu

f32 = jnp.float32
bf16 = jnp.bfloat16

D_MODEL = 1024
D_FF = 2816
DEPTH = 4
SSD_INNER = 2048
SSD_HEADS = 32
SSD_HEAD_DIM = 64
SSD_GROUPS = 4
SSD_STATE = 128
SSD_CONV_K = 4
SSD_CONV_DIM = 3072
SSD_IN_DIM = 5152
SSD_CHUNK = 128
SC_CONV_K = 3
RMS_EPS = 1e-5
N_DEV = 8
LANES = 128
HALO = 16
PACK_W = 1024
VMEM_LIMIT = 56 * 1024 * 1024
NEG_BIG = -1e30

ADAM_LR = 0.001
ADAM_B1 = 0.9
ADAM_B2 = 0.999
ADAM_EPS = 1e-08
ADAM_WD = 0.01
ADAM_STEP = 10

NT_DIMS = (((1,), (1,)), ((), ()))
TN_DIMS = (((0,), (0,)), ((), ()))
MESH = pl.DeviceIdType.MESH


def _params(sem=None):
    return pltpu.CompilerParams(dimension_semantics=sem, vmem_limit_bytes=VMEM_LIMIT)


def _resident(shape):
    nd = len(shape)
    return pl.BlockSpec(tuple(shape), lambda *_: (0,) * nd, pipeline_mode=pl.Buffered(1))


def _rows(tm, width):
    return pl.BlockSpec((tm, width), lambda i: (i, 0))


def _my_core_and_chip():
    return lax.axis_index("c"), 2 * lax.axis_index("x") + lax.axis_index("y")


def _sigmoid(v):
    return 0.5 * jnp.tanh(0.5 * v) + 0.5


def _softplus(v):
    return jnp.maximum(v, 0.0) + jnp.log(1.0 + jnp.exp(-jnp.abs(v)))


def _rms_fwd(xv, w):
    inv = lax.rsqrt(jnp.mean(xv * xv, axis=-1, keepdims=True) + RMS_EPS)
    xh = xv * inv
    return xh * w, xh, inv


def _rms_bwd(dh, xh, inv, w):
    dxh = dh * w
    dx = inv * (dxh - xh * jnp.mean(dxh * xh, axis=-1, keepdims=True))
    return dx, jnp.sum(dh * xh, axis=0, keepdims=True)


def _mm(a, b):
    return jnp.dot(a, b, preferred_element_type=f32)


def _mm_nt(a, b):
    return lax.dot_general(a, b, NT_DIMS, preferred_element_type=f32)


def _mm_tn(a, b):
    return lax.dot_general(a, b, TN_DIMS, preferred_element_type=f32)


def _layer_slab(w, idx):
    tail = w.shape[len(idx):]
    return pl.BlockSpec((None,) * len(idx) + tuple(tail), lambda *_: tuple(idx) + (0,) * len(tail), pipeline_mode=pl.Buffered(1))


def ffn_fwd(x, nw, wg, wu, wd, idx, tm=512, rider=None):
    T = x.shape[0]
    nt = T // tm
    r_in, r_out, r_shapes, r_scratch, r_args = _rider_specs(rider)

    def body(x_ref, nw_ref, wg_ref, wu_ref, wd_ref, xo_ref, g_ref, u_ref, a_ref):
        xv = x_ref[...]
        h, _, _ = _rms_fwd(xv, nw_ref[...])
        hb = h.astype(bf16)
        g = _mm(hb, wg_ref[...])
        u = _mm(hb, wu_ref[...])
        ab = (g * _sigmoid(g) * u).astype(bf16)
        g_ref[...] = g.astype(bf16)
        u_ref[...] = u.astype(bf16)
        a_ref[...] = ab
        xo_ref[...] = xv + 0.5 * _mm(ab, wd_ref[...])

    hosted = _carry(body, 5, 4, rider, lambda: pl.program_id(0) == 0, lambda: pl.program_id(0) == nt - 1,
                    late=lambda: pl.program_id(0) == (7 * nt) // 8)
    return pl.pallas_call(
        hosted, name="ffn_fwd" if rider is None else "ffn_fwd_carrying", grid=(nt,),
        in_specs=[_rows(tm, D_MODEL), _resident((1, D_MODEL)), _layer_slab(wg, idx), _layer_slab(wu, idx), _layer_slab(wd, idx)] + r_in,
        out_specs=[_rows(tm, D_MODEL), _rows(tm, D_FF), _rows(tm, D_FF), _rows(tm, D_FF)] + r_out,
        out_shape=[jax.ShapeDtypeStruct((T, D_MODEL), f32)] + [jax.ShapeDtypeStruct((T, D_FF), bf16)] * 3 + r_shapes,
        scratch_shapes=r_scratch,
        compiler_params=_params(("parallel",) if rider is None else ("arbitrary",)),
    )(x, nw, wg, wu, wd, *r_args)


def ffn_bwd_dx(x, dxo, g, u, nw, wg, wu, wd, idx, tm=256, rider=None):
    T = x.shape[0]
    nt = T // tm
    r_in, r_out, r_shapes, r_scratch, r_args = _rider_specs(rider)

    def body(x_ref, dxo_ref, g_ref, u_ref, nw_ref, wg_ref, wu_ref, wd_ref, dx_ref, h_ref, dy_ref, dg_ref, du_ref, dnw_ref):
        w = nw_ref[...]
        h, xh, inv = _rms_fwd(x_ref[...], w)
        dxo_v = dxo_ref[...]
        dyb = (0.5 * dxo_v).astype(bf16)
        da = _mm_nt(dyb, wd_ref[...])
        gv = g_ref[...].astype(f32)
        uv = u_ref[...].astype(f32)
        s = _sigmoid(gv)
        dgb = (da * uv * (s * (1.0 + gv * (1.0 - s)))).astype(bf16)
        dub = (da * (gv * s)).astype(bf16)
        dg_ref[...] = dgb
        du_ref[...] = dub
        dh = _mm_nt(dgb, wg_ref[...]) + _mm_nt(dub, wu_ref[...])
        dxn, dw = _rms_bwd(dh, xh, inv, w)
        dx_ref[...] = dxo_v + dxn
        h_ref[...] = h.astype(bf16)
        dy_ref[...] = dyb

        @pl.when(pl.program_id(0) == 0)
        def _():
            dnw_ref[...] = jnp.zeros_like(dnw_ref)

        dnw_ref[...] += dw

    hosted = _carry(body, 8, 6, rider, lambda: pl.program_id(0) == 0, lambda: pl.program_id(0) == nt - 1)
    return pl.pallas_call(
        hosted, name="ffn_bwd_dx" if rider is None else "ffn_bwd_dx_carrying", grid=(nt,),
        in_specs=[_rows(tm, D_MODEL), _rows(tm, D_MODEL), _rows(tm, D_FF), _rows(tm, D_FF), _resident((1, D_MODEL)),
                  _layer_slab(wg, idx), _layer_slab(wu, idx), _layer_slab(wd, idx)] + r_in,
        out_specs=[_rows(tm, D_MODEL), _rows(tm, D_MODEL), _rows(tm, D_MODEL), _rows(tm, D_FF), _rows(tm, D_FF),
                   pl.BlockSpec((1, D_MODEL), lambda i: (0, 0))] + r_out,
        out_shape=[jax.ShapeDtypeStruct((T, D_MODEL), f32), jax.ShapeDtypeStruct((T, D_MODEL), bf16), jax.ShapeDtypeStruct((T, D_MODEL), bf16),
                   jax.ShapeDtypeStruct((T, D_FF), bf16), jax.ShapeDtypeStruct((T, D_FF), bf16), jax.ShapeDtypeStruct((1, D_MODEL), f32)] + r_shapes,
        scratch_shapes=r_scratch,
        compiler_params=_params(("arbitrary",)),
    )(x, dxo, g, u, nw, wg, wu, wd, *r_args)


def tn_matmul(a, b, tk=1024):
    T, M = a.shape
    N = b.shape[1]
    bn = N if M * N <= 3_200_000 else N // 2
    nk = T // tk

    def body(a_ref, b_ref, o_ref):
        @pl.when(pl.program_id(1) == 0)
        def _():
            o_ref[...] = jnp.zeros_like(o_ref)

        o_ref[...] += _mm_tn(a_ref[...], b_ref[...])

    return pl.pallas_call(
        body, name=f"tn_matmul_{M}x{N}", grid=(N // bn, nk),
        in_specs=[pl.BlockSpec((tk, M), lambda j, k: (k, 0)), pl.BlockSpec((tk, bn), lambda j, k: (k, j))],
        out_specs=pl.BlockSpec((M, bn), lambda j, k: (0, j)),
        out_shape=jax.ShapeDtypeStruct((M, N), f32),
        compiler_params=_params(("parallel", "arbitrary")),
    )(a, b)


def tn_matmul_to_shards(a, b, buf, idx, axis):
    T, M = a.shape
    N = b.shape[1]
    m, n = buf.shape[-2:]
    tk = 1024 if M * N <= 2_200_000 else 512
    nk = T // tk
    fresh = isinstance(buf, jax.ShapeDtypeStruct)

    def body(a_ref, b_ref, *rest):
        o_ref, acc_ref = rest[-2:]
        k = pl.program_id(0)

        @pl.when(k == 0)
        def _():
            acc_ref[...] = jnp.zeros_like(acc_ref)

        acc_ref[...] += _mm_tn(a_ref[...], b_ref[...])

        @pl.when(k == nk - 1)
        def _():
            my_c, my_chip = _my_core_and_chip()
            for d in range(N_DEV):
                piece = acc_ref[:, pl.ds(d * n, n)] if axis == 1 else acc_ref[pl.ds(d * m, m), :]
                o_ref[(d % 2) ^ my_c, (d // 2) ^ my_chip] = piece

    none = (None,) * len(idx)
    return pl.pallas_call(
        body, name=f"tn_matmul_to_shards_{M}x{N}_{axis}", grid=(nk,),
        in_specs=[pl.BlockSpec((tk, M), lambda k: (k, 0)), pl.BlockSpec((tk, N), lambda k: (k, 0))] + ([] if fresh else [_ANY]),
        out_specs=pl.BlockSpec((2, 4) + none + (m, n), lambda k: (0, 0) + tuple(idx) + (0, 0)),
        out_shape=jax.ShapeDtypeStruct(buf.shape, f32),
        scratch_shapes=[pltpu.VMEM((M, N), f32)],
        input_output_aliases={} if fresh else {2: 0},
        compiler_params=_params(("arbitrary",)),
    )(a, b, *([] if fresh else [buf]))


def in_proj_fwd(x, nw, ws, out_dtypes, tm=512):
    T = x.shape[0]
    n = len(ws)

    def body(*refs):
        x_ref, nw_ref = refs[:2]
        w_refs = refs[2:2 + n]
        o_refs = refs[2 + n:]
        h, _, _ = _rms_fwd(x_ref[...], nw_ref[...])
        hb = h.astype(bf16)
        for w_ref, o_ref in zip(w_refs, o_refs):
            o_ref[...] = _mm(hb, w_ref[...]).astype(o_ref.dtype)

    return pl.pallas_call(
        body, name="in_proj_fwd_" + "_".join(str(w.shape[1]) for w in ws), grid=(T // tm,),
        in_specs=[_rows(tm, D_MODEL), _resident((1, D_MODEL))] + [_resident(w.shape) for w in ws],
        out_specs=[_rows(tm, w.shape[1]) for w in ws],
        out_shape=[jax.ShapeDtypeStruct((T, w.shape[1]), dt) for w, dt in zip(ws, out_dtypes)],
        compiler_params=_params(("parallel",)),
    )(x, nw, *ws)


def in_proj_bwd(x, nw, dxo, dys, ws, tm=512):
    T = x.shape[0]
    n = len(ws)

    def body(*refs):
        x_ref, nw_ref, dxo_ref = refs[:3]
        dy_refs = refs[3:3 + n]
        w_refs = refs[3 + n:3 + 2 * n]
        dx_ref, h_ref, dnw_ref = refs[3 + 2 * n:]
        w = nw_ref[...]
        h, xh, inv = _rms_fwd(x_ref[...], w)
        dh = _mm_nt(dy_refs[0][...], w_refs[0][...])
        for dy_ref, w_ref in zip(dy_refs[1:], w_refs[1:]):
            dh = dh + _mm_nt(dy_ref[...], w_ref[...])
        dxn, dw = _rms_bwd(dh, xh, inv, w)
        dx_ref[...] = dxo_ref[...] + dxn
        h_ref[...] = h.astype(bf16)

        @pl.when(pl.program_id(0) == 0)
        def _():
            dnw_ref[...] = jnp.zeros_like(dnw_ref)

        dnw_ref[...] += dw

    return pl.pallas_call(
        body, name="in_proj_bwd_" + "_".join(str(w.shape[1]) for w in ws), grid=(T // tm,),
        in_specs=[_rows(tm, D_MODEL), _resident((1, D_MODEL)), _rows(tm, D_MODEL)] + [_rows(tm, w.shape[1]) for w in ws]
        + [_resident(w.shape) for w in ws],
        out_specs=[_rows(tm, D_MODEL), _rows(tm, D_MODEL), pl.BlockSpec((1, D_MODEL), lambda i: (0, 0))],
        out_shape=[jax.ShapeDtypeStruct((T, D_MODEL), f32), jax.ShapeDtypeStruct((T, D_MODEL), bf16), jax.ShapeDtypeStruct((1, D_MODEL), f32)],
        compiler_params=_params(("arbitrary",)),
    )(x, nw, dxo, *dys, *ws)


def out_proj_fwd(x, a, w, tm=1024):
    T = x.shape[0]
    K = a.shape[1]

    def body(x_ref, a_ref, w_ref, o_ref):
        o_ref[...] = x_ref[...] + _mm(a_ref[...], w_ref[...])

    return pl.pallas_call(
        body, name=f"out_proj_fwd_{K}", grid=(T // tm,),
        in_specs=[_rows(tm, D_MODEL), _rows(tm, K), _resident(w.shape)],
        out_specs=_rows(tm, D_MODEL), out_shape=jax.ShapeDtypeStruct((T, D_MODEL), f32),
        compiler_params=_params(("parallel",)),
    )(x, a, w)


def _halo_spec(tm, width, n_tiles, reverse):
    per = tm // HALO

    def idx(i):
        t = (n_tiles - 1 - i) if reverse else i
        return (jnp.maximum(t * per - 1, 0), 0)

    return pl.BlockSpec((HALO, width), idx)


def _tile_spec(tm, width, n_tiles, reverse):
    if reverse:
        return pl.BlockSpec((tm, width), lambda i: (n_tiles - 1 - i, 0))
    return _rows(tm, width)


ROW_BLOCK = 64


def _strip(s):
    return pl.ds(pl.multiple_of(s * LANES, LANES), LANES)


def _conv_rows(ext_ref, w_ref, cols, k_w, r0):
    base = HALO - (k_w - 1) + r0
    wins = [ext_ref[pl.ds(base + k, ROW_BLOCK), :] for k in range(k_w)]
    out = w_ref[pl.ds(0, 1), cols] * wins[0]
    for k in range(1, k_w):
        out = out + w_ref[pl.ds(k, 1), cols] * wins[k]
    return out, wins


def _shifted_back(d_ref, w_ref, cols, k_w, r0):
    out = w_ref[pl.ds(0, 1), cols] * d_ref[pl.ds(r0 + k_w - 1, ROW_BLOCK), :]
    for k in range(1, k_w):
        out = out + w_ref[pl.ds(k, 1), cols] * d_ref[pl.ds(r0 + k_w - 1 - k, ROW_BLOCK), :]
    return out


def ssd_conv_fwd(xbc, conv_w, conv_b, dt_raw, dt_bias, tm=512):
    T = xbc.shape[0]
    nt = T // tm
    K = SSD_CONV_K

    def body(x_ref, halo_ref, w_ref, b_ref, dtr_ref, dtb_ref, act_ref, dt_ref, ext_ref):
        first = pl.program_id(0) == 0

        def strip(s, carry):
            cols = _strip(s)
            ext_ref[pl.ds(0, HALO), :] = jnp.where(first, 0.0, halo_ref[:, cols].astype(f32))
            ext_ref[pl.ds(HALO, tm), :] = x_ref[:, cols].astype(f32)
            for r0 in range(0, tm, ROW_BLOCK):
                pre, _ = _conv_rows(ext_ref, w_ref, cols, K, r0)
                pre = pre + b_ref[:, cols]
                act_ref[pl.ds(r0, ROW_BLOCK), cols] = (pre * _sigmoid(pre)).astype(bf16)
            return carry

        lax.fori_loop(0, SSD_CONV_DIM // LANES, strip, 0)
        dt = _softplus(dtr_ref[...] + dtb_ref[...])
        lane = lax.broadcasted_iota(jnp.int32, (1, LANES), 1)
        for g in range(SSD_GROUPS):
            dt_ref[g] = jnp.where(lane < 8, dt if g == 0 else pltpu.roll(dt, LANES - 8 * g, axis=1), 0.0)

    return pl.pallas_call(
        body, name="ssd_conv_fwd", grid=(nt,),
        in_specs=[_rows(tm, SSD_CONV_DIM), _halo_spec(tm, SSD_CONV_DIM, nt, False), _resident(conv_w.shape), _resident(conv_b.shape),
                  _rows(tm, LANES), _resident(dt_bias.shape)],
        out_specs=[_rows(tm, SSD_CONV_DIM), pl.BlockSpec((SSD_GROUPS, tm, LANES), lambda i: (0, i, 0))],
        out_shape=[jax.ShapeDtypeStruct((T, SSD_CONV_DIM), bf16), jax.ShapeDtypeStruct((SSD_GROUPS, T, LANES), f32)],
        scratch_shapes=[pltpu.VMEM((tm + HALO, LANES), f32)],
        compiler_params=_params(("parallel",)),
    )(xbc, xbc, conv_w, conv_b, dt_raw, dt_bias)


def ssd_conv_bwd(xbc, conv_w, conv_b, dt_raw, dt_bias, dxs_a, dxs_b, db, dc, ddt, tm=512):
    T = xbc.shape[0]
    nt = T // tm
    K = SSD_CONV_K

    def body(x_ref, halo_ref, w_ref, b_ref, dtr_ref, dtb_ref, da_ref, dbb_ref, db_ref, dc_ref, ddt_ref,
             dx_ref, ddtr_ref, dw_ref, dbias_ref, ddtb_ref, ext_ref, dpre_ref, carry_ref):
        i = pl.program_id(0)

        @pl.when(i == 0)
        def _():
            carry_ref[...] = jnp.zeros_like(carry_ref)
            dw_ref[...] = jnp.zeros_like(dw_ref)
            dbias_ref[...] = jnp.zeros_like(dbias_ref)
            ddtb_ref[...] = jnp.zeros_like(ddtb_ref)

        first_tile = i == nt - 1

        def run_strips(lo, hi, load_dact):
            def strip(s, carry):
                cols = _strip(s)
                ext_ref[pl.ds(0, HALO), :] = jnp.where(first_tile, 0.0, halo_ref[:, cols].astype(f32))
                ext_ref[pl.ds(HALO, tm), :] = x_ref[:, cols].astype(f32)
                dpre_ref[pl.ds(tm, 8), :] = carry_ref[:, cols]
                bias = b_ref[:, cols]
                dws = [jnp.zeros((1, LANES), f32) for _ in range(K)]
                dbs = jnp.zeros((1, LANES), f32)
                for r0 in range(0, tm, ROW_BLOCK):
                    pre, wins = _conv_rows(ext_ref, w_ref, cols, K, r0)
                    pre = pre + bias
                    sg = _sigmoid(pre)
                    dpre = load_dact(s, r0) * (sg * (1.0 + pre * (1.0 - sg)))
                    dpre_ref[pl.ds(r0, ROW_BLOCK), :] = dpre
                    dbs = dbs + jnp.sum(dpre, axis=0, keepdims=True)
                    for k in range(K):
                        dws[k] = dws[k] + jnp.sum(dpre * wins[k], axis=0, keepdims=True)
                carry_ref[:, cols] = dpre_ref[pl.ds(0, 8), :]
                for r0 in range(0, tm, ROW_BLOCK):
                    dx_ref[pl.ds(r0, ROW_BLOCK), cols] = _shifted_back(dpre_ref, w_ref, cols, K, r0).astype(bf16)
                for k in range(K):
                    dw_ref[pl.ds(k, 1), cols] += dws[k]
                dbias_ref[:, cols] += dbs
                return carry

            lax.fori_loop(lo, hi, strip, 0)

        rows = lambda r0: pl.ds(r0, ROW_BLOCK)
        n_x = SSD_INNER // LANES
        n_g = SSD_GROUPS * SSD_STATE // LANES
        run_strips(0, n_x, lambda s, r0: da_ref[rows(r0), _strip(s)].astype(f32) + dbb_ref[rows(r0), _strip(s)].astype(f32))
        run_strips(n_x, n_x + n_g, lambda s, r0: db_ref[rows(r0), _strip(s - n_x)].astype(f32))
        run_strips(n_x + n_g, n_x + 2 * n_g, lambda s, r0: dc_ref[rows(r0), _strip(s - n_x - n_g)].astype(f32))
        lane = lax.broadcasted_iota(jnp.int32, (1, LANES), 1)
        ddt = jnp.where(lane < 8, ddt_ref[0], 0.0)
        for g in range(1, SSD_GROUPS):
            ddt = ddt + pltpu.roll(jnp.where(lane < 8, ddt_ref[g], 0.0), 8 * g, axis=1)
        ddtr = ddt * _sigmoid(dtr_ref[...] + dtb_ref[...])
        ddtr_ref[...] = ddtr.astype(bf16)
        ddtb_ref[...] += jnp.sum(ddtr, axis=0, keepdims=True)

    rev = functools.partial(_tile_spec, tm, n_tiles=nt, reverse=True)
    const = lambda shape: pl.BlockSpec(shape, lambda i: (0, 0))
    return pl.pallas_call(
        body, name="ssd_conv_bwd", grid=(nt,),
        in_specs=[rev(width=SSD_CONV_DIM), _halo_spec(tm, SSD_CONV_DIM, nt, True), _resident(conv_w.shape), _resident(conv_b.shape),
                  rev(width=LANES), _resident(dt_bias.shape), rev(width=SSD_INNER), rev(width=SSD_INNER),
                  rev(width=SSD_GROUPS * SSD_STATE), rev(width=SSD_GROUPS * SSD_STATE),
                  pl.BlockSpec((SSD_GROUPS, tm, LANES), lambda i: (0, nt - 1 - i, 0))],
        out_specs=[rev(width=SSD_CONV_DIM), rev(width=LANES), const((8, SSD_CONV_DIM)), const((1, SSD_CONV_DIM)), const((1, LANES))],
        out_shape=[jax.ShapeDtypeStruct((T, SSD_CONV_DIM), bf16), jax.ShapeDtypeStruct((T, LANES), bf16),
                   jax.ShapeDtypeStruct((8, SSD_CONV_DIM), f32), jax.ShapeDtypeStruct((1, SSD_CONV_DIM), f32), jax.ShapeDtypeStruct((1, LANES), f32)],
        scratch_shapes=[pltpu.VMEM((tm + HALO, LANES), f32), pltpu.VMEM((tm + 8, LANES), f32), pltpu.VMEM((8, SSD_CONV_DIM), f32)],
        compiler_params=_params(("arbitrary",)),
    )(xbc, xbc, conv_w, conv_b, dt_raw, dt_bias, dxs_a, dxs_b, db, dc, ddt)


def _ssd_chunk(xs, bm, cm, dt, alog, st):
    L = SSD_CHUNK
    row = lax.broadcasted_iota(jnp.int32, (L, L), 0)
    col = lax.broadcasted_iota(jnp.int32, (L, L), 1)
    causal = row >= col
    tril = jnp.where(causal, 1.0, 0.0).astype(f32)
    lane = lax.broadcasted_iota(jnp.int32, (1, LANES), 1)
    sub = lax.broadcasted_iota(jnp.int32, (LANES, 1), 0)
    lo = lane < SSD_HEAD_DIM
    last_row = sub == L - 1

    dta = dt * (-jnp.exp(alog))
    a_cs = jnp.dot(tril, dta, precision=lax.Precision.HIGHEST, preferred_element_type=f32)
    a_cs_t = a_cs.T
    bmb = bm.astype(bf16)
    cmb = cm.astype(bf16)
    cb = _mm_nt(cmb, bmb)
    c_st = _mm(cmb, st.astype(bf16))

    def head_col(v, e):
        return jnp.sum(jnp.where(lane == e, v, 0.0), axis=1, keepdims=True)

    def head_row(v, e):
        return jnp.sum(jnp.where(sub == e, v, 0.0), axis=0, keepdims=True)

    ys, sts = [], []
    for j in range(4):
        e0, e1 = 2 * j, 2 * j + 1
        c0, c1 = head_col(a_cs, e0), head_col(a_cs, e1)
        acs_x = jnp.where(lo, c0, c1)
        dt_x = jnp.where(lo, head_col(dt, e0), head_col(dt, e1))
        xd = xs[:, j * LANES:(j + 1) * LANES] * dt_x
        m0 = cb * jnp.exp(jnp.where(causal, c0 - head_row(a_cs_t, e0), NEG_BIG))
        m1 = cb * jnp.exp(jnp.where(causal, c1 - head_row(a_cs_t, e1), NEG_BIG))
        mcat = jnp.concatenate([m0, m1], axis=1).astype(bf16)
        xcat = jnp.concatenate([jnp.where(lo, xd, 0.0), jnp.where(lo, 0.0, xd)], axis=0).astype(bf16)
        y_diag = _mm(mcat, xcat)
        a_last = jnp.sum(jnp.where(last_row, acs_x, 0.0), axis=0, keepdims=True)
        x_dec = (xd * jnp.exp(a_last - acs_x)).astype(bf16)
        s_new = _mm_tn(bmb, x_dec)
        y_off = c_st[:, j * LANES:(j + 1) * LANES] * jnp.exp(acs_x)
        ys.append(y_diag + y_off)
        sts.append(jnp.exp(a_last) * st[:, j * LANES:(j + 1) * LANES] + s_new)
    return jnp.concatenate(ys, axis=1), jnp.concatenate(sts, axis=1)


SCAN_GROUPS_FWD = 4
SCAN_GROUPS_BWD = 1


def _scan_specs(nc, reverse, gs):
    L = SSD_CHUNK
    ch = (lambda c: nc - 1 - c) if reverse else (lambda c: c)
    gw = SSD_INNER // SSD_GROUPS
    b0 = SSD_INNER // (gs * SSD_STATE)
    c0 = (SSD_INNER + SSD_GROUPS * SSD_STATE) // (gs * SSD_STATE)
    xs = pl.BlockSpec((L, gs * gw), lambda g, c: (ch(c), g))
    bm = pl.BlockSpec((L, gs * SSD_STATE), lambda g, c: (ch(c), b0 + g))
    cm = pl.BlockSpec((L, gs * SSD_STATE), lambda g, c: (ch(c), c0 + g))
    dt = pl.BlockSpec((gs, L, LANES), lambda g, c: (g, ch(c), 0))
    alog = pl.BlockSpec((gs, 1, LANES), lambda g, c: (g, 0, 0))
    st = pl.BlockSpec((gs, None, SSD_STATE, gw), lambda g, c: (g, ch(c), 0, 0))
    y = pl.BlockSpec((L, gs * gw), lambda g, c: (ch(c), g))
    grp = pl.BlockSpec((L, gs * SSD_STATE), lambda g, c: (ch(c), g))
    return xs, bm, cm, dt, alog, st, y, grp


def ssd_scan_fwd(act, dt4, alog4, rider=None):
    T = act.shape[0]
    nc = T // SSD_CHUNK
    gs = SCAN_GROUPS_FWD
    ng = SSD_GROUPS // gs
    gw = SSD_INNER // SSD_GROUPS
    xs_s, bm_s, cm_s, dt_s, alog_s, st_s, y_s, _ = _scan_specs(nc, False, gs)
    r_in, r_out, r_shapes, r_scratch, r_args = _rider_specs(rider)

    def body(xs_ref, bm_ref, cm_ref, dt_ref, alog_ref, y_ref, st_ref, st_scr):
        @pl.when(pl.program_id(1) == 0)
        def _():
            st_scr[...] = jnp.zeros_like(st_scr)

        for q in range(gs):
            xc, gc = pl.ds(q * gw, gw), pl.ds(q * SSD_STATE, SSD_STATE)
            st = st_scr[q]
            st_ref[q] = st
            y, st_new = _ssd_chunk(xs_ref[:, xc].astype(f32), bm_ref[:, gc].astype(f32), cm_ref[:, gc].astype(f32), dt_ref[q], alog_ref[q], st)
            y_ref[:, xc] = y.astype(bf16)
            st_scr[q] = st_new

    first = lambda: jnp.logical_and(pl.program_id(0) == 0, pl.program_id(1) == 0)
    last = lambda: jnp.logical_and(pl.program_id(0) == ng - 1, pl.program_id(1) == nc - 1)
    late = lambda: jnp.logical_and(pl.program_id(0) == ng - 1, pl.program_id(1) == (7 * nc) // 8)
    return pl.pallas_call(
        _carry(body, 5, 2, rider, first, last, late), name="ssd_scan_fwd" if rider is None else "ssd_scan_fwd_carrying", grid=(ng, nc),
        in_specs=[xs_s, bm_s, cm_s, dt_s, alog_s] + r_in, out_specs=[y_s, st_s] + r_out,
        out_shape=[jax.ShapeDtypeStruct((T, SSD_INNER), bf16), jax.ShapeDtypeStruct((SSD_GROUPS, nc, SSD_STATE, gw), f32)] + r_shapes,
        scratch_shapes=[pltpu.VMEM((gs, SSD_STATE, gw), f32)] + r_scratch,
        compiler_params=_params(("parallel" if rider is None else "arbitrary", "arbitrary")),
    )(act, act, act, dt4, alog4, *r_args)


def ssd_scan_bwd(act, dt4, alog4, states, dy, rider=None):
    T = act.shape[0]
    nc = T // SSD_CHUNK
    gs = SCAN_GROUPS_BWD
    ng = SSD_GROUPS // gs
    gw = SSD_INNER // SSD_GROUPS
    xs_s, bm_s, cm_s, dt_s, alog_s, st_s, y_s, grp_s = _scan_specs(nc, True, gs)
    r_in, r_out, r_shapes, r_scratch, r_args = _rider_specs(rider)

    def body(xs_ref, bm_ref, cm_ref, dt_ref, alog_ref, st_ref, dy_ref, dxs_ref, db_ref, dc_ref, ddt_ref, dalog_ref, dst_scr):
        @pl.when(pl.program_id(1) == 0)
        def _():
            dst_scr[...] = jnp.zeros_like(dst_scr)
            dalog_ref[...] = jnp.zeros_like(dalog_ref)

        for q in range(gs):
            xc, gc = pl.ds(q * gw, gw), pl.ds(q * SSD_STATE, SSD_STATE)
            _, vjp = jax.vjp(_ssd_chunk, xs_ref[:, xc].astype(f32), bm_ref[:, gc].astype(f32), cm_ref[:, gc].astype(f32),
                             dt_ref[q], alog_ref[q], st_ref[q])
            dxs, dbm, dcm, ddt, dalog, dst = vjp((dy_ref[:, xc].astype(f32), dst_scr[q]))
            dxs_ref[:, xc] = dxs.astype(bf16)
            db_ref[:, gc] = dbm.astype(bf16)
            dc_ref[:, gc] = dcm.astype(bf16)
            ddt_ref[q] = ddt
            dalog_ref[q] += dalog
            dst_scr[q] = dst

    first = lambda: jnp.logical_and(pl.program_id(0) == 0, pl.program_id(1) == 0)
    last = lambda: jnp.logical_and(pl.program_id(0) == ng - 1, pl.program_id(1) == nc - 1)
    return pl.pallas_call(
        _carry(body, 7, 5, rider, first, last), name="ssd_scan_bwd" if rider is None else "ssd_scan_bwd_carrying", grid=(ng, nc),
        in_specs=[xs_s, bm_s, cm_s, dt_s, alog_s, st_s, y_s] + r_in,
        out_specs=[y_s, grp_s, grp_s, dt_s, alog_s] + r_out,
        out_shape=[jax.ShapeDtypeStruct((T, SSD_INNER), bf16), jax.ShapeDtypeStruct((T, SSD_GROUPS * SSD_STATE), bf16),
                   jax.ShapeDtypeStruct((T, SSD_GROUPS * SSD_STATE), bf16), jax.ShapeDtypeStruct((SSD_GROUPS, T, LANES), f32),
                   jax.ShapeDtypeStruct((SSD_GROUPS, 1, LANES), f32)] + r_shapes,
        scratch_shapes=[pltpu.VMEM((gs, SSD_STATE, gw), f32)] + r_scratch,
        compiler_params=_params(("parallel" if rider is None else "arbitrary", "arbitrary")),
    )(act, act, act, dt4, alog4, states, dy, *r_args)


GATE_ROWS = 256


def _ssd_gate(y, xs, z, d_x, nw):
    g = (y + xs * d_x) * (z * _sigmoid(z))
    return g * lax.rsqrt(jnp.mean(g * g, axis=-1, keepdims=True) + RMS_EPS) * nw


def _gate_blocks(tm, fn):
    gw = SSD_INNER // SSD_GROUPS

    def block(r, carry):
        rows = pl.ds(r * GATE_ROWS if isinstance(r, int) else pl.multiple_of(r * GATE_ROWS, GATE_ROWS), GATE_ROWS)
        for k in range(SSD_GROUPS):
            fn(rows, pl.ds(k * gw, gw))
        return carry

    if tm == GATE_ROWS:
        block(0, 0)
    else:
        lax.fori_loop(0, tm // GATE_ROWS, block, 0)


def ssd_gate_fwd(y, act, z, d_x, nw, tm=512):
    T = y.shape[0]

    def body(y_ref, xs_ref, z_ref, d_ref, nw_ref, o_ref):
        def one(rows, cols):
            o_ref[rows, cols] = _ssd_gate(y_ref[rows, cols].astype(f32), xs_ref[rows, cols].astype(f32), z_ref[rows, cols].astype(f32),
                                          d_ref[:, cols], nw_ref[:, cols]).astype(bf16)

        _gate_blocks(tm, one)

    return pl.pallas_call(
        body, name="ssd_gate_fwd", grid=(T // tm,),
        in_specs=[_rows(tm, SSD_INNER), _rows(tm, SSD_INNER), _rows(tm, SSD_INNER), _resident(d_x.shape), _resident(nw.shape)],
        out_specs=_rows(tm, SSD_INNER), out_shape=jax.ShapeDtypeStruct((T, SSD_INNER), bf16),
        compiler_params=_params(("parallel",)),
    )(y, act, z, d_x, nw)


def ssd_gate_bwd(y, act, z, d_x, nw, dxo, w_out, tm=512):
    T = y.shape[0]

    def body(y_ref, xs_ref, z_ref, d_ref, nw_ref, dxo_ref, w_ref, dy_ref, dxs_ref, dz_ref, dd_ref, dnw_ref, dyb_ref):
        @pl.when(pl.program_id(0) == 0)
        def _():
            dd_ref[...] = jnp.zeros_like(dd_ref)
            dnw_ref[...] = jnp.zeros_like(dnw_ref)

        dyb_ref[...] = dxo_ref[...].astype(bf16)

        def one(rows, cols):
            _, vjp = jax.vjp(_ssd_gate, y_ref[rows, cols].astype(f32), xs_ref[rows, cols].astype(f32), z_ref[rows, cols].astype(f32),
                             d_ref[:, cols], nw_ref[:, cols])
            dy, dxs, dz, dd, dnw = vjp(_mm_nt(dyb_ref[rows, :], w_ref[cols, :]))
            dy_ref[rows, cols] = dy.astype(bf16)
            dxs_ref[rows, cols] = dxs.astype(bf16)
            dz_ref[rows, cols] = dz.astype(bf16)
            dd_ref[:, cols] += dd
            dnw_ref[:, cols] += dnw

        _gate_blocks(tm, one)

    const = pl.BlockSpec((1, SSD_INNER), lambda i: (0, 0))
    return pl.pallas_call(
        body, name="ssd_gate_bwd", grid=(T // tm,),
        in_specs=[_rows(tm, SSD_INNER), _rows(tm, SSD_INNER), _rows(tm, SSD_INNER), _resident(d_x.shape), _resident(nw.shape),
                  _rows(tm, D_MODEL), _resident(w_out.shape)],
        out_specs=[_rows(tm, SSD_INNER)] * 3 + [const, const, _rows(tm, D_MODEL)],
        out_shape=[jax.ShapeDtypeStruct((T, SSD_INNER), bf16)] * 3 + [jax.ShapeDtypeStruct((1, SSD_INNER), f32)] * 2
        + [jax.ShapeDtypeStruct((T, D_MODEL), bf16)],
        compiler_params=_params(("arbitrary",)),
    )(y, act, z, d_x, nw, dxo, w_out)


def sc_mid_fwd(bcu, conv_w, tm=512):
    T = bcu.shape[0]
    nt = T // tm
    Dm = D_MODEL

    def body(x_ref, halo_ref, w_ref, q_ref, ext_ref):
        first = pl.program_id(0) == 0
        n_s = Dm // LANES

        def strip(s, carry):
            cols, c_cols, u_cols = _strip(s), _strip(s + n_s), _strip(s + 2 * n_s)
            ext_ref[pl.ds(0, HALO), :] = jnp.where(first, 0.0, halo_ref[:, c_cols].astype(f32) * halo_ref[:, u_cols].astype(f32))
            ext_ref[pl.ds(HALO, tm), :] = x_ref[:, c_cols].astype(f32) * x_ref[:, u_cols].astype(f32)
            for r0 in range(0, tm, ROW_BLOCK):
                rows = pl.ds(r0, ROW_BLOCK)
                v, _ = _conv_rows(ext_ref, w_ref, cols, SC_CONV_K, r0)
                q_ref[rows, cols] = (x_ref[rows, cols].astype(f32) * v).astype(bf16)
            return carry

        lax.fori_loop(0, n_s, strip, 0)

    return pl.pallas_call(
        body, name="sc_mid_fwd", grid=(nt,),
        in_specs=[_rows(tm, 3 * Dm), _halo_spec(tm, 3 * Dm, nt, False), _resident(conv_w.shape)],
        out_specs=_rows(tm, Dm), out_shape=jax.ShapeDtypeStruct((T, Dm), bf16),
        scratch_shapes=[pltpu.VMEM((tm + HALO, LANES), f32)],
        compiler_params=_params(("parallel",)),
    )(bcu, bcu, conv_w)


def sc_mid_bwd(bcu, conv_w, dxo, w_out, tm=512):
    T = bcu.shape[0]
    nt = T // tm
    Dm = D_MODEL
    K = SC_CONV_K

    def body(x_ref, halo_ref, w_ref, dxo_ref, wo_ref, dx_ref, dw_ref, dyb_ref, ext_ref, dv_ref, carry_ref, dq_ref):
        i = pl.program_id(0)

        @pl.when(i == 0)
        def _():
            carry_ref[...] = jnp.zeros_like(carry_ref)
            dw_ref[...] = jnp.zeros_like(dw_ref)

        dyb = dxo_ref[...].astype(bf16)
        dyb_ref[...] = dyb
        dq_ref[...] = _mm_nt(dyb, wo_ref[...])
        first_tile = i == nt - 1
        n_s = Dm // LANES

        def strip(s, carry):
            cols, c_cols, u_cols = _strip(s), _strip(s + n_s), _strip(s + 2 * n_s)
            ext_ref[pl.ds(0, HALO), :] = jnp.where(first_tile, 0.0, halo_ref[:, c_cols].astype(f32) * halo_ref[:, u_cols].astype(f32))
            ext_ref[pl.ds(HALO, tm), :] = x_ref[:, c_cols].astype(f32) * x_ref[:, u_cols].astype(f32)
            dv_ref[pl.ds(tm, 8), :] = carry_ref[:, cols]
            dws = [jnp.zeros((1, LANES), f32) for _ in range(K)]
            for r0 in range(0, tm, ROW_BLOCK):
                rows = pl.ds(r0, ROW_BLOCK)
                v, wins = _conv_rows(ext_ref, w_ref, cols, K, r0)
                dqv = dq_ref[rows, cols]
                dv = dqv * x_ref[rows, cols].astype(f32)
                dv_ref[rows, :] = dv
                dx_ref[rows, cols] = (dqv * v).astype(bf16)
                for k in range(K):
                    dws[k] = dws[k] + jnp.sum(dv * wins[k], axis=0, keepdims=True)
            carry_ref[:, cols] = dv_ref[pl.ds(0, 8), :]
            for r0 in range(0, tm, ROW_BLOCK):
                rows = pl.ds(r0, ROW_BLOCK)
                dp = _shifted_back(dv_ref, w_ref, cols, K, r0)
                dx_ref[rows, c_cols] = (dp * x_ref[rows, u_cols].astype(f32)).astype(bf16)
                dx_ref[rows, u_cols] = (dp * x_ref[rows, c_cols].astype(f32)).astype(bf16)
            for k in range(K):
                dw_ref[pl.ds(k, 1), cols] += dws[k]
            return carry

        lax.fori_loop(0, n_s, strip, 0)

    return pl.pallas_call(
        body, name="sc_mid_bwd", grid=(nt,),
        in_specs=[_tile_spec(tm, 3 * Dm, nt, True), _halo_spec(tm, 3 * Dm, nt, True), _resident(conv_w.shape), _tile_spec(tm, Dm, nt, True),
                  _resident(w_out.shape)],
        out_specs=[_tile_spec(tm, 3 * Dm, nt, True), pl.BlockSpec((8, Dm), lambda i: (0, 0)), _tile_spec(tm, Dm, nt, True)],
        out_shape=[jax.ShapeDtypeStruct((T, 3 * Dm), bf16), jax.ShapeDtypeStruct((8, Dm), f32), jax.ShapeDtypeStruct((T, Dm), bf16)],
        scratch_shapes=[pltpu.VMEM((tm + HALO, LANES), f32), pltpu.VMEM((tm + 8, LANES), f32), pltpu.VMEM((8, Dm), f32),
                        pltpu.VMEM((tm, Dm), f32)],
        compiler_params=_params(("arbitrary",)),
    )(bcu, bcu, conv_w, dxo, w_out)


def loss_head(x, fw, target, tm=1024):
    T = x.shape[0]

    def body(x_ref, fw_ref, t_ref, loss_ref, dx_ref, dfw_ref):
        @pl.when(pl.program_id(0) == 0)
        def _():
            loss_ref[...] = jnp.zeros_like(loss_ref)
            dfw_ref[...] = jnp.zeros_like(dfw_ref)

        w = fw_ref[...]
        y, xh, inv = _rms_fwd(x_ref[...], w)
        err = y - t_ref[...]
        loss_ref[...] += 0.5 * jnp.sum(jnp.mean(err * err, axis=-1, keepdims=True), axis=0, keepdims=True)
        dx, dw = _rms_bwd(err * (1.0 / D_MODEL), xh, inv, w)
        dx_ref[...] = dx
        dfw_ref[...] += dw

    return pl.pallas_call(
        body, name="loss_head", grid=(T // tm,),
        in_specs=[_rows(tm, D_MODEL), _resident((1, D_MODEL)), _rows(tm, D_MODEL)],
        out_specs=[pl.BlockSpec((1, LANES), lambda i: (0, 0)), _rows(tm, D_MODEL), pl.BlockSpec((1, D_MODEL), lambda i: (0, 0))],
        out_shape=[jax.ShapeDtypeStruct((1, LANES), f32), jax.ShapeDtypeStruct((T, D_MODEL), f32), jax.ShapeDtypeStruct((1, D_MODEL), f32)],
        compiler_params=_params(("arbitrary",)),
    )(x, fw, target)


ELEMENTWISE_TILE_BYTES = 1_600_000


def _row_tile(rows, width):
    row_bytes = 4 * _round_up(width, LANES)
    tile = rows
    while tile * row_bytes > ELEMENTWISE_TILE_BYTES and tile % 16 == 0:
        tile //= 2
    return tile


def adamw(g_parts, w, m, v, name="adamw", a0=0, prev=None):
    A, B, n = w.shape
    tb = _row_tile(B, n)
    n_parts = len(g_parts)
    arrays, specs = [], []
    for part in g_parts:
        lead, arr = part if isinstance(part, tuple) else ((), part)
        specs.append(pl.BlockSpec((None,) * (len(lead) + 1) + (tb, n), lambda a, t, lead=lead: tuple(lead) + (a, t, 0)))
        arrays.append(arr)
    na = arrays[0].shape[-3]
    prev = list(prev) if prev is not None else []

    def body(*refs):
        n = n_parts
        g_refs = refs[:n]
        w_ref, m_ref, v_ref = refs[n:n + 3]
        go_ref, d_ref, mo_ref, vo_ref = refs[n + 3 + len(prev):]
        g = g_refs[0][...].astype(f32)
        for r in g_refs[1:]:
            g = g + r[...].astype(f32)
        m_new = ADAM_B1 * m_ref[...] + (1.0 - ADAM_B1) * g
        v_new = ADAM_B2 * v_ref[...] + (1.0 - ADAM_B2) * (g * g)
        m_hat = m_new / (1.0 - ADAM_B1 ** ADAM_STEP)
        v_hat = v_new / (1.0 - ADAM_B2 ** ADAM_STEP)
        go_ref[...] = g
        d_ref[...] = -ADAM_LR * (m_hat / (jnp.sqrt(v_hat) + ADAM_EPS) + ADAM_WD * w_ref[...])
        mo_ref[...] = m_new
        vo_ref[...] = v_new

    plain = pl.BlockSpec((None, tb, n), lambda a, t: (a + a0, t, 0))
    return pl.pallas_call(
        body, name=name, grid=(na, B // tb), in_specs=specs + [plain] * 3 + [_ANY] * len(prev), out_specs=[plain] * 4,
        out_shape=[jax.ShapeDtypeStruct((A, B, n), f32)] * 4,
        input_output_aliases={n_parts + 3 + k: k for k in range(len(prev))},
        compiler_params=_params(("parallel", "parallel")),
    )(*arrays, w, m, v, *prev)


def pair_sum_bf16(ga, gb, name):
    _, A, B, n = gb.shape
    tb = _row_tile(B, n)

    def body(a_ref, b_ref, o_ref):
        o_ref[...] = (a_ref[...] + b_ref[...]).astype(bf16)

    return pl.pallas_call(
        body, name=name, grid=(3, A, B // tb),
        in_specs=[pl.BlockSpec((None, None, None, tb, n), lambda j, a, t: (0, j + 1, a, t, 0)),
                  pl.BlockSpec((None, None, tb, n), lambda j, a, t: (j + 1, a, t, 0))],
        out_specs=pl.BlockSpec((None, None, tb, n), lambda j, a, t: (j + 1, a, t, 0)),
        out_shape=jax.ShapeDtypeStruct((4, A, B, n), bf16),
        compiler_params=_params(("parallel", "parallel", "parallel")),
    )(ga, gb)


def assemble(gathered, axis, tk=256):
    _, A, K, n = gathered.shape
    if axis == 1:
        def body(w_ref, o_ref):
            o_ref[...] = jnp.concatenate([w_ref[j] for j in range(N_DEV)], axis=1)

        return pl.pallas_call(
            body, name=f"assemble_cols_{K}x{n}", grid=(A, K // tk),
            in_specs=[pl.BlockSpec((N_DEV, None, tk, n), lambda a, t: (0, a, t, 0))],
            out_specs=pl.BlockSpec((None, tk, N_DEV * n), lambda a, t: (a, t, 0)),
            out_shape=jax.ShapeDtypeStruct((A, K, N_DEV * n), gathered.dtype),
            compiler_params=_params(("parallel", "parallel")),
        )(gathered)

    def body(w_ref, o_ref):
        for j in range(N_DEV):
            o_ref[pl.ds(j * K, K), :] = w_ref[j]

    return pl.pallas_call(
        body, name=f"assemble_rows_{K}x{n}", grid=(A,),
        in_specs=[pl.BlockSpec((N_DEV, None, K, n), lambda a: (0, a, 0, 0))],
        out_specs=pl.BlockSpec((None, N_DEV * K, n), lambda a: (a, 0, 0)),
        out_shape=jax.ShapeDtypeStruct((A, N_DEV * K, n), gathered.dtype),
        compiler_params=_params(("parallel",)),
    )(gathered)


SSD_IN_PAD = -(-SSD_IN_DIM // LANES) * LANES


def assemble_ssd_in(gathered, tk=256):
    _, A, K, n = gathered.shape

    def body(w_ref, z_ref, x_ref, dt_ref, full_ref):
        full_ref[:, pl.ds(SSD_IN_PAD - LANES, LANES)] = jnp.zeros((tk, LANES), gathered.dtype)
        for j in range(N_DEV):
            full_ref[:, pl.ds(j * n, n)] = w_ref[j]
        z_ref[...] = full_ref[:, pl.ds(0, SSD_INNER)]
        x_ref[...] = full_ref[:, pl.ds(SSD_INNER, SSD_CONV_DIM)]
        dt_ref[...] = full_ref[:, pl.ds(SSD_INNER + SSD_CONV_DIM, LANES)]

    widths = (SSD_INNER, SSD_CONV_DIM, LANES)
    return pl.pallas_call(
        body, name="assemble_ssd_in", grid=(A, K // tk),
        in_specs=[pl.BlockSpec((N_DEV, None, tk, n), lambda a, t: (0, a, t, 0))],
        out_specs=[pl.BlockSpec((None, tk, w), lambda a, t: (a, t, 0)) for w in widths],
        out_shape=[jax.ShapeDtypeStruct((A, K, w), gathered.dtype) for w in widths],
        scratch_shapes=[pltpu.VMEM((tk, SSD_IN_PAD), gathered.dtype)],
        compiler_params=_params(("parallel", "parallel")),
    )(gathered)


def ssd_in_to_shards(dwz, dwx, dwdt, buf, j, tk=256):
    K = dwz.shape[0]
    n = buf.shape[-1]
    fresh = isinstance(buf, jax.ShapeDtypeStruct)

    def body(z_ref, x_ref, dt_ref, *rest):
        o_ref, full_ref = rest[-2:]
        full_ref[:, pl.ds(0, SSD_INNER)] = z_ref[...]
        full_ref[:, pl.ds(SSD_INNER, SSD_CONV_DIM)] = x_ref[...]
        full_ref[:, pl.ds(SSD_INNER + SSD_CONV_DIM, LANES)] = dt_ref[...]
        my_c, my_chip = _my_core_and_chip()
        for d in range(N_DEV):
            o_ref[(d % 2) ^ my_c, (d // 2) ^ my_chip] = full_ref[:, pl.ds(d * n, n)]

    return pl.pallas_call(
        body, name="ssd_in_to_shards", grid=(K // tk,),
        in_specs=[_rows(tk, SSD_INNER), _rows(tk, SSD_CONV_DIM), _rows(tk, LANES)] + ([] if fresh else [_ANY]),
        out_specs=pl.BlockSpec((2, 4, None, tk, n), lambda t: (0, 0, j, t, 0)),
        out_shape=jax.ShapeDtypeStruct(buf.shape, f32),
        scratch_shapes=[pltpu.VMEM((tk, SSD_IN_PAD), f32)],
        input_output_aliases={} if fresh else {3: 0},
        compiler_params=_params(("parallel",)),
    )(dwz, dwx, dwdt, *([] if fresh else [buf]))


def sum_over_devices(gathered):
    _, R, W = gathered.shape

    def body(g_ref, o_ref):
        acc = g_ref[0]
        for k in range(1, N_DEV):
            acc = acc + g_ref[k]
        o_ref[...] = acc

    return pl.pallas_call(
        body, name="sum_over_devices", grid=(1,),
        in_specs=[pl.BlockSpec((N_DEV, R, W), lambda i: (0, 0, 0))], out_specs=pl.BlockSpec((R, W), lambda i: (0, 0)),
        out_shape=jax.ShapeDtypeStruct((R, W), f32), compiler_params=_params(("arbitrary",)),
    )(gathered)


_ANY = pl.BlockSpec(memory_space=pl.ANY)


class _Exchange:
    def __init__(self, inputs, out_shapes, scratch, start, finish, relay=None):
        self.inputs, self.out_shapes, self.scratch, self.start, self.finish = inputs, out_shapes, scratch, start, finish
        self.relay = relay

    def run(self, name):
        ni, no = len(self.inputs), len(self.out_shapes)

        def body(*refs):
            parts = (refs[:ni], refs[ni:ni + no], refs[ni + no:])
            self.start(*parts)
            if self.relay is not None:
                self.relay(*parts)
            self.finish(*parts)

        return pl.pallas_call(body, name=name, in_specs=[_ANY] * ni, out_specs=[_ANY] * no, out_shape=self.out_shapes,
                              scratch_shapes=self.scratch)(*self.inputs)


def _carry(body, n_in, n_out, rider, first, last, late=None):
    if rider is None:
        return body
    ri, ro = len(rider.inputs), len(rider.out_shapes)

    def hosted(*refs):
        a, b, c = n_in + ri, n_in + ri + n_out, n_in + ri + n_out + ro
        rs = len(refs) - c - len(rider.scratch)
        parts = (refs[n_in:a], refs[b:c], refs[c + rs:])

        @pl.when(first())
        def _():
            rider.start(*parts)

        if rider.relay is not None and late is not None:
            @pl.when(late())
            def _():
                rider.relay(*parts)

        body(*refs[:n_in], *refs[a:b], *refs[c:c + rs])

        @pl.when(last())
        def _():
            if rider.relay is not None and late is None:
                rider.relay(*parts)
            rider.finish(*parts)

    return hosted


def _rider_specs(rider):
    if rider is None:
        return [], [], [], [], []
    return [_ANY] * len(rider.inputs), [_ANY] * len(rider.out_shapes), list(rider.out_shapes), list(rider.scratch), list(rider.inputs)


def all_gather(blocks):
    n = len(blocks)

    def plan(x_refs, out_refs, sems):
        send_sems, recv_sems, local_sems = sems
        x, y, c = lax.axis_index("x"), lax.axis_index("y"), lax.axis_index("c")
        me, sibling = (x, y, c), (x, y, 1 - c)
        chips = [(1 - x, y), (x, 1 - y), (1 - x, 1 - y)]

        def copy(a, k, blk, to, src=None):
            px, py, pc = blk
            slot = out_refs[a].at[4 * px + 2 * py + pc]
            return pltpu.make_async_remote_copy(
                src_ref=slot if src is None else src, dst_ref=slot,
                send_sem=send_sems.at[7 * a + k], recv_sem=recv_sems.at[7 * a + k], device_id=to, device_id_type=MESH)

        mine = [pltpu.make_async_copy(x_refs[a], out_refs[a].at[4 * x + 2 * y + c], local_sems.at[a]) for a in range(n)]
        first = []
        for a in range(n):
            first += [copy(a, 0, me, sibling, src=x_refs[a])] + [copy(a, 1 + j, me, (*chip, c), src=x_refs[a]) for j, chip in enumerate(chips)]
        return c, me, sibling, chips, copy, mine, first

    def start(x_refs, out_refs, sems):
        _, _, _, _, _, mine, first = plan(x_refs, out_refs, sems)
        for cp in mine + first:
            cp.start()

    def relay(x_refs, out_refs, sems):
        c, me, sibling, chips, copy, _, _ = plan(x_refs, out_refs, sems)
        for j, chip in enumerate(chips):
            for a in range(n):
                copy(a, 1 + j, (*chip, c), me).wait_recv()
                copy(a, 4 + j, (*chip, c), sibling).start()

    def finish(x_refs, out_refs, sems):
        c, me, sibling, chips, copy, mine, first = plan(x_refs, out_refs, sems)
        passed = [copy(a, 4 + j, (*chip, c), sibling) for j, chip in enumerate(chips) for a in range(n)]
        for a in range(n):
            copy(a, 0, sibling, me).wait_recv()
            for j, chip in enumerate(chips):
                copy(a, 4 + j, (*chip, 1 - c), me).wait_recv()
        for cp in first + passed:
            cp.wait_send()
        for cp in mine:
            cp.wait()

    return _Exchange(list(blocks), [jax.ShapeDtypeStruct((N_DEV,) + b.shape, b.dtype) for b in blocks],
                     [pltpu.SemaphoreType.DMA((7 * n,)), pltpu.SemaphoreType.DMA((7 * n,)), pltpu.SemaphoreType.DMA((n,))], start, finish,
                     relay=relay)


def exchange_with_sibling(gs):
    n = len(gs)

    def plan(g_refs, recv_refs, sems):
        send_sems, recv_sems = sems
        x, y, c = lax.axis_index("x"), lax.axis_index("y"), lax.axis_index("c")
        return [pltpu.make_async_remote_copy(src_ref=g_refs[a].at[1], dst_ref=recv_refs[a], send_sem=send_sems.at[a],
                                             recv_sem=recv_sems.at[a], device_id=(x, y, 1 - c), device_id_type=MESH) for a in range(n)]

    def start(*refs):
        for cp in plan(*refs):
            cp.start()

    def finish(*refs):
        for cp in plan(*refs):
            cp.wait()

    return _Exchange(list(gs), [jax.ShapeDtypeStruct(g.shape[1:], g.dtype) for g in gs],
                     [pltpu.SemaphoreType.DMA((n,)), pltpu.SemaphoreType.DMA((n,))], start, finish)


def exchange_between_chips(parts):
    n = len(parts)

    def plan(p_refs, recv_refs, sems):
        send_sems, recv_sems = sems
        x, y, c = lax.axis_index("x"), lax.axis_index("y"), lax.axis_index("c")
        chips = [(2, (1 - x, y)), (1, (x, 1 - y)), (3, (1 - x, 1 - y))]
        return [pltpu.make_async_remote_copy(src_ref=p_refs[a].at[slot], dst_ref=recv_refs[a].at[k], send_sem=send_sems.at[3 * a + k],
                                             recv_sem=recv_sems.at[3 * a + k], device_id=(px, py, c), device_id_type=MESH)
                for a in range(n) for k, (slot, (px, py)) in enumerate(chips)]

    def start(*refs):
        for cp in plan(*refs):
            cp.start()

    def finish(*refs):
        for cp in plan(*refs):
            cp.wait()

    return _Exchange(list(parts), [jax.ShapeDtypeStruct((3,) + p.shape[1:], p.dtype) for p in parts],
                     [pltpu.SemaphoreType.DMA((3 * n,)), pltpu.SemaphoreType.DMA((3 * n,))], start, finish)


PARAMS = {
    "norm_w": ((DEPTH, 3, D_MODEL), 2),
    "ffn_w_gate": ((DEPTH, 2, D_MODEL, D_FF), 3),
    "ffn_w_up": ((DEPTH, 2, D_MODEL, D_FF), 3),
    "ffn_w_down": ((DEPTH, 2, D_FF, D_MODEL), 2),
    "ssd_w_in": ((2, D_MODEL, SSD_IN_DIM), 2),
    "ssd_conv_w": ((2, SSD_CONV_K, SSD_CONV_DIM), 2),
    "ssd_conv_b": ((2, SSD_CONV_DIM), None),
    "ssd_dt_bias": ((2, SSD_HEADS), None),
    "ssd_a_log": ((2, SSD_HEADS), None),
    "ssd_d": ((2, SSD_HEADS), None),
    "ssd_norm_w": ((2, SSD_INNER), None),
    "ssd_w_out": ((2, SSD_INNER, D_MODEL), 1),
    "sc_w_in": ((2, D_MODEL, 3 * D_MODEL), 2),
    "sc_conv_w": ((2, SC_CONV_K, D_MODEL), 2),
    "sc_w_out": ((2, D_MODEL, D_MODEL), 1),
    "final_norm_w": ((D_MODEL,), None),
}
NAMES = list(PARAMS)
BIG = ["ffn_w_gate", "ffn_w_up", "ffn_w_down", "ssd_w_in", "ssd_w_out", "sc_w_in", "sc_w_out"]
SMALL = [n for n in NAMES if n not in BIG]
SMALL_SHARDED = [n for n in SMALL if PARAMS[n][1] is not None]


def _round_up(n, m):
    return -(-n // m) * m


def _pack(flat_list, rows_multiple):
    flat = jnp.concatenate(flat_list)
    rows = _round_up(_round_up(flat.shape[0], PACK_W) // PACK_W, rows_multiple)
    return jnp.pad(flat, (0, rows * PACK_W - flat.shape[0])).reshape(rows, PACK_W)


def _unpack(packed, shapes, lead=()):
    flat = packed.reshape(lead + (-1,))
    out, off = [], 0
    for shp in shapes:
        n = 1
        for s in shp:
            n *= s
        out.append(flat[..., off:off + n].reshape(lead + tuple(shp)))
        off += n
    return out


def _local_shape(name):
    shp, ax = PARAMS[name]
    if ax is None:
        return shp
    return shp[:ax] + (shp[ax] // N_DEV,) + shp[ax + 1:]


def _full_from_gathered(g, name):
    shp, ax = PARAMS[name]
    return jnp.moveaxis(g, 0, ax).reshape(shp)


def _by_destination(full, name):
    shp, ax = PARAMS[name]
    loc = shp[ax] // N_DEV
    return jnp.moveaxis(full.reshape(shp[:ax] + (N_DEV, loc) + shp[ax + 1:]), ax, 0)


def _ssd_layer_fwd(xin, nw, p, rider=None):
    z, xbc, dt_raw = in_proj_fwd(xin, nw, [p["ssd_wz"], p["ssd_wx"], p["ssd_wdt"]], [bf16, bf16, f32])
    act, dt4 = ssd_conv_fwd(xbc, p["ssd_conv_w"], p["ssd_conv_b"], dt_raw, p["ssd_dt_bias"])
    y, states, *got = ssd_scan_fwd(act, dt4, p["ssd_alog4"], rider=rider)
    gn = ssd_gate_fwd(y, act, z, p["ssd_dx"], p["ssd_norm_w"])
    xout = out_proj_fwd(xin, gn, p["ssd_w_out"])
    return xout, (xin, z, xbc, dt_raw, act, dt4, y, states, gn), got


def _ssd_layer_bwd(dxo, nw, p, saved, gbuf, slab, rider=None):
    xin, z, xbc, dt_raw, act, dt4, y, states, gn = saved
    T = xin.shape[0]
    dy, dxs_skip, dz, dd_x, dgnw, dyb = ssd_gate_bwd(y, act, z, p["ssd_dx"], p["ssd_norm_w"], dxo, p["ssd_w_out"])
    gbuf["ssd_w_out"] = tn_matmul_to_shards(gn, dyb, gbuf["ssd_w_out"], (slab,), 0)
    g = {}
    g["ssd_norm_w"] = dgnw[0]
    g["ssd_d"] = jnp.sum(dd_x.reshape(SSD_HEADS, SSD_HEAD_DIM), axis=1)
    dxs, db, dc, ddt4, dalog4, *got = ssd_scan_bwd(act, dt4, p["ssd_alog4"], states, dy, rider=rider)
    g["ssd_a_log"] = dalog4[:, 0, :8].reshape(SSD_HEADS)
    dxbc, ddt_raw, dcw, dcb, ddtb = ssd_conv_bwd(xbc, p["ssd_conv_w"], p["ssd_conv_b"], dt_raw, p["ssd_dt_bias"], dxs, dxs_skip, db, dc, ddt4)
    g["ssd_conv_w"] = dcw[:SSD_CONV_K]
    g["ssd_conv_b"] = dcb[0]
    g["ssd_dt_bias"] = ddtb[0, :SSD_HEADS]
    dx, h, dnw = in_proj_bwd(xin, nw, dxo, [dz, dxbc, ddt_raw], [p["ssd_wz"], p["ssd_wx"], p["ssd_wdt"]])
    gbuf["ssd_w_in"] = ssd_in_to_shards(tn_matmul(h, dz), tn_matmul(h, dxbc), tn_matmul(h, ddt_raw), gbuf["ssd_w_in"], slab)
    return dx, dnw, g, got


def _sc_layer_fwd(xin, nw, p):
    (bcu,) = in_proj_fwd(xin, nw, [p["sc_w_in"]], [bf16])
    q = sc_mid_fwd(bcu, p["sc_conv_w"])
    return out_proj_fwd(xin, q, p["sc_w_out"]), (xin, bcu, q)


def _sc_layer_bwd(dxo, nw, p, saved, gbuf, slab):
    xin, bcu, q = saved
    dbcu, dcw, dyb = sc_mid_bwd(bcu, p["sc_conv_w"], dxo, p["sc_w_out"])
    gbuf["sc_w_out"] = tn_matmul_to_shards(q, dyb, gbuf["sc_w_out"], (slab,), 0)
    g = {"sc_conv_w": dcw[:SC_CONV_K]}
    dx, h, dnw = in_proj_bwd(xin, nw, dxo, [dbcu], [p["sc_w_in"]])
    gbuf["sc_w_in"] = tn_matmul_to_shards(h, dbcu, gbuf["sc_w_in"], (slab,), 1)
    return dx, dnw, g


def kernel(x, norm_w, ffn_w_gate, ffn_w_up, ffn_w_down, ssd_w_in, ssd_conv_w, ssd_conv_b, ssd_dt_bias, ssd_a_log, ssd_d, ssd_norm_w, ssd_w_out, sc_w_in, sc_conv_w, sc_w_out, final_norm_w, loss_target, m_norm_w, m_ffn_w_gate, m_ffn_w_up, m_ffn_w_down, m_ssd_w_in, m_ssd_conv_w, m_ssd_conv_b, m_ssd_dt_bias, m_ssd_a_log, m_ssd_d, m_ssd_norm_w, m_ssd_w_out, m_sc_w_in, m_sc_conv_w, m_sc_w_out, m_final_norm_w, v_norm_w, v_ffn_w_gate, v_ffn_w_up, v_ffn_w_down, v_ssd_w_in, v_ssd_conv_w, v_ssd_conv_b, v_ssd_dt_bias, v_ssd_a_log, v_ssd_d, v_ssd_norm_w, v_ssd_w_out, v_sc_w_in, v_sc_conv_w, v_sc_w_out, v_final_norm_w):
    w_loc = dict(zip(NAMES, (norm_w, ffn_w_gate, ffn_w_up, ffn_w_down, ssd_w_in, ssd_conv_w, ssd_conv_b, ssd_dt_bias, ssd_a_log, ssd_d, ssd_norm_w, ssd_w_out, sc_w_in, sc_conv_w, sc_w_out, final_norm_w)))
    m_loc = dict(zip(NAMES, (m_norm_w, m_ffn_w_gate, m_ffn_w_up, m_ffn_w_down, m_ssd_w_in, m_ssd_conv_w, m_ssd_conv_b, m_ssd_dt_bias, m_ssd_a_log, m_ssd_d, m_ssd_norm_w, m_ssd_w_out, m_sc_w_in, m_sc_conv_w, m_sc_w_out, m_final_norm_w)))
    v_loc = dict(zip(NAMES, (v_norm_w, v_ffn_w_gate, v_ffn_w_up, v_ffn_w_down, v_ssd_w_in, v_ssd_conv_w, v_ssd_conv_b, v_ssd_dt_bias, v_ssd_a_log, v_ssd_d, v_ssd_norm_w, v_ssd_w_out, v_sc_w_in, v_sc_conv_w, v_sc_w_out, v_final_norm_w)))
    my_dev = 4 * lax.axis_index("x") + 2 * lax.axis_index("y") + lax.axis_index("c")

    def as3d(a):
        return a.reshape((-1,) + a.shape[-2:])

    wb = {n: as3d(w_loc[n]).astype(bf16) for n in BIG}

    FFN = ["ffn_w_gate", "ffn_w_up", "ffn_w_down"]

    def mixer_names(i):
        return ["ssd_w_in", "ssd_w_out"] if i % 2 == 0 else ["sc_w_in", "sc_w_out"]

    ag_sets = [[(n, 0, 1) for n in FFN], [(n, 1, 1) for n in FFN] + [(n, 0, 1) for n in mixer_names(0)]]
    ag_sets += [[(n, 2 * r, 2) for n in FFN] + [(n, r // 2, 1) for n in mixer_names(r)] for r in (1, 2, 3)]

    def set_blocks(spec):
        return [wb[n][a0:a0 + na] for n, a0, na in spec]

    def set_weights(spec, gathered):
        q = {}
        for (n, _, _), g in zip(spec, gathered):
            if n == "ssd_w_in":
                q["ssd_wz"], q["ssd_wx"], q["ssd_wdt"] = assemble_ssd_in(g)
            else:
                q[n] = assemble(g, 1 if PARAMS[n][1] == len(PARAMS[n][0]) - 1 else 0)
        return q

    ss_shapes = [_local_shape(n) for n in SMALL_SHARDED]
    gathered0 = all_gather(set_blocks(ag_sets[0]) + [_pack([w_loc[n].reshape(-1) for n in SMALL_SHARDED], 8)]).run("all_gather_first")
    full = {}
    for n, part in zip(SMALL_SHARDED, _unpack(gathered0[-1], ss_shapes, lead=(N_DEV,))):
        full[n] = _full_from_gathered(part, n)
    for n in SMALL:
        if PARAMS[n][1] is None:
            full[n] = w_loc[n]
    small = {
        "ssd_conv_w": full["ssd_conv_w"],
        "ssd_conv_b": full["ssd_conv_b"].reshape(2, 1, SSD_CONV_DIM),
        "ssd_dt_bias": jnp.pad(full["ssd_dt_bias"], ((0, 0), (0, LANES - SSD_HEADS))).reshape(2, 1, LANES),
        "ssd_alog4": jnp.pad(full["ssd_a_log"].reshape(2, SSD_GROUPS, 1, 8), ((0, 0), (0, 0), (0, 0), (0, LANES - 8))),
        "ssd_dx": jnp.repeat(full["ssd_d"], SSD_HEAD_DIM, axis=1).reshape(2, 1, SSD_INNER),
        "ssd_norm_w": full["ssd_norm_w"].reshape(2, 1, SSD_INNER),
        "sc_conv_w": full["sc_conv_w"],
    }
    nw_all = full["norm_w"].reshape(DEPTH, 3, 1, D_MODEL)

    ffn_w = [[None, None] for _ in range(DEPTH)]
    mix_w = [None] * DEPTH

    def arrived(s, gathered):
        q = set_weights(ag_sets[s], gathered)
        ffn = tuple(q[n] for n in FFN)
        if s == 0:
            ffn_w[0][0] = ffn + ((0,),)
            return
        i = 0 if s == 1 else s - 1
        if s == 1:
            ffn_w[0][1] = ffn + ((0,),)
        else:
            ffn_w[i] = [ffn + ((0,),), ffn + ((1,),)]
        m = {n: v[0] for n, v in q.items() if n not in FFN}
        m.update({n: v[i // 2] for n, v in small.items() if n.startswith("ssd" if i % 2 == 0 else "sc")})
        mix_w[i] = m

    def rider_for(s):
        return all_gather(set_blocks(ag_sets[s]))

    xc = x[0]
    saved = []
    arrived(0, gathered0[:-1])
    for i in range(DEPTH):
        carried = {0: (1, 2, 3), 1: (4, None, None)}.get(i, (None, None, None))
        wg, wu, wd, idx = ffn_w[i][0]
        x1, g1, u1, a1, *got = ffn_fwd(xc, nw_all[i, 0], wg, wu, wd, idx, rider=rider_for(carried[0]) if carried[0] else None)
        if carried[0]:
            arrived(carried[0], got)
        if i % 2 == 0:
            x2, mix_saved, got = _ssd_layer_fwd(x1, nw_all[i, 1], mix_w[i], rider=rider_for(carried[1]) if carried[1] else None)
            if carried[1]:
                arrived(carried[1], got)
        else:
            x2, mix_saved = _sc_layer_fwd(x1, nw_all[i, 1], mix_w[i])
        wg, wu, wd, idx = ffn_w[i][1]
        x3, g3, u3, a3, *got = ffn_fwd(x2, nw_all[i, 2], wg, wu, wd, idx, rider=rider_for(carried[2]) if carried[2] else None)
        if carried[2]:
            arrived(carried[2], got)
        saved.append(((xc, g1, u1, a1), mix_saved, (x2, g3, u3, a3)))
        xc = x3

    loss_row, dx, dfw = loss_head(xc, full["final_norm_w"].reshape(1, D_MODEL), loss_target[0])
    loss = lax.psum(loss_row[0, 0], ("x", "y", "c"))

    grads = {n: [None] * PARAMS[n][0][0] for n in SMALL if n != "final_norm_w"}
    grads["final_norm_w"] = dfw[0]
    dnorm = [[None] * 3 for _ in range(DEPTH)]
    def slabs(n, which):
        if n.startswith("ffn"):
            return {"early": (2, 6), "mid": (1, 1), "last": (0, 1)}[which]
        if n.startswith("ssd"):
            return {"early": (1, 1), "mid": (0, 1), "last": (0, 0)}[which]
        return {"early": (0, 2), "mid": (0, 0), "last": (0, 0)}[which]

    gb = {which: {n: jax.ShapeDtypeStruct((2, 4, slabs(n, which)[1]) + wb[n].shape[1:], f32) for n in BIG if slabs(n, which)[1]}
          for which in ("early", "mid", "last")}

    def ffn_back(i, k, dxo, sv, rider=None):
        xin, g_, u_, a_ = sv
        which = "early" if i > 0 else ("mid" if k == 1 else "last")
        gbuf = gb[which]
        slab = 2 * i + k - slabs("ffn_w_gate", which)[0]
        wg, wu, wd, idx = ffn_w[i][k]
        dxi, h, dyb, dg, du, dnw, *got = ffn_bwd_dx(xin, dxo, g_, u_, nw_all[i, 2 * k], wg, wu, wd, idx, rider=rider)
        dnorm[i][2 * k] = dnw[0]
        gbuf["ffn_w_gate"] = tn_matmul_to_shards(h, dg, gbuf["ffn_w_gate"], (slab,), 1)
        gbuf["ffn_w_up"] = tn_matmul_to_shards(h, du, gbuf["ffn_w_up"], (slab,), 1)
        gbuf["ffn_w_down"] = tn_matmul_to_shards(a_, dyb, gbuf["ffn_w_down"], (slab,), 0)
        return dxi, got

    def reduce_in_chip(gbuf, from_sibling=None):
        names = list(gbuf)
        bufs = [gbuf[n] for n in names]
        if from_sibling is None:
            from_sibling = exchange_with_sibling(bufs).run("exchange_with_sibling")
        return names, bufs, from_sibling, [pair_sum_bf16(g, fs, "pair_sum_" + n) for n, g, fs in zip(names, bufs, from_sibling)]

    reduced, from_chips = {}, {}
    for i in reversed(range(DEPTH)):
        j = i // 2
        sv_a, sv_mix, sv_b = saved[i]
        if i == 0:
            dx, got = ffn_back(i, 1, dx, sv_b, rider=exchange_with_sibling(list(gb["early"].values())))
            reduced["early"] = reduce_in_chip(gb["early"], from_sibling=got)
        else:
            dx, _ = ffn_back(i, 1, dx, sv_b)
        if i % 2 == 0:
            rider = exchange_between_chips(reduced["early"][3]) if i == 0 else None
            dx, dnw, gm, got = _ssd_layer_bwd(dx, nw_all[i, 1], mix_w[i], sv_mix, gb["mid" if i == 0 else "early"], 0, rider=rider)
            if i == 0:
                from_chips["early"] = got
                reduced["mid"] = reduce_in_chip(gb["mid"])
        else:
            dx, dnw, gm = _sc_layer_bwd(dx, nw_all[i, 1], mix_w[i], sv_mix, gb["early"], j)
        dnorm[i][1] = dnw[0]
        for n, val in gm.items():
            grads[n][j] = val
        dx, got = ffn_back(i, 0, dx, sv_a, rider=exchange_between_chips(reduced["mid"][3]) if i == 0 else None)
        if i == 0:
            from_chips["mid"] = got

    grads["norm_w"] = jnp.stack([jnp.stack(r) for r in dnorm])
    for n in SMALL:
        if isinstance(grads[n], list):
            grads[n] = jnp.stack(grads[n])

    reduced["last"] = reduce_in_chip(gb["last"])
    from_chips["last"] = exchange_between_chips(reduced["last"][3]).run("exchange_between_chips")
    results = [{}, {}, {}, {}]
    outs = {}
    for which in ("last", "mid", "early"):
        names, bufs, from_sibling, _ = reduced[which]
        for n, g, fs, fc in zip(names, bufs, from_sibling, from_chips[which]):
            parts = [((0, 0), g), ((0,), fs), ((0,), fc), ((1,), fc), ((2,), fc)]
            outs[n] = adamw(parts, as3d(w_loc[n]), as3d(m_loc[n]), as3d(v_loc[n]), name="adamw_" + n + "_" + which,
                            a0=slabs(n, which)[0], prev=outs.get(n))
    for n in BIG:
        for k in range(4):
            results[k][n] = outs[n][k].reshape(_local_shape(n))

    g_small = _pack([grads[n].reshape(-1) for n in SMALL], 8)
    g_small = sum_over_devices(all_gather([g_small]).run("all_gather_small_grads")[0])
    g_small_full = dict(zip(SMALL, _unpack(g_small, [PARAMS[n][0] for n in SMALL])))
    g_small_loc = []
    for n in SMALL:
        if PARAMS[n][1] is None:
            g_small_loc.append(g_small_full[n])
        else:
            g_small_loc.append(lax.dynamic_index_in_dim(_by_destination(g_small_full[n], n), my_dev, axis=0, keepdims=False))
    small_shapes = [_local_shape(n) for n in SMALL]
    pack_small = lambda d: _pack([d[n].reshape(-1) for n in SMALL], 8)[None]
    small_out = adamw([_pack([gl.reshape(-1) for gl in g_small_loc], 8)[None]], pack_small(w_loc), pack_small(m_loc), pack_small(v_loc), name="adamw_small")
    for k in range(4):
        results[k].update(zip(SMALL, _unpack(small_out[k], small_shapes)))
    return (loss, dx[None], *[results[0][n] for n in NAMES], *[results[1][n] for n in NAMES],
            *[results[2][n] for n in NAMES], *[results[3][n] for n in NAMES])
```

```python
import functools

import jax
import jax.numpy as jnp
from jax import lax
from jax.experimental import pallas as pl
from jax.experimental.pallas import tpu as pltpu

f32 = jnp.float32
bf16 = jnp.bfloat16

D_MODEL = 1024
D_FF = 2816
DEPTH = 4
SSD_INNER = 2048
SSD_HEADS = 32
SSD_HEAD_DIM = 64
SSD_GROUPS = 4
SSD_STATE = 128
SSD_CONV_K = 4
SSD_CONV_DIM = 3072
SSD_IN_DIM = 5152
SSD_CHUNK = 128
SC_CONV_K = 3
RMS_EPS = 1e-5
N_DEV = 8
LANES = 128
HALO = 16
PACK_W = 1024
VMEM_LIMIT = 56 * 1024 * 1024
NEG_BIG = -1e30

ADAM_LR = 0.001
ADAM_B1 = 0.9
ADAM_B2 = 0.999
ADAM_EPS = 1e-08
ADAM_WD = 0.01
ADAM_STEP = 10

NT_DIMS = (((1,), (1,)), ((), ()))
TN_DIMS = (((0,), (0,)), ((), ()))
MESH = pl.DeviceIdType.MESH


def _params(sem=None):
    return pltpu.CompilerParams(dimension_semantics=sem, vmem_limit_bytes=VMEM_LIMIT)


def _resident(shape):
    nd = len(shape)
    return pl.BlockSpec(tuple(shape), lambda *_: (0,) * nd, pipeline_mode=pl.Buffered(1))


def _rows(tm, width):
    return pl.BlockSpec((tm, width), lambda i: (i, 0))


def _my_core_and_chip():
    return lax.axis_index("c"), 2 * lax.axis_index("x") + lax.axis_index("y")


def _sigmoid(v):
    return 0.5 * jnp.tanh(0.5 * v) + 0.5


def _softplus(v):
    return jnp.maximum(v, 0.0) + jnp.log(1.0 + jnp.exp(-jnp.abs(v)))


def _rms_fwd(xv, w):
    inv = lax.rsqrt(jnp.mean(xv * xv, axis=-1, keepdims=True) + RMS_EPS)
    xh = xv * inv
    return xh * w, xh, inv


def _rms_bwd(dh, xh, inv, w):
    dxh = dh * w
    dx = inv * (dxh - xh * jnp.mean(dxh * xh, axis=-1, keepdims=True))
    return dx, jnp.sum(dh * xh, axis=0, keepdims=True)


def _mm(a, b):
    return jnp.dot(a, b, preferred_element_type=f32)


def _mm_nt(a, b):
    return lax.dot_general(a, b, NT_DIMS, preferred_element_type=f32)


def _mm_tn(a, b):
    return lax.dot_general(a, b, TN_DIMS, preferred_element_type=f32)


def _layer_slab(w, idx):
    tail = w.shape[len(idx):]
    return pl.BlockSpec((None,) * len(idx) + tuple(tail), lambda *_: tuple(idx) + (0,) * len(tail), pipeline_mode=pl.Buffered(1))


def ffn_fwd(x, nw, wg, wu, wd, idx, tm=512, rider=None):
    T = x.shape[0]
    nt = T // tm
    r_in, r_out, r_shapes, r_scratch, r_args = _rider_specs(rider)

    def body(x_ref, nw_ref, wg_ref, wu_ref, wd_ref, xo_ref, g_ref, u_ref, a_ref):
        xv = x_ref[...]
        h, _, _ = _rms_fwd(xv, nw_ref[...])
        hb = h.astype(bf16)
        g = _mm(hb, wg_ref[...])
        u = _mm(hb, wu_ref[...])
        ab = (g * _sigmoid(g) * u).astype(bf16)
        g_ref[...] = g.astype(bf16)
        u_ref[...] = u.astype(bf16)
        a_ref[...] = ab
        xo_ref[...] = xv + 0.5 * _mm(ab, wd_ref[...])

    hosted = _carry(body, 5, 4, rider, lambda: pl.program_id(0) == 0, lambda: pl.program_id(0) == nt - 1,
                    late=lambda: pl.program_id(0) == (7 * nt) // 8)
    return pl.pallas_call(
        hosted, name="ffn_fwd" if rider is None else "ffn_fwd_carrying", grid=(nt,),
        in_specs=[_rows(tm, D_MODEL), _resident((1, D_MODEL)), _layer_slab(wg, idx), _layer_slab(wu, idx), _layer_slab(wd, idx)] + r_in,
        out_specs=[_rows(tm, D_MODEL), _rows(tm, D_FF), _rows(tm, D_FF), _rows(tm, D_FF)] + r_out,
        out_shape=[jax.ShapeDtypeStruct((T, D_MODEL), f32)] + [jax.ShapeDtypeStruct((T, D_FF), bf16)] * 3 + r_shapes,
        scratch_shapes=r_scratch,
        compiler_params=_params(("parallel",) if rider is None else ("arbitrary",)),
    )(x, nw, wg, wu, wd, *r_args)


def ffn_bwd_dx(x, dxo, g, u, nw, wg, wu, wd, idx, tm=256, rider=None):
    T = x.shape[0]
    nt = T // tm
    r_in, r_out, r_shapes, r_scratch, r_args = _rider_specs(rider)

    def body(x_ref, dxo_ref, g_ref, u_ref, nw_ref, wg_ref, wu_ref, wd_ref, dx_ref, h_ref, dy_ref, dg_ref, du_ref, dnw_ref):
        w = nw_ref[...]
        h, xh, inv = _rms_fwd(x_ref[...], w)
        dxo_v = dxo_ref[...]
        dyb = (0.5 * dxo_v).astype(bf16)
        da = _mm_nt(dyb, wd_ref[...])
        gv = g_ref[...].astype(f32)
        uv = u_ref[...].astype(f32)
        s = _sigmoid(gv)
        dgb = (da * uv * (s * (1.0 + gv * (1.0 - s)))).astype(bf16)
        dub = (da * (gv * s)).astype(bf16)
        dg_ref[...] = dgb
        du_ref[...] = dub
        dh = _mm_nt(dgb, wg_ref[...]) + _mm_nt(dub, wu_ref[...])
        dxn, dw = _rms_bwd(dh, xh, inv, w)
        dx_ref[...] = dxo_v + dxn
        h_ref[...] = h.astype(bf16)
        dy_ref[...] = dyb

        @pl.when(pl.program_id(0) == 0)
        def _():
            dnw_ref[...] = jnp.zeros_like(dnw_ref)

        dnw_ref[...] += dw

    hosted = _carry(body, 8, 6, rider, lambda: pl.program_id(0) == 0, lambda: pl.program_id(0) == nt - 1)
    return pl.pallas_call(
        hosted, name="ffn_bwd_dx" if rider is None else "ffn_bwd_dx_carrying", grid=(nt,),
        in_specs=[_rows(tm, D_MODEL), _rows(tm, D_MODEL), _rows(tm, D_FF), _rows(tm, D_FF), _resident((1, D_MODEL)),
                  _layer_slab(wg, idx), _layer_slab(wu, idx), _layer_slab(wd, idx)] + r_in,
        out_specs=[_rows(tm, D_MODEL), _rows(tm, D_MODEL), _rows(tm, D_MODEL), _rows(tm, D_FF), _rows(tm, D_FF),
                   pl.BlockSpec((1, D_MODEL), lambda i: (0, 0))] + r_out,
        out_shape=[jax.ShapeDtypeStruct((T, D_MODEL), f32), jax.ShapeDtypeStruct((T, D_MODEL), bf16), jax.ShapeDtypeStruct((T, D_MODEL), bf16),
                   jax.ShapeDtypeStruct((T, D_FF), bf16), jax.ShapeDtypeStruct((T, D_FF), bf16), jax.ShapeDtypeStruct((1, D_MODEL), f32)] + r_shapes,
        scratch_shapes=r_scratch,
        compiler_params=_params(("arbitrary",)),
    )(x, dxo, g, u, nw, wg, wu, wd, *r_args)


def tn_matmul(a, b, tk=1024):
    T, M = a.shape
    N = b.shape[1]
    bn = N if M * N <= 3_200_000 else N // 2
    nk = T // tk

    def body(a_ref, b_ref, o_ref):
        @pl.when(pl.program_id(1) == 0)
        def _():
            o_ref[...] = jnp.zeros_like(o_ref)

        o_ref[...] += _mm_tn(a_ref[...], b_ref[...])

    return pl.pallas_call(
        body, name=f"tn_matmul_{M}x{N}", grid=(N // bn, nk),
        in_specs=[pl.BlockSpec((tk, M), lambda j, k: (k, 0)), pl.BlockSpec((tk, bn), lambda j, k: (k, j))],
        out_specs=pl.BlockSpec((M, bn), lambda j, k: (0, j)),
        out_shape=jax.ShapeDtypeStruct((M, N), f32),
        compiler_params=_params(("parallel", "arbitrary")),
    )(a, b)


def tn_matmul_to_shards(a, b, buf, idx, axis):
    T, M = a.shape
    N = b.shape[1]
    m, n = buf.shape[-2:]
    (slab,) = idx
    tk = 1024
    nk = T // tk
    fresh = isinstance(buf, jax.ShapeDtypeStruct)

    def body(a_ref, b_ref, *rest):
        o_ref, acc_ref, stage_ref, sem = rest[-4:]
        k = pl.program_id(0)

        @pl.when(k == 0)
        def _():
            acc_ref[...] = jnp.zeros_like(acc_ref)

        acc_ref[...] += _mm_tn(a_ref[...], b_ref[...])

        @pl.when(k == nk - 1)
        def _():
            my_c, my_chip = _my_core_and_chip()
            for d in range(N_DEV):
                piece = acc_ref[:, pl.ds(d * n, n)] if axis == 1 else acc_ref[pl.ds(d * m, m), :]
                stage_ref[(d % 2) ^ my_c, (d // 2) ^ my_chip] = piece
            out = pltpu.make_async_copy(stage_ref, o_ref.at[:, :, slab], sem)
            out.start()
            out.wait()

    return pl.pallas_call(
        body, name=f"tn_matmul_to_shards_{M}x{N}_{axis}", grid=(nk,),
        in_specs=[pl.BlockSpec((tk, M), lambda k: (k, 0)), pl.BlockSpec((tk, N), lambda k: (k, 0))] + ([] if fresh else [_ANY]),
        out_specs=_ANY,
        out_shape=jax.ShapeDtypeStruct(buf.shape, f32),
        scratch_shapes=[pltpu.VMEM((M, N), f32), pltpu.VMEM((2, 4, m, n), f32), pltpu.SemaphoreType.DMA],
        input_output_aliases={} if fresh else {2: 0},
        compiler_params=_params(("arbitrary",)),
    )(a, b, *([] if fresh else [buf]))


def in_proj_fwd(x, nw, ws, out_dtypes, tm=512):
    T = x.shape[0]
    n = len(ws)

    def body(*refs):
        x_ref, nw_ref = refs[:2]
        w_refs = refs[2:2 + n]
        o_refs = refs[2 + n:]
        h, _, _ = _rms_fwd(x_ref[...], nw_ref[...])
        hb = h.astype(bf16)
        for w_ref, o_ref in zip(w_refs, o_refs):
            o_ref[...] = _mm(hb, w_ref[...]).astype(o_ref.dtype)

    return pl.pallas_call(
        body, name="in_proj_fwd_" + "_".join(str(w.shape[1]) for w in ws), grid=(T // tm,),
        in_specs=[_rows(tm, D_MODEL), _resident((1, D_MODEL))] + [_resident(w.shape) for w in ws],
        out_specs=[_rows(tm, w.shape[1]) for w in ws],
        out_shape=[jax.ShapeDtypeStruct((T, w.shape[1]), dt) for w, dt in zip(ws, out_dtypes)],
        compiler_params=_params(("parallel",)),
    )(x, nw, *ws)


def in_proj_bwd(x, nw, dxo, dys, ws, tm=512):
    T = x.shape[0]
    n = len(ws)

    def body(*refs):
        x_ref, nw_ref, dxo_ref = refs[:3]
        dy_refs = refs[3:3 + n]
        w_refs = refs[3 + n:3 + 2 * n]
        dx_ref, h_ref, dnw_ref = refs[3 + 2 * n:]
        w = nw_ref[...]
        h, xh, inv = _rms_fwd(x_ref[...], w)
        dh = _mm_nt(dy_refs[0][...], w_refs[0][...])
        for dy_ref, w_ref in zip(dy_refs[1:], w_refs[1:]):
            dh = dh + _mm_nt(dy_ref[...], w_ref[...])
        dxn, dw = _rms_bwd(dh, xh, inv, w)
        dx_ref[...] = dxo_ref[...] + dxn
        h_ref[...] = h.astype(bf16)

        @pl.when(pl.program_id(0) == 0)
        def _():
            dnw_ref[...] = jnp.zeros_like(dnw_ref)

        dnw_ref[...] += dw

    return pl.pallas_call(
        body, name="in_proj_bwd_" + "_".join(str(w.shape[1]) for w in ws), grid=(T // tm,),
        in_specs=[_rows(tm, D_MODEL), _resident((1, D_MODEL)), _rows(tm, D_MODEL)] + [_rows(tm, w.shape[1]) for w in ws]
        + [_resident(w.shape) for w in ws],
        out_specs=[_rows(tm, D_MODEL), _rows(tm, D_MODEL), pl.BlockSpec((1, D_MODEL), lambda i: (0, 0))],
        out_shape=[jax.ShapeDtypeStruct((T, D_MODEL), f32), jax.ShapeDtypeStruct((T, D_MODEL), bf16), jax.ShapeDtypeStruct((1, D_MODEL), f32)],
        compiler_params=_params(("arbitrary",)),
    )(x, nw, dxo, *dys, *ws)


def out_proj_fwd(x, a, w, tm=1024):
    T = x.shape[0]
    K = a.shape[1]

    def body(x_ref, a_ref, w_ref, o_ref):
        o_ref[...] = x_ref[...] + _mm(a_ref[...], w_ref[...])

    return pl.pallas_call(
        body, name=f"out_proj_fwd_{K}", grid=(T // tm,),
        in_specs=[_rows(tm, D_MODEL), _rows(tm, K), _resident(w.shape)],
        out_specs=_rows(tm, D_MODEL), out_shape=jax.ShapeDtypeStruct((T, D_MODEL), f32),
        compiler_params=_params(("parallel",)),
    )(x, a, w)


def _halo_spec(tm, width, n_tiles, reverse):
    per = tm // HALO

    def idx(i):
        t = (n_tiles - 1 - i) if reverse else i
        return (jnp.maximum(t * per - 1, 0), 0)

    return pl.BlockSpec((HALO, width), idx)


def _tile_spec(tm, width, n_tiles, reverse):
    if reverse:
        return pl.BlockSpec((tm, width), lambda i: (n_tiles - 1 - i, 0))
    return _rows(tm, width)


ROW_BLOCK = 64


def _strip(s):
    return pl.ds(pl.multiple_of(s * LANES, LANES), LANES)


def _conv_rows(ext_ref, w_ref, cols, k_w, r0):
    base = HALO - (k_w - 1) + r0
    wins = [ext_ref[pl.ds(base + k, ROW_BLOCK), :] for k in range(k_w)]
    out = w_ref[pl.ds(0, 1), cols] * wins[0]
    for k in range(1, k_w):
        out = out + w_ref[pl.ds(k, 1), cols] * wins[k]
    return out, wins


def _shifted_back(d_ref, w_ref, cols, k_w, r0):
    out = w_ref[pl.ds(0, 1), cols] * d_ref[pl.ds(r0 + k_w - 1, ROW_BLOCK), :]
    for k in range(1, k_w):
        out = out + w_ref[pl.ds(k, 1), cols] * d_ref[pl.ds(r0 + k_w - 1 - k, ROW_BLOCK), :]
    return out


def ssd_conv_fwd(xbc, conv_w, conv_b, dt_raw, dt_bias, tm=512):
    T = xbc.shape[0]
    nt = T // tm
    K = SSD_CONV_K

    def body(x_ref, halo_ref, w_ref, b_ref, dtr_ref, dtb_ref, act_ref, dt_ref, ext_ref):
        first = pl.program_id(0) == 0

        def strip(s, carry):
            cols = _strip(s)
            ext_ref[pl.ds(0, HALO), :] = jnp.where(first, 0.0, halo_ref[:, cols].astype(f32))
            ext_ref[pl.ds(HALO, tm), :] = x_ref[:, cols].astype(f32)
            for r0 in range(0, tm, ROW_BLOCK):
                pre, _ = _conv_rows(ext_ref, w_ref, cols, K, r0)
                pre = pre + b_ref[:, cols]
                act_ref[pl.ds(r0, ROW_BLOCK), cols] = (pre * _sigmoid(pre)).astype(bf16)
            return carry

        lax.fori_loop(0, SSD_CONV_DIM // LANES, strip, 0)
        dt = _softplus(dtr_ref[...] + dtb_ref[...])
        lane = lax.broadcasted_iota(jnp.int32, (1, LANES), 1)
        for g in range(SSD_GROUPS):
            dt_ref[g] = jnp.where(lane < 8, dt if g == 0 else pltpu.roll(dt, LANES - 8 * g, axis=1), 0.0)

    return pl.pallas_call(
        body, name="ssd_conv_fwd", grid=(nt,),
        in_specs=[_rows(tm, SSD_CONV_DIM), _halo_spec(tm, SSD_CONV_DIM, nt, False), _resident(conv_w.shape), _resident(conv_b.shape),
                  _rows(tm, LANES), _resident(dt_bias.shape)],
        out_specs=[_rows(tm, SSD_CONV_DIM), pl.BlockSpec((SSD_GROUPS, tm, LANES), lambda i: (0, i, 0))],
        out_shape=[jax.ShapeDtypeStruct((T, SSD_CONV_DIM), bf16), jax.ShapeDtypeStruct((SSD_GROUPS, T, LANES), f32)],
        scratch_shapes=[pltpu.VMEM((tm + HALO, LANES), f32)],
        compiler_params=_params(("parallel",)),
    )(xbc, xbc, conv_w, conv_b, dt_raw, dt_bias)


def ssd_conv_bwd(xbc, conv_w, conv_b, dt_raw, dt_bias, dxs_a, dxs_b, db, dc, ddt, tm=512):
    T = xbc.shape[0]
    nt = T // tm
    K = SSD_CONV_K

    def body(x_ref, halo_ref, w_ref, b_ref, dtr_ref, dtb_ref, da_ref, dbb_ref, db_ref, dc_ref, ddt_ref,
             dx_ref, ddtr_ref, dw_ref, dbias_ref, ddtb_ref, ext_ref, dpre_ref, carry_ref):
        i = pl.program_id(0)

        @pl.when(i == 0)
        def _():
            carry_ref[...] = jnp.zeros_like(carry_ref)
            dw_ref[...] = jnp.zeros_like(dw_ref)
            dbias_ref[...] = jnp.zeros_like(dbias_ref)
            ddtb_ref[...] = jnp.zeros_like(ddtb_ref)

        first_tile = i == nt - 1

        def run_strips(lo, hi, load_dact):
            def strip(s, carry):
                cols = _strip(s)
                ext_ref[pl.ds(0, HALO), :] = jnp.where(first_tile, 0.0, halo_ref[:, cols].astype(f32))
                ext_ref[pl.ds(HALO, tm), :] = x_ref[:, cols].astype(f32)
                dpre_ref[pl.ds(tm, 8), :] = carry_ref[:, cols]
                bias = b_ref[:, cols]
                dws = [jnp.zeros((1, LANES), f32) for _ in range(K)]
                dbs = jnp.zeros((1, LANES), f32)
                for r0 in range(0, tm, ROW_BLOCK):
                    pre, wins = _conv_rows(ext_ref, w_ref, cols, K, r0)
                    pre = pre + bias
                    sg = _sigmoid(pre)
                    dpre = load_dact(s, r0) * (sg * (1.0 + pre * (1.0 - sg)))
                    dpre_ref[pl.ds(r0, ROW_BLOCK), :] = dpre
                    dbs = dbs + jnp.sum(dpre, axis=0, keepdims=True)
                    for k in range(K):
                        dws[k] = dws[k] + jnp.sum(dpre * wins[k], axis=0, keepdims=True)
                carry_ref[:, cols] = dpre_ref[pl.ds(0, 8), :]
                for r0 in range(0, tm, ROW_BLOCK):
                    dx_ref[pl.ds(r0, ROW_BLOCK), cols] = _shifted_back(dpre_ref, w_ref, cols, K, r0).astype(bf16)
                for k in range(K):
                    dw_ref[pl.ds(k, 1), cols] += dws[k]
                dbias_ref[:, cols] += dbs
                return carry

            lax.fori_loop(lo, hi, strip, 0)

        rows = lambda r0: pl.ds(r0, ROW_BLOCK)
        n_x = SSD_INNER // LANES
        n_g = SSD_GROUPS * SSD_STATE // LANES
        run_strips(0, n_x, lambda s, r0: da_ref[rows(r0), _strip(s)].astype(f32) + dbb_ref[rows(r0), _strip(s)].astype(f32))
        run_strips(n_x, n_x + n_g, lambda s, r0: db_ref[rows(r0), _strip(s - n_x)].astype(f32))
        run_strips(n_x + n_g, n_x + 2 * n_g, lambda s, r0: dc_ref[rows(r0), _strip(s - n_x - n_g)].astype(f32))
        lane = lax.broadcasted_iota(jnp.int32, (1, LANES), 1)
        ddt = jnp.where(lane < 8, ddt_ref[0], 0.0)
        for g in range(1, SSD_GROUPS):
            ddt = ddt + pltpu.roll(jnp.where(lane < 8, ddt_ref[g], 0.0), 8 * g, axis=1)
        ddtr = ddt * _sigmoid(dtr_ref[...] + dtb_ref[...])
        ddtr_ref[...] = ddtr.astype(bf16)
        ddtb_ref[...] += jnp.sum(ddtr, axis=0, keepdims=True)

    rev = functools.partial(_tile_spec, tm, n_tiles=nt, reverse=True)
    const = lambda shape: pl.BlockSpec(shape, lambda i: (0, 0))
    return pl.pallas_call(
        body, name="ssd_conv_bwd", grid=(nt,),
        in_specs=[rev(width=SSD_CONV_DIM), _halo_spec(tm, SSD_CONV_DIM, nt, True), _resident(conv_w.shape), _resident(conv_b.shape),
                  rev(width=LANES), _resident(dt_bias.shape), rev(width=SSD_INNER), rev(width=SSD_INNER),
                  rev(width=SSD_GROUPS * SSD_STATE), rev(width=SSD_GROUPS * SSD_STATE),
                  pl.BlockSpec((SSD_GROUPS, tm, LANES), lambda i: (0, nt - 1 - i, 0))],
        out_specs=[rev(width=SSD_CONV_DIM), rev(width=LANES), const((8, SSD_CONV_DIM)), const((1, SSD_CONV_DIM)), const((1, LANES))],
        out_shape=[jax.ShapeDtypeStruct((T, SSD_CONV_DIM), bf16), jax.ShapeDtypeStruct((T, LANES), bf16),
                   jax.ShapeDtypeStruct((8, SSD_CONV_DIM), f32), jax.ShapeDtypeStruct((1, SSD_CONV_DIM), f32), jax.ShapeDtypeStruct((1, LANES), f32)],
        scratch_shapes=[pltpu.VMEM((tm + HALO, LANES), f32), pltpu.VMEM((tm + 8, LANES), f32), pltpu.VMEM((8, SSD_CONV_DIM), f32)],
        compiler_params=_params(("arbitrary",)),
    )(xbc, xbc, conv_w, conv_b, dt_raw, dt_bias, dxs_a, dxs_b, db, dc, ddt)


def _ssd_chunk(xs, bm, cm, dt, alog, st):
    L = SSD_CHUNK
    row = lax.broadcasted_iota(jnp.int32, (L, L), 0)
    col = lax.broadcasted_iota(jnp.int32, (L, L), 1)
    causal = row >= col
    tril = jnp.where(causal, 1.0, 0.0).astype(f32)
    lane = lax.broadcasted_iota(jnp.int32, (1, LANES), 1)
    sub = lax.broadcasted_iota(jnp.int32, (LANES, 1), 0)
    lo = lane < SSD_HEAD_DIM
    last_row = sub == L - 1

    dta = dt * (-jnp.exp(alog))
    a_cs = jnp.dot(tril, dta, precision=lax.Precision.HIGHEST, preferred_element_type=f32)
    a_cs_t = a_cs.T
    bmb = bm.astype(bf16)
    cmb = cm.astype(bf16)
    cb = _mm_nt(cmb, bmb)
    c_st = _mm(cmb, st.astype(bf16))

    def head_col(v, e):
        return jnp.sum(jnp.where(lane == e, v, 0.0), axis=1, keepdims=True)

    def head_row(v, e):
        return jnp.sum(jnp.where(sub == e, v, 0.0), axis=0, keepdims=True)

    ys, sts = [], []
    for j in range(4):
        e0, e1 = 2 * j, 2 * j + 1
        c0, c1 = head_col(a_cs, e0), head_col(a_cs, e1)
        acs_x = jnp.where(lo, c0, c1)
        dt_x = jnp.where(lo, head_col(dt, e0), head_col(dt, e1))
        xd = xs[:, j * LANES:(j + 1) * LANES] * dt_x
        m0 = cb * jnp.exp(jnp.where(causal, c0 - head_row(a_cs_t, e0), NEG_BIG))
        m1 = cb * jnp.exp(jnp.where(causal, c1 - head_row(a_cs_t, e1), NEG_BIG))
        mcat = jnp.concatenate([m0, m1], axis=1).astype(bf16)
        xcat = jnp.concatenate([jnp.where(lo, xd, 0.0), jnp.where(lo, 0.0, xd)], axis=0).astype(bf16)
        y_diag = _mm(mcat, xcat)
        a_last = jnp.sum(jnp.where(last_row, acs_x, 0.0), axis=0, keepdims=True)
        x_dec = (xd * jnp.exp(a_last - acs_x)).astype(bf16)
        s_new = _mm_tn(bmb, x_dec)
        y_off = c_st[:, j * LANES:(j + 1) * LANES] * jnp.exp(acs_x)
        ys.append(y_diag + y_off)
        sts.append(jnp.exp(a_last) * st[:, j * LANES:(j + 1) * LANES] + s_new)
    return jnp.concatenate(ys, axis=1), jnp.concatenate(sts, axis=1)


SCAN_GROUPS_FWD = 4
SCAN_GROUPS_BWD = 1


def _scan_specs(nc, reverse, gs):
    L = SSD_CHUNK
    ch = (lambda c: nc - 1 - c) if reverse else (lambda c: c)
    gw = SSD_INNER // SSD_GROUPS
    b0 = SSD_INNER // (gs * SSD_STATE)
    c0 = (SSD_INNER + SSD_GROUPS * SSD_STATE) // (gs * SSD_STATE)
    xs = pl.BlockSpec((L, gs * gw), lambda g, c: (ch(c), g))
    bm = pl.BlockSpec((L, gs * SSD_STATE), lambda g, c: (ch(c), b0 + g))
    cm = pl.BlockSpec((L, gs * SSD_STATE), lambda g, c: (ch(c), c0 + g))
    dt = pl.BlockSpec((gs, L, LANES), lambda g, c: (g, ch(c), 0))
    alog = pl.BlockSpec((gs, 1, LANES), lambda g, c: (g, 0, 0))
    st = pl.BlockSpec((gs, None, SSD_STATE, gw), lambda g, c: (g, ch(c), 0, 0))
    y = pl.BlockSpec((L, gs * gw), lambda g, c: (ch(c), g))
    grp = pl.BlockSpec((L, gs * SSD_STATE), lambda g, c: (ch(c), g))
    return xs, bm, cm, dt, alog, st, y, grp


def ssd_scan_fwd(act, dt4, alog4, rider=None):
    T = act.shape[0]
    nc = T // SSD_CHUNK
    gs = SCAN_GROUPS_FWD
    ng = SSD_GROUPS // gs
    gw = SSD_INNER // SSD_GROUPS
    xs_s, bm_s, cm_s, dt_s, alog_s, st_s, y_s, _ = _scan_specs(nc, False, gs)
    r_in, r_out, r_shapes, r_scratch, r_args = _rider_specs(rider)

    def body(xs_ref, bm_ref, cm_ref, dt_ref, alog_ref, y_ref, st_ref, st_scr):
        @pl.when(pl.program_id(1) == 0)
        def _():
            st_scr[...] = jnp.zeros_like(st_scr)

        for q in range(gs):
            xc, gc = pl.ds(q * gw, gw), pl.ds(q * SSD_STATE, SSD_STATE)
            st = st_scr[q]
            st_ref[q] = st
            y, st_new = _ssd_chunk(xs_ref[:, xc].astype(f32), bm_ref[:, gc].astype(f32), cm_ref[:, gc].astype(f32), dt_ref[q], alog_ref[q], st)
            y_ref[:, xc] = y.astype(bf16)
            st_scr[q] = st_new

    first = lambda: jnp.logical_and(pl.program_id(0) == 0, pl.program_id(1) == 0)
    last = lambda: jnp.logical_and(pl.program_id(0) == ng - 1, pl.program_id(1) == nc - 1)
    late = lambda: jnp.logical_and(pl.program_id(0) == ng - 1, pl.program_id(1) == (7 * nc) // 8)
    return pl.pallas_call(
        _carry(body, 5, 2, rider, first, last, late), name="ssd_scan_fwd" if rider is None else "ssd_scan_fwd_carrying", grid=(ng, nc),
        in_specs=[xs_s, bm_s, cm_s, dt_s, alog_s] + r_in, out_specs=[y_s, st_s] + r_out,
        out_shape=[jax.ShapeDtypeStruct((T, SSD_INNER), bf16), jax.ShapeDtypeStruct((SSD_GROUPS, nc, SSD_STATE, gw), f32)] + r_shapes,
        scratch_shapes=[pltpu.VMEM((gs, SSD_STATE, gw), f32)] + r_scratch,
        compiler_params=_params(("parallel" if rider is None else "arbitrary", "arbitrary")),
    )(act, act, act, dt4, alog4, *r_args)


def ssd_scan_bwd(act, dt4, alog4, states, dy, rider=None):
    T = act.shape[0]
    nc = T // SSD_CHUNK
    gs = SCAN_GROUPS_BWD
    ng = SSD_GROUPS // gs
    gw = SSD_INNER // SSD_GROUPS
    xs_s, bm_s, cm_s, dt_s, alog_s, st_s, y_s, grp_s = _scan_specs(nc, True, gs)
    r_in, r_out, r_shapes, r_scratch, r_args = _rider_specs(rider)

    def body(xs_ref, bm_ref, cm_ref, dt_ref, alog_ref, st_ref, dy_ref, dxs_ref, db_ref, dc_ref, ddt_ref, dalog_ref, dst_scr):
        @pl.when(pl.program_id(1) == 0)
        def _():
            dst_scr[...] = jnp.zeros_like(dst_scr)
            dalog_ref[...] = jnp.zeros_like(dalog_ref)

        for q in range(gs):
            xc, gc = pl.ds(q * gw, gw), pl.ds(q * SSD_STATE, SSD_STATE)
            _, vjp = jax.vjp(_ssd_chunk, xs_ref[:, xc].astype(f32), bm_ref[:, gc].astype(f32), cm_ref[:, gc].astype(f32),
                             dt_ref[q], alog_ref[q], st_ref[q])
            dxs, dbm, dcm, ddt, dalog, dst = vjp((dy_ref[:, xc].astype(f32), dst_scr[q]))
            dxs_ref[:, xc] = dxs.astype(bf16)
            db_ref[:, gc] = dbm.astype(bf16)
            dc_ref[:, gc] = dcm.astype(bf16)
            ddt_ref[q] = ddt
            dalog_ref[q] += dalog
            dst_scr[q] = dst

    first = lambda: jnp.logical_and(pl.program_id(0) == 0, pl.program_id(1) == 0)
    last = lambda: jnp.logical_and(pl.program_id(0) == ng - 1, pl.program_id(1) == nc - 1)
    return pl.pallas_call(
        _carry(body, 7, 5, rider, first, last), name="ssd_scan_bwd" if rider is None else "ssd_scan_bwd_carrying", grid=(ng, nc),
        in_specs=[xs_s, bm_s, cm_s, dt_s, alog_s, st_s, y_s] + r_in,
        out_specs=[y_s, grp_s, grp_s, dt_s, alog_s] + r_out,
        out_shape=[jax.ShapeDtypeStruct((T, SSD_INNER), bf16), jax.ShapeDtypeStruct((T, SSD_GROUPS * SSD_STATE), bf16),
                   jax.ShapeDtypeStruct((T, SSD_GROUPS * SSD_STATE), bf16), jax.ShapeDtypeStruct((SSD_GROUPS, T, LANES), f32),
                   jax.ShapeDtypeStruct((SSD_GROUPS, 1, LANES), f32)] + r_shapes,
        scratch_shapes=[pltpu.VMEM((gs, SSD_STATE, gw), f32)] + r_scratch,
        compiler_params=_params(("parallel" if rider is None else "arbitrary", "arbitrary")),
    )(act, act, act, dt4, alog4, states, dy, *r_args)


GATE_ROWS = 256


def _ssd_gate(y, xs, z, d_x, nw):
    g = (y + xs * d_x) * (z * _sigmoid(z))
    return g * lax.rsqrt(jnp.mean(g * g, axis=-1, keepdims=True) + RMS_EPS) * nw


def _gate_blocks(tm, fn):
    gw = SSD_INNER // SSD_GROUPS

    def block(r, carry):
        rows = pl.ds(r * GATE_ROWS if isinstance(r, int) else pl.multiple_of(r * GATE_ROWS, GATE_ROWS), GATE_ROWS)
        for k in range(SSD_GROUPS):
            fn(rows, pl.ds(k * gw, gw))
        return carry

    if tm == GATE_ROWS:
        block(0, 0)
    else:
        lax.fori_loop(0, tm // GATE_ROWS, block, 0)


def ssd_gate_fwd(y, act, z, d_x, nw, tm=512):
    T = y.shape[0]

    def body(y_ref, xs_ref, z_ref, d_ref, nw_ref, o_ref):
        def one(rows, cols):
            o_ref[rows, cols] = _ssd_gate(y_ref[rows, cols].astype(f32), xs_ref[rows, cols].astype(f32), z_ref[rows, cols].astype(f32),
                                          d_ref[:, cols], nw_ref[:, cols]).astype(bf16)

        _gate_blocks(tm, one)

    return pl.pallas_call(
        body, name="ssd_gate_fwd", grid=(T // tm,),
        in_specs=[_rows(tm, SSD_INNER), _rows(tm, SSD_INNER), _rows(tm, SSD_INNER), _resident(d_x.shape), _resident(nw.shape)],
        out_specs=_rows(tm, SSD_INNER), out_shape=jax.ShapeDtypeStruct((T, SSD_INNER), bf16),
        compiler_params=_params(("parallel",)),
    )(y, act, z, d_x, nw)


def ssd_gate_bwd(y, act, z, d_x, nw, dxo, w_out, tm=512):
    T = y.shape[0]

    def body(y_ref, xs_ref, z_ref, d_ref, nw_ref, dxo_ref, w_ref, dy_ref, dxs_ref, dz_ref, dd_ref, dnw_ref, dyb_ref):
        @pl.when(pl.program_id(0) == 0)
        def _():
            dd_ref[...] = jnp.zeros_like(dd_ref)
            dnw_ref[...] = jnp.zeros_like(dnw_ref)

        dyb_ref[...] = dxo_ref[...].astype(bf16)

        def one(rows, cols):
            _, vjp = jax.vjp(_ssd_gate, y_ref[rows, cols].astype(f32), xs_ref[rows, cols].astype(f32), z_ref[rows, cols].astype(f32),
                             d_ref[:, cols], nw_ref[:, cols])
            dy, dxs, dz, dd, dnw = vjp(_mm_nt(dyb_ref[rows, :], w_ref[cols, :]))
            dy_ref[rows, cols] = dy.astype(bf16)
            dxs_ref[rows, cols] = dxs.astype(bf16)
            dz_ref[rows, cols] = dz.astype(bf16)
            dd_ref[:, cols] += dd
            dnw_ref[:, cols] += dnw

        _gate_blocks(tm, one)

    const = pl.BlockSpec((1, SSD_INNER), lambda i: (0, 0))
    return pl.pallas_call(
        body, name="ssd_gate_bwd", grid=(T // tm,),
        in_specs=[_rows(tm, SSD_INNER), _rows(tm, SSD_INNER), _rows(tm, SSD_INNER), _resident(d_x.shape), _resident(nw.shape),
                  _rows(tm, D_MODEL), _resident(w_out.shape)],
        out_specs=[_rows(tm, SSD_INNER)] * 3 + [const, const, _rows(tm, D_MODEL)],
        out_shape=[jax.ShapeDtypeStruct((T, SSD_INNER), bf16)] * 3 + [jax.ShapeDtypeStruct((1, SSD_INNER), f32)] * 2
        + [jax.ShapeDtypeStruct((T, D_MODEL), bf16)],
        compiler_params=_params(("arbitrary",)),
    )(y, act, z, d_x, nw, dxo, w_out)


def sc_mid_fwd(bcu, conv_w, tm=512):
    T = bcu.shape[0]
    nt = T // tm
    Dm = D_MODEL

    def body(x_ref, halo_ref, w_ref, q_ref, ext_ref):
        first = pl.program_id(0) == 0
        n_s = Dm // LANES

        def strip(s, carry):
            cols, c_cols, u_cols = _strip(s), _strip(s + n_s), _strip(s + 2 * n_s)
            ext_ref[pl.ds(0, HALO), :] = jnp.where(first, 0.0, halo_ref[:, c_cols].astype(f32) * halo_ref[:, u_cols].astype(f32))
            ext_ref[pl.ds(HALO, tm), :] = x_ref[:, c_cols].astype(f32) * x_ref[:, u_cols].astype(f32)
            for r0 in range(0, tm, ROW_BLOCK):
                rows = pl.ds(r0, ROW_BLOCK)
                v, _ = _conv_rows(ext_ref, w_ref, cols, SC_CONV_K, r0)
                q_ref[rows, cols] = (x_ref[rows, cols].astype(f32) * v).astype(bf16)
            return carry

        lax.fori_loop(0, n_s, strip, 0)

    return pl.pallas_call(
        body, name="sc_mid_fwd", grid=(nt,),
        in_specs=[_rows(tm, 3 * Dm), _halo_spec(tm, 3 * Dm, nt, False), _resident(conv_w.shape)],
        out_specs=_rows(tm, Dm), out_shape=jax.ShapeDtypeStruct((T, Dm), bf16),
        scratch_shapes=[pltpu.VMEM((tm + HALO, LANES), f32)],
        compiler_params=_params(("parallel",)),
    )(bcu, bcu, conv_w)


def sc_mid_bwd(bcu, conv_w, dxo, w_out, tm=512):
    T = bcu.shape[0]
    nt = T // tm
    Dm = D_MODEL
    K = SC_CONV_K

    def body(x_ref, halo_ref, w_ref, dxo_ref, wo_ref, dx_ref, dw_ref, dyb_ref, ext_ref, dv_ref, carry_ref, dq_ref):
        i = pl.program_id(0)

        @pl.when(i == 0)
        def _():
            carry_ref[...] = jnp.zeros_like(carry_ref)
            dw_ref[...] = jnp.zeros_like(dw_ref)

        dyb = dxo_ref[...].astype(bf16)
        dyb_ref[...] = dyb
        dq_ref[...] = _mm_nt(dyb, wo_ref[...])
        first_tile = i == nt - 1
        n_s = Dm // LANES

        def strip(s, carry):
            cols, c_cols, u_cols = _strip(s), _strip(s + n_s), _strip(s + 2 * n_s)
            ext_ref[pl.ds(0, HALO), :] = jnp.where(first_tile, 0.0, halo_ref[:, c_cols].astype(f32) * halo_ref[:, u_cols].astype(f32))
            ext_ref[pl.ds(HALO, tm), :] = x_ref[:, c_cols].astype(f32) * x_ref[:, u_cols].astype(f32)
            dv_ref[pl.ds(tm, 8), :] = carry_ref[:, cols]
            dws = [jnp.zeros((1, LANES), f32) for _ in range(K)]
            for r0 in range(0, tm, ROW_BLOCK):
                rows = pl.ds(r0, ROW_BLOCK)
                v, wins = _conv_rows(ext_ref, w_ref, cols, K, r0)
                dqv = dq_ref[rows, cols]
                dv = dqv * x_ref[rows, cols].astype(f32)
                dv_ref[rows, :] = dv
                dx_ref[rows, cols] = (dqv * v).astype(bf16)
                for k in range(K):
                    dws[k] = dws[k] + jnp.sum(dv * wins[k], axis=0, keepdims=True)
            carry_ref[:, cols] = dv_ref[pl.ds(0, 8), :]
            for r0 in range(0, tm, ROW_BLOCK):
                rows = pl.ds(r0, ROW_BLOCK)
                dp = _shifted_back(dv_ref, w_ref, cols, K, r0)
                dx_ref[rows, c_cols] = (dp * x_ref[rows, u_cols].astype(f32)).astype(bf16)
                dx_ref[rows, u_cols] = (dp * x_ref[rows, c_cols].astype(f32)).astype(bf16)
            for k in range(K):
                dw_ref[pl.ds(k, 1), cols] += dws[k]
            return carry

        lax.fori_loop(0, n_s, strip, 0)

    return pl.pallas_call(
        body, name="sc_mid_bwd", grid=(nt,),
        in_specs=[_tile_spec(tm, 3 * Dm, nt, True), _halo_spec(tm, 3 * Dm, nt, True), _resident(conv_w.shape), _tile_spec(tm, Dm, nt, True),
                  _resident(w_out.shape)],
        out_specs=[_tile_spec(tm, 3 * Dm, nt, True), pl.BlockSpec((8, Dm), lambda i: (0, 0)), _tile_spec(tm, Dm, nt, True)],
        out_shape=[jax.ShapeDtypeStruct((T, 3 * Dm), bf16), jax.ShapeDtypeStruct((8, Dm), f32), jax.ShapeDtypeStruct((T, Dm), bf16)],
        scratch_shapes=[pltpu.VMEM((tm + HALO, LANES), f32), pltpu.VMEM((tm + 8, LANES), f32), pltpu.VMEM((8, Dm), f32),
                        pltpu.VMEM((tm, Dm), f32)],
        compiler_params=_params(("arbitrary",)),
    )(bcu, bcu, conv_w, dxo, w_out)


def loss_head(x, fw, target, tm=1024):
    T = x.shape[0]

    def body(x_ref, fw_ref, t_ref, loss_ref, dx_ref, dfw_ref):
        @pl.when(pl.program_id(0) == 0)
        def _():
            loss_ref[...] = jnp.zeros_like(loss_ref)
            dfw_ref[...] = jnp.zeros_like(dfw_ref)

        w = fw_ref[...]
        y, xh, inv = _rms_fwd(x_ref[...], w)
        err = y - t_ref[...]
        loss_ref[...] += 0.5 * jnp.sum(jnp.mean(err * err, axis=-1, keepdims=True), axis=0, keepdims=True)
        dx, dw = _rms_bwd(err * (1.0 / D_MODEL), xh, inv, w)
        dx_ref[...] = dx
        dfw_ref[...] += dw

    return pl.pallas_call(
        body, name="loss_head", grid=(T // tm,),
        in_specs=[_rows(tm, D_MODEL), _resident((1, D_MODEL)), _rows(tm, D_MODEL)],
        out_specs=[pl.BlockSpec((1, LANES), lambda i: (0, 0)), _rows(tm, D_MODEL), pl.BlockSpec((1, D_MODEL), lambda i: (0, 0))],
        out_shape=[jax.ShapeDtypeStruct((1, LANES), f32), jax.ShapeDtypeStruct((T, D_MODEL), f32), jax.ShapeDtypeStruct((1, D_MODEL), f32)],
        compiler_params=_params(("arbitrary",)),
    )(x, fw, target)


ELEMENTWISE_TILE_BYTES = 1_600_000


def _row_tile(rows, width):
    row_bytes = 4 * _round_up(width, LANES)
    tile = rows
    while tile * row_bytes > ELEMENTWISE_TILE_BYTES and tile % 16 == 0:
        tile //= 2
    return tile


def adamw(g_parts, w, m, v, name="adamw", a0=0, prev=None):
    A, B, n = w.shape
    tb = _row_tile(B, n)
    n_parts = len(g_parts)
    arrays, specs = [], []
    for part in g_parts:
        lead, arr = part if isinstance(part, tuple) else ((), part)
        specs.append(pl.BlockSpec((None,) * (len(lead) + 1) + (tb, n), lambda a, t, lead=lead: tuple(lead) + (a, t, 0)))
        arrays.append(arr)
    na = arrays[0].shape[-3]
    prev = list(prev) if prev is not None else []

    def body(*refs):
        n = n_parts
        g_refs = refs[:n]
        w_ref, m_ref, v_ref = refs[n:n + 3]
        go_ref, d_ref, mo_ref, vo_ref = refs[n + 3 + len(prev):]
        g = g_refs[0][...].astype(f32)
        for r in g_refs[1:]:
            g = g + r[...].astype(f32)
        m_new = ADAM_B1 * m_ref[...] + (1.0 - ADAM_B1) * g
        v_new = ADAM_B2 * v_ref[...] + (1.0 - ADAM_B2) * (g * g)
        m_hat = m_new / (1.0 - ADAM_B1 ** ADAM_STEP)
        v_hat = v_new / (1.0 - ADAM_B2 ** ADAM_STEP)
        go_ref[...] = g
        d_ref[...] = -ADAM_LR * (m_hat / (jnp.sqrt(v_hat) + ADAM_EPS) + ADAM_WD * w_ref[...])
        mo_ref[...] = m_new
        vo_ref[...] = v_new

    plain = pl.BlockSpec((None, tb, n), lambda a, t: (a + a0, t, 0))
    return pl.pallas_call(
        body, name=name, grid=(na, B // tb), in_specs=specs + [plain] * 3 + [_ANY] * len(prev), out_specs=[plain] * 4,
        out_shape=[jax.ShapeDtypeStruct((A, B, n), f32)] * 4,
        input_output_aliases={n_parts + 3 + k: k for k in range(len(prev))},
        compiler_params=_params(("parallel", "parallel")),
    )(*arrays, w, m, v, *prev)


def pair_sum_bf16(ga, gb, name):
    _, A, B, n = gb.shape
    tb = _row_tile(B, n)

    def body(a_ref, b_ref, o_ref):
        o_ref[...] = (a_ref[...] + b_ref[...]).astype(bf16)

    return pl.pallas_call(
        body, name=name, grid=(3, A, B // tb),
        in_specs=[pl.BlockSpec((None, None, None, tb, n), lambda j, a, t: (0, j + 1, a, t, 0)),
                  pl.BlockSpec((None, None, tb, n), lambda j, a, t: (j + 1, a, t, 0))],
        out_specs=pl.BlockSpec((None, None, tb, n), lambda j, a, t: (j + 1, a, t, 0)),
        out_shape=jax.ShapeDtypeStruct((4, A, B, n), bf16),
        compiler_params=_params(("parallel", "parallel", "parallel")),
    )(ga, gb)


def assemble(gathered, axis, tk=256):
    _, A, K, n = gathered.shape
    if axis == 1:
        def body(w_ref, o_ref):
            o_ref[...] = jnp.concatenate([w_ref[j] for j in range(N_DEV)], axis=1)

        return pl.pallas_call(
            body, name=f"assemble_cols_{K}x{n}", grid=(A, K // tk),
            in_specs=[pl.BlockSpec((N_DEV, None, tk, n), lambda a, t: (0, a, t, 0))],
            out_specs=pl.BlockSpec((None, tk, N_DEV * n), lambda a, t: (a, t, 0)),
            out_shape=jax.ShapeDtypeStruct((A, K, N_DEV * n), gathered.dtype),
            compiler_params=_params(("parallel", "parallel")),
        )(gathered)

    def body(w_ref, o_ref):
        for j in range(N_DEV):
            o_ref[pl.ds(j * K, K), :] = w_ref[j]

    return pl.pallas_call(
        body, name=f"assemble_rows_{K}x{n}", grid=(A,),
        in_specs=[pl.BlockSpec((N_DEV, None, K, n), lambda a: (0, a, 0, 0))],
        out_specs=pl.BlockSpec((None, N_DEV * K, n), lambda a: (a, 0, 0)),
        out_shape=jax.ShapeDtypeStruct((A, N_DEV * K, n), gathered.dtype),
        compiler_params=_params(("parallel",)),
    )(gathered)


SSD_IN_PAD = -(-SSD_IN_DIM // LANES) * LANES


def assemble_ssd_in(gathered, tk=256):
    _, A, K, n = gathered.shape

    def body(w_ref, z_ref, x_ref, dt_ref, full_ref):
        full_ref[:, pl.ds(SSD_IN_PAD - LANES, LANES)] = jnp.zeros((tk, LANES), gathered.dtype)
        for j in range(N_DEV):
            full_ref[:, pl.ds(j * n, n)] = w_ref[j]
        z_ref[...] = full_ref[:, pl.ds(0, SSD_INNER)]
        x_ref[...] = full_ref[:, pl.ds(SSD_INNER, SSD_CONV_DIM)]
        dt_ref[...] = full_ref[:, pl.ds(SSD_INNER + SSD_CONV_DIM, LANES)]

    widths = (SSD_INNER, SSD_CONV_DIM, LANES)
    return pl.pallas_call(
        body, name="assemble_ssd_in", grid=(A, K // tk),
        in_specs=[pl.BlockSpec((N_DEV, None, tk, n), lambda a, t: (0, a, t, 0))],
        out_specs=[pl.BlockSpec((None, tk, w), lambda a, t: (a, t, 0)) for w in widths],
        out_shape=[jax.ShapeDtypeStruct((A, K, w), gathered.dtype) for w in widths],
        scratch_shapes=[pltpu.VMEM((tk, SSD_IN_PAD), gathered.dtype)],
        compiler_params=_params(("parallel", "parallel")),
    )(gathered)


def ssd_in_to_shards(dwz, dwx, dwdt, buf, j, tk=256):
    K = dwz.shape[0]
    n = buf.shape[-1]
    fresh = isinstance(buf, jax.ShapeDtypeStruct)

    def body(z_ref, x_ref, dt_ref, *rest):
        o_ref, full_ref = rest[-2:]
        full_ref[:, pl.ds(0, SSD_INNER)] = z_ref[...]
        full_ref[:, pl.ds(SSD_INNER, SSD_CONV_DIM)] = x_ref[...]
        full_ref[:, pl.ds(SSD_INNER + SSD_CONV_DIM, LANES)] = dt_ref[...]
        my_c, my_chip = _my_core_and_chip()
        for d in range(N_DEV):
            o_ref[(d % 2) ^ my_c, (d // 2) ^ my_chip] = full_ref[:, pl.ds(d * n, n)]

    return pl.pallas_call(
        body, name="ssd_in_to_shards", grid=(K // tk,),
        in_specs=[_rows(tk, SSD_INNER), _rows(tk, SSD_CONV_DIM), _rows(tk, LANES)] + ([] if fresh else [_ANY]),
        out_specs=pl.BlockSpec((2, 4, None, tk, n), lambda t: (0, 0, j, t, 0)),
        out_shape=jax.ShapeDtypeStruct(buf.shape, f32),
        scratch_shapes=[pltpu.VMEM((tk, SSD_IN_PAD), f32)],
        input_output_aliases={} if fresh else {3: 0},
        compiler_params=_params(("parallel",)),
    )(dwz, dwx, dwdt, *([] if fresh else [buf]))


def sum_over_devices(gathered):
    _, R, W = gathered.shape

    def body(g_ref, o_ref):
        acc = g_ref[0]
        for k in range(1, N_DEV):
            acc = acc + g_ref[k]
        o_ref[...] = acc

    return pl.pallas_call(
        body, name="sum_over_devices", grid=(1,),
        in_specs=[pl.BlockSpec((N_DEV, R, W), lambda i: (0, 0, 0))], out_specs=pl.BlockSpec((R, W), lambda i: (0, 0)),
        out_shape=jax.ShapeDtypeStruct((R, W), f32), compiler_params=_params(("arbitrary",)),
    )(gathered)


_ANY = pl.BlockSpec(memory_space=pl.ANY)


class _Exchange:
    def __init__(self, inputs, out_shapes, scratch, start, finish, relay=None):
        self.inputs, self.out_shapes, self.scratch, self.start, self.finish = inputs, out_shapes, scratch, start, finish
        self.relay = relay

    def run(self, name):
        ni, no = len(self.inputs), len(self.out_shapes)

        def body(*refs):
            parts = (refs[:ni], refs[ni:ni + no], refs[ni + no:])
            self.start(*parts)
            if self.relay is not None:
                self.relay(*parts)
            self.finish(*parts)

        return pl.pallas_call(body, name=name, in_specs=[_ANY] * ni, out_specs=[_ANY] * no, out_shape=self.out_shapes,
                              scratch_shapes=self.scratch)(*self.inputs)


def _carry(body, n_in, n_out, rider, first, last, late=None):
    if rider is None:
        return body
    ri, ro = len(rider.inputs), len(rider.out_shapes)

    def hosted(*refs):
        a, b, c = n_in + ri, n_in + ri + n_out, n_in + ri + n_out + ro
        rs = len(refs) - c - len(rider.scratch)
        parts = (refs[n_in:a], refs[b:c], refs[c + rs:])

        @pl.when(first())
        def _():
            rider.start(*parts)

        if rider.relay is not None and late is not None:
            @pl.when(late())
            def _():
                rider.relay(*parts)

        body(*refs[:n_in], *refs[a:b], *refs[c:c + rs])

        @pl.when(last())
        def _():
            if rider.relay is not None and late is None:
                rider.relay(*parts)
            rider.finish(*parts)

    return hosted


def _rider_specs(rider):
    if rider is None:
        return [], [], [], [], []
    return [_ANY] * len(rider.inputs), [_ANY] * len(rider.out_shapes), list(rider.out_shapes), list(rider.scratch), list(rider.inputs)


def all_gather(blocks):
    n = len(blocks)

    def plan(x_refs, out_refs, sems):
        send_sems, recv_sems, local_sems = sems
        x, y, c = lax.axis_index("x"), lax.axis_index("y"), lax.axis_index("c")
        me, sibling = (x, y, c), (x, y, 1 - c)
        chips = [(1 - x, y), (x, 1 - y), (1 - x, 1 - y)]

        def copy(a, k, blk, to, src=None):
            px, py, pc = blk
            slot = out_refs[a].at[4 * px + 2 * py + pc]
            return pltpu.make_async_remote_copy(
                src_ref=slot if src is None else src, dst_ref=slot,
                send_sem=send_sems.at[7 * a + k], recv_sem=recv_sems.at[7 * a + k], device_id=to, device_id_type=MESH)

        mine = [pltpu.make_async_copy(x_refs[a], out_refs[a].at[4 * x + 2 * y + c], local_sems.at[a]) for a in range(n)]
        first = []
        for a in range(n):
            first += [copy(a, 0, me, sibling, src=x_refs[a])] + [copy(a, 1 + j, me, (*chip, c), src=x_refs[a]) for j, chip in enumerate(chips)]
        return c, me, sibling, chips, copy, mine, first

    def start(x_refs, out_refs, sems):
        _, _, _, _, _, mine, first = plan(x_refs, out_refs, sems)
        for cp in mine + first:
            cp.start()

    def relay(x_refs, out_refs, sems):
        c, me, sibling, chips, copy, _, _ = plan(x_refs, out_refs, sems)
        for j, chip in enumerate(chips):
            for a in range(n):
                copy(a, 1 + j, (*chip, c), me).wait_recv()
                copy(a, 4 + j, (*chip, c), sibling).start()

    def finish(x_refs, out_refs, sems):
        c, me, sibling, chips, copy, mine, first = plan(x_refs, out_refs, sems)
        passed = [copy(a, 4 + j, (*chip, c), sibling) for j, chip in enumerate(chips) for a in range(n)]
        for a in range(n):
            copy(a, 0, sibling, me).wait_recv()
            for j, chip in enumerate(chips):
                copy(a, 4 + j, (*chip, 1 - c), me).wait_recv()
        for cp in first + passed:
            cp.wait_send()
        for cp in mine:
            cp.wait()

    return _Exchange(list(blocks), [jax.ShapeDtypeStruct((N_DEV,) + b.shape, b.dtype) for b in blocks],
                     [pltpu.SemaphoreType.DMA((7 * n,)), pltpu.SemaphoreType.DMA((7 * n,)), pltpu.SemaphoreType.DMA((n,))], start, finish,
                     relay=relay)


def exchange_with_sibling(gs):
    n = len(gs)

    def plan(g_refs, recv_refs, sems):
        send_sems, recv_sems = sems
        x, y, c = lax.axis_index("x"), lax.axis_index("y"), lax.axis_index("c")
        return [pltpu.make_async_remote_copy(src_ref=g_refs[a].at[1], dst_ref=recv_refs[a], send_sem=send_sems.at[a],
                                             recv_sem=recv_sems.at[a], device_id=(x, y, 1 - c), device_id_type=MESH) for a in range(n)]

    def start(*refs):
        for cp in plan(*refs):
            cp.start()

    def finish(*refs):
        for cp in plan(*refs):
            cp.wait()

    return _Exchange(list(gs), [jax.ShapeDtypeStruct(g.shape[1:], g.dtype) for g in gs],
                     [pltpu.SemaphoreType.DMA((n,)), pltpu.SemaphoreType.DMA((n,))], start, finish)


def exchange_between_chips(parts):
    n = len(parts)

    def plan(p_refs, recv_refs, sems):
        send_sems, recv_sems = sems
        x, y, c = lax.axis_index("x"), lax.axis_index("y"), lax.axis_index("c")
        chips = [(2, (1 - x, y)), (1, (x, 1 - y)), (3, (1 - x, 1 - y))]
        return [pltpu.make_async_remote_copy(src_ref=p_refs[a].at[slot], dst_ref=recv_refs[a].at[k], send_sem=send_sems.at[3 * a + k],
                                             recv_sem=recv_sems.at[3 * a + k], device_id=(px, py, c), device_id_type=MESH)
                for a in range(n) for k, (slot, (px, py)) in enumerate(chips)]

    def start(*refs):
        for cp in plan(*refs):
            cp.start()

    def finish(*refs):
        for cp in plan(*refs):
            cp.wait()

    return _Exchange(list(parts), [jax.ShapeDtypeStruct((3,) + p.shape[1:], p.dtype) for p in parts],
                     [pltpu.SemaphoreType.DMA((3 * n,)), pltpu.SemaphoreType.DMA((3 * n,))], start, finish)


PARAMS = {
    "norm_w": ((DEPTH, 3, D_MODEL), 2),
    "ffn_w_gate": ((DEPTH, 2, D_MODEL, D_FF), 3),
    "ffn_w_up": ((DEPTH, 2, D_MODEL, D_FF), 3),
    "ffn_w_down": ((DEPTH, 2, D_FF, D_MODEL), 2),
    "ssd_w_in": ((2, D_MODEL, SSD_IN_DIM), 2),
    "ssd_conv_w": ((2, SSD_CONV_K, SSD_CONV_DIM), 2),
    "ssd_conv_b": ((2, SSD_CONV_DIM), None),
    "ssd_dt_bias": ((2, SSD_HEADS), None),
    "ssd_a_log": ((2, SSD_HEADS), None),
    "ssd_d": ((2, SSD_HEADS), None),
    "ssd_norm_w": ((2, SSD_INNER), None),
    "ssd_w_out": ((2, SSD_INNER, D_MODEL), 1),
    "sc_w_in": ((2, D_MODEL, 3 * D_MODEL), 2),
    "sc_conv_w": ((2, SC_CONV_K, D_MODEL), 2),
    "sc_w_out": ((2, D_MODEL, D_MODEL), 1),
    "final_norm_w": ((D_MODEL,), None),
}
NAMES = list(PARAMS)
BIG = ["ffn_w_gate", "ffn_w_up", "ffn_w_down", "ssd_w_in", "ssd_w_out", "sc_w_in", "sc_w_out"]
SMALL = [n for n in NAMES if n not in BIG]
SMALL_SHARDED = [n for n in SMALL if PARAMS[n][1] is not None]


def _round_up(n, m):
    return -(-n // m) * m


def _pack(flat_list, rows_multiple):
    flat = jnp.concatenate(flat_list)
    rows = _round_up(_round_up(flat.shape[0], PACK_W) // PACK_W, rows_multiple)
    return jnp.pad(flat, (0, rows * PACK_W - flat.shape[0])).reshape(rows, PACK_W)


def _unpack(packed, shapes, lead=()):
    flat = packed.reshape(lead + (-1,))
    out, off = [], 0
    for shp in shapes:
        n = 1
        for s in shp:
            n *= s
        out.append(flat[..., off:off + n].reshape(lead + tuple(shp)))
        off += n
    return out


def _local_shape(name):
    shp, ax = PARAMS[name]
    if ax is None:
        return shp
    return shp[:ax] + (shp[ax] // N_DEV,) + shp[ax + 1:]


def _full_from_gathered(g, name):
    shp, ax = PARAMS[name]
    return jnp.moveaxis(g, 0, ax).reshape(shp)


def _by_destination(full, name):
    shp, ax = PARAMS[name]
    loc = shp[ax] // N_DEV
    return jnp.moveaxis(full.reshape(shp[:ax] + (N_DEV, loc) + shp[ax + 1:]), ax, 0)


def _ssd_layer_fwd(xin, nw, p, rider=None):
    z, xbc, dt_raw = in_proj_fwd(xin, nw, [p["ssd_wz"], p["ssd_wx"], p["ssd_wdt"]], [bf16, bf16, f32])
    act, dt4 = ssd_conv_fwd(xbc, p["ssd_conv_w"], p["ssd_conv_b"], dt_raw, p["ssd_dt_bias"])
    y, states, *got = ssd_scan_fwd(act, dt4, p["ssd_alog4"], rider=rider)
    gn = ssd_gate_fwd(y, act, z, p["ssd_dx"], p["ssd_norm_w"])
    xout = out_proj_fwd(xin, gn, p["ssd_w_out"])
    return xout, (xin, z, xbc, dt_raw, act, dt4, y, states, gn), got


def _ssd_layer_bwd(dxo, nw, p, saved, gbuf, slab, rider=None):
    xin, z, xbc, dt_raw, act, dt4, y, states, gn = saved
    T = xin.shape[0]
    dy, dxs_skip, dz, dd_x, dgnw, dyb = ssd_gate_bwd(y, act, z, p["ssd_dx"], p["ssd_norm_w"], dxo, p["ssd_w_out"])
    gbuf["ssd_w_out"] = tn_matmul_to_shards(gn, dyb, gbuf["ssd_w_out"], (slab,), 0)
    g = {}
    g["ssd_norm_w"] = dgnw[0]
    g["ssd_d"] = jnp.sum(dd_x.reshape(SSD_HEADS, SSD_HEAD_DIM), axis=1)
    dxs, db, dc, ddt4, dalog4, *got = ssd_scan_bwd(act, dt4, p["ssd_alog4"], states, dy, rider=rider)
    g["ssd_a_log"] = dalog4[:, 0, :8].reshape(SSD_HEADS)
    dxbc, ddt_raw, dcw, dcb, ddtb = ssd_conv_bwd(xbc, p["ssd_conv_w"], p["ssd_conv_b"], dt_raw, p["ssd_dt_bias"], dxs, dxs_skip, db, dc, ddt4)
    g["ssd_conv_w"] = dcw[:SSD_CONV_K]
    g["ssd_conv_b"] = dcb[0]
    g["ssd_dt_bias"] = ddtb[0, :SSD_HEADS]
    dx, h, dnw = in_proj_bwd(xin, nw, dxo, [dz, dxbc, ddt_raw], [p["ssd_wz"], p["ssd_wx"], p["ssd_wdt"]])
    gbuf["ssd_w_in"] = ssd_in_to_shards(tn_matmul(h, dz), tn_matmul(h, dxbc), tn_matmul(h, ddt_raw), gbuf["ssd_w_in"], slab)
    return dx, dnw, g, got


def _sc_layer_fwd(xin, nw, p):
    (bcu,) = in_proj_fwd(xin, nw, [p["sc_w_in"]], [bf16])
    q = sc_mid_fwd(bcu, p["sc_conv_w"])
    return out_proj_fwd(xin, q, p["sc_w_out"]), (xin, bcu, q)


def _sc_layer_bwd(dxo, nw, p, saved, gbuf, slab):
    xin, bcu, q = saved
    dbcu, dcw, dyb = sc_mid_bwd(bcu, p["sc_conv_w"], dxo, p["sc_w_out"])
    gbuf["sc_w_out"] = tn_matmul_to_shards(q, dyb, gbuf["sc_w_out"], (slab,), 0)
    g = {"sc_conv_w": dcw[:SC_CONV_K]}
    dx, h, dnw = in_proj_bwd(xin, nw, dxo, [dbcu], [p["sc_w_in"]])
    gbuf["sc_w_in"] = tn_matmul_to_shards(h, dbcu, gbuf["sc_w_in"], (slab,), 1)
    return dx, dnw, g


def kernel(x, norm_w, ffn_w_gate, ffn_w_up, ffn_w_down, ssd_w_in, ssd_conv_w, ssd_conv_b, ssd_dt_bias, ssd_a_log, ssd_d, ssd_norm_w, ssd_w_out, sc_w_in, sc_conv_w, sc_w_out, final_norm_w, loss_target, m_norm_w, m_ffn_w_gate, m_ffn_w_up, m_ffn_w_down, m_ssd_w_in, m_ssd_conv_w, m_ssd_conv_b, m_ssd_dt_bias, m_ssd_a_log, m_ssd_d, m_ssd_norm_w, m_ssd_w_out, m_sc_w_in, m_sc_conv_w, m_sc_w_out, m_final_norm_w, v_norm_w, v_ffn_w_gate, v_ffn_w_up, v_ffn_w_down, v_ssd_w_in, v_ssd_conv_w, v_ssd_conv_b, v_ssd_dt_bias, v_ssd_a_log, v_ssd_d, v_ssd_norm_w, v_ssd_w_out, v_sc_w_in, v_sc_conv_w, v_sc_w_out, v_final_norm_w):
    w_loc = dict(zip(NAMES, (norm_w, ffn_w_gate, ffn_w_up, ffn_w_down, ssd_w_in, ssd_conv_w, ssd_conv_b, ssd_dt_bias, ssd_a_log, ssd_d, ssd_norm_w, ssd_w_out, sc_w_in, sc_conv_w, sc_w_out, final_norm_w)))
    m_loc = dict(zip(NAMES, (m_norm_w, m_ffn_w_gate, m_ffn_w_up, m_ffn_w_down, m_ssd_w_in, m_ssd_conv_w, m_ssd_conv_b, m_ssd_dt_bias, m_ssd_a_log, m_ssd_d, m_ssd_norm_w, m_ssd_w_out, m_sc_w_in, m_sc_conv_w, m_sc_w_out, m_final_norm_w)))
    v_loc = dict(zip(NAMES, (v_norm_w, v_ffn_w_gate, v_ffn_w_up, v_ffn_w_down, v_ssd_w_in, v_ssd_conv_w, v_ssd_conv_b, v_ssd_dt_bias, v_ssd_a_log, v_ssd_d, v_ssd_norm_w, v_ssd_w_out, v_sc_w_in, v_sc_conv_w, v_sc_w_out, v_final_norm_w)))
    my_dev = 4 * lax.axis_index("x") + 2 * lax.axis_index("y") + lax.axis_index("c")

    def as3d(a):
        return a.reshape((-1,) + a.shape[-2:])

    wb = {n: as3d(w_loc[n]).astype(bf16) for n in BIG}

    FFN = ["ffn_w_gate", "ffn_w_up", "ffn_w_down"]

    def mixer_names(i):
        return ["ssd_w_in", "ssd_w_out"] if i % 2 == 0 else ["sc_w_in", "sc_w_out"]

    ag_sets = [[(n, 0, 1) for n in FFN], [(n, 1, 1) for n in FFN] + [(n, 0, 1) for n in mixer_names(0)]]
    ag_sets += [[(n, 2 * r, 2) for n in FFN] + [(n, r // 2, 1) for n in mixer_names(r)] for r in (1, 2, 3)]

    def set_blocks(spec):
        return [wb[n][a0:a0 + na] for n, a0, na in spec]

    def set_weights(spec, gathered):
        q = {}
        for (n, _, _), g in zip(spec, gathered):
            if n == "ssd_w_in":
                q["ssd_wz"], q["ssd_wx"], q["ssd_wdt"] = assemble_ssd_in(g)
            else:
                q[n] = assemble(g, 1 if PARAMS[n][1] == len(PARAMS[n][0]) - 1 else 0)
        return q

    ss_shapes = [_local_shape(n) for n in SMALL_SHARDED]
    gathered0 = all_gather(set_blocks(ag_sets[0]) + [_pack([w_loc[n].reshape(-1) for n in SMALL_SHARDED], 8)]).run("all_gather_first")
    full = {}
    for n, part in zip(SMALL_SHARDED, _unpack(gathered0[-1], ss_shapes, lead=(N_DEV,))):
        full[n] = _full_from_gathered(part, n)
    for n in SMALL:
        if PARAMS[n][1] is None:
            full[n] = w_loc[n]
    small = {
        "ssd_conv_w": full["ssd_conv_w"],
        "ssd_conv_b": full["ssd_conv_b"].reshape(2, 1, SSD_CONV_DIM),
        "ssd_dt_bias": jnp.pad(full["ssd_dt_bias"], ((0, 0), (0, LANES - SSD_HEADS))).reshape(2, 1, LANES),
        "ssd_alog4": jnp.pad(full["ssd_a_log"].reshape(2, SSD_GROUPS, 1, 8), ((0, 0), (0, 0), (0, 0), (0, LANES - 8))),
        "ssd_dx": jnp.repeat(full["ssd_d"], SSD_HEAD_DIM, axis=1).reshape(2, 1, SSD_INNER),
        "ssd_norm_w": full["ssd_norm_w"].reshape(2, 1, SSD_INNER),
        "sc_conv_w": full["sc_conv_w"],
    }
    nw_all = full["norm_w"].reshape(DEPTH, 3, 1, D_MODEL)

    ffn_w = [[None, None] for _ in range(DEPTH)]
    mix_w = [None] * DEPTH

    def arrived(s, gathered):
        q = set_weights(ag_sets[s], gathered)
        ffn = tuple(q[n] for n in FFN)
        if s == 0:
            ffn_w[0][0] = ffn + ((0,),)
            return
        i = 0 if s == 1 else s - 1
        if s == 1:
            ffn_w[0][1] = ffn + ((0,),)
        else:
            ffn_w[i] = [ffn + ((0,),), ffn + ((1,),)]
        m = {n: v[0] for n, v in q.items() if n not in FFN}
        m.update({n: v[i // 2] for n, v in small.items() if n.startswith("ssd" if i % 2 == 0 else "sc")})
        mix_w[i] = m

    def rider_for(s):
        return all_gather(set_blocks(ag_sets[s]))

    xc = x[0]
    saved = []
    arrived(0, gathered0[:-1])
    for i in range(DEPTH):
        carried = {0: (1, 2, 3), 1: (4, None, None)}.get(i, (None, None, None))
        wg, wu, wd, idx = ffn_w[i][0]
        x1, g1, u1, a1, *got = ffn_fwd(xc, nw_all[i, 0], wg, wu, wd, idx, rider=rider_for(carried[0]) if carried[0] else None)
        if carried[0]:
            arrived(carried[0], got)
        if i % 2 == 0:
            x2, mix_saved, got = _ssd_layer_fwd(x1, nw_all[i, 1], mix_w[i], rider=rider_for(carried[1]) if carried[1] else None)
            if carried[1]:
                arrived(carried[1], got)
        else:
            x2, mix_saved = _sc_layer_fwd(x1, nw_all[i, 1], mix_w[i])
        wg, wu, wd, idx = ffn_w[i][1]
        x3, g3, u3, a3, *got = ffn_fwd(x2, nw_all[i, 2], wg, wu, wd, idx, rider=rider_for(carried[2]) if carried[2] else None)
        if carried[2]:
            arrived(carried[2], got)
        saved.append(((xc, g1, u1, a1), mix_saved, (x2, g3, u3, a3)))
        xc = x3

    loss_row, dx, dfw = loss_head(xc, full["final_norm_w"].reshape(1, D_MODEL), loss_target[0])
    loss = lax.psum(loss_row[0, 0], ("x", "y", "c"))

    grads = {n: [None] * PARAMS[n][0][0] for n in SMALL if n != "final_norm_w"}
    grads["final_norm_w"] = dfw[0]
    dnorm = [[None] * 3 for _ in range(DEPTH)]
    def slabs(n, which):
        if n.startswith("ffn"):
            return {"early": (2, 6), "mid": (1, 1), "last": (0, 1)}[which]
        if n.startswith("ssd"):
            return {"early": (1, 1), "mid": (0, 1), "last": (0, 0)}[which]
        return {"early": (0, 2), "mid": (0, 0), "last": (0, 0)}[which]

    gb = {which: {n: jax.ShapeDtypeStruct((2, 4, slabs(n, which)[1]) + wb[n].shape[1:], f32) for n in BIG if slabs(n, which)[1]}
          for which in ("early", "mid", "last")}

    def ffn_back(i, k, dxo, sv, rider=None):
        xin, g_, u_, a_ = sv
        which = "early" if i > 0 else ("mid" if k == 1 else "last")
        gbuf = gb[which]
        slab = 2 * i + k - slabs("ffn_w_gate", which)[0]
        wg, wu, wd, idx = ffn_w[i][k]
        dxi, h, dyb, dg, du, dnw, *got = ffn_bwd_dx(xin, dxo, g_, u_, nw_all[i, 2 * k], wg, wu, wd, idx, rider=rider)
        dnorm[i][2 * k] = dnw[0]
        gbuf["ffn_w_gate"] = tn_matmul_to_shards(h, dg, gbuf["ffn_w_gate"], (slab,), 1)
        gbuf["ffn_w_up"] = tn_matmul_to_shards(h, du, gbuf["ffn_w_up"], (slab,), 1)
        gbuf["ffn_w_down"] = tn_matmul_to_shards(a_, dyb, gbuf["ffn_w_down"], (slab,), 0)
        return dxi, got

    def reduce_in_chip(gbuf, from_sibling=None):
        names = list(gbuf)
        bufs = [gbuf[n] for n in names]
        if from_sibling is None:
            from_sibling = exchange_with_sibling(bufs).run("exchange_with_sibling")
        return names, bufs, from_sibling, [pair_sum_bf16(g, fs, "pair_sum_" + n) for n, g, fs in zip(names, bufs, from_sibling)]

    reduced, from_chips = {}, {}
    for i in reversed(range(DEPTH)):
        j = i // 2
        sv_a, sv_mix, sv_b = saved[i]
        if i == 0:
            dx, got = ffn_back(i, 1, dx, sv_b, rider=exchange_with_sibling(list(gb["early"].values())))
            reduced["early"] = reduce_in_chip(gb["early"], from_sibling=got)
        else:
            dx, _ = ffn_back(i, 1, dx, sv_b)
        if i % 2 == 0:
            rider = exchange_between_chips(reduced["early"][3]) if i == 0 else None
            dx, dnw, gm, got = _ssd_layer_bwd(dx, nw_all[i, 1], mix_w[i], sv_mix, gb["mid" if i == 0 else "early"], 0, rider=rider)
            if i == 0:
                from_chips["early"] = got
                reduced["mid"] = reduce_in_chip(gb["mid"])
        else:
            dx, dnw, gm = _sc_layer_bwd(dx, nw_all[i, 1], mix_w[i], sv_mix, gb["early"], j)
        dnorm[i][1] = dnw[0]
        for n, val in gm.items():
            grads[n][j] = val
        dx, got = ffn_back(i, 0, dx, sv_a, rider=exchange_between_chips(reduced["mid"][3]) if i == 0 else None)
        if i == 0:
            from_chips["mid"] = got

    grads["norm_w"] = jnp.stack([jnp.stack(r) for r in dnorm])
    for n in SMALL:
        if isinstance(grads[n], list):
            grads[n] = jnp.stack(grads[n])

    reduced["last"] = reduce_in_chip(gb["last"])
    from_chips["last"] = exchange_between_chips(reduced["last"][3]).run("exchange_between_chips")
    results = [{}, {}, {}, {}]
    outs = {}
    for which in ("last", "mid", "early"):
        names, bufs, from_sibling, _ = reduced[which]
        for n, g, fs, fc in zip(names, bufs, from_sibling, from_chips[which]):
            parts = [((0, 0), g), ((0,), fs), ((0,), fc), ((1,), fc), ((2,), fc)]
            outs[n] = adamw(parts, as3d(w_loc[n]), as3d(m_loc[n]), as3d(v_loc[n]), name="adamw_" + n + "_" + which,
                            a0=slabs(n, which)[0], prev=outs.get(n))
    for n in BIG:
        for k in range(4):
            results[k][n] = outs[n][k].reshape(_local_shape(n))

    g_small = _pack([grads[n].reshape(-1) for n in SMALL], 8)
    g_small = sum_over_devices(all_gather([g_small]).run("all_gather_small_grads")[0])
    g_small_full = dict(zip(SMALL, _unpack(g_small, [PARAMS[n][0] for n in SMALL])))
    g_small_loc = []
    for n in SMALL:
        if PARAMS[n][1] is None:
            g_small_loc.append(g_small_full[n])
        else:
            g_small_loc.append(lax.dynamic_index_in_dim(_by_destination(g_small_full[n], n), my_dev, axis=0, keepdims=False))
    small_shapes = [_local_shape(n) for n in SMALL]
    pack_small = lambda d: _pack([d[n].reshape(-1) for n in SMALL], 8)[None]
    small_out = adamw([_pack([gl.reshape(-1) for gl in g_small_loc], 8)[None]], pack_small(w_loc), pack_small(m_loc), pack_small(v_loc), name="adamw_small")
    for k in range(4):
        results[k].update(zip(SMALL, _unpack(small_out[k], small_shapes)))
    return (loss, dx[None], *[results[0][n] for n in NAMES], *[results[1][n] for n in NAMES],
            *[results[2][n] for n in NAMES], *[results[3][n] for n in NAMES])
```

```python
import functools

import jax
import jax.numpy as jnp
from jax import lax
from jax.experimental import pallas as pl
from jax.experimental.pallas import tpu as pltpu

f32 = jnp.float32
bf16 = jnp.bfloat16

D_MODEL = 1024
D_FF = 2816
DEPTH = 4
SSD_INNER = 2048
SSD_HEADS = 32
SSD_HEAD_DIM = 64
SSD_GROUPS = 4
SSD_STATE = 128
SSD_CONV_K = 4
SSD_CONV_DIM = 3072
SSD_IN_DIM = 5152
SSD_CHUNK = 128
SC_CONV_K = 3
RMS_EPS = 1e-5
N_DEV = 8
LANES = 128
HALO = 16
PACK_W = 1024
VMEM_LIMIT = 56 * 1024 * 1024
NEG_BIG = -1e30

ADAM_LR = 0.001
ADAM_B1 = 0.9
ADAM_B2 = 0.999
ADAM_EPS = 1e-08
ADAM_WD = 0.01
ADAM_STEP = 10

NT_DIMS = (((1,), (1,)), ((), ()))
TN_DIMS = (((0,), (0,)), ((), ()))
MESH = pl.DeviceIdType.MESH


def _params(sem=None):
    return pltpu.CompilerParams(dimension_semantics=sem, vmem_limit_bytes=VMEM_LIMIT)


def _resident(shape):
    nd = len(shape)
    return pl.BlockSpec(tuple(shape), lambda *_: (0,) * nd, pipeline_mode=pl.Buffered(1))


def _rows(tm, width):
    return pl.BlockSpec((tm, width), lambda i: (i, 0))


def _my_core_and_chip():
    return lax.axis_index("c"), 2 * lax.axis_index("x") + lax.axis_index("y")


def _sigmoid(v):
    return 0.5 * jnp.tanh(0.5 * v) + 0.5


def _softplus(v):
    return jnp.maximum(v, 0.0) + jnp.log(1.0 + jnp.exp(-jnp.abs(v)))


def _rms_fwd(xv, w):
    inv = lax.rsqrt(jnp.mean(xv * xv, axis=-1, keepdims=True) + RMS_EPS)
    xh = xv * inv
    return xh * w, xh, inv


def _rms_bwd(dh, xh, inv, w):
    dxh = dh * w
    dx = inv * (dxh - xh * jnp.mean(dxh * xh, axis=-1, keepdims=True))
    return dx, jnp.sum(dh * xh, axis=0, keepdims=True)


def _mm(a, b):
    return jnp.dot(a, b, preferred_element_type=f32)


def _mm_nt(a, b):
    return lax.dot_general(a, b, NT_DIMS, preferred_element_type=f32)


def _mm_tn(a, b):
    return lax.dot_general(a, b, TN_DIMS, preferred_element_type=f32)


def _layer_slab(w, idx):
    tail = w.shape[len(idx):]
    return pl.BlockSpec((None,) * len(idx) + tuple(tail), lambda *_: tuple(idx) + (0,) * len(tail), pipeline_mode=pl.Buffered(1))


def ffn_fwd(x, nw, wg, wu, wd, idx, tm=512, rider=None):
    T = x.shape[0]
    nt = T // tm
    r_in, r_out, r_shapes, r_scratch, r_args = _rider_specs(rider)

    def body(x_ref, nw_ref, wg_ref, wu_ref, wd_ref, xo_ref, g_ref, u_ref, a_ref):
        xv = x_ref[...]
        h, _, _ = _rms_fwd(xv, nw_ref[...])
        hb = h.astype(bf16)
        g = _mm(hb, wg_ref[...])
        u = _mm(hb, wu_ref[...])
        ab = (g * _sigmoid(g) * u).astype(bf16)
        g_ref[...] = g.astype(bf16)
        u_ref[...] = u.astype(bf16)
        a_ref[...] = ab
        xo_ref[...] = xv + 0.5 * _mm(ab, wd_ref[...])

    hosted = _carry(body, 5, 4, rider, lambda: pl.program_id(0) == 0, lambda: pl.program_id(0) == nt - 1,
                    late=lambda: pl.program_id(0) == (7 * nt) // 8)
    return pl.pallas_call(
        hosted, name="ffn_fwd" if rider is None else "ffn_fwd_carrying", grid=(nt,),
        in_specs=[_rows(tm, D_MODEL), _resident((1, D_MODEL)), _layer_slab(wg, idx), _layer_slab(wu, idx), _layer_slab(wd, idx)] + r_in,
        out_specs=[_rows(tm, D_MODEL), _rows(tm, D_FF), _rows(tm, D_FF), _rows(tm, D_FF)] + r_out,
        out_shape=[jax.ShapeDtypeStruct((T, D_MODEL), f32)] + [jax.ShapeDtypeStruct((T, D_FF), bf16)] * 3 + r_shapes,
        scratch_shapes=r_scratch,
        compiler_params=_params(("parallel",) if rider is None else ("arbitrary",)),
    )(x, nw, wg, wu, wd, *r_args)


def ffn_bwd_dx(x, dxo, g, u, nw, wg, wu, wd, idx, tm=256, rider=None):
    T = x.shape[0]
    nt = T // tm
    r_in, r_out, r_shapes, r_scratch, r_args = _rider_specs(rider)

    def body(x_ref, dxo_ref, g_ref, u_ref, nw_ref, wg_ref, wu_ref, wd_ref, dx_ref, h_ref, dy_ref, dg_ref, du_ref, dnw_ref):
        w = nw_ref[...]
        h, xh, inv = _rms_fwd(x_ref[...], w)
        dxo_v = dxo_ref[...]
        dyb = (0.5 * dxo_v).astype(bf16)
        da = _mm_nt(dyb, wd_ref[...])
        gv = g_ref[...].astype(f32)
        uv = u_ref[...].astype(f32)
        s = _sigmoid(gv)
        dgb = (da * uv * (s * (1.0 + gv * (1.0 - s)))).astype(bf16)
        dub = (da * (gv * s)).astype(bf16)
        dg_ref[...] = dgb
        du_ref[...] = dub
        dh = _mm_nt(dgb, wg_ref[...]) + _mm_nt(dub, wu_ref[...])
        dxn, dw = _rms_bwd(dh, xh, inv, w)
        dx_ref[...] = dxo_v + dxn
        h_ref[...] = h.astype(bf16)
        dy_ref[...] = dyb

        @pl.when(pl.program_id(0) == 0)
        def _():
            dnw_ref[...] = jnp.zeros_like(dnw_ref)

        dnw_ref[...] += dw

    hosted = _carry(body, 8, 6, rider, lambda: pl.program_id(0) == 0, lambda: pl.program_id(0) == nt - 1)
    return pl.pallas_call(
        hosted, name="ffn_bwd_dx" if rider is None else "ffn_bwd_dx_carrying", grid=(nt,),
        in_specs=[_rows(tm, D_MODEL), _rows(tm, D_MODEL), _rows(tm, D_FF), _rows(tm, D_FF), _resident((1, D_MODEL)),
                  _layer_slab(wg, idx), _layer_slab(wu, idx), _layer_slab(wd, idx)] + r_in,
        out_specs=[_rows(tm, D_MODEL), _rows(tm, D_MODEL), _rows(tm, D_MODEL), _rows(tm, D_FF), _rows(tm, D_FF),
                   pl.BlockSpec((1, D_MODEL), lambda i: (0, 0))] + r_out,
        out_shape=[jax.ShapeDtypeStruct((T, D_MODEL), f32), jax.ShapeDtypeStruct((T, D_MODEL), bf16), jax.ShapeDtypeStruct((T, D_MODEL), bf16),
                   jax.ShapeDtypeStruct((T, D_FF), bf16), jax.ShapeDtypeStruct((T, D_FF), bf16), jax.ShapeDtypeStruct((1, D_MODEL), f32)] + r_shapes,
        scratch_shapes=r_scratch,
        compiler_params=_params(("arbitrary",)),
    )(x, dxo, g, u, nw, wg, wu, wd, *r_args)


def tn_matmul(a, b, tk=1024):
    T, M = a.shape
    N = b.shape[1]
    bn = N if M * N <= 3_200_000 else N // 2
    nk = T // tk

    def body(a_ref, b_ref, o_ref):
        @pl.when(pl.program_id(1) == 0)
        def _():
            o_ref[...] = jnp.zeros_like(o_ref)

        o_ref[...] += _mm_tn(a_ref[...], b_ref[...])

    return pl.pallas_call(
        body, name=f"tn_matmul_{M}x{N}", grid=(N // bn, nk),
        in_specs=[pl.BlockSpec((tk, M), lambda j, k: (k, 0)), pl.BlockSpec((tk, bn), lambda j, k: (k, j))],
        out_specs=pl.BlockSpec((M, bn), lambda j, k: (0, j)),
        out_shape=jax.ShapeDtypeStruct((M, N), f32),
        compiler_params=_params(("parallel", "arbitrary")),
    )(a, b)


def tn_matmul_to_shards(a, b, buf, idx, axis):
    T, M = a.shape
    N = b.shape[1]
    m, n = buf.shape[-2:]
    (slab,) = idx
    tk = 1024
    nk = T // tk
    fresh = isinstance(buf, jax.ShapeDtypeStruct)

    def body(a_ref, b_ref, *rest):
        o_ref, acc_ref, stage_ref, sem = rest[-4:]
        k = pl.program_id(0)

        @pl.when(k == 0)
        def _():
            acc_ref[...] = jnp.zeros_like(acc_ref)

        acc_ref[...] += _mm_tn(a_ref[...], b_ref[...])

        @pl.when(k == nk - 1)
        def _():
            my_c, my_chip = _my_core_and_chip()
            for d in range(N_DEV):
                piece = acc_ref[:, pl.ds(d * n, n)] if axis == 1 else acc_ref[pl.ds(d * m, m), :]
                stage_ref[(d % 2) ^ my_c, (d // 2) ^ my_chip] = piece
            out = pltpu.make_async_copy(stage_ref, o_ref.at[:, :, slab], sem)
            out.start()
            out.wait()

    return pl.pallas_call(
        body, name=f"tn_matmul_to_shards_{M}x{N}_{axis}", grid=(nk,),
        in_specs=[pl.BlockSpec((tk, M), lambda k: (k, 0)), pl.BlockSpec((tk, N), lambda k: (k, 0))] + ([] if fresh else [_ANY]),
        out_specs=_ANY,
        out_shape=jax.ShapeDtypeStruct(buf.shape, f32),
        scratch_shapes=[pltpu.VMEM((M, N), f32), pltpu.VMEM((2, 4, m, n), f32), pltpu.SemaphoreType.DMA],
        input_output_aliases={} if fresh else {2: 0},
        compiler_params=_params(("arbitrary",)),
    )(a, b, *([] if fresh else [buf]))


def in_proj_fwd(x, nw, ws, out_dtypes, tm=512):
    T = x.shape[0]
    n = len(ws)

    def body(*refs):
        x_ref, nw_ref = refs[:2]
        w_refs = refs[2:2 + n]
        o_refs = refs[2 + n:]
        h, _, _ = _rms_fwd(x_ref[...], nw_ref[...])
        hb = h.astype(bf16)
        for w_ref, o_ref in zip(w_refs, o_refs):
            o_ref[...] = _mm(hb, w_ref[...]).astype(o_ref.dtype)

    return pl.pallas_call(
        body, name="in_proj_fwd_" + "_".join(str(w.shape[1]) for w in ws), grid=(T // tm,),
        in_specs=[_rows(tm, D_MODEL), _resident((1, D_MODEL))] + [_resident(w.shape) for w in ws],
        out_specs=[_rows(tm, w.shape[1]) for w in ws],
        out_shape=[jax.ShapeDtypeStruct((T, w.shape[1]), dt) for w, dt in zip(ws, out_dtypes)],
        compiler_params=_params(("parallel",)),
    )(x, nw, *ws)


def in_proj_bwd(x, nw, dxo, dys, ws, tm=512):
    T = x.shape[0]
    n = len(ws)

    def body(*refs):
        x_ref, nw_ref, dxo_ref = refs[:3]
        dy_refs = refs[3:3 + n]
        w_refs = refs[3 + n:3 + 2 * n]
        dx_ref, h_ref, dnw_ref = refs[3 + 2 * n:]
        w = nw_ref[...]
        h, xh, inv = _rms_fwd(x_ref[...], w)
        dh = _mm_nt(dy_refs[0][...], w_refs[0][...])
        for dy_ref, w_ref in zip(dy_refs[1:], w_refs[1:]):
            dh = dh + _mm_nt(dy_ref[...], w_ref[...])
        dxn, dw = _rms_bwd(dh, xh, inv, w)
        dx_ref[...] = dxo_ref[...] + dxn
        h_ref[...] = h.astype(bf16)

        @pl.when(pl.program_id(0) == 0)
        def _():
            dnw_ref[...] = jnp.zeros_like(dnw_ref)

        dnw_ref[...] += dw

    return pl.pallas_call(
        body, name="in_proj_bwd_" + "_".join(str(w.shape[1]) for w in ws), grid=(T // tm,),
        in_specs=[_rows(tm, D_MODEL), _resident((1, D_MODEL)), _rows(tm, D_MODEL)] + [_rows(tm, w.shape[1]) for w in ws]
        + [_resident(w.shape) for w in ws],
        out_specs=[_rows(tm, D_MODEL), _rows(tm, D_MODEL), pl.BlockSpec((1, D_MODEL), lambda i: (0, 0))],
        out_shape=[jax.ShapeDtypeStruct((T, D_MODEL), f32), jax.ShapeDtypeStruct((T, D_MODEL), bf16), jax.ShapeDtypeStruct((1, D_MODEL), f32)],
        compiler_params=_params(("arbitrary",)),
    )(x, nw, dxo, *dys, *ws)


def out_proj_fwd(x, a, w, tm=1024):
    T = x.shape[0]
    K = a.shape[1]

    def body(x_ref, a_ref, w_ref, o_ref):
        o_ref[...] = x_ref[...] + _mm(a_ref[...], w_ref[...])

    return pl.pallas_call(
        body, name=f"out_proj_fwd_{K}", grid=(T // tm,),
        in_specs=[_rows(tm, D_MODEL), _rows(tm, K), _resident(w.shape)],
        out_specs=_rows(tm, D_MODEL), out_shape=jax.ShapeDtypeStruct((T, D_MODEL), f32),
        compiler_params=_params(("parallel",)),
    )(x, a, w)


def _halo_spec(tm, width, n_tiles, reverse):
    per = tm // HALO

    def idx(i):
        t = (n_tiles - 1 - i) if reverse else i
        return (jnp.maximum(t * per - 1, 0), 0)

    return pl.BlockSpec((HALO, width), idx)


def _tile_spec(tm, width, n_tiles, reverse):
    if reverse:
        return pl.BlockSpec((tm, width), lambda i: (n_tiles - 1 - i, 0))
    return _rows(tm, width)


ROW_BLOCK = 64


def _strip(s):
    return pl.ds(pl.multiple_of(s * LANES, LANES), LANES)


def _conv_rows(ext_ref, w_ref, cols, k_w, r0):
    base = HALO - (k_w - 1) + r0
    wins = [ext_ref[pl.ds(base + k, ROW_BLOCK), :] for k in range(k_w)]
    out = w_ref[pl.ds(0, 1), cols] * wins[0]
    for k in range(1, k_w):
        out = out + w_ref[pl.ds(k, 1), cols] * wins[k]
    return out, wins


def _shifted_back(d_ref, w_ref, cols, k_w, r0):
    out = w_ref[pl.ds(0, 1), cols] * d_ref[pl.ds(r0 + k_w - 1, ROW_BLOCK), :]
    for k in range(1, k_w):
        out = out + w_ref[pl.ds(k, 1), cols] * d_ref[pl.ds(r0 + k_w - 1 - k, ROW_BLOCK), :]
    return out


def ssd_conv_fwd(xbc, conv_w, conv_b, dt_raw, dt_bias, tm=512):
    T = xbc.shape[0]
    nt = T // tm
    K = SSD_CONV_K

    def body(x_ref, halo_ref, w_ref, b_ref, dtr_ref, dtb_ref, act_ref, dt_ref, ext_ref):
        first = pl.program_id(0) == 0

        def strip(s, carry):
            cols = _strip(s)
            ext_ref[pl.ds(0, HALO), :] = jnp.where(first, 0.0, halo_ref[:, cols].astype(f32))
            ext_ref[pl.ds(HALO, tm), :] = x_ref[:, cols].astype(f32)
            for r0 in range(0, tm, ROW_BLOCK):
                pre, _ = _conv_rows(ext_ref, w_ref, cols, K, r0)
                pre = pre + b_ref[:, cols]
                act_ref[pl.ds(r0, ROW_BLOCK), cols] = (pre * _sigmoid(pre)).astype(bf16)
            return carry

        lax.fori_loop(0, SSD_CONV_DIM // LANES, strip, 0)
        dt = _softplus(dtr_ref[...] + dtb_ref[...])
        lane = lax.broadcasted_iota(jnp.int32, (1, LANES), 1)
        for g in range(SSD_GROUPS):
            dt_ref[g] = jnp.where(lane < 8, dt if g == 0 else pltpu.roll(dt, LANES - 8 * g, axis=1), 0.0)

    return pl.pallas_call(
        body, name="ssd_conv_fwd", grid=(nt,),
        in_specs=[_rows(tm, SSD_CONV_DIM), _halo_spec(tm, SSD_CONV_DIM, nt, False), _resident(conv_w.shape), _resident(conv_b.shape),
                  _rows(tm, LANES), _resident(dt_bias.shape)],
        out_specs=[_rows(tm, SSD_CONV_DIM), pl.BlockSpec((SSD_GROUPS, tm, LANES), lambda i: (0, i, 0))],
        out_shape=[jax.ShapeDtypeStruct((T, SSD_CONV_DIM), bf16), jax.ShapeDtypeStruct((SSD_GROUPS, T, LANES), f32)],
        scratch_shapes=[pltpu.VMEM((tm + HALO, LANES), f32)],
        compiler_params=_params(("parallel",)),
    )(xbc, xbc, conv_w, conv_b, dt_raw, dt_bias)


def ssd_conv_bwd(xbc, conv_w, conv_b, dt_raw, dt_bias, dxs_a, dxs_b, db, dc, ddt, tm=512):
    T = xbc.shape[0]
    nt = T // tm
    K = SSD_CONV_K

    def body(x_ref, halo_ref, w_ref, b_ref, dtr_ref, dtb_ref, da_ref, dbb_ref, db_ref, dc_ref, ddt_ref,
             dx_ref, ddtr_ref, dw_ref, dbias_ref, ddtb_ref, ext_ref, dpre_ref, carry_ref):
        i = pl.program_id(0)

        @pl.when(i == 0)
        def _():
            carry_ref[...] = jnp.zeros_like(carry_ref)
            dw_ref[...] = jnp.zeros_like(dw_ref)
            dbias_ref[...] = jnp.zeros_like(dbias_ref)
            ddtb_ref[...] = jnp.zeros_like(ddtb_ref)

        first_tile = i == nt - 1

        def run_strips(lo, hi, load_dact):
            def strip(s, carry):
                cols = _strip(s)
                ext_ref[pl.ds(0, HALO), :] = jnp.where(first_tile, 0.0, halo_ref[:, cols].astype(f32))
                ext_ref[pl.ds(HALO, tm), :] = x_ref[:, cols].astype(f32)
                dpre_ref[pl.ds(tm, 8), :] = carry_ref[:, cols]
                bias = b_ref[:, cols]
                dws = [jnp.zeros((1, LANES), f32) for _ in range(K)]
                dbs = jnp.zeros((1, LANES), f32)
                for r0 in range(0, tm, ROW_BLOCK):
                    pre, wins = _conv_rows(ext_ref, w_ref, cols, K, r0)
                    pre = pre + bias
                    sg = _sigmoid(pre)
                    dpre = load_dact(s, r0) * (sg * (1.0 + pre * (1.0 - sg)))
                    dpre_ref[pl.ds(r0, ROW_BLOCK), :] = dpre
                    dbs = dbs + jnp.sum(dpre, axis=0, keepdims=True)
                    for k in range(K):
                        dws[k] = dws[k] + jnp.sum(dpre * wins[k], axis=0, keepdims=True)
                carry_ref[:, cols] = dpre_ref[pl.ds(0, 8), :]
                for r0 in range(0, tm, ROW_BLOCK):
                    dx_ref[pl.ds(r0, ROW_BLOCK), cols] = _shifted_back(dpre_ref, w_ref, cols, K, r0).astype(bf16)
                for k in range(K):
                    dw_ref[pl.ds(k, 1), cols] += dws[k]
                dbias_ref[:, cols] += dbs
                return carry

            lax.fori_loop(lo, hi, strip, 0)

        rows = lambda r0: pl.ds(r0, ROW_BLOCK)
        n_x = SSD_INNER // LANES
        n_g = SSD_GROUPS * SSD_STATE // LANES
        run_strips(0, n_x, lambda s, r0: da_ref[rows(r0), _strip(s)].astype(f32) + dbb_ref[rows(r0), _strip(s)].astype(f32))
        run_strips(n_x, n_x + n_g, lambda s, r0: db_ref[rows(r0), _strip(s - n_x)].astype(f32))
        run_strips(n_x + n_g, n_x + 2 * n_g, lambda s, r0: dc_ref[rows(r0), _strip(s - n_x - n_g)].astype(f32))
        lane = lax.broadcasted_iota(jnp.int32, (1, LANES), 1)
        ddt = jnp.where(lane < 8, ddt_ref[0], 0.0)
        for g in range(1, SSD_GROUPS):
            ddt = ddt + pltpu.roll(jnp.where(lane < 8, ddt_ref[g], 0.0), 8 * g, axis=1)
        ddtr = ddt * _sigmoid(dtr_ref[...] + dtb_ref[...])
        ddtr_ref[...] = ddtr.astype(bf16)
        ddtb_ref[...] += jnp.sum(ddtr, axis=0, keepdims=True)

    rev = functools.partial(_tile_spec, tm, n_tiles=nt, reverse=True)
    const = lambda shape: pl.BlockSpec(shape, lambda i: (0, 0))
    return pl.pallas_call(
        body, name="ssd_conv_bwd", grid=(nt,),
        in_specs=[rev(width=SSD_CONV_DIM), _halo_spec(tm, SSD_CONV_DIM, nt, True), _resident(conv_w.shape), _resident(conv_b.shape),
                  rev(width=LANES), _resident(dt_bias.shape), rev(width=SSD_INNER), rev(width=SSD_INNER),
                  rev(width=SSD_GROUPS * SSD_STATE), rev(width=SSD_GROUPS * SSD_STATE),
                  pl.BlockSpec((SSD_GROUPS, tm, LANES), lambda i: (0, nt - 1 - i, 0))],
        out_specs=[rev(width=SSD_CONV_DIM), rev(width=LANES), const((8, SSD_CONV_DIM)), const((1, SSD_CONV_DIM)), const((1, LANES))],
        out_shape=[jax.ShapeDtypeStruct((T, SSD_CONV_DIM), bf16), jax.ShapeDtypeStruct((T, LANES), bf16),
                   jax.ShapeDtypeStruct((8, SSD_CONV_DIM), f32), jax.ShapeDtypeStruct((1, SSD_CONV_DIM), f32), jax.ShapeDtypeStruct((1, LANES), f32)],
        scratch_shapes=[pltpu.VMEM((tm + HALO, LANES), f32), pltpu.VMEM((tm + 8, LANES), f32), pltpu.VMEM((8, SSD_CONV_DIM), f32)],
        compiler_params=_params(("arbitrary",)),
    )(xbc, xbc, conv_w, conv_b, dt_raw, dt_bias, dxs_a, dxs_b, db, dc, ddt)


def _ssd_chunk(xs, bm, cm, dt, alog, st):
    L = SSD_CHUNK
    row = lax.broadcasted_iota(jnp.int32, (L, L), 0)
    col = lax.broadcasted_iota(jnp.int32, (L, L), 1)
    causal = row >= col
    tril = jnp.where(causal, 1.0, 0.0).astype(f32)
    lane = lax.broadcasted_iota(jnp.int32, (1, LANES), 1)
    sub = lax.broadcasted_iota(jnp.int32, (LANES, 1), 0)
    lo = lane < SSD_HEAD_DIM
    last_row = sub == L - 1

    dta = dt * (-jnp.exp(alog))
    a_cs = jnp.dot(tril, dta, precision=lax.Precision.HIGHEST, preferred_element_type=f32)
    a_cs_t = a_cs.T
    bmb = bm.astype(bf16)
    cmb = cm.astype(bf16)
    cb = _mm_nt(cmb, bmb)
    c_st = _mm(cmb, st.astype(bf16))

    def head_col(v, e):
        return jnp.sum(jnp.where(lane == e, v, 0.0), axis=1, keepdims=True)

    def head_row(v, e):
        return jnp.sum(jnp.where(sub == e, v, 0.0), axis=0, keepdims=True)

    ys, sts = [], []
    for j in range(4):
        e0, e1 = 2 * j, 2 * j + 1
        c0, c1 = head_col(a_cs, e0), head_col(a_cs, e1)
        acs_x = jnp.where(lo, c0, c1)
        dt_x = jnp.where(lo, head_col(dt, e0), head_col(dt, e1))
        xd = xs[:, j * LANES:(j + 1) * LANES] * dt_x
        m0 = cb * jnp.exp(jnp.where(causal, c0 - head_row(a_cs_t, e0), NEG_BIG))
        m1 = cb * jnp.exp(jnp.where(causal, c1 - head_row(a_cs_t, e1), NEG_BIG))
        mcat = jnp.concatenate([m0, m1], axis=1).astype(bf16)
        xcat = jnp.concatenate([jnp.where(lo, xd, 0.0), jnp.where(lo, 0.0, xd)], axis=0).astype(bf16)
        y_diag = _mm(mcat, xcat)
        a_last = jnp.sum(jnp.where(last_row, acs_x, 0.0), axis=0, keepdims=True)
        x_dec = (xd * jnp.exp(a_last - acs_x)).astype(bf16)
        s_new = _mm_tn(bmb, x_dec)
        y_off = c_st[:, j * LANES:(j + 1) * LANES] * jnp.exp(acs_x)
        ys.append(y_diag + y_off)
        sts.append(jnp.exp(a_last) * st[:, j * LANES:(j + 1) * LANES] + s_new)
    return jnp.concatenate(ys, axis=1), jnp.concatenate(sts, axis=1)


SCAN_GROUPS_FWD = 4
SCAN_GROUPS_BWD = 1


def _scan_specs(nc, reverse, gs):
    L = SSD_CHUNK
    ch = (lambda c: nc - 1 - c) if reverse else (lambda c: c)
    gw = SSD_INNER // SSD_GROUPS
    b0 = SSD_INNER // (gs * SSD_STATE)
    c0 = (SSD_INNER + SSD_GROUPS * SSD_STATE) // (gs * SSD_STATE)
    xs = pl.BlockSpec((L, gs * gw), lambda g, c: (ch(c), g))
    bm = pl.BlockSpec((L, gs * SSD_STATE), lambda g, c: (ch(c), b0 + g))
    cm = pl.BlockSpec((L, gs * SSD_STATE), lambda g, c: (ch(c), c0 + g))
    dt = pl.BlockSpec((gs, L, LANES), lambda g, c: (g, ch(c), 0))
    alog = pl.BlockSpec((gs, 1, LANES), lambda g, c: (g, 0, 0))
    st = pl.BlockSpec((gs, None, SSD_STATE, gw), lambda g, c: (g, ch(c), 0, 0))
    y = pl.BlockSpec((L, gs * gw), lambda g, c: (ch(c), g))
    grp = pl.BlockSpec((L, gs * SSD_STATE), lambda g, c: (ch(c), g))
    return xs, bm, cm, dt, alog, st, y, grp


def ssd_scan_fwd(act, dt4, alog4, rider=None):
    T = act.shape[0]
    nc = T // SSD_CHUNK
    gs = SCAN_GROUPS_FWD
    ng = SSD_GROUPS // gs
    gw = SSD_INNER // SSD_GROUPS
    xs_s, bm_s, cm_s, dt_s, alog_s, st_s, y_s, _ = _scan_specs(nc, False, gs)
    r_in, r_out, r_shapes, r_scratch, r_args = _rider_specs(rider)

    def body(xs_ref, bm_ref, cm_ref, dt_ref, alog_ref, y_ref, st_ref, st_scr):
        @pl.when(pl.program_id(1) == 0)
        def _():
            st_scr[...] = jnp.zeros_like(st_scr)

        for q in range(gs):
            xc, gc = pl.ds(q * gw, gw), pl.ds(q * SSD_STATE, SSD_STATE)
            st = st_scr[q]
            st_ref[q] = st
            y, st_new = _ssd_chunk(xs_ref[:, xc].astype(f32), bm_ref[:, gc].astype(f32), cm_ref[:, gc].astype(f32), dt_ref[q], alog_ref[q], st)
            y_ref[:, xc] = y.astype(bf16)
            st_scr[q] = st_new

    first = lambda: jnp.logical_and(pl.program_id(0) == 0, pl.program_id(1) == 0)
    last = lambda: jnp.logical_and(pl.program_id(0) == ng - 1, pl.program_id(1) == nc - 1)
    late = lambda: jnp.logical_and(pl.program_id(0) == ng - 1, pl.program_id(1) == (7 * nc) // 8)
    return pl.pallas_call(
        _carry(body, 5, 2, rider, first, last, late), name="ssd_scan_fwd" if rider is None else "ssd_scan_fwd_carrying", grid=(ng, nc),
        in_specs=[xs_s, bm_s, cm_s, dt_s, alog_s] + r_in, out_specs=[y_s, st_s] + r_out,
        out_shape=[jax.ShapeDtypeStruct((T, SSD_INNER), bf16), jax.ShapeDtypeStruct((SSD_GROUPS, nc, SSD_STATE, gw), f32)] + r_shapes,
        scratch_shapes=[pltpu.VMEM((gs, SSD_STATE, gw), f32)] + r_scratch,
        compiler_params=_params(("parallel" if rider is None else "arbitrary", "arbitrary")),
    )(act, act, act, dt4, alog4, *r_args)


def ssd_scan_bwd(act, dt4, alog4, states, dy, rider=None):
    T = act.shape[0]
    nc = T // SSD_CHUNK
    gs = SCAN_GROUPS_BWD
    ng = SSD_GROUPS // gs
    gw = SSD_INNER // SSD_GROUPS
    xs_s, bm_s, cm_s, dt_s, alog_s, st_s, y_s, grp_s = _scan_specs(nc, True, gs)
    r_in, r_out, r_shapes, r_scratch, r_args = _rider_specs(rider)

    def body(xs_ref, bm_ref, cm_ref, dt_ref, alog_ref, st_ref, dy_ref, dxs_ref, db_ref, dc_ref, ddt_ref, dalog_ref, dst_scr):
        @pl.when(pl.program_id(1) == 0)
        def _():
            dst_scr[...] = jnp.zeros_like(dst_scr)
            dalog_ref[...] = jnp.zeros_like(dalog_ref)

        for q in range(gs):
            xc, gc = pl.ds(q * gw, gw), pl.ds(q * SSD_STATE, SSD_STATE)
            _, vjp = jax.vjp(_ssd_chunk, xs_ref[:, xc].astype(f32), bm_ref[:, gc].astype(f32), cm_ref[:, gc].astype(f32),
                             dt_ref[q], alog_ref[q], st_ref[q])
            dxs, dbm, dcm, ddt, dalog, dst = vjp((dy_ref[:, xc].astype(f32), dst_scr[q]))
            dxs_ref[:, xc] = dxs.astype(bf16)
            db_ref[:, gc] = dbm.astype(bf16)
            dc_ref[:, gc] = dcm.astype(bf16)
            ddt_ref[q] = ddt
            dalog_ref[q] += dalog
            dst_scr[q] = dst

    first = lambda: jnp.logical_and(pl.program_id(0) == 0, pl.program_id(1) == 0)
    last = lambda: jnp.logical_and(pl.program_id(0) == ng - 1, pl.program_id(1) == nc - 1)
    return pl.pallas_call(
        _carry(body, 7, 5, rider, first, last), name="ssd_scan_bwd" if rider is None else "ssd_scan_bwd_carrying", grid=(ng, nc),
        in_specs=[xs_s, bm_s, cm_s, dt_s, alog_s, st_s, y_s] + r_in,
        out_specs=[y_s, grp_s, grp_s, dt_s, alog_s] + r_out,
        out_shape=[jax.ShapeDtypeStruct((T, SSD_INNER), bf16), jax.ShapeDtypeStruct((T, SSD_GROUPS * SSD_STATE), bf16),
                   jax.ShapeDtypeStruct((T, SSD_GROUPS * SSD_STATE), bf16), jax.ShapeDtypeStruct((SSD_GROUPS, T, LANES), f32),
                   jax.ShapeDtypeStruct((SSD_GROUPS, 1, LANES), f32)] + r_shapes,
        scratch_shapes=[pltpu.VMEM((gs, SSD_STATE, gw), f32)] + r_scratch,
        compiler_params=_params(("parallel" if rider is None else "arbitrary", "arbitrary")),
    )(act, act, act, dt4, alog4, states, dy, *r_args)


GATE_ROWS = 256


def _ssd_gate(y, xs, z, d_x, nw):
    g = (y + xs * d_x) * (z * _sigmoid(z))
    return g * lax.rsqrt(jnp.mean(g * g, axis=-1, keepdims=True) + RMS_EPS) * nw


def _gate_blocks(tm, fn):
    gw = SSD_INNER // SSD_GROUPS

    def block(r, carry):
        rows = pl.ds(r * GATE_ROWS if isinstance(r, int) else pl.multiple_of(r * GATE_ROWS, GATE_ROWS), GATE_ROWS)
        for k in range(SSD_GROUPS):
            fn(rows, pl.ds(k * gw, gw))
        return carry

    if tm == GATE_ROWS:
        block(0, 0)
    else:
        lax.fori_loop(0, tm // GATE_ROWS, block, 0)


def ssd_gate_fwd(y, act, z, d_x, nw, tm=512):
    T = y.shape[0]

    def body(y_ref, xs_ref, z_ref, d_ref, nw_ref, o_ref):
        def one(rows, cols):
            o_ref[rows, cols] = _ssd_gate(y_ref[rows, cols].astype(f32), xs_ref[rows, cols].astype(f32), z_ref[rows, cols].astype(f32),
                                          d_ref[:, cols], nw_ref[:, cols]).astype(bf16)

        _gate_blocks(tm, one)

    return pl.pallas_call(
        body, name="ssd_gate_fwd", grid=(T // tm,),
        in_specs=[_rows(tm, SSD_INNER), _rows(tm, SSD_INNER), _rows(tm, SSD_INNER), _resident(d_x.shape), _resident(nw.shape)],
        out_specs=_rows(tm, SSD_INNER), out_shape=jax.ShapeDtypeStruct((T, SSD_INNER), bf16),
        compiler_params=_params(("parallel",)),
    )(y, act, z, d_x, nw)


def ssd_gate_bwd(y, act, z, d_x, nw, dxo, w_out, tm=512):
    T = y.shape[0]

    def body(y_ref, xs_ref, z_ref, d_ref, nw_ref, dxo_ref, w_ref, dy_ref, dxs_ref, dz_ref, dd_ref, dnw_ref, dyb_ref):
        @pl.when(pl.program_id(0) == 0)
        def _():
            dd_ref[...] = jnp.zeros_like(dd_ref)
            dnw_ref[...] = jnp.zeros_like(dnw_ref)

        dyb_ref[...] = dxo_ref[...].astype(bf16)

        def one(rows, cols):
            _, vjp = jax.vjp(_ssd_gate, y_ref[rows, cols].astype(f32), xs_ref[rows, cols].astype(f32), z_ref[rows, cols].astype(f32),
                             d_ref[:, cols], nw_ref[:, cols])
            dy, dxs, dz, dd, dnw = vjp(_mm_nt(dyb_ref[rows, :], w_ref[cols, :]))
            dy_ref[rows, cols] = dy.astype(bf16)
            dxs_ref[rows, cols] = dxs.astype(bf16)
            dz_ref[rows, cols] = dz.astype(bf16)
            dd_ref[:, cols] += dd
            dnw_ref[:, cols] += dnw

        _gate_blocks(tm, one)

    const = pl.BlockSpec((1, SSD_INNER), lambda i: (0, 0))
    return pl.pallas_call(
        body, name="ssd_gate_bwd", grid=(T // tm,),
        in_specs=[_rows(tm, SSD_INNER), _rows(tm, SSD_INNER), _rows(tm, SSD_INNER), _resident(d_x.shape), _resident(nw.shape),
                  _rows(tm, D_MODEL), _resident(w_out.shape)],
        out_specs=[_rows(tm, SSD_INNER)] * 3 + [const, const, _rows(tm, D_MODEL)],
        out_shape=[jax.ShapeDtypeStruct((T, SSD_INNER), bf16)] * 3 + [jax.ShapeDtypeStruct((1, SSD_INNER), f32)] * 2
        + [jax.ShapeDtypeStruct((T, D_MODEL), bf16)],
        compiler_params=_params(("arbitrary",)),
    )(y, act, z, d_x, nw, dxo, w_out)


def sc_mid_fwd(bcu, conv_w, tm=512):
    T = bcu.shape[0]
    nt = T // tm
    Dm = D_MODEL

    def body(x_ref, halo_ref, w_ref, q_ref, ext_ref):
        first = pl.program_id(0) == 0
        n_s = Dm // LANES

        def strip(s, carry):
            cols, c_cols, u_cols = _strip(s), _strip(s + n_s), _strip(s + 2 * n_s)
            ext_ref[pl.ds(0, HALO), :] = jnp.where(first, 0.0, halo_ref[:, c_cols].astype(f32) * halo_ref[:, u_cols].astype(f32))
            ext_ref[pl.ds(HALO, tm), :] = x_ref[:, c_cols].astype(f32) * x_ref[:, u_cols].astype(f32)
            for r0 in range(0, tm, ROW_BLOCK):
                rows = pl.ds(r0, ROW_BLOCK)
                v, _ = _conv_rows(ext_ref, w_ref, cols, SC_CONV_K, r0)
                q_ref[rows, cols] = (x_ref[rows, cols].astype(f32) * v).astype(bf16)
            return carry

        lax.fori_loop(0, n_s, strip, 0)

    return pl.pallas_call(
        body, name="sc_mid_fwd", grid=(nt,),
        in_specs=[_rows(tm, 3 * Dm), _halo_spec(tm, 3 * Dm, nt, False), _resident(conv_w.shape)],
        out_specs=_rows(tm, Dm), out_shape=jax.ShapeDtypeStruct((T, Dm), bf16),
        scratch_shapes=[pltpu.VMEM((tm + HALO, LANES), f32)],
        compiler_params=_params(("parallel",)),
    )(bcu, bcu, conv_w)


def sc_mid_bwd(bcu, conv_w, dxo, w_out, tm=512):
    T = bcu.shape[0]
    nt = T // tm
    Dm = D_MODEL
    K = SC_CONV_K

    def body(x_ref, halo_ref, w_ref, dxo_ref, wo_ref, dx_ref, dw_ref, dyb_ref, ext_ref, dv_ref, carry_ref, dq_ref):
        i = pl.program_id(0)

        @pl.when(i == 0)
        def _():
            carry_ref[...] = jnp.zeros_like(carry_ref)
            dw_ref[...] = jnp.zeros_like(dw_ref)

        dyb = dxo_ref[...].astype(bf16)
        dyb_ref[...] = dyb
        dq_ref[...] = _mm_nt(dyb, wo_ref[...])
        first_tile = i == nt - 1
        n_s = Dm // LANES

        def strip(s, carry):
            cols, c_cols, u_cols = _strip(s), _strip(s + n_s), _strip(s + 2 * n_s)
            ext_ref[pl.ds(0, HALO), :] = jnp.where(first_tile, 0.0, halo_ref[:, c_cols].astype(f32) * halo_ref[:, u_cols].astype(f32))
            ext_ref[pl.ds(HALO, tm), :] = x_ref[:, c_cols].astype(f32) * x_ref[:, u_cols].astype(f32)
            dv_ref[pl.ds(tm, 8), :] = carry_ref[:, cols]
            dws = [jnp.zeros((1, LANES), f32) for _ in range(K)]
            for r0 in range(0, tm, ROW_BLOCK):
                rows = pl.ds(r0, ROW_BLOCK)
                v, wins = _conv_rows(ext_ref, w_ref, cols, K, r0)
                dqv = dq_ref[rows, cols]
                dv = dqv * x_ref[rows, cols].astype(f32)
                dv_ref[rows, :] = dv
                dx_ref[rows, cols] = (dqv * v).astype(bf16)
                for k in range(K):
                    dws[k] = dws[k] + jnp.sum(dv * wins[k], axis=0, keepdims=True)
            carry_ref[:, cols] = dv_ref[pl.ds(0, 8), :]
            for r0 in range(0, tm, ROW_BLOCK):
                rows = pl.ds(r0, ROW_BLOCK)
                dp = _shifted_back(dv_ref, w_ref, cols, K, r0)
                dx_ref[rows, c_cols] = (dp * x_ref[rows, u_cols].astype(f32)).astype(bf16)
                dx_ref[rows, u_cols] = (dp * x_ref[rows, c_cols].astype(f32)).astype(bf16)
            for k in range(K):
                dw_ref[pl.ds(k, 1), cols] += dws[k]
            return carry

        lax.fori_loop(0, n_s, strip, 0)

    return pl.pallas_call(
        body, name="sc_mid_bwd", grid=(nt,),
        in_specs=[_tile_spec(tm, 3 * Dm, nt, True), _halo_spec(tm, 3 * Dm, nt, True), _resident(conv_w.shape), _tile_spec(tm, Dm, nt, True),
                  _resident(w_out.shape)],
        out_specs=[_tile_spec(tm, 3 * Dm, nt, True), pl.BlockSpec((8, Dm), lambda i: (0, 0)), _tile_spec(tm, Dm, nt, True)],
        out_shape=[jax.ShapeDtypeStruct((T, 3 * Dm), bf16), jax.ShapeDtypeStruct((8, Dm), f32), jax.ShapeDtypeStruct((T, Dm), bf16)],
        scratch_shapes=[pltpu.VMEM((tm + HALO, LANES), f32), pltpu.VMEM((tm + 8, LANES), f32), pltpu.VMEM((8, Dm), f32),
                        pltpu.VMEM((tm, Dm), f32)],
        compiler_params=_params(("arbitrary",)),
    )(bcu, bcu, conv_w, dxo, w_out)


def loss_head(x, fw, target, tm=1024):
    T = x.shape[0]

    def body(x_ref, fw_ref, t_ref, loss_ref, dx_ref, dfw_ref):
        @pl.when(pl.program_id(0) == 0)
        def _():
            loss_ref[...] = jnp.zeros_like(loss_ref)
            dfw_ref[...] = jnp.zeros_like(dfw_ref)

        w = fw_ref[...]
        y, xh, inv = _rms_fwd(x_ref[...], w)
        err = y - t_ref[...]
        loss_ref[...] += 0.5 * jnp.sum(jnp.mean(err * err, axis=-1, keepdims=True), axis=0, keepdims=True)
        dx, dw = _rms_bwd(err * (1.0 / D_MODEL), xh, inv, w)
        dx_ref[...] = dx
        dfw_ref[...] += dw

    return pl.pallas_call(
        body, name="loss_head", grid=(T // tm,),
        in_specs=[_rows(tm, D_MODEL), _resident((1, D_MODEL)), _rows(tm, D_MODEL)],
        out_specs=[pl.BlockSpec((1, LANES), lambda i: (0, 0)), _rows(tm, D_MODEL), pl.BlockSpec((1, D_MODEL), lambda i: (0, 0))],
        out_shape=[jax.ShapeDtypeStruct((1, LANES), f32), jax.ShapeDtypeStruct((T, D_MODEL), f32), jax.ShapeDtypeStruct((1, D_MODEL), f32)],
        compiler_params=_params(("arbitrary",)),
    )(x, fw, target)


ELEMENTWISE_TILE_BYTES = 1_600_000


def _row_tile(rows, width):
    row_bytes = 4 * _round_up(width, LANES)
    tile = rows
    while tile * row_bytes > ELEMENTWISE_TILE_BYTES and tile % 16 == 0:
        tile //= 2
    return tile


def adamw(g_parts, w, m, v, name="adamw", a0=0, prev=None):
    A, B, n = w.shape
    tb = _row_tile(B, n)
    n_parts = len(g_parts)
    arrays, specs = [], []
    for part in g_parts:
        lead, arr = part if isinstance(part, tuple) else ((), part)
        specs.append(pl.BlockSpec((None,) * (len(lead) + 1) + (tb, n), lambda a, t, lead=lead: tuple(lead) + (a, t, 0)))
        arrays.append(arr)
    na = arrays[0].shape[-3]
    prev = list(prev) if prev is not None else []

    def body(*refs):
        n = n_parts
        g_refs = refs[:n]
        w_ref, m_ref, v_ref = refs[n:n + 3]
        go_ref, d_ref, mo_ref, vo_ref = refs[n + 3 + len(prev):]
        g = g_refs[0][...].astype(f32)
        for r in g_refs[1:]:
            g = g + r[...].astype(f32)
        m_new = ADAM_B1 * m_ref[...] + (1.0 - ADAM_B1) * g
        v_new = ADAM_B2 * v_ref[...] + (1.0 - ADAM_B2) * (g * g)
        m_hat = m_new / (1.0 - ADAM_B1 ** ADAM_STEP)
        v_hat = v_new / (1.0 - ADAM_B2 ** ADAM_STEP)
        go_ref[...] = g
        d_ref[...] = -ADAM_LR * (m_hat / (jnp.sqrt(v_hat) + ADAM_EPS) + ADAM_WD * w_ref[...])
        mo_ref[...] = m_new
        vo_ref[...] = v_new

    plain = pl.BlockSpec((None, tb, n), lambda a, t: (a + a0, t, 0))
    return pl.pallas_call(
        body, name=name, grid=(na, B // tb), in_specs=specs + [plain] * 3 + [_ANY] * len(prev), out_specs=[plain] * 4,
        out_shape=[jax.ShapeDtypeStruct((A, B, n), f32)] * 4,
        input_output_aliases={n_parts + 3 + k: k for k in range(len(prev))},
        compiler_params=_params(("parallel", "parallel")),
    )(*arrays, w, m, v, *prev)


def pair_sum_bf16(ga, gb, name):
    _, A, B, n = gb.shape
    tb = _row_tile(B, n)

    def body(a_ref, b_ref, o_ref):
        o_ref[...] = (a_ref[...] + b_ref[...]).astype(bf16)

    return pl.pallas_call(
        body, name=name, grid=(3, A, B // tb),
        in_specs=[pl.BlockSpec((None, None, None, tb, n), lambda j, a, t: (0, j + 1, a, t, 0)),
                  pl.BlockSpec((None, None, tb, n), lambda j, a, t: (j + 1, a, t, 0))],
        out_specs=pl.BlockSpec((None, None, tb, n), lambda j, a, t: (j + 1, a, t, 0)),
        out_shape=jax.ShapeDtypeStruct((4, A, B, n), bf16),
        compiler_params=_params(("parallel", "parallel", "parallel")),
    )(ga, gb)


def assemble(gathered, axis, tk=256):
    _, A, K, n = gathered.shape
    if axis == 1:
        def body(w_ref, o_ref):
            o_ref[...] = jnp.concatenate([w_ref[j] for j in range(N_DEV)], axis=1)

        return pl.pallas_call(
            body, name=f"assemble_cols_{K}x{n}", grid=(A, K // tk),
            in_specs=[pl.BlockSpec((N_DEV, None, tk, n), lambda a, t: (0, a, t, 0))],
            out_specs=pl.BlockSpec((None, tk, N_DEV * n), lambda a, t: (a, t, 0)),
            out_shape=jax.ShapeDtypeStruct((A, K, N_DEV * n), gathered.dtype),
            compiler_params=_params(("parallel", "parallel")),
        )(gathered)

    def body(w_ref, o_ref):
        for j in range(N_DEV):
            o_ref[pl.ds(j * K, K), :] = w_ref[j]

    return pl.pallas_call(
        body, name=f"assemble_rows_{K}x{n}", grid=(A,),
        in_specs=[pl.BlockSpec((N_DEV, None, K, n), lambda a: (0, a, 0, 0))],
        out_specs=pl.BlockSpec((None, N_DEV * K, n), lambda a: (a, 0, 0)),
        out_shape=jax.ShapeDtypeStruct((A, N_DEV * K, n), gathered.dtype),
        compiler_params=_params(("parallel",)),
    )(gathered)


SSD_IN_PAD = -(-SSD_IN_DIM // LANES) * LANES


def assemble_ssd_in(gathered, tk=256):
    _, A, K, n = gathered.shape

    def body(w_ref, z_ref, x_ref, dt_ref, full_ref):
        full_ref[:, pl.ds(SSD_IN_PAD - LANES, LANES)] = jnp.zeros((tk, LANES), gathered.dtype)
        for j in range(N_DEV):
            full_ref[:, pl.ds(j * n, n)] = w_ref[j]
        z_ref[...] = full_ref[:, pl.ds(0, SSD_INNER)]
        x_ref[...] = full_ref[:, pl.ds(SSD_INNER, SSD_CONV_DIM)]
        dt_ref[...] = full_ref[:, pl.ds(SSD_INNER + SSD_CONV_DIM, LANES)]

    widths = (SSD_INNER, SSD_CONV_DIM, LANES)
    return pl.pallas_call(
        body, name="assemble_ssd_in", grid=(A, K // tk),
        in_specs=[pl.BlockSpec((N_DEV, None, tk, n), lambda a, t: (0, a, t, 0))],
        out_specs=[pl.BlockSpec((None, tk, w), lambda a, t: (a, t, 0)) for w in widths],
        out_shape=[jax.ShapeDtypeStruct((A, K, w), gathered.dtype) for w in widths],
        scratch_shapes=[pltpu.VMEM((tk, SSD_IN_PAD), gathered.dtype)],
        compiler_params=_params(("parallel", "parallel")),
    )(gathered)


def ssd_in_to_shards(dwz, dwx, dwdt, buf, j, tk=256):
    K = dwz.shape[0]
    n = buf.shape[-1]
    fresh = isinstance(buf, jax.ShapeDtypeStruct)

    def body(z_ref, x_ref, dt_ref, *rest):
        o_ref, full_ref = rest[-2:]
        full_ref[:, pl.ds(0, SSD_INNER)] = z_ref[...]
        full_ref[:, pl.ds(SSD_INNER, SSD_CONV_DIM)] = x_ref[...]
        full_ref[:, pl.ds(SSD_INNER + SSD_CONV_DIM, LANES)] = dt_ref[...]
        my_c, my_chip = _my_core_and_chip()
        for d in range(N_DEV):
            o_ref[(d % 2) ^ my_c, (d // 2) ^ my_chip] = full_ref[:, pl.ds(d * n, n)]

    return pl.pallas_call(
        body, name="ssd_in_to_shards", grid=(K // tk,),
        in_specs=[_rows(tk, SSD_INNER), _rows(tk, SSD_CONV_DIM), _rows(tk, LANES)] + ([] if fresh else [_ANY]),
        out_specs=pl.BlockSpec((2, 4, None, tk, n), lambda t: (0, 0, j, t, 0)),
        out_shape=jax.ShapeDtypeStruct(buf.shape, f32),
        scratch_shapes=[pltpu.VMEM((tk, SSD_IN_PAD), f32)],
        input_output_aliases={} if fresh else {3: 0},
        compiler_params=_params(("parallel",)),
    )(dwz, dwx, dwdt, *([] if fresh else [buf]))


def sum_over_devices(gathered):
    _, R, W = gathered.shape

    def body(g_ref, o_ref):
        acc = g_ref[0]
        for k in range(1, N_DEV):
            acc = acc + g_ref[k]
        o_ref[...] = acc

    return pl.pallas_call(
        body, name="sum_over_devices", grid=(1,),
        in_specs=[pl.BlockSpec((N_DEV, R, W), lambda i: (0, 0, 0))], out_specs=pl.BlockSpec((R, W), lambda i: (0, 0)),
        out_shape=jax.ShapeDtypeStruct((R, W), f32), compiler_params=_params(("arbitrary",)),
    )(gathered)


_ANY = pl.BlockSpec(memory_space=pl.ANY)


class _Exchange:
    def __init__(self, inputs, out_shapes, scratch, start, finish, relay=None):
        self.inputs, self.out_shapes, self.scratch, self.start, self.finish = inputs, out_shapes, scratch, start, finish
        self.relay = relay

    def run(self, name):
        ni, no = len(self.inputs), len(self.out_shapes)

        def body(*refs):
            parts = (refs[:ni], refs[ni:ni + no], refs[ni + no:])
            self.start(*parts)
            if self.relay is not None:
                self.relay(*parts)
            self.finish(*parts)

        return pl.pallas_call(body, name=name, in_specs=[_ANY] * ni, out_specs=[_ANY] * no, out_shape=self.out_shapes,
                              scratch_shapes=self.scratch)(*self.inputs)


def _carry(body, n_in, n_out, rider, first, last, late=None):
    if rider is None:
        return body
    ri, ro = len(rider.inputs), len(rider.out_shapes)

    def hosted(*refs):
        a, b, c = n_in + ri, n_in + ri + n_out, n_in + ri + n_out + ro
        rs = len(refs) - c - len(rider.scratch)
        parts = (refs[n_in:a], refs[b:c], refs[c + rs:])

        @pl.when(first())
        def _():
            rider.start(*parts)

        if rider.relay is not None and late is not None:
            @pl.when(late())
            def _():
                rider.relay(*parts)

        body(*refs[:n_in], *refs[a:b], *refs[c:c + rs])

        @pl.when(last())
        def _():
            if rider.relay is not None and late is None:
                rider.relay(*parts)
            rider.finish(*parts)

    return hosted


def _rider_specs(rider):
    if rider is None:
        return [], [], [], [], []
    return [_ANY] * len(rider.inputs), [_ANY] * len(rider.out_shapes), list(rider.out_shapes), list(rider.scratch), list(rider.inputs)


def all_gather(blocks):
    n = len(blocks)

    def plan(x_refs, out_refs, sems):
        send_sems, recv_sems, local_sems = sems
        x, y, c = lax.axis_index("x"), lax.axis_index("y"), lax.axis_index("c")
        me, sibling = (x, y, c), (x, y, 1 - c)
        chips = [(1 - x, y), (x, 1 - y), (1 - x, 1 - y)]

        def copy(a, k, blk, to, src=None):
            px, py, pc = blk
            slot = out_refs[a].at[4 * px + 2 * py + pc]
            return pltpu.make_async_remote_copy(
                src_ref=slot if src is None else src, dst_ref=slot,
                send_sem=send_sems.at[7 * a + k], recv_sem=recv_sems.at[7 * a + k], device_id=to, device_id_type=MESH)

        mine = [pltpu.make_async_copy(x_refs[a], out_refs[a].at[4 * x + 2 * y + c], local_sems.at[a]) for a in range(n)]
        first = []
        for a in range(n):
            first += [copy(a, 0, me, sibling, src=x_refs[a])] + [copy(a, 1 + j, me, (*chip, c), src=x_refs[a]) for j, chip in enumerate(chips)]
        return c, me, sibling, chips, copy, mine, first

    def start(x_refs, out_refs, sems):
        _, _, _, _, _, mine, first = plan(x_refs, out_refs, sems)
        for cp in mine + first:
            cp.start()

    def relay(x_refs, out_refs, sems):
        c, me, sibling, chips, copy, _, _ = plan(x_refs, out_refs, sems)
        for j, chip in enumerate(chips):
            for a in range(n):
                copy(a, 1 + j, (*chip, c), me).wait_recv()
                copy(a, 4 + j, (*chip, c), sibling).start()

    def finish(x_refs, out_refs, sems):
        c, me, sibling, chips, copy, mine, first = plan(x_refs, out_refs, sems)
        passed = [copy(a, 4 + j, (*chip, c), sibling) for j, chip in enumerate(chips) for a in range(n)]
        for a in range(n):
            copy(a, 0, sibling, me).wait_recv()
            for j, chip in enumerate(chips):
                copy(a, 4 + j, (*chip, 1 - c), me).wait_recv()
        for cp in first + passed:
            cp.wait_send()
        for cp in mine:
            cp.wait()

    return _Exchange(list(blocks), [jax.ShapeDtypeStruct((N_DEV,) + b.shape, b.dtype) for b in blocks],
                     [pltpu.SemaphoreType.DMA((7 * n,)), pltpu.SemaphoreType.DMA((7 * n,)), pltpu.SemaphoreType.DMA((n,))], start, finish,
                     relay=relay)


def exchange_with_sibling(gs):
    n = len(gs)

    def plan(g_refs, recv_refs, sems):
        send_sems, recv_sems = sems
        x, y, c = lax.axis_index("x"), lax.axis_index("y"), lax.axis_index("c")
        return [pltpu.make_async_remote_copy(src_ref=g_refs[a].at[1], dst_ref=recv_refs[a], send_sem=send_sems.at[a],
                                             recv_sem=recv_sems.at[a], device_id=(x, y, 1 - c), device_id_type=MESH) for a in range(n)]

    def start(*refs):
        for cp in plan(*refs):
            cp.start()

    def finish(*refs):
        for cp in plan(*refs):
            cp.wait()

    return _Exchange(list(gs), [jax.ShapeDtypeStruct(g.shape[1:], g.dtype) for g in gs],
                     [pltpu.SemaphoreType.DMA((n,)), pltpu.SemaphoreType.DMA((n,))], start, finish)


def exchange_between_chips(parts):
    n = len(parts)

    def plan(p_refs, recv_refs, sems):
        send_sems, recv_sems = sems
        x, y, c = lax.axis_index("x"), lax.axis_index("y"), lax.axis_index("c")
        chips = [(2, (1 - x, y)), (1, (x, 1 - y)), (3, (1 - x, 1 - y))]
        return [pltpu.make_async_remote_copy(src_ref=p_refs[a].at[slot], dst_ref=recv_refs[a].at[k], send_sem=send_sems.at[3 * a + k],
                                             recv_sem=recv_sems.at[3 * a + k], device_id=(px, py, c), device_id_type=MESH)
                for a in range(n) for k, (slot, (px, py)) in enumerate(chips)]

    def start(*refs):
        for cp in plan(*refs):
            cp.start()

    def finish(*refs):
        for cp in plan(*refs):
            cp.wait()

    return _Exchange(list(parts), [jax.ShapeDtypeStruct((3,) + p.shape[1:], p.dtype) for p in parts],
                     [pltpu.SemaphoreType.DMA((3 * n,)), pltpu.SemaphoreType.DMA((3 * n,))], start, finish)


PARAMS = {
    "norm_w": ((DEPTH, 3, D_MODEL), 2),
    "ffn_w_gate": ((DEPTH, 2, D_MODEL, D_FF), 3),
    "ffn_w_up": ((DEPTH, 2, D_MODEL, D_FF), 3),
    "ffn_w_down": ((DEPTH, 2, D_FF, D_MODEL), 2),
    "ssd_w_in": ((2, D_MODEL, SSD_IN_DIM), 2),
    "ssd_conv_w": ((2, SSD_CONV_K, SSD_CONV_DIM), 2),
    "ssd_conv_b": ((2, SSD_CONV_DIM), None),
    "ssd_dt_bias": ((2, SSD_HEADS), None),
    "ssd_a_log": ((2, SSD_HEADS), None),
    "ssd_d": ((2, SSD_HEADS), None),
    "ssd_norm_w": ((2, SSD_INNER), None),
    "ssd_w_out": ((2, SSD_INNER, D_MODEL), 1),
    "sc_w_in": ((2, D_MODEL, 3 * D_MODEL), 2),
    "sc_conv_w": ((2, SC_CONV_K, D_MODEL), 2),
    "sc_w_out": ((2, D_MODEL, D_MODEL), 1),
    "final_norm_w": ((D_MODEL,), None),
}
NAMES = list(PARAMS)
BIG = ["ffn_w_gate", "ffn_w_up", "ffn_w_down", "ssd_w_in", "ssd_w_out", "sc_w_in", "sc_w_out"]
SMALL = [n for n in NAMES if n not in BIG]
SMALL_SHARDED = [n for n in SMALL if PARAMS[n][1] is not None]


def _round_up(n, m):
    return -(-n // m) * m


def _pack(flat_list, rows_multiple):
    flat = jnp.concatenate(flat_list)
    rows = _round_up(_round_up(flat.shape[0], PACK_W) // PACK_W, rows_multiple)
    return jnp.pad(flat, (0, rows * PACK_W - flat.shape[0])).reshape(rows, PACK_W)


def _unpack(packed, shapes, lead=()):
    flat = packed.reshape(lead + (-1,))
    out, off = [], 0
    for shp in shapes:
        n = 1
        for s in shp:
            n *= s
        out.append(flat[..., off:off + n].reshape(lead + tuple(shp)))
        off += n
    return out


def _local_shape(name):
    shp, ax = PARAMS[name]
    if ax is None:
        return shp
    return shp[:ax] + (shp[ax] // N_DEV,) + shp[ax + 1:]


def _full_from_gathered(g, name):
    shp, ax = PARAMS[name]
    return jnp.moveaxis(g, 0, ax).reshape(shp)


def _by_destination(full, name):
    shp, ax = PARAMS[name]
    loc = shp[ax] // N_DEV
    return jnp.moveaxis(full.reshape(shp[:ax] + (N_DEV, loc) + shp[ax + 1:]), ax, 0)


def _ssd_layer_fwd(xin, nw, p, rider=None):
    z, xbc, dt_raw = in_proj_fwd(xin, nw, [p["ssd_wz"], p["ssd_wx"], p["ssd_wdt"]], [bf16, bf16, f32])
    act, dt4 = ssd_conv_fwd(xbc, p["ssd_conv_w"], p["ssd_conv_b"], dt_raw, p["ssd_dt_bias"])
    y, states, *got = ssd_scan_fwd(act, dt4, p["ssd_alog4"], rider=rider)
    gn = ssd_gate_fwd(y, act, z, p["ssd_dx"], p["ssd_norm_w"])
    xout = out_proj_fwd(xin, gn, p["ssd_w_out"])
    return xout, (xin, z, xbc, dt_raw, act, dt4, y, states, gn), got


def _ssd_layer_bwd(dxo, nw, p, saved, gbuf, slab, rider=None):
    xin, z, xbc, dt_raw, act, dt4, y, states, gn = saved
    T = xin.shape[0]
    dy, dxs_skip, dz, dd_x, dgnw, dyb = ssd_gate_bwd(y, act, z, p["ssd_dx"], p["ssd_norm_w"], dxo, p["ssd_w_out"])
    gbuf["ssd_w_out"] = tn_matmul_to_shards(gn, dyb, gbuf["ssd_w_out"], (slab,), 0)
    g = {}
    g["ssd_norm_w"] = dgnw[0]
    g["ssd_d"] = jnp.sum(dd_x.reshape(SSD_HEADS, SSD_HEAD_DIM), axis=1)
    dxs, db, dc, ddt4, dalog4, *got = ssd_scan_bwd(act, dt4, p["ssd_alog4"], states, dy, rider=rider)
    g["ssd_a_log"] = dalog4[:, 0, :8].reshape(SSD_HEADS)
    dxbc, ddt_raw, dcw, dcb, ddtb = ssd_conv_bwd(xbc, p["ssd_conv_w"], p["ssd_conv_b"], dt_raw, p["ssd_dt_bias"], dxs, dxs_skip, db, dc, ddt4)
    g["ssd_conv_w"] = dcw[:SSD_CONV_K]
    g["ssd_conv_b"] = dcb[0]
    g["ssd_dt_bias"] = ddtb[0, :SSD_HEADS]
    dx, h, dnw = in_proj_bwd(xin, nw, dxo, [dz, dxbc, ddt_raw], [p["ssd_wz"], p["ssd_wx"], p["ssd_wdt"]])
    gbuf["ssd_w_in"] = ssd_in_to_shards(tn_matmul(h, dz), tn_matmul(h, dxbc), tn_matmul(h, ddt_raw), gbuf["ssd_w_in"], slab)
    return dx, dnw, g, got


def _sc_layer_fwd(xin, nw, p):
    (bcu,) = in_proj_fwd(xin, nw, [p["sc_w_in"]], [bf16])
    q = sc_mid_fwd(bcu, p["sc_conv_w"])
    return out_proj_fwd(xin, q, p["sc_w_out"]), (xin, bcu, q)


def _sc_layer_bwd(dxo, nw, p, saved, gbuf, slab):
    xin, bcu, q = saved
    dbcu, dcw, dyb = sc_mid_bwd(bcu, p["sc_conv_w"], dxo, p["sc_w_out"])
    gbuf["sc_w_out"] = tn_matmul_to_shards(q, dyb, gbuf["sc_w_out"], (slab,), 0)
    g = {"sc_conv_w": dcw[:SC_CONV_K]}
    dx, h, dnw = in_proj_bwd(xin, nw, dxo, [dbcu], [p["sc_w_in"]])
    gbuf["sc_w_in"] = tn_matmul_to_shards(h, dbcu, gbuf["sc_w_in"], (slab,), 1)
    return dx, dnw, g


def kernel(x, norm_w, ffn_w_gate, ffn_w_up, ffn_w_down, ssd_w_in, ssd_conv_w, ssd_conv_b, ssd_dt_bias, ssd_a_log, ssd_d, ssd_norm_w, ssd_w_out, sc_w_in, sc_conv_w, sc_w_out, final_norm_w, loss_target, m_norm_w, m_ffn_w_gate, m_ffn_w_up, m_ffn_w_down, m_ssd_w_in, m_ssd_conv_w, m_ssd_conv_b, m_ssd_dt_bias, m_ssd_a_log, m_ssd_d, m_ssd_norm_w, m_ssd_w_out, m_sc_w_in, m_sc_conv_w, m_sc_w_out, m_final_norm_w, v_norm_w, v_ffn_w_gate, v_ffn_w_up, v_ffn_w_down, v_ssd_w_in, v_ssd_conv_w, v_ssd_conv_b, v_ssd_dt_bias, v_ssd_a_log, v_ssd_d, v_ssd_norm_w, v_ssd_w_out, v_sc_w_in, v_sc_conv_w, v_sc_w_out, v_final_norm_w):
    w_loc = dict(zip(NAMES, (norm_w, ffn_w_gate, ffn_w_up, ffn_w_down, ssd_w_in, ssd_conv_w, ssd_conv_b, ssd_dt_bias, ssd_a_log, ssd_d, ssd_norm_w, ssd_w_out, sc_w_in, sc_conv_w, sc_w_out, final_norm_w)))
    m_loc = dict(zip(NAMES, (m_norm_w, m_ffn_w_gate, m_ffn_w_up, m_ffn_w_down, m_ssd_w_in, m_ssd_conv_w, m_ssd_conv_b, m_ssd_dt_bias, m_ssd_a_log, m_ssd_d, m_ssd_norm_w, m_ssd_w_out, m_sc_w_in, m_sc_conv_w, m_sc_w_out, m_final_norm_w)))
    v_loc = dict(zip(NAMES, (v_norm_w, v_ffn_w_gate, v_ffn_w_up, v_ffn_w_down, v_ssd_w_in, v_ssd_conv_w, v_ssd_conv_b, v_ssd_dt_bias, v_ssd_a_log, v_ssd_d, v_ssd_norm_w, v_ssd_w_out, v_sc_w_in, v_sc_conv_w, v_sc_w_out, v_final_norm_w)))
    my_dev = 4 * lax.axis_index("x") + 2 * lax.axis_index("y") + lax.axis_index("c")

    def as3d(a):
        return a.reshape((-1,) + a.shape[-2:])

    wb = {n: as3d(w_loc[n]).astype(bf16) for n in BIG}

    FFN = ["ffn_w_gate", "ffn_w_up", "ffn_w_down"]

    def mixer_names(i):
        return ["ssd_w_in", "ssd_w_out"] if i % 2 == 0 else ["sc_w_in", "sc_w_out"]

    ag_sets = [[(n, 0, 1) for n in FFN], [(n, 1, 1) for n in FFN] + [(n, 0, 1) for n in mixer_names(0)]]
    ag_sets += [[(n, 2 * r, 2) for n in FFN] + [(n, r // 2, 1) for n in mixer_names(r)] for r in (1, 2, 3)]

    def set_blocks(spec):
        return [wb[n][a0:a0 + na] for n, a0, na in spec]

    def set_weights(spec, gathered):
        q = {}
        for (n, _, _), g in zip(spec, gathered):
            if n == "ssd_w_in":
                q["ssd_wz"], q["ssd_wx"], q["ssd_wdt"] = assemble_ssd_in(g)
            else:
                q[n] = assemble(g, 1 if PARAMS[n][1] == len(PARAMS[n][0]) - 1 else 0)
        return q

    ss_shapes = [_local_shape(n) for n in SMALL_SHARDED]
    gathered0 = all_gather(set_blocks(ag_sets[0]) + [_pack([w_loc[n].reshape(-1) for n in SMALL_SHARDED], 8)]).run("all_gather_first")
    full = {}
    for n, part in zip(SMALL_SHARDED, _unpack(gathered0[-1], ss_shapes, lead=(N_DEV,))):
        full[n] = _full_from_gathered(part, n)
    for n in SMALL:
        if PARAMS[n][1] is None:
            full[n] = w_loc[n]
    small = {
        "ssd_conv_w": full["ssd_conv_w"],
        "ssd_conv_b": full["ssd_conv_b"].reshape(2, 1, SSD_CONV_DIM),
        "ssd_dt_bias": jnp.pad(full["ssd_dt_bias"], ((0, 0), (0, LANES - SSD_HEADS))).reshape(2, 1, LANES),
        "ssd_alog4": jnp.pad(full["ssd_a_log"].reshape(2, SSD_GROUPS, 1, 8), ((0, 0), (0, 0), (0, 0), (0, LANES - 8))),
        "ssd_dx": jnp.repeat(full["ssd_d"], SSD_HEAD_DIM, axis=1).reshape(2, 1, SSD_INNER),
        "ssd_norm_w": full["ssd_norm_w"].reshape(2, 1, SSD_INNER),
        "sc_conv_w": full["sc_conv_w"],
    }
    nw_all = full["norm_w"].reshape(DEPTH, 3, 1, D_MODEL)

    ffn_w = [[None, None] for _ in range(DEPTH)]
    mix_w = [None] * DEPTH

    def arrived(s, gathered):
        q = set_weights(ag_sets[s], gathered)
        ffn = tuple(q[n] for n in FFN)
        if s == 0:
            ffn_w[0][0] = ffn + ((0,),)
            return
        i = 0 if s == 1 else s - 1
        if s == 1:
            ffn_w[0][1] = ffn + ((0,),)
        else:
            ffn_w[i] = [ffn + ((0,),), ffn + ((1,),)]
        m = {n: v[0] for n, v in q.items() if n not in FFN}
        m.update({n: v[i // 2] for n, v in small.items() if n.startswith("ssd" if i % 2 == 0 else "sc")})
        mix_w[i] = m

    def rider_for(s):
        return all_gather(set_blocks(ag_sets[s]))

    xc = x[0]
    saved = []
    arrived(0, gathered0[:-1])
    for i in range(DEPTH):
        carried = {0: (1, 2, 3), 1: (4, None, None)}.get(i, (None, None, None))
        wg, wu, wd, idx = ffn_w[i][0]
        x1, g1, u1, a1, *got = ffn_fwd(xc, nw_all[i, 0], wg, wu, wd, idx, rider=rider_for(carried[0]) if carried[0] else None)
        if carried[0]:
            arrived(carried[0], got)
        if i % 2 == 0:
            x2, mix_saved, got = _ssd_layer_fwd(x1, nw_all[i, 1], mix_w[i], rider=rider_for(carried[1]) if carried[1] else None)
            if carried[1]:
                arrived(carried[1], got)
        else:
            x2, mix_saved = _sc_layer_fwd(x1, nw_all[i, 1], mix_w[i])
        wg, wu, wd, idx = ffn_w[i][1]
        x3, g3, u3, a3, *got = ffn_fwd(x2, nw_all[i, 2], wg, wu, wd, idx, rider=rider_for(carried[2]) if carried[2] else None)
        if carried[2]:
            arrived(carried[2], got)
        saved.append(((xc, g1, u1, a1), mix_saved, (x2, g3, u3, a3)))
        xc = x3

    loss_row, dx, dfw = loss_head(xc, full["final_norm_w"].reshape(1, D_MODEL), loss_target[0])
    loss = lax.psum(loss_row[0, 0], ("x", "y", "c"))

    grads = {n: [None] * PARAMS[n][0][0] for n in SMALL if n != "final_norm_w"}
    grads["final_norm_w"] = dfw[0]
    dnorm = [[None] * 3 for _ in range(DEPTH)]
    def slabs(n, which):
        if n.startswith("ffn"):
            return {"early": (2, 6), "mid": (1, 1), "last": (0, 1)}[which]
        if n.startswith("ssd"):
            return {"early": (1, 1), "mid": (0, 1), "last": (0, 0)}[which]
        return {"early": (0, 2), "mid": (0, 0), "last": (0, 0)}[which]

    TRANSPOSED = ("ffn_w_gate", "ffn_w_up")

    def shard3d(a, n):
        return jnp.swapaxes(as3d(a), 1, 2) if n in TRANSPOSED else as3d(a)

    gb = {which: {n: jax.ShapeDtypeStruct((2, 4, slabs(n, which)[1]) + shard3d(wb[n], n).shape[1:], f32) for n in BIG if slabs(n, which)[1]}
          for which in ("early", "mid", "last")}

    def ffn_back(i, k, dxo, sv, rider=None):
        xin, g_, u_, a_ = sv
        which = "early" if i > 0 else ("mid" if k == 1 else "last")
        gbuf = gb[which]
        slab = 2 * i + k - slabs("ffn_w_gate", which)[0]
        wg, wu, wd, idx = ffn_w[i][k]
        dxi, h, dyb, dg, du, dnw, *got = ffn_bwd_dx(xin, dxo, g_, u_, nw_all[i, 2 * k], wg, wu, wd, idx, rider=rider)
        dnorm[i][2 * k] = dnw[0]
        gbuf["ffn_w_gate"] = tn_matmul_to_shards(dg, h, gbuf["ffn_w_gate"], (slab,), 0)
        gbuf["ffn_w_up"] = tn_matmul_to_shards(du, h, gbuf["ffn_w_up"], (slab,), 0)
        gbuf["ffn_w_down"] = tn_matmul_to_shards(a_, dyb, gbuf["ffn_w_down"], (slab,), 0)
        return dxi, got

    def reduce_in_chip(gbuf, from_sibling=None):
        names = list(gbuf)
        bufs = [gbuf[n] for n in names]
        if from_sibling is None:
            from_sibling = exchange_with_sibling(bufs).run("exchange_with_sibling")
        return names, bufs, from_sibling, [pair_sum_bf16(g, fs, "pair_sum_" + n) for n, g, fs in zip(names, bufs, from_sibling)]

    reduced, from_chips = {}, {}
    for i in reversed(range(DEPTH)):
        j = i // 2
        sv_a, sv_mix, sv_b = saved[i]
        if i == 0:
            dx, got = ffn_back(i, 1, dx, sv_b, rider=exchange_with_sibling(list(gb["early"].values())))
            reduced["early"] = reduce_in_chip(gb["early"], from_sibling=got)
        else:
            dx, _ = ffn_back(i, 1, dx, sv_b)
        if i % 2 == 0:
            rider = exchange_between_chips(reduced["early"][3]) if i == 0 else None
            dx, dnw, gm, got = _ssd_layer_bwd(dx, nw_all[i, 1], mix_w[i], sv_mix, gb["mid" if i == 0 else "early"], 0, rider=rider)
            if i == 0:
                from_chips["early"] = got
                reduced["mid"] = reduce_in_chip(gb["mid"])
        else:
            dx, dnw, gm = _sc_layer_bwd(dx, nw_all[i, 1], mix_w[i], sv_mix, gb["early"], j)
        dnorm[i][1] = dnw[0]
        for n, val in gm.items():
            grads[n][j] = val
        dx, got = ffn_back(i, 0, dx, sv_a, rider=exchange_between_chips(reduced["mid"][3]) if i == 0 else None)
        if i == 0:
            from_chips["mid"] = got

    grads["norm_w"] = jnp.stack([jnp.stack(r) for r in dnorm])
    for n in SMALL:
        if isinstance(grads[n], list):
            grads[n] = jnp.stack(grads[n])

    reduced["last"] = reduce_in_chip(gb["last"])
    from_chips["last"] = exchange_between_chips(reduced["last"][3]).run("exchange_between_chips")
    results = [{}, {}, {}, {}]
    outs = {}
    for which in ("last", "mid", "early"):
        names, bufs, from_sibling, _ = reduced[which]
        for n, g, fs, fc in zip(names, bufs, from_sibling, from_chips[which]):
            parts = [((0, 0), g), ((0,), fs), ((0,), fc), ((1,), fc), ((2,), fc)]
            outs[n] = adamw(parts, shard3d(w_loc[n], n), shard3d(m_loc[n], n), shard3d(v_loc[n], n), name="adamw_" + n + "_" + which,
                            a0=slabs(n, which)[0], prev=outs.get(n))
    for n in BIG:
        for k in range(4):
            o = outs[n][k]
            results[k][n] = (jnp.swapaxes(o, 1, 2) if n in TRANSPOSED else o).reshape(_local_shape(n))

    g_small = _pack([grads[n].reshape(-1) for n in SMALL], 8)
    g_small = sum_over_devices(all_gather([g_small]).run("all_gather_small_grads")[0])
    g_small_full = dict(zip(SMALL, _unpack(g_small, [PARAMS[n][0] for n in SMALL])))
    g_small_loc = []
    for n in SMALL:
        if PARAMS[n][1] is None:
            g_small_loc.append(g_small_full[n])
        else:
            g_small_loc.append(lax.dynamic_index_in_dim(_by_destination(g_small_full[n], n), my_dev, axis=0, keepdims=False))
    small_shapes = [_local_shape(n) for n in SMALL]
    pack_small = lambda d: _pack([d[n].reshape(-1) for n in SMALL], 8)[None]
    small_out = adamw([_pack([gl.reshape(-1) for gl in g_small_loc], 8)[None]], pack_small(w_loc), pack_small(m_loc), pack_small(v_loc), name="adamw_small")
    for k in range(4):
        results[k].update(zip(SMALL, _unpack(small_out[k], small_shapes)))
    return (loss, dx[None], *[results[0][n] for n in NAMES], *[results[1][n] for n in NAMES],
            *[results[2][n] for n in NAMES], *[results[3][n] for n in NAMES])
```

```python
import functools

import jax
import jax.numpy as jnp
from jax import lax
from jax.experimental import pallas as pl
from jax.experimental.pallas import tpu as pltpu

f32 = jnp.float32
bf16 = jnp.bfloat16

D_MODEL = 1024
D_FF = 2816
DEPTH = 4
SSD_INNER = 2048
SSD_HEADS = 32
SSD_HEAD_DIM = 64
SSD_GROUPS = 4
SSD_STATE = 128
SSD_CONV_K = 4
SSD_CONV_DIM = 3072
SSD_IN_DIM = 5152
SSD_CHUNK = 128
SC_CONV_K = 3
RMS_EPS = 1e-5
N_DEV = 8
LANES = 128
HALO = 16
PACK_W = 1024
VMEM_LIMIT = 56 * 1024 * 1024
NEG_BIG = -1e30

ADAM_LR = 0.001
ADAM_B1 = 0.9
ADAM_B2 = 0.999
ADAM_EPS = 1e-08
ADAM_WD = 0.01
ADAM_STEP = 10

NT_DIMS = (((1,), (1,)), ((), ()))
TN_DIMS = (((0,), (0,)), ((), ()))
MESH = pl.DeviceIdType.MESH


def _params(sem=None, vmem_limit=VMEM_LIMIT):
    return pltpu.CompilerParams(dimension_semantics=sem, vmem_limit_bytes=vmem_limit)


def _resident(shape):
    nd = len(shape)
    return pl.BlockSpec(tuple(shape), lambda *_: (0,) * nd, pipeline_mode=pl.Buffered(1))


def _rows(tm, width):
    return pl.BlockSpec((tm, width), lambda i: (i, 0))


def _my_core_and_chip():
    return lax.axis_index("c"), 2 * lax.axis_index("x") + lax.axis_index("y")


def _sigmoid(v):
    return 0.5 * jnp.tanh(0.5 * v) + 0.5


def _softplus(v):
    return jnp.maximum(v, 0.0) + jnp.log(1.0 + jnp.exp(-jnp.abs(v)))


def _rms_fwd(xv, w):
    inv = lax.rsqrt(jnp.mean(xv * xv, axis=-1, keepdims=True) + RMS_EPS)
    xh = xv * inv
    return xh * w, xh, inv


def _rms_bwd(dh, xh, inv, w):
    dxh = dh * w
    dx = inv * (dxh - xh * jnp.mean(dxh * xh, axis=-1, keepdims=True))
    return dx, jnp.sum(dh * xh, axis=0, keepdims=True)


def _mm(a, b):
    return jnp.dot(a, b, preferred_element_type=f32)


def _mm_nt(a, b):
    return lax.dot_general(a, b, NT_DIMS, preferred_element_type=f32)


def _mm_tn(a, b):
    return lax.dot_general(a, b, TN_DIMS, preferred_element_type=f32)


def _layer_slab(w, idx):
    tail = w.shape[len(idx):]
    return pl.BlockSpec((None,) * len(idx) + tuple(tail), lambda *_: tuple(idx) + (0,) * len(tail), pipeline_mode=pl.Buffered(1))


def ffn_fwd(x, nw, wg, wu, wd, idx, tm=512, rider=None):
    T = x.shape[0]
    nt = T // tm
    r_in, r_out, r_shapes, r_scratch, r_args = _rider_specs(rider)

    def body(x_ref, nw_ref, wg_ref, wu_ref, wd_ref, xo_ref, g_ref, u_ref, a_ref):
        xv = x_ref[...]
        h, _, _ = _rms_fwd(xv, nw_ref[...])
        hb = h.astype(bf16)
        g = _mm(hb, wg_ref[...])
        u = _mm(hb, wu_ref[...])
        ab = (g * _sigmoid(g) * u).astype(bf16)
        g_ref[...] = g.astype(bf16)
        u_ref[...] = u.astype(bf16)
        a_ref[...] = ab
        xo_ref[...] = xv + 0.5 * _mm(ab, wd_ref[...])

    hosted = _carry(body, 5, 4, rider, lambda: pl.program_id(0) == 0, lambda: pl.program_id(0) == nt - 1,
                    late=lambda: pl.program_id(0) == (7 * nt) // 8)
    return pl.pallas_call(
        hosted, name="ffn_fwd" if rider is None else "ffn_fwd_carrying", grid=(nt,),
        in_specs=[_rows(tm, D_MODEL), _resident((1, D_MODEL)), _layer_slab(wg, idx), _layer_slab(wu, idx), _layer_slab(wd, idx)] + r_in,
        out_specs=[_rows(tm, D_MODEL), _rows(tm, D_FF), _rows(tm, D_FF), _rows(tm, D_FF)] + r_out,
        out_shape=[jax.ShapeDtypeStruct((T, D_MODEL), f32)] + [jax.ShapeDtypeStruct((T, D_FF), bf16)] * 3 + r_shapes,
        scratch_shapes=r_scratch,
        compiler_params=_params(("parallel",) if rider is None else ("arbitrary",)),
    )(x, nw, wg, wu, wd, *r_args)


def ffn_bwd_dx(x, dxo, g, u, nw, wg, wu, wd, idx, tm=512, rider=None):
    T = x.shape[0]
    nt = T // tm
    r_in, r_out, r_shapes, r_scratch, r_args = _rider_specs(rider)

    def body(x_ref, dxo_ref, g_ref, u_ref, nw_ref, wg_ref, wu_ref, wd_ref, dx_hbm, h_hbm, dy_hbm, dg_hbm, du_hbm, dnw_ref,
             dx_ref, h_ref, dy_ref, dg_ref, du_ref, out_sems):
        i = pl.program_id(0)

        def out_copies(step):
            rows = pl.ds(pl.multiple_of(step * tm, tm), tm)
            pairs = [(dg_ref, dg_hbm), (du_ref, du_hbm), (dx_ref, dx_hbm), (h_ref, h_hbm), (dy_ref, dy_hbm)]
            return [pltpu.make_async_copy(src, dst.at[rows, :], out_sems.at[k]) for k, (src, dst) in enumerate(pairs)]

        @pl.when(i == 0)
        def _():
            dnw_ref[...] = jnp.zeros_like(dnw_ref)

        @pl.when(i > 0)
        def _():
            for cp in out_copies(i - 1):
                cp.wait()

        w = nw_ref[...]
        half = tm // 2
        for r0 in (0, half):
            rows = pl.ds(r0, half)
            h, xh, inv = _rms_fwd(x_ref[rows, :], w)
            dxo_v = dxo_ref[rows, :]
            dyb = (0.5 * dxo_v).astype(bf16)
            da = _mm_nt(dyb, wd_ref[...])
            gv = g_ref[rows, :].astype(f32)
            uv = u_ref[rows, :].astype(f32)
            s = _sigmoid(gv)
            dg_ref[rows, :] = (da * uv * (s * (1.0 + gv * (1.0 - s)))).astype(bf16)
            du_ref[rows, :] = (da * (gv * s)).astype(bf16)
            dh = _mm_nt(dg_ref[rows, :], wg_ref[...]) + _mm_nt(du_ref[rows, :], wu_ref[...])
            dxn, dw = _rms_bwd(dh, xh, inv, w)
            dx_ref[rows, :] = dxo_v + dxn
            h_ref[rows, :] = h.astype(bf16)
            dy_ref[rows, :] = dyb
            dnw_ref[...] += dw
        for cp in out_copies(i):
            cp.start()

        @pl.when(i == nt - 1)
        def _():
            for cp in out_copies(i):
                cp.wait()

    hosted = _carry(body, 8, 6, rider, lambda: pl.program_id(0) == 0, lambda: pl.program_id(0) == nt - 1)
    return pl.pallas_call(
        hosted, name="ffn_bwd_dx" if rider is None else "ffn_bwd_dx_carrying", grid=(nt,),
        in_specs=[_rows(tm, D_MODEL), _rows(tm, D_MODEL), _rows(tm, D_FF), _rows(tm, D_FF), _resident((1, D_MODEL)),
                  _layer_slab(wg, idx), _layer_slab(wu, idx), _layer_slab(wd, idx)] + r_in,
        out_specs=[_ANY, _ANY, _ANY, _ANY, _ANY, pl.BlockSpec((1, D_MODEL), lambda i: (0, 0))] + r_out,
        out_shape=[jax.ShapeDtypeStruct((T, D_MODEL), f32), jax.ShapeDtypeStruct((T, D_MODEL), bf16), jax.ShapeDtypeStruct((T, D_MODEL), bf16),
                   jax.ShapeDtypeStruct((T, D_FF), bf16), jax.ShapeDtypeStruct((T, D_FF), bf16), jax.ShapeDtypeStruct((1, D_MODEL), f32)] + r_shapes,
        scratch_shapes=[pltpu.VMEM((tm, D_MODEL), f32), pltpu.VMEM((tm, D_MODEL), bf16), pltpu.VMEM((tm, D_MODEL), bf16),
                        pltpu.VMEM((tm, D_FF), bf16), pltpu.VMEM((tm, D_FF), bf16), pltpu.SemaphoreType.DMA((5,))] + r_scratch,
        compiler_params=_params(("arbitrary",)),
    )(x, dxo, g, u, nw, wg, wu, wd, *r_args)


def tn_matmul(a, b, tk=1024):
    T, M = a.shape
    N = b.shape[1]
    bn = N if M * N <= 3_200_000 else N // 2
    nk = T // tk

    def body(a_ref, b_ref, o_ref):
        @pl.when(pl.program_id(1) == 0)
        def _():
            o_ref[...] = jnp.zeros_like(o_ref)

        o_ref[...] += _mm_tn(a_ref[...], b_ref[...])

    return pl.pallas_call(
        body, name=f"tn_matmul_{M}x{N}", grid=(N // bn, nk),
        in_specs=[pl.BlockSpec((tk, M), lambda j, k: (k, 0)), pl.BlockSpec((tk, bn), lambda j, k: (k, j))],
        out_specs=pl.BlockSpec((M, bn), lambda j, k: (0, j)),
        out_shape=jax.ShapeDtypeStruct((M, N), f32),
        compiler_params=_params(("parallel", "arbitrary")),
    )(a, b)


def tn_matmul_to_shards(a, b, buf, idx, axis):
    T, M = a.shape
    N = b.shape[1]
    m, n = buf.shape[-2:]
    (slab,) = idx
    tk = 1024
    nk = T // tk
    fresh = isinstance(buf, jax.ShapeDtypeStruct)

    def body(a_ref, b_ref, *rest):
        o_ref, acc_ref, stage_ref, sem = rest[-4:]
        k = pl.program_id(0)

        @pl.when(k == 0)
        def _():
            acc_ref[...] = jnp.zeros_like(acc_ref)

        acc_ref[...] += _mm_tn(a_ref[...], b_ref[...])

        @pl.when(k == nk - 1)
        def _():
            my_c, my_chip = _my_core_and_chip()
            for d in range(N_DEV):
                piece = acc_ref[:, pl.ds(d * n, n)] if axis == 1 else acc_ref[pl.ds(d * m, m), :]
                stage_ref[(d % 2) ^ my_c, (d // 2) ^ my_chip] = piece
            out = pltpu.make_async_copy(stage_ref, o_ref.at[:, :, slab], sem)
            out.start()
            out.wait()

    return pl.pallas_call(
        body, name=f"tn_matmul_to_shards_{M}x{N}_{axis}", grid=(nk,),
        in_specs=[pl.BlockSpec((tk, M), lambda k: (k, 0)), pl.BlockSpec((tk, N), lambda k: (k, 0))] + ([] if fresh else [_ANY]),
        out_specs=_ANY,
        out_shape=jax.ShapeDtypeStruct(buf.shape, f32),
        scratch_shapes=[pltpu.VMEM((M, N), f32), pltpu.VMEM((2, 4, m, n), f32), pltpu.SemaphoreType.DMA],
        input_output_aliases={} if fresh else {2: 0},
        compiler_params=_params(("arbitrary",)),
    )(a, b, *([] if fresh else [buf]))


def in_proj_fwd(x, nw, ws, out_dtypes, tm=512):
    T = x.shape[0]
    n = len(ws)

    def body(*refs):
        x_ref, nw_ref = refs[:2]
        w_refs = refs[2:2 + n]
        o_refs = refs[2 + n:]
        h, _, _ = _rms_fwd(x_ref[...], nw_ref[...])
        hb = h.astype(bf16)
        for w_ref, o_ref in zip(w_refs, o_refs):
            o_ref[...] = _mm(hb, w_ref[...]).astype(o_ref.dtype)

    return pl.pallas_call(
        body, name="in_proj_fwd_" + "_".join(str(w.shape[1]) for w in ws), grid=(T // tm,),
        in_specs=[_rows(tm, D_MODEL), _resident((1, D_MODEL))] + [_resident(w.shape) for w in ws],
        out_specs=[_rows(tm, w.shape[1]) for w in ws],
        out_shape=[jax.ShapeDtypeStruct((T, w.shape[1]), dt) for w, dt in zip(ws, out_dtypes)],
        compiler_params=_params(("parallel",)),
    )(x, nw, *ws)


def in_proj_bwd(x, nw, dxo, dys, ws, tm=512):
    T = x.shape[0]
    n = len(ws)

    def body(*refs):
        x_ref, nw_ref, dxo_ref = refs[:3]
        dy_refs = refs[3:3 + n]
        w_refs = refs[3 + n:3 + 2 * n]
        dx_ref, h_ref, dnw_ref = refs[3 + 2 * n:]
        w = nw_ref[...]
        h, xh, inv = _rms_fwd(x_ref[...], w)
        dh = _mm_nt(dy_refs[0][...], w_refs[0][...])
        for dy_ref, w_ref in zip(dy_refs[1:], w_refs[1:]):
            dh = dh + _mm_nt(dy_ref[...], w_ref[...])
        dxn, dw = _rms_bwd(dh, xh, inv, w)
        dx_ref[...] = dxo_ref[...] + dxn
        h_ref[...] = h.astype(bf16)

        @pl.when(pl.program_id(0) == 0)
        def _():
            dnw_ref[...] = jnp.zeros_like(dnw_ref)

        dnw_ref[...] += dw

    return pl.pallas_call(
        body, name="in_proj_bwd_" + "_".join(str(w.shape[1]) for w in ws), grid=(T // tm,),
        in_specs=[_rows(tm, D_MODEL), _resident((1, D_MODEL)), _rows(tm, D_MODEL)] + [_rows(tm, w.shape[1]) for w in ws]
        + [_resident(w.shape) for w in ws],
        out_specs=[_rows(tm, D_MODEL), _rows(tm, D_MODEL), pl.BlockSpec((1, D_MODEL), lambda i: (0, 0))],
        out_shape=[jax.ShapeDtypeStruct((T, D_MODEL), f32), jax.ShapeDtypeStruct((T, D_MODEL), bf16), jax.ShapeDtypeStruct((1, D_MODEL), f32)],
        compiler_params=_params(("arbitrary",)),
    )(x, nw, dxo, *dys, *ws)


def out_proj_fwd(x, a, w, tm=1024):
    T = x.shape[0]
    K = a.shape[1]

    def body(x_ref, a_ref, w_ref, o_ref):
        o_ref[...] = x_ref[...] + _mm(a_ref[...], w_ref[...])

    return pl.pallas_call(
        body, name=f"out_proj_fwd_{K}", grid=(T // tm,),
        in_specs=[_rows(tm, D_MODEL), _rows(tm, K), _resident(w.shape)],
        out_specs=_rows(tm, D_MODEL), out_shape=jax.ShapeDtypeStruct((T, D_MODEL), f32),
        compiler_params=_params(("parallel",)),
    )(x, a, w)


def _halo_spec(tm, width, n_tiles, reverse):
    per = tm // HALO

    def idx(i):
        t = (n_tiles - 1 - i) if reverse else i
        return (jnp.maximum(t * per - 1, 0), 0)

    return pl.BlockSpec((HALO, width), idx)


def _tile_spec(tm, width, n_tiles, reverse):
    if reverse:
        return pl.BlockSpec((tm, width), lambda i: (n_tiles - 1 - i, 0))
    return _rows(tm, width)


ROW_BLOCK = 64


def _strip(s):
    return pl.ds(pl.multiple_of(s * LANES, LANES), LANES)


def _conv_rows(ext_ref, w_ref, cols, k_w, r0):
    base = HALO - (k_w - 1) + r0
    wins = [ext_ref[pl.ds(base + k, ROW_BLOCK), :] for k in range(k_w)]
    out = w_ref[pl.ds(0, 1), cols] * wins[0]
    for k in range(1, k_w):
        out = out + w_ref[pl.ds(k, 1), cols] * wins[k]
    return out, wins


def _shifted_back(d_ref, w_ref, cols, k_w, r0):
    out = w_ref[pl.ds(0, 1), cols] * d_ref[pl.ds(r0 + k_w - 1, ROW_BLOCK), :]
    for k in range(1, k_w):
        out = out + w_ref[pl.ds(k, 1), cols] * d_ref[pl.ds(r0 + k_w - 1 - k, ROW_BLOCK), :]
    return out


def ssd_conv_fwd(xbc, conv_w, conv_b, dt_raw, dt_bias, tm=512):
    T = xbc.shape[0]
    nt = T // tm
    K = SSD_CONV_K

    def body(x_ref, halo_ref, w_ref, b_ref, dtr_ref, dtb_ref, act_ref, dt_ref, ext_ref):
        first = pl.program_id(0) == 0

        def strip(s, carry):
            cols = _strip(s)
            ext_ref[pl.ds(0, HALO), :] = jnp.where(first, 0.0, halo_ref[:, cols].astype(f32))
            ext_ref[pl.ds(HALO, tm), :] = x_ref[:, cols].astype(f32)
            for r0 in range(0, tm, ROW_BLOCK):
                pre, _ = _conv_rows(ext_ref, w_ref, cols, K, r0)
                pre = pre + b_ref[:, cols]
                act_ref[pl.ds(r0, ROW_BLOCK), cols] = (pre * _sigmoid(pre)).astype(bf16)
            return carry

        lax.fori_loop(0, SSD_CONV_DIM // LANES, strip, 0)
        dt = _softplus(dtr_ref[...] + dtb_ref[...])
        lane = lax.broadcasted_iota(jnp.int32, (1, LANES), 1)
        for g in range(SSD_GROUPS):
            dt_ref[g] = jnp.where(lane < 8, dt if g == 0 else pltpu.roll(dt, LANES - 8 * g, axis=1), 0.0)

    return pl.pallas_call(
        body, name="ssd_conv_fwd", grid=(nt,),
        in_specs=[_rows(tm, SSD_CONV_DIM), _halo_spec(tm, SSD_CONV_DIM, nt, False), _resident(conv_w.shape), _resident(conv_b.shape),
                  _rows(tm, LANES), _resident(dt_bias.shape)],
        out_specs=[_rows(tm, SSD_CONV_DIM), pl.BlockSpec((SSD_GROUPS, tm, LANES), lambda i: (0, i, 0))],
        out_shape=[jax.ShapeDtypeStruct((T, SSD_CONV_DIM), bf16), jax.ShapeDtypeStruct((SSD_GROUPS, T, LANES), f32)],
        scratch_shapes=[pltpu.VMEM((tm + HALO, LANES), f32)],
        compiler_params=_params(("parallel",)),
    )(xbc, xbc, conv_w, conv_b, dt_raw, dt_bias)


def ssd_conv_bwd(xbc, conv_w, conv_b, dt_raw, dt_bias, dxs_a, dxs_b, db, dc, ddt, tm=512):
    T = xbc.shape[0]
    nt = T // tm
    K = SSD_CONV_K

    def body(x_ref, halo_ref, w_ref, b_ref, dtr_ref, dtb_ref, da_ref, dbb_ref, db_ref, dc_ref, ddt_ref,
             dx_ref, ddtr_ref, dw_ref, dbias_ref, ddtb_ref, ext_ref, dpre_ref, carry_ref):
        i = pl.program_id(0)

        @pl.when(i == 0)
        def _():
            carry_ref[...] = jnp.zeros_like(carry_ref)
            dw_ref[...] = jnp.zeros_like(dw_ref)
            dbias_ref[...] = jnp.zeros_like(dbias_ref)
            ddtb_ref[...] = jnp.zeros_like(ddtb_ref)

        first_tile = i == nt - 1

        def run_strips(lo, hi, load_dact):
            def strip(s, carry):
                cols = _strip(s)
                ext_ref[pl.ds(0, HALO), :] = jnp.where(first_tile, 0.0, halo_ref[:, cols].astype(f32))
                ext_ref[pl.ds(HALO, tm), :] = x_ref[:, cols].astype(f32)
                dpre_ref[pl.ds(tm, 8), :] = carry_ref[:, cols]
                bias = b_ref[:, cols]
                dws = [jnp.zeros((1, LANES), f32) for _ in range(K)]
                dbs = jnp.zeros((1, LANES), f32)
                for r0 in range(0, tm, ROW_BLOCK):
                    pre, wins = _conv_rows(ext_ref, w_ref, cols, K, r0)
                    pre = pre + bias
                    sg = _sigmoid(pre)
                    dpre = load_dact(s, r0) * (sg * (1.0 + pre * (1.0 - sg)))
                    dpre_ref[pl.ds(r0, ROW_BLOCK), :] = dpre
                    dbs = dbs + jnp.sum(dpre, axis=0, keepdims=True)
                    for k in range(K):
                        dws[k] = dws[k] + jnp.sum(dpre * wins[k], axis=0, keepdims=True)
                carry_ref[:, cols] = dpre_ref[pl.ds(0, 8), :]
                for r0 in range(0, tm, ROW_BLOCK):
                    dx_ref[pl.ds(r0, ROW_BLOCK), cols] = _shifted_back(dpre_ref, w_ref, cols, K, r0).astype(bf16)
                for k in range(K):
                    dw_ref[pl.ds(k, 1), cols] += dws[k]
                dbias_ref[:, cols] += dbs
                return carry

            lax.fori_loop(lo, hi, strip, 0)

        rows = lambda r0: pl.ds(r0, ROW_BLOCK)
        n_x = SSD_INNER // LANES
        n_g = SSD_GROUPS * SSD_STATE // LANES
        run_strips(0, n_x, lambda s, r0: da_ref[rows(r0), _strip(s)].astype(f32) + dbb_ref[rows(r0), _strip(s)].astype(f32))
        run_strips(n_x, n_x + n_g, lambda s, r0: db_ref[rows(r0), _strip(s - n_x)].astype(f32))
        run_strips(n_x + n_g, n_x + 2 * n_g, lambda s, r0: dc_ref[rows(r0), _strip(s - n_x - n_g)].astype(f32))
        lane = lax.broadcasted_iota(jnp.int32, (1, LANES), 1)
        ddt = jnp.where(lane < 8, ddt_ref[0], 0.0)
        for g in range(1, SSD_GROUPS):
            ddt = ddt + pltpu.roll(jnp.where(lane < 8, ddt_ref[g], 0.0), 8 * g, axis=1)
        ddtr = ddt * _sigmoid(dtr_ref[...] + dtb_ref[...])
        ddtr_ref[...] = ddtr.astype(bf16)
        ddtb_ref[...] += jnp.sum(ddtr, axis=0, keepdims=True)

    rev = functools.partial(_tile_spec, tm, n_tiles=nt, reverse=True)
    const = lambda shape: pl.BlockSpec(shape, lambda i: (0, 0))
    return pl.pallas_call(
        body, name="ssd_conv_bwd", grid=(nt,),
        in_specs=[rev(width=SSD_CONV_DIM), _halo_spec(tm, SSD_CONV_DIM, nt, True), _resident(conv_w.shape), _resident(conv_b.shape),
                  rev(width=LANES), _resident(dt_bias.shape), rev(width=SSD_INNER), rev(width=SSD_INNER),
                  rev(width=SSD_GROUPS * SSD_STATE), rev(width=SSD_GROUPS * SSD_STATE),
                  pl.BlockSpec((SSD_GROUPS, tm, LANES), lambda i: (0, nt - 1 - i, 0))],
        out_specs=[rev(width=SSD_CONV_DIM), rev(width=LANES), const((8, SSD_CONV_DIM)), const((1, SSD_CONV_DIM)), const((1, LANES))],
        out_shape=[jax.ShapeDtypeStruct((T, SSD_CONV_DIM), bf16), jax.ShapeDtypeStruct((T, LANES), bf16),
                   jax.ShapeDtypeStruct((8, SSD_CONV_DIM), f32), jax.ShapeDtypeStruct((1, SSD_CONV_DIM), f32), jax.ShapeDtypeStruct((1, LANES), f32)],
        scratch_shapes=[pltpu.VMEM((tm + HALO, LANES), f32), pltpu.VMEM((tm + 8, LANES), f32), pltpu.VMEM((8, SSD_CONV_DIM), f32)],
        compiler_params=_params(("arbitrary",)),
    )(xbc, xbc, conv_w, conv_b, dt_raw, dt_bias, dxs_a, dxs_b, db, dc, ddt)


def _ssd_chunk(xs, bm, cm, dt, alog, st):
    L = SSD_CHUNK
    row = lax.broadcasted_iota(jnp.int32, (L, L), 0)
    col = lax.broadcasted_iota(jnp.int32, (L, L), 1)
    causal = row >= col
    tril = jnp.where(causal, 1.0, 0.0).astype(f32)
    lane = lax.broadcasted_iota(jnp.int32, (1, LANES), 1)
    sub = lax.broadcasted_iota(jnp.int32, (LANES, 1), 0)
    lo = lane < SSD_HEAD_DIM
    last_row = sub == L - 1

    dta = dt * (-jnp.exp(alog))
    a_cs = jnp.dot(tril, dta, precision=lax.Precision.HIGHEST, preferred_element_type=f32)
    a_cs_t = a_cs.T
    bmb = bm.astype(bf16)
    cmb = cm.astype(bf16)
    cb = _mm_nt(cmb, bmb)
    c_st = _mm(cmb, st.astype(bf16))

    def head_col(v, e):
        return jnp.sum(jnp.where(lane == e, v, 0.0), axis=1, keepdims=True)

    def head_row(v, e):
        return jnp.sum(jnp.where(sub == e, v, 0.0), axis=0, keepdims=True)

    ys, sts = [], []
    for j in range(4):
        e0, e1 = 2 * j, 2 * j + 1
        c0, c1 = head_col(a_cs, e0), head_col(a_cs, e1)
        acs_x = jnp.where(lo, c0, c1)
        dt_x = jnp.where(lo, head_col(dt, e0), head_col(dt, e1))
        xd = xs[:, j * LANES:(j + 1) * LANES] * dt_x
        m0 = cb * jnp.exp(jnp.where(causal, c0 - head_row(a_cs_t, e0), NEG_BIG))
        m1 = cb * jnp.exp(jnp.where(causal, c1 - head_row(a_cs_t, e1), NEG_BIG))
        mcat = jnp.concatenate([m0, m1], axis=1).astype(bf16)
        xcat = jnp.concatenate([jnp.where(lo, xd, 0.0), jnp.where(lo, 0.0, xd)], axis=0).astype(bf16)
        y_diag = _mm(mcat, xcat)
        a_last = jnp.sum(jnp.where(last_row, acs_x, 0.0), axis=0, keepdims=True)
        x_dec = (xd * jnp.exp(a_last - acs_x)).astype(bf16)
        s_new = _mm_tn(bmb, x_dec)
        y_off = c_st[:, j * LANES:(j + 1) * LANES] * jnp.exp(acs_x)
        ys.append(y_diag + y_off)
        sts.append(jnp.exp(a_last) * st[:, j * LANES:(j + 1) * LANES] + s_new)
    return jnp.concatenate(ys, axis=1), jnp.concatenate(sts, axis=1)


SCAN_GROUPS_FWD = 4
SCAN_GROUPS_BWD = 1


def _scan_specs(nc, reverse, gs):
    L = SSD_CHUNK
    ch = (lambda c: nc - 1 - c) if reverse else (lambda c: c)
    gw = SSD_INNER // SSD_GROUPS
    b0 = SSD_INNER // (gs * SSD_STATE)
    c0 = (SSD_INNER + SSD_GROUPS * SSD_STATE) // (gs * SSD_STATE)
    xs = pl.BlockSpec((L, gs * gw), lambda g, c: (ch(c), g))
    bm = pl.BlockSpec((L, gs * SSD_STATE), lambda g, c: (ch(c), b0 + g))
    cm = pl.BlockSpec((L, gs * SSD_STATE), lambda g, c: (ch(c), c0 + g))
    dt = pl.BlockSpec((gs, L, LANES), lambda g, c: (g, ch(c), 0))
    alog = pl.BlockSpec((gs, 1, LANES), lambda g, c: (g, 0, 0))
    st = pl.BlockSpec((gs, None, SSD_STATE, gw), lambda g, c: (g, ch(c), 0, 0))
    y = pl.BlockSpec((L, gs * gw), lambda g, c: (ch(c), g))
    grp = pl.BlockSpec((L, gs * SSD_STATE), lambda g, c: (ch(c), g))
    return xs, bm, cm, dt, alog, st, y, grp


def ssd_scan_fwd(act, dt4, alog4, rider=None):
    T = act.shape[0]
    nc = T // SSD_CHUNK
    gs = SCAN_GROUPS_FWD
    ng = SSD_GROUPS // gs
    gw = SSD_INNER // SSD_GROUPS
    xs_s, bm_s, cm_s, dt_s, alog_s, st_s, y_s, _ = _scan_specs(nc, False, gs)
    r_in, r_out, r_shapes, r_scratch, r_args = _rider_specs(rider)

    def body(xs_ref, bm_ref, cm_ref, dt_ref, alog_ref, y_ref, st_ref, st_scr):
        @pl.when(pl.program_id(1) == 0)
        def _():
            st_scr[...] = jnp.zeros_like(st_scr)

        for q in range(gs):
            xc, gc = pl.ds(q * gw, gw), pl.ds(q * SSD_STATE, SSD_STATE)
            st = st_scr[q]
            st_ref[q] = st
            y, st_new = _ssd_chunk(xs_ref[:, xc].astype(f32), bm_ref[:, gc].astype(f32), cm_ref[:, gc].astype(f32), dt_ref[q], alog_ref[q], st)
            y_ref[:, xc] = y.astype(bf16)
            st_scr[q] = st_new

    first = lambda: jnp.logical_and(pl.program_id(0) == 0, pl.program_id(1) == 0)
    last = lambda: jnp.logical_and(pl.program_id(0) == ng - 1, pl.program_id(1) == nc - 1)
    late = lambda: jnp.logical_and(pl.program_id(0) == ng - 1, pl.program_id(1) == (7 * nc) // 8)
    return pl.pallas_call(
        _carry(body, 5, 2, rider, first, last, late), name="ssd_scan_fwd" if rider is None else "ssd_scan_fwd_carrying", grid=(ng, nc),
        in_specs=[xs_s, bm_s, cm_s, dt_s, alog_s] + r_in, out_specs=[y_s, st_s] + r_out,
        out_shape=[jax.ShapeDtypeStruct((T, SSD_INNER), bf16), jax.ShapeDtypeStruct((SSD_GROUPS, nc, SSD_STATE, gw), f32)] + r_shapes,
        scratch_shapes=[pltpu.VMEM((gs, SSD_STATE, gw), f32)] + r_scratch,
        compiler_params=_params(("parallel" if rider is None else "arbitrary", "arbitrary")),
    )(act, act, act, dt4, alog4, *r_args)


def ssd_scan_bwd(act, dt4, alog4, states, dy, rider=None):
    T = act.shape[0]
    nc = T // SSD_CHUNK
    gs = SCAN_GROUPS_BWD
    ng = SSD_GROUPS // gs
    gw = SSD_INNER // SSD_GROUPS
    xs_s, bm_s, cm_s, dt_s, alog_s, st_s, y_s, grp_s = _scan_specs(nc, True, gs)
    r_in, r_out, r_shapes, r_scratch, r_args = _rider_specs(rider)

    def body(xs_ref, bm_ref, cm_ref, dt_ref, alog_ref, st_ref, dy_ref, dxs_ref, db_ref, dc_ref, ddt_ref, dalog_ref, dst_scr):
        @pl.when(pl.program_id(1) == 0)
        def _():
            dst_scr[...] = jnp.zeros_like(dst_scr)
            dalog_ref[...] = jnp.zeros_like(dalog_ref)

        for q in range(gs):
            xc, gc = pl.ds(q * gw, gw), pl.ds(q * SSD_STATE, SSD_STATE)
            _, vjp = jax.vjp(_ssd_chunk, xs_ref[:, xc].astype(f32), bm_ref[:, gc].astype(f32), cm_ref[:, gc].astype(f32),
                             dt_ref[q], alog_ref[q], st_ref[q])
            dxs, dbm, dcm, ddt, dalog, dst = vjp((dy_ref[:, xc].astype(f32), dst_scr[q]))
            dxs_ref[:, xc] = dxs.astype(bf16)
            db_ref[:, gc] = dbm.astype(bf16)
            dc_ref[:, gc] = dcm.astype(bf16)
            ddt_ref[q] = ddt
            dalog_ref[q] += dalog
            dst_scr[q] = dst

    first = lambda: jnp.logical_and(pl.program_id(0) == 0, pl.program_id(1) == 0)
    last = lambda: jnp.logical_and(pl.program_id(0) == ng - 1, pl.program_id(1) == nc - 1)
    return pl.pallas_call(
        _carry(body, 7, 5, rider, first, last), name="ssd_scan_bwd" if rider is None else "ssd_scan_bwd_carrying", grid=(ng, nc),
        in_specs=[xs_s, bm_s, cm_s, dt_s, alog_s, st_s, y_s] + r_in,
        out_specs=[y_s, grp_s, grp_s, dt_s, alog_s] + r_out,
        out_shape=[jax.ShapeDtypeStruct((T, SSD_INNER), bf16), jax.ShapeDtypeStruct((T, SSD_GROUPS * SSD_STATE), bf16),
                   jax.ShapeDtypeStruct((T, SSD_GROUPS * SSD_STATE), bf16), jax.ShapeDtypeStruct((SSD_GROUPS, T, LANES), f32),
                   jax.ShapeDtypeStruct((SSD_GROUPS, 1, LANES), f32)] + r_shapes,
        scratch_shapes=[pltpu.VMEM((gs, SSD_STATE, gw), f32)] + r_scratch,
        compiler_params=_params(("parallel" if rider is None else "arbitrary", "arbitrary")),
    )(act, act, act, dt4, alog4, states, dy, *r_args)


GATE_ROWS = 256


def _ssd_gate(y, xs, z, d_x, nw):
    g = (y + xs * d_x) * (z * _sigmoid(z))
    return g * lax.rsqrt(jnp.mean(g * g, axis=-1, keepdims=True) + RMS_EPS) * nw


def _gate_blocks(tm, fn):
    gw = SSD_INNER // SSD_GROUPS

    def block(r, carry):
        rows = pl.ds(r * GATE_ROWS if isinstance(r, int) else pl.multiple_of(r * GATE_ROWS, GATE_ROWS), GATE_ROWS)
        for k in range(SSD_GROUPS):
            fn(rows, pl.ds(k * gw, gw))
        return carry

    if tm == GATE_ROWS:
        block(0, 0)
    else:
        lax.fori_loop(0, tm // GATE_ROWS, block, 0)


def ssd_gate_fwd(y, act, z, d_x, nw, tm=512):
    T = y.shape[0]

    def body(y_ref, xs_ref, z_ref, d_ref, nw_ref, o_ref):
        def one(rows, cols):
            o_ref[rows, cols] = _ssd_gate(y_ref[rows, cols].astype(f32), xs_ref[rows, cols].astype(f32), z_ref[rows, cols].astype(f32),
                                          d_ref[:, cols], nw_ref[:, cols]).astype(bf16)

        _gate_blocks(tm, one)

    return pl.pallas_call(
        body, name="ssd_gate_fwd", grid=(T // tm,),
        in_specs=[_rows(tm, SSD_INNER), _rows(tm, SSD_INNER), _rows(tm, SSD_INNER), _resident(d_x.shape), _resident(nw.shape)],
        out_specs=_rows(tm, SSD_INNER), out_shape=jax.ShapeDtypeStruct((T, SSD_INNER), bf16),
        compiler_params=_params(("parallel",)),
    )(y, act, z, d_x, nw)


def ssd_gate_bwd(y, act, z, d_x, nw, dxo, w_out, tm=512):
    T = y.shape[0]

    def body(y_ref, xs_ref, z_ref, d_ref, nw_ref, dxo_ref, w_ref, dy_ref, dxs_ref, dz_ref, dd_ref, dnw_ref, dyb_ref):
        @pl.when(pl.program_id(0) == 0)
        def _():
            dd_ref[...] = jnp.zeros_like(dd_ref)
            dnw_ref[...] = jnp.zeros_like(dnw_ref)

        dyb_ref[...] = dxo_ref[...].astype(bf16)

        def one(rows, cols):
            _, vjp = jax.vjp(_ssd_gate, y_ref[rows, cols].astype(f32), xs_ref[rows, cols].astype(f32), z_ref[rows, cols].astype(f32),
                             d_ref[:, cols], nw_ref[:, cols])
            dy, dxs, dz, dd, dnw = vjp(_mm_nt(dyb_ref[rows, :], w_ref[cols, :]))
            dy_ref[rows, cols] = dy.astype(bf16)
            dxs_ref[rows, cols] = dxs.astype(bf16)
            dz_ref[rows, cols] = dz.astype(bf16)
            dd_ref[:, cols] += dd
            dnw_ref[:, cols] += dnw

        _gate_blocks(tm, one)

    const = pl.BlockSpec((1, SSD_INNER), lambda i: (0, 0))
    return pl.pallas_call(
        body, name="ssd_gate_bwd", grid=(T // tm,),
        in_specs=[_rows(tm, SSD_INNER), _rows(tm, SSD_INNER), _rows(tm, SSD_INNER), _resident(d_x.shape), _resident(nw.shape),
                  _rows(tm, D_MODEL), _resident(w_out.shape)],
        out_specs=[_rows(tm, SSD_INNER)] * 3 + [const, const, _rows(tm, D_MODEL)],
        out_shape=[jax.ShapeDtypeStruct((T, SSD_INNER), bf16)] * 3 + [jax.ShapeDtypeStruct((1, SSD_INNER), f32)] * 2
        + [jax.ShapeDtypeStruct((T, D_MODEL), bf16)],
        compiler_params=_params(("arbitrary",)),
    )(y, act, z, d_x, nw, dxo, w_out)


def sc_mid_fwd(bcu, conv_w, tm=512):
    T = bcu.shape[0]
    nt = T // tm
    Dm = D_MODEL

    def body(x_ref, halo_ref, w_ref, q_ref, ext_ref):
        first = pl.program_id(0) == 0
        n_s = Dm // LANES

        def strip(s, carry):
            cols, c_cols, u_cols = _strip(s), _strip(s + n_s), _strip(s + 2 * n_s)
            ext_ref[pl.ds(0, HALO), :] = jnp.where(first, 0.0, halo_ref[:, c_cols].astype(f32) * halo_ref[:, u_cols].astype(f32))
            ext_ref[pl.ds(HALO, tm), :] = x_ref[:, c_cols].astype(f32) * x_ref[:, u_cols].astype(f32)
            for r0 in range(0, tm, ROW_BLOCK):
                rows = pl.ds(r0, ROW_BLOCK)
                v, _ = _conv_rows(ext_ref, w_ref, cols, SC_CONV_K, r0)
                q_ref[rows, cols] = (x_ref[rows, cols].astype(f32) * v).astype(bf16)
            return carry

        lax.fori_loop(0, n_s, strip, 0)

    return pl.pallas_call(
        body, name="sc_mid_fwd", grid=(nt,),
        in_specs=[_rows(tm, 3 * Dm), _halo_spec(tm, 3 * Dm, nt, False), _resident(conv_w.shape)],
        out_specs=_rows(tm, Dm), out_shape=jax.ShapeDtypeStruct((T, Dm), bf16),
        scratch_shapes=[pltpu.VMEM((tm + HALO, LANES), f32)],
        compiler_params=_params(("parallel",)),
    )(bcu, bcu, conv_w)


def sc_mid_bwd(bcu, conv_w, dxo, w_out, tm=512):
    T = bcu.shape[0]
    nt = T // tm
    Dm = D_MODEL
    K = SC_CONV_K

    def body(x_ref, halo_ref, w_ref, dxo_ref, wo_ref, dx_ref, dw_ref, dyb_ref, ext_ref, dv_ref, carry_ref, dq_ref):
        i = pl.program_id(0)

        @pl.when(i == 0)
        def _():
            carry_ref[...] = jnp.zeros_like(carry_ref)
            dw_ref[...] = jnp.zeros_like(dw_ref)

        dyb = dxo_ref[...].astype(bf16)
        dyb_ref[...] = dyb
        dq_ref[...] = _mm_nt(dyb, wo_ref[...])
        first_tile = i == nt - 1
        n_s = Dm // LANES

        def strip(s, carry):
            cols, c_cols, u_cols = _strip(s), _strip(s + n_s), _strip(s + 2 * n_s)
            ext_ref[pl.ds(0, HALO), :] = jnp.where(first_tile, 0.0, halo_ref[:, c_cols].astype(f32) * halo_ref[:, u_cols].astype(f32))
            ext_ref[pl.ds(HALO, tm), :] = x_ref[:, c_cols].astype(f32) * x_ref[:, u_cols].astype(f32)
            dv_ref[pl.ds(tm, 8), :] = carry_ref[:, cols]
            dws = [jnp.zeros((1, LANES), f32) for _ in range(K)]
            for r0 in range(0, tm, ROW_BLOCK):
                rows = pl.ds(r0, ROW_BLOCK)
                v, wins = _conv_rows(ext_ref, w_ref, cols, K, r0)
                dqv = dq_ref[rows, cols]
                dv = dqv * x_ref[rows, cols].astype(f32)
                dv_ref[rows, :] = dv
                dx_ref[rows, cols] = (dqv * v).astype(bf16)
                for k in range(K):
                    dws[k] = dws[k] + jnp.sum(dv * wins[k], axis=0, keepdims=True)
            carry_ref[:, cols] = dv_ref[pl.ds(0, 8), :]
            for r0 in range(0, tm, ROW_BLOCK):
                rows = pl.ds(r0, ROW_BLOCK)
                dp = _shifted_back(dv_ref, w_ref, cols, K, r0)
                dx_ref[rows, c_cols] = (dp * x_ref[rows, u_cols].astype(f32)).astype(bf16)
                dx_ref[rows, u_cols] = (dp * x_ref[rows, c_cols].astype(f32)).astype(bf16)
            for k in range(K):
                dw_ref[pl.ds(k, 1), cols] += dws[k]
            return carry

        lax.fori_loop(0, n_s, strip, 0)

    return pl.pallas_call(
        body, name="sc_mid_bwd", grid=(nt,),
        in_specs=[_tile_spec(tm, 3 * Dm, nt, True), _halo_spec(tm, 3 * Dm, nt, True), _resident(conv_w.shape), _tile_spec(tm, Dm, nt, True),
                  _resident(w_out.shape)],
        out_specs=[_tile_spec(tm, 3 * Dm, nt, True), pl.BlockSpec((8, Dm), lambda i: (0, 0)), _tile_spec(tm, Dm, nt, True)],
        out_shape=[jax.ShapeDtypeStruct((T, 3 * Dm), bf16), jax.ShapeDtypeStruct((8, Dm), f32), jax.ShapeDtypeStruct((T, Dm), bf16)],
        scratch_shapes=[pltpu.VMEM((tm + HALO, LANES), f32), pltpu.VMEM((tm + 8, LANES), f32), pltpu.VMEM((8, Dm), f32),
                        pltpu.VMEM((tm, Dm), f32)],
        compiler_params=_params(("arbitrary",)),
    )(bcu, bcu, conv_w, dxo, w_out)


def loss_head(x, fw, target, tm=1024):
    T = x.shape[0]

    def body(x_ref, fw_ref, t_ref, loss_ref, dx_ref, dfw_ref):
        @pl.when(pl.program_id(0) == 0)
        def _():
            loss_ref[...] = jnp.zeros_like(loss_ref)
            dfw_ref[...] = jnp.zeros_like(dfw_ref)

        w = fw_ref[...]
        y, xh, inv = _rms_fwd(x_ref[...], w)
        err = y - t_ref[...]
        loss_ref[...] += 0.5 * jnp.sum(jnp.mean(err * err, axis=-1, keepdims=True), axis=0, keepdims=True)
        dx, dw = _rms_bwd(err * (1.0 / D_MODEL), xh, inv, w)
        dx_ref[...] = dx
        dfw_ref[...] += dw

    return pl.pallas_call(
        body, name="loss_head", grid=(T // tm,),
        in_specs=[_rows(tm, D_MODEL), _resident((1, D_MODEL)), _rows(tm, D_MODEL)],
        out_specs=[pl.BlockSpec((1, LANES), lambda i: (0, 0)), _rows(tm, D_MODEL), pl.BlockSpec((1, D_MODEL), lambda i: (0, 0))],
        out_shape=[jax.ShapeDtypeStruct((1, LANES), f32), jax.ShapeDtypeStruct((T, D_MODEL), f32), jax.ShapeDtypeStruct((1, D_MODEL), f32)],
        compiler_params=_params(("arbitrary",)),
    )(x, fw, target)


ELEMENTWISE_TILE_BYTES = 1_600_000


def _row_tile(rows, width):
    row_bytes = 4 * _round_up(width, LANES)
    tile = rows
    while tile * row_bytes > ELEMENTWISE_TILE_BYTES and tile % 16 == 0:
        tile //= 2
    return tile


def adamw(g_parts, w, m, v, name="adamw", a0=0, prev=None):
    A, B, n = w.shape
    tb = _row_tile(B, n)
    n_parts = len(g_parts)
    arrays, specs = [], []
    for part in g_parts:
        lead, arr = part if isinstance(part, tuple) else ((), part)
        specs.append(pl.BlockSpec((None,) * (len(lead) + 1) + (tb, n), lambda a, t, lead=lead: tuple(lead) + (a, t, 0)))
        arrays.append(arr)
    na = arrays[0].shape[-3]
    prev = list(prev) if prev is not None else []

    def body(*refs):
        n = n_parts
        g_refs = refs[:n]
        w_ref, m_ref, v_ref = refs[n:n + 3]
        go_ref, d_ref, mo_ref, vo_ref = refs[n + 3 + len(prev):]
        g = g_refs[0][...].astype(f32)
        for r in g_refs[1:]:
            g = g + r[...].astype(f32)
        m_new = ADAM_B1 * m_ref[...] + (1.0 - ADAM_B1) * g
        v_new = ADAM_B2 * v_ref[...] + (1.0 - ADAM_B2) * (g * g)
        m_hat = m_new / (1.0 - ADAM_B1 ** ADAM_STEP)
        v_hat = v_new / (1.0 - ADAM_B2 ** ADAM_STEP)
        go_ref[...] = g
        d_ref[...] = -ADAM_LR * (m_hat / (jnp.sqrt(v_hat) + ADAM_EPS) + ADAM_WD * w_ref[...])
        mo_ref[...] = m_new
        vo_ref[...] = v_new

    plain = pl.BlockSpec((None, tb, n), lambda a, t: (a + a0, t, 0))
    return pl.pallas_call(
        body, name=name, grid=(na, B // tb), in_specs=specs + [plain] * 3 + [_ANY] * len(prev), out_specs=[plain] * 4,
        out_shape=[jax.ShapeDtypeStruct((A, B, n), f32)] * 4,
        input_output_aliases={n_parts + 3 + k: k for k in range(len(prev))},
        compiler_params=_params(("parallel", "parallel")),
    )(*arrays, w, m, v, *prev)


def pair_sum_bf16(ga, gb, name):
    _, A, B, n = gb.shape
    tb = _row_tile(B, n)

    def body(a_ref, b_ref, o_ref):
        o_ref[...] = (a_ref[...] + b_ref[...]).astype(bf16)

    return pl.pallas_call(
        body, name=name, grid=(3, A, B // tb),
        in_specs=[pl.BlockSpec((None, None, None, tb, n), lambda j, a, t: (0, j + 1, a, t, 0)),
                  pl.BlockSpec((None, None, tb, n), lambda j, a, t: (j + 1, a, t, 0))],
        out_specs=pl.BlockSpec((None, None, tb, n), lambda j, a, t: (j + 1, a, t, 0)),
        out_shape=jax.ShapeDtypeStruct((4, A, B, n), bf16),
        compiler_params=_params(("parallel", "parallel", "parallel")),
    )(ga, gb)


def assemble(gathered, axis, tk=256):
    _, A, K, n = gathered.shape
    if axis == 1:
        def body(w_ref, o_ref):
            o_ref[...] = jnp.concatenate([w_ref[j] for j in range(N_DEV)], axis=1)

        return pl.pallas_call(
            body, name=f"assemble_cols_{K}x{n}", grid=(A, K // tk),
            in_specs=[pl.BlockSpec((N_DEV, None, tk, n), lambda a, t: (0, a, t, 0))],
            out_specs=pl.BlockSpec((None, tk, N_DEV * n), lambda a, t: (a, t, 0)),
            out_shape=jax.ShapeDtypeStruct((A, K, N_DEV * n), gathered.dtype),
            compiler_params=_params(("parallel", "parallel")),
        )(gathered)

    def body(w_ref, o_ref):
        for j in range(N_DEV):
            o_ref[pl.ds(j * K, K), :] = w_ref[j]

    return pl.pallas_call(
        body, name=f"assemble_rows_{K}x{n}", grid=(A,),
        in_specs=[pl.BlockSpec((N_DEV, None, K, n), lambda a: (0, a, 0, 0))],
        out_specs=pl.BlockSpec((None, N_DEV * K, n), lambda a: (a, 0, 0)),
        out_shape=jax.ShapeDtypeStruct((A, N_DEV * K, n), gathered.dtype),
        compiler_params=_params(("parallel",)),
    )(gathered)


SSD_IN_PAD = -(-SSD_IN_DIM // LANES) * LANES


def assemble_ssd_in(gathered, tk=256):
    _, A, K, n = gathered.shape

    def body(w_ref, z_ref, x_ref, dt_ref, full_ref):
        full_ref[:, pl.ds(SSD_IN_PAD - LANES, LANES)] = jnp.zeros((tk, LANES), gathered.dtype)
        for j in range(N_DEV):
            full_ref[:, pl.ds(j * n, n)] = w_ref[j]
        z_ref[...] = full_ref[:, pl.ds(0, SSD_INNER)]
        x_ref[...] = full_ref[:, pl.ds(SSD_INNER, SSD_CONV_DIM)]
        dt_ref[...] = full_ref[:, pl.ds(SSD_INNER + SSD_CONV_DIM, LANES)]

    widths = (SSD_INNER, SSD_CONV_DIM, LANES)
    return pl.pallas_call(
        body, name="assemble_ssd_in", grid=(A, K // tk),
        in_specs=[pl.BlockSpec((N_DEV, None, tk, n), lambda a, t: (0, a, t, 0))],
        out_specs=[pl.BlockSpec((None, tk, w), lambda a, t: (a, t, 0)) for w in widths],
        out_shape=[jax.ShapeDtypeStruct((A, K, w), gathered.dtype) for w in widths],
        scratch_shapes=[pltpu.VMEM((tk, SSD_IN_PAD), gathered.dtype)],
        compiler_params=_params(("parallel", "parallel")),
    )(gathered)


def ssd_in_to_shards(dwz, dwx, dwdt, buf, j, tk=256):
    K = dwz.shape[0]
    n = buf.shape[-1]
    fresh = isinstance(buf, jax.ShapeDtypeStruct)

    def body(z_ref, x_ref, dt_ref, *rest):
        o_ref, full_ref = rest[-2:]
        full_ref[:, pl.ds(0, SSD_INNER)] = z_ref[...]
        full_ref[:, pl.ds(SSD_INNER, SSD_CONV_DIM)] = x_ref[...]
        full_ref[:, pl.ds(SSD_INNER + SSD_CONV_DIM, LANES)] = dt_ref[...]
        my_c, my_chip = _my_core_and_chip()
        for d in range(N_DEV):
            o_ref[(d % 2) ^ my_c, (d // 2) ^ my_chip] = full_ref[:, pl.ds(d * n, n)]

    return pl.pallas_call(
        body, name="ssd_in_to_shards", grid=(K // tk,),
        in_specs=[_rows(tk, SSD_INNER), _rows(tk, SSD_CONV_DIM), _rows(tk, LANES)] + ([] if fresh else [_ANY]),
        out_specs=pl.BlockSpec((2, 4, None, tk, n), lambda t: (0, 0, j, t, 0)),
        out_shape=jax.ShapeDtypeStruct(buf.shape, f32),
        scratch_shapes=[pltpu.VMEM((tk, SSD_IN_PAD), f32)],
        input_output_aliases={} if fresh else {3: 0},
        compiler_params=_params(("parallel",)),
    )(dwz, dwx, dwdt, *([] if fresh else [buf]))


def sum_over_devices(gathered):
    _, R, W = gathered.shape

    def body(g_ref, o_ref):
        acc = g_ref[0]
        for k in range(1, N_DEV):
            acc = acc + g_ref[k]
        o_ref[...] = acc

    return pl.pallas_call(
        body, name="sum_over_devices", grid=(1,),
        in_specs=[pl.BlockSpec((N_DEV, R, W), lambda i: (0, 0, 0))], out_specs=pl.BlockSpec((R, W), lambda i: (0, 0)),
        out_shape=jax.ShapeDtypeStruct((R, W), f32), compiler_params=_params(("arbitrary",)),
    )(gathered)


_ANY = pl.BlockSpec(memory_space=pl.ANY)


class _Exchange:
    def __init__(self, inputs, out_shapes, scratch, start, finish, relay=None):
        self.inputs, self.out_shapes, self.scratch, self.start, self.finish = inputs, out_shapes, scratch, start, finish
        self.relay = relay

    def run(self, name):
        ni, no = len(self.inputs), len(self.out_shapes)

        def body(*refs):
            parts = (refs[:ni], refs[ni:ni + no], refs[ni + no:])
            self.start(*parts)
            if self.relay is not None:
                self.relay(*parts)
            self.finish(*parts)

        return pl.pallas_call(body, name=name, in_specs=[_ANY] * ni, out_specs=[_ANY] * no, out_shape=self.out_shapes,
                              scratch_shapes=self.scratch)(*self.inputs)


def _carry(body, n_in, n_out, rider, first, last, late=None):
    if rider is None:
        return body
    ri, ro = len(rider.inputs), len(rider.out_shapes)

    def hosted(*refs):
        a, b, c = n_in + ri, n_in + ri + n_out, n_in + ri + n_out + ro
        rs = len(refs) - c - len(rider.scratch)
        parts = (refs[n_in:a], refs[b:c], refs[c + rs:])

        @pl.when(first())
        def _():
            rider.start(*parts)

        if rider.relay is not None and late is not None:
            @pl.when(late())
            def _():
                rider.relay(*parts)

        body(*refs[:n_in], *refs[a:b], *refs[c:c + rs])

        @pl.when(last())
        def _():
            if rider.relay is not None and late is None:
                rider.relay(*parts)
            rider.finish(*parts)

    return hosted


def _rider_specs(rider):
    if rider is None:
        return [], [], [], [], []
    return [_ANY] * len(rider.inputs), [_ANY] * len(rider.out_shapes), list(rider.out_shapes), list(rider.scratch), list(rider.inputs)


def all_gather(blocks):
    n = len(blocks)

    def plan(x_refs, out_refs, sems):
        send_sems, recv_sems, local_sems = sems
        x, y, c = lax.axis_index("x"), lax.axis_index("y"), lax.axis_index("c")
        me, sibling = (x, y, c), (x, y, 1 - c)
        chips = [(1 - x, y), (x, 1 - y), (1 - x, 1 - y)]

        def copy(a, k, blk, to, src=None):
            px, py, pc = blk
            slot = out_refs[a].at[4 * px + 2 * py + pc]
            return pltpu.make_async_remote_copy(
                src_ref=slot if src is None else src, dst_ref=slot,
                send_sem=send_sems.at[7 * a + k], recv_sem=recv_sems.at[7 * a + k], device_id=to, device_id_type=MESH)

        mine = [pltpu.make_async_copy(x_refs[a], out_refs[a].at[4 * x + 2 * y + c], local_sems.at[a]) for a in range(n)]
        first = []
        for a in range(n):
            first += [copy(a, 0, me, sibling, src=x_refs[a])] + [copy(a, 1 + j, me, (*chip, c), src=x_refs[a]) for j, chip in enumerate(chips)]
        return c, me, sibling, chips, copy, mine, first

    def start(x_refs, out_refs, sems):
        _, _, _, _, _, mine, first = plan(x_refs, out_refs, sems)
        for cp in mine + first:
            cp.start()

    def relay(x_refs, out_refs, sems):
        c, me, sibling, chips, copy, _, _ = plan(x_refs, out_refs, sems)
        for j, chip in enumerate(chips):
            for a in range(n):
                copy(a, 1 + j, (*chip, c), me).wait_recv()
                copy(a, 4 + j, (*chip, c), sibling).start()

    def finish(x_refs, out_refs, sems):
        c, me, sibling, chips, copy, mine, first = plan(x_refs, out_refs, sems)
        passed = [copy(a, 4 + j, (*chip, c), sibling) for j, chip in enumerate(chips) for a in range(n)]
        for a in range(n):
            copy(a, 0, sibling, me).wait_recv()
            for j, chip in enumerate(chips):
                copy(a, 4 + j, (*chip, 1 - c), me).wait_recv()
        for cp in first + passed:
            cp.wait_send()
        for cp in mine:
            cp.wait()

    return _Exchange(list(blocks), [jax.ShapeDtypeStruct((N_DEV,) + b.shape, b.dtype) for b in blocks],
                     [pltpu.SemaphoreType.DMA((7 * n,)), pltpu.SemaphoreType.DMA((7 * n,)), pltpu.SemaphoreType.DMA((n,))], start, finish,
                     relay=relay)


def exchange_with_sibling(gs):
    n = len(gs)

    def plan(g_refs, recv_refs, sems):
        send_sems, recv_sems = sems
        x, y, c = lax.axis_index("x"), lax.axis_index("y"), lax.axis_index("c")
        return [pltpu.make_async_remote_copy(src_ref=g_refs[a].at[1], dst_ref=recv_refs[a], send_sem=send_sems.at[a],
                                             recv_sem=recv_sems.at[a], device_id=(x, y, 1 - c), device_id_type=MESH) for a in range(n)]

    def start(*refs):
        for cp in plan(*refs):
            cp.start()

    def finish(*refs):
        for cp in plan(*refs):
            cp.wait()

    return _Exchange(list(gs), [jax.ShapeDtypeStruct(g.shape[1:], g.dtype) for g in gs],
                     [pltpu.SemaphoreType.DMA((n,)), pltpu.SemaphoreType.DMA((n,))], start, finish)


def exchange_between_chips(parts):
    n = len(parts)

    def plan(p_refs, recv_refs, sems):
        send_sems, recv_sems = sems
        x, y, c = lax.axis_index("x"), lax.axis_index("y"), lax.axis_index("c")
        chips = [(2, (1 - x, y)), (1, (x, 1 - y)), (3, (1 - x, 1 - y))]
        return [pltpu.make_async_remote_copy(src_ref=p_refs[a].at[slot], dst_ref=recv_refs[a].at[k], send_sem=send_sems.at[3 * a + k],
                                             recv_sem=recv_sems.at[3 * a + k], device_id=(px, py, c), device_id_type=MESH)
                for a in range(n) for k, (slot, (px, py)) in enumerate(chips)]

    def start(*refs):
        for cp in plan(*refs):
            cp.start()

    def finish(*refs):
        for cp in plan(*refs):
            cp.wait()

    return _Exchange(list(parts), [jax.ShapeDtypeStruct((3,) + p.shape[1:], p.dtype) for p in parts],
                     [pltpu.SemaphoreType.DMA((3 * n,)), pltpu.SemaphoreType.DMA((3 * n,))], start, finish)


PARAMS = {
    "norm_w": ((DEPTH, 3, D_MODEL), 2),
    "ffn_w_gate": ((DEPTH, 2, D_MODEL, D_FF), 3),
    "ffn_w_up": ((DEPTH, 2, D_MODEL, D_FF), 3),
    "ffn_w_down": ((DEPTH, 2, D_FF, D_MODEL), 2),
    "ssd_w_in": ((2, D_MODEL, SSD_IN_DIM), 2),
    "ssd_conv_w": ((2, SSD_CONV_K, SSD_CONV_DIM), 2),
    "ssd_conv_b": ((2, SSD_CONV_DIM), None),
    "ssd_dt_bias": ((2, SSD_HEADS), None),
    "ssd_a_log": ((2, SSD_HEADS), None),
    "ssd_d": ((2, SSD_HEADS), None),
    "ssd_norm_w": ((2, SSD_INNER), None),
    "ssd_w_out": ((2, SSD_INNER, D_MODEL), 1),
    "sc_w_in": ((2, D_MODEL, 3 * D_MODEL), 2),
    "sc_conv_w": ((2, SC_CONV_K, D_MODEL), 2),
    "sc_w_out": ((2, D_MODEL, D_MODEL), 1),
    "final_norm_w": ((D_MODEL,), None),
}
NAMES = list(PARAMS)
BIG = ["ffn_w_gate", "ffn_w_up", "ffn_w_down", "ssd_w_in", "ssd_w_out", "sc_w_in", "sc_w_out"]
SMALL = [n for n in NAMES if n not in BIG]
SMALL_SHARDED = [n for n in SMALL if PARAMS[n][1] is not None]


def _round_up(n, m):
    return -(-n // m) * m


def _pack(flat_list, rows_multiple):
    flat = jnp.concatenate(flat_list)
    rows = _round_up(_round_up(flat.shape[0], PACK_W) // PACK_W, rows_multiple)
    return jnp.pad(flat, (0, rows * PACK_W - flat.shape[0])).reshape(rows, PACK_W)


def _unpack(packed, shapes, lead=()):
    flat = packed.reshape(lead + (-1,))
    out, off = [], 0
    for shp in shapes:
        n = 1
        for s in shp:
            n *= s
        out.append(flat[..., off:off + n].reshape(lead + tuple(shp)))
        off += n
    return out


def _local_shape(name):
    shp, ax = PARAMS[name]
    if ax is None:
        return shp
    return shp[:ax] + (shp[ax] // N_DEV,) + shp[ax + 1:]


def _full_from_gathered(g, name):
    shp, ax = PARAMS[name]
    return jnp.moveaxis(g, 0, ax).reshape(shp)


def _by_destination(full, name):
    shp, ax = PARAMS[name]
    loc = shp[ax] // N_DEV
    return jnp.moveaxis(full.reshape(shp[:ax] + (N_DEV, loc) + shp[ax + 1:]), ax, 0)


def _ssd_layer_fwd(xin, nw, p, rider=None):
    z, xbc, dt_raw = in_proj_fwd(xin, nw, [p["ssd_wz"], p["ssd_wx"], p["ssd_wdt"]], [bf16, bf16, f32])
    act, dt4 = ssd_conv_fwd(xbc, p["ssd_conv_w"], p["ssd_conv_b"], dt_raw, p["ssd_dt_bias"])
    y, states, *got = ssd_scan_fwd(act, dt4, p["ssd_alog4"], rider=rider)
    gn = ssd_gate_fwd(y, act, z, p["ssd_dx"], p["ssd_norm_w"])
    xout = out_proj_fwd(xin, gn, p["ssd_w_out"])
    return xout, (xin, z, xbc, dt_raw, act, dt4, y, states, gn), got


def _ssd_layer_bwd(dxo, nw, p, saved, gbuf, slab, rider=None):
    xin, z, xbc, dt_raw, act, dt4, y, states, gn = saved
    T = xin.shape[0]
    dy, dxs_skip, dz, dd_x, dgnw, dyb = ssd_gate_bwd(y, act, z, p["ssd_dx"], p["ssd_norm_w"], dxo, p["ssd_w_out"])
    gbuf["ssd_w_out"] = tn_matmul_to_shards(gn, dyb, gbuf["ssd_w_out"], (slab,), 0)
    g = {}
    g["ssd_norm_w"] = dgnw[0]
    g["ssd_d"] = jnp.sum(dd_x.reshape(SSD_HEADS, SSD_HEAD_DIM), axis=1)
    dxs, db, dc, ddt4, dalog4, *got = ssd_scan_bwd(act, dt4, p["ssd_alog4"], states, dy, rider=rider)
    g["ssd_a_log"] = dalog4[:, 0, :8].reshape(SSD_HEADS)
    dxbc, ddt_raw, dcw, dcb, ddtb = ssd_conv_bwd(xbc, p["ssd_conv_w"], p["ssd_conv_b"], dt_raw, p["ssd_dt_bias"], dxs, dxs_skip, db, dc, ddt4)
    g["ssd_conv_w"] = dcw[:SSD_CONV_K]
    g["ssd_conv_b"] = dcb[0]
    g["ssd_dt_bias"] = ddtb[0, :SSD_HEADS]
    dx, h, dnw = in_proj_bwd(xin, nw, dxo, [dz, dxbc, ddt_raw], [p["ssd_wz"], p["ssd_wx"], p["ssd_wdt"]])
    gbuf["ssd_w_in"] = ssd_in_to_shards(tn_matmul(h, dz), tn_matmul(h, dxbc), tn_matmul(h, ddt_raw), gbuf["ssd_w_in"], slab)
    return dx, dnw, g, got


def _sc_layer_fwd(xin, nw, p):
    (bcu,) = in_proj_fwd(xin, nw, [p["sc_w_in"]], [bf16])
    q = sc_mid_fwd(bcu, p["sc_conv_w"])
    return out_proj_fwd(xin, q, p["sc_w_out"]), (xin, bcu, q)


def _sc_layer_bwd(dxo, nw, p, saved, gbuf, slab):
    xin, bcu, q = saved
    dbcu, dcw, dyb = sc_mid_bwd(bcu, p["sc_conv_w"], dxo, p["sc_w_out"])
    gbuf["sc_w_out"] = tn_matmul_to_shards(q, dyb, gbuf["sc_w_out"], (slab,), 0)
    g = {"sc_conv_w": dcw[:SC_CONV_K]}
    dx, h, dnw = in_proj_bwd(xin, nw, dxo, [dbcu], [p["sc_w_in"]])
    gbuf["sc_w_in"] = tn_matmul_to_shards(h, dbcu, gbuf["sc_w_in"], (slab,), 1)
    return dx, dnw, g


def kernel(x, norm_w, ffn_w_gate, ffn_w_up, ffn_w_down, ssd_w_in, ssd_conv_w, ssd_conv_b, ssd_dt_bias, ssd_a_log, ssd_d, ssd_norm_w, ssd_w_out, sc_w_in, sc_conv_w, sc_w_out, final_norm_w, loss_target, m_norm_w, m_ffn_w_gate, m_ffn_w_up, m_ffn_w_down, m_ssd_w_in, m_ssd_conv_w, m_ssd_conv_b, m_ssd_dt_bias, m_ssd_a_log, m_ssd_d, m_ssd_norm_w, m_ssd_w_out, m_sc_w_in, m_sc_conv_w, m_sc_w_out, m_final_norm_w, v_norm_w, v_ffn_w_gate, v_ffn_w_up, v_ffn_w_down, v_ssd_w_in, v_ssd_conv_w, v_ssd_conv_b, v_ssd_dt_bias, v_ssd_a_log, v_ssd_d, v_ssd_norm_w, v_ssd_w_out, v_sc_w_in, v_sc_conv_w, v_sc_w_out, v_final_norm_w):
    w_loc = dict(zip(NAMES, (norm_w, ffn_w_gate, ffn_w_up, ffn_w_down, ssd_w_in, ssd_conv_w, ssd_conv_b, ssd_dt_bias, ssd_a_log, ssd_d, ssd_norm_w, ssd_w_out, sc_w_in, sc_conv_w, sc_w_out, final_norm_w)))
    m_loc = dict(zip(NAMES, (m_norm_w, m_ffn_w_gate, m_ffn_w_up, m_ffn_w_down, m_ssd_w_in, m_ssd_conv_w, m_ssd_conv_b, m_ssd_dt_bias, m_ssd_a_log, m_ssd_d, m_ssd_norm_w, m_ssd_w_out, m_sc_w_in, m_sc_conv_w, m_sc_w_out, m_final_norm_w)))
    v_loc = dict(zip(NAMES, (v_norm_w, v_ffn_w_gate, v_ffn_w_up, v_ffn_w_down, v_ssd_w_in, v_ssd_conv_w, v_ssd_conv_b, v_ssd_dt_bias, v_ssd_a_log, v_ssd_d, v_ssd_norm_w, v_ssd_w_out, v_sc_w_in, v_sc_conv_w, v_sc_w_out, v_final_norm_w)))
    my_dev = 4 * lax.axis_index("x") + 2 * lax.axis_index("y") + lax.axis_index("c")

    def as3d(a):
        return a.reshape((-1,) + a.shape[-2:])

    wb = {n: as3d(w_loc[n]).astype(bf16) for n in BIG}

    FFN = ["ffn_w_gate", "ffn_w_up", "ffn_w_down"]

    def mixer_names(i):
        return ["ssd_w_in", "ssd_w_out"] if i % 2 == 0 else ["sc_w_in", "sc_w_out"]

    ag_sets = [[(n, 0, 1) for n in FFN], [(n, 1, 1) for n in FFN] + [(n, 0, 1) for n in mixer_names(0)]]
    ag_sets += [[(n, 2 * r, 2) for n in FFN] + [(n, r // 2, 1) for n in mixer_names(r)] for r in (1, 2, 3)]

    def set_blocks(spec):
        return [wb[n][a0:a0 + na] for n, a0, na in spec]

    def set_weights(spec, gathered):
        q = {}
        for (n, _, _), g in zip(spec, gathered):
            if n == "ssd_w_in":
                q["ssd_wz"], q["ssd_wx"], q["ssd_wdt"] = assemble_ssd_in(g)
            else:
                q[n] = assemble(g, 1 if PARAMS[n][1] == len(PARAMS[n][0]) - 1 else 0)
        return q

    ss_shapes = [_local_shape(n) for n in SMALL_SHARDED]
    gathered0 = all_gather(set_blocks(ag_sets[0]) + [_pack([w_loc[n].reshape(-1) for n in SMALL_SHARDED], 8)]).run("all_gather_first")
    full = {}
    for n, part in zip(SMALL_SHARDED, _unpack(gathered0[-1], ss_shapes, lead=(N_DEV,))):
        full[n] = _full_from_gathered(part, n)
    for n in SMALL:
        if PARAMS[n][1] is None:
            full[n] = w_loc[n]
    small = {
        "ssd_conv_w": full["ssd_conv_w"],
        "ssd_conv_b": full["ssd_conv_b"].reshape(2, 1, SSD_CONV_DIM),
        "ssd_dt_bias": jnp.pad(full["ssd_dt_bias"], ((0, 0), (0, LANES - SSD_HEADS))).reshape(2, 1, LANES),
        "ssd_alog4": jnp.pad(full["ssd_a_log"].reshape(2, SSD_GROUPS, 1, 8), ((0, 0), (0, 0), (0, 0), (0, LANES - 8))),
        "ssd_dx": jnp.repeat(full["ssd_d"], SSD_HEAD_DIM, axis=1).reshape(2, 1, SSD_INNER),
        "ssd_norm_w": full["ssd_norm_w"].reshape(2, 1, SSD_INNER),
        "sc_conv_w": full["sc_conv_w"],
    }
    nw_all = full["norm_w"].reshape(DEPTH, 3, 1, D_MODEL)

    ffn_w = [[None, None] for _ in range(DEPTH)]
    mix_w = [None] * DEPTH

    def arrived(s, gathered):
        q = set_weights(ag_sets[s], gathered)
        ffn = tuple(q[n] for n in FFN)
        if s == 0:
            ffn_w[0][0] = ffn + ((0,),)
            return
        i = 0 if s == 1 else s - 1
        if s == 1:
            ffn_w[0][1] = ffn + ((0,),)
        else:
            ffn_w[i] = [ffn + ((0,),), ffn + ((1,),)]
        m = {n: v[0] for n, v in q.items() if n not in FFN}
        m.update({n: v[i // 2] for n, v in small.items() if n.startswith("ssd" if i % 2 == 0 else "sc")})
        mix_w[i] = m

    def rider_for(s):
        return all_gather(set_blocks(ag_sets[s]))

    xc = x[0]
    saved = []
    arrived(0, gathered0[:-1])
    for i in range(DEPTH):
        carried = {0: (1, 2, 3), 1: (4, None, None)}.get(i, (None, None, None))
        wg, wu, wd, idx = ffn_w[i][0]
        x1, g1, u1, a1, *got = ffn_fwd(xc, nw_all[i, 0], wg, wu, wd, idx, rider=rider_for(carried[0]) if carried[0] else None)
        if carried[0]:
            arrived(carried[0], got)
        if i % 2 == 0:
            x2, mix_saved, got = _ssd_layer_fwd(x1, nw_all[i, 1], mix_w[i], rider=rider_for(carried[1]) if carried[1] else None)
            if carried[1]:
                arrived(carried[1], got)
        else:
            x2, mix_saved = _sc_layer_fwd(x1, nw_all[i, 1], mix_w[i])
        wg, wu, wd, idx = ffn_w[i][1]
        x3, g3, u3, a3, *got = ffn_fwd(x2, nw_all[i, 2], wg, wu, wd, idx, rider=rider_for(carried[2]) if carried[2] else None)
        if carried[2]:
            arrived(carried[2], got)
        saved.append(((xc, g1, u1, a1), mix_saved, (x2, g3, u3, a3)))
        xc = x3

    loss_row, dx, dfw = loss_head(xc, full["final_norm_w"].reshape(1, D_MODEL), loss_target[0])
    loss = lax.psum(loss_row[0, 0], ("x", "y", "c"))

    grads = {n: [None] * PARAMS[n][0][0] for n in SMALL if n != "final_norm_w"}
    grads["final_norm_w"] = dfw[0]
    dnorm = [[None] * 3 for _ in range(DEPTH)]
    def slabs(n, which):
        if n.startswith("ffn"):
            return {"early": (2, 6), "mid": (1, 1), "last": (0, 1)}[which]
        if n.startswith("ssd"):
            return {"early": (1, 1), "mid": (0, 1), "last": (0, 0)}[which]
        return {"early": (0, 2), "mid": (0, 0), "last": (0, 0)}[which]

    TRANSPOSED = ("ffn_w_gate", "ffn_w_up")

    def shard3d(a, n):
        return jnp.swapaxes(as3d(a), 1, 2) if n in TRANSPOSED else as3d(a)

    gb = {which: {n: jax.ShapeDtypeStruct((2, 4, slabs(n, which)[1]) + shard3d(wb[n], n).shape[1:], f32) for n in BIG if slabs(n, which)[1]}
          for which in ("early", "mid", "last")}

    def ffn_back(i, k, dxo, sv, rider=None):
        xin, g_, u_, a_ = sv
        which = "early" if i > 0 else ("mid" if k == 1 else "last")
        gbuf = gb[which]
        slab = 2 * i + k - slabs("ffn_w_gate", which)[0]
        wg, wu, wd, idx = ffn_w[i][k]
        dxi, h, dyb, dg, du, dnw, *got = ffn_bwd_dx(xin, dxo, g_, u_, nw_all[i, 2 * k], wg, wu, wd, idx, rider=rider)
        dnorm[i][2 * k] = dnw[0]
        gbuf["ffn_w_gate"] = tn_matmul_to_shards(dg, h, gbuf["ffn_w_gate"], (slab,), 0)
        gbuf["ffn_w_up"] = tn_matmul_to_shards(du, h, gbuf["ffn_w_up"], (slab,), 0)
        gbuf["ffn_w_down"] = tn_matmul_to_shards(a_, dyb, gbuf["ffn_w_down"], (slab,), 0)
        return dxi, got

    def reduce_in_chip(gbuf, from_sibling=None):
        names = list(gbuf)
        bufs = [gbuf[n] for n in names]
        if from_sibling is None:
            from_sibling = exchange_with_sibling(bufs).run("exchange_with_sibling")
        return names, bufs, from_sibling, [pair_sum_bf16(g, fs, "pair_sum_" + n) for n, g, fs in zip(names, bufs, from_sibling)]

    reduced, from_chips = {}, {}
    for i in reversed(range(DEPTH)):
        j = i // 2
        sv_a, sv_mix, sv_b = saved[i]
        if i == 0:
            dx, got = ffn_back(i, 1, dx, sv_b, rider=exchange_with_sibling(list(gb["early"].values())))
            reduced["early"] = reduce_in_chip(gb["early"], from_sibling=got)
        else:
            dx, _ = ffn_back(i, 1, dx, sv_b)
        if i % 2 == 0:
            rider = exchange_between_chips(reduced["early"][3]) if i == 0 else None
            dx, dnw, gm, got = _ssd_layer_bwd(dx, nw_all[i, 1], mix_w[i], sv_mix, gb["mid" if i == 0 else "early"], 0, rider=rider)
            if i == 0:
                from_chips["early"] = got
                reduced["mid"] = reduce_in_chip(gb["mid"])
        else:
            dx, dnw, gm = _sc_layer_bwd(dx, nw_all[i, 1], mix_w[i], sv_mix, gb["early"], j)
        dnorm[i][1] = dnw[0]
        for n, val in gm.items():
            grads[n][j] = val
        dx, got = ffn_back(i, 0, dx, sv_a, rider=exchange_between_chips(reduced["mid"][3]) if i == 0 else None)
        if i == 0:
            from_chips["mid"] = got

    grads["norm_w"] = jnp.stack([jnp.stack(r) for r in dnorm])
    for n in SMALL:
        if isinstance(grads[n], list):
            grads[n] = jnp.stack(grads[n])

    reduced["last"] = reduce_in_chip(gb["last"])
    from_chips["last"] = exchange_between_chips(reduced["last"][3]).run("exchange_between_chips")
    results = [{}, {}, {}, {}]
    outs = {}
    for which in ("last", "mid", "early"):
        names, bufs, from_sibling, _ = reduced[which]
        for n, g, fs, fc in zip(names, bufs, from_sibling, from_chips[which]):
            parts = [((0, 0), g), ((0,), fs), ((0,), fc), ((1,), fc), ((2,), fc)]
            outs[n] = adamw(parts, shard3d(w_loc[n], n), shard3d(m_loc[n], n), shard3d(v_loc[n], n), name="adamw_" + n + "_" + which,
                            a0=slabs(n, which)[0], prev=outs.get(n))
    for n in BIG:
        for k in range(4):
            o = outs[n][k]
            results[k][n] = (jnp.swapaxes(o, 1, 2) if n in TRANSPOSED else o).reshape(_local_shape(n))

    g_small = _pack([grads[n].reshape(-1) for n in SMALL], 8)
    g_small = sum_over_devices(all_gather([g_small]).run("all_gather_small_grads")[0])
    g_small_full = dict(zip(SMALL, _unpack(g_small, [PARAMS[n][0] for n in SMALL])))
    g_small_loc = []
    for n in SMALL:
        if PARAMS[n][1] is None:
            g_small_loc.append(g_small_full[n])
        else:
            g_small_loc.append(lax.dynamic_index_in_dim(_by_destination(g_small_full[n], n), my_dev, axis=0, keepdims=False))
    small_shapes = [_local_shape(n) for n in SMALL]
    pack_small = lambda d: _pack([d[n].reshape(-1) for n in SMALL], 8)[None]
    small_out = adamw([_pack([gl.reshape(-1) for gl in g_small_loc], 8)[None]], pack_small(w_loc), pack_small(m_loc), pack_small(v_loc), name="adamw_small")
    for k in range(4):
        results[k].update(zip(SMALL, _unpack(small_out[k], small_shapes)))
    return (loss, dx[None], *[results[0][n] for n in NAMES], *[results[1][n] for n in NAMES],
            *[results[2][n] for n in NAMES], *[results[3][n] for n in NAMES])
```

```python
import functools

import jax
import jax.numpy as jnp
from jax import lax
from jax.experimental import pallas as pl
from jax.experimental.pallas import tpu as pltpu

f32 = jnp.float32
bf16 = jnp.bfloat16

D_MODEL = 1024
D_FF = 2816
DEPTH = 4
SSD_INNER = 2048
SSD_HEADS = 32
SSD_HEAD_DIM = 64
SSD_GROUPS = 4
SSD_STATE = 128
SSD_CONV_K = 4
SSD_CONV_DIM = 3072
SSD_IN_DIM = 5152
SSD_CHUNK = 128
SC_CONV_K = 3
RMS_EPS = 1e-5
N_DEV = 8
LANES = 128
HALO = 16
PACK_W = 1024
VMEM_LIMIT = 56 * 1024 * 1024
NEG_BIG = -1e30

ADAM_LR = 0.001
ADAM_B1 = 0.9
ADAM_B2 = 0.999
ADAM_EPS = 1e-08
ADAM_WD = 0.01
ADAM_STEP = 10

NT_DIMS = (((1,), (1,)), ((), ()))
TN_DIMS = (((0,), (0,)), ((), ()))
MESH = pl.DeviceIdType.MESH


def _params(sem=None, vmem_limit=VMEM_LIMIT):
    return pltpu.CompilerParams(dimension_semantics=sem, vmem_limit_bytes=vmem_limit)


def _resident(shape):
    nd = len(shape)
    return pl.BlockSpec(tuple(shape), lambda *_: (0,) * nd, pipeline_mode=pl.Buffered(1))


def _rows(tm, width):
    return pl.BlockSpec((tm, width), lambda i: (i, 0))


def _my_core_and_chip():
    return lax.axis_index("c"), 2 * lax.axis_index("x") + lax.axis_index("y")


def _sigmoid(v):
    return 0.5 * jnp.tanh(0.5 * v) + 0.5


def _softplus(v):
    return jnp.maximum(v, 0.0) + jnp.log(1.0 + jnp.exp(-jnp.abs(v)))


def _rms_fwd(xv, w):
    inv = lax.rsqrt(jnp.mean(xv * xv, axis=-1, keepdims=True) + RMS_EPS)
    xh = xv * inv
    return xh * w, xh, inv


def _rms_bwd(dh, xh, inv, w):
    dxh = dh * w
    dx = inv * (dxh - xh * jnp.mean(dxh * xh, axis=-1, keepdims=True))
    return dx, jnp.sum(dh * xh, axis=0, keepdims=True)


def _mm(a, b):
    return jnp.dot(a, b, preferred_element_type=f32)


def _mm_nt(a, b):
    return lax.dot_general(a, b, NT_DIMS, preferred_element_type=f32)


def _mm_tn(a, b):
    return lax.dot_general(a, b, TN_DIMS, preferred_element_type=f32)


def _layer_slab(w, idx):
    tail = w.shape[len(idx):]
    return pl.BlockSpec((None,) * len(idx) + tuple(tail), lambda *_: tuple(idx) + (0,) * len(tail), pipeline_mode=pl.Buffered(1))


def ffn_fwd(x, nw, wg, wu, wd, idx, tm=512, rider=None):
    T = x.shape[0]
    nt = T // tm
    r_in, r_out, r_shapes, r_scratch, r_args = _rider_specs(rider)

    def body(x_ref, nw_ref, wg_ref, wu_ref, wd_ref, xo_ref, g_ref, u_ref, a_ref):
        xv = x_ref[...]
        h, _, _ = _rms_fwd(xv, nw_ref[...])
        hb = h.astype(bf16)
        g = _mm(hb, wg_ref[...])
        u = _mm(hb, wu_ref[...])
        ab = (g * _sigmoid(g) * u).astype(bf16)
        g_ref[...] = g.astype(bf16)
        u_ref[...] = u.astype(bf16)
        a_ref[...] = ab
        xo_ref[...] = xv + 0.5 * _mm(ab, wd_ref[...])

    hosted = _carry(body, 5, 4, rider, lambda: pl.program_id(0) == 0, lambda: pl.program_id(0) == nt - 1,
                    late=lambda: pl.program_id(0) == (7 * nt) // 8)
    return pl.pallas_call(
        hosted, name="ffn_fwd" if rider is None else "ffn_fwd_carrying", grid=(nt,),
        in_specs=[_rows(tm, D_MODEL), _resident((1, D_MODEL)), _layer_slab(wg, idx), _layer_slab(wu, idx), _layer_slab(wd, idx)] + r_in,
        out_specs=[_rows(tm, D_MODEL), _rows(tm, D_FF), _rows(tm, D_FF), _rows(tm, D_FF)] + r_out,
        out_shape=[jax.ShapeDtypeStruct((T, D_MODEL), f32)] + [jax.ShapeDtypeStruct((T, D_FF), bf16)] * 3 + r_shapes,
        scratch_shapes=r_scratch,
        compiler_params=_params(("parallel",) if rider is None else ("arbitrary",)),
    )(x, nw, wg, wu, wd, *r_args)


def ffn_bwd_dx(x, dxo, g, u, nw, wg, wu, wd, idx, tm=512, rider=None):
    T = x.shape[0]
    nt = T // tm
    r_in, r_out, r_shapes, r_scratch, r_args = _rider_specs(rider)

    def body(x_ref, dxo_ref, g_ref, u_ref, nw_ref, wg_ref, wu_ref, wd_ref, dx_hbm, h_hbm, dy_hbm, dg_hbm, du_hbm, dnw_ref,
             dx_ref, h_ref, dy_ref, dg_ref, du_ref, out_sems):
        i = pl.program_id(0)

        def out_copies(step):
            rows = pl.ds(pl.multiple_of(step * tm, tm), tm)
            pairs = [(dg_ref, dg_hbm), (du_ref, du_hbm), (dx_ref, dx_hbm), (h_ref, h_hbm), (dy_ref, dy_hbm)]
            return [pltpu.make_async_copy(src, dst.at[rows, :], out_sems.at[k]) for k, (src, dst) in enumerate(pairs)]

        @pl.when(i == 0)
        def _():
            dnw_ref[...] = jnp.zeros_like(dnw_ref)

        w = nw_ref[...]
        half = tm // 2
        for r0 in (0, half):
            rows = pl.ds(r0, half)
            h, xh, inv = _rms_fwd(x_ref[rows, :], w)
            dxo_v = dxo_ref[rows, :]
            dyb = (0.5 * dxo_v).astype(bf16)
            da = _mm_nt(dyb, wd_ref[...])
            gv = g_ref[rows, :].astype(f32)
            uv = u_ref[rows, :].astype(f32)
            s = _sigmoid(gv)
            dgb = (da * uv * (s * (1.0 + gv * (1.0 - s)))).astype(bf16)
            dub = (da * (gv * s)).astype(bf16)
            if r0 == 0:
                @pl.when(i > 0)
                def _():
                    for cp in out_copies(i - 1):
                        cp.wait()

            dg_ref[rows, :] = dgb
            du_ref[rows, :] = dub
            dh = _mm_nt(dg_ref[rows, :], wg_ref[...]) + _mm_nt(du_ref[rows, :], wu_ref[...])
            dxn, dw = _rms_bwd(dh, xh, inv, w)
            dx_ref[rows, :] = dxo_v + dxn
            h_ref[rows, :] = h.astype(bf16)
            dy_ref[rows, :] = dyb
            dnw_ref[...] += dw
        for cp in out_copies(i):
            cp.start()

        @pl.when(i == nt - 1)
        def _():
            for cp in out_copies(i):
                cp.wait()

    hosted = _carry(body, 8, 6, rider, lambda: pl.program_id(0) == 0, lambda: pl.program_id(0) == nt - 1)
    return pl.pallas_call(
        hosted, name="ffn_bwd_dx" if rider is None else "ffn_bwd_dx_carrying", grid=(nt,),
        in_specs=[_rows(tm, D_MODEL), _rows(tm, D_MODEL), _rows(tm, D_FF), _rows(tm, D_FF), _resident((1, D_MODEL)),
                  _layer_slab(wg, idx), _layer_slab(wu, idx), _layer_slab(wd, idx)] + r_in,
        out_specs=[_ANY, _ANY, _ANY, _ANY, _ANY, pl.BlockSpec((1, D_MODEL), lambda i: (0, 0))] + r_out,
        out_shape=[jax.ShapeDtypeStruct((T, D_MODEL), f32), jax.ShapeDtypeStruct((T, D_MODEL), bf16), jax.ShapeDtypeStruct((T, D_MODEL), bf16),
                   jax.ShapeDtypeStruct((T, D_FF), bf16), jax.ShapeDtypeStruct((T, D_FF), bf16), jax.ShapeDtypeStruct((1, D_MODEL), f32)] + r_shapes,
        scratch_shapes=[pltpu.VMEM((tm, D_MODEL), f32), pltpu.VMEM((tm, D_MODEL), bf16), pltpu.VMEM((tm, D_MODEL), bf16),
                        pltpu.VMEM((tm, D_FF), bf16), pltpu.VMEM((tm, D_FF), bf16), pltpu.SemaphoreType.DMA((5,))] + r_scratch,
        compiler_params=_params(("arbitrary",)),
    )(x, dxo, g, u, nw, wg, wu, wd, *r_args)


def tn_matmul(a, b, tk=1024):
    T, M = a.shape
    N = b.shape[1]
    bn = N if M * N <= 3_200_000 else N // 2
    nk = T // tk

    def body(a_ref, b_ref, o_ref):
        @pl.when(pl.program_id(1) == 0)
        def _():
            o_ref[...] = jnp.zeros_like(o_ref)

        o_ref[...] += _mm_tn(a_ref[...], b_ref[...])

    return pl.pallas_call(
        body, name=f"tn_matmul_{M}x{N}", grid=(N // bn, nk),
        in_specs=[pl.BlockSpec((tk, M), lambda j, k: (k, 0)), pl.BlockSpec((tk, bn), lambda j, k: (k, j))],
        out_specs=pl.BlockSpec((M, bn), lambda j, k: (0, j)),
        out_shape=jax.ShapeDtypeStruct((M, N), f32),
        compiler_params=_params(("parallel", "arbitrary")),
    )(a, b)


def tn_matmul_to_shards(a, b, buf, idx, axis):
    T, M = a.shape
    N = b.shape[1]
    m, n = buf.shape[-2:]
    (slab,) = idx
    tk = 1024
    nk = T // tk
    fresh = isinstance(buf, jax.ShapeDtypeStruct)

    def body(a_ref, b_ref, *rest):
        o_ref, acc_ref, stage_ref, sem = rest[-4:]
        k = pl.program_id(0)

        @pl.when(k == 0)
        def _():
            acc_ref[...] = jnp.zeros_like(acc_ref)

        acc_ref[...] += _mm_tn(a_ref[...], b_ref[...])

        @pl.when(k == nk - 1)
        def _():
            my_c, my_chip = _my_core_and_chip()
            for d in range(N_DEV):
                piece = acc_ref[:, pl.ds(d * n, n)] if axis == 1 else acc_ref[pl.ds(d * m, m), :]
                stage_ref[(d % 2) ^ my_c, (d // 2) ^ my_chip] = piece
            out = pltpu.make_async_copy(stage_ref, o_ref.at[:, :, slab], sem)
            out.start()
            out.wait()

    return pl.pallas_call(
        body, name=f"tn_matmul_to_shards_{M}x{N}_{axis}", grid=(nk,),
        in_specs=[pl.BlockSpec((tk, M), lambda k: (k, 0)), pl.BlockSpec((tk, N), lambda k: (k, 0))] + ([] if fresh else [_ANY]),
        out_specs=_ANY,
        out_shape=jax.ShapeDtypeStruct(buf.shape, f32),
        scratch_shapes=[pltpu.VMEM((M, N), f32), pltpu.VMEM((2, 4, m, n), f32), pltpu.SemaphoreType.DMA],
        input_output_aliases={} if fresh else {2: 0},
        compiler_params=_params(("arbitrary",)),
    )(a, b, *([] if fresh else [buf]))


def in_proj_fwd(x, nw, ws, out_dtypes, tm=512):
    T = x.shape[0]
    n = len(ws)

    def body(*refs):
        x_ref, nw_ref = refs[:2]
        w_refs = refs[2:2 + n]
        o_refs = refs[2 + n:]
        h, _, _ = _rms_fwd(x_ref[...], nw_ref[...])
        hb = h.astype(bf16)
        for w_ref, o_ref in zip(w_refs, o_refs):
            o_ref[...] = _mm(hb, w_ref[...]).astype(o_ref.dtype)

    return pl.pallas_call(
        body, name="in_proj_fwd_" + "_".join(str(w.shape[1]) for w in ws), grid=(T // tm,),
        in_specs=[_rows(tm, D_MODEL), _resident((1, D_MODEL))] + [_resident(w.shape) for w in ws],
        out_specs=[_rows(tm, w.shape[1]) for w in ws],
        out_shape=[jax.ShapeDtypeStruct((T, w.shape[1]), dt) for w, dt in zip(ws, out_dtypes)],
        compiler_params=_params(("parallel",)),
    )(x, nw, *ws)


def in_proj_bwd(x, nw, dxo, dys, ws, tm=512):
    T = x.shape[0]
    n = len(ws)

    def body(*refs):
        x_ref, nw_ref, dxo_ref = refs[:3]
        dy_refs = refs[3:3 + n]
        w_refs = refs[3 + n:3 + 2 * n]
        dx_ref, h_ref, dnw_ref = refs[3 + 2 * n:]
        w = nw_ref[...]
        h, xh, inv = _rms_fwd(x_ref[...], w)
        dh = _mm_nt(dy_refs[0][...], w_refs[0][...])
        for dy_ref, w_ref in zip(dy_refs[1:], w_refs[1:]):
            dh = dh + _mm_nt(dy_ref[...], w_ref[...])
        dxn, dw = _rms_bwd(dh, xh, inv, w)
        dx_ref[...] = dxo_ref[...] + dxn
        h_ref[...] = h.astype(bf16)

        @pl.when(pl.program_id(0) == 0)
        def _():
            dnw_ref[...] = jnp.zeros_like(dnw_ref)

        dnw_ref[...] += dw

    return pl.pallas_call(
        body, name="in_proj_bwd_" + "_".join(str(w.shape[1]) for w in ws), grid=(T // tm,),
        in_specs=[_rows(tm, D_MODEL), _resident((1, D_MODEL)), _rows(tm, D_MODEL)] + [_rows(tm, w.shape[1]) for w in ws]
        + [_resident(w.shape) for w in ws],
        out_specs=[_rows(tm, D_MODEL), _rows(tm, D_MODEL), pl.BlockSpec((1, D_MODEL), lambda i: (0, 0))],
        out_shape=[jax.ShapeDtypeStruct((T, D_MODEL), f32), jax.ShapeDtypeStruct((T, D_MODEL), bf16), jax.ShapeDtypeStruct((1, D_MODEL), f32)],
        compiler_params=_params(("arbitrary",)),
    )(x, nw, dxo, *dys, *ws)


def out_proj_fwd(x, a, w, tm=1024):
    T = x.shape[0]
    K = a.shape[1]

    def body(x_ref, a_ref, w_ref, o_ref):
        o_ref[...] = x_ref[...] + _mm(a_ref[...], w_ref[...])

    return pl.pallas_call(
        body, name=f"out_proj_fwd_{K}", grid=(T // tm,),
        in_specs=[_rows(tm, D_MODEL), _rows(tm, K), _resident(w.shape)],
        out_specs=_rows(tm, D_MODEL), out_shape=jax.ShapeDtypeStruct((T, D_MODEL), f32),
        compiler_params=_params(("parallel",)),
    )(x, a, w)


def _halo_spec(tm, width, n_tiles, reverse):
    per = tm // HALO

    def idx(i):
        t = (n_tiles - 1 - i) if reverse else i
        return (jnp.maximum(t * per - 1, 0), 0)

    return pl.BlockSpec((HALO, width), idx)


def _tile_spec(tm, width, n_tiles, reverse):
    if reverse:
        return pl.BlockSpec((tm, width), lambda i: (n_tiles - 1 - i, 0))
    return _rows(tm, width)


ROW_BLOCK = 64


def _strip(s):
    return pl.ds(pl.multiple_of(s * LANES, LANES), LANES)


def _conv_rows(ext_ref, w_ref, cols, k_w, r0):
    base = HALO - (k_w - 1) + r0
    wins = [ext_ref[pl.ds(base + k, ROW_BLOCK), :] for k in range(k_w)]
    out = w_ref[pl.ds(0, 1), cols] * wins[0]
    for k in range(1, k_w):
        out = out + w_ref[pl.ds(k, 1), cols] * wins[k]
    return out, wins


def _shifted_back(d_ref, w_ref, cols, k_w, r0):
    out = w_ref[pl.ds(0, 1), cols] * d_ref[pl.ds(r0 + k_w - 1, ROW_BLOCK), :]
    for k in range(1, k_w):
        out = out + w_ref[pl.ds(k, 1), cols] * d_ref[pl.ds(r0 + k_w - 1 - k, ROW_BLOCK), :]
    return out


def ssd_conv_fwd(xbc, conv_w, conv_b, dt_raw, dt_bias, tm=512):
    T = xbc.shape[0]
    nt = T // tm
    K = SSD_CONV_K

    def body(x_ref, halo_ref, w_ref, b_ref, dtr_ref, dtb_ref, act_ref, dt_ref, ext_ref):
        first = pl.program_id(0) == 0

        def strip(s, carry):
            cols = _strip(s)
            ext_ref[pl.ds(0, HALO), :] = jnp.where(first, 0.0, halo_ref[:, cols].astype(f32))
            ext_ref[pl.ds(HALO, tm), :] = x_ref[:, cols].astype(f32)
            for r0 in range(0, tm, ROW_BLOCK):
                pre, _ = _conv_rows(ext_ref, w_ref, cols, K, r0)
                pre = pre + b_ref[:, cols]
                act_ref[pl.ds(r0, ROW_BLOCK), cols] = (pre * _sigmoid(pre)).astype(bf16)
            return carry

        lax.fori_loop(0, SSD_CONV_DIM // LANES, strip, 0)
        dt = _softplus(dtr_ref[...] + dtb_ref[...])
        lane = lax.broadcasted_iota(jnp.int32, (1, LANES), 1)
        for g in range(SSD_GROUPS):
            dt_ref[g] = jnp.where(lane < 8, dt if g == 0 else pltpu.roll(dt, LANES - 8 * g, axis=1), 0.0)

    return pl.pallas_call(
        body, name="ssd_conv_fwd", grid=(nt,),
        in_specs=[_rows(tm, SSD_CONV_DIM), _halo_spec(tm, SSD_CONV_DIM, nt, False), _resident(conv_w.shape), _resident(conv_b.shape),
                  _rows(tm, LANES), _resident(dt_bias.shape)],
        out_specs=[_rows(tm, SSD_CONV_DIM), pl.BlockSpec((SSD_GROUPS, tm, LANES), lambda i: (0, i, 0))],
        out_shape=[jax.ShapeDtypeStruct((T, SSD_CONV_DIM), bf16), jax.ShapeDtypeStruct((SSD_GROUPS, T, LANES), f32)],
        scratch_shapes=[pltpu.VMEM((tm + HALO, LANES), f32)],
        compiler_params=_params(("parallel",)),
    )(xbc, xbc, conv_w, conv_b, dt_raw, dt_bias)


def ssd_conv_bwd(xbc, conv_w, conv_b, dt_raw, dt_bias, dxs_a, dxs_b, db, dc, ddt, tm=512):
    T = xbc.shape[0]
    nt = T // tm
    K = SSD_CONV_K

    def body(x_ref, halo_ref, w_ref, b_ref, dtr_ref, dtb_ref, da_ref, dbb_ref, db_ref, dc_ref, ddt_ref,
             dx_ref, ddtr_ref, dw_ref, dbias_ref, ddtb_ref, ext_ref, dpre_ref, carry_ref):
        i = pl.program_id(0)

        @pl.when(i == 0)
        def _():
            carry_ref[...] = jnp.zeros_like(carry_ref)
            dw_ref[...] = jnp.zeros_like(dw_ref)
            dbias_ref[...] = jnp.zeros_like(dbias_ref)
            ddtb_ref[...] = jnp.zeros_like(ddtb_ref)

        first_tile = i == nt - 1

        def run_strips(lo, hi, load_dact):
            def strip(s, carry):
                cols = _strip(s)
                ext_ref[pl.ds(0, HALO), :] = jnp.where(first_tile, 0.0, halo_ref[:, cols].astype(f32))
                ext_ref[pl.ds(HALO, tm), :] = x_ref[:, cols].astype(f32)
                dpre_ref[pl.ds(tm, 8), :] = carry_ref[:, cols]
                bias = b_ref[:, cols]
                dws = [jnp.zeros((1, LANES), f32) for _ in range(K)]
                dbs = jnp.zeros((1, LANES), f32)
                for r0 in range(0, tm, ROW_BLOCK):
                    pre, wins = _conv_rows(ext_ref, w_ref, cols, K, r0)
                    pre = pre + bias
                    sg = _sigmoid(pre)
                    dpre = load_dact(s, r0) * (sg * (1.0 + pre * (1.0 - sg)))
                    dpre_ref[pl.ds(r0, ROW_BLOCK), :] = dpre
                    dbs = dbs + jnp.sum(dpre, axis=0, keepdims=True)
                    for k in range(K):
                        dws[k] = dws[k] + jnp.sum(dpre * wins[k], axis=0, keepdims=True)
                carry_ref[:, cols] = dpre_ref[pl.ds(0, 8), :]
                for r0 in range(0, tm, ROW_BLOCK):
                    dx_ref[pl.ds(r0, ROW_BLOCK), cols] = _shifted_back(dpre_ref, w_ref, cols, K, r0).astype(bf16)
                for k in range(K):
                    dw_ref[pl.ds(k, 1), cols] += dws[k]
                dbias_ref[:, cols] += dbs
                return carry

            lax.fori_loop(lo, hi, strip, 0)

        rows = lambda r0: pl.ds(r0, ROW_BLOCK)
        n_x = SSD_INNER // LANES
        n_g = SSD_GROUPS * SSD_STATE // LANES
        run_strips(0, n_x, lambda s, r0: da_ref[rows(r0), _strip(s)].astype(f32) + dbb_ref[rows(r0), _strip(s)].astype(f32))
        run_strips(n_x, n_x + n_g, lambda s, r0: db_ref[rows(r0), _strip(s - n_x)].astype(f32))
        run_strips(n_x + n_g, n_x + 2 * n_g, lambda s, r0: dc_ref[rows(r0), _strip(s - n_x - n_g)].astype(f32))
        lane = lax.broadcasted_iota(jnp.int32, (1, LANES), 1)
        ddt = jnp.where(lane < 8, ddt_ref[0], 0.0)
        for g in range(1, SSD_GROUPS):
            ddt = ddt + pltpu.roll(jnp.where(lane < 8, ddt_ref[g], 0.0), 8 * g, axis=1)
        ddtr = ddt * _sigmoid(dtr_ref[...] + dtb_ref[...])
        ddtr_ref[...] = ddtr.astype(bf16)
        ddtb_ref[...] += jnp.sum(ddtr, axis=0, keepdims=True)

    rev = functools.partial(_tile_spec, tm, n_tiles=nt, reverse=True)
    const = lambda shape: pl.BlockSpec(shape, lambda i: (0, 0))
    return pl.pallas_call(
        body, name="ssd_conv_bwd", grid=(nt,),
        in_specs=[rev(width=SSD_CONV_DIM), _halo_spec(tm, SSD_CONV_DIM, nt, True), _resident(conv_w.shape), _resident(conv_b.shape),
                  rev(width=LANES), _resident(dt_bias.shape), rev(width=SSD_INNER), rev(width=SSD_INNER),
                  rev(width=SSD_GROUPS * SSD_STATE), rev(width=SSD_GROUPS * SSD_STATE),
                  pl.BlockSpec((SSD_GROUPS, tm, LANES), lambda i: (0, nt - 1 - i, 0))],
        out_specs=[rev(width=SSD_CONV_DIM), rev(width=LANES), const((8, SSD_CONV_DIM)), const((1, SSD_CONV_DIM)), const((1, LANES))],
        out_shape=[jax.ShapeDtypeStruct((T, SSD_CONV_DIM), bf16), jax.ShapeDtypeStruct((T, LANES), bf16),
                   jax.ShapeDtypeStruct((8, SSD_CONV_DIM), f32), jax.ShapeDtypeStruct((1, SSD_CONV_DIM), f32), jax.ShapeDtypeStruct((1, LANES), f32)],
        scratch_shapes=[pltpu.VMEM((tm + HALO, LANES), f32), pltpu.VMEM((tm + 8, LANES), f32), pltpu.VMEM((8, SSD_CONV_DIM), f32)],
        compiler_params=_params(("arbitrary",)),
    )(xbc, xbc, conv_w, conv_b, dt_raw, dt_bias, dxs_a, dxs_b, db, dc, ddt)


def _ssd_chunk(xs, bm, cm, dt, alog, st):
    L = SSD_CHUNK
    row = lax.broadcasted_iota(jnp.int32, (L, L), 0)
    col = lax.broadcasted_iota(jnp.int32, (L, L), 1)
    causal = row >= col
    tril = jnp.where(causal, 1.0, 0.0).astype(f32)
    lane = lax.broadcasted_iota(jnp.int32, (1, LANES), 1)
    sub = lax.broadcasted_iota(jnp.int32, (LANES, 1), 0)
    lo = lane < SSD_HEAD_DIM
    last_row = sub == L - 1

    dta = dt * (-jnp.exp(alog))
    a_cs = jnp.dot(tril, dta, precision=lax.Precision.HIGHEST, preferred_element_type=f32)
    a_cs_t = a_cs.T
    bmb = bm.astype(bf16)
    cmb = cm.astype(bf16)
    cb = _mm_nt(cmb, bmb)
    c_st = _mm(cmb, st.astype(bf16))

    def head_col(v, e):
        return jnp.sum(jnp.where(lane == e, v, 0.0), axis=1, keepdims=True)

    def head_row(v, e):
        return jnp.sum(jnp.where(sub == e, v, 0.0), axis=0, keepdims=True)

    ys, sts = [], []
    for j in range(4):
        e0, e1 = 2 * j, 2 * j + 1
        c0, c1 = head_col(a_cs, e0), head_col(a_cs, e1)
        acs_x = jnp.where(lo, c0, c1)
        dt_x = jnp.where(lo, head_col(dt, e0), head_col(dt, e1))
        xd = xs[:, j * LANES:(j + 1) * LANES] * dt_x
        m0 = cb * jnp.exp(jnp.where(causal, c0 - head_row(a_cs_t, e0), NEG_BIG))
        m1 = cb * jnp.exp(jnp.where(causal, c1 - head_row(a_cs_t, e1), NEG_BIG))
        mcat = jnp.concatenate([m0, m1], axis=1).astype(bf16)
        xcat = jnp.concatenate([jnp.where(lo, xd, 0.0), jnp.where(lo, 0.0, xd)], axis=0).astype(bf16)
        y_diag = _mm(mcat, xcat)
        a_last = jnp.sum(jnp.where(last_row, acs_x, 0.0), axis=0, keepdims=True)
        x_dec = (xd * jnp.exp(a_last - acs_x)).astype(bf16)
        s_new = _mm_tn(bmb, x_dec)
        y_off = c_st[:, j * LANES:(j + 1) * LANES] * jnp.exp(acs_x)
        ys.append(y_diag + y_off)
        sts.append(jnp.exp(a_last) * st[:, j * LANES:(j + 1) * LANES] + s_new)
    return jnp.concatenate(ys, axis=1), jnp.concatenate(sts, axis=1)


SCAN_GROUPS_FWD = 4
SCAN_GROUPS_BWD = 1


def _scan_specs(nc, reverse, gs):
    L = SSD_CHUNK
    ch = (lambda c: nc - 1 - c) if reverse else (lambda c: c)
    gw = SSD_INNER // SSD_GROUPS
    b0 = SSD_INNER // (gs * SSD_STATE)
    c0 = (SSD_INNER + SSD_GROUPS * SSD_STATE) // (gs * SSD_STATE)
    xs = pl.BlockSpec((L, gs * gw), lambda g, c: (ch(c), g))
    bm = pl.BlockSpec((L, gs * SSD_STATE), lambda g, c: (ch(c), b0 + g))
    cm = pl.BlockSpec((L, gs * SSD_STATE), lambda g, c: (ch(c), c0 + g))
    dt = pl.BlockSpec((gs, L, LANES), lambda g, c: (g, ch(c), 0))
    alog = pl.BlockSpec((gs, 1, LANES), lambda g, c: (g, 0, 0))
    st = pl.BlockSpec((gs, None, SSD_STATE, gw), lambda g, c: (g, ch(c), 0, 0))
    y = pl.BlockSpec((L, gs * gw), lambda g, c: (ch(c), g))
    grp = pl.BlockSpec((L, gs * SSD_STATE), lambda g, c: (ch(c), g))
    return xs, bm, cm, dt, alog, st, y, grp


def ssd_scan_fwd(act, dt4, alog4, rider=None):
    T = act.shape[0]
    nc = T // SSD_CHUNK
    gs = SCAN_GROUPS_FWD
    ng = SSD_GROUPS // gs
    gw = SSD_INNER // SSD_GROUPS
    xs_s, bm_s, cm_s, dt_s, alog_s, st_s, y_s, _ = _scan_specs(nc, False, gs)
    r_in, r_out, r_shapes, r_scratch, r_args = _rider_specs(rider)

    def body(xs_ref, bm_ref, cm_ref, dt_ref, alog_ref, y_ref, st_ref, st_scr):
        @pl.when(pl.program_id(1) == 0)
        def _():
            st_scr[...] = jnp.zeros_like(st_scr)

        for q in range(gs):
            xc, gc = pl.ds(q * gw, gw), pl.ds(q * SSD_STATE, SSD_STATE)
            st = st_scr[q]
            st_ref[q] = st
            y, st_new = _ssd_chunk(xs_ref[:, xc].astype(f32), bm_ref[:, gc].astype(f32), cm_ref[:, gc].astype(f32), dt_ref[q], alog_ref[q], st)
            y_ref[:, xc] = y.astype(bf16)
            st_scr[q] = st_new

    first = lambda: jnp.logical_and(pl.program_id(0) == 0, pl.program_id(1) == 0)
    last = lambda: jnp.logical_and(pl.program_id(0) == ng - 1, pl.program_id(1) == nc - 1)
    late = lambda: jnp.logical_and(pl.program_id(0) == ng - 1, pl.program_id(1) == (7 * nc) // 8)
    return pl.pallas_call(
        _carry(body, 5, 2, rider, first, last, late), name="ssd_scan_fwd" if rider is None else "ssd_scan_fwd_carrying", grid=(ng, nc),
        in_specs=[xs_s, bm_s, cm_s, dt_s, alog_s] + r_in, out_specs=[y_s, st_s] + r_out,
        out_shape=[jax.ShapeDtypeStruct((T, SSD_INNER), bf16), jax.ShapeDtypeStruct((SSD_GROUPS, nc, SSD_STATE, gw), f32)] + r_shapes,
        scratch_shapes=[pltpu.VMEM((gs, SSD_STATE, gw), f32)] + r_scratch,
        compiler_params=_params(("parallel" if rider is None else "arbitrary", "arbitrary")),
    )(act, act, act, dt4, alog4, *r_args)


def ssd_scan_bwd(act, dt4, alog4, states, dy, rider=None):
    T = act.shape[0]
    nc = T // SSD_CHUNK
    gs = SCAN_GROUPS_BWD
    ng = SSD_GROUPS // gs
    gw = SSD_INNER // SSD_GROUPS
    xs_s, bm_s, cm_s, dt_s, alog_s, st_s, y_s, grp_s = _scan_specs(nc, True, gs)
    r_in, r_out, r_shapes, r_scratch, r_args = _rider_specs(rider)

    def body(xs_ref, bm_ref, cm_ref, dt_ref, alog_ref, st_ref, dy_ref, dxs_ref, db_ref, dc_ref, ddt_ref, dalog_ref, dst_scr):
        @pl.when(pl.program_id(1) == 0)
        def _():
            dst_scr[...] = jnp.zeros_like(dst_scr)
            dalog_ref[...] = jnp.zeros_like(dalog_ref)

        for q in range(gs):
            xc, gc = pl.ds(q * gw, gw), pl.ds(q * SSD_STATE, SSD_STATE)
            _, vjp = jax.vjp(_ssd_chunk, xs_ref[:, xc].astype(f32), bm_ref[:, gc].astype(f32), cm_ref[:, gc].astype(f32),
                             dt_ref[q], alog_ref[q], st_ref[q])
            dxs, dbm, dcm, ddt, dalog, dst = vjp((dy_ref[:, xc].astype(f32), dst_scr[q]))
            dxs_ref[:, xc] = dxs.astype(bf16)
            db_ref[:, gc] = dbm.astype(bf16)
            dc_ref[:, gc] = dcm.astype(bf16)
            ddt_ref[q] = ddt
            dalog_ref[q] += dalog
            dst_scr[q] = dst

    first = lambda: jnp.logical_and(pl.program_id(0) == 0, pl.program_id(1) == 0)
    last = lambda: jnp.logical_and(pl.program_id(0) == ng - 1, pl.program_id(1) == nc - 1)
    return pl.pallas_call(
        _carry(body, 7, 5, rider, first, last), name="ssd_scan_bwd" if rider is None else "ssd_scan_bwd_carrying", grid=(ng, nc),
        in_specs=[xs_s, bm_s, cm_s, dt_s, alog_s, st_s, y_s] + r_in,
        out_specs=[y_s, grp_s, grp_s, dt_s, alog_s] + r_out,
        out_shape=[jax.ShapeDtypeStruct((T, SSD_INNER), bf16), jax.ShapeDtypeStruct((T, SSD_GROUPS * SSD_STATE), bf16),
                   jax.ShapeDtypeStruct((T, SSD_GROUPS * SSD_STATE), bf16), jax.ShapeDtypeStruct((SSD_GROUPS, T, LANES), f32),
                   jax.ShapeDtypeStruct((SSD_GROUPS, 1, LANES), f32)] + r_shapes,
        scratch_shapes=[pltpu.VMEM((gs, SSD_STATE, gw), f32)] + r_scratch,
        compiler_params=_params(("parallel" if rider is None else "arbitrary", "arbitrary")),
    )(act, act, act, dt4, alog4, states, dy, *r_args)


GATE_ROWS = 256


def _ssd_gate(y, xs, z, d_x, nw):
    g = (y + xs * d_x) * (z * _sigmoid(z))
    return g * lax.rsqrt(jnp.mean(g * g, axis=-1, keepdims=True) + RMS_EPS) * nw


def _gate_blocks(tm, fn):
    gw = SSD_INNER // SSD_GROUPS

    def block(r, carry):
        rows = pl.ds(r * GATE_ROWS if isinstance(r, int) else pl.multiple_of(r * GATE_ROWS, GATE_ROWS), GATE_ROWS)
        for k in range(SSD_GROUPS):
            fn(rows, pl.ds(k * gw, gw))
        return carry

    if tm == GATE_ROWS:
        block(0, 0)
    else:
        lax.fori_loop(0, tm // GATE_ROWS, block, 0)


def ssd_gate_fwd(y, act, z, d_x, nw, tm=512):
    T = y.shape[0]

    def body(y_ref, xs_ref, z_ref, d_ref, nw_ref, o_ref):
        def one(rows, cols):
            o_ref[rows, cols] = _ssd_gate(y_ref[rows, cols].astype(f32), xs_ref[rows, cols].astype(f32), z_ref[rows, cols].astype(f32),
                                          d_ref[:, cols], nw_ref[:, cols]).astype(bf16)

        _gate_blocks(tm, one)

    return pl.pallas_call(
        body, name="ssd_gate_fwd", grid=(T // tm,),
        in_specs=[_rows(tm, SSD_INNER), _rows(tm, SSD_INNER), _rows(tm, SSD_INNER), _resident(d_x.shape), _resident(nw.shape)],
        out_specs=_rows(tm, SSD_INNER), out_shape=jax.ShapeDtypeStruct((T, SSD_INNER), bf16),
        compiler_params=_params(("parallel",)),
    )(y, act, z, d_x, nw)


def ssd_gate_bwd(y, act, z, d_x, nw, dxo, w_out, tm=512):
    T = y.shape[0]

    def body(y_ref, xs_ref, z_ref, d_ref, nw_ref, dxo_ref, w_ref, dy_ref, dxs_ref, dz_ref, dd_ref, dnw_ref, dyb_ref):
        @pl.when(pl.program_id(0) == 0)
        def _():
            dd_ref[...] = jnp.zeros_like(dd_ref)
            dnw_ref[...] = jnp.zeros_like(dnw_ref)

        dyb_ref[...] = dxo_ref[...].astype(bf16)

        def one(rows, cols):
            _, vjp = jax.vjp(_ssd_gate, y_ref[rows, cols].astype(f32), xs_ref[rows, cols].astype(f32), z_ref[rows, cols].astype(f32),
                             d_ref[:, cols], nw_ref[:, cols])
            dy, dxs, dz, dd, dnw = vjp(_mm_nt(dyb_ref[rows, :], w_ref[cols, :]))
            dy_ref[rows, cols] = dy.astype(bf16)
            dxs_ref[rows, cols] = dxs.astype(bf16)
            dz_ref[rows, cols] = dz.astype(bf16)
            dd_ref[:, cols] += dd
            dnw_ref[:, cols] += dnw

        _gate_blocks(tm, one)

    const = pl.BlockSpec((1, SSD_INNER), lambda i: (0, 0))
    return pl.pallas_call(
        body, name="ssd_gate_bwd", grid=(T // tm,),
        in_specs=[_rows(tm, SSD_INNER), _rows(tm, SSD_INNER), _rows(tm, SSD_INNER), _resident(d_x.shape), _resident(nw.shape),
                  _rows(tm, D_MODEL), _resident(w_out.shape)],
        out_specs=[_rows(tm, SSD_INNER)] * 3 + [const, const, _rows(tm, D_MODEL)],
        out_shape=[jax.ShapeDtypeStruct((T, SSD_INNER), bf16)] * 3 + [jax.ShapeDtypeStruct((1, SSD_INNER), f32)] * 2
        + [jax.ShapeDtypeStruct((T, D_MODEL), bf16)],
        compiler_params=_params(("arbitrary",)),
    )(y, act, z, d_x, nw, dxo, w_out)


def sc_mid_fwd(bcu, conv_w, tm=512):
    T = bcu.shape[0]
    nt = T // tm
    Dm = D_MODEL

    def body(x_ref, halo_ref, w_ref, q_ref, ext_ref):
        first = pl.program_id(0) == 0
        n_s = Dm // LANES

        def strip(s, carry):
            cols, c_cols, u_cols = _strip(s), _strip(s + n_s), _strip(s + 2 * n_s)
            ext_ref[pl.ds(0, HALO), :] = jnp.where(first, 0.0, halo_ref[:, c_cols].astype(f32) * halo_ref[:, u_cols].astype(f32))
            ext_ref[pl.ds(HALO, tm), :] = x_ref[:, c_cols].astype(f32) * x_ref[:, u_cols].astype(f32)
            for r0 in range(0, tm, ROW_BLOCK):
                rows = pl.ds(r0, ROW_BLOCK)
                v, _ = _conv_rows(ext_ref, w_ref, cols, SC_CONV_K, r0)
                q_ref[rows, cols] = (x_ref[rows, cols].astype(f32) * v).astype(bf16)
            return carry

        lax.fori_loop(0, n_s, strip, 0)

    return pl.pallas_call(
        body, name="sc_mid_fwd", grid=(nt,),
        in_specs=[_rows(tm, 3 * Dm), _halo_spec(tm, 3 * Dm, nt, False), _resident(conv_w.shape)],
        out_specs=_rows(tm, Dm), out_shape=jax.ShapeDtypeStruct((T, Dm), bf16),
        scratch_shapes=[pltpu.VMEM((tm + HALO, LANES), f32)],
        compiler_params=_params(("parallel",)),
    )(bcu, bcu, conv_w)


def sc_mid_bwd(bcu, conv_w, dxo, w_out, tm=512):
    T = bcu.shape[0]
    nt = T // tm
    Dm = D_MODEL
    K = SC_CONV_K

    def body(x_ref, halo_ref, w_ref, dxo_ref, wo_ref, dx_ref, dw_ref, dyb_ref, ext_ref, dv_ref, carry_ref, dq_ref):
        i = pl.program_id(0)

        @pl.when(i == 0)
        def _():
            carry_ref[...] = jnp.zeros_like(carry_ref)
            dw_ref[...] = jnp.zeros_like(dw_ref)

        dyb = dxo_ref[...].astype(bf16)
        dyb_ref[...] = dyb
        dq_ref[...] = _mm_nt(dyb, wo_ref[...])
        first_tile = i == nt - 1
        n_s = Dm // LANES

        def strip(s, carry):
            cols, c_cols, u_cols = _strip(s), _strip(s + n_s), _strip(s + 2 * n_s)
            ext_ref[pl.ds(0, HALO), :] = jnp.where(first_tile, 0.0, halo_ref[:, c_cols].astype(f32) * halo_ref[:, u_cols].astype(f32))
            ext_ref[pl.ds(HALO, tm), :] = x_ref[:, c_cols].astype(f32) * x_ref[:, u_cols].astype(f32)
            dv_ref[pl.ds(tm, 8), :] = carry_ref[:, cols]
            dws = [jnp.zeros((1, LANES), f32) for _ in range(K)]
            for r0 in range(0, tm, ROW_BLOCK):
                rows = pl.ds(r0, ROW_BLOCK)
                v, wins = _conv_rows(ext_ref, w_ref, cols, K, r0)
                dqv = dq_ref[rows, cols]
                dv = dqv * x_ref[rows, cols].astype(f32)
                dv_ref[rows, :] = dv
                dx_ref[rows, cols] = (dqv * v).astype(bf16)
                for k in range(K):
                    dws[k] = dws[k] + jnp.sum(dv * wins[k], axis=0, keepdims=True)
            carry_ref[:, cols] = dv_ref[pl.ds(0, 8), :]
            for r0 in range(0, tm, ROW_BLOCK):
                rows = pl.ds(r0, ROW_BLOCK)
                dp = _shifted_back(dv_ref, w_ref, cols, K, r0)
                dx_ref[rows, c_cols] = (dp * x_ref[rows, u_cols].astype(f32)).astype(bf16)
                dx_ref[rows, u_cols] = (dp * x_ref[rows, c_cols].astype(f32)).astype(bf16)
            for k in range(K):
                dw_ref[pl.ds(k, 1), cols] += dws[k]
            return carry

        lax.fori_loop(0, n_s, strip, 0)

    return pl.pallas_call(
        body, name="sc_mid_bwd", grid=(nt,),
        in_specs=[_tile_spec(tm, 3 * Dm, nt, True), _halo_spec(tm, 3 * Dm, nt, True), _resident(conv_w.shape), _tile_spec(tm, Dm, nt, True),
                  _resident(w_out.shape)],
        out_specs=[_tile_spec(tm, 3 * Dm, nt, True), pl.BlockSpec((8, Dm), lambda i: (0, 0)), _tile_spec(tm, Dm, nt, True)],
        out_shape=[jax.ShapeDtypeStruct((T, 3 * Dm), bf16), jax.ShapeDtypeStruct((8, Dm), f32), jax.ShapeDtypeStruct((T, Dm), bf16)],
        scratch_shapes=[pltpu.VMEM((tm + HALO, LANES), f32), pltpu.VMEM((tm + 8, LANES), f32), pltpu.VMEM((8, Dm), f32),
                        pltpu.VMEM((tm, Dm), f32)],
        compiler_params=_params(("arbitrary",)),
    )(bcu, bcu, conv_w, dxo, w_out)


def loss_head(x, fw, target, tm=1024):
    T = x.shape[0]

    def body(x_ref, fw_ref, t_ref, loss_ref, dx_ref, dfw_ref):
        @pl.when(pl.program_id(0) == 0)
        def _():
            loss_ref[...] = jnp.zeros_like(loss_ref)
            dfw_ref[...] = jnp.zeros_like(dfw_ref)

        w = fw_ref[...]
        y, xh, inv = _rms_fwd(x_ref[...], w)
        err = y - t_ref[...]
        loss_ref[...] += 0.5 * jnp.sum(jnp.mean(err * err, axis=-1, keepdims=True), axis=0, keepdims=True)
        dx, dw = _rms_bwd(err * (1.0 / D_MODEL), xh, inv, w)
        dx_ref[...] = dx
        dfw_ref[...] += dw

    return pl.pallas_call(
        body, name="loss_head", grid=(T // tm,),
        in_specs=[_rows(tm, D_MODEL), _resident((1, D_MODEL)), _rows(tm, D_MODEL)],
        out_specs=[pl.BlockSpec((1, LANES), lambda i: (0, 0)), _rows(tm, D_MODEL), pl.BlockSpec((1, D_MODEL), lambda i: (0, 0))],
        out_shape=[jax.ShapeDtypeStruct((1, LANES), f32), jax.ShapeDtypeStruct((T, D_MODEL), f32), jax.ShapeDtypeStruct((1, D_MODEL), f32)],
        compiler_params=_params(("arbitrary",)),
    )(x, fw, target)


ELEMENTWISE_TILE_BYTES = 1_600_000


def _row_tile(rows, width):
    row_bytes = 4 * _round_up(width, LANES)
    tile = rows
    while tile * row_bytes > ELEMENTWISE_TILE_BYTES and tile % 16 == 0:
        tile //= 2
    return tile


def adamw(g_parts, w, m, v, name="adamw", a0=0, prev=None):
    A, B, n = w.shape
    tb = _row_tile(B, n)
    n_parts = len(g_parts)
    arrays, specs = [], []
    for part in g_parts:
        lead, arr = part if isinstance(part, tuple) else ((), part)
        specs.append(pl.BlockSpec((None,) * (len(lead) + 1) + (tb, n), lambda a, t, lead=lead: tuple(lead) + (a, t, 0)))
        arrays.append(arr)
    na = arrays[0].shape[-3]
    prev = list(prev) if prev is not None else []

    def body(*refs):
        n = n_parts
        g_refs = refs[:n]
        w_ref, m_ref, v_ref = refs[n:n + 3]
        go_ref, d_ref, mo_ref, vo_ref = refs[n + 3 + len(prev):]
        g = g_refs[0][...].astype(f32)
        for r in g_refs[1:]:
            g = g + r[...].astype(f32)
        m_new = ADAM_B1 * m_ref[...] + (1.0 - ADAM_B1) * g
        v_new = ADAM_B2 * v_ref[...] + (1.0 - ADAM_B2) * (g * g)
        m_hat = m_new / (1.0 - ADAM_B1 ** ADAM_STEP)
        v_hat = v_new / (1.0 - ADAM_B2 ** ADAM_STEP)
        go_ref[...] = g
        d_ref[...] = -ADAM_LR * (m_hat / (jnp.sqrt(v_hat) + ADAM_EPS) + ADAM_WD * w_ref[...])
        mo_ref[...] = m_new
        vo_ref[...] = v_new

    plain = pl.BlockSpec((None, tb, n), lambda a, t: (a + a0, t, 0))
    return pl.pallas_call(
        body, name=name, grid=(na, B // tb), in_specs=specs + [plain] * 3 + [_ANY] * len(prev), out_specs=[plain] * 4,
        out_shape=[jax.ShapeDtypeStruct((A, B, n), f32)] * 4,
        input_output_aliases={n_parts + 3 + k: k for k in range(len(prev))},
        compiler_params=_params(("parallel", "parallel")),
    )(*arrays, w, m, v, *prev)


def pair_sum_bf16(ga, gb, name):
    _, A, B, n = gb.shape
    tb = _row_tile(B, n)

    def body(a_ref, b_ref, o_ref):
        o_ref[...] = (a_ref[...] + b_ref[...]).astype(bf16)

    return pl.pallas_call(
        body, name=name, grid=(3, A, B // tb),
        in_specs=[pl.BlockSpec((None, None, None, tb, n), lambda j, a, t: (0, j + 1, a, t, 0)),
                  pl.BlockSpec((None, None, tb, n), lambda j, a, t: (j + 1, a, t, 0))],
        out_specs=pl.BlockSpec((None, None, tb, n), lambda j, a, t: (j + 1, a, t, 0)),
        out_shape=jax.ShapeDtypeStruct((4, A, B, n), bf16),
        compiler_params=_params(("parallel", "parallel", "parallel")),
    )(ga, gb)


def assemble(gathered, axis, tk=256):
    _, A, K, n = gathered.shape
    if axis == 1:
        def body(w_ref, o_ref):
            o_ref[...] = jnp.concatenate([w_ref[j] for j in range(N_DEV)], axis=1)

        return pl.pallas_call(
            body, name=f"assemble_cols_{K}x{n}", grid=(A, K // tk),
            in_specs=[pl.BlockSpec((N_DEV, None, tk, n), lambda a, t: (0, a, t, 0))],
            out_specs=pl.BlockSpec((None, tk, N_DEV * n), lambda a, t: (a, t, 0)),
            out_shape=jax.ShapeDtypeStruct((A, K, N_DEV * n), gathered.dtype),
            compiler_params=_params(("parallel", "parallel")),
        )(gathered)

    def body(w_ref, o_ref):
        for j in range(N_DEV):
            o_ref[pl.ds(j * K, K), :] = w_ref[j]

    return pl.pallas_call(
        body, name=f"assemble_rows_{K}x{n}", grid=(A,),
        in_specs=[pl.BlockSpec((N_DEV, None, K, n), lambda a: (0, a, 0, 0))],
        out_specs=pl.BlockSpec((None, N_DEV * K, n), lambda a: (a, 0, 0)),
        out_shape=jax.ShapeDtypeStruct((A, N_DEV * K, n), gathered.dtype),
        compiler_params=_params(("parallel",)),
    )(gathered)


SSD_IN_PAD = -(-SSD_IN_DIM // LANES) * LANES


def assemble_ssd_in(gathered, tk=256):
    _, A, K, n = gathered.shape

    def body(w_ref, z_ref, x_ref, dt_ref, full_ref):
        full_ref[:, pl.ds(SSD_IN_PAD - LANES, LANES)] = jnp.zeros((tk, LANES), gathered.dtype)
        for j in range(N_DEV):
            full_ref[:, pl.ds(j * n, n)] = w_ref[j]
        z_ref[...] = full_ref[:, pl.ds(0, SSD_INNER)]
        x_ref[...] = full_ref[:, pl.ds(SSD_INNER, SSD_CONV_DIM)]
        dt_ref[...] = full_ref[:, pl.ds(SSD_INNER + SSD_CONV_DIM, LANES)]

    widths = (SSD_INNER, SSD_CONV_DIM, LANES)
    return pl.pallas_call(
        body, name="assemble_ssd_in", grid=(A, K // tk),
        in_specs=[pl.BlockSpec((N_DEV, None, tk, n), lambda a, t: (0, a, t, 0))],
        out_specs=[pl.BlockSpec((None, tk, w), lambda a, t: (a, t, 0)) for w in widths],
        out_shape=[jax.ShapeDtypeStruct((A, K, w), gathered.dtype) for w in widths],
        scratch_shapes=[pltpu.VMEM((tk, SSD_IN_PAD), gathered.dtype)],
        compiler_params=_params(("parallel", "parallel")),
    )(gathered)


def ssd_in_to_shards(dwz, dwx, dwdt, buf, j, tk=256):
    K = dwz.shape[0]
    n = buf.shape[-1]
    fresh = isinstance(buf, jax.ShapeDtypeStruct)

    def body(z_ref, x_ref, dt_ref, *rest):
        o_ref, full_ref = rest[-2:]
        full_ref[:, pl.ds(0, SSD_INNER)] = z_ref[...]
        full_ref[:, pl.ds(SSD_INNER, SSD_CONV_DIM)] = x_ref[...]
        full_ref[:, pl.ds(SSD_INNER + SSD_CONV_DIM, LANES)] = dt_ref[...]
        my_c, my_chip = _my_core_and_chip()
        for d in range(N_DEV):
            o_ref[(d % 2) ^ my_c, (d // 2) ^ my_chip] = full_ref[:, pl.ds(d * n, n)]

    return pl.pallas_call(
        body, name="ssd_in_to_shards", grid=(K // tk,),
        in_specs=[_rows(tk, SSD_INNER), _rows(tk, SSD_CONV_DIM), _rows(tk, LANES)] + ([] if fresh else [_ANY]),
        out_specs=pl.BlockSpec((2, 4, None, tk, n), lambda t: (0, 0, j, t, 0)),
        out_shape=jax.ShapeDtypeStruct(buf.shape, f32),
        scratch_shapes=[pltpu.VMEM((tk, SSD_IN_PAD), f32)],
        input_output_aliases={} if fresh else {3: 0},
        compiler_params=_params(("parallel",)),
    )(dwz, dwx, dwdt, *([] if fresh else [buf]))


def sum_over_devices(gathered):
    _, R, W = gathered.shape

    def body(g_ref, o_ref):
        acc = g_ref[0]
        for k in range(1, N_DEV):
            acc = acc + g_ref[k]
        o_ref[...] = acc

    return pl.pallas_call(
        body, name="sum_over_devices", grid=(1,),
        in_specs=[pl.BlockSpec((N_DEV, R, W), lambda i: (0, 0, 0))], out_specs=pl.BlockSpec((R, W), lambda i: (0, 0)),
        out_shape=jax.ShapeDtypeStruct((R, W), f32), compiler_params=_params(("arbitrary",)),
    )(gathered)


_ANY = pl.BlockSpec(memory_space=pl.ANY)


class _Exchange:
    def __init__(self, inputs, out_shapes, scratch, start, finish, relay=None):
        self.inputs, self.out_shapes, self.scratch, self.start, self.finish = inputs, out_shapes, scratch, start, finish
        self.relay = relay

    def run(self, name):
        ni, no = len(self.inputs), len(self.out_shapes)

        def body(*refs):
            parts = (refs[:ni], refs[ni:ni + no], refs[ni + no:])
            self.start(*parts)
            if self.relay is not None:
                self.relay(*parts)
            self.finish(*parts)

        return pl.pallas_call(body, name=name, in_specs=[_ANY] * ni, out_specs=[_ANY] * no, out_shape=self.out_shapes,
                              scratch_shapes=self.scratch)(*self.inputs)


def _carry(body, n_in, n_out, rider, first, last, late=None):
    if rider is None:
        return body
    ri, ro = len(rider.inputs), len(rider.out_shapes)

    def hosted(*refs):
        a, b, c = n_in + ri, n_in + ri + n_out, n_in + ri + n_out + ro
        rs = len(refs) - c - len(rider.scratch)
        parts = (refs[n_in:a], refs[b:c], refs[c + rs:])

        @pl.when(first())
        def _():
            rider.start(*parts)

        if rider.relay is not None and late is not None:
            @pl.when(late())
            def _():
                rider.relay(*parts)

        body(*refs[:n_in], *refs[a:b], *refs[c:c + rs])

        @pl.when(last())
        def _():
            if rider.relay is not None and late is None:
                rider.relay(*parts)
            rider.finish(*parts)

    return hosted


def _rider_specs(rider):
    if rider is None:
        return [], [], [], [], []
    return [_ANY] * len(rider.inputs), [_ANY] * len(rider.out_shapes), list(rider.out_shapes), list(rider.scratch), list(rider.inputs)


def all_gather(blocks):
    n = len(blocks)

    def plan(x_refs, out_refs, sems):
        send_sems, recv_sems, local_sems = sems
        x, y, c = lax.axis_index("x"), lax.axis_index("y"), lax.axis_index("c")
        me, sibling = (x, y, c), (x, y, 1 - c)
        chips = [(1 - x, y), (x, 1 - y), (1 - x, 1 - y)]

        def copy(a, k, blk, to, src=None):
            px, py, pc = blk
            slot = out_refs[a].at[4 * px + 2 * py + pc]
            return pltpu.make_async_remote_copy(
                src_ref=slot if src is None else src, dst_ref=slot,
                send_sem=send_sems.at[7 * a + k], recv_sem=recv_sems.at[7 * a + k], device_id=to, device_id_type=MESH)

        mine = [pltpu.make_async_copy(x_refs[a], out_refs[a].at[4 * x + 2 * y + c], local_sems.at[a]) for a in range(n)]
        first = []
        for a in range(n):
            first += [copy(a, 0, me, sibling, src=x_refs[a])] + [copy(a, 1 + j, me, (*chip, c), src=x_refs[a]) for j, chip in enumerate(chips)]
        return c, me, sibling, chips, copy, mine, first

    def start(x_refs, out_refs, sems):
        _, _, _, _, _, mine, first = plan(x_refs, out_refs, sems)
        for cp in mine + first:
            cp.start()

    def relay(x_refs, out_refs, sems):
        c, me, sibling, chips, copy, _, _ = plan(x_refs, out_refs, sems)
        for j, chip in enumerate(chips):
            for a in range(n):
                copy(a, 1 + j, (*chip, c), me).wait_recv()
                copy(a, 4 + j, (*chip, c), sibling).start()

    def finish(x_refs, out_refs, sems):
        c, me, sibling, chips, copy, mine, first = plan(x_refs, out_refs, sems)
        passed = [copy(a, 4 + j, (*chip, c), sibling) for j, chip in enumerate(chips) for a in range(n)]
        for a in range(n):
            copy(a, 0, sibling, me).wait_recv()
            for j, chip in enumerate(chips):
                copy(a, 4 + j, (*chip, 1 - c), me).wait_recv()
        for cp in first + passed:
            cp.wait_send()
        for cp in mine:
            cp.wait()

    return _Exchange(list(blocks), [jax.ShapeDtypeStruct((N_DEV,) + b.shape, b.dtype) for b in blocks],
                     [pltpu.SemaphoreType.DMA((7 * n,)), pltpu.SemaphoreType.DMA((7 * n,)), pltpu.SemaphoreType.DMA((n,))], start, finish,
                     relay=relay)


def exchange_with_sibling(gs):
    n = len(gs)

    def plan(g_refs, recv_refs, sems):
        send_sems, recv_sems = sems
        x, y, c = lax.axis_index("x"), lax.axis_index("y"), lax.axis_index("c")
        return [pltpu.make_async_remote_copy(src_ref=g_refs[a].at[1], dst_ref=recv_refs[a], send_sem=send_sems.at[a],
                                             recv_sem=recv_sems.at[a], device_id=(x, y, 1 - c), device_id_type=MESH) for a in range(n)]

    def start(*refs):
        for cp in plan(*refs):
            cp.start()

    def finish(*refs):
        for cp in plan(*refs):
            cp.wait()

    return _Exchange(list(gs), [jax.ShapeDtypeStruct(g.shape[1:], g.dtype) for g in gs],
                     [pltpu.SemaphoreType.DMA((n,)), pltpu.SemaphoreType.DMA((n,))], start, finish)


def exchange_between_chips(parts):
    n = len(parts)

    def plan(p_refs, recv_refs, sems):
        send_sems, recv_sems = sems
        x, y, c = lax.axis_index("x"), lax.axis_index("y"), lax.axis_index("c")
        chips = [(2, (1 - x, y)), (1, (x, 1 - y)), (3, (1 - x, 1 - y))]
        return [pltpu.make_async_remote_copy(src_ref=p_refs[a].at[slot], dst_ref=recv_refs[a].at[k], send_sem=send_sems.at[3 * a + k],
                                             recv_sem=recv_sems.at[3 * a + k], device_id=(px, py, c), device_id_type=MESH)
                for a in range(n) for k, (slot, (px, py)) in enumerate(chips)]

    def start(*refs):
        for cp in plan(*refs):
            cp.start()

    def finish(*refs):
        for cp in plan(*refs):
            cp.wait()

    return _Exchange(list(parts), [jax.ShapeDtypeStruct((3,) + p.shape[1:], p.dtype) for p in parts],
                     [pltpu.SemaphoreType.DMA((3 * n,)), pltpu.SemaphoreType.DMA((3 * n,))], start, finish)


PARAMS = {
    "norm_w": ((DEPTH, 3, D_MODEL), 2),
    "ffn_w_gate": ((DEPTH, 2, D_MODEL, D_FF), 3),
    "ffn_w_up": ((DEPTH, 2, D_MODEL, D_FF), 3),
    "ffn_w_down": ((DEPTH, 2, D_FF, D_MODEL), 2),
    "ssd_w_in": ((2, D_MODEL, SSD_IN_DIM), 2),
    "ssd_conv_w": ((2, SSD_CONV_K, SSD_CONV_DIM), 2),
    "ssd_conv_b": ((2, SSD_CONV_DIM), None),
    "ssd_dt_bias": ((2, SSD_HEADS), None),
    "ssd_a_log": ((2, SSD_HEADS), None),
    "ssd_d": ((2, SSD_HEADS), None),
    "ssd_norm_w": ((2, SSD_INNER), None),
    "ssd_w_out": ((2, SSD_INNER, D_MODEL), 1),
    "sc_w_in": ((2, D_MODEL, 3 * D_MODEL), 2),
    "sc_conv_w": ((2, SC_CONV_K, D_MODEL), 2),
    "sc_w_out": ((2, D_MODEL, D_MODEL), 1),
    "final_norm_w": ((D_MODEL,), None),
}
NAMES = list(PARAMS)
BIG = ["ffn_w_gate", "ffn_w_up", "ffn_w_down", "ssd_w_in", "ssd_w_out", "sc_w_in", "sc_w_out"]
SMALL = [n for n in NAMES if n not in BIG]
SMALL_SHARDED = [n for n in SMALL if PARAMS[n][1] is not None]


def _round_up(n, m):
    return -(-n // m) * m


def _pack(flat_list, rows_multiple):
    flat = jnp.concatenate(flat_list)
    rows = _round_up(_round_up(flat.shape[0], PACK_W) // PACK_W, rows_multiple)
    return jnp.pad(flat, (0, rows * PACK_W - flat.shape[0])).reshape(rows, PACK_W)


def _unpack(packed, shapes, lead=()):
    flat = packed.reshape(lead + (-1,))
    out, off = [], 0
    for shp in shapes:
        n = 1
        for s in shp:
            n *= s
        out.append(flat[..., off:off + n].reshape(lead + tuple(shp)))
        off += n
    return out


def _local_shape(name):
    shp, ax = PARAMS[name]
    if ax is None:
        return shp
    return shp[:ax] + (shp[ax] // N_DEV,) + shp[ax + 1:]


def _full_from_gathered(g, name):
    shp, ax = PARAMS[name]
    return jnp.moveaxis(g, 0, ax).reshape(shp)


def _by_destination(full, name):
    shp, ax = PARAMS[name]
    loc = shp[ax] // N_DEV
    return jnp.moveaxis(full.reshape(shp[:ax] + (N_DEV, loc) + shp[ax + 1:]), ax, 0)


def _ssd_layer_fwd(xin, nw, p, rider=None):
    z, xbc, dt_raw = in_proj_fwd(xin, nw, [p["ssd_wz"], p["ssd_wx"], p["ssd_wdt"]], [bf16, bf16, f32])
    act, dt4 = ssd_conv_fwd(xbc, p["ssd_conv_w"], p["ssd_conv_b"], dt_raw, p["ssd_dt_bias"])
    y, states, *got = ssd_scan_fwd(act, dt4, p["ssd_alog4"], rider=rider)
    gn = ssd_gate_fwd(y, act, z, p["ssd_dx"], p["ssd_norm_w"])
    xout = out_proj_fwd(xin, gn, p["ssd_w_out"])
    return xout, (xin, z, xbc, dt_raw, act, dt4, y, states, gn), got


def _ssd_layer_bwd(dxo, nw, p, saved, gbuf, slab, rider=None):
    xin, z, xbc, dt_raw, act, dt4, y, states, gn = saved
    T = xin.shape[0]
    dy, dxs_skip, dz, dd_x, dgnw, dyb = ssd_gate_bwd(y, act, z, p["ssd_dx"], p["ssd_norm_w"], dxo, p["ssd_w_out"])
    gbuf["ssd_w_out"] = tn_matmul_to_shards(gn, dyb, gbuf["ssd_w_out"], (slab,), 0)
    g = {}
    g["ssd_norm_w"] = dgnw[0]
    g["ssd_d"] = jnp.sum(dd_x.reshape(SSD_HEADS, SSD_HEAD_DIM), axis=1)
    dxs, db, dc, ddt4, dalog4, *got = ssd_scan_bwd(act, dt4, p["ssd_alog4"], states, dy, rider=rider)
    g["ssd_a_log"] = dalog4[:, 0, :8].reshape(SSD_HEADS)
    dxbc, ddt_raw, dcw, dcb, ddtb = ssd_conv_bwd(xbc, p["ssd_conv_w"], p["ssd_conv_b"], dt_raw, p["ssd_dt_bias"], dxs, dxs_skip, db, dc, ddt4)
    g["ssd_conv_w"] = dcw[:SSD_CONV_K]
    g["ssd_conv_b"] = dcb[0]
    g["ssd_dt_bias"] = ddtb[0, :SSD_HEADS]
    dx, h, dnw = in_proj_bwd(xin, nw, dxo, [dz, dxbc, ddt_raw], [p["ssd_wz"], p["ssd_wx"], p["ssd_wdt"]])
    gbuf["ssd_w_in"] = ssd_in_to_shards(tn_matmul(h, dz), tn_matmul(h, dxbc), tn_matmul(h, ddt_raw), gbuf["ssd_w_in"], slab)
    return dx, dnw, g, got


def _sc_layer_fwd(xin, nw, p):
    (bcu,) = in_proj_fwd(xin, nw, [p["sc_w_in"]], [bf16])
    q = sc_mid_fwd(bcu, p["sc_conv_w"])
    return out_proj_fwd(xin, q, p["sc_w_out"]), (xin, bcu, q)


def _sc_layer_bwd(dxo, nw, p, saved, gbuf, slab):
    xin, bcu, q = saved
    dbcu, dcw, dyb = sc_mid_bwd(bcu, p["sc_conv_w"], dxo, p["sc_w_out"])
    gbuf["sc_w_out"] = tn_matmul_to_shards(q, dyb, gbuf["sc_w_out"], (slab,), 0)
    g = {"sc_conv_w": dcw[:SC_CONV_K]}
    dx, h, dnw = in_proj_bwd(xin, nw, dxo, [dbcu], [p["sc_w_in"]])
    gbuf["sc_w_in"] = tn_matmul_to_shards(h, dbcu, gbuf["sc_w_in"], (slab,), 1)
    return dx, dnw, g


def kernel(x, norm_w, ffn_w_gate, ffn_w_up, ffn_w_down, ssd_w_in, ssd_conv_w, ssd_conv_b, ssd_dt_bias, ssd_a_log, ssd_d, ssd_norm_w, ssd_w_out, sc_w_in, sc_conv_w, sc_w_out, final_norm_w, loss_target, m_norm_w, m_ffn_w_gate, m_ffn_w_up, m_ffn_w_down, m_ssd_w_in, m_ssd_conv_w, m_ssd_conv_b, m_ssd_dt_bias, m_ssd_a_log, m_ssd_d, m_ssd_norm_w, m_ssd_w_out, m_sc_w_in, m_sc_conv_w, m_sc_w_out, m_final_norm_w, v_norm_w, v_ffn_w_gate, v_ffn_w_up, v_ffn_w_down, v_ssd_w_in, v_ssd_conv_w, v_ssd_conv_b, v_ssd_dt_bias, v_ssd_a_log, v_ssd_d, v_ssd_norm_w, v_ssd_w_out, v_sc_w_in, v_sc_conv_w, v_sc_w_out, v_final_norm_w):
    w_loc = dict(zip(NAMES, (norm_w, ffn_w_gate, ffn_w_up, ffn_w_down, ssd_w_in, ssd_conv_w, ssd_conv_b, ssd_dt_bias, ssd_a_log, ssd_d, ssd_norm_w, ssd_w_out, sc_w_in, sc_conv_w, sc_w_out, final_norm_w)))
    m_loc = dict(zip(NAMES, (m_norm_w, m_ffn_w_gate, m_ffn_w_up, m_ffn_w_down, m_ssd_w_in, m_ssd_conv_w, m_ssd_conv_b, m_ssd_dt_bias, m_ssd_a_log, m_ssd_d, m_ssd_norm_w, m_ssd_w_out, m_sc_w_in, m_sc_conv_w, m_sc_w_out, m_final_norm_w)))
    v_loc = dict(zip(NAMES, (v_norm_w, v_ffn_w_gate, v_ffn_w_up, v_ffn_w_down, v_ssd_w_in, v_ssd_conv_w, v_ssd_conv_b, v_ssd_dt_bias, v_ssd_a_log, v_ssd_d, v_ssd_norm_w, v_ssd_w_out, v_sc_w_in, v_sc_conv_w, v_sc_w_out, v_final_norm_w)))
    my_dev = 4 * lax.axis_index("x") + 2 * lax.axis_index("y") + lax.axis_index("c")

    def as3d(a):
        return a.reshape((-1,) + a.shape[-2:])

    wb = {n: as3d(w_loc[n]).astype(bf16) for n in BIG}

    FFN = ["ffn_w_gate", "ffn_w_up", "ffn_w_down"]

    def mixer_names(i):
        return ["ssd_w_in", "ssd_w_out"] if i % 2 == 0 else ["sc_w_in", "sc_w_out"]

    ag_sets = [[(n, 0, 1) for n in FFN], [(n, 1, 1) for n in FFN] + [(n, 0, 1) for n in mixer_names(0)]]
    ag_sets += [[(n, 2 * r, 2) for n in FFN] + [(n, r // 2, 1) for n in mixer_names(r)] for r in (1, 2, 3)]

    def set_blocks(spec):
        return [wb[n][a0:a0 + na] for n, a0, na in spec]

    def set_weights(spec, gathered):
        q = {}
        for (n, _, _), g in zip(spec, gathered):
            if n == "ssd_w_in":
                q["ssd_wz"], q["ssd_wx"], q["ssd_wdt"] = assemble_ssd_in(g)
            else:
                q[n] = assemble(g, 1 if PARAMS[n][1] == len(PARAMS[n][0]) - 1 else 0)
        return q

    ss_shapes = [_local_shape(n) for n in SMALL_SHARDED]
    gathered0 = all_gather(set_blocks(ag_sets[0]) + [_pack([w_loc[n].reshape(-1) for n in SMALL_SHARDED], 8)]).run("all_gather_first")
    full = {}
    for n, part in zip(SMALL_SHARDED, _unpack(gathered0[-1], ss_shapes, lead=(N_DEV,))):
        full[n] = _full_from_gathered(part, n)
    for n in SMALL:
        if PARAMS[n][1] is None:
            full[n] = w_loc[n]
    small = {
        "ssd_conv_w": full["ssd_conv_w"],
        "ssd_conv_b": full["ssd_conv_b"].reshape(2, 1, SSD_CONV_DIM),
        "ssd_dt_bias": jnp.pad(full["ssd_dt_bias"], ((0, 0), (0, LANES - SSD_HEADS))).reshape(2, 1, LANES),
        "ssd_alog4": jnp.pad(full["ssd_a_log"].reshape(2, SSD_GROUPS, 1, 8), ((0, 0), (0, 0), (0, 0), (0, LANES - 8))),
        "ssd_dx": jnp.repeat(full["ssd_d"], SSD_HEAD_DIM, axis=1).reshape(2, 1, SSD_INNER),
        "ssd_norm_w": full["ssd_norm_w"].reshape(2, 1, SSD_INNER),
        "sc_conv_w": full["sc_conv_w"],
    }
    nw_all = full["norm_w"].reshape(DEPTH, 3, 1, D_MODEL)

    ffn_w = [[None, None] for _ in range(DEPTH)]
    mix_w = [None] * DEPTH

    def arrived(s, gathered):
        q = set_weights(ag_sets[s], gathered)
        ffn = tuple(q[n] for n in FFN)
        if s == 0:
            ffn_w[0][0] = ffn + ((0,),)
            return
        i = 0 if s == 1 else s - 1
        if s == 1:
            ffn_w[0][1] = ffn + ((0,),)
        else:
            ffn_w[i] = [ffn + ((0,),), ffn + ((1,),)]
        m = {n: v[0] for n, v in q.items() if n not in FFN}
        m.update({n: v[i // 2] for n, v in small.items() if n.startswith("ssd" if i % 2 == 0 else "sc")})
        mix_w[i] = m

    def rider_for(s):
        return all_gather(set_blocks(ag_sets[s]))

    xc = x[0]
    saved = []
    arrived(0, gathered0[:-1])
    for i in range(DEPTH):
        carried = {0: (1, 2, 3), 1: (4, None, None)}.get(i, (None, None, None))
        wg, wu, wd, idx = ffn_w[i][0]
        x1, g1, u1, a1, *got = ffn_fwd(xc, nw_all[i, 0], wg, wu, wd, idx, rider=rider_for(carried[0]) if carried[0] else None)
        if carried[0]:
            arrived(carried[0], got)
        if i % 2 == 0:
            x2, mix_saved, got = _ssd_layer_fwd(x1, nw_all[i, 1], mix_w[i], rider=rider_for(carried[1]) if carried[1] else None)
            if carried[1]:
                arrived(carried[1], got)
        else:
            x2, mix_saved = _sc_layer_fwd(x1, nw_all[i, 1], mix_w[i])
        wg, wu, wd, idx = ffn_w[i][1]
        x3, g3, u3, a3, *got = ffn_fwd(x2, nw_all[i, 2], wg, wu, wd, idx, rider=rider_for(carried[2]) if carried[2] else None)
        if carried[2]:
            arrived(carried[2], got)
        saved.append(((xc, g1, u1, a1), mix_saved, (x2, g3, u3, a3)))
        xc = x3

    loss_row, dx, dfw = loss_head(xc, full["final_norm_w"].reshape(1, D_MODEL), loss_target[0])
    loss = lax.psum(loss_row[0, 0], ("x", "y", "c"))

    grads = {n: [None] * PARAMS[n][0][0] for n in SMALL if n != "final_norm_w"}
    grads["final_norm_w"] = dfw[0]
    dnorm = [[None] * 3 for _ in range(DEPTH)]
    def slabs(n, which):
        if n.startswith("ffn"):
            return {"early": (2, 6), "mid": (1, 1), "last": (0, 1)}[which]
        if n.startswith("ssd"):
            return {"early": (1, 1), "mid": (0, 1), "last": (0, 0)}[which]
        return {"early": (0, 2), "mid": (0, 0), "last": (0, 0)}[which]

    TRANSPOSED = ("ffn_w_gate", "ffn_w_up")

    def shard3d(a, n):
        return jnp.swapaxes(as3d(a), 1, 2) if n in TRANSPOSED else as3d(a)

    gb = {which: {n: jax.ShapeDtypeStruct((2, 4, slabs(n, which)[1]) + shard3d(wb[n], n).shape[1:], f32) for n in BIG if slabs(n, which)[1]}
          for which in ("early", "mid", "last")}

    def ffn_back(i, k, dxo, sv, rider=None):
        xin, g_, u_, a_ = sv
        which = "early" if i > 0 else ("mid" if k == 1 else "last")
        gbuf = gb[which]
        slab = 2 * i + k - slabs("ffn_w_gate", which)[0]
        wg, wu, wd, idx = ffn_w[i][k]
        dxi, h, dyb, dg, du, dnw, *got = ffn_bwd_dx(xin, dxo, g_, u_, nw_all[i, 2 * k], wg, wu, wd, idx, rider=rider)
        dnorm[i][2 * k] = dnw[0]
        gbuf["ffn_w_gate"] = tn_matmul_to_shards(dg, h, gbuf["ffn_w_gate"], (slab,), 0)
        gbuf["ffn_w_up"] = tn_matmul_to_shards(du, h, gbuf["ffn_w_up"], (slab,), 0)
        gbuf["ffn_w_down"] = tn_matmul_to_shards(a_, dyb, gbuf["ffn_w_down"], (slab,), 0)
        return dxi, got

    def reduce_in_chip(gbuf, from_sibling=None):
        names = list(gbuf)
        bufs = [gbuf[n] for n in names]
        if from_sibling is None:
            from_sibling = exchange_with_sibling(bufs).run("exchange_with_sibling")
        return names, bufs, from_sibling, [pair_sum_bf16(g, fs, "pair_sum_" + n) for n, g, fs in zip(names, bufs, from_sibling)]

    reduced, from_chips = {}, {}
    for i in reversed(range(DEPTH)):
        j = i // 2
        sv_a, sv_mix, sv_b = saved[i]
        if i == 0:
            dx, got = ffn_back(i, 1, dx, sv_b, rider=exchange_with_sibling(list(gb["early"].values())))
            reduced["early"] = reduce_in_chip(gb["early"], from_sibling=got)
        else:
            dx, _ = ffn_back(i, 1, dx, sv_b)
        if i % 2 == 0:
            rider = exchange_between_chips(reduced["early"][3]) if i == 0 else None
            dx, dnw, gm, got = _ssd_layer_bwd(dx, nw_all[i, 1], mix_w[i], sv_mix, gb["mid" if i == 0 else "early"], 0, rider=rider)
            if i == 0:
                from_chips["early"] = got
                reduced["mid"] = reduce_in_chip(gb["mid"])
        else:
            dx, dnw, gm = _sc_layer_bwd(dx, nw_all[i, 1], mix_w[i], sv_mix, gb["early"], j)
        dnorm[i][1] = dnw[0]
        for n, val in gm.items():
            grads[n][j] = val
        dx, got = ffn_back(i, 0, dx, sv_a, rider=exchange_between_chips(reduced["mid"][3]) if i == 0 else None)
        if i == 0:
            from_chips["mid"] = got

    grads["norm_w"] = jnp.stack([jnp.stack(r) for r in dnorm])
    for n in SMALL:
        if isinstance(grads[n], list):
            grads[n] = jnp.stack(grads[n])

    reduced["last"] = reduce_in_chip(gb["last"])
    from_chips["last"] = exchange_between_chips(reduced["last"][3]).run("exchange_between_chips")
    results = [{}, {}, {}, {}]
    outs = {}
    for which in ("last", "mid", "early"):
        names, bufs, from_sibling, _ = reduced[which]
        for n, g, fs, fc in zip(names, bufs, from_sibling, from_chips[which]):
            parts = [((0, 0), g), ((0,), fs), ((0,), fc), ((1,), fc), ((2,), fc)]
            outs[n] = adamw(parts, shard3d(w_loc[n], n), shard3d(m_loc[n], n), shard3d(v_loc[n], n), name="adamw_" + n + "_" + which,
                            a0=slabs(n, which)[0], prev=outs.get(n))
    for n in BIG:
        for k in range(4):
            o = outs[n][k]
            results[k][n] = (jnp.swapaxes(o, 1, 2) if n in TRANSPOSED else o).reshape(_local_shape(n))

    g_small = _pack([grads[n].reshape(-1) for n in SMALL], 8)
    g_small = sum_over_devices(all_gather([g_small]).run("all_gather_small_grads")[0])
    g_small_full = dict(zip(SMALL, _unpack(g_small, [PARAMS[n][0] for n in SMALL])))
    g_small_loc = []
    for n in SMALL:
        if PARAMS[n][1] is None:
            g_small_loc.append(g_small_full[n])
        else:
            g_small_loc.append(lax.dynamic_index_in_dim(_by_destination(g_small_full[n], n), my_dev, axis=0, keepdims=False))
    small_shapes = [_local_shape(n) for n in SMALL]
    pack_small = lambda d: _pack([d[n].reshape(-1) for n in SMALL], 8)[None]
    small_out = adamw([_pack([gl.reshape(-1) for gl in g_small_loc], 8)[None]], pack_small(w_loc), pack_small(m_loc), pack_small(v_loc), name="adamw_small")
    for k in range(4):
        results[k].update(zip(SMALL, _unpack(small_out[k], small_shapes)))
    return (loss, dx[None], *[results[0][n] for n in NAMES], *[results[1][n] for n in NAMES],
            *[results[2][n] for n in NAMES], *[results[3][n] for n in NAMES])
```
